```python
import math
import jax, jax.numpy as jnp
from jax import lax
import numpy as np

D_MODEL = 2048
BATCH = 8
SEQ = 4096
DEPTH = 4

N_A = DEPTH // 2
N_B = DEPTH - N_A
N_META = 16
CONV_WIDTH = 3
HEAD_DIM = 128
N_HEADS = D_MODEL // HEAD_DIM
D_FF = ((8 * D_MODEL // 3 + 255) // 256) * 256
BLOCK = 128
PAD = BLOCK - N_META
EPS = 1e-6
NEG = -1e30

kernel_name = "yoco_shortconv_forgetting_attn_meta"


def rms_norm(x, g):
    xf = x.astype(jnp.float32)
    y = xf * lax.rsqrt(jnp.mean(xf * xf, axis=-1, keepdims=True) + EPS)
    return (y * g.astype(jnp.float32)).astype(x.dtype)


def short_conv_mixer(xn, w_in, conv_w, w_out):
    L = xn.shape[1]
    b_gate, c_gate, h = jnp.split(xn @ w_in, 3, axis=-1)
    u = jnp.pad(c_gate * h, ((0, 0), (CONV_WIDTH - 1, 0), (0, 0)))
    conv = sum(u[:, j:j + L, :] * conv_w[j] for j in range(CONV_WIDTH))
    return (b_gate * conv) @ w_out


def swiglu(xn, w_gu, w_down):
    g, u = jnp.split(xn @ w_gu, 2, axis=-1)
    return (jax.nn.silu(g) * u) @ w_down


def shared_kv(h, kv_norm, w_kv, k_norm, w_f, b_f):
    Bsz, L, _ = h.shape
    xn = rms_norm(h, kv_norm)
    k, v = jnp.split(xn @ w_kv, 2, axis=-1)
    k = rms_norm(k.reshape(Bsz, L, N_HEADS, HEAD_DIM), k_norm)
    v = v.reshape(Bsz, L, N_HEADS, HEAD_DIM)
    log_f = jax.nn.log_sigmoid((xn @ w_f + b_f).astype(jnp.float32))
    pad4 = ((0, 0), (PAD, 0), (0, 0), (0, 0))
    k = jnp.pad(k, pad4)
    v = jnp.pad(v, pad4)
    c = jnp.cumsum(jnp.pad(log_f, ((0, 0), (PAD, 0), (0, 0))), axis=1)
    return k, v, jnp.transpose(c, (0, 2, 1))


def forgetting_attention(xn, w_q, q_norm, w_o, k, v, c):
    Bsz, L, _ = xn.shape
    Lp = k.shape[1]
    q = rms_norm((xn @ w_q).reshape(Bsz, L, N_HEADS, HEAD_DIM), q_norm)
    q = jnp.pad(q, ((0, 0), (PAD, 0), (0, 0), (0, 0)))
    scale = 1.0 / math.sqrt(HEAD_DIM)
    kpos = jnp.arange(Lp)

    def block(i):
        start = i * BLOCK
        qb = lax.dynamic_slice_in_dim(q, start, BLOCK, axis=1)
        cq = lax.dynamic_slice_in_dim(c, start, BLOCK, axis=2)
        s = jnp.einsum('bqhd,bkhd->bhqk', qb, k).astype(jnp.float32) * scale
        s = s + (cq[..., :, None] - c[..., None, :])
        qpos = start + jnp.arange(BLOCK)
        mask = (kpos[None, :] <= qpos[:, None]) & (kpos[None, :] >= PAD)
        s = jnp.where(mask, s, NEG)
        p = jax.nn.softmax(s, axis=-1).astype(v.dtype)
        return jnp.einsum('bhqk,bkhd->bqhd', p, v)

    o = lax.map(block, jnp.arange(Lp // BLOCK))
    o = jnp.transpose(o, (1, 0, 2, 3, 4)).reshape(Bsz, Lp, N_HEADS * HEAD_DIM)[:, PAD:]
    return o @ w_o


def _fwd_setup_inputs(seed: int = 0) -> dict:
    key = jax.random.key(seed)
    ks = jax.random.split(key, 24)
    D, F, H, Dh = D_MODEL, D_FF, N_HEADS, HEAD_DIM
    nrm = lambda k, shape, s: jax.random.normal(k, shape, jnp.float32) * s
    gain = lambda k, shape: 1.0 + nrm(k, shape, 0.02)
    out_s = 0.5 / math.sqrt(DEPTH)
    return {
        "x": nrm(ks[0], (BATCH, SEQ, D), 1.0),
        "meta": nrm(ks[1], (N_META, D), 1.0),
        "a_norm": gain(ks[2], (N_A, D)),
        "a_w_in": nrm(ks[3], (N_A, D, 3 * D), D ** -0.5),
        "a_conv": nrm(ks[4], (N_A, CONV_WIDTH, D), CONV_WIDTH ** -0.5),
        "a_w_out": nrm(ks[5], (N_A, D, D), D ** -0.5 * out_s),
        "kv_norm": gain(ks[6], (D,)),
        "w_kv": nrm(ks[7], (D, 2 * H * Dh), D ** -0.5),
        "k_norm": gain(ks[8], (Dh,)),
        "w_f": nrm(ks[9], (D, H), D ** -0.5),
        "b_f": 3.0 + nrm(ks[10], (H,), 0.5),
        "b_norm": gain(ks[11], (N_B, D)),
        "b_w_q": nrm(ks[12], (N_B, D, H * Dh), D ** -0.5),
        "b_q_norm": gain(ks[13], (N_B, Dh)),
        "b_w_o": nrm(ks[14], (N_B, H * Dh, D), D ** -0.5 * out_s),
        "ffn_norm": gain(ks[15], (DEPTH, D)),
        "ffn_w_gu": nrm(ks[16], (DEPTH, D, 2 * F), D ** -0.5),
        "ffn_w_down": nrm(ks[17], (DEPTH, F, D), F ** -0.5 * out_s),
    }


def _fwd_reference(x, meta, a_norm, a_w_in, a_conv, a_w_out, kv_norm, w_kv, k_norm, w_f, b_f,
              b_norm, b_w_q, b_q_norm, b_w_o, ffn_norm, ffn_w_gu, ffn_w_down):
    Bsz = x.shape[0]
    meta_b = jnp.broadcast_to(meta.astype(x.dtype)[None], (Bsz, N_META, D_MODEL))
    h = jnp.concatenate([meta_b, x], axis=1)
    k = v = c = None
    for layer in range(DEPTH):
        if layer < N_A:
            h = h + short_conv_mixer(rms_norm(h, a_norm[layer]), a_w_in[layer],
                                     a_conv[layer], a_w_out[layer])
        else:
            if layer == N_A:
                k, v, c = shared_kv(h, kv_norm, w_kv, k_norm, w_f, b_f)
            j = layer - N_A
            h = h + forgetting_attention(rms_norm(h, b_norm[j]), b_w_q[j], b_q_norm[j],
                                         b_w_o[j], k, v, c)
        h = h + swiglu(rms_norm(h, ffn_norm[layer]), ffn_w_gu[layer], ffn_w_down[layer])
    return h[:, N_META:, :]


import jax as _jax
import jax.numpy as _jnp

TWIN_FORMAT = 'train_step'
FWD_PARAMS = ['x', 'meta', 'a_norm', 'a_w_in', 'a_conv', 'a_w_out', 'kv_norm', 'w_kv', 'k_norm', 'w_f', 'b_f', 'b_norm', 'b_w_q', 'b_q_norm', 'b_w_o', 'ffn_norm', 'ffn_w_gu', 'ffn_w_down']
TWIN_WEIGHTS = ['meta', 'a_norm', 'a_w_in', 'a_conv', 'a_w_out', 'kv_norm', 'w_kv', 'k_norm', 'w_f', 'b_f', 'b_norm', 'b_w_q', 'b_q_norm', 'b_w_o', 'ffn_norm', 'ffn_w_gu', 'ffn_w_down']
TWIN_DIFF_INPUT = 'x'
TWIN_INPUTS = ['x', 'meta', 'a_norm', 'a_w_in', 'a_conv', 'a_w_out', 'kv_norm', 'w_kv', 'k_norm', 'w_f', 'b_f', 'b_norm', 'b_w_q', 'b_q_norm', 'b_w_o', 'ffn_norm', 'ffn_w_gu', 'ffn_w_down', 'loss_target', 'm_meta', 'm_a_norm', 'm_a_w_in', 'm_a_conv', 'm_a_w_out', 'm_kv_norm', 'm_w_kv', 'm_k_norm', 'm_w_f', 'm_b_f', 'm_b_norm', 'm_b_w_q', 'm_b_q_norm', 'm_b_w_o', 'm_ffn_norm', 'm_ffn_w_gu', 'm_ffn_w_down', 'v_meta', 'v_a_norm', 'v_a_w_in', 'v_a_conv', 'v_a_w_out', 'v_kv_norm', 'v_w_kv', 'v_k_norm', 'v_w_f', 'v_b_f', 'v_b_norm', 'v_b_w_q', 'v_b_q_norm', 'v_b_w_o', 'v_ffn_norm', 'v_ffn_w_gu', 'v_ffn_w_down']
TWIN_OUTPUTS = ['loss', 'grad_x', 'grad_meta', 'grad_a_norm', 'grad_a_w_in', 'grad_a_conv', 'grad_a_w_out', 'grad_kv_norm', 'grad_w_kv', 'grad_k_norm', 'grad_w_f', 'grad_b_f', 'grad_b_norm', 'grad_b_w_q', 'grad_b_q_norm', 'grad_b_w_o', 'grad_ffn_norm', 'grad_ffn_w_gu', 'grad_ffn_w_down', 'delta_meta', 'delta_a_norm', 'delta_a_w_in', 'delta_a_conv', 'delta_a_w_out', 'delta_kv_norm', 'delta_w_kv', 'delta_k_norm', 'delta_w_f', 'delta_b_f', 'delta_b_norm', 'delta_b_w_q', 'delta_b_q_norm', 'delta_b_w_o', 'delta_ffn_norm', 'delta_ffn_w_gu', 'delta_ffn_w_down', 'new_m_meta', 'new_m_a_norm', 'new_m_a_w_in', 'new_m_a_conv', 'new_m_a_w_out', 'new_m_kv_norm', 'new_m_w_kv', 'new_m_k_norm', 'new_m_w_f', 'new_m_b_f', 'new_m_b_norm', 'new_m_b_w_q', 'new_m_b_q_norm', 'new_m_b_w_o', 'new_m_ffn_norm', 'new_m_ffn_w_gu', 'new_m_ffn_w_down', 'new_v_meta', 'new_v_a_norm', 'new_v_a_w_in', 'new_v_a_conv', 'new_v_a_w_out', 'new_v_kv_norm', 'new_v_w_kv', 'new_v_k_norm', 'new_v_w_f', 'new_v_b_f', 'new_v_b_norm', 'new_v_b_w_q', 'new_v_b_q_norm', 'new_v_b_w_o', 'new_v_ffn_norm', 'new_v_ffn_w_gu', 'new_v_ffn_w_down']
TWIN_LEAF_KINDS = {'loss': 'loss', 'grad_x': 'grad_x', 'grad_meta': 'grad_w', 'grad_a_norm': 'grad_w', 'grad_a_w_in': 'grad_w', 'grad_a_conv': 'grad_w', 'grad_a_w_out': 'grad_w', 'grad_kv_norm': 'grad_w', 'grad_w_kv': 'grad_w', 'grad_k_norm': 'grad_w', 'grad_w_f': 'grad_w', 'grad_b_f': 'grad_w', 'grad_b_norm': 'grad_w', 'grad_b_w_q': 'grad_w', 'grad_b_q_norm': 'grad_w', 'grad_b_w_o': 'grad_w', 'grad_ffn_norm': 'grad_w', 'grad_ffn_w_gu': 'grad_w', 'grad_ffn_w_down': 'grad_w', 'delta_meta': 'delta_w', 'delta_a_norm': 'delta_w', 'delta_a_w_in': 'delta_w', 'delta_a_conv': 'delta_w', 'delta_a_w_out': 'delta_w', 'delta_kv_norm': 'delta_w', 'delta_w_kv': 'delta_w', 'delta_k_norm': 'delta_w', 'delta_w_f': 'delta_w', 'delta_b_f': 'delta_w', 'delta_b_norm': 'delta_w', 'delta_b_w_q': 'delta_w', 'delta_b_q_norm': 'delta_w', 'delta_b_w_o': 'delta_w', 'delta_ffn_norm': 'delta_w', 'delta_ffn_w_gu': 'delta_w', 'delta_ffn_w_down': 'delta_w', 'new_m_meta': 'new_m', 'new_m_a_norm': 'new_m', 'new_m_a_w_in': 'new_m', 'new_m_a_conv': 'new_m', 'new_m_a_w_out': 'new_m', 'new_m_kv_norm': 'new_m', 'new_m_w_kv': 'new_m', 'new_m_k_norm': 'new_m', 'new_m_w_f': 'new_m', 'new_m_b_f': 'new_m', 'new_m_b_norm': 'new_m', 'new_m_b_w_q': 'new_m', 'new_m_b_q_norm': 'new_m', 'new_m_b_w_o': 'new_m', 'new_m_ffn_norm': 'new_m', 'new_m_ffn_w_gu': 'new_m', 'new_m_ffn_w_down': 'new_m', 'new_v_meta': 'new_v', 'new_v_a_norm': 'new_v', 'new_v_a_w_in': 'new_v', 'new_v_a_conv': 'new_v', 'new_v_a_w_out': 'new_v', 'new_v_kv_norm': 'new_v', 'new_v_w_kv': 'new_v', 'new_v_k_norm': 'new_v', 'new_v_w_f': 'new_v', 'new_v_b_f': 'new_v', 'new_v_b_norm': 'new_v', 'new_v_b_w_q': 'new_v', 'new_v_b_q_norm': 'new_v', 'new_v_b_w_o': 'new_v', 'new_v_ffn_norm': 'new_v', 'new_v_ffn_w_gu': 'new_v', 'new_v_ffn_w_down': 'new_v'}


def _forward(args):
    return _fwd_reference(*[args[k] for k in FWD_PARAMS])


def _output_shape():
    def fwd():
        inp = _fwd_setup_inputs(0)
        return _fwd_reference(*[inp[k] for k in FWD_PARAMS])
    out = _jax.eval_shape(fwd)
    return out.shape, out.dtype

N_MICROBATCH = 1
ADAM_LR = 0.001
ADAM_B1 = 0.9
ADAM_B2 = 0.999
ADAM_EPS = 1e-08
ADAM_WD = 0.01
ADAM_STEP = 10
PER_EXAMPLE_BATCH_AXIS = {'x': 0, 'loss_target': 0}
SHARED_INPUTS = []
_WEIGHT_DTYPES = {'meta': _jnp.float32, 'a_norm': _jnp.float32, 'a_w_in': _jnp.float32, 'a_conv': _jnp.float32, 'a_w_out': _jnp.float32, 'kv_norm': _jnp.float32, 'w_kv': _jnp.float32, 'k_norm': _jnp.float32, 'w_f': _jnp.float32, 'b_f': _jnp.float32, 'b_norm': _jnp.float32, 'b_w_q': _jnp.float32, 'b_q_norm': _jnp.float32, 'b_w_o': _jnp.float32, 'ffn_norm': _jnp.float32, 'ffn_w_gu': _jnp.float32, 'ffn_w_down': _jnp.float32}
MOMENT_SCALE = {'meta': 2.041228e-03, 'a_norm': 3.039859e+00, 'a_w_in': 4.725116e-02, 'a_conv': 5.711834e-01, 'a_w_out': 1.721395e-01, 'kv_norm': 2.093257e-01, 'w_kv': 1.408633e-02, 'k_norm': 1.401839e+00, 'w_f': 1.425546e-01, 'b_f': 7.927261e+00, 'b_norm': 8.857209e-03, 'b_w_q': 8.730178e-03, 'b_q_norm': 6.984764e-01, 'b_w_o': 4.398385e-02, 'ffn_norm': 7.681457e-01, 'ffn_w_gu': 1.441388e-02, 'ffn_w_down': 1.010422e-01}


def _to_microbatches(a, axis):
    t = _jnp.moveaxis(a, axis, 0)
    t = t.reshape((N_MICROBATCH, t.shape[0] // N_MICROBATCH) + t.shape[1:])
    return _jnp.moveaxis(t, 1, axis + 1)


def setup_inputs(seed: int = 0) -> dict:
    inp = _fwd_setup_inputs(seed)
    key = _jax.random.fold_in(_jax.random.key(seed), 7919)
    shape, _ = _output_shape()
    out = dict(inp)
    out["loss_target"] = _jax.random.normal(_jax.random.fold_in(key, 0), shape, _jnp.float32)
    for i, name in enumerate(TWIN_WEIGHTS):
        w = inp[name].astype(_jnp.float32)
        if MOMENT_SCALE is None:
            s = _jnp.sqrt(_jnp.mean(_jnp.square(w)) + 1e-30)
        else:
            s = MOMENT_SCALE[name]
        km, kv = _jax.random.split(_jax.random.fold_in(key, i + 1))
        out[name] = w
        out["m_" + name] = s * _jax.random.normal(km, w.shape, _jnp.float32)
        out["v_" + name] = (s * s) * _jax.random.uniform(kv, w.shape, _jnp.float32, 0.5, 1.5)
    if N_MICROBATCH > 1:
        for name, axis in PER_EXAMPLE_BATCH_AXIS.items():
            out[name] = _to_microbatches(out[name], axis)
    return {'x': out['x'], 'meta': out['meta'], 'a_norm': out['a_norm'], 'a_w_in': out['a_w_in'], 'a_conv': out['a_conv'], 'a_w_out': out['a_w_out'], 'kv_norm': out['kv_norm'], 'w_kv': out['w_kv'], 'k_norm': out['k_norm'], 'w_f': out['w_f'], 'b_f': out['b_f'], 'b_norm': out['b_norm'], 'b_w_q': out['b_w_q'], 'b_q_norm': out['b_q_norm'], 'b_w_o': out['b_w_o'], 'ffn_norm': out['ffn_norm'], 'ffn_w_gu': out['ffn_w_gu'], 'ffn_w_down': out['ffn_w_down'], 'loss_target': out['loss_target'], 'm_meta': out['m_meta'], 'm_a_norm': out['m_a_norm'], 'm_a_w_in': out['m_a_w_in'], 'm_a_conv': out['m_a_conv'], 'm_a_w_out': out['m_a_w_out'], 'm_kv_norm': out['m_kv_norm'], 'm_w_kv': out['m_w_kv'], 'm_k_norm': out['m_k_norm'], 'm_w_f': out['m_w_f'], 'm_b_f': out['m_b_f'], 'm_b_norm': out['m_b_norm'], 'm_b_w_q': out['m_b_w_q'], 'm_b_q_norm': out['m_b_q_norm'], 'm_b_w_o': out['m_b_w_o'], 'm_ffn_norm': out['m_ffn_norm'], 'm_ffn_w_gu': out['m_ffn_w_gu'], 'm_ffn_w_down': out['m_ffn_w_down'], 'v_meta': out['v_meta'], 'v_a_norm': out['v_a_norm'], 'v_a_w_in': out['v_a_w_in'], 'v_a_conv': out['v_a_conv'], 'v_a_w_out': out['v_a_w_out'], 'v_kv_norm': out['v_kv_norm'], 'v_w_kv': out['v_w_kv'], 'v_k_norm': out['v_k_norm'], 'v_w_f': out['v_w_f'], 'v_b_f': out['v_b_f'], 'v_b_norm': out['v_b_norm'], 'v_b_w_q': out['v_b_w_q'], 'v_b_q_norm': out['v_b_q_norm'], 'v_b_w_o': out['v_b_w_o'], 'v_ffn_norm': out['v_ffn_norm'], 'v_ffn_w_gu': out['v_ffn_w_gu'], 'v_ffn_w_down': out['v_ffn_w_down']}


def _loss(weights, diff, rest, loss_target):
    with _jax.named_scope("forward"):
        args = {**rest, TWIN_DIFF_INPUT: diff, **{k: w.astype(_WEIGHT_DTYPES[k]) for k, w in weights.items()}}
        y = _forward(args)
    with _jax.named_scope("loss_head"):
        err = _jnp.square(y.astype(_jnp.float32) - loss_target)
        return 0.5 * _jnp.sum(_jnp.mean(err, axis=-1)) if err.ndim else 0.5 * err


def _adamw(w, g, m, v):
    m = ADAM_B1 * m + (1.0 - ADAM_B1) * g
    v = ADAM_B2 * v + (1.0 - ADAM_B2) * _jnp.square(g)
    m_hat = m / (1.0 - ADAM_B1 ** ADAM_STEP)
    v_hat = v / (1.0 - ADAM_B2 ** ADAM_STEP)
    delta = -ADAM_LR * (m_hat / (_jnp.sqrt(v_hat) + ADAM_EPS) + ADAM_WD * w)
    return delta, m, v


def reference(x, meta, a_norm, a_w_in, a_conv, a_w_out, kv_norm, w_kv, k_norm, w_f, b_f, b_norm, b_w_q, b_q_norm, b_w_o, ffn_norm, ffn_w_gu, ffn_w_down, loss_target, m_meta, m_a_norm, m_a_w_in, m_a_conv, m_a_w_out, m_kv_norm, m_w_kv, m_k_norm, m_w_f, m_b_f, m_b_norm, m_b_w_q, m_b_q_norm, m_b_w_o, m_ffn_norm, m_ffn_w_gu, m_ffn_w_down, v_meta, v_a_norm, v_a_w_in, v_a_conv, v_a_w_out, v_kv_norm, v_w_kv, v_k_norm, v_w_f, v_b_f, v_b_norm, v_b_w_q, v_b_q_norm, v_b_w_o, v_ffn_norm, v_ffn_w_gu, v_ffn_w_down):
    given = dict(x=x, meta=meta, a_norm=a_norm, a_w_in=a_w_in, a_conv=a_conv, a_w_out=a_w_out, kv_norm=kv_norm, w_kv=w_kv, k_norm=k_norm, w_f=w_f, b_f=b_f, b_norm=b_norm, b_w_q=b_w_q, b_q_norm=b_q_norm, b_w_o=b_w_o, ffn_norm=ffn_norm, ffn_w_gu=ffn_w_gu, ffn_w_down=ffn_w_down, loss_target=loss_target, m_meta=m_meta, m_a_norm=m_a_norm, m_a_w_in=m_a_w_in, m_a_conv=m_a_conv, m_a_w_out=m_a_w_out, m_kv_norm=m_kv_norm, m_w_kv=m_w_kv, m_k_norm=m_k_norm, m_w_f=m_w_f, m_b_f=m_b_f, m_b_norm=m_b_norm, m_b_w_q=m_b_w_q, m_b_q_norm=m_b_q_norm, m_b_w_o=m_b_w_o, m_ffn_norm=m_ffn_norm, m_ffn_w_gu=m_ffn_w_gu, m_ffn_w_down=m_ffn_w_down, v_meta=v_meta, v_a_norm=v_a_norm, v_a_w_in=v_a_w_in, v_a_conv=v_a_conv, v_a_w_out=v_a_w_out, v_kv_norm=v_kv_norm, v_w_kv=v_w_kv, v_k_norm=v_k_norm, v_w_f=v_w_f, v_b_f=v_b_f, v_b_norm=v_b_norm, v_b_w_q=v_b_w_q, v_b_q_norm=v_b_q_norm, v_b_w_o=v_b_w_o, v_ffn_norm=v_ffn_norm, v_ffn_w_gu=v_ffn_w_gu, v_ffn_w_down=v_ffn_w_down)
    weights = {n: given[n] for n in TWIN_WEIGHTS}
    shared = {n: given[n] for n in SHARED_INPUTS}
    per_example = {n: given[n] for n in ['x']}
    grad_fn = _jax.value_and_grad(_loss, argnums=(0, 1))

    def one_microbatch(ex, loss_target):
        ex = dict(ex)
        diff = ex.pop(TWIN_DIFF_INPUT)
        return grad_fn(weights, diff, {**shared, **ex}, loss_target)

    if N_MICROBATCH == 1:
        loss, (grad_w, grad_x) = one_microbatch(per_example, given["loss_target"])
    else:
        def body(carry, xs):
            loss_sum, grad_sum = carry
            l_k, (gw_k, gx_k) = one_microbatch(xs[0], xs[1])
            with _jax.named_scope("update"):
                return (loss_sum + l_k, _jax.tree.map(_jnp.add, grad_sum, gw_k)), gx_k

        init = (_jnp.zeros((), _jnp.float32), _jax.tree.map(_jnp.zeros_like, weights))
        (loss, grad_w), grad_x = _jax.lax.scan(body, init, (per_example, given["loss_target"]))
    with _jax.named_scope("update"):
        delta_w, new_m, new_v = {}, {}, {}
        for n in TWIN_WEIGHTS:
            delta_w[n], new_m[n], new_v[n] = _adamw(weights[n], grad_w[n], given["m_" + n], given["v_" + n])
    return (loss, grad_x, *[grad_w[n] for n in TWIN_WEIGHTS], *[delta_w[n] for n in TWIN_WEIGHTS],
            *[new_m[n] for n in TWIN_WEIGHTS], *[new_v[n] for n in TWIN_WEIGHTS])
```

```python
import functools
import math

import jax
import jax.numpy as jnp
from jax import lax
from jax.experimental import pallas as pl
from jax.experimental.pallas import tpu as pltpu

N_DEV = 8
MESH_AXES = ("x", "y", "c")
EPS = 1e-6
NEG = -1e30
HEAD_DIM = 128
BLOCK = 128
LANES = 128
V7X_VMEM_LIMIT = 56 * 1024 * 1024

ADAM_LR = 0.001
ADAM_B1 = 0.9
ADAM_B2 = 0.999
ADAM_EPS = 1e-08
ADAM_WD = 0.01
ADAM_STEP = 10

BF = jnp.bfloat16
F32 = jnp.float32


def _tile(n, target, mult):
    best = None
    for t in range(mult, min(n, target) + 1, mult):
        if n % t == 0:
            best = t
    return n if best is None else best


def _params(*sem):
    return pltpu.CompilerParams(dimension_semantics=sem, vmem_limit_bytes=V7X_VMEM_LIMIT)


def mm_nn(a, w, *, name, add=None, out_dtype=F32, tm_target=1056, tn_target=1024, tk_target=2048):
    M, K = a.shape
    G, K2, n = w.shape
    assert K == K2
    tm = _tile(M, tm_target, 16)
    tn = _tile(n, tn_target, LANES)
    tk = _tile(K, tk_target, LANES)
    nj, nk = n // tn, K // tk
    has_add = add is not None

    def body(*refs):
        if has_add:
            a_ref, w_ref, add_ref, o_ref = refs[:4]
        else:
            a_ref, w_ref, o_ref = refs[:3]
            add_ref = None

        def finish(r):
            if has_add:
                r = r + add_ref[...]
            o_ref[...] = r.astype(out_dtype)

        part = jnp.dot(a_ref[...], w_ref[...], preferred_element_type=F32)
        if nk == 1:
            finish(part)
        else:
            acc_ref = refs[-1]
            k = pl.program_id(2)

            @pl.when(k == 0)
            def _():
                acc_ref[...] = part

            @pl.when(k > 0)
            def _():
                acc_ref[...] += part

            @pl.when(k == nk - 1)
            def _():
                finish(acc_ref[...])

    in_specs = [
        pl.BlockSpec((tm, tk), lambda i, j, k: (i, k)),
        pl.BlockSpec((None, tk, tn), lambda i, j, k: (j // nj, k, j % nj)),
    ]
    args = [a, w]
    if has_add:
        in_specs.append(pl.BlockSpec((tm, tn), lambda i, j, k: (i, j)))
        args.append(add)
    return pl.pallas_call(
        body,
        out_shape=jax.ShapeDtypeStruct((M, G * n), out_dtype),
        grid=(M // tm, G * nj, nk),
        in_specs=in_specs,
        out_specs=pl.BlockSpec((tm, tn), lambda i, j, k: (i, j)),
        scratch_shapes=[pltpu.VMEM((tm, tn), F32)] if nk > 1 else [],
        compiler_params=_params("parallel", "parallel", "arbitrary"),
        name=name,
    )(*args)


def mm_swiglu(xn, wgu, *, name, save_dtype=BF, tm_target=528, tk_target=512):
    M, K = xn.shape
    G, _, n = wgu.shape
    half = G // 2
    tm = _tile(M, tm_target, 16)
    tn = _tile(n, 1408, LANES)
    tk = _tile(K, tk_target, LANES)
    nj, nk = n // tn, K // tk
    Fh = half * n

    def body(a_ref, wg_ref, wu_ref, act_ref, g_ref, u_ref, accg_ref, accu_ref):
        k = pl.program_id(2)
        a = a_ref[...]
        pg = jnp.dot(a, wg_ref[...], preferred_element_type=F32)
        pu = jnp.dot(a, wu_ref[...], preferred_element_type=F32)

        @pl.when(k == 0)
        def _():
            accg_ref[...] = pg
            accu_ref[...] = pu

        @pl.when(k > 0)
        def _():
            accg_ref[...] += pg
            accu_ref[...] += pu

        @pl.when(k == nk - 1)
        def _():
            g = accg_ref[...]
            u = accu_ref[...]
            act_ref[...] = (g * jax.nn.sigmoid(g) * u).astype(BF)
            g_ref[...] = g.astype(save_dtype)
            u_ref[...] = u.astype(save_dtype)

    out_block = pl.BlockSpec((tm, tn), lambda i, j, k: (i, j))
    return pl.pallas_call(
        body,
        out_shape=(jax.ShapeDtypeStruct((M, Fh), BF),
                   jax.ShapeDtypeStruct((M, Fh), save_dtype),
                   jax.ShapeDtypeStruct((M, Fh), save_dtype)),
        grid=(M // tm, half * nj, nk),
        in_specs=[
            pl.BlockSpec((tm, tk), lambda i, j, k: (i, k)),
            pl.BlockSpec((None, tk, tn), lambda i, j, k: (j // nj, k, j % nj)),
            pl.BlockSpec((None, tk, tn), lambda i, j, k: (half + j // nj, k, j % nj)),
        ],
        out_specs=(out_block, out_block, out_block),
        scratch_shapes=[pltpu.VMEM((tm, tn), F32), pltpu.VMEM((tm, tn), F32)],
        compiler_params=_params("parallel", "parallel", "arbitrary"),
        name=name,
    )(xn, wgu, wgu)


def mm_nt(dy, w, *, name, add=None, out_dtype=F32, tm_target=1056, tko_target=1024, tc_target=1408):
    M, N = dy.shape
    G, K, n = w.shape
    assert N == G * n
    tm = _tile(M, tm_target, 16)
    tko = _tile(K, tko_target, LANES)
    tc = _tile(n, tc_target, LANES)
    nc = n // tc
    steps = G * nc
    has_add = add is not None

    def body(*refs):
        if has_add:
            dy_ref, w_ref, add_ref, o_ref = refs[:4]
        else:
            dy_ref, w_ref, o_ref = refs[:3]
            add_ref = None

        def finish(r):
            if has_add:
                r = r + add_ref[...]
            o_ref[...] = r.astype(out_dtype)

        part = lax.dot_general(dy_ref[...], w_ref[...], (((1,), (1,)), ((), ())),
                               preferred_element_type=F32)
        if steps == 1:
            finish(part)
        else:
            acc_ref = refs[-1]
            s = pl.program_id(2)

            @pl.when(s == 0)
            def _():
                acc_ref[...] = part

            @pl.when(s > 0)
            def _():
                acc_ref[...] += part

            @pl.when(s == steps - 1)
            def _():
                finish(acc_ref[...])

    in_specs = [
        pl.BlockSpec((tm, tc), lambda i, o, s: (i, s)),
        pl.BlockSpec((None, tko, tc), lambda i, o, s: (s // nc, o, s % nc)),
    ]
    args = [dy, w]
    if has_add:
        in_specs.append(pl.BlockSpec((tm, tko), lambda i, o, s: (i, o)))
        args.append(add)
    return pl.pallas_call(
        body,
        out_shape=jax.ShapeDtypeStruct((M, K), out_dtype),
        grid=(M // tm, K // tko, steps),
        in_specs=in_specs,
        out_specs=pl.BlockSpec((tm, tko), lambda i, o, s: (i, o)),
        scratch_shapes=[pltpu.VMEM((tm, tko), F32)] if steps > 1 else [],
        compiler_params=_params("parallel", "parallel", "arbitrary"),
        name=name,
    )(*args)


def mm_nt_dswiglu(dh, w_down, g_s, u_s, *, name, tm_target=1056, tf_target=512):
    M, D = dh.shape
    _, Fh, D2 = w_down.shape
    assert D == D2
    tm = _tile(M, tm_target, 16)
    tf = _tile(Fh, tf_target, LANES)

    def body(dh_ref, w_ref, g_ref, u_ref, dg_ref, du_ref):
        dact = lax.dot_general(dh_ref[...], w_ref[...], (((1,), (1,)), ((), ())),
                               preferred_element_type=F32)
        g = g_ref[...].astype(F32)
        u = u_ref[...].astype(F32)
        sig = jax.nn.sigmoid(g)
        du_ref[...] = (dact * (g * sig)).astype(BF)
        dg_ref[...] = (dact * u * (sig * (1.0 + g * (1.0 - sig)))).astype(BF)

    blk = pl.BlockSpec((tm, tf), lambda i, f: (i, f))
    return pl.pallas_call(
        body,
        out_shape=(jax.ShapeDtypeStruct((M, Fh), BF), jax.ShapeDtypeStruct((M, Fh), BF)),
        grid=(M // tm, Fh // tf),
        in_specs=[
            pl.BlockSpec((tm, D), lambda i, f: (i, 0)),
            pl.BlockSpec((None, tf, D), lambda i, f: (0, f, 0)),
            blk, blk,
        ],
        out_specs=(blk, blk),
        compiler_params=_params("parallel", "parallel"),
        name=name,
    )(dh, w_down, g_s, u_s)


def mm_tn(a, dy, groups, *, name, out_dtype=BF, tk_target=512, tn_target=1408):
    M, K = a.shape
    M2, N = dy.shape
    assert M == M2 and N % groups == 0
    n = N // groups
    tk = _tile(K, tk_target, LANES)
    tn = _tile(n, tn_target, LANES)
    nj = n // tn

    def body(a_ref, dy_ref, o_ref):
        o_ref[...] = lax.dot_general(a_ref[...], dy_ref[...], (((0,), (0,)), ((), ())),
                                     preferred_element_type=F32).astype(out_dtype)

    return pl.pallas_call(
        body,
        out_shape=jax.ShapeDtypeStruct((groups, K, n), out_dtype),
        grid=(K // tk, groups * nj),
        in_specs=[
            pl.BlockSpec((M, tk), lambda i, j: (0, i)),
            pl.BlockSpec((M, tn), lambda i, j: (0, j)),
        ],
        out_specs=pl.BlockSpec((None, tk, tn), lambda i, j: (j // nj, i, j % nj)),
        compiler_params=_params("parallel", "parallel"),
        name=name,
    )(a, dy)


def rms_fwd(h, g, *, name):
    T, D = h.shape
    tm = _tile(T, 528, 16)

    def body(h_ref, g_ref, o_ref):
        x = h_ref[...]
        r = lax.rsqrt(jnp.mean(x * x, axis=-1, keepdims=True) + EPS)
        o_ref[...] = ((x * r) * g_ref[...]).astype(BF)

    return pl.pallas_call(
        body,
        out_shape=jax.ShapeDtypeStruct((T, D), BF),
        grid=(T // tm,),
        in_specs=[pl.BlockSpec((tm, D), lambda i: (i, 0)), pl.BlockSpec((1, D), lambda i: (0, 0))],
        out_specs=pl.BlockSpec((tm, D), lambda i: (i, 0)),
        compiler_params=_params("parallel"),
        name=name,
    )(h, g.reshape(1, D))


def rms_bwd(dxn, h, g, add, *, name):
    T, D = h.shape
    tm = _tile(T, 264, 8)

    def body(dxn_ref, h_ref, g_ref, add_ref, dh_ref, dg_ref):
        x = h_ref[...]
        dy = dxn_ref[...]
        r = lax.rsqrt(jnp.mean(x * x, axis=-1, keepdims=True) + EPS)
        xhat = x * r
        part = jnp.sum(dy * xhat, axis=0, keepdims=True)

        @pl.when(pl.program_id(0) == 0)
        def _():
            dg_ref[...] = part

        @pl.when(pl.program_id(0) > 0)
        def _():
            dg_ref[...] += part

        dxh = dy * g_ref[...]
        dh_ref[...] = add_ref[...] + r * (dxh - xhat * jnp.mean(dxh * xhat, axis=-1, keepdims=True))

    row = pl.BlockSpec((tm, D), lambda i: (i, 0))
    vec = pl.BlockSpec((1, D), lambda i: (0, 0))
    return pl.pallas_call(
        body,
        out_shape=(jax.ShapeDtypeStruct((T, D), F32), jax.ShapeDtypeStruct((1, D), F32)),
        grid=(T // tm,),
        in_specs=[row, row, vec, row],
        out_specs=(row, vec),
        compiler_params=_params("arbitrary"),
        name=name,
    )(dxn, h, g.reshape(1, D), add)


def _head_norm(x, gain):
    r = lax.rsqrt(jnp.mean(x * x, axis=-1, keepdims=True) + EPS)
    return (x * r) * gain


def hn_fwd(qraw, gain, *, name):
    T, D = qraw.shape
    H = D // HEAD_DIM
    tm = _tile(T, 528, 16)

    def body(q_ref, g_ref, o_ref):
        gain_v = g_ref[...]
        for hd in range(H):
            sl = slice(hd * HEAD_DIM, (hd + 1) * HEAD_DIM)
            o_ref[:, sl] = _head_norm(q_ref[:, sl], gain_v).astype(BF)

    return pl.pallas_call(
        body,
        out_shape=jax.ShapeDtypeStruct((T, D), BF),
        grid=(T // tm,),
        in_specs=[pl.BlockSpec((tm, D), lambda i: (i, 0)),
                  pl.BlockSpec((1, HEAD_DIM), lambda i: (0, 0))],
        out_specs=pl.BlockSpec((tm, D), lambda i: (i, 0)),
        compiler_params=_params("parallel"),
        name=name,
    )(qraw, gain.reshape(1, HEAD_DIM))


def kv_post(kv, gain, *, name):
    T, D2 = kv.shape
    D = D2 // 2
    H = D // HEAD_DIM
    tm = _tile(T, 528, 16)

    def body(k_ref, v_ref, g_ref, ko_ref, vo_ref):
        gain_v = g_ref[...]
        for hd in range(H):
            sl = slice(hd * HEAD_DIM, (hd + 1) * HEAD_DIM)
            ko_ref[:, sl] = _head_norm(k_ref[:, sl], gain_v).astype(BF)
        vo_ref[...] = v_ref[...].astype(BF)

    blk = pl.BlockSpec((tm, D), lambda i: (i, 0))
    return pl.pallas_call(
        body,
        out_shape=(jax.ShapeDtypeStruct((T, D), BF), jax.ShapeDtypeStruct((T, D), BF)),
        grid=(T // tm,),
        in_specs=[blk, pl.BlockSpec((tm, D), lambda i: (i, 1)),
                  pl.BlockSpec((1, HEAD_DIM), lambda i: (0, 0))],
        out_specs=(blk, blk),
        compiler_params=_params("parallel"),
        name=name,
    )(kv, kv, gain.reshape(1, HEAD_DIM))


def hn_bwd(dq, qraw, gain, *, name):
    T, D = dq.shape
    H = D // HEAD_DIM
    tm = _tile(T, 264, 16)

    def body(dq_ref, q_ref, g_ref, o_ref, dg_ref):
        gain_v = g_ref[...]
        part = jnp.zeros((1, HEAD_DIM), F32)
        for hd in range(H):
            sl = slice(hd * HEAD_DIM, (hd + 1) * HEAD_DIM)
            x = q_ref[:, sl]
            dy = dq_ref[:, sl]
            r = lax.rsqrt(jnp.mean(x * x, axis=-1, keepdims=True) + EPS)
            xhat = x * r
            part = part + jnp.sum(dy * xhat, axis=0, keepdims=True)
            dxh = dy * gain_v
            o_ref[:, sl] = (r * (dxh - xhat * jnp.mean(dxh * xhat, axis=-1, keepdims=True))).astype(BF)

        @pl.when(pl.program_id(0) == 0)
        def _():
            dg_ref[...] = part

        @pl.when(pl.program_id(0) > 0)
        def _():
            dg_ref[...] += part

    blk = pl.BlockSpec((tm, D), lambda i: (i, 0))
    vec = pl.BlockSpec((1, HEAD_DIM), lambda i: (0, 0))
    return pl.pallas_call(
        body,
        out_shape=(jax.ShapeDtypeStruct((T, D), BF), jax.ShapeDtypeStruct((1, HEAD_DIM), F32)),
        grid=(T // tm,),
        in_specs=[blk, blk, vec],
        out_specs=(blk, vec),
        compiler_params=_params("arbitrary"),
        name=name,
    )(dq, qraw, gain.reshape(1, HEAD_DIM))


def _shift_down(cur, above, k, rowc):
    out = pltpu.roll(cur, k, 0)
    for i in range(k):
        out = jnp.where(rowc == i, above[8 - k + i:8 - k + i + 1], out)
    return out


def _shift_up(cur, below, k, rowc):
    R = cur.shape[0]
    out = pltpu.roll(cur, R - k, 0)
    for i in range(k):
        out = jnp.where(rowc == R - k + i, below[i:i + 1], out)
    return out


def _conv3(u, u_above, wv, rowc):
    u1 = _shift_down(u, u_above, 1, rowc)
    u2 = _shift_down(u, u_above, 2, rowc)
    return wv[0:1] * u2 + wv[1:2] * u1 + wv[2:3] * u, u1, u2


def conv_fwd(proj, w, *, name):
    T, D3 = proj.shape
    D = D3 // 3
    tc = LANES if D % LANES == 0 else D
    nb = D // tc
    R = _tile(T, 264, 8)

    def body(b_ref, c_ref, h_ref, w_ref, y_ref):
        rowc = lax.broadcasted_iota(jnp.int32, (R, 1), 0)
        wv = w_ref[...]
        for r0 in range(0, T, R):
            rows = slice(r0, r0 + R)
            u = c_ref[rows, :] * h_ref[rows, :]
            if r0 == 0:
                above = jnp.zeros((8, tc), F32)
            else:
                above = c_ref[r0 - 8:r0, :] * h_ref[r0 - 8:r0, :]
            conv, _, _ = _conv3(u, above, wv, rowc)
            y_ref[rows, :] = (b_ref[rows, :] * conv).astype(BF)

    return pl.pallas_call(
        body,
        out_shape=jax.ShapeDtypeStruct((T, D), BF),
        grid=(nb,),
        in_specs=[
            pl.BlockSpec((T, tc), lambda j: (0, j)),
            pl.BlockSpec((T, tc), lambda j: (0, nb + j)),
            pl.BlockSpec((T, tc), lambda j: (0, 2 * nb + j)),
            pl.BlockSpec((3, tc), lambda j: (0, j)),
        ],
        out_specs=pl.BlockSpec((T, tc), lambda j: (0, j)),
        compiler_params=_params("parallel"),
        name=name,
    )(proj, proj, proj, w)


def conv_bwd(dy, proj, w, *, name):
    T, D = dy.shape
    tc = LANES if D % LANES == 0 else D
    nb = D // tc
    R = _tile(T, 264, 8)

    def body(dy_ref, b_ref, c_ref, h_ref, w_ref, db_ref, dc_ref, dh_ref, dw_ref):
        rowc = lax.broadcasted_iota(jnp.int32, (R, 1), 0)
        wv = w_ref[...]
        dw = [jnp.zeros((1, tc), F32) for _ in range(3)]
        for r0 in range(0, T, R):
            rows = slice(r0, r0 + R)
            c = c_ref[rows, :]
            hh = h_ref[rows, :]
            u = c * hh
            if r0 == 0:
                above = jnp.zeros((8, tc), F32)
            else:
                above = c_ref[r0 - 8:r0, :] * h_ref[r0 - 8:r0, :]
            conv, u1, u2 = _conv3(u, above, wv, rowc)
            dyv = dy_ref[rows, :]
            db_ref[rows, :] = (dyv * conv).astype(BF)
            dconv = dyv * b_ref[rows, :]
            if r0 + R == T:
                below = jnp.zeros((8, tc), F32)
            else:
                below = dy_ref[r0 + R:r0 + R + 8, :] * b_ref[r0 + R:r0 + R + 8, :]
            dw[0] = dw[0] + jnp.sum(dconv * u2, axis=0, keepdims=True)
            dw[1] = dw[1] + jnp.sum(dconv * u1, axis=0, keepdims=True)
            dw[2] = dw[2] + jnp.sum(dconv * u, axis=0, keepdims=True)
            du = (wv[2:3] * dconv + wv[1:2] * _shift_up(dconv, below, 1, rowc)
                  + wv[0:1] * _shift_up(dconv, below, 2, rowc))
            dc_ref[rows, :] = (du * hh).astype(BF)
            dh_ref[rows, :] = (du * c).astype(BF)
        for i in range(3):
            dw_ref[i:i + 1, :] = dw[i]

    strip = pl.BlockSpec((T, tc), lambda j: (0, j))
    wblk = pl.BlockSpec((3, tc), lambda j: (0, j))
    out = jax.ShapeDtypeStruct((T, D), BF)
    return pl.pallas_call(
        body,
        out_shape=(out, out, out, jax.ShapeDtypeStruct((3, D), F32)),
        grid=(nb,),
        in_specs=[
            strip,
            pl.BlockSpec((T, tc), lambda j: (0, j)),
            pl.BlockSpec((T, tc), lambda j: (0, nb + j)),
            pl.BlockSpec((T, tc), lambda j: (0, 2 * nb + j)),
            wblk,
        ],
        out_specs=(strip, strip, strip, wblk),
        compiler_params=_params("parallel"),
        name=name,
    )(dy, proj, proj, proj, w)


def _log_sigmoid(z):
    return jnp.minimum(z, 0.0) - jnp.log(1.0 + jnp.exp(-jnp.abs(z)))


def fgate_fwd(logits, bias, pad, *, name):
    T, W = logits.shape
    cb = _tile(T, 128, 8)
    nblk = T // cb

    def body(z_ref, b_ref, c_ref, lf_ref):
        row = lax.broadcasted_iota(jnp.int32, (T, 1), 0)
        lf_ref[...] = jnp.where(row >= pad, _log_sigmoid(z_ref[...] + b_ref[...]), 0.0)
        ri = lax.broadcasted_iota(jnp.int32, (cb, cb), 0)
        ci = lax.broadcasted_iota(jnp.int32, (cb, cb), 1)
        tri = (ci <= ri).astype(F32)

        def step(i, carry):
            rows = pl.ds(pl.multiple_of(i * cb, cb), cb)
            blk = lf_ref[rows, :]
            c_ref[rows, :] = carry + jnp.dot(tri, blk, precision=lax.Precision.HIGHEST,
                                             preferred_element_type=F32)
            return carry + jnp.sum(blk, axis=0, keepdims=True)

        lax.fori_loop(0, nblk, step, jnp.zeros((1, W), F32))

    return pl.pallas_call(
        body,
        out_shape=jax.ShapeDtypeStruct((T, W), F32),
        in_specs=[pl.BlockSpec(memory_space=pltpu.VMEM), pl.BlockSpec(memory_space=pltpu.VMEM)],
        out_specs=pl.BlockSpec(memory_space=pltpu.VMEM),
        scratch_shapes=[pltpu.VMEM((T, W), F32)],
        compiler_params=pltpu.CompilerParams(vmem_limit_bytes=V7X_VMEM_LIMIT),
        name=name,
    )(logits, bias)


def fgate_bwd(dc, logits, bias, pad, *, name):
    T, W = logits.shape
    cb = _tile(T, 128, 8)
    nblk = T // cb

    def body(dc_ref, z_ref, b_ref, dz_ref, db_ref, rs_ref):
        ri = lax.broadcasted_iota(jnp.int32, (cb, cb), 0)
        ci = lax.broadcasted_iota(jnp.int32, (cb, cb), 1)
        triu = (ci >= ri).astype(F32)

        def step(i, carry):
            rows = pl.ds(pl.multiple_of((nblk - 1 - i) * cb, cb), cb)
            blk = dc_ref[rows, :]
            rs_ref[rows, :] = carry + jnp.dot(triu, blk, precision=lax.Precision.HIGHEST,
                                              preferred_element_type=F32)
            return carry + jnp.sum(blk, axis=0, keepdims=True)

        lax.fori_loop(0, nblk, step, jnp.zeros((1, W), F32))
        row = lax.broadcasted_iota(jnp.int32, (T, 1), 0)
        z = z_ref[...] + b_ref[...]
        dz = jnp.where(row >= pad, rs_ref[...] * jax.nn.sigmoid(-z), 0.0)
        dz_ref[...] = dz.astype(BF)
        db_ref[...] = jnp.sum(dz, axis=0, keepdims=True)

    vm = pl.BlockSpec(memory_space=pltpu.VMEM)
    return pl.pallas_call(
        body,
        out_shape=(jax.ShapeDtypeStruct((T, W), BF), jax.ShapeDtypeStruct((1, W), F32)),
        in_specs=[vm, vm, vm],
        out_specs=(vm, vm),
        scratch_shapes=[pltpu.VMEM((T, W), F32)],
        compiler_params=pltpu.CompilerParams(vmem_limit_bytes=V7X_VMEM_LIMIT),
        name=name,
    )(dc, logits, bias)


def _scores(qb, kb, cq, ck, row, col, pad, scale):
    s = lax.dot_general(qb, kb, (((1,), (1,)), ((), ())), preferred_element_type=F32) * scale
    s = s + (cq - ck)
    return jnp.where((col <= row) & (col >= pad), s, NEG)


def attn_fwd(q, k, v, ccol, crow, pad, *, name):
    T, D = q.shape
    H = D // HEAD_DIM
    nk, tk = crow.shape[1], crow.shape[3]
    tq = tk
    nq = T // tq
    scale = 1.0 / math.sqrt(HEAD_DIM)

    def body(q_ref, k_ref, v_ref, cc_ref, cr_ref, o_ref, lse_ref):
        qi = pl.program_id(1)
        qb = q_ref[...]
        cq = cc_ref[...]
        row = qi * tq + lax.broadcasted_iota(jnp.int32, (tq, 1), 0)

        def step(kc, carry):
            m, l, acc = carry
            rows = pl.ds(pl.multiple_of(kc * tk, tk), tk)
            col = kc * tk + lax.broadcasted_iota(jnp.int32, (1, tk), 1)
            s = _scores(qb, k_ref[rows, :], cq, cr_ref[kc], row, col, pad, scale)
            m_new = jnp.maximum(m, jnp.max(s, axis=-1, keepdims=True))
            alpha = jnp.exp(m - m_new)
            p = jnp.exp(s - m_new)
            l = alpha * l + jnp.sum(p, axis=-1, keepdims=True)
            acc = alpha * acc + jnp.dot(p.astype(BF), v_ref[rows, :], preferred_element_type=F32)
            return m_new, l, acc

        init = (jnp.full((tq, 1), NEG, F32), jnp.zeros((tq, 1), F32), jnp.zeros((tq, HEAD_DIM), F32))
        m, l, acc = lax.fori_loop(0, qi + 1, step, init)
        valid = row >= pad
        o_ref[...] = jnp.where(valid, acc / l, 0.0).astype(BF)
        lse_ref[...] = jnp.where(valid, m + jnp.log(l), 0.0)

    return pl.pallas_call(
        body,
        out_shape=(jax.ShapeDtypeStruct((T, D), BF), jax.ShapeDtypeStruct((H, T, 1), F32)),
        grid=(H, nq),
        in_specs=[
            pl.BlockSpec((tq, HEAD_DIM), lambda h, i: (i, h)),
            pl.BlockSpec((T, HEAD_DIM), lambda h, i: (0, h)),
            pl.BlockSpec((T, HEAD_DIM), lambda h, i: (0, h)),
            pl.BlockSpec((None, tq, 1), lambda h, i: (h, i, 0)),
            pl.BlockSpec((None, nk, 1, tk), lambda h, i: (h, 0, 0, 0)),
        ],
        out_specs=(pl.BlockSpec((tq, HEAD_DIM), lambda h, i: (i, h)),
                   pl.BlockSpec((None, tq, 1), lambda h, i: (h, i, 0))),
        compiler_params=_params("parallel", "arbitrary"),
        name=name,
    )(q, k, v, ccol, crow)


def attn_bwd(q, k, v, do, o, lse, ccol, crow, prev, pad, *, name):
    T, D = q.shape
    H = D // HEAD_DIM
    nk, tk = crow.shape[1], crow.shape[3]
    tq = tk
    nq = T // tq
    scale = 1.0 / math.sqrt(HEAD_DIM)
    has_prev = prev is not None

    def body(*refs):
        q_ref, k_ref, v_ref, do_ref, o_ref, lse_ref, cc_ref, cr_ref = refs[:8]
        refs = refs[8:]
        if has_prev:
            pk_ref, pv_ref, pc_ref, pq_ref = refs[:4]
            refs = refs[4:]
        dq_ref, dk_ref, dv_ref, dck_ref, dcq_ref, delta_ref = refs
        kc = pl.program_id(1)

        @pl.when(kc == 0)
        def _():
            dq_ref[...] = jnp.zeros_like(dq_ref)
            dcq_ref[...] = pq_ref[...] if has_prev else jnp.zeros_like(dcq_ref)
            do_used = do_ref[...].astype(BF).astype(F32)
            delta_ref[...] = jnp.sum(do_used * o_ref[...].astype(F32), axis=-1, keepdims=True)

        kb = k_ref[...]
        vb = v_ref[...]
        ck = cr_ref[...]
        col = kc * tk + lax.broadcasted_iota(jnp.int32, (1, tk), 1)

        def step(qi, carry):
            dk, dv, dck = carry
            rows = pl.ds(pl.multiple_of(qi * tq, tq), tq)
            row = qi * tq + lax.broadcasted_iota(jnp.int32, (tq, 1), 0)
            qb = q_ref[rows, :]
            dob = do_ref[rows, :].astype(BF)
            s = _scores(qb, kb, cc_ref[rows, :], ck, row, col, pad, scale)
            p = jnp.exp(s - lse_ref[rows, :])
            dp = lax.dot_general(dob, vb, (((1,), (1,)), ((), ())), preferred_element_type=F32)
            ds = p * (dp - delta_ref[rows, :])
            dsb = ds.astype(BF)
            dv = dv + lax.dot_general(p.astype(BF), dob, (((0,), (0,)), ((), ())),
                                      preferred_element_type=F32)
            dk = dk + lax.dot_general(dsb, qb, (((0,), (0,)), ((), ())), preferred_element_type=F32)
            dq_ref[rows, :] += jnp.dot(dsb, kb, preferred_element_type=F32) * scale
            dcq_ref[rows, :] += jnp.sum(ds, axis=1, keepdims=True)
            dck = dck - jnp.sum(ds, axis=0, keepdims=True)
            return dk, dv, dck

        init = (jnp.zeros((tk, HEAD_DIM), F32), jnp.zeros((tk, HEAD_DIM), F32), jnp.zeros((1, tk), F32))
        dk, dv, dck = lax.fori_loop(kc, nq, step, init)
        dk = dk * scale
        if has_prev:
            dk = dk + pk_ref[...]
            dv = dv + pv_ref[...]
            dck = dck + pc_ref[...]
        dk_ref[...] = dk
        dv_ref[...] = dv
        dck_ref[...] = dck

    head_all = pl.BlockSpec((T, HEAD_DIM), lambda h, j: (0, h))
    head_blk = pl.BlockSpec((tk, HEAD_DIM), lambda h, j: (j, h))
    col_all = pl.BlockSpec((None, T, 1), lambda h, j: (h, 0, 0))
    row_blk = pl.BlockSpec((None, None, 1, tk), lambda h, j: (h, j, 0, 0))
    in_specs = [head_all, head_blk, head_blk, head_all, head_all, col_all, col_all, row_blk]
    args = [q, k, v, do, o, lse, ccol, crow]
    if has_prev:
        in_specs += [head_blk, head_blk, row_blk, col_all]
        args += list(prev)
    return pl.pallas_call(
        body,
        out_shape=(jax.ShapeDtypeStruct((T, D), F32), jax.ShapeDtypeStruct((T, D), F32),
                   jax.ShapeDtypeStruct((T, D), F32), jax.ShapeDtypeStruct((H, nk, 1, tk), F32),
                   jax.ShapeDtypeStruct((H, T, 1), F32)),
        grid=(H, nk),
        in_specs=in_specs,
        out_specs=(head_all, head_blk, head_blk, row_blk, col_all),
        scratch_shapes=[pltpu.VMEM((T, 1), F32)],
        compiler_params=_params("parallel", "arbitrary"),
        name=name,
    )(*args)


def loss_head(h, target, lead, *, name):
    T, D = h.shape
    tm = lead
    assert T % tm == 0 and target.shape[0] % tm == 0
    inv_d = 1.0 / D

    def body(h_ref, t_ref, dh_ref, loss_ref):
        i = pl.program_id(0)

        @pl.when(i == 0)
        def _():
            dh_ref[...] = jnp.zeros_like(dh_ref)
            loss_ref[...] = jnp.zeros_like(loss_ref)

        @pl.when(i > 0)
        def _():
            e = h_ref[...] - t_ref[...]
            dh_ref[...] = e * inv_d
            loss_ref[...] += 0.5 * inv_d * jnp.sum(e * e)

    return pl.pallas_call(
        body,
        out_shape=(jax.ShapeDtypeStruct((T, D), F32), jax.ShapeDtypeStruct((8, LANES), F32)),
        grid=(T // tm,),
        in_specs=[pl.BlockSpec((tm, D), lambda i: (i, 0)),
                  pl.BlockSpec((tm, D), lambda i: (jnp.maximum(i - 1, 0), 0))],
        out_specs=(pl.BlockSpec((tm, D), lambda i: (i, 0)),
                   pl.BlockSpec((8, LANES), lambda i: (0, 0))),
        compiler_params=_params("arbitrary"),
        name=name,
    )(h, target)


def adamw(parts, w, m, v, *, name):
    P, R, C = parts.shape
    tr = _tile(R, max(16, (128 * 1024) // C), 16)

    def body(p_ref, w_ref, m_ref, v_ref, g_ref, d_ref, mo_ref, vo_ref):
        g = p_ref[0].astype(F32)
        for i in range(1, P):
            g = g + p_ref[i].astype(F32)
        m_new = ADAM_B1 * m_ref[...] + (1.0 - ADAM_B1) * g
        v_new = ADAM_B2 * v_ref[...] + (1.0 - ADAM_B2) * jnp.square(g)
        m_hat = m_new / (1.0 - ADAM_B1 ** ADAM_STEP)
        v_hat = v_new / (1.0 - ADAM_B2 ** ADAM_STEP)
        g_ref[...] = g
        d_ref[...] = -ADAM_LR * (m_hat / (jnp.sqrt(v_hat) + ADAM_EPS) + ADAM_WD * w_ref[...])
        mo_ref[...] = m_new
        vo_ref[...] = v_new

    blk = pl.BlockSpec((tr, C), lambda i: (i, 0))
    out = jax.ShapeDtypeStruct((R, C), F32)
    return pl.pallas_call(
        body,
        out_shape=(out, out, out, out),
        grid=(R // tr,),
        in_specs=[pl.BlockSpec((P, tr, C), lambda i: (0, i, 0)), blk, blk, blk],
        out_specs=(blk, blk, blk, blk),
        compiler_params=_params("parallel"),
        name=name,
    )(parts, w, m, v)


def _flip(v, bit):
    return 1 - v if bit else v


def all_gather(shard, *, name):
    def body(x_ref, out_ref, send_sems, recv_sems, local_sem):
        x, y, c = lax.axis_index("x"), lax.axis_index("y"), lax.axis_index("c")
        me, sibling = (x, y, c), (x, y, 1 - c)
        chips = [(1 - x, y), (x, 1 - y), (1 - x, 1 - y)]

        def block(px, py, pc):
            return out_ref.at[4 * px + 2 * py + pc]

        def copy(k, blk, to, src=None):
            return pltpu.make_async_remote_copy(
                src_ref=block(*blk) if src is None else src,
                dst_ref=block(*blk),
                send_sem=send_sems.at[k],
                recv_sem=recv_sems.at[k],
                device_id=to,
                device_id_type=pl.DeviceIdType.MESH,
            )

        mine = pltpu.make_async_copy(x_ref, block(*me), local_sem)
        mine.start()
        first = [copy(0, me, sibling, src=x_ref)]
        first += [copy(1 + j, me, (*chip, c), src=x_ref) for j, chip in enumerate(chips)]
        for cp in first:
            cp.start()
        passed = [copy(4 + j, (*chip, c), sibling) for j, chip in enumerate(chips)]
        for j, chip in enumerate(chips):
            copy(1 + j, (*chip, c), me).wait_recv()
            passed[j].start()
        copy(0, sibling, me).wait_recv()
        for j, chip in enumerate(chips):
            copy(4 + j, (*chip, 1 - c), me).wait_recv()
        for cp in first + passed:
            cp.wait_send()
        mine.wait()

    return pl.pallas_call(
        body,
        out_shape=jax.ShapeDtypeStruct((N_DEV,) + shard.shape, shard.dtype),
        in_specs=[pl.BlockSpec(memory_space=pl.ANY)],
        out_specs=pl.BlockSpec(memory_space=pl.ANY),
        scratch_shapes=[pltpu.SemaphoreType.DMA((7,)), pltpu.SemaphoreType.DMA((7,)),
                        pltpu.SemaphoreType.DMA],
        name=name,
    )(shard)


def exchange_slabs(slabs, *, name):
    def body(g_ref, r_ref, send_sems, recv_sems, local_sem):
        x, y, c = lax.axis_index("x"), lax.axis_index("y"), lax.axis_index("c")
        me = 4 * x + 2 * y + c
        mine = pltpu.make_async_copy(g_ref.at[me], r_ref.at[me], local_sem)
        mine.start()
        sends, recvs = [], []
        for k in range(1, N_DEV):
            px, py, pc = _flip(x, (k >> 2) & 1), _flip(y, (k >> 1) & 1), _flip(c, k & 1)
            peer = 4 * px + 2 * py + pc
            sends.append(pltpu.make_async_remote_copy(
                src_ref=g_ref.at[peer], dst_ref=r_ref.at[me],
                send_sem=send_sems.at[k - 1], recv_sem=recv_sems.at[k - 1],
                device_id=(px, py, pc), device_id_type=pl.DeviceIdType.MESH))
            recvs.append(pltpu.make_async_remote_copy(
                src_ref=g_ref.at[peer], dst_ref=r_ref.at[peer],
                send_sem=send_sems.at[k - 1], recv_sem=recv_sems.at[k - 1],
                device_id=(px, py, pc), device_id_type=pl.DeviceIdType.MESH))
        for cp in sends:
            cp.start()
        for cp in recvs:
            cp.wait_recv()
        for cp in sends:
            cp.wait_send()
        mine.wait()

    return pl.pallas_call(
        body,
        out_shape=jax.ShapeDtypeStruct(slabs.shape, slabs.dtype),
        in_specs=[pl.BlockSpec(memory_space=pl.ANY)],
        out_specs=pl.BlockSpec(memory_space=pl.ANY),
        scratch_shapes=[pltpu.SemaphoreType.DMA((7,)), pltpu.SemaphoreType.DMA((7,)),
                        pltpu.SemaphoreType.DMA],
        name=name,
    )(slabs)


def reduce_adamw(slabs, w, m, v, *, name):
    got = exchange_slabs(slabs, name=name + "_xchg")
    return adamw(got, w, m, v, name=name + "_adamw")


def _pad_rows(a, rows):
    return jnp.pad(a, ((0, rows - a.shape[0]), (0, 0)))


def _pad_cols(a, cols):
    return jnp.pad(a, ((0, 0), (0, cols - a.shape[1])))


def kernel(x, meta, a_norm, a_w_in, a_conv, a_w_out, kv_norm, w_kv, k_norm, w_f, b_f, b_norm, b_w_q, b_q_norm, b_w_o, ffn_norm, ffn_w_gu, ffn_w_down, loss_target, m_meta, m_a_norm, m_a_w_in, m_a_conv, m_a_w_out, m_kv_norm, m_w_kv, m_k_norm, m_w_f, m_b_f, m_b_norm, m_b_w_q, m_b_q_norm, m_b_w_o, m_ffn_norm, m_ffn_w_gu, m_ffn_w_down, v_meta, v_a_norm, v_a_w_in, v_a_conv, v_a_w_out, v_kv_norm, v_w_kv, v_k_norm, v_w_f, v_b_f, v_b_norm, v_b_w_q, v_b_q_norm, v_b_w_o, v_ffn_norm, v_ffn_w_gu, v_ffn_w_down):
    S, D = x.shape[1], x.shape[2]
    n_meta = meta.shape[0]
    Ds = meta.shape[1]
    H = D // HEAD_DIM
    n_a, n_b = a_w_in.shape[0], b_w_q.shape[0]
    depth = n_a + n_b
    Fs = ffn_w_down.shape[1]
    pad = BLOCK - n_meta
    lead = pad + n_meta
    T = lead + S
    tk_attn = _tile(T, 384, LANES)
    nk_attn = T // tk_attn
    my = 4 * lax.axis_index("x") + 2 * lax.axis_index("y") + lax.axis_index("c")

    wf_t = w_f.reshape(H, Ds)
    small = jnp.concatenate([meta, _pad_rows(a_norm, 8), _pad_rows(a_conv.reshape(n_a * 3, Ds), 8), wf_t], axis=0)
    r_an, r_ac, r_wf = n_meta, n_meta + 8, n_meta + 16
    gs = all_gather(small, name="ag_small")
    unshard = lambda blk: jnp.transpose(blk, (1, 0, 2)).reshape(blk.shape[1], D)
    meta_full = unshard(gs[:, 0:n_meta])
    a_norm_full = unshard(gs[:, r_an:r_an + n_a])
    a_conv_full = unshard(gs[:, r_ac:r_ac + 3 * n_a]).reshape(n_a, 3, D)
    w_f_full = gs[:, r_wf:r_wf + H].reshape(D, H)
    wf_pad = _pad_cols(w_f_full, LANES).astype(BF)[None]
    bf_pad = _pad_cols(b_f.reshape(1, H), LANES)

    W_in = [all_gather(a_w_in[l].astype(BF), name=f"ag_w_in{l}") for l in range(n_a)]
    W_out = [all_gather(a_w_out[l].astype(BF), name=f"ag_w_out{l}").reshape(1, D, D) for l in range(n_a)]
    W_kv = all_gather(w_kv.astype(BF), name="ag_w_kv")
    W_q = [all_gather(b_w_q[j].astype(BF), name=f"ag_w_q{j}").reshape(1, D, D) for j in range(n_b)]
    W_o = [all_gather(b_w_o[j].astype(BF), name=f"ag_w_o{j}").reshape(1, D, D) for j in range(n_b)]
    W_gu = [all_gather(ffn_w_gu[l].astype(BF), name=f"ag_w_gu{l}") for l in range(depth)]
    W_dn = [all_gather(ffn_w_down[l].astype(BF), name=f"ag_w_dn{l}").reshape(1, N_DEV * Fs, D) for l in range(depth)]

    h = jnp.concatenate([jnp.zeros((pad, D), F32), meta_full, x[0]], axis=0)
    saved = []
    shared = None
    for l in range(depth):
        rec = {"h": h}
        if l < n_a:
            xn = rms_fwd(h, a_norm_full[l], name=f"a{l}_norm")
            proj = mm_nn(xn, W_in[l], name=f"a{l}_in")
            y = conv_fwd(proj, a_conv_full[l], name=f"a{l}_conv")
            h1 = mm_nn(y, W_out[l], add=h, name=f"a{l}_out")
            rec.update(xn=xn, proj=proj, y=y)
        else:
            j = l - n_a
            if j == 0:
                xnk = rms_fwd(h, kv_norm, name="kv_norm")
                kv = mm_nn(xnk, W_kv, name="kv_proj")
                k, v = kv_post(kv, k_norm, name="kv_post")
                logits = mm_nn(xnk, wf_pad, name="f_logits", tn_target=LANES)
                cfull = fgate_fwd(logits, bf_pad, pad, name="f_gate")
                c_t = jnp.transpose(cfull[:, :H])
                ccol = c_t.reshape(H, T, 1)
                crow = c_t.reshape(H, nk_attn, 1, tk_attn)
                shared = dict(h=h, xnk=xnk, kv=kv, logits=logits)
            xn = rms_fwd(h, b_norm[j], name=f"b{j}_norm")
            qraw = mm_nn(xn, W_q[j], name=f"b{j}_q")
            q = hn_fwd(qraw, b_q_norm[j], name=f"b{j}_qnorm")
            o, lse = attn_fwd(q, k, v, ccol, crow, pad, name=f"b{j}_attn")
            h1 = mm_nn(o, W_o[j], add=h, name=f"b{j}_o")
            rec.update(xn=xn, qraw=qraw, q=q, o=o, lse=lse)
        xn2 = rms_fwd(h1, ffn_norm[l], name=f"f{l}_norm")
        act, g_s, u_s = mm_swiglu(xn2, W_gu[l], name=f"f{l}_gu")
        h = mm_nn(act, W_dn[l], add=h1, name=f"f{l}_down", tk_target=1408)
        rec.update(h1=h1, xn2=xn2, act=act, g=g_s, u=u_s)
        saved.append(rec)

    dh, loss_tile = loss_head(h, loss_target[0], lead, name="loss")
    loss = lax.psum(loss_tile[0, 0], MESH_AXES)

    upd = {}
    small_g = {}

    def big(name, l, slabs, w, m, v):
        shp = w.shape
        parts = slabs.reshape(N_DEV, -1, shp[-1])
        res = reduce_adamw(parts, w.reshape(-1, shp[-1]), m.reshape(-1, shp[-1]), v.reshape(-1, shp[-1]),
                           name=f"{name}{l}")
        upd.setdefault(name, {})[l] = [r.reshape(shp) for r in res]

    dk = dv = dck = dcq = None
    for l in reversed(range(depth)):
        rec = saved[l]
        dhb = dh.astype(BF)
        dg, du = mm_nt_dswiglu(dhb, W_dn[l], rec["g"], rec["u"], name=f"f{l}_ddown")
        big("ffn_w_down", l, mm_tn(rec["act"], dhb, 1, name=f"f{l}_wdown").reshape(N_DEV, Fs, D),
            ffn_w_down[l], m_ffn_w_down[l], v_ffn_w_down[l])
        dgu = jnp.concatenate([dg, du], axis=1)
        big("ffn_w_gu", l, mm_tn(rec["xn2"], dgu, N_DEV, name=f"f{l}_wgu"),
            ffn_w_gu[l], m_ffn_w_gu[l], v_ffn_w_gu[l])
        dxn2 = mm_nt(dgu, W_gu[l], name=f"f{l}_dgu")
        dh1, dgf = rms_bwd(dxn2, rec["h1"], ffn_norm[l], dh, name=f"f{l}_dnorm")
        small_g[("ffn_norm", l)] = dgf
        dhb = dh1.astype(BF)
        if l < n_a:
            dy = mm_nt(dhb, W_out[l], name=f"a{l}_dout")
            big("a_w_out", l, mm_tn(rec["y"], dhb, 1, name=f"a{l}_wout").reshape(N_DEV, Ds, D),
                a_w_out[l], m_a_w_out[l], v_a_w_out[l])
            db, dc, dhh, dcw = conv_bwd(dy, rec["proj"], a_conv_full[l], name=f"a{l}_dconv")
            small_g[("a_conv", l)] = dcw
            dproj = jnp.concatenate([db, dc, dhh], axis=1)
            big("a_w_in", l, mm_tn(rec["xn"], dproj, N_DEV, name=f"a{l}_win"),
                a_w_in[l], m_a_w_in[l], v_a_w_in[l])
            dxn = mm_nt(dproj, W_in[l], name=f"a{l}_din")
            dh, dga = rms_bwd(dxn, rec["h"], a_norm_full[l], dh1, name=f"a{l}_dnorm")
            small_g[("a_norm", l)] = dga
        else:
            j = l - n_a
            do = mm_nt(dhb, W_o[j], name=f"b{j}_do")
            big("b_w_o", j, mm_tn(rec["o"], dhb, 1, name=f"b{j}_wo").reshape(N_DEV, Ds, D),
                b_w_o[j], m_b_w_o[j], v_b_w_o[j])
            prev = None if dk is None else (dk, dv, dck, dcq)
            dq, dk, dv, dck, dcq = attn_bwd(rec["q"], k, v, do, rec["o"], rec["lse"], ccol, crow, prev, pad,
                                       name=f"b{j}_dattn")
            dqraw, dqn = hn_bwd(dq, rec["qraw"], b_q_norm[j], name=f"b{j}_dqnorm")
            small_g[("b_q_norm", j)] = dqn
            big("b_w_q", j, mm_tn(rec["xn"], dqraw, 1, name=f"b{j}_wq").reshape(N_DEV, Ds, D),
                b_w_q[j], m_b_w_q[j], v_b_w_q[j])
            dxn = mm_nt(dqraw, W_q[j], name=f"b{j}_dq")
            dh, dgb = rms_bwd(dxn, rec["h"], b_norm[j], dh1, name=f"b{j}_dnorm")
            small_g[("b_norm", j)] = dgb
            if j == 0:
                dkraw, dkn = hn_bwd(dk, shared["kv"], k_norm, name="kv_dknorm")
                dkv = jnp.concatenate([dkraw, dv.astype(BF)], axis=1)
                dc_full = _pad_cols(jnp.transpose(dck.reshape(H, T) + dcq.reshape(H, T)), LANES)
                dz, dbf = fgate_bwd(dc_full, shared["logits"], bf_pad, pad, name="f_dgate")
                big("w_kv", 0, mm_tn(shared["xnk"], dkv, N_DEV, name="kv_wkv"), w_kv, m_w_kv, v_w_kv)
                dwf_t = mm_tn(dz, shared["xnk"], 1, name="f_wf", out_dtype=F32, tn_target=1024)[0, :H]
                dxn_f = mm_nt(dz, wf_pad, name="f_dxn")
                dxnk = mm_nt(dkv, W_kv, add=dxn_f, name="kv_dxn")
                dh, dgkv = rms_bwd(dxnk, shared["h"], kv_norm, dh, name="kv_dnorm")

    grad_x = dh[lead:][None]

    row8 = lambda a: _pad_rows(_pad_cols(a, D), 8)
    stack = lambda key, n: jnp.concatenate([small_g[(key, i)] for i in range(n)], axis=0)
    g_sharded = jnp.concatenate([dh[pad:lead], row8(stack("a_norm", n_a)), row8(stack("a_conv", n_a)), dwf_t], axis=0)
    g_repl = jnp.concatenate([row8(jnp.concatenate([dgkv, stack("b_norm", n_b)], axis=0)),
                              row8(stack("ffn_norm", depth)),
                              row8(jnp.concatenate([_pad_cols(dkn, D), _pad_cols(stack("b_q_norm", n_b), D),
                                                    _pad_cols(dbf[:, :H], D)], axis=0))], axis=0)
    n_sh = g_sharded.shape[0]
    gathered = all_gather(jnp.concatenate([g_sharded, g_repl], axis=0), name="ag_small_grads")
    parts_sh = lax.dynamic_slice_in_dim(gathered[:, :n_sh], my * Ds, Ds, axis=2)
    parts_rp = gathered[:, n_sh:]

    def pack_sh(t_meta, t_an, t_ac, t_wf):
        return jnp.concatenate([t_meta, _pad_rows(t_an, 8), _pad_rows(t_ac.reshape(n_a * 3, Ds), 8),
                                jnp.transpose(t_wf)], axis=0)

    def pack_rp(t_kv, t_bn, t_fn, t_kn, t_qn, t_bf):
        return jnp.concatenate([row8(jnp.concatenate([t_kv.reshape(1, D), t_bn], axis=0)), row8(t_fn),
                                row8(jnp.concatenate([_pad_cols(t_kn.reshape(1, -1), D), _pad_cols(t_qn, D),
                                                      _pad_cols(t_bf.reshape(1, -1), D)], axis=0))], axis=0)

    res_sh = adamw(parts_sh, pack_sh(meta, a_norm, a_conv, w_f), pack_sh(m_meta, m_a_norm, m_a_conv, m_w_f),
                   pack_sh(v_meta, v_a_norm, v_a_conv, v_w_f), name="small_sharded_adamw")
    res_rp = adamw(parts_rp, pack_rp(kv_norm, b_norm, ffn_norm, k_norm, b_q_norm, b_f),
                   pack_rp(m_kv_norm, m_b_norm, m_ffn_norm, m_k_norm, m_b_q_norm, m_b_f),
                   pack_rp(v_kv_norm, v_b_norm, v_ffn_norm, v_k_norm, v_b_q_norm, v_b_f), name="small_repl_adamw")

    def unpack(kind):
        sh, rp = res_sh[kind], res_rp[kind]
        out = {
            "meta": sh[0:n_meta],
            "a_norm": sh[r_an:r_an + n_a],
            "a_conv": sh[r_ac:r_ac + 3 * n_a].reshape(n_a, 3, Ds),
            "w_f": jnp.transpose(sh[r_wf:r_wf + H]),
            "kv_norm": rp[0],
            "b_norm": rp[1:1 + n_b],
            "ffn_norm": rp[8:8 + depth],
            "k_norm": rp[16, :HEAD_DIM],
            "b_q_norm": rp[17:17 + n_b, :HEAD_DIM],
            "b_f": rp[17 + n_b, :H],
        }
        for name, n in (("a_w_in", n_a), ("a_w_out", n_a), ("b_w_q", n_b), ("b_w_o", n_b),
                        ("ffn_w_gu", depth), ("ffn_w_down", depth)):
            out[name] = jnp.stack([upd[name][i][kind] for i in range(n)], axis=0)
        out["w_kv"] = upd["w_kv"][0][kind]
        return out

    order = ["meta", "a_norm", "a_w_in", "a_conv", "a_w_out", "kv_norm", "w_kv", "k_norm", "w_f", "b_f",
             "b_norm", "b_w_q", "b_q_norm", "b_w_o", "ffn_norm", "ffn_w_gu", "ffn_w_down"]
    outs = [loss, grad_x]
    for kind in range(4):
        vals = unpack(kind)
        outs += [vals[n] for n in order]
    return tuple(outs)
```

```python
import functools
import math

import jax
import jax.numpy as jnp
from jax import lax
from jax.experimental import pallas as pl
from jax.experimental.pallas import tpu as pltpu

N_DEV = 8
MESH_AXES = ("x", "y", "c")
EPS = 1e-6
NEG = -1e30
HEAD_DIM = 128
BLOCK = 128
LANES = 128
V7X_VMEM_LIMIT = 56 * 1024 * 1024

ADAM_LR = 0.001
ADAM_B1 = 0.9
ADAM_B2 = 0.999
ADAM_EPS = 1e-08
ADAM_WD = 0.01
ADAM_STEP = 10

BF = jnp.bfloat16
F32 = jnp.float32


def _tile(n, target, mult):
    best = None
    for t in range(mult, min(n, target) + 1, mult):
        if n % t == 0:
            best = t
    return n if best is None else best


def _params(*sem):
    return pltpu.CompilerParams(dimension_semantics=sem, vmem_limit_bytes=V7X_VMEM_LIMIT)


def mm_nn(a, w, *, name, add=None, out_dtype=F32, tm_target=1056, tn_target=1024, tk_target=2048):
    M, K = a.shape
    G, K2, n = w.shape
    assert K == K2
    tm = _tile(M, tm_target, 16)
    tn = _tile(n, tn_target, LANES)
    tk = _tile(K, tk_target, LANES)
    nj, nk = n // tn, K // tk
    has_add = add is not None

    def body(*refs):
        if has_add:
            a_ref, w_ref, add_ref, o_ref = refs[:4]
        else:
            a_ref, w_ref, o_ref = refs[:3]
            add_ref = None

        def finish(r):
            if has_add:
                r = r + add_ref[...]
            o_ref[...] = r.astype(out_dtype)

        part = jnp.dot(a_ref[...], w_ref[...], preferred_element_type=F32)
        if nk == 1:
            finish(part)
        else:
            acc_ref = refs[-1]
            k = pl.program_id(2)

            @pl.when(k == 0)
            def _():
                acc_ref[...] = part

            @pl.when(k > 0)
            def _():
                acc_ref[...] += part

            @pl.when(k == nk - 1)
            def _():
                finish(acc_ref[...])

    in_specs = [
        pl.BlockSpec((tm, tk), lambda i, j, k: (i, k)),
        pl.BlockSpec((None, tk, tn), lambda i, j, k: (j // nj, k, j % nj)),
    ]
    args = [a, w]
    if has_add:
        in_specs.append(pl.BlockSpec((tm, tn), lambda i, j, k: (i, j)))
        args.append(add)
    return pl.pallas_call(
        body,
        out_shape=jax.ShapeDtypeStruct((M, G * n), out_dtype),
        grid=(M // tm, G * nj, nk),
        in_specs=in_specs,
        out_specs=pl.BlockSpec((tm, tn), lambda i, j, k: (i, j)),
        scratch_shapes=[pltpu.VMEM((tm, tn), F32)] if nk > 1 else [],
        compiler_params=_params("parallel", "parallel", "arbitrary"),
        name=name,
    )(*args)


def mm_swiglu(xn, wgu, *, name, save_dtype=BF, tm_target=528, tk_target=512):
    M, K = xn.shape
    G, _, n = wgu.shape
    half = G // 2
    tm = _tile(M, tm_target, 16)
    tn = _tile(n, 1408, LANES)
    tk = _tile(K, tk_target, LANES)
    nj, nk = n // tn, K // tk
    Fh = half * n

    def body(a_ref, wg_ref, wu_ref, act_ref, g_ref, u_ref, accg_ref, accu_ref):
        k = pl.program_id(2)
        a = a_ref[...]
        pg = jnp.dot(a, wg_ref[...], preferred_element_type=F32)
        pu = jnp.dot(a, wu_ref[...], preferred_element_type=F32)

        @pl.when(k == 0)
        def _():
            accg_ref[...] = pg
            accu_ref[...] = pu

        @pl.when(k > 0)
        def _():
            accg_ref[...] += pg
            accu_ref[...] += pu

        @pl.when(k == nk - 1)
        def _():
            g = accg_ref[...]
            u = accu_ref[...]
            act_ref[...] = (g * jax.nn.sigmoid(g) * u).astype(BF)
            g_ref[...] = g.astype(save_dtype)
            u_ref[...] = u.astype(save_dtype)

    out_block = pl.BlockSpec((tm, tn), lambda i, j, k: (i, j))
    return pl.pallas_call(
        body,
        out_shape=(jax.ShapeDtypeStruct((M, Fh), BF),
                   jax.ShapeDtypeStruct((M, Fh), save_dtype),
                   jax.ShapeDtypeStruct((M, Fh), save_dtype)),
        grid=(M // tm, half * nj, nk),
        in_specs=[
            pl.BlockSpec((tm, tk), lambda i, j, k: (i, k)),
            pl.BlockSpec((None, tk, tn), lambda i, j, k: (j // nj, k, j % nj)),
            pl.BlockSpec((None, tk, tn), lambda i, j, k: (half + j // nj, k, j % nj)),
        ],
        out_specs=(out_block, out_block, out_block),
        scratch_shapes=[pltpu.VMEM((tm, tn), F32), pltpu.VMEM((tm, tn), F32)],
        compiler_params=_params("parallel", "parallel", "arbitrary"),
        name=name,
    )(xn, wgu, wgu)


def mm_nt(dy, w, *, name, add=None, out_dtype=F32, tm_target=1056, tko_target=1024, tc_target=1408):
    M, N = dy.shape
    G, K, n = w.shape
    assert N == G * n
    tm = _tile(M, tm_target, 16)
    tko = _tile(K, tko_target, LANES)
    tc = _tile(n, tc_target, LANES)
    nc = n // tc
    steps = G * nc
    has_add = add is not None

    def body(*refs):
        if has_add:
            dy_ref, w_ref, add_ref, o_ref = refs[:4]
        else:
            dy_ref, w_ref, o_ref = refs[:3]
            add_ref = None

        def finish(r):
            if has_add:
                r = r + add_ref[...]
            o_ref[...] = r.astype(out_dtype)

        part = lax.dot_general(dy_ref[...], w_ref[...], (((1,), (1,)), ((), ())),
                               preferred_element_type=F32)
        if steps == 1:
            finish(part)
        else:
            acc_ref = refs[-1]
            s = pl.program_id(2)

            @pl.when(s == 0)
            def _():
                acc_ref[...] = part

            @pl.when(s > 0)
            def _():
                acc_ref[...] += part

            @pl.when(s == steps - 1)
            def _():
                finish(acc_ref[...])

    in_specs = [
        pl.BlockSpec((tm, tc), lambda i, o, s: (i, s)),
        pl.BlockSpec((None, tko, tc), lambda i, o, s: (s // nc, o, s % nc)),
    ]
    args = [dy, w]
    if has_add:
        in_specs.append(pl.BlockSpec((tm, tko), lambda i, o, s: (i, o)))
        args.append(add)
    return pl.pallas_call(
        body,
        out_shape=jax.ShapeDtypeStruct((M, K), out_dtype),
        grid=(M // tm, K // tko, steps),
        in_specs=in_specs,
        out_specs=pl.BlockSpec((tm, tko), lambda i, o, s: (i, o)),
        scratch_shapes=[pltpu.VMEM((tm, tko), F32)] if steps > 1 else [],
        compiler_params=_params("parallel", "parallel", "arbitrary"),
        name=name,
    )(*args)


def mm_nt_dswiglu(dh, w_down, g_s, u_s, *, name, tm_target=1056, tf_target=512):
    M, D = dh.shape
    _, Fh, D2 = w_down.shape
    assert D == D2
    tm = _tile(M, tm_target, 16)
    tf = _tile(Fh, tf_target, LANES)

    def body(dh_ref, w_ref, g_ref, u_ref, dg_ref, du_ref):
        dact = lax.dot_general(dh_ref[...], w_ref[...], (((1,), (1,)), ((), ())),
                               preferred_element_type=F32)
        g = g_ref[...].astype(F32)
        u = u_ref[...].astype(F32)
        sig = jax.nn.sigmoid(g)
        du_ref[...] = (dact * (g * sig)).astype(BF)
        dg_ref[...] = (dact * u * (sig * (1.0 + g * (1.0 - sig)))).astype(BF)

    blk = pl.BlockSpec((tm, tf), lambda i, f: (i, f))
    return pl.pallas_call(
        body,
        out_shape=(jax.ShapeDtypeStruct((M, Fh), BF), jax.ShapeDtypeStruct((M, Fh), BF)),
        grid=(M // tm, Fh // tf),
        in_specs=[
            pl.BlockSpec((tm, D), lambda i, f: (i, 0)),
            pl.BlockSpec((None, tf, D), lambda i, f: (0, f, 0)),
            blk, blk,
        ],
        out_specs=(blk, blk),
        compiler_params=_params("parallel", "parallel"),
        name=name,
    )(dh, w_down, g_s, u_s)


def mm_tn(a, dy, groups, *, name, out_dtype=BF, tk_target=512, tn_target=1408):
    M, K = a.shape
    M2, N = dy.shape
    assert M == M2 and N % groups == 0
    n = N // groups
    tk = _tile(K, tk_target, LANES)
    tn = _tile(n, tn_target, LANES)
    nj = n // tn

    def body(a_ref, dy_ref, o_ref):
        o_ref[...] = lax.dot_general(a_ref[...], dy_ref[...], (((0,), (0,)), ((), ())),
                                     preferred_element_type=F32).astype(out_dtype)

    return pl.pallas_call(
        body,
        out_shape=jax.ShapeDtypeStruct((groups, K, n), out_dtype),
        grid=(K // tk, groups * nj),
        in_specs=[
            pl.BlockSpec((M, tk), lambda i, j: (0, i)),
            pl.BlockSpec((M, tn), lambda i, j: (0, j)),
        ],
        out_specs=pl.BlockSpec((None, tk, tn), lambda i, j: (j // nj, i, j % nj)),
        compiler_params=_params("parallel", "parallel"),
        name=name,
    )(a, dy)


def rms_fwd(h, g, *, name):
    T, D = h.shape
    tm = _tile(T, 528, 16)

    def body(h_ref, g_ref, o_ref):
        x = h_ref[...]
        r = lax.rsqrt(jnp.mean(x * x, axis=-1, keepdims=True) + EPS)
        o_ref[...] = ((x * r) * g_ref[...]).astype(BF)

    return pl.pallas_call(
        body,
        out_shape=jax.ShapeDtypeStruct((T, D), BF),
        grid=(T // tm,),
        in_specs=[pl.BlockSpec((tm, D), lambda i: (i, 0)), pl.BlockSpec((1, D), lambda i: (0, 0))],
        out_specs=pl.BlockSpec((tm, D), lambda i: (i, 0)),
        compiler_params=_params("parallel"),
        name=name,
    )(h, g.reshape(1, D))


def rms_bwd(dxn, h, g, add, *, name):
    T, D = h.shape
    tm = _tile(T, 264, 8)

    def body(dxn_ref, h_ref, g_ref, add_ref, dh_ref, dg_ref):
        x = h_ref[...]
        dy = dxn_ref[...]
        r = lax.rsqrt(jnp.mean(x * x, axis=-1, keepdims=True) + EPS)
        xhat = x * r
        part = jnp.sum(dy * xhat, axis=0, keepdims=True)

        @pl.when(pl.program_id(0) == 0)
        def _():
            dg_ref[...] = part

        @pl.when(pl.program_id(0) > 0)
        def _():
            dg_ref[...] += part

        dxh = dy * g_ref[...]
        dh_ref[...] = add_ref[...] + r * (dxh - xhat * jnp.mean(dxh * xhat, axis=-1, keepdims=True))

    row = pl.BlockSpec((tm, D), lambda i: (i, 0))
    vec = pl.BlockSpec((1, D), lambda i: (0, 0))
    return pl.pallas_call(
        body,
        out_shape=(jax.ShapeDtypeStruct((T, D), F32), jax.ShapeDtypeStruct((1, D), F32)),
        grid=(T // tm,),
        in_specs=[row, row, vec, row],
        out_specs=(row, vec),
        compiler_params=_params("arbitrary"),
        name=name,
    )(dxn, h, g.reshape(1, D), add)


def _head_norm(x, gain):
    r = lax.rsqrt(jnp.mean(x * x, axis=-1, keepdims=True) + EPS)
    return (x * r) * gain


def hn_fwd(qraw, gain, *, name):
    T, D = qraw.shape
    H = D // HEAD_DIM
    tm = _tile(T, 528, 16)

    def body(q_ref, g_ref, o_ref):
        gain_v = g_ref[...]
        for hd in range(H):
            sl = slice(hd * HEAD_DIM, (hd + 1) * HEAD_DIM)
            o_ref[:, sl] = _head_norm(q_ref[:, sl], gain_v).astype(BF)

    return pl.pallas_call(
        body,
        out_shape=jax.ShapeDtypeStruct((T, D), BF),
        grid=(T // tm,),
        in_specs=[pl.BlockSpec((tm, D), lambda i: (i, 0)),
                  pl.BlockSpec((1, HEAD_DIM), lambda i: (0, 0))],
        out_specs=pl.BlockSpec((tm, D), lambda i: (i, 0)),
        compiler_params=_params("parallel"),
        name=name,
    )(qraw, gain.reshape(1, HEAD_DIM))


def kv_post(kv, gain, *, name):
    T, D2 = kv.shape
    D = D2 // 2
    H = D // HEAD_DIM
    tm = _tile(T, 528, 16)

    def body(k_ref, v_ref, g_ref, ko_ref, vo_ref):
        gain_v = g_ref[...]
        for hd in range(H):
            sl = slice(hd * HEAD_DIM, (hd + 1) * HEAD_DIM)
            ko_ref[:, sl] = _head_norm(k_ref[:, sl], gain_v).astype(BF)
        vo_ref[...] = v_ref[...].astype(BF)

    blk = pl.BlockSpec((tm, D), lambda i: (i, 0))
    return pl.pallas_call(
        body,
        out_shape=(jax.ShapeDtypeStruct((T, D), BF), jax.ShapeDtypeStruct((T, D), BF)),
        grid=(T // tm,),
        in_specs=[blk, pl.BlockSpec((tm, D), lambda i: (i, 1)),
                  pl.BlockSpec((1, HEAD_DIM), lambda i: (0, 0))],
        out_specs=(blk, blk),
        compiler_params=_params("parallel"),
        name=name,
    )(kv, kv, gain.reshape(1, HEAD_DIM))


def hn_bwd(dq, qraw, gain, *, name):
    T, D = dq.shape
    H = D // HEAD_DIM
    tm = _tile(T, 264, 16)

    def body(dq_ref, q_ref, g_ref, o_ref, dg_ref):
        gain_v = g_ref[...]
        part = jnp.zeros((1, HEAD_DIM), F32)
        for hd in range(H):
            sl = slice(hd * HEAD_DIM, (hd + 1) * HEAD_DIM)
            x = q_ref[:, sl]
            dy = dq_ref[:, sl]
            r = lax.rsqrt(jnp.mean(x * x, axis=-1, keepdims=True) + EPS)
            xhat = x * r
            part = part + jnp.sum(dy * xhat, axis=0, keepdims=True)
            dxh = dy * gain_v
            o_ref[:, sl] = (r * (dxh - xhat * jnp.mean(dxh * xhat, axis=-1, keepdims=True))).astype(BF)

        @pl.when(pl.program_id(0) == 0)
        def _():
            dg_ref[...] = part

        @pl.when(pl.program_id(0) > 0)
        def _():
            dg_ref[...] += part

    blk = pl.BlockSpec((tm, D), lambda i: (i, 0))
    vec = pl.BlockSpec((1, HEAD_DIM), lambda i: (0, 0))
    return pl.pallas_call(
        body,
        out_shape=(jax.ShapeDtypeStruct((T, D), BF), jax.ShapeDtypeStruct((1, HEAD_DIM), F32)),
        grid=(T // tm,),
        in_specs=[blk, blk, vec],
        out_specs=(blk, vec),
        compiler_params=_params("arbitrary"),
        name=name,
    )(dq, qraw, gain.reshape(1, HEAD_DIM))


def _shift_down(cur, above, k, rowc):
    out = pltpu.roll(cur, k, 0)
    for i in range(k):
        out = jnp.where(rowc == i, above[8 - k + i:8 - k + i + 1], out)
    return out


def _shift_up(cur, below, k, rowc):
    R = cur.shape[0]
    out = pltpu.roll(cur, R - k, 0)
    for i in range(k):
        out = jnp.where(rowc == R - k + i, below[i:i + 1], out)
    return out


def _conv3(u, u_above, wv, rowc):
    u1 = _shift_down(u, u_above, 1, rowc)
    u2 = _shift_down(u, u_above, 2, rowc)
    return wv[0:1] * u2 + wv[1:2] * u1 + wv[2:3] * u, u1, u2


def conv_fwd(proj, w, *, name):
    T, D3 = proj.shape
    D = D3 // 3
    tc = LANES if D % LANES == 0 else D
    nb = D // tc
    R = _tile(T, 264, 8)

    def body(b_ref, c_ref, h_ref, w_ref, y_ref):
        rowc = lax.broadcasted_iota(jnp.int32, (R, 1), 0)
        wv = w_ref[...]
        for r0 in range(0, T, R):
            rows = slice(r0, r0 + R)
            u = c_ref[rows, :] * h_ref[rows, :]
            if r0 == 0:
                above = jnp.zeros((8, tc), F32)
            else:
                above = c_ref[r0 - 8:r0, :] * h_ref[r0 - 8:r0, :]
            conv, _, _ = _conv3(u, above, wv, rowc)
            y_ref[rows, :] = (b_ref[rows, :] * conv).astype(BF)

    return pl.pallas_call(
        body,
        out_shape=jax.ShapeDtypeStruct((T, D), BF),
        grid=(nb,),
        in_specs=[
            pl.BlockSpec((T, tc), lambda j: (0, j)),
            pl.BlockSpec((T, tc), lambda j: (0, nb + j)),
            pl.BlockSpec((T, tc), lambda j: (0, 2 * nb + j)),
            pl.BlockSpec((3, tc), lambda j: (0, j)),
        ],
        out_specs=pl.BlockSpec((T, tc), lambda j: (0, j)),
        compiler_params=_params("parallel"),
        name=name,
    )(proj, proj, proj, w)


def conv_bwd(dy, proj, w, *, name):
    T, D = dy.shape
    tc = LANES if D % LANES == 0 else D
    nb = D // tc
    R = _tile(T, 264, 8)

    def body(dy_ref, b_ref, c_ref, h_ref, w_ref, db_ref, dc_ref, dh_ref, dw_ref):
        rowc = lax.broadcasted_iota(jnp.int32, (R, 1), 0)
        wv = w_ref[...]
        dw = [jnp.zeros((1, tc), F32) for _ in range(3)]
        for r0 in range(0, T, R):
            rows = slice(r0, r0 + R)
            c = c_ref[rows, :]
            hh = h_ref[rows, :]
            u = c * hh
            if r0 == 0:
                above = jnp.zeros((8, tc), F32)
            else:
                above = c_ref[r0 - 8:r0, :] * h_ref[r0 - 8:r0, :]
            conv, u1, u2 = _conv3(u, above, wv, rowc)
            dyv = dy_ref[rows, :]
            db_ref[rows, :] = (dyv * conv).astype(BF)
            dconv = dyv * b_ref[rows, :]
            if r0 + R == T:
                below = jnp.zeros((8, tc), F32)
            else:
                below = dy_ref[r0 + R:r0 + R + 8, :] * b_ref[r0 + R:r0 + R + 8, :]
            dw[0] = dw[0] + jnp.sum(dconv * u2, axis=0, keepdims=True)
            dw[1] = dw[1] + jnp.sum(dconv * u1, axis=0, keepdims=True)
            dw[2] = dw[2] + jnp.sum(dconv * u, axis=0, keepdims=True)
            du = (wv[2:3] * dconv + wv[1:2] * _shift_up(dconv, below, 1, rowc)
                  + wv[0:1] * _shift_up(dconv, below, 2, rowc))
            dc_ref[rows, :] = (du * hh).astype(BF)
            dh_ref[rows, :] = (du * c).astype(BF)
        for i in range(3):
            dw_ref[i:i + 1, :] = dw[i]

    strip = pl.BlockSpec((T, tc), lambda j: (0, j))
    wblk = pl.BlockSpec((3, tc), lambda j: (0, j))
    out = jax.ShapeDtypeStruct((T, D), BF)
    return pl.pallas_call(
        body,
        out_shape=(out, out, out, jax.ShapeDtypeStruct((3, D), F32)),
        grid=(nb,),
        in_specs=[
            strip,
            pl.BlockSpec((T, tc), lambda j: (0, j)),
            pl.BlockSpec((T, tc), lambda j: (0, nb + j)),
            pl.BlockSpec((T, tc), lambda j: (0, 2 * nb + j)),
            wblk,
        ],
        out_specs=(strip, strip, strip, wblk),
        compiler_params=_params("parallel"),
        name=name,
    )(dy, proj, proj, proj, w)


def _log_sigmoid(z):
    return jnp.minimum(z, 0.0) - jnp.log(1.0 + jnp.exp(-jnp.abs(z)))


def fgate_fwd(logits, bias, pad, *, name):
    T, W = logits.shape
    cb = _tile(T, 128, 8)
    nblk = T // cb

    def body(z_ref, b_ref, c_ref, lf_ref):
        row = lax.broadcasted_iota(jnp.int32, (T, 1), 0)
        lf_ref[...] = jnp.where(row >= pad, _log_sigmoid(z_ref[...] + b_ref[...]), 0.0)
        ri = lax.broadcasted_iota(jnp.int32, (cb, cb), 0)
        ci = lax.broadcasted_iota(jnp.int32, (cb, cb), 1)
        tri = (ci <= ri).astype(F32)

        def step(i, carry):
            rows = pl.ds(pl.multiple_of(i * cb, cb), cb)
            blk = lf_ref[rows, :]
            c_ref[rows, :] = carry + jnp.dot(tri, blk, precision=lax.Precision.HIGHEST,
                                             preferred_element_type=F32)
            return carry + jnp.sum(blk, axis=0, keepdims=True)

        lax.fori_loop(0, nblk, step, jnp.zeros((1, W), F32))

    return pl.pallas_call(
        body,
        out_shape=jax.ShapeDtypeStruct((T, W), F32),
        in_specs=[pl.BlockSpec(memory_space=pltpu.VMEM), pl.BlockSpec(memory_space=pltpu.VMEM)],
        out_specs=pl.BlockSpec(memory_space=pltpu.VMEM),
        scratch_shapes=[pltpu.VMEM((T, W), F32)],
        compiler_params=pltpu.CompilerParams(vmem_limit_bytes=V7X_VMEM_LIMIT),
        name=name,
    )(logits, bias)


def fgate_bwd(dc, logits, bias, pad, *, name):
    T, W = logits.shape
    cb = _tile(T, 128, 8)
    nblk = T // cb

    def body(dc_ref, z_ref, b_ref, dz_ref, db_ref, rs_ref):
        ri = lax.broadcasted_iota(jnp.int32, (cb, cb), 0)
        ci = lax.broadcasted_iota(jnp.int32, (cb, cb), 1)
        triu = (ci >= ri).astype(F32)

        def step(i, carry):
            rows = pl.ds(pl.multiple_of((nblk - 1 - i) * cb, cb), cb)
            blk = dc_ref[rows, :]
            rs_ref[rows, :] = carry + jnp.dot(triu, blk, precision=lax.Precision.HIGHEST,
                                              preferred_element_type=F32)
            return carry + jnp.sum(blk, axis=0, keepdims=True)

        lax.fori_loop(0, nblk, step, jnp.zeros((1, W), F32))
        row = lax.broadcasted_iota(jnp.int32, (T, 1), 0)
        z = z_ref[...] + b_ref[...]
        dz = jnp.where(row >= pad, rs_ref[...] * jax.nn.sigmoid(-z), 0.0)
        dz_ref[...] = dz.astype(BF)
        db_ref[...] = jnp.sum(dz, axis=0, keepdims=True)

    vm = pl.BlockSpec(memory_space=pltpu.VMEM)
    return pl.pallas_call(
        body,
        out_shape=(jax.ShapeDtypeStruct((T, W), BF), jax.ShapeDtypeStruct((1, W), F32)),
        in_specs=[vm, vm, vm],
        out_specs=(vm, vm),
        scratch_shapes=[pltpu.VMEM((T, W), F32)],
        compiler_params=pltpu.CompilerParams(vmem_limit_bytes=V7X_VMEM_LIMIT),
        name=name,
    )(dc, logits, bias)


def _scores(qb, kb, cq, ck, row, col, pad, scale):
    s = lax.dot_general(qb, kb, (((1,), (1,)), ((), ())), preferred_element_type=F32) * scale
    s = s + (cq - ck)
    return jnp.where((col <= row) & (col >= pad), s, NEG)


def attn_fwd(q, k, v, ccol, crow, pad, *, name):
    T, D = q.shape
    H = D // HEAD_DIM
    nk, tk = crow.shape[1], crow.shape[3]
    tq = tk
    nq = T // tq
    scale = 1.0 / math.sqrt(HEAD_DIM)

    def body(q_ref, k_ref, v_ref, cc_ref, cr_ref, o_ref, lse_ref):
        qi = pl.program_id(1)
        qb = q_ref[...]
        cq = cc_ref[...]
        row = qi * tq + lax.broadcasted_iota(jnp.int32, (tq, 1), 0)

        def step(kc, carry):
            m, l, acc = carry
            rows = pl.ds(pl.multiple_of(kc * tk, tk), tk)
            col = kc * tk + lax.broadcasted_iota(jnp.int32, (1, tk), 1)
            s = _scores(qb, k_ref[rows, :], cq, cr_ref[kc], row, col, pad, scale)
            m_new = jnp.maximum(m, jnp.max(s, axis=-1, keepdims=True))
            alpha = jnp.exp(m - m_new)
            p = jnp.exp(s - m_new)
            l = alpha * l + jnp.sum(p, axis=-1, keepdims=True)
            acc = alpha * acc + jnp.dot(p.astype(BF), v_ref[rows, :], preferred_element_type=F32)
            return m_new, l, acc

        init = (jnp.full((tq, 1), NEG, F32), jnp.zeros((tq, 1), F32), jnp.zeros((tq, HEAD_DIM), F32))
        m, l, acc = lax.fori_loop(0, qi + 1, step, init)
        valid = row >= pad
        o_ref[...] = jnp.where(valid, acc / l, 0.0).astype(BF)
        lse_ref[...] = jnp.where(valid, m + jnp.log(l), 0.0)

    return pl.pallas_call(
        body,
        out_shape=(jax.ShapeDtypeStruct((T, D), BF), jax.ShapeDtypeStruct((H, T, 1), F32)),
        grid=(H, nq),
        in_specs=[
            pl.BlockSpec((tq, HEAD_DIM), lambda h, i: (i, h)),
            pl.BlockSpec((T, HEAD_DIM), lambda h, i: (0, h)),
            pl.BlockSpec((T, HEAD_DIM), lambda h, i: (0, h)),
            pl.BlockSpec((None, tq, 1), lambda h, i: (h, i, 0)),
            pl.BlockSpec((None, nk, 1, tk), lambda h, i: (h, 0, 0, 0)),
        ],
        out_specs=(pl.BlockSpec((tq, HEAD_DIM), lambda h, i: (i, h)),
                   pl.BlockSpec((None, tq, 1), lambda h, i: (h, i, 0))),
        compiler_params=_params("parallel", "arbitrary"),
        name=name,
    )(q, k, v, ccol, crow)


def attn_bwd(q, k, v, do, o, lse, ccol, crow, prev, pad, *, name):
    T, D = q.shape
    H = D // HEAD_DIM
    nk, tk = crow.shape[1], crow.shape[3]
    tq = tk
    nq = T // tq
    scale = 1.0 / math.sqrt(HEAD_DIM)
    has_prev = prev is not None

    def body(*refs):
        q_ref, k_ref, v_ref, do_ref, o_ref, lse_ref, cc_ref, cr_ref = refs[:8]
        refs = refs[8:]
        if has_prev:
            pk_ref, pv_ref, pc_ref, pq_ref = refs[:4]
            refs = refs[4:]
        dq_ref, dk_ref, dv_ref, dck_ref, dcq_ref, delta_ref = refs
        kc = pl.program_id(1)

        @pl.when(kc == 0)
        def _():
            dq_ref[...] = jnp.zeros_like(dq_ref)
            dcq_ref[...] = pq_ref[...] if has_prev else jnp.zeros_like(dcq_ref)
            do_used = do_ref[...].astype(BF).astype(F32)
            delta_ref[...] = jnp.sum(do_used * o_ref[...].astype(F32), axis=-1, keepdims=True)

        kb = k_ref[...]
        vb = v_ref[...]
        ck = cr_ref[...]
        col = kc * tk + lax.broadcasted_iota(jnp.int32, (1, tk), 1)

        def step(qi, carry):
            dk, dv, dck = carry
            rows = pl.ds(pl.multiple_of(qi * tq, tq), tq)
            row = qi * tq + lax.broadcasted_iota(jnp.int32, (tq, 1), 0)
            qb = q_ref[rows, :]
            dob = do_ref[rows, :].astype(BF)
            s = _scores(qb, kb, cc_ref[rows, :], ck, row, col, pad, scale)
            p = jnp.exp(s - lse_ref[rows, :])
            dp = lax.dot_general(dob, vb, (((1,), (1,)), ((), ())), preferred_element_type=F32)
            ds = p * (dp - delta_ref[rows, :])
            dsb = ds.astype(BF)
            dv = dv + lax.dot_general(p.astype(BF), dob, (((0,), (0,)), ((), ())),
                                      preferred_element_type=F32)
            dk = dk + lax.dot_general(dsb, qb, (((0,), (0,)), ((), ())), preferred_element_type=F32)
            dq_ref[rows, :] += jnp.dot(dsb, kb, preferred_element_type=F32) * scale
            dcq_ref[rows, :] += jnp.sum(ds, axis=1, keepdims=True)
            dck = dck - jnp.sum(ds, axis=0, keepdims=True)
            return dk, dv, dck

        init = (jnp.zeros((tk, HEAD_DIM), F32), jnp.zeros((tk, HEAD_DIM), F32), jnp.zeros((1, tk), F32))
        dk, dv, dck = lax.fori_loop(kc, nq, step, init)
        dk = dk * scale
        if has_prev:
            dk = dk + pk_ref[...]
            dv = dv + pv_ref[...]
            dck = dck + pc_ref[...]
        dk_ref[...] = dk
        dv_ref[...] = dv
        dck_ref[...] = dck

    head_all = pl.BlockSpec((T, HEAD_DIM), lambda h, j: (0, h))
    head_blk = pl.BlockSpec((tk, HEAD_DIM), lambda h, j: (j, h))
    col_all = pl.BlockSpec((None, T, 1), lambda h, j: (h, 0, 0))
    row_blk = pl.BlockSpec((None, None, 1, tk), lambda h, j: (h, j, 0, 0))
    in_specs = [head_all, head_blk, head_blk, head_all, head_all, col_all, col_all, row_blk]
    args = [q, k, v, do, o, lse, ccol, crow]
    if has_prev:
        in_specs += [head_blk, head_blk, row_blk, col_all]
        args += list(prev)
    return pl.pallas_call(
        body,
        out_shape=(jax.ShapeDtypeStruct((T, D), F32), jax.ShapeDtypeStruct((T, D), F32),
                   jax.ShapeDtypeStruct((T, D), F32), jax.ShapeDtypeStruct((H, nk, 1, tk), F32),
                   jax.ShapeDtypeStruct((H, T, 1), F32)),
        grid=(H, nk),
        in_specs=in_specs,
        out_specs=(head_all, head_blk, head_blk, row_blk, col_all),
        scratch_shapes=[pltpu.VMEM((T, 1), F32)],
        compiler_params=_params("parallel", "arbitrary"),
        name=name,
    )(*args)


def loss_head(h, target, lead, *, name):
    T, D = h.shape
    tm = lead
    assert T % tm == 0 and target.shape[0] % tm == 0
    inv_d = 1.0 / D

    def body(h_ref, t_ref, dh_ref, loss_ref):
        i = pl.program_id(0)

        @pl.when(i == 0)
        def _():
            dh_ref[...] = jnp.zeros_like(dh_ref)
            loss_ref[...] = jnp.zeros_like(loss_ref)

        @pl.when(i > 0)
        def _():
            e = h_ref[...] - t_ref[...]
            dh_ref[...] = e * inv_d
            loss_ref[...] += 0.5 * inv_d * jnp.sum(e * e)

    return pl.pallas_call(
        body,
        out_shape=(jax.ShapeDtypeStruct((T, D), F32), jax.ShapeDtypeStruct((8, LANES), F32)),
        grid=(T // tm,),
        in_specs=[pl.BlockSpec((tm, D), lambda i: (i, 0)),
                  pl.BlockSpec((tm, D), lambda i: (jnp.maximum(i - 1, 0), 0))],
        out_specs=(pl.BlockSpec((tm, D), lambda i: (i, 0)),
                   pl.BlockSpec((8, LANES), lambda i: (0, 0))),
        compiler_params=_params("arbitrary"),
        name=name,
    )(h, target)


def adamw(parts, w, m, v, *, name):
    P, R, C = parts.shape
    tr = _tile(R, max(16, (128 * 1024) // C), 16)

    def body(p_ref, w_ref, m_ref, v_ref, g_ref, d_ref, mo_ref, vo_ref):
        g = p_ref[0].astype(F32)
        for i in range(1, P):
            g = g + p_ref[i].astype(F32)
        m_new = ADAM_B1 * m_ref[...] + (1.0 - ADAM_B1) * g
        v_new = ADAM_B2 * v_ref[...] + (1.0 - ADAM_B2) * jnp.square(g)
        m_hat = m_new / (1.0 - ADAM_B1 ** ADAM_STEP)
        v_hat = v_new / (1.0 - ADAM_B2 ** ADAM_STEP)
        g_ref[...] = g
        d_ref[...] = -ADAM_LR * (m_hat / (jnp.sqrt(v_hat) + ADAM_EPS) + ADAM_WD * w_ref[...])
        mo_ref[...] = m_new
        vo_ref[...] = v_new

    blk = pl.BlockSpec((tr, C), lambda i: (i, 0))
    out = jax.ShapeDtypeStruct((R, C), F32)
    return pl.pallas_call(
        body,
        out_shape=(out, out, out, out),
        grid=(R // tr,),
        in_specs=[pl.BlockSpec((P, tr, C), lambda i: (0, i, 0)), blk, blk, blk],
        out_specs=(blk, blk, blk, blk),
        compiler_params=_params("parallel"),
        name=name,
    )(parts, w, m, v)


def _flip(v, bit):
    return 1 - v if bit else v


def all_gather(shard, *, name):
    def body(x_ref, out_ref, send_sems, recv_sems, local_sem):
        x, y, c = lax.axis_index("x"), lax.axis_index("y"), lax.axis_index("c")
        me, sibling = (x, y, c), (x, y, 1 - c)
        chips = [(1 - x, y), (x, 1 - y), (1 - x, 1 - y)]

        def block(px, py, pc):
            return out_ref.at[4 * px + 2 * py + pc]

        def copy(k, blk, to, src=None):
            return pltpu.make_async_remote_copy(
                src_ref=block(*blk) if src is None else src,
                dst_ref=block(*blk),
                send_sem=send_sems.at[k],
                recv_sem=recv_sems.at[k],
                device_id=to,
                device_id_type=pl.DeviceIdType.MESH,
            )

        mine = pltpu.make_async_copy(x_ref, block(*me), local_sem)
        mine.start()
        first = [copy(0, me, sibling, src=x_ref)]
        first += [copy(1 + j, me, (*chip, c), src=x_ref) for j, chip in enumerate(chips)]
        for cp in first:
            cp.start()
        passed = [copy(4 + j, (*chip, c), sibling) for j, chip in enumerate(chips)]
        for j, chip in enumerate(chips):
            copy(1 + j, (*chip, c), me).wait_recv()
            passed[j].start()
        copy(0, sibling, me).wait_recv()
        for j, chip in enumerate(chips):
            copy(4 + j, (*chip, 1 - c), me).wait_recv()
        for cp in first + passed:
            cp.wait_send()
        mine.wait()

    return pl.pallas_call(
        body,
        out_shape=jax.ShapeDtypeStruct((N_DEV,) + shard.shape, shard.dtype),
        in_specs=[pl.BlockSpec(memory_space=pl.ANY)],
        out_specs=pl.BlockSpec(memory_space=pl.ANY),
        scratch_shapes=[pltpu.SemaphoreType.DMA((7,)), pltpu.SemaphoreType.DMA((7,)),
                        pltpu.SemaphoreType.DMA],
        name=name,
    )(shard)


def exchange_slabs(slabs, *, name):
    def body(g_ref, r_ref, send_sems, recv_sems, local_sem):
        x, y, c = lax.axis_index("x"), lax.axis_index("y"), lax.axis_index("c")
        me = 4 * x + 2 * y + c
        mine = pltpu.make_async_copy(g_ref.at[me], r_ref.at[me], local_sem)
        mine.start()
        sends, recvs = [], []
        for k in range(1, N_DEV):
            px, py, pc = _flip(x, (k >> 2) & 1), _flip(y, (k >> 1) & 1), _flip(c, k & 1)
            peer = 4 * px + 2 * py + pc
            sends.append(pltpu.make_async_remote_copy(
                src_ref=g_ref.at[peer], dst_ref=r_ref.at[me],
                send_sem=send_sems.at[k - 1], recv_sem=recv_sems.at[k - 1],
                device_id=(px, py, pc), device_id_type=pl.DeviceIdType.MESH))
            recvs.append(pltpu.make_async_remote_copy(
                src_ref=g_ref.at[peer], dst_ref=r_ref.at[peer],
                send_sem=send_sems.at[k - 1], recv_sem=recv_sems.at[k - 1],
                device_id=(px, py, pc), device_id_type=pl.DeviceIdType.MESH))
        for cp in sends:
            cp.start()
        for cp in recvs:
            cp.wait_recv()
        for cp in sends:
            cp.wait_send()
        mine.wait()

    return pl.pallas_call(
        body,
        out_shape=jax.ShapeDtypeStruct(slabs.shape, slabs.dtype),
        in_specs=[pl.BlockSpec(memory_space=pl.ANY)],
        out_specs=pl.BlockSpec(memory_space=pl.ANY),
        scratch_shapes=[pltpu.SemaphoreType.DMA((7,)), pltpu.SemaphoreType.DMA((7,)),
                        pltpu.SemaphoreType.DMA],
        name=name,
    )(slabs)


def reduce_adamw(slabs, w, m, v, *, name):
    got = exchange_slabs(slabs, name=name + "_xchg")
    return adamw(got, w, m, v, name=name + "_adamw")


_HBM = pl.BlockSpec(memory_space=pltpu.HBM)
_SEM = pl.BlockSpec(memory_space=pltpu.SEMAPHORE)
_ANY = pl.BlockSpec(memory_space=pl.ANY)
_EFFECT = pltpu.SideEffectType.DATAFLOW_SIDE_EFFECTING
_N_FIRST = 4


def behind(x, token):
    return lax.optimization_barrier((x, token))[0]


def _first_copies(x_ref, land_ref, send_sems, recv_sems):
    x, y, c = lax.axis_index("x"), lax.axis_index("y"), lax.axis_index("c")
    me = 4 * x + 2 * y + c
    targets = [(x, y, 1 - c), (1 - x, y, c), (x, 1 - y, c), (1 - x, 1 - y, c)]
    sends, recvs = [], []
    for k, (px, py, pc) in enumerate(targets):
        common = dict(send_sem=send_sems.at[k], recv_sem=recv_sems.at[k], device_id=(px, py, pc),
                      device_id_type=pl.DeviceIdType.MESH)
        sends.append(pltpu.make_async_remote_copy(src_ref=x_ref, dst_ref=land_ref.at[me], **common))
        recvs.append(pltpu.make_async_remote_copy(src_ref=x_ref, dst_ref=land_ref.at[4 * px + 2 * py + pc],
                                                  **common))
    return sends, recvs


def gather_start(shard, after, *, name):
    land = lax.empty((N_DEV,) + shard.shape, shard.dtype)

    def body(x_ref, land_ref, after_ref, send_sems, recv_sems, x_thru, land_thru, token):
        sends, _ = _first_copies(x_ref, land_ref, send_sems, recv_sems)
        for cp in sends:
            cp.start()
        token[...] = jnp.zeros_like(token)

    send_sems, recv_sems, x_thru, land_thru, token = pl.pallas_call(
        body, name=name,
        out_shape=(pltpu.SemaphoreType.DMA((_N_FIRST,)), pltpu.SemaphoreType.DMA((_N_FIRST,)),
                   pltpu.HBM(shard.shape, shard.dtype), pltpu.HBM(land.shape, land.dtype),
                   jax.ShapeDtypeStruct((8, LANES), F32)),
        in_specs=(_HBM, _HBM, _ANY),
        out_specs=(_SEM, _SEM, _HBM, _HBM, pl.BlockSpec(memory_space=pltpu.VMEM)),
        input_output_aliases={0: 2, 1: 3},
        compiler_params=pltpu.CompilerParams(has_side_effects=_EFFECT),
    )(pltpu.with_memory_space_constraint(shard, pltpu.HBM),
      pltpu.with_memory_space_constraint(land, pltpu.HBM), after)
    return (send_sems, recv_sems, x_thru, land_thru), token


def gather_finish(handle, after, *, name):
    send_sems, recv_sems, x_thru, land_thru = handle

    def wait_body(x_ref, land_ref, send_sems, recv_sems, after_ref, x_dead, got_ref):
        sends, recvs = _first_copies(x_ref, land_ref, send_sems, recv_sems)
        for cp in sends:
            cp.wait_send()
        for cp in recvs:
            cp.wait_recv()

    x_done, got = pl.pallas_call(
        wait_body, name=name + "_wait",
        out_shape=(pltpu.HBM(x_thru.shape, x_thru.dtype), pltpu.HBM(land_thru.shape, land_thru.dtype)),
        in_specs=(_HBM, _HBM, _SEM, _SEM, _ANY),
        out_specs=(_HBM, _HBM),
        input_output_aliases={0: 0, 1: 1},
        compiler_params=pltpu.CompilerParams(has_side_effects=_EFFECT),
    )(x_thru, land_thru, send_sems, recv_sems, after)

    def pass_body(x_ref, in_ref, out_ref, send_sems, recv_sems, local_sem):
        x, y, c = lax.axis_index("x"), lax.axis_index("y"), lax.axis_index("c")
        chips = [(1 - x, y), (x, 1 - y), (1 - x, 1 - y)]
        mine = pltpu.make_async_copy(x_ref, out_ref.at[4 * x + 2 * y + c], local_sem)
        mine.start()
        sends, recvs = [], []
        for j, (px, py) in enumerate(chips):
            common = dict(send_sem=send_sems.at[j], recv_sem=recv_sems.at[j], device_id=(x, y, 1 - c),
                          device_id_type=pl.DeviceIdType.MESH)
            blk = out_ref.at[4 * px + 2 * py + c]
            sends.append(pltpu.make_async_remote_copy(src_ref=blk, dst_ref=blk, **common))
            got_blk = out_ref.at[4 * px + 2 * py + (1 - c)]
            recvs.append(pltpu.make_async_remote_copy(src_ref=got_blk, dst_ref=got_blk, **common))
        for cp in sends:
            cp.start()
        for cp in recvs:
            cp.wait_recv()
        for cp in sends:
            cp.wait_send()
        mine.wait()

    return pl.pallas_call(
        pass_body, name=name + "_pass",
        out_shape=jax.ShapeDtypeStruct(got.shape, got.dtype),
        in_specs=[_ANY, _ANY],
        out_specs=_ANY,
        input_output_aliases={1: 0},
        scratch_shapes=[pltpu.SemaphoreType.DMA((3,)), pltpu.SemaphoreType.DMA((3,)), pltpu.SemaphoreType.DMA],
    )(x_done, got)


def _slab_copies(g_ref, r_ref, send_sems, recv_sems):
    x, y, c = lax.axis_index("x"), lax.axis_index("y"), lax.axis_index("c")
    me = 4 * x + 2 * y + c
    sends, recvs = [], []
    for k in range(1, N_DEV):
        px, py, pc = _flip(x, (k >> 2) & 1), _flip(y, (k >> 1) & 1), _flip(c, k & 1)
        peer = 4 * px + 2 * py + pc
        common = dict(send_sem=send_sems.at[k - 1], recv_sem=recv_sems.at[k - 1], device_id=(px, py, pc),
                      device_id_type=pl.DeviceIdType.MESH)
        sends.append(pltpu.make_async_remote_copy(src_ref=g_ref.at[peer], dst_ref=r_ref.at[me], **common))
        recvs.append(pltpu.make_async_remote_copy(src_ref=g_ref.at[peer], dst_ref=r_ref.at[peer], **common))
    return sends, recvs


def exchange_start(slabs, *, name):
    land = lax.empty(slabs.shape, slabs.dtype)

    def body(g_ref, r_ref, send_sems, recv_sems, g_thru, r_thru, token):
        sends, _ = _slab_copies(g_ref, r_ref, send_sems, recv_sems)
        for cp in sends:
            cp.start()
        token[...] = jnp.zeros_like(token)

    send_sems, recv_sems, g_thru, r_thru, token = pl.pallas_call(
        body, name=name,
        out_shape=(pltpu.SemaphoreType.DMA((N_DEV - 1,)), pltpu.SemaphoreType.DMA((N_DEV - 1,)),
                   pltpu.HBM(slabs.shape, slabs.dtype), pltpu.HBM(slabs.shape, slabs.dtype),
                   jax.ShapeDtypeStruct((8, LANES), F32)),
        in_specs=(_HBM, _HBM),
        out_specs=(_SEM, _SEM, _HBM, _HBM, pl.BlockSpec(memory_space=pltpu.VMEM)),
        input_output_aliases={0: 2, 1: 3},
        compiler_params=pltpu.CompilerParams(has_side_effects=_EFFECT),
    )(pltpu.with_memory_space_constraint(slabs, pltpu.HBM), pltpu.with_memory_space_constraint(land, pltpu.HBM))
    return (send_sems, recv_sems, g_thru, r_thru), token


def exchange_finish(handle, after, *, name):
    send_sems, recv_sems, g_thru, r_thru = handle

    def body(g_ref, r_ref, send_sems, recv_sems, after_ref, g_out, r_out):
        sends, recvs = _slab_copies(g_ref, r_ref, send_sems, recv_sems)
        for cp in sends:
            cp.wait_send()
        for cp in recvs:
            cp.wait_recv()

    return pl.pallas_call(
        body, name=name,
        out_shape=(pltpu.HBM(g_thru.shape, g_thru.dtype), pltpu.HBM(r_thru.shape, r_thru.dtype)),
        in_specs=(_HBM, _HBM, _SEM, _SEM, _ANY),
        out_specs=(_HBM, _HBM),
        input_output_aliases={0: 0, 1: 1},
        compiler_params=pltpu.CompilerParams(has_side_effects=_EFFECT),
    )(g_thru, r_thru, send_sems, recv_sems, after)


def adamw_own(own, got, me, w, m, v, *, name):
    P, R, C = got.shape
    tr = _tile(R, max(16, (128 * 1024) // C), 16)

    def body(me_ref, own_ref, p_ref, w_ref, m_ref, v_ref, g_ref, d_ref, mo_ref, vo_ref):
        mine = own_ref[...].astype(F32)
        g = None
        for i in range(P):
            term = jnp.where(me_ref[0] == i, mine, p_ref[i].astype(F32))
            g = term if g is None else g + term
        m_new = ADAM_B1 * m_ref[...] + (1.0 - ADAM_B1) * g
        v_new = ADAM_B2 * v_ref[...] + (1.0 - ADAM_B2) * jnp.square(g)
        m_hat = m_new / (1.0 - ADAM_B1 ** ADAM_STEP)
        v_hat = v_new / (1.0 - ADAM_B2 ** ADAM_STEP)
        g_ref[...] = g
        d_ref[...] = -ADAM_LR * (m_hat / (jnp.sqrt(v_hat) + ADAM_EPS) + ADAM_WD * w_ref[...])
        mo_ref[...] = m_new
        vo_ref[...] = v_new

    blk = pl.BlockSpec((tr, C), lambda i, me_ref: (i, 0))
    out = jax.ShapeDtypeStruct((R, C), F32)
    return pl.pallas_call(
        body,
        out_shape=(out, out, out, out),
        grid_spec=pltpu.PrefetchScalarGridSpec(
            num_scalar_prefetch=1,
            grid=(R // tr,),
            in_specs=[pl.BlockSpec((None, tr, C), lambda i, me_ref: (me_ref[0], i, 0)),
                      pl.BlockSpec((P, tr, C), lambda i, me_ref: (0, i, 0)), blk, blk, blk],
            out_specs=(blk, blk, blk, blk),
        ),
        compiler_params=_params("parallel"),
        name=name,
    )(me.reshape(1).astype(jnp.int32), own, got, w, m, v)


def _pad_rows(a, rows):
    return jnp.pad(a, ((0, rows - a.shape[0]), (0, 0)))


def _pad_cols(a, cols):
    return jnp.pad(a, ((0, 0), (0, cols - a.shape[1])))


def kernel(x, meta, a_norm, a_w_in, a_conv, a_w_out, kv_norm, w_kv, k_norm, w_f, b_f, b_norm, b_w_q, b_q_norm, b_w_o, ffn_norm, ffn_w_gu, ffn_w_down, loss_target, m_meta, m_a_norm, m_a_w_in, m_a_conv, m_a_w_out, m_kv_norm, m_w_kv, m_k_norm, m_w_f, m_b_f, m_b_norm, m_b_w_q, m_b_q_norm, m_b_w_o, m_ffn_norm, m_ffn_w_gu, m_ffn_w_down, v_meta, v_a_norm, v_a_w_in, v_a_conv, v_a_w_out, v_kv_norm, v_w_kv, v_k_norm, v_w_f, v_b_f, v_b_norm, v_b_w_q, v_b_q_norm, v_b_w_o, v_ffn_norm, v_ffn_w_gu, v_ffn_w_down):
    S, D = x.shape[1], x.shape[2]
    n_meta = meta.shape[0]
    Ds = meta.shape[1]
    H = D // HEAD_DIM
    n_a, n_b = a_w_in.shape[0], b_w_q.shape[0]
    depth = n_a + n_b
    Fs = ffn_w_down.shape[1]
    pad = BLOCK - n_meta
    lead = pad + n_meta
    T = lead + S
    tk_attn = _tile(T, 384, LANES)
    nk_attn = T // tk_attn
    my = 4 * lax.axis_index("x") + 2 * lax.axis_index("y") + lax.axis_index("c")

    wf_t = w_f.reshape(H, Ds)
    small = jnp.concatenate([meta, _pad_rows(a_norm, 8), _pad_rows(a_conv.reshape(n_a * 3, Ds), 8), wf_t], axis=0)
    r_an, r_ac, r_wf = n_meta, n_meta + 8, n_meta + 16
    gs = all_gather(small, name="ag_small")
    unshard = lambda blk: jnp.transpose(blk, (1, 0, 2)).reshape(blk.shape[1], D)
    meta_full = unshard(gs[:, 0:n_meta])
    a_norm_full = unshard(gs[:, r_an:r_an + n_a])
    a_conv_full = unshard(gs[:, r_ac:r_ac + 3 * n_a]).reshape(n_a, 3, D)
    w_f_full = gs[:, r_wf:r_wf + H].reshape(D, H)
    wf_pad = _pad_cols(w_f_full, LANES).astype(BF)[None]
    bf_pad = _pad_cols(b_f.reshape(1, H), LANES)

    def layer_shards(l):
        if l < n_a:
            mix = [(("in", l), a_w_in[l]), (("out", l), a_w_out[l])]
        else:
            j = l - n_a
            mix = ([(("kv", 0), w_kv)] if j == 0 else []) + [(("q", j), b_w_q[j]), (("o", j), b_w_o[j])]
        return mix + [(("gu", l), ffn_w_gu[l]), (("dn", l), ffn_w_down[l])]

    pending, W = {}, {}

    def start_layer(l, after):
        token = None
        for key, shard in layer_shards(l):
            shard = shard.astype(BF)
            pending[key], token = gather_start(shard, shard if after is None else after,
                                               name=f"ag_{key[0]}{key[1]}")
        return token

    def weight(key, after, shape=None):
        w = gather_finish(pending.pop(key), after, name=f"ag_{key[0]}{key[1]}")
        W[key] = w if shape is None else w.reshape(shape)
        return W[key]

    h = jnp.concatenate([jnp.zeros((pad, D), F32), meta_full, x[0]], axis=0)
    h = behind(h, start_layer(0, None))
    saved = []
    shared = None
    for l in range(depth):
        rec = {"h": h}
        if l < n_a:
            xn = rms_fwd(h, a_norm_full[l], name=f"a{l}_norm")
            proj = mm_nn(xn, weight(("in", l), xn), name=f"a{l}_in")
            y = conv_fwd(proj, a_conv_full[l], name=f"a{l}_conv")
            h1 = mm_nn(y, weight(("out", l), y, (1, D, D)), add=h, name=f"a{l}_out")
            rec.update(xn=xn, proj=proj, y=y)
        else:
            j = l - n_a
            if j == 0:
                xnk = rms_fwd(h, kv_norm, name="kv_norm")
                kv = mm_nn(xnk, weight(("kv", 0), xnk), name="kv_proj")
                k, v = kv_post(kv, k_norm, name="kv_post")
                logits = mm_nn(xnk, wf_pad, name="f_logits", tn_target=LANES)
                cfull = fgate_fwd(logits, bf_pad, pad, name="f_gate")
                c_t = jnp.transpose(cfull[:, :H])
                ccol = c_t.reshape(H, T, 1)
                crow = c_t.reshape(H, nk_attn, 1, tk_attn)
                shared = dict(h=h, xnk=xnk, kv=kv, logits=logits)
            xn = rms_fwd(h, b_norm[j], name=f"b{j}_norm")
            qraw = mm_nn(xn, weight(("q", j), xn, (1, D, D)), name=f"b{j}_q")
            q = hn_fwd(qraw, b_q_norm[j], name=f"b{j}_qnorm")
            o, lse = attn_fwd(q, k, v, ccol, crow, pad, name=f"b{j}_attn")
            h1 = mm_nn(o, weight(("o", j), o, (1, D, D)), add=h, name=f"b{j}_o")
            rec.update(xn=xn, qraw=qraw, q=q, o=o, lse=lse)
        xn2 = rms_fwd(h1, ffn_norm[l], name=f"f{l}_norm")
        act, g_s, u_s = mm_swiglu(xn2, weight(("gu", l), xn2), name=f"f{l}_gu")
        if l + 1 < depth:
            act = behind(act, start_layer(l + 1, act))
        h = mm_nn(act, weight(("dn", l), act, (1, N_DEV * Fs, D)), add=h1, name=f"f{l}_down", tk_target=1408)
        rec.update(h1=h1, xn2=xn2, act=act, g=g_s, u=u_s)
        saved.append(rec)

    dh, loss_tile = loss_head(h, loss_target[0], lead, name="loss")
    loss = lax.psum(loss_tile[0, 0], MESH_AXES)

    upd = {}
    small_g = {}
    inflight = []

    def big(name, l, section, slabs, w, m, v):
        handle, token = exchange_start(slabs.reshape(N_DEV, -1, w.shape[-1]), name=f"{name}{l}_xs")
        inflight.append((section, name, l, handle, w, m, v))
        return token

    def land(sections, after):
        for entry in [e for e in inflight if sections is None or e[0] in sections]:
            inflight.remove(entry)
            _, name, l, handle, w, m, v = entry
            own, got = exchange_finish(handle, after, name=f"{name}{l}_xw")
            shp = w.shape
            flat = lambda t: t.reshape(-1, shp[-1])
            res = adamw_own(own, got, my, flat(w), flat(m), flat(v), name=f"{name}{l}_adamw")
            upd.setdefault(name, {})[l] = [r.reshape(shp) for r in res]

    dk = dv = dck = dcq = None
    for l in reversed(range(depth)):
        rec = saved[l]
        land([("ffn", l + 1)], dh)
        dhb = dh.astype(BF)
        dg, du = mm_nt_dswiglu(dhb, W[("dn", l)], rec["g"], rec["u"], name=f"f{l}_ddown")
        t = big("ffn_w_down", l, ("ffn", l), mm_tn(rec["act"], dhb, 1, name=f"f{l}_wdown"),
                ffn_w_down[l], m_ffn_w_down[l], v_ffn_w_down[l])
        dgu = behind(jnp.concatenate([dg, du], axis=1), t)
        t = big("ffn_w_gu", l, ("ffn", l), mm_tn(rec["xn2"], dgu, N_DEV, name=f"f{l}_wgu"),
                ffn_w_gu[l], m_ffn_w_gu[l], v_ffn_w_gu[l])
        dxn2 = mm_nt(behind(dgu, t), W[("gu", l)], name=f"f{l}_dgu")
        dh1, dgf = rms_bwd(dxn2, rec["h1"], ffn_norm[l], dh, name=f"f{l}_dnorm")
        small_g[("ffn_norm", l)] = dgf
        land([("mix", l + 1)], dh1)
        dhb = dh1.astype(BF)
        if l < n_a:
            dy = mm_nt(dhb, W[("out", l)], name=f"a{l}_dout")
            t = big("a_w_out", l, ("mix", l), mm_tn(rec["y"], dhb, 1, name=f"a{l}_wout"),
                    a_w_out[l], m_a_w_out[l], v_a_w_out[l])
            db, dc, dhh, dcw = conv_bwd(behind(dy, t), rec["proj"], a_conv_full[l], name=f"a{l}_dconv")
            small_g[("a_conv", l)] = dcw
            dproj = jnp.concatenate([db, dc, dhh], axis=1)
            t = big("a_w_in", l, ("mix", l), mm_tn(rec["xn"], dproj, N_DEV, name=f"a{l}_win"),
                    a_w_in[l], m_a_w_in[l], v_a_w_in[l])
            dxn = mm_nt(behind(dproj, t), W[("in", l)], name=f"a{l}_din")
            dh, dga = rms_bwd(dxn, rec["h"], a_norm_full[l], dh1, name=f"a{l}_dnorm")
            small_g[("a_norm", l)] = dga
        else:
            j = l - n_a
            do = mm_nt(dhb, W[("o", j)], name=f"b{j}_do")
            t = big("b_w_o", j, ("mix", l), mm_tn(rec["o"], dhb, 1, name=f"b{j}_wo"),
                    b_w_o[j], m_b_w_o[j], v_b_w_o[j])
            prev = None if dk is None else (dk, dv, dck, dcq)
            dq, dk, dv, dck, dcq = attn_bwd(rec["q"], k, v, behind(do, t), rec["o"], rec["lse"], ccol, crow,
                                            prev, pad, name=f"b{j}_dattn")
            dqraw, dqn = hn_bwd(dq, rec["qraw"], b_q_norm[j], name=f"b{j}_dqnorm")
            small_g[("b_q_norm", j)] = dqn
            t = big("b_w_q", j, ("mix", l), mm_tn(rec["xn"], dqraw, 1, name=f"b{j}_wq"),
                    b_w_q[j], m_b_w_q[j], v_b_w_q[j])
            dxn = mm_nt(behind(dqraw, t), W[("q", j)], name=f"b{j}_dq")
            dh, dgb = rms_bwd(dxn, rec["h"], b_norm[j], dh1, name=f"b{j}_dnorm")
            small_g[("b_norm", j)] = dgb
            if j == 0:
                dkraw, dkn = hn_bwd(dk, shared["kv"], k_norm, name="kv_dknorm")
                dkv = jnp.concatenate([dkraw, dv.astype(BF)], axis=1)
                dc_full = _pad_cols(jnp.transpose(dck.reshape(H, T) + dcq.reshape(H, T)), LANES)
                dz, dbf = fgate_bwd(dc_full, shared["logits"], bf_pad, pad, name="f_dgate")
                t = big("w_kv", 0, ("mix", l), mm_tn(shared["xnk"], dkv, N_DEV, name="kv_wkv"), w_kv, m_w_kv, v_w_kv)
                dwf_t = mm_tn(dz, shared["xnk"], 1, name="f_wf", out_dtype=F32, tn_target=1024)[0, :H]
                dxn_f = mm_nt(dz, wf_pad, name="f_dxn")
                dxnk = mm_nt(behind(dkv, t), W[("kv", 0)], add=dxn_f, name="kv_dxn")
                dh, dgkv = rms_bwd(dxnk, shared["h"], kv_norm, dh, name="kv_dnorm")
    land(None, dh)

    grad_x = dh[lead:][None]

    row8 = lambda a: _pad_rows(_pad_cols(a, D), 8)
    stack = lambda key, n: jnp.concatenate([small_g[(key, i)] for i in range(n)], axis=0)
    g_sharded = jnp.concatenate([dh[pad:lead], row8(stack("a_norm", n_a)), row8(stack("a_conv", n_a)), dwf_t], axis=0)
    g_repl = jnp.concatenate([row8(jnp.concatenate([dgkv, stack("b_norm", n_b)], axis=0)),
                              row8(stack("ffn_norm", depth)),
                              row8(jnp.concatenate([_pad_cols(dkn, D), _pad_cols(stack("b_q_norm", n_b), D),
                                                    _pad_cols(dbf[:, :H], D)], axis=0))], axis=0)
    n_sh = g_sharded.shape[0]
    gathered = all_gather(jnp.concatenate([g_sharded, g_repl], axis=0), name="ag_small_grads")
    parts_sh = lax.dynamic_slice_in_dim(gathered[:, :n_sh], my * Ds, Ds, axis=2)
    parts_rp = gathered[:, n_sh:]

    def pack_sh(t_meta, t_an, t_ac, t_wf):
        return jnp.concatenate([t_meta, _pad_rows(t_an, 8), _pad_rows(t_ac.reshape(n_a * 3, Ds), 8),
                                jnp.transpose(t_wf)], axis=0)

    def pack_rp(t_kv, t_bn, t_fn, t_kn, t_qn, t_bf):
        return jnp.concatenate([row8(jnp.concatenate([t_kv.reshape(1, D), t_bn], axis=0)), row8(t_fn),
                                row8(jnp.concatenate([_pad_cols(t_kn.reshape(1, -1), D), _pad_cols(t_qn, D),
                                                      _pad_cols(t_bf.reshape(1, -1), D)], axis=0))], axis=0)

    res_sh = adamw(parts_sh, pack_sh(meta, a_norm, a_conv, w_f), pack_sh(m_meta, m_a_norm, m_a_conv, m_w_f),
                   pack_sh(v_meta, v_a_norm, v_a_conv, v_w_f), name="small_sharded_adamw")
    res_rp = adamw(parts_rp, pack_rp(kv_norm, b_norm, ffn_norm, k_norm, b_q_norm, b_f),
                   pack_rp(m_kv_norm, m_b_norm, m_ffn_norm, m_k_norm, m_b_q_norm, m_b_f),
                   pack_rp(v_kv_norm, v_b_norm, v_ffn_norm, v_k_norm, v_b_q_norm, v_b_f), name="small_repl_adamw")

    def unpack(kind):
        sh, rp = res_sh[kind], res_rp[kind]
        out = {
            "meta": sh[0:n_meta],
            "a_norm": sh[r_an:r_an + n_a],
            "a_conv": sh[r_ac:r_ac + 3 * n_a].reshape(n_a, 3, Ds),
            "w_f": jnp.transpose(sh[r_wf:r_wf + H]),
            "kv_norm": rp[0],
            "b_norm": rp[1:1 + n_b],
            "ffn_norm": rp[8:8 + depth],
            "k_norm": rp[16, :HEAD_DIM],
            "b_q_norm": rp[17:17 + n_b, :HEAD_DIM],
            "b_f": rp[17 + n_b, :H],
        }
        for name, n in (("a_w_in", n_a), ("a_w_out", n_a), ("b_w_q", n_b), ("b_w_o", n_b),
                        ("ffn_w_gu", depth), ("ffn_w_down", depth)):
            out[name] = jnp.stack([upd[name][i][kind] for i in range(n)], axis=0)
        out["w_kv"] = upd["w_kv"][0][kind]
        return out

    order = ["meta", "a_norm", "a_w_in", "a_conv", "a_w_out", "kv_norm", "w_kv", "k_norm", "w_f", "b_f",
             "b_norm", "b_w_q", "b_q_norm", "b_w_o", "ffn_norm", "ffn_w_gu", "ffn_w_down"]
    outs = [loss, grad_x]
    for kind in range(4):
        vals = unpack(kind)
        outs += [vals[n] for n in order]
    return tuple(outs)
```

```python
import functools
import math

import jax
import jax.numpy as jnp
from jax import lax
from jax.experimental import pallas as pl
from jax.experimental.pallas import tpu as pltpu

N_DEV = 8
MESH_AXES = ("x", "y", "c")
EPS = 1e-6
NEG = -1e30
HEAD_DIM = 128
BLOCK = 128
LANES = 128
V7X_VMEM_LIMIT = 56 * 1024 * 1024

ADAM_LR = 0.001
ADAM_B1 = 0.9
ADAM_B2 = 0.999
ADAM_EPS = 1e-08
ADAM_WD = 0.01
ADAM_STEP = 10

BF = jnp.bfloat16
F32 = jnp.float32


def _tile(n, target, mult):
    best = None
    for t in range(mult, min(n, target) + 1, mult):
        if n % t == 0:
            best = t
    return n if best is None else best


def _params(*sem):
    return pltpu.CompilerParams(dimension_semantics=sem, vmem_limit_bytes=V7X_VMEM_LIMIT)


def _with_dep(body, in_specs, args, dep):
    if dep is None:
        return body, list(in_specs), list(args)
    n_in = len(args)

    def body_dep(*refs):
        body(*refs[:n_in], *refs[n_in + 1:])

    return body_dep, list(in_specs) + [pl.BlockSpec(memory_space=pl.ANY)], list(args) + [dep]


def mm_nn(a, w, *, name, add=None, dep=None, out_dtype=F32, tm_target=1056, tn_target=1024, tk_target=2048):
    M, K = a.shape
    G, K2, n = w.shape
    assert K == K2
    tm = _tile(M, tm_target, 16)
    tn = _tile(n, tn_target, LANES)
    tk = _tile(K, tk_target, LANES)
    nj, nk = n // tn, K // tk
    has_add = add is not None

    def body(*refs):
        if has_add:
            a_ref, w_ref, add_ref, o_ref = refs[:4]
        else:
            a_ref, w_ref, o_ref = refs[:3]
            add_ref = None

        def finish(r):
            if has_add:
                r = r + add_ref[...]
            o_ref[...] = r.astype(out_dtype)

        part = jnp.dot(a_ref[...], w_ref[...], preferred_element_type=F32)
        if nk == 1:
            finish(part)
        else:
            acc_ref = refs[-1]
            k = pl.program_id(2)

            @pl.when(k == 0)
            def _():
                acc_ref[...] = part

            @pl.when(k > 0)
            def _():
                acc_ref[...] += part

            @pl.when(k == nk - 1)
            def _():
                finish(acc_ref[...])

    in_specs = [
        pl.BlockSpec((tm, tk), lambda i, j, k: (i, k)),
        pl.BlockSpec((None, tk, tn), lambda i, j, k: (j // nj, k, j % nj)),
    ]
    args = [a, w]
    if has_add:
        in_specs.append(pl.BlockSpec((tm, tn), lambda i, j, k: (i, j)))
        args.append(add)
    body, in_specs, args = _with_dep(body, in_specs, args, dep)
    return pl.pallas_call(
        body,
        out_shape=jax.ShapeDtypeStruct((M, G * n), out_dtype),
        grid=(M // tm, G * nj, nk),
        in_specs=in_specs,
        out_specs=pl.BlockSpec((tm, tn), lambda i, j, k: (i, j)),
        scratch_shapes=[pltpu.VMEM((tm, tn), F32)] if nk > 1 else [],
        compiler_params=_params("parallel", "parallel", "arbitrary"),
        name=name,
    )(*args)


def mm_swiglu(xn, wgu, *, name, dep=None, save_dtype=BF, tm_target=528, tk_target=512):
    M, K = xn.shape
    G, _, n = wgu.shape
    half = G // 2
    tm = _tile(M, tm_target, 16)
    tn = _tile(n, 1408, LANES)
    tk = _tile(K, tk_target, LANES)
    nj, nk = n // tn, K // tk
    Fh = half * n

    def body(a_ref, wg_ref, wu_ref, act_ref, g_ref, u_ref, accg_ref, accu_ref):
        k = pl.program_id(2)
        a = a_ref[...]
        pg = jnp.dot(a, wg_ref[...], preferred_element_type=F32)
        pu = jnp.dot(a, wu_ref[...], preferred_element_type=F32)

        @pl.when(k == 0)
        def _():
            accg_ref[...] = pg
            accu_ref[...] = pu

        @pl.when(k > 0)
        def _():
            accg_ref[...] += pg
            accu_ref[...] += pu

        @pl.when(k == nk - 1)
        def _():
            g = accg_ref[...]
            u = accu_ref[...]
            act_ref[...] = (g * jax.nn.sigmoid(g) * u).astype(BF)
            g_ref[...] = g.astype(save_dtype)
            u_ref[...] = u.astype(save_dtype)

    out_block = pl.BlockSpec((tm, tn), lambda i, j, k: (i, j))
    body, in_specs, args = _with_dep(body, [
        pl.BlockSpec((tm, tk), lambda i, j, k: (i, k)),
        pl.BlockSpec((None, tk, tn), lambda i, j, k: (j // nj, k, j % nj)),
        pl.BlockSpec((None, tk, tn), lambda i, j, k: (half + j // nj, k, j % nj)),
    ], [xn, wgu, wgu], dep)
    return pl.pallas_call(
        body,
        out_shape=(jax.ShapeDtypeStruct((M, Fh), BF),
                   jax.ShapeDtypeStruct((M, Fh), save_dtype),
                   jax.ShapeDtypeStruct((M, Fh), save_dtype)),
        grid=(M // tm, half * nj, nk),
        in_specs=in_specs,
        out_specs=(out_block, out_block, out_block),
        scratch_shapes=[pltpu.VMEM((tm, tn), F32), pltpu.VMEM((tm, tn), F32)],
        compiler_params=_params("parallel", "parallel", "arbitrary"),
        name=name,
    )(*args)


def mm_nt(dy, w, *, name, add=None, dep=None, out_dtype=F32, tm_target=1056, tko_target=1024, tc_target=1408):
    M, N = dy.shape
    G, K, n = w.shape
    assert N == G * n
    tm = _tile(M, tm_target, 16)
    tko = _tile(K, tko_target, LANES)
    tc = _tile(n, tc_target, LANES)
    nc = n // tc
    steps = G * nc
    has_add = add is not None

    def body(*refs):
        if has_add:
            dy_ref, w_ref, add_ref, o_ref = refs[:4]
        else:
            dy_ref, w_ref, o_ref = refs[:3]
            add_ref = None

        def finish(r):
            if has_add:
                r = r + add_ref[...]
            o_ref[...] = r.astype(out_dtype)

        part = lax.dot_general(dy_ref[...], w_ref[...], (((1,), (1,)), ((), ())),
                               preferred_element_type=F32)
        if steps == 1:
            finish(part)
        else:
            acc_ref = refs[-1]
            s = pl.program_id(2)

            @pl.when(s == 0)
            def _():
                acc_ref[...] = part

            @pl.when(s > 0)
            def _():
                acc_ref[...] += part

            @pl.when(s == steps - 1)
            def _():
                finish(acc_ref[...])

    in_specs = [
        pl.BlockSpec((tm, tc), lambda i, o, s: (i, s)),
        pl.BlockSpec((None, tko, tc), lambda i, o, s: (s // nc, o, s % nc)),
    ]
    args = [dy, w]
    if has_add:
        in_specs.append(pl.BlockSpec((tm, tko), lambda i, o, s: (i, o)))
        args.append(add)
    body, in_specs, args = _with_dep(body, in_specs, args, dep)
    return pl.pallas_call(
        body,
        out_shape=jax.ShapeDtypeStruct((M, K), out_dtype),
        grid=(M // tm, K // tko, steps),
        in_specs=in_specs,
        out_specs=pl.BlockSpec((tm, tko), lambda i, o, s: (i, o)),
        scratch_shapes=[pltpu.VMEM((tm, tko), F32)] if steps > 1 else [],
        compiler_params=_params("parallel", "parallel", "arbitrary"),
        name=name,
    )(*args)


def mm_nt_dswiglu(dh, w_down, g_s, u_s, *, name, tm_target=1056, tf_target=512):
    M, D = dh.shape
    _, Fh, D2 = w_down.shape
    assert D == D2
    tm = _tile(M, tm_target, 16)
    tf = _tile(Fh, tf_target, LANES)

    def body(dh_ref, w_ref, g_ref, u_ref, dg_ref, du_ref):
        dact = lax.dot_general(dh_ref[...], w_ref[...], (((1,), (1,)), ((), ())),
                               preferred_element_type=F32)
        g = g_ref[...].astype(F32)
        u = u_ref[...].astype(F32)
        sig = jax.nn.sigmoid(g)
        du_ref[...] = (dact * (g * sig)).astype(BF)
        dg_ref[...] = (dact * u * (sig * (1.0 + g * (1.0 - sig)))).astype(BF)

    blk = pl.BlockSpec((tm, tf), lambda i, f: (i, f))
    return pl.pallas_call(
        body,
        out_shape=(jax.ShapeDtypeStruct((M, Fh), BF), jax.ShapeDtypeStruct((M, Fh), BF)),
        grid=(M // tm, Fh // tf),
        in_specs=[
            pl.BlockSpec((tm, D), lambda i, f: (i, 0)),
            pl.BlockSpec((None, tf, D), lambda i, f: (0, f, 0)),
            blk, blk,
        ],
        out_specs=(blk, blk),
        compiler_params=_params("parallel", "parallel"),
        name=name,
    )(dh, w_down, g_s, u_s)


def mm_tn(a, dy, groups, *, name, dep=None, out_dtype=BF, tk_target=512, tn_target=1408):
    M, K = a.shape
    M2, N = dy.shape
    assert M == M2 and N % groups == 0
    n = N // groups
    tk = _tile(K, tk_target, LANES)
    tn = _tile(n, tn_target, LANES)
    nj = n // tn

    def body(a_ref, dy_ref, o_ref):
        o_ref[...] = lax.dot_general(a_ref[...], dy_ref[...], (((0,), (0,)), ((), ())),
                                     preferred_element_type=F32).astype(out_dtype)

    body, in_specs, args = _with_dep(body, [
        pl.BlockSpec((M, tk), lambda i, j: (0, i)),
        pl.BlockSpec((M, tn), lambda i, j: (0, j)),
    ], [a, dy], dep)
    return pl.pallas_call(
        body,
        out_shape=jax.ShapeDtypeStruct((groups, K, n), out_dtype),
        grid=(K // tk, groups * nj),
        in_specs=in_specs,
        out_specs=pl.BlockSpec((None, tk, tn), lambda i, j: (j // nj, i, j % nj)),
        compiler_params=_params("parallel", "parallel"),
        name=name,
    )(*args)


def rms_fwd(h, g, *, name, dep=None):
    T, D = h.shape
    tm = _tile(T, 528, 16)

    def body(h_ref, g_ref, o_ref):
        x = h_ref[...]
        r = lax.rsqrt(jnp.mean(x * x, axis=-1, keepdims=True) + EPS)
        o_ref[...] = ((x * r) * g_ref[...]).astype(BF)

    body, in_specs, args = _with_dep(
        body, [pl.BlockSpec((tm, D), lambda i: (i, 0)), pl.BlockSpec((1, D), lambda i: (0, 0))],
        [h, g.reshape(1, D)], dep)
    return pl.pallas_call(
        body,
        out_shape=jax.ShapeDtypeStruct((T, D), BF),
        grid=(T // tm,),
        in_specs=in_specs,
        out_specs=pl.BlockSpec((tm, D), lambda i: (i, 0)),
        compiler_params=_params("parallel"),
        name=name,
    )(*args)


def rms_bwd(dxn, h, g, add, *, name):
    T, D = h.shape
    tm = _tile(T, 264, 8)

    def body(dxn_ref, h_ref, g_ref, add_ref, dh_ref, dg_ref):
        x = h_ref[...]
        dy = dxn_ref[...]
        r = lax.rsqrt(jnp.mean(x * x, axis=-1, keepdims=True) + EPS)
        xhat = x * r
        part = jnp.sum(dy * xhat, axis=0, keepdims=True)

        @pl.when(pl.program_id(0) == 0)
        def _():
            dg_ref[...] = part

        @pl.when(pl.program_id(0) > 0)
        def _():
            dg_ref[...] += part

        dxh = dy * g_ref[...]
        dh_ref[...] = add_ref[...] + r * (dxh - xhat * jnp.mean(dxh * xhat, axis=-1, keepdims=True))

    row = pl.BlockSpec((tm, D), lambda i: (i, 0))
    vec = pl.BlockSpec((1, D), lambda i: (0, 0))
    return pl.pallas_call(
        body,
        out_shape=(jax.ShapeDtypeStruct((T, D), F32), jax.ShapeDtypeStruct((1, D), F32)),
        grid=(T // tm,),
        in_specs=[row, row, vec, row],
        out_specs=(row, vec),
        compiler_params=_params("arbitrary"),
        name=name,
    )(dxn, h, g.reshape(1, D), add)


def _head_norm(x, gain):
    r = lax.rsqrt(jnp.mean(x * x, axis=-1, keepdims=True) + EPS)
    return (x * r) * gain


def hn_fwd(qraw, gain, *, name):
    T, D = qraw.shape
    H = D // HEAD_DIM
    tm = _tile(T, 528, 16)

    def body(q_ref, g_ref, o_ref):
        gain_v = g_ref[...]
        for hd in range(H):
            sl = slice(hd * HEAD_DIM, (hd + 1) * HEAD_DIM)
            o_ref[:, sl] = _head_norm(q_ref[:, sl], gain_v).astype(BF)

    return pl.pallas_call(
        body,
        out_shape=jax.ShapeDtypeStruct((T, D), BF),
        grid=(T // tm,),
        in_specs=[pl.BlockSpec((tm, D), lambda i: (i, 0)),
                  pl.BlockSpec((1, HEAD_DIM), lambda i: (0, 0))],
        out_specs=pl.BlockSpec((tm, D), lambda i: (i, 0)),
        compiler_params=_params("parallel"),
        name=name,
    )(qraw, gain.reshape(1, HEAD_DIM))


def kv_post(kv, gain, *, name):
    T, D2 = kv.shape
    D = D2 // 2
    H = D // HEAD_DIM
    tm = _tile(T, 528, 16)

    def body(k_ref, v_ref, g_ref, ko_ref, vo_ref):
        gain_v = g_ref[...]
        for hd in range(H):
            sl = slice(hd * HEAD_DIM, (hd + 1) * HEAD_DIM)
            ko_ref[:, sl] = _head_norm(k_ref[:, sl], gain_v).astype(BF)
        vo_ref[...] = v_ref[...].astype(BF)

    blk = pl.BlockSpec((tm, D), lambda i: (i, 0))
    return pl.pallas_call(
        body,
        out_shape=(jax.ShapeDtypeStruct((T, D), BF), jax.ShapeDtypeStruct((T, D), BF)),
        grid=(T // tm,),
        in_specs=[blk, pl.BlockSpec((tm, D), lambda i: (i, 1)),
                  pl.BlockSpec((1, HEAD_DIM), lambda i: (0, 0))],
        out_specs=(blk, blk),
        compiler_params=_params("parallel"),
        name=name,
    )(kv, kv, gain.reshape(1, HEAD_DIM))


def hn_bwd(dq, qraw, gain, *, name):
    T, D = dq.shape
    H = D // HEAD_DIM
    tm = _tile(T, 264, 16)

    def body(dq_ref, q_ref, g_ref, o_ref, dg_ref):
        gain_v = g_ref[...]
        part = jnp.zeros((1, HEAD_DIM), F32)
        for hd in range(H):
            sl = slice(hd * HEAD_DIM, (hd + 1) * HEAD_DIM)
            x = q_ref[:, sl]
            dy = dq_ref[:, sl]
            r = lax.rsqrt(jnp.mean(x * x, axis=-1, keepdims=True) + EPS)
            xhat = x * r
            part = part + jnp.sum(dy * xhat, axis=0, keepdims=True)
            dxh = dy * gain_v
            o_ref[:, sl] = (r * (dxh - xhat * jnp.mean(dxh * xhat, axis=-1, keepdims=True))).astype(BF)

        @pl.when(pl.program_id(0) == 0)
        def _():
            dg_ref[...] = part

        @pl.when(pl.program_id(0) > 0)
        def _():
            dg_ref[...] += part

    blk = pl.BlockSpec((tm, D), lambda i: (i, 0))
    vec = pl.BlockSpec((1, HEAD_DIM), lambda i: (0, 0))
    return pl.pallas_call(
        body,
        out_shape=(jax.ShapeDtypeStruct((T, D), BF), jax.ShapeDtypeStruct((1, HEAD_DIM), F32)),
        grid=(T // tm,),
        in_specs=[blk, blk, vec],
        out_specs=(blk, vec),
        compiler_params=_params("arbitrary"),
        name=name,
    )(dq, qraw, gain.reshape(1, HEAD_DIM))


def _shift_down(cur, above, k, rowc):
    out = pltpu.roll(cur, k, 0)
    for i in range(k):
        out = jnp.where(rowc == i, above[8 - k + i:8 - k + i + 1], out)
    return out


def _shift_up(cur, below, k, rowc):
    R = cur.shape[0]
    out = pltpu.roll(cur, R - k, 0)
    for i in range(k):
        out = jnp.where(rowc == R - k + i, below[i:i + 1], out)
    return out


def _conv3(u, u_above, wv, rowc):
    u1 = _shift_down(u, u_above, 1, rowc)
    u2 = _shift_down(u, u_above, 2, rowc)
    return wv[0:1] * u2 + wv[1:2] * u1 + wv[2:3] * u, u1, u2


def conv_fwd(proj, w, *, name):
    T, D3 = proj.shape
    D = D3 // 3
    tc = LANES if D % LANES == 0 else D
    nb = D // tc
    R = _tile(T, 264, 8)

    def body(b_ref, c_ref, h_ref, w_ref, y_ref):
        rowc = lax.broadcasted_iota(jnp.int32, (R, 1), 0)
        wv = w_ref[...]
        for r0 in range(0, T, R):
            rows = slice(r0, r0 + R)
            u = c_ref[rows, :] * h_ref[rows, :]
            if r0 == 0:
                above = jnp.zeros((8, tc), F32)
            else:
                above = c_ref[r0 - 8:r0, :] * h_ref[r0 - 8:r0, :]
            conv, _, _ = _conv3(u, above, wv, rowc)
            y_ref[rows, :] = (b_ref[rows, :] * conv).astype(BF)

    return pl.pallas_call(
        body,
        out_shape=jax.ShapeDtypeStruct((T, D), BF),
        grid=(nb,),
        in_specs=[
            pl.BlockSpec((T, tc), lambda j: (0, j)),
            pl.BlockSpec((T, tc), lambda j: (0, nb + j)),
            pl.BlockSpec((T, tc), lambda j: (0, 2 * nb + j)),
            pl.BlockSpec((3, tc), lambda j: (0, j)),
        ],
        out_specs=pl.BlockSpec((T, tc), lambda j: (0, j)),
        compiler_params=_params("parallel"),
        name=name,
    )(proj, proj, proj, w)


def conv_bwd(dy, proj, w, *, name, dep=None):
    T, D = dy.shape
    tc = LANES if D % LANES == 0 else D
    nb = D // tc
    R = _tile(T, 264, 8)

    def body(dy_ref, b_ref, c_ref, h_ref, w_ref, db_ref, dc_ref, dh_ref, dw_ref):
        rowc = lax.broadcasted_iota(jnp.int32, (R, 1), 0)
        wv = w_ref[...]
        dw = [jnp.zeros((1, tc), F32) for _ in range(3)]
        for r0 in range(0, T, R):
            rows = slice(r0, r0 + R)
            c = c_ref[rows, :]
            hh = h_ref[rows, :]
            u = c * hh
            if r0 == 0:
                above = jnp.zeros((8, tc), F32)
            else:
                above = c_ref[r0 - 8:r0, :] * h_ref[r0 - 8:r0, :]
            conv, u1, u2 = _conv3(u, above, wv, rowc)
            dyv = dy_ref[rows, :]
            db_ref[rows, :] = (dyv * conv).astype(BF)
            dconv = dyv * b_ref[rows, :]
            if r0 + R == T:
                below = jnp.zeros((8, tc), F32)
            else:
                below = dy_ref[r0 + R:r0 + R + 8, :] * b_ref[r0 + R:r0 + R + 8, :]
            dw[0] = dw[0] + jnp.sum(dconv * u2, axis=0, keepdims=True)
            dw[1] = dw[1] + jnp.sum(dconv * u1, axis=0, keepdims=True)
            dw[2] = dw[2] + jnp.sum(dconv * u, axis=0, keepdims=True)
            du = (wv[2:3] * dconv + wv[1:2] * _shift_up(dconv, below, 1, rowc)
                  + wv[0:1] * _shift_up(dconv, below, 2, rowc))
            dc_ref[rows, :] = (du * hh).astype(BF)
            dh_ref[rows, :] = (du * c).astype(BF)
        for i in range(3):
            dw_ref[i:i + 1, :] = dw[i]

    strip = pl.BlockSpec((T, tc), lambda j: (0, j))
    wblk = pl.BlockSpec((3, tc), lambda j: (0, j))
    out = jax.ShapeDtypeStruct((T, D), BF)
    body, in_specs, args = _with_dep(body, [
        strip,
        pl.BlockSpec((T, tc), lambda j: (0, j)),
        pl.BlockSpec((T, tc), lambda j: (0, nb + j)),
        pl.BlockSpec((T, tc), lambda j: (0, 2 * nb + j)),
        wblk,
    ], [dy, proj, proj, proj, w], dep)
    return pl.pallas_call(
        body,
        out_shape=(out, out, out, jax.ShapeDtypeStruct((3, D), F32)),
        grid=(nb,),
        in_specs=in_specs,
        out_specs=(strip, strip, strip, wblk),
        compiler_params=_params("parallel"),
        name=name,
    )(*args)


def _log_sigmoid(z):
    return jnp.minimum(z, 0.0) - jnp.log(1.0 + jnp.exp(-jnp.abs(z)))


def fgate_fwd(logits, bias, pad, *, name):
    T, W = logits.shape
    cb = _tile(T, 128, 8)
    nblk = T // cb

    def body(z_ref, b_ref, c_ref, lf_ref):
        row = lax.broadcasted_iota(jnp.int32, (T, 1), 0)
        lf_ref[...] = jnp.where(row >= pad, _log_sigmoid(z_ref[...] + b_ref[...]), 0.0)
        ri = lax.broadcasted_iota(jnp.int32, (cb, cb), 0)
        ci = lax.broadcasted_iota(jnp.int32, (cb, cb), 1)
        tri = (ci <= ri).astype(F32)

        def step(i, carry):
            rows = pl.ds(pl.multiple_of(i * cb, cb), cb)
            blk = lf_ref[rows, :]
            c_ref[rows, :] = carry + jnp.dot(tri, blk, precision=lax.Precision.HIGHEST,
                                             preferred_element_type=F32)
            return carry + jnp.sum(blk, axis=0, keepdims=True)

        lax.fori_loop(0, nblk, step, jnp.zeros((1, W), F32))

    return pl.pallas_call(
        body,
        out_shape=jax.ShapeDtypeStruct((T, W), F32),
        in_specs=[pl.BlockSpec(memory_space=pltpu.VMEM), pl.BlockSpec(memory_space=pltpu.VMEM)],
        out_specs=pl.BlockSpec(memory_space=pltpu.VMEM),
        scratch_shapes=[pltpu.VMEM((T, W), F32)],
        compiler_params=pltpu.CompilerParams(vmem_limit_bytes=V7X_VMEM_LIMIT),
        name=name,
    )(logits, bias)


def fgate_bwd(dc, logits, bias, pad, *, name):
    T, W = logits.shape
    cb = _tile(T, 128, 8)
    nblk = T // cb

    def body(dc_ref, z_ref, b_ref, dz_ref, db_ref, rs_ref):
        ri = lax.broadcasted_iota(jnp.int32, (cb, cb), 0)
        ci = lax.broadcasted_iota(jnp.int32, (cb, cb), 1)
        triu = (ci >= ri).astype(F32)

        def step(i, carry):
            rows = pl.ds(pl.multiple_of((nblk - 1 - i) * cb, cb), cb)
            blk = dc_ref[rows, :]
            rs_ref[rows, :] = carry + jnp.dot(triu, blk, precision=lax.Precision.HIGHEST,
                                              preferred_element_type=F32)
            return carry + jnp.sum(blk, axis=0, keepdims=True)

        lax.fori_loop(0, nblk, step, jnp.zeros((1, W), F32))
        row = lax.broadcasted_iota(jnp.int32, (T, 1), 0)
        z = z_ref[...] + b_ref[...]
        dz = jnp.where(row >= pad, rs_ref[...] * jax.nn.sigmoid(-z), 0.0)
        dz_ref[...] = dz.astype(BF)
        db_ref[...] = jnp.sum(dz, axis=0, keepdims=True)

    vm = pl.BlockSpec(memory_space=pltpu.VMEM)
    return pl.pallas_call(
        body,
        out_shape=(jax.ShapeDtypeStruct((T, W), BF), jax.ShapeDtypeStruct((1, W), F32)),
        in_specs=[vm, vm, vm],
        out_specs=(vm, vm),
        scratch_shapes=[pltpu.VMEM((T, W), F32)],
        compiler_params=pltpu.CompilerParams(vmem_limit_bytes=V7X_VMEM_LIMIT),
        name=name,
    )(dc, logits, bias)


def _scores(qb, kb, cq, ck, row, col, pad, scale):
    s = lax.dot_general(qb, kb, (((1,), (1,)), ((), ())), preferred_element_type=F32) * scale
    s = s + (cq - ck)
    return jnp.where((col <= row) & (col >= pad), s, NEG)


def attn_fwd(q, k, v, ccol, crow, pad, *, name):
    T, D = q.shape
    H = D // HEAD_DIM
    nk, tk = crow.shape[1], crow.shape[3]
    tq = tk
    nq = T // tq
    scale = 1.0 / math.sqrt(HEAD_DIM)

    def body(q_ref, k_ref, v_ref, cc_ref, cr_ref, o_ref, lse_ref):
        qi = pl.program_id(1)
        qb = q_ref[...]
        cq = cc_ref[...]
        row = qi * tq + lax.broadcasted_iota(jnp.int32, (tq, 1), 0)

        def step(kc, carry):
            m, l, acc = carry
            rows = pl.ds(pl.multiple_of(kc * tk, tk), tk)
            col = kc * tk + lax.broadcasted_iota(jnp.int32, (1, tk), 1)
            s = _scores(qb, k_ref[rows, :], cq, cr_ref[kc], row, col, pad, scale)
            m_new = jnp.maximum(m, jnp.max(s, axis=-1, keepdims=True))
            alpha = jnp.exp(m - m_new)
            p = jnp.exp(s - m_new)
            l = alpha * l + jnp.sum(p, axis=-1, keepdims=True)
            acc = alpha * acc + jnp.dot(p.astype(BF), v_ref[rows, :], preferred_element_type=F32)
            return m_new, l, acc

        init = (jnp.full((tq, 1), NEG, F32), jnp.zeros((tq, 1), F32), jnp.zeros((tq, HEAD_DIM), F32))
        m, l, acc = lax.fori_loop(0, qi + 1, step, init)
        valid = row >= pad
        o_ref[...] = jnp.where(valid, acc / l, 0.0).astype(BF)
        lse_ref[...] = jnp.where(valid, m + jnp.log(l), 0.0)

    return pl.pallas_call(
        body,
        out_shape=(jax.ShapeDtypeStruct((T, D), BF), jax.ShapeDtypeStruct((H, T, 1), F32)),
        grid=(H, nq),
        in_specs=[
            pl.BlockSpec((tq, HEAD_DIM), lambda h, i: (i, h)),
            pl.BlockSpec((T, HEAD_DIM), lambda h, i: (0, h)),
            pl.BlockSpec((T, HEAD_DIM), lambda h, i: (0, h)),
            pl.BlockSpec((None, tq, 1), lambda h, i: (h, i, 0)),
            pl.BlockSpec((None, nk, 1, tk), lambda h, i: (h, 0, 0, 0)),
        ],
        out_specs=(pl.BlockSpec((tq, HEAD_DIM), lambda h, i: (i, h)),
                   pl.BlockSpec((None, tq, 1), lambda h, i: (h, i, 0))),
        compiler_params=_params("parallel", "arbitrary"),
        name=name,
    )(q, k, v, ccol, crow)


def attn_bwd(q, k, v, do, o, lse, ccol, crow, prev, pad, *, name, dep=None):
    T, D = q.shape
    H = D // HEAD_DIM
    nk, tk = crow.shape[1], crow.shape[3]
    tq = tk
    nq = T // tq
    scale = 1.0 / math.sqrt(HEAD_DIM)
    has_prev = prev is not None

    def body(*refs):
        q_ref, k_ref, v_ref, do_ref, o_ref, lse_ref, cc_ref, cr_ref = refs[:8]
        refs = refs[8:]
        if has_prev:
            pk_ref, pv_ref, pc_ref, pq_ref = refs[:4]
            refs = refs[4:]
        dq_ref, dk_ref, dv_ref, dck_ref, dcq_ref, delta_ref = refs
        kc = pl.program_id(1)

        @pl.when(kc == 0)
        def _():
            dq_ref[...] = jnp.zeros_like(dq_ref)
            dcq_ref[...] = pq_ref[...] if has_prev else jnp.zeros_like(dcq_ref)
            do_used = do_ref[...].astype(BF).astype(F32)
            delta_ref[...] = jnp.sum(do_used * o_ref[...].astype(F32), axis=-1, keepdims=True)

        kb = k_ref[...]
        vb = v_ref[...]
        ck = cr_ref[...]
        col = kc * tk + lax.broadcasted_iota(jnp.int32, (1, tk), 1)

        def step(qi, carry):
            dk, dv, dck = carry
            rows = pl.ds(pl.multiple_of(qi * tq, tq), tq)
            row = qi * tq + lax.broadcasted_iota(jnp.int32, (tq, 1), 0)
            qb = q_ref[rows, :]
            dob = do_ref[rows, :].astype(BF)
            s = _scores(qb, kb, cc_ref[rows, :], ck, row, col, pad, scale)
            p = jnp.exp(s - lse_ref[rows, :])
            dp = lax.dot_general(dob, vb, (((1,), (1,)), ((), ())), preferred_element_type=F32)
            ds = p * (dp - delta_ref[rows, :])
            dsb = ds.astype(BF)
            dv = dv + lax.dot_general(p.astype(BF), dob, (((0,), (0,)), ((), ())),
                                      preferred_element_type=F32)
            dk = dk + lax.dot_general(dsb, qb, (((0,), (0,)), ((), ())), preferred_element_type=F32)
            dq_ref[rows, :] += jnp.dot(dsb, kb, preferred_element_type=F32) * scale
            dcq_ref[rows, :] += jnp.sum(ds, axis=1, keepdims=True)
            dck = dck - jnp.sum(ds, axis=0, keepdims=True)
            return dk, dv, dck

        init = (jnp.zeros((tk, HEAD_DIM), F32), jnp.zeros((tk, HEAD_DIM), F32), jnp.zeros((1, tk), F32))
        dk, dv, dck = lax.fori_loop(kc, nq, step, init)
        dk = dk * scale
        if has_prev:
            dk = dk + pk_ref[...]
            dv = dv + pv_ref[...]
            dck = dck + pc_ref[...]
        dk_ref[...] = dk
        dv_ref[...] = dv
        dck_ref[...] = dck

    head_all = pl.BlockSpec((T, HEAD_DIM), lambda h, j: (0, h))
    head_blk = pl.BlockSpec((tk, HEAD_DIM), lambda h, j: (j, h))
    col_all = pl.BlockSpec((None, T, 1), lambda h, j: (h, 0, 0))
    row_blk = pl.BlockSpec((None, None, 1, tk), lambda h, j: (h, j, 0, 0))
    in_specs = [head_all, head_blk, head_blk, head_all, head_all, col_all, col_all, row_blk]
    args = [q, k, v, do, o, lse, ccol, crow]
    if has_prev:
        in_specs += [head_blk, head_blk, row_blk, col_all]
        args += list(prev)
    body, in_specs, args = _with_dep(body, in_specs, args, dep)
    return pl.pallas_call(
        body,
        out_shape=(jax.ShapeDtypeStruct((T, D), F32), jax.ShapeDtypeStruct((T, D), F32),
                   jax.ShapeDtypeStruct((T, D), F32), jax.ShapeDtypeStruct((H, nk, 1, tk), F32),
                   jax.ShapeDtypeStruct((H, T, 1), F32)),
        grid=(H, nk),
        in_specs=in_specs,
        out_specs=(head_all, head_blk, head_blk, row_blk, col_all),
        scratch_shapes=[pltpu.VMEM((T, 1), F32)],
        compiler_params=_params("parallel", "arbitrary"),
        name=name,
    )(*args)


def loss_head(h, target, lead, *, name):
    T, D = h.shape
    tm = lead
    assert T % tm == 0 and target.shape[0] % tm == 0
    inv_d = 1.0 / D

    def body(h_ref, t_ref, dh_ref, loss_ref):
        i = pl.program_id(0)

        @pl.when(i == 0)
        def _():
            dh_ref[...] = jnp.zeros_like(dh_ref)
            loss_ref[...] = jnp.zeros_like(loss_ref)

        @pl.when(i > 0)
        def _():
            e = h_ref[...] - t_ref[...]
            dh_ref[...] = e * inv_d
            loss_ref[...] += 0.5 * inv_d * jnp.sum(e * e)

    return pl.pallas_call(
        body,
        out_shape=(jax.ShapeDtypeStruct((T, D), F32), jax.ShapeDtypeStruct((8, LANES), F32)),
        grid=(T // tm,),
        in_specs=[pl.BlockSpec((tm, D), lambda i: (i, 0)),
                  pl.BlockSpec((tm, D), lambda i: (jnp.maximum(i - 1, 0), 0))],
        out_specs=(pl.BlockSpec((tm, D), lambda i: (i, 0)),
                   pl.BlockSpec((8, LANES), lambda i: (0, 0))),
        compiler_params=_params("arbitrary"),
        name=name,
    )(h, target)


def adamw(parts, w, m, v, *, name):
    P, R, C = parts.shape
    tr = _tile(R, max(16, (128 * 1024) // C), 16)

    def body(p_ref, w_ref, m_ref, v_ref, g_ref, d_ref, mo_ref, vo_ref):
        g = p_ref[0].astype(F32)
        for i in range(1, P):
            g = g + p_ref[i].astype(F32)
        m_new = ADAM_B1 * m_ref[...] + (1.0 - ADAM_B1) * g
        v_new = ADAM_B2 * v_ref[...] + (1.0 - ADAM_B2) * jnp.square(g)
        m_hat = m_new / (1.0 - ADAM_B1 ** ADAM_STEP)
        v_hat = v_new / (1.0 - ADAM_B2 ** ADAM_STEP)
        g_ref[...] = g
        d_ref[...] = -ADAM_LR * (m_hat / (jnp.sqrt(v_hat) + ADAM_EPS) + ADAM_WD * w_ref[...])
        mo_ref[...] = m_new
        vo_ref[...] = v_new

    blk = pl.BlockSpec((tr, C), lambda i: (i, 0))
    out = jax.ShapeDtypeStruct((R, C), F32)
    return pl.pallas_call(
        body,
        out_shape=(out, out, out, out),
        grid=(R // tr,),
        in_specs=[pl.BlockSpec((P, tr, C), lambda i: (0, i, 0)), blk, blk, blk],
        out_specs=(blk, blk, blk, blk),
        compiler_params=_params("parallel"),
        name=name,
    )(parts, w, m, v)


def _flip(v, bit):
    return 1 - v if bit else v


def all_gather(shard, *, name):
    def body(x_ref, out_ref, send_sems, recv_sems, local_sem):
        x, y, c = lax.axis_index("x"), lax.axis_index("y"), lax.axis_index("c")
        me, sibling = (x, y, c), (x, y, 1 - c)
        chips = [(1 - x, y), (x, 1 - y), (1 - x, 1 - y)]

        def block(px, py, pc):
            return out_ref.at[4 * px + 2 * py + pc]

        def copy(k, blk, to, src=None):
            return pltpu.make_async_remote_copy(
                src_ref=block(*blk) if src is None else src,
                dst_ref=block(*blk),
                send_sem=send_sems.at[k],
                recv_sem=recv_sems.at[k],
                device_id=to,
                device_id_type=pl.DeviceIdType.MESH,
            )

        mine = pltpu.make_async_copy(x_ref, block(*me), local_sem)
        mine.start()
        first = [copy(0, me, sibling, src=x_ref)]
        first += [copy(1 + j, me, (*chip, c), src=x_ref) for j, chip in enumerate(chips)]
        for cp in first:
            cp.start()
        passed = [copy(4 + j, (*chip, c), sibling) for j, chip in enumerate(chips)]
        for j, chip in enumerate(chips):
            copy(1 + j, (*chip, c), me).wait_recv()
            passed[j].start()
        copy(0, sibling, me).wait_recv()
        for j, chip in enumerate(chips):
            copy(4 + j, (*chip, 1 - c), me).wait_recv()
        for cp in first + passed:
            cp.wait_send()
        mine.wait()

    return pl.pallas_call(
        body,
        out_shape=jax.ShapeDtypeStruct((N_DEV,) + shard.shape, shard.dtype),
        in_specs=[pl.BlockSpec(memory_space=pl.ANY)],
        out_specs=pl.BlockSpec(memory_space=pl.ANY),
        scratch_shapes=[pltpu.SemaphoreType.DMA((7,)), pltpu.SemaphoreType.DMA((7,)),
                        pltpu.SemaphoreType.DMA],
        name=name,
    )(shard)


def exchange_slabs(slabs, *, name):
    def body(g_ref, r_ref, send_sems, recv_sems, local_sem):
        x, y, c = lax.axis_index("x"), lax.axis_index("y"), lax.axis_index("c")
        me = 4 * x + 2 * y + c
        mine = pltpu.make_async_copy(g_ref.at[me], r_ref.at[me], local_sem)
        mine.start()
        sends, recvs = [], []
        for k in range(1, N_DEV):
            px, py, pc = _flip(x, (k >> 2) & 1), _flip(y, (k >> 1) & 1), _flip(c, k & 1)
            peer = 4 * px + 2 * py + pc
            sends.append(pltpu.make_async_remote_copy(
                src_ref=g_ref.at[peer], dst_ref=r_ref.at[me],
                send_sem=send_sems.at[k - 1], recv_sem=recv_sems.at[k - 1],
                device_id=(px, py, pc), device_id_type=pl.DeviceIdType.MESH))
            recvs.append(pltpu.make_async_remote_copy(
                src_ref=g_ref.at[peer], dst_ref=r_ref.at[peer],
                send_sem=send_sems.at[k - 1], recv_sem=recv_sems.at[k - 1],
                device_id=(px, py, pc), device_id_type=pl.DeviceIdType.MESH))
        for cp in sends:
            cp.start()
        for cp in recvs:
            cp.wait_recv()
        for cp in sends:
            cp.wait_send()
        mine.wait()

    return pl.pallas_call(
        body,
        out_shape=jax.ShapeDtypeStruct(slabs.shape, slabs.dtype),
        in_specs=[pl.BlockSpec(memory_space=pl.ANY)],
        out_specs=pl.BlockSpec(memory_space=pl.ANY),
        scratch_shapes=[pltpu.SemaphoreType.DMA((7,)), pltpu.SemaphoreType.DMA((7,)),
                        pltpu.SemaphoreType.DMA],
        name=name,
    )(slabs)


def reduce_adamw(slabs, w, m, v, *, name):
    got = exchange_slabs(slabs, name=name + "_xchg")
    return adamw(got, w, m, v, name=name + "_adamw")


_HBM = pl.BlockSpec(memory_space=pltpu.HBM)
_SEM = pl.BlockSpec(memory_space=pltpu.SEMAPHORE)
_ANY = pl.BlockSpec(memory_space=pl.ANY)
_EFFECT = pltpu.SideEffectType.DATAFLOW_SIDE_EFFECTING
_N_FIRST = 4


def _first_copies(x_ref, land_ref, send_sems, recv_sems):
    x, y, c = lax.axis_index("x"), lax.axis_index("y"), lax.axis_index("c")
    me = 4 * x + 2 * y + c
    targets = [(x, y, 1 - c), (1 - x, y, c), (x, 1 - y, c), (1 - x, 1 - y, c)]
    sends, recvs = [], []
    for k, (px, py, pc) in enumerate(targets):
        common = dict(send_sem=send_sems.at[k], recv_sem=recv_sems.at[k], device_id=(px, py, pc),
                      device_id_type=pl.DeviceIdType.MESH)
        sends.append(pltpu.make_async_remote_copy(src_ref=x_ref, dst_ref=land_ref.at[me], **common))
        recvs.append(pltpu.make_async_remote_copy(src_ref=x_ref, dst_ref=land_ref.at[4 * px + 2 * py + pc],
                                                  **common))
    return sends, recvs


def _second_copies(land_ref, send_sems, recv_sems):
    x, y, c = lax.axis_index("x"), lax.axis_index("y"), lax.axis_index("c")
    sends, recvs = [], []
    for j, (px, py) in enumerate([(1 - x, y), (x, 1 - y), (1 - x, 1 - y)]):
        common = dict(send_sem=send_sems.at[j], recv_sem=recv_sems.at[j], device_id=(x, y, 1 - c),
                      device_id_type=pl.DeviceIdType.MESH)
        blk = land_ref.at[4 * px + 2 * py + c]
        sends.append(pltpu.make_async_remote_copy(src_ref=blk, dst_ref=blk, **common))
        got = land_ref.at[4 * px + 2 * py + (1 - c)]
        recvs.append(pltpu.make_async_remote_copy(src_ref=got, dst_ref=got, **common))
    return sends, recvs


def gather_start(shard, after, *, name):
    def own_body(x_ref, land_ref, sem):
        me = 4 * lax.axis_index("x") + 2 * lax.axis_index("y") + lax.axis_index("c")
        cp = pltpu.make_async_copy(x_ref, land_ref.at[me], sem)
        cp.start()
        cp.wait()

    land = pl.pallas_call(
        own_body, name=name + "_own",
        out_shape=jax.ShapeDtypeStruct((N_DEV,) + shard.shape, shard.dtype),
        in_specs=[_ANY], out_specs=_ANY,
        scratch_shapes=[pltpu.SemaphoreType.DMA],
    )(shard)

    def body(x_ref, land_ref, after_ref, send_sems, recv_sems, x_thru, land_thru, token):
        sends, _ = _first_copies(x_ref, land_ref, send_sems, recv_sems)
        for cp in sends:
            cp.start()
        token[...] = jnp.zeros_like(token)

    send_sems, recv_sems, x_thru, land_thru, token = pl.pallas_call(
        body, name=name + "_s1",
        out_shape=(pltpu.SemaphoreType.DMA((_N_FIRST,)), pltpu.SemaphoreType.DMA((_N_FIRST,)),
                   pltpu.HBM(shard.shape, shard.dtype), pltpu.HBM(land.shape, land.dtype),
                   jax.ShapeDtypeStruct((8, LANES), F32)),
        in_specs=(_HBM, _HBM, _ANY),
        out_specs=(_SEM, _SEM, _HBM, _HBM, pl.BlockSpec(memory_space=pltpu.VMEM)),
        input_output_aliases={0: 2, 1: 3},
        compiler_params=pltpu.CompilerParams(has_side_effects=_EFFECT),
    )(pltpu.with_memory_space_constraint(shard, pltpu.HBM),
      pltpu.with_memory_space_constraint(land, pltpu.HBM), after)
    return (send_sems, recv_sems, x_thru, land_thru), token


def gather_mid(handle, after, *, name):
    send_sems, recv_sems, x_thru, land_thru = handle

    def body(x_ref, land_ref, send1, recv1, after_ref, send2, recv2, x_dead, land_out, token):
        sends, recvs = _first_copies(x_ref, land_ref, send1, recv1)
        for cp in sends:
            cp.wait_send()
        for cp in recvs:
            cp.wait_recv()
        seconds, _ = _second_copies(land_ref, send2, recv2)
        for cp in seconds:
            cp.start()
        token[...] = jnp.zeros_like(token)

    send2, recv2, _, land2, token = pl.pallas_call(
        body, name=name + "_s2",
        out_shape=(pltpu.SemaphoreType.DMA((3,)), pltpu.SemaphoreType.DMA((3,)),
                   pltpu.HBM(x_thru.shape, x_thru.dtype), pltpu.HBM(land_thru.shape, land_thru.dtype),
                   jax.ShapeDtypeStruct((8, LANES), F32)),
        in_specs=(_HBM, _HBM, _SEM, _SEM, _ANY),
        out_specs=(_SEM, _SEM, _HBM, _HBM, pl.BlockSpec(memory_space=pltpu.VMEM)),
        input_output_aliases={0: 2, 1: 3},
        compiler_params=pltpu.CompilerParams(has_side_effects=_EFFECT),
    )(x_thru, land_thru, send_sems, recv_sems, after)
    return (send2, recv2, land2), token


def gather_finish(handle, after, *, name):
    send2, recv2, land2 = handle

    def body(land_ref, send2, recv2, after_ref, got_ref):
        sends, recvs = _second_copies(land_ref, send2, recv2)
        for cp in sends:
            cp.wait_send()
        for cp in recvs:
            cp.wait_recv()

    return pl.pallas_call(
        body, name=name + "_w",
        out_shape=pltpu.HBM(land2.shape, land2.dtype),
        in_specs=(_HBM, _SEM, _SEM, _ANY),
        out_specs=_HBM,
        input_output_aliases={0: 0},
        compiler_params=pltpu.CompilerParams(has_side_effects=_EFFECT),
    )(land2, send2, recv2, after)


def _slab_copies(g_ref, r_ref, send_sems, recv_sems):
    x, y, c = lax.axis_index("x"), lax.axis_index("y"), lax.axis_index("c")
    me = 4 * x + 2 * y + c
    sends, recvs = [], []
    for k in range(1, N_DEV):
        px, py, pc = _flip(x, (k >> 2) & 1), _flip(y, (k >> 1) & 1), _flip(c, k & 1)
        peer = 4 * px + 2 * py + pc
        common = dict(send_sem=send_sems.at[k - 1], recv_sem=recv_sems.at[k - 1], device_id=(px, py, pc),
                      device_id_type=pl.DeviceIdType.MESH)
        sends.append(pltpu.make_async_remote_copy(src_ref=g_ref.at[peer], dst_ref=r_ref.at[me], **common))
        recvs.append(pltpu.make_async_remote_copy(src_ref=g_ref.at[peer], dst_ref=r_ref.at[peer], **common))
    return sends, recvs


def exchange_start(slabs, *, name):
    land = lax.empty(slabs.shape, slabs.dtype)

    def body(g_ref, r_ref, send_sems, recv_sems, g_thru, r_thru, token):
        sends, _ = _slab_copies(g_ref, r_ref, send_sems, recv_sems)
        for cp in sends:
            cp.start()
        token[...] = jnp.zeros_like(token)

    send_sems, recv_sems, g_thru, r_thru, token = pl.pallas_call(
        body, name=name,
        out_shape=(pltpu.SemaphoreType.DMA((N_DEV - 1,)), pltpu.SemaphoreType.DMA((N_DEV - 1,)),
                   pltpu.HBM(slabs.shape, slabs.dtype), pltpu.HBM(slabs.shape, slabs.dtype),
                   jax.ShapeDtypeStruct((8, LANES), F32)),
        in_specs=(_HBM, _HBM),
        out_specs=(_SEM, _SEM, _HBM, _HBM, pl.BlockSpec(memory_space=pltpu.VMEM)),
        input_output_aliases={0: 2, 1: 3},
        compiler_params=pltpu.CompilerParams(has_side_effects=_EFFECT),
    )(pltpu.with_memory_space_constraint(slabs, pltpu.HBM), pltpu.with_memory_space_constraint(land, pltpu.HBM))
    return (send_sems, recv_sems, g_thru, r_thru), token


def exchange_finish(handle, after, *, name):
    send_sems, recv_sems, g_thru, r_thru = handle

    def body(g_ref, r_ref, send_sems, recv_sems, after_ref, g_out, r_out):
        sends, recvs = _slab_copies(g_ref, r_ref, send_sems, recv_sems)
        for cp in sends:
            cp.wait_send()
        for cp in recvs:
            cp.wait_recv()

    return pl.pallas_call(
        body, name=name,
        out_shape=(pltpu.HBM(g_thru.shape, g_thru.dtype), pltpu.HBM(r_thru.shape, r_thru.dtype)),
        in_specs=(_HBM, _HBM, _SEM, _SEM, _ANY),
        out_specs=(_HBM, _HBM),
        input_output_aliases={0: 0, 1: 1},
        compiler_params=pltpu.CompilerParams(has_side_effects=_EFFECT),
    )(g_thru, r_thru, send_sems, recv_sems, after)


def adamw_own(own, got, me, w, m, v, *, name):
    P, R, C = got.shape
    tr = _tile(R, max(16, (128 * 1024) // C), 16)

    def body(me_ref, own_ref, p_ref, w_ref, m_ref, v_ref, g_ref, d_ref, mo_ref, vo_ref):
        mine = own_ref[...].astype(F32)
        g = None
        for i in range(P):
            term = jnp.where(me_ref[0] == i, mine, p_ref[i].astype(F32))
            g = term if g is None else g + term
        m_new = ADAM_B1 * m_ref[...] + (1.0 - ADAM_B1) * g
        v_new = ADAM_B2 * v_ref[...] + (1.0 - ADAM_B2) * jnp.square(g)
        m_hat = m_new / (1.0 - ADAM_B1 ** ADAM_STEP)
        v_hat = v_new / (1.0 - ADAM_B2 ** ADAM_STEP)
        g_ref[...] = g
        d_ref[...] = -ADAM_LR * (m_hat / (jnp.sqrt(v_hat) + ADAM_EPS) + ADAM_WD * w_ref[...])
        mo_ref[...] = m_new
        vo_ref[...] = v_new

    blk = pl.BlockSpec((tr, C), lambda i, me_ref: (i, 0))
    out = jax.ShapeDtypeStruct((R, C), F32)
    return pl.pallas_call(
        body,
        out_shape=(out, out, out, out),
        grid_spec=pltpu.PrefetchScalarGridSpec(
            num_scalar_prefetch=1,
            grid=(R // tr,),
            in_specs=[pl.BlockSpec((None, tr, C), lambda i, me_ref: (me_ref[0], i, 0)),
                      pl.BlockSpec((P, tr, C), lambda i, me_ref: (0, i, 0)), blk, blk, blk],
            out_specs=(blk, blk, blk, blk),
        ),
        compiler_params=_params("parallel"),
        name=name,
    )(me.reshape(1).astype(jnp.int32), own, got, w, m, v)


def _pad_rows(a, rows):
    return jnp.pad(a, ((0, rows - a.shape[0]), (0, 0)))


def _pad_cols(a, cols):
    return jnp.pad(a, ((0, 0), (0, cols - a.shape[1])))


def kernel(x, meta, a_norm, a_w_in, a_conv, a_w_out, kv_norm, w_kv, k_norm, w_f, b_f, b_norm, b_w_q, b_q_norm, b_w_o, ffn_norm, ffn_w_gu, ffn_w_down, loss_target, m_meta, m_a_norm, m_a_w_in, m_a_conv, m_a_w_out, m_kv_norm, m_w_kv, m_k_norm, m_w_f, m_b_f, m_b_norm, m_b_w_q, m_b_q_norm, m_b_w_o, m_ffn_norm, m_ffn_w_gu, m_ffn_w_down, v_meta, v_a_norm, v_a_w_in, v_a_conv, v_a_w_out, v_kv_norm, v_w_kv, v_k_norm, v_w_f, v_b_f, v_b_norm, v_b_w_q, v_b_q_norm, v_b_w_o, v_ffn_norm, v_ffn_w_gu, v_ffn_w_down):
    S, D = x.shape[1], x.shape[2]
    n_meta = meta.shape[0]
    Ds = meta.shape[1]
    H = D // HEAD_DIM
    n_a, n_b = a_w_in.shape[0], b_w_q.shape[0]
    depth = n_a + n_b
    Fs = ffn_w_down.shape[1]
    pad = BLOCK - n_meta
    lead = pad + n_meta
    T = lead + S
    tk_attn = _tile(T, 384, LANES)
    nk_attn = T // tk_attn
    my = 4 * lax.axis_index("x") + 2 * lax.axis_index("y") + lax.axis_index("c")

    wf_t = w_f.reshape(H, Ds)
    small = jnp.concatenate([meta, _pad_rows(a_norm, 8), _pad_rows(a_conv.reshape(n_a * 3, Ds), 8), wf_t], axis=0)
    r_an, r_ac, r_wf = n_meta, n_meta + 8, n_meta + 16
    gs = all_gather(small, name="ag_small")
    unshard = lambda blk: jnp.transpose(blk, (1, 0, 2)).reshape(blk.shape[1], D)
    meta_full = unshard(gs[:, 0:n_meta])
    a_norm_full = unshard(gs[:, r_an:r_an + n_a])
    a_conv_full = unshard(gs[:, r_ac:r_ac + 3 * n_a]).reshape(n_a, 3, D)
    w_f_full = gs[:, r_wf:r_wf + H].reshape(D, H)
    wf_pad = _pad_cols(w_f_full, LANES).astype(BF)[None]
    bf_pad = _pad_cols(b_f.reshape(1, H), LANES)

    def layer_shards(l):
        if l < n_a:
            mix = [(("in", l), a_w_in[l]), (("out", l), a_w_out[l])]
        else:
            j = l - n_a
            mix = ([(("kv", 0), w_kv)] if j == 0 else []) + [(("q", j), b_w_q[j]), (("o", j), b_w_o[j])]
        return mix + [(("gu", l), ffn_w_gu[l]), (("dn", l), ffn_w_down[l])]

    first_level, second_level, W = {}, {}, {}
    st = {"done": None, "tok": None}

    def note(val):
        st["done"] = val
        return val

    def take():
        tok, st["tok"] = st["tok"], None
        return tok

    def chain_after(default):
        if st["tok"] is not None:
            return st["tok"]
        return default if st["done"] is None else st["done"]

    def ag_name(key):
        return f"ag_{key[0]}{key[1]}"

    def start_layer(l):
        for key, shard in layer_shards(l):
            shard = shard.astype(BF)
            first_level[key], st["tok"] = gather_start(shard, chain_after(shard), name=ag_name(key))

    def pass_on(keys):
        for key in keys:
            second_level[key], st["tok"] = gather_mid(first_level.pop(key), chain_after(None), name=ag_name(key))

    def weight(key, shape=None):
        w = gather_finish(second_level.pop(key), st["done"], name=ag_name(key))
        W[key] = w if shape is None else w.reshape(shape)
        return W[key]

    def layer_keys(l):
        keys = [key for key, _ in layer_shards(l)]
        return keys[:-2], keys[-2:]

    h = note(jnp.concatenate([jnp.zeros((pad, D), F32), meta_full, x[0]], axis=0))
    start_layer(0)
    pass_on(layer_keys(0)[0])
    saved = []
    shared = None
    for l in range(depth):
        rec = {"h": h}
        mix_keys, ffn_keys = layer_keys(l)

        def ahead():
            if l >= 1:
                pass_on(ffn_keys)
            if l + 1 < depth:
                start_layer(l + 1)

        if l < n_a:
            xn = note(rms_fwd(h, a_norm_full[l], name=f"a{l}_norm", dep=take()))
            ahead()
            proj = note(mm_nn(xn, weight(("in", l)), name=f"a{l}_in", dep=take()))
            if l == 0:
                pass_on(ffn_keys)
            y = note(conv_fwd(proj, a_conv_full[l], name=f"a{l}_conv"))
            h1 = note(mm_nn(y, weight(("out", l), (1, D, D)), add=h, name=f"a{l}_out", dep=take()))
            rec.update(xn=xn, proj=proj, y=y)
        else:
            j = l - n_a
            if j == 0:
                xnk = note(rms_fwd(h, kv_norm, name="kv_norm", dep=take()))
                ahead()
                kv = note(mm_nn(xnk, weight(("kv", 0)), name="kv_proj", dep=take()))
                k, v = kv_post(kv, k_norm, name="kv_post")
                logits = mm_nn(xnk, wf_pad, name="f_logits", tn_target=LANES)
                cfull = fgate_fwd(logits, bf_pad, pad, name="f_gate")
                c_t = jnp.transpose(cfull[:, :H])
                ccol = c_t.reshape(H, T, 1)
                crow = c_t.reshape(H, nk_attn, 1, tk_attn)
                shared = dict(h=h, xnk=xnk, kv=kv, logits=logits)
                xn = note(rms_fwd(h, b_norm[j], name=f"b{j}_norm"))
            else:
                xn = note(rms_fwd(h, b_norm[j], name=f"b{j}_norm", dep=take()))
                ahead()
            qraw = note(mm_nn(xn, weight(("q", j), (1, D, D)), name=f"b{j}_q", dep=take()))
            q = hn_fwd(qraw, b_q_norm[j], name=f"b{j}_qnorm")
            o, lse = attn_fwd(q, k, v, ccol, crow, pad, name=f"b{j}_attn")
            note(o)
            h1 = note(mm_nn(o, weight(("o", j), (1, D, D)), add=h, name=f"b{j}_o"))
            rec.update(xn=xn, qraw=qraw, q=q, o=o, lse=lse)
        xn2 = note(rms_fwd(h1, ffn_norm[l], name=f"f{l}_norm", dep=take()))
        if l + 1 < depth:
            pass_on(layer_keys(l + 1)[0])
        act, g_s, u_s = mm_swiglu(xn2, weight(("gu", l)), name=f"f{l}_gu", dep=take())
        note(act)
        h = note(mm_nn(act, weight(("dn", l), (1, N_DEV * Fs, D)), add=h1, name=f"f{l}_down", tk_target=1408))
        rec.update(h1=h1, xn2=xn2, act=act, g=g_s, u=u_s)
        saved.append(rec)

    dh, loss_tile = loss_head(h, loss_target[0], lead, name="loss")
    loss = lax.psum(loss_tile[0, 0], MESH_AXES)

    upd = {}
    small_g = {}
    inflight = []

    def big(name, l, section, slabs, w, m, v):
        handle, st["tok"] = exchange_start(slabs.reshape(N_DEV, -1, w.shape[-1]), name=f"{name}{l}_xs")
        inflight.append((section, name, l, handle, w, m, v))

    def land(sections, after):
        for entry in [e for e in inflight if sections is None or e[0] in sections]:
            inflight.remove(entry)
            _, name, l, handle, w, m, v = entry
            own, got = exchange_finish(handle, after, name=f"{name}{l}_xw")
            shp = w.shape
            flat = lambda t: t.reshape(-1, shp[-1])
            res = adamw_own(own, got, my, flat(w), flat(m), flat(v), name=f"{name}{l}_adamw")
            upd.setdefault(name, {})[l] = [r.reshape(shp) for r in res]

    dk = dv = dck = dcq = None
    for l in reversed(range(depth)):
        rec = saved[l]
        land([("ffn", l + 1)], dh)
        dhb = dh.astype(BF)
        dg, du = mm_nt_dswiglu(dhb, W[("dn", l)], rec["g"], rec["u"], name=f"f{l}_ddown")
        big("ffn_w_down", l, ("ffn", l), mm_tn(rec["act"], dhb, 1, name=f"f{l}_wdown"),
            ffn_w_down[l], m_ffn_w_down[l], v_ffn_w_down[l])
        dgu = jnp.concatenate([dg, du], axis=1)
        big("ffn_w_gu", l, ("ffn", l), mm_tn(rec["xn2"], dgu, N_DEV, name=f"f{l}_wgu", dep=take()),
            ffn_w_gu[l], m_ffn_w_gu[l], v_ffn_w_gu[l])
        dxn2 = mm_nt(dgu, W[("gu", l)], name=f"f{l}_dgu", dep=take())
        dh1, dgf = rms_bwd(dxn2, rec["h1"], ffn_norm[l], dh, name=f"f{l}_dnorm")
        small_g[("ffn_norm", l)] = dgf
        land([("mix", l + 1)], dh1)
        dhb = dh1.astype(BF)
        if l < n_a:
            dy = mm_nt(dhb, W[("out", l)], name=f"a{l}_dout")
            big("a_w_out", l, ("mix", l), mm_tn(rec["y"], dhb, 1, name=f"a{l}_wout"),
                a_w_out[l], m_a_w_out[l], v_a_w_out[l])
            db, dc, dhh, dcw = conv_bwd(dy, rec["proj"], a_conv_full[l], name=f"a{l}_dconv", dep=take())
            small_g[("a_conv", l)] = dcw
            dproj = jnp.concatenate([db, dc, dhh], axis=1)
            big("a_w_in", l, ("mix", l), mm_tn(rec["xn"], dproj, N_DEV, name=f"a{l}_win"),
                a_w_in[l], m_a_w_in[l], v_a_w_in[l])
            dxn = mm_nt(dproj, W[("in", l)], name=f"a{l}_din", dep=take())
            dh, dga = rms_bwd(dxn, rec["h"], a_norm_full[l], dh1, name=f"a{l}_dnorm")
            small_g[("a_norm", l)] = dga
        else:
            j = l - n_a
            do = mm_nt(dhb, W[("o", j)], name=f"b{j}_do")
            big("b_w_o", j, ("mix", l), mm_tn(rec["o"], dhb, 1, name=f"b{j}_wo"),
                b_w_o[j], m_b_w_o[j], v_b_w_o[j])
            prev = None if dk is None else (dk, dv, dck, dcq)
            dq, dk, dv, dck, dcq = attn_bwd(rec["q"], k, v, do, rec["o"], rec["lse"], ccol, crow, prev, pad,
                                            name=f"b{j}_dattn", dep=take())
            dqraw, dqn = hn_bwd(dq, rec["qraw"], b_q_norm[j], name=f"b{j}_dqnorm")
            small_g[("b_q_norm", j)] = dqn
            big("b_w_q", j, ("mix", l), mm_tn(rec["xn"], dqraw, 1, name=f"b{j}_wq"),
                b_w_q[j], m_b_w_q[j], v_b_w_q[j])
            dxn = mm_nt(dqraw, W[("q", j)], name=f"b{j}_dq", dep=take())
            dh, dgb = rms_bwd(dxn, rec["h"], b_norm[j], dh1, name=f"b{j}_dnorm")
            small_g[("b_norm", j)] = dgb
            if j == 0:
                dkraw, dkn = hn_bwd(dk, shared["kv"], k_norm, name="kv_dknorm")
                dkv = jnp.concatenate([dkraw, dv.astype(BF)], axis=1)
                dc_full = _pad_cols(jnp.transpose(dck.reshape(H, T) + dcq.reshape(H, T)), LANES)
                dz, dbf = fgate_bwd(dc_full, shared["logits"], bf_pad, pad, name="f_dgate")
                big("w_kv", 0, ("mix", l), mm_tn(shared["xnk"], dkv, N_DEV, name="kv_wkv"), w_kv, m_w_kv, v_w_kv)
                dwf_t = mm_tn(dz, shared["xnk"], 1, name="f_wf", out_dtype=F32, tn_target=1024,
                              dep=take())[0, :H]
                dxn_f = mm_nt(dz, wf_pad, name="f_dxn")
                dxnk = mm_nt(dkv, W[("kv", 0)], add=dxn_f, name="kv_dxn")
                dh, dgkv = rms_bwd(dxnk, shared["h"], kv_norm, dh, name="kv_dnorm")
    land(None, dh)

    grad_x = dh[lead:][None]

    row8 = lambda a: _pad_rows(_pad_cols(a, D), 8)
    stack = lambda key, n: jnp.concatenate([small_g[(key, i)] for i in range(n)], axis=0)
    g_sharded = jnp.concatenate([dh[pad:lead], row8(stack("a_norm", n_a)), row8(stack("a_conv", n_a)), dwf_t], axis=0)
    g_repl = jnp.concatenate([row8(jnp.concatenate([dgkv, stack("b_norm", n_b)], axis=0)),
                              row8(stack("ffn_norm", depth)),
                              row8(jnp.concatenate([_pad_cols(dkn, D), _pad_cols(stack("b_q_norm", n_b), D),
                                                    _pad_cols(dbf[:, :H], D)], axis=0))], axis=0)
    n_sh = g_sharded.shape[0]
    gathered = all_gather(jnp.concatenate([g_sharded, g_repl], axis=0), name="ag_small_grads")
    parts_sh = lax.dynamic_slice_in_dim(gathered[:, :n_sh], my * Ds, Ds, axis=2)
    parts_rp = gathered[:, n_sh:]

    def pack_sh(t_meta, t_an, t_ac, t_wf):
        return jnp.concatenate([t_meta, _pad_rows(t_an, 8), _pad_rows(t_ac.reshape(n_a * 3, Ds), 8),
                                jnp.transpose(t_wf)], axis=0)

    def pack_rp(t_kv, t_bn, t_fn, t_kn, t_qn, t_bf):
        return jnp.concatenate([row8(jnp.concatenate([t_kv.reshape(1, D), t_bn], axis=0)), row8(t_fn),
                                row8(jnp.concatenate([_pad_cols(t_kn.reshape(1, -1), D), _pad_cols(t_qn, D),
                                                      _pad_cols(t_bf.reshape(1, -1), D)], axis=0))], axis=0)

    res_sh = adamw(parts_sh, pack_sh(meta, a_norm, a_conv, w_f), pack_sh(m_meta, m_a_norm, m_a_conv, m_w_f),
                   pack_sh(v_meta, v_a_norm, v_a_conv, v_w_f), name="small_sharded_adamw")
    res_rp = adamw(parts_rp, pack_rp(kv_norm, b_norm, ffn_norm, k_norm, b_q_norm, b_f),
                   pack_rp(m_kv_norm, m_b_norm, m_ffn_norm, m_k_norm, m_b_q_norm, m_b_f),
                   pack_rp(v_kv_norm, v_b_norm, v_ffn_norm, v_k_norm, v_b_q_norm, v_b_f), name="small_repl_adamw")

    def unpack(kind):
        sh, rp = res_sh[kind], res_rp[kind]
        out = {
            "meta": sh[0:n_meta],
            "a_norm": sh[r_an:r_an + n_a],
            "a_conv": sh[r_ac:r_ac + 3 * n_a].reshape(n_a, 3, Ds),
            "w_f": jnp.transpose(sh[r_wf:r_wf + H]),
            "kv_norm": rp[0],
            "b_norm": rp[1:1 + n_b],
            "ffn_norm": rp[8:8 + depth],
            "k_norm": rp[16, :HEAD_DIM],
            "b_q_norm": rp[17:17 + n_b, :HEAD_DIM],
            "b_f": rp[17 + n_b, :H],
        }
        for name, n in (("a_w_in", n_a), ("a_w_out", n_a), ("b_w_q", n_b), ("b_w_o", n_b),
                        ("ffn_w_gu", depth), ("ffn_w_down", depth)):
            out[name] = jnp.stack([upd[name][i][kind] for i in range(n)], axis=0)
        out["w_kv"] = upd["w_kv"][0][kind]
        return out

    order = ["meta", "a_norm", "a_w_in", "a_conv", "a_w_out", "kv_norm", "w_kv", "k_norm", "w_f", "b_f",
             "b_norm", "b_w_q", "b_q_norm", "b_w_o", "ffn_norm", "ffn_w_gu", "ffn_w_down"]
    outs = [loss, grad_x]
    for kind in range(4):
        vals = unpack(kind)
        outs += [vals[n] for n in order]
    return tuple(outs)
```

```python
import functools
import math

import jax
import jax.numpy as jnp
from jax import lax
from jax.experimental import pallas as pl
from jax.experimental.pallas import tpu as pltpu

N_DEV = 8
MESH_AXES = ("x", "y", "c")
EPS = 1e-6
NEG = -1e30
HEAD_DIM = 128
BLOCK = 128
LANES = 128
V7X_VMEM_LIMIT = 56 * 1024 * 1024

ADAM_LR = 0.001
ADAM_B1 = 0.9
ADAM_B2 = 0.999
ADAM_EPS = 1e-08
ADAM_WD = 0.01
ADAM_STEP = 10

BF = jnp.bfloat16
F32 = jnp.float32


def _tile(n, target, mult):
    best = None
    for t in range(mult, min(n, target) + 1, mult):
        if n % t == 0:
            best = t
    return n if best is None else best


def _params(*sem):
    return pltpu.CompilerParams(dimension_semantics=sem, vmem_limit_bytes=V7X_VMEM_LIMIT)


def _with_dep(body, in_specs, args, dep):
    if dep is None:
        return body, list(in_specs), list(args)
    n_in = len(args)

    def body_dep(*refs):
        body(*refs[:n_in], *refs[n_in + 1:])

    return body_dep, list(in_specs) + [pl.BlockSpec(memory_space=pl.ANY)], list(args) + [dep]


def mm_nn(a, w, *, name, add=None, dep=None, out_dtype=F32, tm_target=1056, tn_target=1024, tk_target=2048):
    M, K = a.shape
    G, K2, n = w.shape
    assert K == K2
    tm = _tile(M, tm_target, 16)
    tn = _tile(n, tn_target, LANES)
    tk = _tile(K, tk_target, LANES)
    nj, nk = n // tn, K // tk
    has_add = add is not None

    def body(*refs):
        if has_add:
            a_ref, w_ref, add_ref, o_ref = refs[:4]
        else:
            a_ref, w_ref, o_ref = refs[:3]
            add_ref = None

        def finish(r):
            if has_add:
                r = r + add_ref[...]
            o_ref[...] = r.astype(out_dtype)

        part = jnp.dot(a_ref[...], w_ref[...], preferred_element_type=F32)
        if nk == 1:
            finish(part)
        else:
            acc_ref = refs[-1]
            k = pl.program_id(2)

            @pl.when(k == 0)
            def _():
                acc_ref[...] = part

            @pl.when(k > 0)
            def _():
                acc_ref[...] += part

            @pl.when(k == nk - 1)
            def _():
                finish(acc_ref[...])

    in_specs = [
        pl.BlockSpec((tm, tk), lambda i, j, k: (i, k)),
        pl.BlockSpec((None, tk, tn), lambda i, j, k: (j // nj, k, j % nj)),
    ]
    args = [a, w]
    if has_add:
        in_specs.append(pl.BlockSpec((tm, tn), lambda i, j, k: (i, j)))
        args.append(add)
    body, in_specs, args = _with_dep(body, in_specs, args, dep)
    return pl.pallas_call(
        body,
        out_shape=jax.ShapeDtypeStruct((M, G * n), out_dtype),
        grid=(M // tm, G * nj, nk),
        in_specs=in_specs,
        out_specs=pl.BlockSpec((tm, tn), lambda i, j, k: (i, j)),
        scratch_shapes=[pltpu.VMEM((tm, tn), F32)] if nk > 1 else [],
        compiler_params=_params("parallel", "parallel", "arbitrary"),
        name=name,
    )(*args)


def mm_swiglu(xn, wgu, *, name, dep=None, save_dtype=BF, tm_target=528):
    M, K = xn.shape
    G, _, n = wgu.shape
    half = G // 2
    tm = _tile(M, tm_target, 16)
    tn = _tile(n, 1408, LANES)
    nj = n // tn
    Fh = half * n

    def body(a_ref, wg_ref, wu_ref, act_ref, g_ref, u_ref):
        a = a_ref[...]
        g = jnp.dot(a, wg_ref[...], preferred_element_type=F32)
        g_ref[...] = g.astype(save_dtype)
        silu = g * jax.nn.sigmoid(g)
        u = jnp.dot(a, wu_ref[...], preferred_element_type=F32)
        u_ref[...] = u.astype(save_dtype)
        act_ref[...] = (silu * u).astype(BF)

    out_block = pl.BlockSpec((tm, tn), lambda j, i: (i, j))
    once = pl.Buffered(1)
    body, in_specs, args = _with_dep(body, [
        pl.BlockSpec((tm, K), lambda j, i: (i, 0)),
        pl.BlockSpec((None, K, tn), lambda j, i: (j // nj, 0, j % nj), pipeline_mode=once),
        pl.BlockSpec((None, K, tn), lambda j, i: (half + j // nj, 0, j % nj), pipeline_mode=once),
    ], [xn, wgu, wgu], dep)
    return pl.pallas_call(
        body,
        out_shape=(jax.ShapeDtypeStruct((M, Fh), BF),
                   jax.ShapeDtypeStruct((M, Fh), save_dtype),
                   jax.ShapeDtypeStruct((M, Fh), save_dtype)),
        grid=(half * nj, M // tm),
        in_specs=in_specs,
        out_specs=(out_block, out_block, out_block),
        compiler_params=_params("parallel", "parallel"),
        name=name,
    )(*args)


def mm_nt(dy, w, *, name, add=None, dep=None, out_dtype=F32, tm_target=1056, tko_target=1024, tc_target=1408):
    M, N = dy.shape
    G, K, n = w.shape
    assert N == G * n
    tm = _tile(M, tm_target, 16)
    tko = _tile(K, tko_target, LANES)
    tc = _tile(n, tc_target, LANES)
    nc = n // tc
    steps = G * nc
    has_add = add is not None

    def body(*refs):
        if has_add:
            dy_ref, w_ref, add_ref, o_ref = refs[:4]
        else:
            dy_ref, w_ref, o_ref = refs[:3]
            add_ref = None

        def finish(r):
            if has_add:
                r = r + add_ref[...]
            o_ref[...] = r.astype(out_dtype)

        part = lax.dot_general(dy_ref[...], w_ref[...], (((1,), (1,)), ((), ())),
                               preferred_element_type=F32)
        if steps == 1:
            finish(part)
        else:
            acc_ref = refs[-1]
            s = pl.program_id(2)

            @pl.when(s == 0)
            def _():
                acc_ref[...] = part

            @pl.when(s > 0)
            def _():
                acc_ref[...] += part

            @pl.when(s == steps - 1)
            def _():
                finish(acc_ref[...])

    in_specs = [
        pl.BlockSpec((tm, tc), lambda i, o, s: (i, s)),
        pl.BlockSpec((None, tko, tc), lambda i, o, s: (s // nc, o, s % nc)),
    ]
    args = [dy, w]
    if has_add:
        in_specs.append(pl.BlockSpec((tm, tko), lambda i, o, s: (i, o)))
        args.append(add)
    body, in_specs, args = _with_dep(body, in_specs, args, dep)
    return pl.pallas_call(
        body,
        out_shape=jax.ShapeDtypeStruct((M, K), out_dtype),
        grid=(M // tm, K // tko, steps),
        in_specs=in_specs,
        out_specs=pl.BlockSpec((tm, tko), lambda i, o, s: (i, o)),
        scratch_shapes=[pltpu.VMEM((tm, tko), F32)] if steps > 1 else [],
        compiler_params=_params("parallel", "parallel", "arbitrary"),
        name=name,
    )(*args)


def mm_nt_dswiglu(dh, w_down, g_s, u_s, *, name, tm_target=1056, tf_target=512):
    M, D = dh.shape
    _, Fh, D2 = w_down.shape
    assert D == D2
    tm = _tile(M, tm_target, 16)
    tf = _tile(Fh, tf_target, LANES)

    def body(dh_ref, w_ref, g_ref, u_ref, dg_ref, du_ref):
        dact = lax.dot_general(dh_ref[...], w_ref[...], (((1,), (1,)), ((), ())),
                               preferred_element_type=F32)
        g = g_ref[...].astype(F32)
        u = u_ref[...].astype(F32)
        sig = jax.nn.sigmoid(g)
        du_ref[...] = (dact * (g * sig)).astype(BF)
        dg_ref[...] = (dact * u * (sig * (1.0 + g * (1.0 - sig)))).astype(BF)

    blk = pl.BlockSpec((tm, tf), lambda i, f: (i, f))
    return pl.pallas_call(
        body,
        out_shape=(jax.ShapeDtypeStruct((M, Fh), BF), jax.ShapeDtypeStruct((M, Fh), BF)),
        grid=(M // tm, Fh // tf),
        in_specs=[
            pl.BlockSpec((tm, D), lambda i, f: (i, 0)),
            pl.BlockSpec((None, tf, D), lambda i, f: (0, f, 0)),
            blk, blk,
        ],
        out_specs=(blk, blk),
        compiler_params=_params("parallel", "parallel"),
        name=name,
    )(dh, w_down, g_s, u_s)


def mm_tn(a, dy, groups, *, name, dep=None, out_dtype=BF, tk_target=512, tn_target=1408):
    M, K = a.shape
    M2, N = dy.shape
    assert M == M2 and N % groups == 0
    n = N // groups
    tk = _tile(K, tk_target, LANES)
    tn = _tile(n, tn_target, LANES)
    nj = n // tn

    def body(a_ref, dy_ref, o_ref):
        o_ref[...] = lax.dot_general(a_ref[...], dy_ref[...], (((0,), (0,)), ((), ())),
                                     preferred_element_type=F32).astype(out_dtype)

    body, in_specs, args = _with_dep(body, [
        pl.BlockSpec((M, tk), lambda i, j: (0, i)),
        pl.BlockSpec((M, tn), lambda i, j: (0, j)),
    ], [a, dy], dep)
    return pl.pallas_call(
        body,
        out_shape=jax.ShapeDtypeStruct((groups, K, n), out_dtype),
        grid=(K // tk, groups * nj),
        in_specs=in_specs,
        out_specs=pl.BlockSpec((None, tk, tn), lambda i, j: (j // nj, i, j % nj)),
        compiler_params=_params("parallel", "parallel"),
        name=name,
    )(*args)


def rms_fwd(h, g, *, name, dep=None):
    T, D = h.shape
    tm = _tile(T, 528, 16)

    def body(h_ref, g_ref, o_ref):
        x = h_ref[...]
        r = lax.rsqrt(jnp.mean(x * x, axis=-1, keepdims=True) + EPS)
        o_ref[...] = ((x * r) * g_ref[...]).astype(BF)

    body, in_specs, args = _with_dep(
        body, [pl.BlockSpec((tm, D), lambda i: (i, 0)), pl.BlockSpec((1, D), lambda i: (0, 0))],
        [h, g.reshape(1, D)], dep)
    return pl.pallas_call(
        body,
        out_shape=jax.ShapeDtypeStruct((T, D), BF),
        grid=(T // tm,),
        in_specs=in_specs,
        out_specs=pl.BlockSpec((tm, D), lambda i: (i, 0)),
        compiler_params=_params("parallel"),
        name=name,
    )(*args)


def rms_bwd(dxn, h, g, add, *, name):
    T, D = h.shape
    tm = _tile(T, 264, 8)

    def body(dxn_ref, h_ref, g_ref, add_ref, dh_ref, dg_ref):
        x = h_ref[...]
        dy = dxn_ref[...]
        r = lax.rsqrt(jnp.mean(x * x, axis=-1, keepdims=True) + EPS)
        xhat = x * r
        part = jnp.sum(dy * xhat, axis=0, keepdims=True)

        @pl.when(pl.program_id(0) == 0)
        def _():
            dg_ref[...] = part

        @pl.when(pl.program_id(0) > 0)
        def _():
            dg_ref[...] += part

        dxh = dy * g_ref[...]
        dh_ref[...] = add_ref[...] + r * (dxh - xhat * jnp.mean(dxh * xhat, axis=-1, keepdims=True))

    row = pl.BlockSpec((tm, D), lambda i: (i, 0))
    vec = pl.BlockSpec((1, D), lambda i: (0, 0))
    return pl.pallas_call(
        body,
        out_shape=(jax.ShapeDtypeStruct((T, D), F32), jax.ShapeDtypeStruct((1, D), F32)),
        grid=(T // tm,),
        in_specs=[row, row, vec, row],
        out_specs=(row, vec),
        compiler_params=_params("arbitrary"),
        name=name,
    )(dxn, h, g.reshape(1, D), add)


def _head_norm(x, gain):
    r = lax.rsqrt(jnp.mean(x * x, axis=-1, keepdims=True) + EPS)
    return (x * r) * gain


def hn_fwd(qraw, gain, *, name):
    T, D = qraw.shape
    H = D // HEAD_DIM
    tm = _tile(T, 528, 16)

    def body(q_ref, g_ref, o_ref):
        gain_v = g_ref[...]
        for hd in range(H):
            sl = slice(hd * HEAD_DIM, (hd + 1) * HEAD_DIM)
            o_ref[:, sl] = _head_norm(q_ref[:, sl], gain_v).astype(BF)

    return pl.pallas_call(
        body,
        out_shape=jax.ShapeDtypeStruct((T, D), BF),
        grid=(T // tm,),
        in_specs=[pl.BlockSpec((tm, D), lambda i: (i, 0)),
                  pl.BlockSpec((1, HEAD_DIM), lambda i: (0, 0))],
        out_specs=pl.BlockSpec((tm, D), lambda i: (i, 0)),
        compiler_params=_params("parallel"),
        name=name,
    )(qraw, gain.reshape(1, HEAD_DIM))


def kv_post(kv, gain, *, name):
    T, D2 = kv.shape
    D = D2 // 2
    H = D // HEAD_DIM
    tm = _tile(T, 528, 16)

    def body(k_ref, v_ref, g_ref, ko_ref, vo_ref):
        gain_v = g_ref[...]
        for hd in range(H):
            sl = slice(hd * HEAD_DIM, (hd + 1) * HEAD_DIM)
            ko_ref[:, sl] = _head_norm(k_ref[:, sl], gain_v).astype(BF)
        vo_ref[...] = v_ref[...].astype(BF)

    blk = pl.BlockSpec((tm, D), lambda i: (i, 0))
    return pl.pallas_call(
        body,
        out_shape=(jax.ShapeDtypeStruct((T, D), BF), jax.ShapeDtypeStruct((T, D), BF)),
        grid=(T // tm,),
        in_specs=[blk, pl.BlockSpec((tm, D), lambda i: (i, 1)),
                  pl.BlockSpec((1, HEAD_DIM), lambda i: (0, 0))],
        out_specs=(blk, blk),
        compiler_params=_params("parallel"),
        name=name,
    )(kv, kv, gain.reshape(1, HEAD_DIM))


def hn_bwd(dq, qraw, gain, *, name):
    T, D = dq.shape
    H = D // HEAD_DIM
    tm = _tile(T, 264, 16)

    def body(dq_ref, q_ref, g_ref, o_ref, dg_ref):
        gain_v = g_ref[...]
        part = jnp.zeros((1, HEAD_DIM), F32)
        for hd in range(H):
            sl = slice(hd * HEAD_DIM, (hd + 1) * HEAD_DIM)
            x = q_ref[:, sl]
            dy = dq_ref[:, sl]
            r = lax.rsqrt(jnp.mean(x * x, axis=-1, keepdims=True) + EPS)
            xhat = x * r
            part = part + jnp.sum(dy * xhat, axis=0, keepdims=True)
            dxh = dy * gain_v
            o_ref[:, sl] = (r * (dxh - xhat * jnp.mean(dxh * xhat, axis=-1, keepdims=True))).astype(BF)

        @pl.when(pl.program_id(0) == 0)
        def _():
            dg_ref[...] = part

        @pl.when(pl.program_id(0) > 0)
        def _():
            dg_ref[...] += part

    blk = pl.BlockSpec((tm, D), lambda i: (i, 0))
    vec = pl.BlockSpec((1, HEAD_DIM), lambda i: (0, 0))
    return pl.pallas_call(
        body,
        out_shape=(jax.ShapeDtypeStruct((T, D), BF), jax.ShapeDtypeStruct((1, HEAD_DIM), F32)),
        grid=(T // tm,),
        in_specs=[blk, blk, vec],
        out_specs=(blk, vec),
        compiler_params=_params("arbitrary"),
        name=name,
    )(dq, qraw, gain.reshape(1, HEAD_DIM))


def _shift_down(cur, above, k, rowc):
    out = pltpu.roll(cur, k, 0)
    for i in range(k):
        out = jnp.where(rowc == i, above[8 - k + i:8 - k + i + 1], out)
    return out


def _shift_up(cur, below, k, rowc):
    R = cur.shape[0]
    out = pltpu.roll(cur, R - k, 0)
    for i in range(k):
        out = jnp.where(rowc == R - k + i, below[i:i + 1], out)
    return out


def _conv3(u, u_above, wv, rowc):
    u1 = _shift_down(u, u_above, 1, rowc)
    u2 = _shift_down(u, u_above, 2, rowc)
    return wv[0:1] * u2 + wv[1:2] * u1 + wv[2:3] * u, u1, u2


def conv_fwd(proj, w, *, name):
    T, D3 = proj.shape
    D = D3 // 3
    tc = LANES if D % LANES == 0 else D
    nb = D // tc
    R = _tile(T, 264, 8)

    def body(b_ref, c_ref, h_ref, w_ref, y_ref):
        rowc = lax.broadcasted_iota(jnp.int32, (R, 1), 0)
        wv = w_ref[...]
        for r0 in range(0, T, R):
            rows = slice(r0, r0 + R)
            u = c_ref[rows, :] * h_ref[rows, :]
            if r0 == 0:
                above = jnp.zeros((8, tc), F32)
            else:
                above = c_ref[r0 - 8:r0, :] * h_ref[r0 - 8:r0, :]
            conv, _, _ = _conv3(u, above, wv, rowc)
            y_ref[rows, :] = (b_ref[rows, :] * conv).astype(BF)

    return pl.pallas_call(
        body,
        out_shape=jax.ShapeDtypeStruct((T, D), BF),
        grid=(nb,),
        in_specs=[
            pl.BlockSpec((T, tc), lambda j: (0, j)),
            pl.BlockSpec((T, tc), lambda j: (0, nb + j)),
            pl.BlockSpec((T, tc), lambda j: (0, 2 * nb + j)),
            pl.BlockSpec((3, tc), lambda j: (0, j)),
        ],
        out_specs=pl.BlockSpec((T, tc), lambda j: (0, j)),
        compiler_params=_params("parallel"),
        name=name,
    )(proj, proj, proj, w)


def conv_bwd(dy, proj, w, *, name, dep=None):
    T, D = dy.shape
    tc = LANES if D % LANES == 0 else D
    nb = D // tc
    R = _tile(T, 264, 8)

    def body(dy_ref, b_ref, c_ref, h_ref, w_ref, db_ref, dc_ref, dh_ref, dw_ref):
        rowc = lax.broadcasted_iota(jnp.int32, (R, 1), 0)
        wv = w_ref[...]
        dw = [jnp.zeros((1, tc), F32) for _ in range(3)]
        for r0 in range(0, T, R):
            rows = slice(r0, r0 + R)
            c = c_ref[rows, :]
            hh = h_ref[rows, :]
            u = c * hh
            if r0 == 0:
                above = jnp.zeros((8, tc), F32)
            else:
                above = c_ref[r0 - 8:r0, :] * h_ref[r0 - 8:r0, :]
            conv, u1, u2 = _conv3(u, above, wv, rowc)
            dyv = dy_ref[rows, :]
            db_ref[rows, :] = (dyv * conv).astype(BF)
            dconv = dyv * b_ref[rows, :]
            if r0 + R == T:
                below = jnp.zeros((8, tc), F32)
            else:
                below = dy_ref[r0 + R:r0 + R + 8, :] * b_ref[r0 + R:r0 + R + 8, :]
            dw[0] = dw[0] + jnp.sum(dconv * u2, axis=0, keepdims=True)
            dw[1] = dw[1] + jnp.sum(dconv * u1, axis=0, keepdims=True)
            dw[2] = dw[2] + jnp.sum(dconv * u, axis=0, keepdims=True)
            du = (wv[2:3] * dconv + wv[1:2] * _shift_up(dconv, below, 1, rowc)
                  + wv[0:1] * _shift_up(dconv, below, 2, rowc))
            dc_ref[rows, :] = (du * hh).astype(BF)
            dh_ref[rows, :] = (du * c).astype(BF)
        for i in range(3):
            dw_ref[i:i + 1, :] = dw[i]

    strip = pl.BlockSpec((T, tc), lambda j: (0, j))
    wblk = pl.BlockSpec((3, tc), lambda j: (0, j))
    out = jax.ShapeDtypeStruct((T, D), BF)
    body, in_specs, args = _with_dep(body, [
        strip,
        pl.BlockSpec((T, tc), lambda j: (0, j)),
        pl.BlockSpec((T, tc), lambda j: (0, nb + j)),
        pl.BlockSpec((T, tc), lambda j: (0, 2 * nb + j)),
        wblk,
    ], [dy, proj, proj, proj, w], dep)
    return pl.pallas_call(
        body,
        out_shape=(out, out, out, jax.ShapeDtypeStruct((3, D), F32)),
        grid=(nb,),
        in_specs=in_specs,
        out_specs=(strip, strip, strip, wblk),
        compiler_params=_params("parallel"),
        name=name,
    )(*args)


def _log_sigmoid(z):
    return jnp.minimum(z, 0.0) - jnp.log(1.0 + jnp.exp(-jnp.abs(z)))


def fgate_fwd(logits, bias, pad, *, name):
    T, W = logits.shape
    cb = _tile(T, 128, 8)
    nblk = T // cb

    def body(z_ref, b_ref, c_ref, lf_ref):
        row = lax.broadcasted_iota(jnp.int32, (T, 1), 0)
        lf_ref[...] = jnp.where(row >= pad, _log_sigmoid(z_ref[...] + b_ref[...]), 0.0)
        ri = lax.broadcasted_iota(jnp.int32, (cb, cb), 0)
        ci = lax.broadcasted_iota(jnp.int32, (cb, cb), 1)
        tri = (ci <= ri).astype(F32)

        def step(i, carry):
            rows = pl.ds(pl.multiple_of(i * cb, cb), cb)
            blk = lf_ref[rows, :]
            c_ref[rows, :] = carry + jnp.dot(tri, blk, precision=lax.Precision.HIGHEST,
                                             preferred_element_type=F32)
            return carry + jnp.sum(blk, axis=0, keepdims=True)

        lax.fori_loop(0, nblk, step, jnp.zeros((1, W), F32))

    return pl.pallas_call(
        body,
        out_shape=jax.ShapeDtypeStruct((T, W), F32),
        in_specs=[pl.BlockSpec(memory_space=pltpu.VMEM), pl.BlockSpec(memory_space=pltpu.VMEM)],
        out_specs=pl.BlockSpec(memory_space=pltpu.VMEM),
        scratch_shapes=[pltpu.VMEM((T, W), F32)],
        compiler_params=pltpu.CompilerParams(vmem_limit_bytes=V7X_VMEM_LIMIT),
        name=name,
    )(logits, bias)


def fgate_bwd(dc, logits, bias, pad, *, name):
    T, W = logits.shape
    cb = _tile(T, 128, 8)
    nblk = T // cb

    def body(dc_ref, z_ref, b_ref, dz_ref, db_ref, rs_ref):
        ri = lax.broadcasted_iota(jnp.int32, (cb, cb), 0)
        ci = lax.broadcasted_iota(jnp.int32, (cb, cb), 1)
        triu = (ci >= ri).astype(F32)

        def step(i, carry):
            rows = pl.ds(pl.multiple_of((nblk - 1 - i) * cb, cb), cb)
            blk = dc_ref[rows, :]
            rs_ref[rows, :] = carry + jnp.dot(triu, blk, precision=lax.Precision.HIGHEST,
                                              preferred_element_type=F32)
            return carry + jnp.sum(blk, axis=0, keepdims=True)

        lax.fori_loop(0, nblk, step, jnp.zeros((1, W), F32))
        row = lax.broadcasted_iota(jnp.int32, (T, 1), 0)
        z = z_ref[...] + b_ref[...]
        dz = jnp.where(row >= pad, rs_ref[...] * jax.nn.sigmoid(-z), 0.0)
        dz_ref[...] = dz.astype(BF)
        db_ref[...] = jnp.sum(dz, axis=0, keepdims=True)

    vm = pl.BlockSpec(memory_space=pltpu.VMEM)
    return pl.pallas_call(
        body,
        out_shape=(jax.ShapeDtypeStruct((T, W), BF), jax.ShapeDtypeStruct((1, W), F32)),
        in_specs=[vm, vm, vm],
        out_specs=(vm, vm),
        scratch_shapes=[pltpu.VMEM((T, W), F32)],
        compiler_params=pltpu.CompilerParams(vmem_limit_bytes=V7X_VMEM_LIMIT),
        name=name,
    )(dc, logits, bias)


def _scores(qb, kb, cq, ck, row, col, pad, scale):
    s = lax.dot_general(qb, kb, (((1,), (1,)), ((), ())), preferred_element_type=F32) * scale
    s = s + (cq - ck)
    return jnp.where((col <= row) & (col >= pad), s, NEG)


def attn_fwd(q, k, v, ccol, crow, pad, *, name):
    T, D = q.shape
    H = D // HEAD_DIM
    nk, tk = crow.shape[1], crow.shape[3]
    tq = tk
    nq = T // tq
    scale = 1.0 / math.sqrt(HEAD_DIM)

    def body(q_ref, k_ref, v_ref, cc_ref, cr_ref, o_ref, lse_ref):
        qi = pl.program_id(1)
        qb = q_ref[...]
        cq = cc_ref[...]
        row = qi * tq + lax.broadcasted_iota(jnp.int32, (tq, 1), 0)

        def step(kc, carry):
            m, l, acc = carry
            rows = pl.ds(pl.multiple_of(kc * tk, tk), tk)
            col = kc * tk + lax.broadcasted_iota(jnp.int32, (1, tk), 1)
            s = _scores(qb, k_ref[rows, :], cq, cr_ref[kc], row, col, pad, scale)
            m_new = jnp.maximum(m, jnp.max(s, axis=-1, keepdims=True))
            alpha = jnp.exp(m - m_new)
            p = jnp.exp(s - m_new)
            l = alpha * l + jnp.sum(p, axis=-1, keepdims=True)
            acc = alpha * acc + jnp.dot(p.astype(BF), v_ref[rows, :], preferred_element_type=F32)
            return m_new, l, acc

        init = (jnp.full((tq, 1), NEG, F32), jnp.zeros((tq, 1), F32), jnp.zeros((tq, HEAD_DIM), F32))
        m, l, acc = lax.fori_loop(0, qi + 1, step, init)
        valid = row >= pad
        o_ref[...] = jnp.where(valid, acc / l, 0.0).astype(BF)
        lse_ref[...] = jnp.where(valid, m + jnp.log(l), 0.0)

    return pl.pallas_call(
        body,
        out_shape=(jax.ShapeDtypeStruct((T, D), BF), jax.ShapeDtypeStruct((H, T, 1), F32)),
        grid=(H, nq),
        in_specs=[
            pl.BlockSpec((tq, HEAD_DIM), lambda h, i: (i, h)),
            pl.BlockSpec((T, HEAD_DIM), lambda h, i: (0, h)),
            pl.BlockSpec((T, HEAD_DIM), lambda h, i: (0, h)),
            pl.BlockSpec((None, tq, 1), lambda h, i: (h, i, 0)),
            pl.BlockSpec((None, nk, 1, tk), lambda h, i: (h, 0, 0, 0)),
        ],
        out_specs=(pl.BlockSpec((tq, HEAD_DIM), lambda h, i: (i, h)),
                   pl.BlockSpec((None, tq, 1), lambda h, i: (h, i, 0))),
        compiler_params=_params("parallel", "arbitrary"),
        name=name,
    )(q, k, v, ccol, crow)


def attn_bwd(q, k, v, do, o, lse, ccol, crow, prev, pad, *, name, dep=None):
    T, D = q.shape
    H = D // HEAD_DIM
    nk, tk = crow.shape[1], crow.shape[3]
    tq = tk
    nq = T // tq
    scale = 1.0 / math.sqrt(HEAD_DIM)
    has_prev = prev is not None

    def body(*refs):
        q_ref, k_ref, v_ref, do_ref, o_ref, lse_ref, cc_ref, cr_ref = refs[:8]
        refs = refs[8:]
        if has_prev:
            pk_ref, pv_ref, pc_ref, pq_ref = refs[:4]
            refs = refs[4:]
        dq_ref, dk_ref, dv_ref, dck_ref, dcq_ref, delta_ref = refs
        kc = pl.program_id(1)

        @pl.when(kc == 0)
        def _():
            dq_ref[...] = jnp.zeros_like(dq_ref)
            dcq_ref[...] = pq_ref[...] if has_prev else jnp.zeros_like(dcq_ref)
            do_used = do_ref[...].astype(BF).astype(F32)
            delta_ref[...] = jnp.sum(do_used * o_ref[...].astype(F32), axis=-1, keepdims=True)

        kb = k_ref[...]
        vb = v_ref[...]
        ck = cr_ref[...]
        col = kc * tk + lax.broadcasted_iota(jnp.int32, (1, tk), 1)

        def step(qi, carry):
            dk, dv, dck = carry
            rows = pl.ds(pl.multiple_of(qi * tq, tq), tq)
            row = qi * tq + lax.broadcasted_iota(jnp.int32, (tq, 1), 0)
            qb = q_ref[rows, :]
            dob = do_ref[rows, :].astype(BF)
            s = _scores(qb, kb, cc_ref[rows, :], ck, row, col, pad, scale)
            p = jnp.exp(s - lse_ref[rows, :])
            dp = lax.dot_general(dob, vb, (((1,), (1,)), ((), ())), preferred_element_type=F32)
            ds = p * (dp - delta_ref[rows, :])
            dsb = ds.astype(BF)
            dv = dv + lax.dot_general(p.astype(BF), dob, (((0,), (0,)), ((), ())),
                                      preferred_element_type=F32)
            dk = dk + lax.dot_general(dsb, qb, (((0,), (0,)), ((), ())), preferred_element_type=F32)
            dq_ref[rows, :] += jnp.dot(dsb, kb, preferred_element_type=F32) * scale
            dcq_ref[rows, :] += jnp.sum(ds, axis=1, keepdims=True)
            dck = dck - jnp.sum(ds, axis=0, keepdims=True)
            return dk, dv, dck

        init = (jnp.zeros((tk, HEAD_DIM), F32), jnp.zeros((tk, HEAD_DIM), F32), jnp.zeros((1, tk), F32))
        dk, dv, dck = lax.fori_loop(kc, nq, step, init)
        dk = dk * scale
        if has_prev:
            dk = dk + pk_ref[...]
            dv = dv + pv_ref[...]
            dck = dck + pc_ref[...]
        dk_ref[...] = dk
        dv_ref[...] = dv
        dck_ref[...] = dck

    head_all = pl.BlockSpec((T, HEAD_DIM), lambda h, j: (0, h))
    head_blk = pl.BlockSpec((tk, HEAD_DIM), lambda h, j: (j, h))
    col_all = pl.BlockSpec((None, T, 1), lambda h, j: (h, 0, 0))
    row_blk = pl.BlockSpec((None, None, 1, tk), lambda h, j: (h, j, 0, 0))
    in_specs = [head_all, head_blk, head_blk, head_all, head_all, col_all, col_all, row_blk]
    args = [q, k, v, do, o, lse, ccol, crow]
    if has_prev:
        in_specs += [head_blk, head_blk, row_blk, col_all]
        args += list(prev)
    body, in_specs, args = _with_dep(body, in_specs, args, dep)
    return pl.pallas_call(
        body,
        out_shape=(jax.ShapeDtypeStruct((T, D), F32), jax.ShapeDtypeStruct((T, D), F32),
                   jax.ShapeDtypeStruct((T, D), F32), jax.ShapeDtypeStruct((H, nk, 1, tk), F32),
                   jax.ShapeDtypeStruct((H, T, 1), F32)),
        grid=(H, nk),
        in_specs=in_specs,
        out_specs=(head_all, head_blk, head_blk, row_blk, col_all),
        scratch_shapes=[pltpu.VMEM((T, 1), F32)],
        compiler_params=_params("parallel", "arbitrary"),
        name=name,
    )(*args)


def loss_head(h, target, lead, *, name):
    T, D = h.shape
    tm = lead
    assert T % tm == 0 and target.shape[0] % tm == 0
    inv_d = 1.0 / D

    def body(h_ref, t_ref, dh_ref, loss_ref):
        i = pl.program_id(0)

        @pl.when(i == 0)
        def _():
            dh_ref[...] = jnp.zeros_like(dh_ref)
            loss_ref[...] = jnp.zeros_like(loss_ref)

        @pl.when(i > 0)
        def _():
            e = h_ref[...] - t_ref[...]
            dh_ref[...] = e * inv_d
            loss_ref[...] += 0.5 * inv_d * jnp.sum(e * e)

    return pl.pallas_call(
        body,
        out_shape=(jax.ShapeDtypeStruct((T, D), F32), jax.ShapeDtypeStruct((8, LANES), F32)),
        grid=(T // tm,),
        in_specs=[pl.BlockSpec((tm, D), lambda i: (i, 0)),
                  pl.BlockSpec((tm, D), lambda i: (jnp.maximum(i - 1, 0), 0))],
        out_specs=(pl.BlockSpec((tm, D), lambda i: (i, 0)),
                   pl.BlockSpec((8, LANES), lambda i: (0, 0))),
        compiler_params=_params("arbitrary"),
        name=name,
    )(h, target)


def adamw(parts, w, m, v, *, name):
    P, R, C = parts.shape
    tr = _tile(R, max(16, (128 * 1024) // C), 16)

    def body(p_ref, w_ref, m_ref, v_ref, g_ref, d_ref, mo_ref, vo_ref):
        g = p_ref[0].astype(F32)
        for i in range(1, P):
            g = g + p_ref[i].astype(F32)
        m_new = ADAM_B1 * m_ref[...] + (1.0 - ADAM_B1) * g
        v_new = ADAM_B2 * v_ref[...] + (1.0 - ADAM_B2) * jnp.square(g)
        m_hat = m_new / (1.0 - ADAM_B1 ** ADAM_STEP)
        v_hat = v_new / (1.0 - ADAM_B2 ** ADAM_STEP)
        g_ref[...] = g
        d_ref[...] = -ADAM_LR * (m_hat / (jnp.sqrt(v_hat) + ADAM_EPS) + ADAM_WD * w_ref[...])
        mo_ref[...] = m_new
        vo_ref[...] = v_new

    blk = pl.BlockSpec((tr, C), lambda i: (i, 0))
    out = jax.ShapeDtypeStruct((R, C), F32)
    return pl.pallas_call(
        body,
        out_shape=(out, out, out, out),
        grid=(R // tr,),
        in_specs=[pl.BlockSpec((P, tr, C), lambda i: (0, i, 0)), blk, blk, blk],
        out_specs=(blk, blk, blk, blk),
        compiler_params=_params("parallel"),
        name=name,
    )(parts, w, m, v)


def _flip(v, bit):
    return 1 - v if bit else v


def all_gather(shard, *, name):
    def body(x_ref, out_ref, send_sems, recv_sems, local_sem):
        x, y, c = lax.axis_index("x"), lax.axis_index("y"), lax.axis_index("c")
        me, sibling = (x, y, c), (x, y, 1 - c)
        chips = [(1 - x, y), (x, 1 - y), (1 - x, 1 - y)]

        def block(px, py, pc):
            return out_ref.at[4 * px + 2 * py + pc]

        def copy(k, blk, to, src=None):
            return pltpu.make_async_remote_copy(
                src_ref=block(*blk) if src is None else src,
                dst_ref=block(*blk),
                send_sem=send_sems.at[k],
                recv_sem=recv_sems.at[k],
                device_id=to,
                device_id_type=pl.DeviceIdType.MESH,
            )

        mine = pltpu.make_async_copy(x_ref, block(*me), local_sem)
        mine.start()
        first = [copy(0, me, sibling, src=x_ref)]
        first += [copy(1 + j, me, (*chip, c), src=x_ref) for j, chip in enumerate(chips)]
        for cp in first:
            cp.start()
        passed = [copy(4 + j, (*chip, c), sibling) for j, chip in enumerate(chips)]
        for j, chip in enumerate(chips):
            copy(1 + j, (*chip, c), me).wait_recv()
            passed[j].start()
        copy(0, sibling, me).wait_recv()
        for j, chip in enumerate(chips):
            copy(4 + j, (*chip, 1 - c), me).wait_recv()
        for cp in first + passed:
            cp.wait_send()
        mine.wait()

    return pl.pallas_call(
        body,
        out_shape=jax.ShapeDtypeStruct((N_DEV,) + shard.shape, shard.dtype),
        in_specs=[pl.BlockSpec(memory_space=pl.ANY)],
        out_specs=pl.BlockSpec(memory_space=pl.ANY),
        scratch_shapes=[pltpu.SemaphoreType.DMA((7,)), pltpu.SemaphoreType.DMA((7,)),
                        pltpu.SemaphoreType.DMA],
        name=name,
    )(shard)


def exchange_slabs(slabs, *, name):
    def body(g_ref, r_ref, send_sems, recv_sems, local_sem):
        x, y, c = lax.axis_index("x"), lax.axis_index("y"), lax.axis_index("c")
        me = 4 * x + 2 * y + c
        mine = pltpu.make_async_copy(g_ref.at[me], r_ref.at[me], local_sem)
        mine.start()
        sends, recvs = [], []
        for k in range(1, N_DEV):
            px, py, pc = _flip(x, (k >> 2) & 1), _flip(y, (k >> 1) & 1), _flip(c, k & 1)
            peer = 4 * px + 2 * py + pc
            sends.append(pltpu.make_async_remote_copy(
                src_ref=g_ref.at[peer], dst_ref=r_ref.at[me],
                send_sem=send_sems.at[k - 1], recv_sem=recv_sems.at[k - 1],
                device_id=(px, py, pc), device_id_type=pl.DeviceIdType.MESH))
            recvs.append(pltpu.make_async_remote_copy(
                src_ref=g_ref.at[peer], dst_ref=r_ref.at[peer],
                send_sem=send_sems.at[k - 1], recv_sem=recv_sems.at[k - 1],
                device_id=(px, py, pc), device_id_type=pl.DeviceIdType.MESH))
        for cp in sends:
            cp.start()
        for cp in recvs:
            cp.wait_recv()
        for cp in sends:
            cp.wait_send()
        mine.wait()

    return pl.pallas_call(
        body,
        out_shape=jax.ShapeDtypeStruct(slabs.shape, slabs.dtype),
        in_specs=[pl.BlockSpec(memory_space=pl.ANY)],
        out_specs=pl.BlockSpec(memory_space=pl.ANY),
        scratch_shapes=[pltpu.SemaphoreType.DMA((7,)), pltpu.SemaphoreType.DMA((7,)),
                        pltpu.SemaphoreType.DMA],
        name=name,
    )(slabs)


def reduce_adamw(slabs, w, m, v, *, name):
    got = exchange_slabs(slabs, name=name + "_xchg")
    return adamw(got, w, m, v, name=name + "_adamw")


_HBM = pl.BlockSpec(memory_space=pltpu.HBM)
_SEM = pl.BlockSpec(memory_space=pltpu.SEMAPHORE)
_ANY = pl.BlockSpec(memory_space=pl.ANY)
_EFFECT = pltpu.SideEffectType.DATAFLOW_SIDE_EFFECTING
_N_FIRST = 4


def _first_copies(land_ref, send_sems, recv_sems):
    x, y, c = lax.axis_index("x"), lax.axis_index("y"), lax.axis_index("c")
    mine = land_ref.at[4 * x + 2 * y + c]
    targets = [(x, y, 1 - c), (1 - x, y, c), (x, 1 - y, c), (1 - x, 1 - y, c)]
    sends, recvs = [], []
    for k, (px, py, pc) in enumerate(targets):
        common = dict(send_sem=send_sems.at[k], recv_sem=recv_sems.at[k], device_id=(px, py, pc),
                      device_id_type=pl.DeviceIdType.MESH)
        sends.append(pltpu.make_async_remote_copy(src_ref=mine, dst_ref=mine, **common))
        theirs = land_ref.at[4 * px + 2 * py + pc]
        recvs.append(pltpu.make_async_remote_copy(src_ref=theirs, dst_ref=theirs, **common))
    return sends, recvs


def _second_copies(land_ref, send_sems, recv_sems):
    x, y, c = lax.axis_index("x"), lax.axis_index("y"), lax.axis_index("c")
    sends, recvs = [], []
    for j, (px, py) in enumerate([(1 - x, y), (x, 1 - y), (1 - x, 1 - y)]):
        common = dict(send_sem=send_sems.at[j], recv_sem=recv_sems.at[j], device_id=(x, y, 1 - c),
                      device_id_type=pl.DeviceIdType.MESH)
        blk = land_ref.at[4 * px + 2 * py + c]
        sends.append(pltpu.make_async_remote_copy(src_ref=blk, dst_ref=blk, **common))
        got = land_ref.at[4 * px + 2 * py + (1 - c)]
        recvs.append(pltpu.make_async_remote_copy(src_ref=got, dst_ref=got, **common))
    return sends, recvs


def gather_start(shard, me, after, *, name):
    R, C = shard.shape
    tr = _tile(R, max(16, (512 * 1024) // C), 16)

    def place_body(me_ref, x_ref, o_ref):
        o_ref[...] = x_ref[...].astype(BF)

    land = pl.pallas_call(
        place_body, name=name + "_own",
        out_shape=jax.ShapeDtypeStruct((N_DEV, R, C), BF),
        grid_spec=pltpu.PrefetchScalarGridSpec(
            num_scalar_prefetch=1,
            grid=(R // tr,),
            in_specs=[pl.BlockSpec((tr, C), lambda i, me_ref: (i, 0))],
            out_specs=pl.BlockSpec((None, tr, C), lambda i, me_ref: (me_ref[0], i, 0)),
        ),
        compiler_params=_params("parallel"),
    )(me.reshape(1).astype(jnp.int32), shard)

    def body(land_ref, after_ref, send_sems, recv_sems, land_thru, token):
        sends, _ = _first_copies(land_ref, send_sems, recv_sems)
        for cp in sends:
            cp.start()
        token[...] = jnp.zeros_like(token)

    send_sems, recv_sems, land_thru, token = pl.pallas_call(
        body, name=name + "_s1",
        out_shape=(pltpu.SemaphoreType.DMA((_N_FIRST,)), pltpu.SemaphoreType.DMA((_N_FIRST,)),
                   pltpu.HBM(land.shape, land.dtype), jax.ShapeDtypeStruct((8, LANES), F32)),
        in_specs=(_HBM, _ANY),
        out_specs=(_SEM, _SEM, _HBM, pl.BlockSpec(memory_space=pltpu.VMEM)),
        input_output_aliases={0: 2},
        compiler_params=pltpu.CompilerParams(has_side_effects=_EFFECT),
    )(pltpu.with_memory_space_constraint(land, pltpu.HBM), after)
    return (send_sems, recv_sems, land_thru), token


def gather_mid(handle, after, *, name):
    send_sems, recv_sems, land_thru = handle

    def body(land_ref, send1, recv1, after_ref, send2, recv2, land_out, token):
        sends, recvs = _first_copies(land_ref, send1, recv1)
        for cp in sends:
            cp.wait_send()
        for cp in recvs:
            cp.wait_recv()
        seconds, _ = _second_copies(land_ref, send2, recv2)
        for cp in seconds:
            cp.start()
        token[...] = jnp.zeros_like(token)

    send2, recv2, land2, token = pl.pallas_call(
        body, name=name + "_s2",
        out_shape=(pltpu.SemaphoreType.DMA((3,)), pltpu.SemaphoreType.DMA((3,)),
                   pltpu.HBM(land_thru.shape, land_thru.dtype), jax.ShapeDtypeStruct((8, LANES), F32)),
        in_specs=(_HBM, _SEM, _SEM, _ANY),
        out_specs=(_SEM, _SEM, _HBM, pl.BlockSpec(memory_space=pltpu.VMEM)),
        input_output_aliases={0: 2},
        compiler_params=pltpu.CompilerParams(has_side_effects=_EFFECT),
    )(land_thru, send_sems, recv_sems, after)
    return (send2, recv2, land2), token


def gather_finish(handle, after, *, name):
    send2, recv2, land2 = handle

    def body(land_ref, send2, recv2, after_ref, got_ref):
        sends, recvs = _second_copies(land_ref, send2, recv2)
        for cp in sends:
            cp.wait_send()
        for cp in recvs:
            cp.wait_recv()

    return pl.pallas_call(
        body, name=name + "_w",
        out_shape=pltpu.HBM(land2.shape, land2.dtype),
        in_specs=(_HBM, _SEM, _SEM, _ANY),
        out_specs=_HBM,
        input_output_aliases={0: 0},
        compiler_params=pltpu.CompilerParams(has_side_effects=_EFFECT),
    )(land2, send2, recv2, after)


def _slab_copies(g_ref, r_ref, send_sems, recv_sems):
    x, y, c = lax.axis_index("x"), lax.axis_index("y"), lax.axis_index("c")
    me = 4 * x + 2 * y + c
    sends, recvs = [], []
    for k in range(1, N_DEV):
        px, py, pc = _flip(x, (k >> 2) & 1), _flip(y, (k >> 1) & 1), _flip(c, k & 1)
        peer = 4 * px + 2 * py + pc
        common = dict(send_sem=send_sems.at[k - 1], recv_sem=recv_sems.at[k - 1], device_id=(px, py, pc),
                      device_id_type=pl.DeviceIdType.MESH)
        sends.append(pltpu.make_async_remote_copy(src_ref=g_ref.at[peer], dst_ref=r_ref.at[me], **common))
        recvs.append(pltpu.make_async_remote_copy(src_ref=g_ref.at[peer], dst_ref=r_ref.at[peer], **common))
    return sends, recvs


def exchange_start(slabs, *, name):
    land = lax.empty(slabs.shape, slabs.dtype)

    def body(g_ref, r_ref, send_sems, recv_sems, g_thru, r_thru, token):
        sends, _ = _slab_copies(g_ref, r_ref, send_sems, recv_sems)
        for cp in sends:
            cp.start()
        token[...] = jnp.zeros_like(token)

    send_sems, recv_sems, g_thru, r_thru, token = pl.pallas_call(
        body, name=name,
        out_shape=(pltpu.SemaphoreType.DMA((N_DEV - 1,)), pltpu.SemaphoreType.DMA((N_DEV - 1,)),
                   pltpu.HBM(slabs.shape, slabs.dtype), pltpu.HBM(slabs.shape, slabs.dtype),
                   jax.ShapeDtypeStruct((8, LANES), F32)),
        in_specs=(_HBM, _HBM),
        out_specs=(_SEM, _SEM, _HBM, _HBM, pl.BlockSpec(memory_space=pltpu.VMEM)),
        input_output_aliases={0: 2, 1: 3},
        compiler_params=pltpu.CompilerParams(has_side_effects=_EFFECT),
    )(pltpu.with_memory_space_constraint(slabs, pltpu.HBM), pltpu.with_memory_space_constraint(land, pltpu.HBM))
    return (send_sems, recv_sems, g_thru, r_thru), token


def exchange_finish(handle, after, *, name):
    send_sems, recv_sems, g_thru, r_thru = handle

    def body(g_ref, r_ref, send_sems, recv_sems, after_ref, g_out, r_out):
        sends, recvs = _slab_copies(g_ref, r_ref, send_sems, recv_sems)
        for cp in sends:
            cp.wait_send()
        for cp in recvs:
            cp.wait_recv()

    return pl.pallas_call(
        body, name=name,
        out_shape=(pltpu.HBM(g_thru.shape, g_thru.dtype), pltpu.HBM(r_thru.shape, r_thru.dtype)),
        in_specs=(_HBM, _HBM, _SEM, _SEM, _ANY),
        out_specs=(_HBM, _HBM),
        input_output_aliases={0: 0, 1: 1},
        compiler_params=pltpu.CompilerParams(has_side_effects=_EFFECT),
    )(g_thru, r_thru, send_sems, recv_sems, after)


def adamw_own(own, got, me, w, m, v, *, name):
    P, R, C = got.shape
    tr = _tile(R, max(16, (128 * 1024) // C), 16)

    def body(me_ref, own_ref, p_ref, w_ref, m_ref, v_ref, g_ref, d_ref, mo_ref, vo_ref):
        mine = own_ref[...].astype(F32)
        g = None
        for i in range(P):
            term = jnp.where(me_ref[0] == i, mine, p_ref[i].astype(F32))
            g = term if g is None else g + term
        m_new = ADAM_B1 * m_ref[...] + (1.0 - ADAM_B1) * g
        v_new = ADAM_B2 * v_ref[...] + (1.0 - ADAM_B2) * jnp.square(g)
        m_hat = m_new / (1.0 - ADAM_B1 ** ADAM_STEP)
        v_hat = v_new / (1.0 - ADAM_B2 ** ADAM_STEP)
        g_ref[...] = g
        d_ref[...] = -ADAM_LR * (m_hat / (jnp.sqrt(v_hat) + ADAM_EPS) + ADAM_WD * w_ref[...])
        mo_ref[...] = m_new
        vo_ref[...] = v_new

    blk = pl.BlockSpec((tr, C), lambda i, me_ref: (i, 0))
    out = jax.ShapeDtypeStruct((R, C), F32)
    return pl.pallas_call(
        body,
        out_shape=(out, out, out, out),
        grid_spec=pltpu.PrefetchScalarGridSpec(
            num_scalar_prefetch=1,
            grid=(R // tr,),
            in_specs=[pl.BlockSpec((None, tr, C), lambda i, me_ref: (me_ref[0], i, 0)),
                      pl.BlockSpec((P, tr, C), lambda i, me_ref: (0, i, 0)), blk, blk, blk],
            out_specs=(blk, blk, blk, blk),
        ),
        compiler_params=_params("parallel"),
        name=name,
    )(me.reshape(1).astype(jnp.int32), own, got, w, m, v)


def _pad_rows(a, rows):
    return jnp.pad(a, ((0, rows - a.shape[0]), (0, 0)))


def _pad_cols(a, cols):
    return jnp.pad(a, ((0, 0), (0, cols - a.shape[1])))


def kernel(x, meta, a_norm, a_w_in, a_conv, a_w_out, kv_norm, w_kv, k_norm, w_f, b_f, b_norm, b_w_q, b_q_norm, b_w_o, ffn_norm, ffn_w_gu, ffn_w_down, loss_target, m_meta, m_a_norm, m_a_w_in, m_a_conv, m_a_w_out, m_kv_norm, m_w_kv, m_k_norm, m_w_f, m_b_f, m_b_norm, m_b_w_q, m_b_q_norm, m_b_w_o, m_ffn_norm, m_ffn_w_gu, m_ffn_w_down, v_meta, v_a_norm, v_a_w_in, v_a_conv, v_a_w_out, v_kv_norm, v_w_kv, v_k_norm, v_w_f, v_b_f, v_b_norm, v_b_w_q, v_b_q_norm, v_b_w_o, v_ffn_norm, v_ffn_w_gu, v_ffn_w_down):
    S, D = x.shape[1], x.shape[2]
    n_meta = meta.shape[0]
    Ds = meta.shape[1]
    H = D // HEAD_DIM
    n_a, n_b = a_w_in.shape[0], b_w_q.shape[0]
    depth = n_a + n_b
    Fs = ffn_w_down.shape[1]
    pad = BLOCK - n_meta
    lead = pad + n_meta
    T = lead + S
    tk_attn = _tile(T, 384, LANES)
    nk_attn = T // tk_attn
    my = 4 * lax.axis_index("x") + 2 * lax.axis_index("y") + lax.axis_index("c")

    wf_t = w_f.reshape(H, Ds)
    small = jnp.concatenate([meta, _pad_rows(a_norm, 8), _pad_rows(a_conv.reshape(n_a * 3, Ds), 8), wf_t], axis=0)
    r_an, r_ac, r_wf = n_meta, n_meta + 8, n_meta + 16
    gs = all_gather(small, name="ag_small")
    unshard = lambda blk: jnp.transpose(blk, (1, 0, 2)).reshape(blk.shape[1], D)
    meta_full = unshard(gs[:, 0:n_meta])
    a_norm_full = unshard(gs[:, r_an:r_an + n_a])
    a_conv_full = unshard(gs[:, r_ac:r_ac + 3 * n_a]).reshape(n_a, 3, D)
    w_f_full = gs[:, r_wf:r_wf + H].reshape(D, H)
    wf_pad = _pad_cols(w_f_full, LANES).astype(BF)[None]
    bf_pad = _pad_cols(b_f.reshape(1, H), LANES)

    def layer_shards(l):
        if l < n_a:
            mix = [(("in", l), a_w_in[l]), (("out", l), a_w_out[l])]
        else:
            j = l - n_a
            mix = ([(("kv", 0), w_kv)] if j == 0 else []) + [(("q", j), b_w_q[j]), (("o", j), b_w_o[j])]
        return mix + [(("gu", l), ffn_w_gu[l]), (("dn", l), ffn_w_down[l])]

    first_level, second_level, W = {}, {}, {}
    st = {"done": None, "tok": None}

    def note(val):
        st["done"] = val
        return val

    def take():
        tok, st["tok"] = st["tok"], None
        return tok

    def chain_after(default):
        if st["tok"] is not None:
            return st["tok"]
        return default if st["done"] is None else st["done"]

    def ag_name(key):
        return f"ag_{key[0]}{key[1]}"

    def start_layer(l):
        for key, shard in layer_shards(l):
            first_level[key], st["tok"] = gather_start(shard, my, chain_after(shard), name=ag_name(key))

    def pass_on(keys):
        for key in keys:
            second_level[key], st["tok"] = gather_mid(first_level.pop(key), chain_after(None), name=ag_name(key))

    def weight(key, shape=None):
        w = gather_finish(second_level.pop(key), st["done"], name=ag_name(key))
        W[key] = w if shape is None else w.reshape(shape)
        return W[key]

    def layer_keys(l):
        keys = [key for key, _ in layer_shards(l)]
        return keys[:-2], keys[-2:]

    h = note(jnp.concatenate([jnp.zeros((pad, D), F32), meta_full, x[0]], axis=0))
    start_layer(0)
    pass_on(layer_keys(0)[0])
    saved = []
    shared = None
    for l in range(depth):
        rec = {"h": h}
        mix_keys, ffn_keys = layer_keys(l)

        def ahead():
            if l >= 1:
                pass_on(ffn_keys)
            if l + 1 < depth:
                start_layer(l + 1)

        if l < n_a:
            xn = note(rms_fwd(h, a_norm_full[l], name=f"a{l}_norm", dep=take()))
            ahead()
            proj = note(mm_nn(xn, weight(("in", l)), name=f"a{l}_in", dep=take()))
            if l == 0:
                pass_on(ffn_keys)
            y = note(conv_fwd(proj, a_conv_full[l], name=f"a{l}_conv"))
            h1 = note(mm_nn(y, weight(("out", l), (1, D, D)), add=h, name=f"a{l}_out", dep=take()))
            rec.update(xn=xn, proj=proj, y=y)
        else:
            j = l - n_a
            if j == 0:
                xnk = note(rms_fwd(h, kv_norm, name="kv_norm", dep=take()))
                ahead()
                kv = note(mm_nn(xnk, weight(("kv", 0)), name="kv_proj", dep=take()))
                k, v = kv_post(kv, k_norm, name="kv_post")
                logits = mm_nn(xnk, wf_pad, name="f_logits", tn_target=LANES)
                cfull = fgate_fwd(logits, bf_pad, pad, name="f_gate")
                c_t = jnp.transpose(cfull[:, :H])
                ccol = c_t.reshape(H, T, 1)
                crow = c_t.reshape(H, nk_attn, 1, tk_attn)
                shared = dict(h=h, xnk=xnk, kv=kv, logits=logits)
                xn = note(rms_fwd(h, b_norm[j], name=f"b{j}_norm"))
            else:
                xn = note(rms_fwd(h, b_norm[j], name=f"b{j}_norm", dep=take()))
                ahead()
            qraw = note(mm_nn(xn, weight(("q", j), (1, D, D)), name=f"b{j}_q", dep=take()))
            q = hn_fwd(qraw, b_q_norm[j], name=f"b{j}_qnorm")
            o, lse = attn_fwd(q, k, v, ccol, crow, pad, name=f"b{j}_attn")
            note(o)
            h1 = note(mm_nn(o, weight(("o", j), (1, D, D)), add=h, name=f"b{j}_o"))
            rec.update(xn=xn, qraw=qraw, q=q, o=o, lse=lse)
        xn2 = note(rms_fwd(h1, ffn_norm[l], name=f"f{l}_norm", dep=take()))
        if l + 1 < depth:
            pass_on(layer_keys(l + 1)[0])
        act, g_s, u_s = mm_swiglu(xn2, weight(("gu", l)), name=f"f{l}_gu", dep=take())
        note(act)
        h = note(mm_nn(act, weight(("dn", l), (1, N_DEV * Fs, D)), add=h1, name=f"f{l}_down", tk_target=1408))
        rec.update(h1=h1, xn2=xn2, act=act, g=g_s, u=u_s)
        saved.append(rec)

    dh, loss_tile = loss_head(h, loss_target[0], lead, name="loss")
    loss = lax.psum(loss_tile[0, 0], MESH_AXES)

    upd = {}
    small_g = {}
    inflight = []

    def big(name, l, section, slabs, w, m, v):
        handle, st["tok"] = exchange_start(slabs.reshape(N_DEV, -1, w.shape[-1]), name=f"{name}{l}_xs")
        inflight.append((section, name, l, handle, w, m, v))

    def land(sections, after):
        for entry in [e for e in inflight if sections is None or e[0] in sections]:
            inflight.remove(entry)
            _, name, l, handle, w, m, v = entry
            own, got = exchange_finish(handle, after, name=f"{name}{l}_xw")
            shp = w.shape
            flat = lambda t: t.reshape(-1, shp[-1])
            res = adamw_own(own, got, my, flat(w), flat(m), flat(v), name=f"{name}{l}_adamw")
            upd.setdefault(name, {})[l] = [r.reshape(shp) for r in res]

    dk = dv = dck = dcq = None
    for l in reversed(range(depth)):
        rec = saved[l]
        land([("ffn", l + 1)], dh)
        dhb = dh.astype(BF)
        dg, du = mm_nt_dswiglu(dhb, W[("dn", l)], rec["g"], rec["u"], name=f"f{l}_ddown")
        big("ffn_w_down", l, ("ffn", l), mm_tn(rec["act"], dhb, 1, name=f"f{l}_wdown"),
            ffn_w_down[l], m_ffn_w_down[l], v_ffn_w_down[l])
        dgu = jnp.concatenate([dg, du], axis=1)
        big("ffn_w_gu", l, ("ffn", l), mm_tn(rec["xn2"], dgu, N_DEV, name=f"f{l}_wgu", dep=take()),
            ffn_w_gu[l], m_ffn_w_gu[l], v_ffn_w_gu[l])
        dxn2 = mm_nt(dgu, W[("gu", l)], name=f"f{l}_dgu", dep=take())
        dh1, dgf = rms_bwd(dxn2, rec["h1"], ffn_norm[l], dh, name=f"f{l}_dnorm")
        small_g[("ffn_norm", l)] = dgf
        land([("mix", l + 1)], dh1)
        dhb = dh1.astype(BF)
        if l < n_a:
            dy = mm_nt(dhb, W[("out", l)], name=f"a{l}_dout")
            big("a_w_out", l, ("mix", l), mm_tn(rec["y"], dhb, 1, name=f"a{l}_wout"),
                a_w_out[l], m_a_w_out[l], v_a_w_out[l])
            db, dc, dhh, dcw = conv_bwd(dy, rec["proj"], a_conv_full[l], name=f"a{l}_dconv", dep=take())
            small_g[("a_conv", l)] = dcw
            dproj = jnp.concatenate([db, dc, dhh], axis=1)
            big("a_w_in", l, ("mix", l), mm_tn(rec["xn"], dproj, N_DEV, name=f"a{l}_win"),
                a_w_in[l], m_a_w_in[l], v_a_w_in[l])
            dxn = mm_nt(dproj, W[("in", l)], name=f"a{l}_din", dep=take())
            dh, dga = rms_bwd(dxn, rec["h"], a_norm_full[l], dh1, name=f"a{l}_dnorm")
            small_g[("a_norm", l)] = dga
        else:
            j = l - n_a
            do = mm_nt(dhb, W[("o", j)], name=f"b{j}_do")
            big("b_w_o", j, ("mix", l), mm_tn(rec["o"], dhb, 1, name=f"b{j}_wo"),
                b_w_o[j], m_b_w_o[j], v_b_w_o[j])
            prev = None if dk is None else (dk, dv, dck, dcq)
            dq, dk, dv, dck, dcq = attn_bwd(rec["q"], k, v, do, rec["o"], rec["lse"], ccol, crow, prev, pad,
                                            name=f"b{j}_dattn", dep=take())
            dqraw, dqn = hn_bwd(dq, rec["qraw"], b_q_norm[j], name=f"b{j}_dqnorm")
            small_g[("b_q_norm", j)] = dqn
            big("b_w_q", j, ("mix", l), mm_tn(rec["xn"], dqraw, 1, name=f"b{j}_wq"),
                b_w_q[j], m_b_w_q[j], v_b_w_q[j])
            dxn = mm_nt(dqraw, W[("q", j)], name=f"b{j}_dq", dep=take())
            dh, dgb = rms_bwd(dxn, rec["h"], b_norm[j], dh1, name=f"b{j}_dnorm")
            small_g[("b_norm", j)] = dgb
            if j == 0:
                dkraw, dkn = hn_bwd(dk, shared["kv"], k_norm, name="kv_dknorm")
                dkv = jnp.concatenate([dkraw, dv.astype(BF)], axis=1)
                dc_full = _pad_cols(jnp.transpose(dck.reshape(H, T) + dcq.reshape(H, T)), LANES)
                dz, dbf = fgate_bwd(dc_full, shared["logits"], bf_pad, pad, name="f_dgate")
                big("w_kv", 0, ("mix", l), mm_tn(shared["xnk"], dkv, N_DEV, name="kv_wkv"), w_kv, m_w_kv, v_w_kv)
                dwf_t = mm_tn(dz, shared["xnk"], 1, name="f_wf", out_dtype=F32, tn_target=1024,
                              dep=take())[0, :H]
                dxn_f = mm_nt(dz, wf_pad, name="f_dxn")
                dxnk = mm_nt(dkv, W[("kv", 0)], add=dxn_f, name="kv_dxn")
                dh, dgkv = rms_bwd(dxnk, shared["h"], kv_norm, dh, name="kv_dnorm")
    land(None, dh)

    grad_x = dh[lead:][None]

    row8 = lambda a: _pad_rows(_pad_cols(a, D), 8)
    stack = lambda key, n: jnp.concatenate([small_g[(key, i)] for i in range(n)], axis=0)
    g_sharded = jnp.concatenate([dh[pad:lead], row8(stack("a_norm", n_a)), row8(stack("a_conv", n_a)), dwf_t], axis=0)
    g_repl = jnp.concatenate([row8(jnp.concatenate([dgkv, stack("b_norm", n_b)], axis=0)),
                              row8(stack("ffn_norm", depth)),
                              row8(jnp.concatenate([_pad_cols(dkn, D), _pad_cols(stack("b_q_norm", n_b), D),
                                                    _pad_cols(dbf[:, :H], D)], axis=0))], axis=0)
    n_sh = g_sharded.shape[0]
    gathered = all_gather(jnp.concatenate([g_sharded, g_repl], axis=0), name="ag_small_grads")
    parts_sh = lax.dynamic_slice_in_dim(gathered[:, :n_sh], my * Ds, Ds, axis=2)
    parts_rp = gathered[:, n_sh:]

    def pack_sh(t_meta, t_an, t_ac, t_wf):
        return jnp.concatenate([t_meta, _pad_rows(t_an, 8), _pad_rows(t_ac.reshape(n_a * 3, Ds), 8),
                                jnp.transpose(t_wf)], axis=0)

    def pack_rp(t_kv, t_bn, t_fn, t_kn, t_qn, t_bf):
        return jnp.concatenate([row8(jnp.concatenate([t_kv.reshape(1, D), t_bn], axis=0)), row8(t_fn),
                                row8(jnp.concatenate([_pad_cols(t_kn.reshape(1, -1), D), _pad_cols(t_qn, D),
                                                      _pad_cols(t_bf.reshape(1, -1), D)], axis=0))], axis=0)

    res_sh = adamw(parts_sh, pack_sh(meta, a_norm, a_conv, w_f), pack_sh(m_meta, m_a_norm, m_a_conv, m_w_f),
                   pack_sh(v_meta, v_a_norm, v_a_conv, v_w_f), name="small_sharded_adamw")
    res_rp = adamw(parts_rp, pack_rp(kv_norm, b_norm, ffn_norm, k_norm, b_q_norm, b_f),
                   pack_rp(m_kv_norm, m_b_norm, m_ffn_norm, m_k_norm, m_b_q_norm, m_b_f),
                   pack_rp(v_kv_norm, v_b_norm, v_ffn_norm, v_k_norm, v_b_q_norm, v_b_f), name="small_repl_adamw")

    def unpack(kind):
        sh, rp = res_sh[kind], res_rp[kind]
        out = {
            "meta": sh[0:n_meta],
            "a_norm": sh[r_an:r_an + n_a],
            "a_conv": sh[r_ac:r_ac + 3 * n_a].reshape(n_a, 3, Ds),
            "w_f": jnp.transpose(sh[r_wf:r_wf + H]),
            "kv_norm": rp[0],
            "b_norm": rp[1:1 + n_b],
            "ffn_norm": rp[8:8 + depth],
            "k_norm": rp[16, :HEAD_DIM],
            "b_q_norm": rp[17:17 + n_b, :HEAD_DIM],
            "b_f": rp[17 + n_b, :H],
        }
        for name, n in (("a_w_in", n_a), ("a_w_out", n_a), ("b_w_q", n_b), ("b_w_o", n_b),
                        ("ffn_w_gu", depth), ("ffn_w_down", depth)):
            out[name] = jnp.stack([upd[name][i][kind] for i in range(n)], axis=0)
        out["w_kv"] = upd["w_kv"][0][kind]
        return out

    order = ["meta", "a_norm", "a_w_in", "a_conv", "a_w_out", "kv_norm", "w_kv", "k_norm", "w_f", "b_f",
             "b_norm", "b_w_q", "b_q_norm", "b_w_o", "ffn_norm", "ffn_w_gu", "ffn_w_down"]
    outs = [loss, grad_x]
    for kind in range(4):
        vals = unpack(kind)
        outs += [vals[n] for n in order]
    return tuple(outs)
```

```python
import functools
import math

import jax
import jax.numpy as jnp
from jax import lax
from jax.experimental import pallas as pl
from jax.experimental.pallas import tpu as pltpu

N_DEV = 8
MESH_AXES = ("x", "y", "c")
EPS = 1e-6
NEG = -1e30
HEAD_DIM = 128
BLOCK = 128
LANES = 128
V7X_VMEM_LIMIT = 56 * 1024 * 1024

ADAM_LR = 0.001
ADAM_B1 = 0.9
ADAM_B2 = 0.999
ADAM_EPS = 1e-08
ADAM_WD = 0.01
ADAM_STEP = 10

BF = jnp.bfloat16
F32 = jnp.float32


def _tile(n, target, mult):
    best = None
    for t in range(mult, min(n, target) + 1, mult):
        if n % t == 0:
            best = t
    return n if best is None else best


def _params(*sem):
    return pltpu.CompilerParams(dimension_semantics=sem, vmem_limit_bytes=V7X_VMEM_LIMIT)


def _with_dep(body, in_specs, args, dep):
    if dep is None:
        return body, list(in_specs), list(args)
    n_in = len(args)

    def body_dep(*refs):
        body(*refs[:n_in], *refs[n_in + 1:])

    return body_dep, list(in_specs) + [pl.BlockSpec(memory_space=pl.ANY)], list(args) + [dep]


def mm_nn(a, w, *, name, add=None, dep=None, out_dtype=F32, tm_target=1056, tn_target=1024, tk_target=2048):
    M, K = a.shape
    G, K2, n = w.shape
    assert K == K2
    tm = _tile(M, tm_target, 16)
    tn = _tile(n, tn_target, LANES)
    tk = _tile(K, tk_target, LANES)
    nj, nk = n // tn, K // tk
    has_add = add is not None

    def body(*refs):
        if has_add:
            a_ref, w_ref, add_ref, o_ref = refs[:4]
        else:
            a_ref, w_ref, o_ref = refs[:3]
            add_ref = None

        def finish(r):
            if has_add:
                r = r + add_ref[...]
            o_ref[...] = r.astype(out_dtype)

        part = jnp.dot(a_ref[...], w_ref[...], preferred_element_type=F32)
        if nk == 1:
            finish(part)
        else:
            acc_ref = refs[-1]
            k = pl.program_id(2)

            @pl.when(k == 0)
            def _():
                acc_ref[...] = part

            @pl.when(k > 0)
            def _():
                acc_ref[...] += part

            @pl.when(k == nk - 1)
            def _():
                finish(acc_ref[...])

    in_specs = [
        pl.BlockSpec((tm, tk), lambda i, j, k: (i, k)),
        pl.BlockSpec((None, tk, tn), lambda i, j, k: (j // nj, k, j % nj)),
    ]
    args = [a, w]
    if has_add:
        in_specs.append(pl.BlockSpec((tm, tn), lambda i, j, k: (i, j)))
        args.append(add)
    body, in_specs, args = _with_dep(body, in_specs, args, dep)
    return pl.pallas_call(
        body,
        out_shape=jax.ShapeDtypeStruct((M, G * n), out_dtype),
        grid=(M // tm, G * nj, nk),
        in_specs=in_specs,
        out_specs=pl.BlockSpec((tm, tn), lambda i, j, k: (i, j)),
        scratch_shapes=[pltpu.VMEM((tm, tn), F32)] if nk > 1 else [],
        compiler_params=_params("parallel", "parallel", "arbitrary"),
        name=name,
    )(*args)


def mm_swiglu(xn, wgu, *, name, dep=None, save_dtype=BF, tm_target=528):
    M, K = xn.shape
    G, _, n = wgu.shape
    half = G // 2
    tm = _tile(M, tm_target, 16)
    tn = _tile(n, 1408, LANES)
    nj = n // tn
    Fh = half * n

    def body(a_ref, wg_ref, wu_ref, act_ref, g_ref, u_ref):
        a = a_ref[...]
        g = jnp.dot(a, wg_ref[...], preferred_element_type=F32)
        g_ref[...] = g.astype(save_dtype)
        silu = g * jax.nn.sigmoid(g)
        u = jnp.dot(a, wu_ref[...], preferred_element_type=F32)
        u_ref[...] = u.astype(save_dtype)
        act_ref[...] = (silu * u).astype(BF)

    out_block = pl.BlockSpec((tm, tn), lambda j, i: (i, j))
    once = pl.Buffered(1)
    body, in_specs, args = _with_dep(body, [
        pl.BlockSpec((tm, K), lambda j, i: (i, 0)),
        pl.BlockSpec((None, K, tn), lambda j, i: (j // nj, 0, j % nj), pipeline_mode=once),
        pl.BlockSpec((None, K, tn), lambda j, i: (half + j // nj, 0, j % nj), pipeline_mode=once),
    ], [xn, wgu, wgu], dep)
    return pl.pallas_call(
        body,
        out_shape=(jax.ShapeDtypeStruct((M, Fh), BF),
                   jax.ShapeDtypeStruct((M, Fh), save_dtype),
                   jax.ShapeDtypeStruct((M, Fh), save_dtype)),
        grid=(half * nj, M // tm),
        in_specs=in_specs,
        out_specs=(out_block, out_block, out_block),
        compiler_params=_params("parallel", "parallel"),
        name=name,
    )(*args)


def mm_nt(dy, w, *, name, add=None, dep=None, out_dtype=F32, tm_target=1056, tko_target=1024, tc_target=1408):
    M, N = dy.shape
    G, K, n = w.shape
    assert N == G * n
    tm = _tile(M, tm_target, 16)
    tko = _tile(K, tko_target, LANES)
    tc = _tile(n, tc_target, LANES)
    nc = n // tc
    steps = G * nc
    has_add = add is not None

    def body(*refs):
        if has_add:
            dy_ref, w_ref, add_ref, o_ref = refs[:4]
        else:
            dy_ref, w_ref, o_ref = refs[:3]
            add_ref = None

        def finish(r):
            if has_add:
                r = r + add_ref[...]
            o_ref[...] = r.astype(out_dtype)

        part = lax.dot_general(dy_ref[...], w_ref[...], (((1,), (1,)), ((), ())),
                               preferred_element_type=F32)
        if steps == 1:
            finish(part)
        else:
            acc_ref = refs[-1]
            s = pl.program_id(2)

            @pl.when(s == 0)
            def _():
                acc_ref[...] = part

            @pl.when(s > 0)
            def _():
                acc_ref[...] += part

            @pl.when(s == steps - 1)
            def _():
                finish(acc_ref[...])

    in_specs = [
        pl.BlockSpec((tm, tc), lambda i, o, s: (i, s)),
        pl.BlockSpec((None, tko, tc), lambda i, o, s: (s // nc, o, s % nc)),
    ]
    args = [dy, w]
    if has_add:
        in_specs.append(pl.BlockSpec((tm, tko), lambda i, o, s: (i, o)))
        args.append(add)
    body, in_specs, args = _with_dep(body, in_specs, args, dep)
    return pl.pallas_call(
        body,
        out_shape=jax.ShapeDtypeStruct((M, K), out_dtype),
        grid=(M // tm, K // tko, steps),
        in_specs=in_specs,
        out_specs=pl.BlockSpec((tm, tko), lambda i, o, s: (i, o)),
        scratch_shapes=[pltpu.VMEM((tm, tko), F32)] if steps > 1 else [],
        compiler_params=_params("parallel", "parallel", "arbitrary"),
        name=name,
    )(*args)


def mm_nt_dswiglu(dh, w_down, g_s, u_s, *, name, tm_target=1056, tf_target=512):
    M, D = dh.shape
    _, Fh, D2 = w_down.shape
    assert D == D2
    tm = _tile(M, tm_target, 16)
    tf = _tile(Fh, tf_target, LANES)

    def body(dh_ref, w_ref, g_ref, u_ref, dg_ref, du_ref):
        dact = lax.dot_general(dh_ref[...], w_ref[...], (((1,), (1,)), ((), ())),
                               preferred_element_type=F32)
        g = g_ref[...].astype(F32)
        u = u_ref[...].astype(F32)
        sig = jax.nn.sigmoid(g)
        du_ref[...] = (dact * (g * sig)).astype(BF)
        dg_ref[...] = (dact * u * (sig * (1.0 + g * (1.0 - sig)))).astype(BF)

    blk = pl.BlockSpec((tm, tf), lambda i, f: (i, f))
    return pl.pallas_call(
        body,
        out_shape=(jax.ShapeDtypeStruct((M, Fh), BF), jax.ShapeDtypeStruct((M, Fh), BF)),
        grid=(M // tm, Fh // tf),
        in_specs=[
            pl.BlockSpec((tm, D), lambda i, f: (i, 0)),
            pl.BlockSpec((None, tf, D), lambda i, f: (0, f, 0)),
            blk, blk,
        ],
        out_specs=(blk, blk),
        compiler_params=_params("parallel", "parallel"),
        name=name,
    )(dh, w_down, g_s, u_s)


def mm_tn(a, dy, groups, *, name, dep=None, out_dtype=BF, tk_target=512, tn_target=1408):
    M, K = a.shape
    M2, N = dy.shape
    assert M == M2 and N % groups == 0
    n = N // groups
    tk = _tile(K, tk_target, LANES)
    tn = _tile(n, tn_target, LANES)
    nj = n // tn

    def body(a_ref, dy_ref, o_ref):
        o_ref[...] = lax.dot_general(a_ref[...], dy_ref[...], (((0,), (0,)), ((), ())),
                                     preferred_element_type=F32).astype(out_dtype)

    body, in_specs, args = _with_dep(body, [
        pl.BlockSpec((M, tk), lambda i, j: (0, i)),
        pl.BlockSpec((M, tn), lambda i, j: (0, j)),
    ], [a, dy], dep)
    return pl.pallas_call(
        body,
        out_shape=jax.ShapeDtypeStruct((groups, K, n), out_dtype),
        grid=(K // tk, groups * nj),
        in_specs=in_specs,
        out_specs=pl.BlockSpec((None, tk, tn), lambda i, j: (j // nj, i, j % nj)),
        compiler_params=_params("parallel", "parallel"),
        name=name,
    )(*args)


def rms_fwd(h, g, *, name, dep=None):
    T, D = h.shape
    tm = _tile(T, 528, 16)

    def body(h_ref, g_ref, o_ref):
        x = h_ref[...]
        r = lax.rsqrt(jnp.mean(x * x, axis=-1, keepdims=True) + EPS)
        o_ref[...] = ((x * r) * g_ref[...]).astype(BF)

    body, in_specs, args = _with_dep(
        body, [pl.BlockSpec((tm, D), lambda i: (i, 0)), pl.BlockSpec((1, D), lambda i: (0, 0))],
        [h, g.reshape(1, D)], dep)
    return pl.pallas_call(
        body,
        out_shape=jax.ShapeDtypeStruct((T, D), BF),
        grid=(T // tm,),
        in_specs=in_specs,
        out_specs=pl.BlockSpec((tm, D), lambda i: (i, 0)),
        compiler_params=_params("parallel"),
        name=name,
    )(*args)


def rms_bwd(dxn, h, g, add, *, name):
    T, D = h.shape
    tm = _tile(T, 264, 16)

    def body(dxn_ref, h_ref, g_ref, add_ref, dh_ref, dhb_ref, dg_ref):
        x = h_ref[...]
        dy = dxn_ref[...]
        r = lax.rsqrt(jnp.mean(x * x, axis=-1, keepdims=True) + EPS)
        xhat = x * r
        part = jnp.sum(dy * xhat, axis=0, keepdims=True)

        @pl.when(pl.program_id(0) == 0)
        def _():
            dg_ref[...] = part

        @pl.when(pl.program_id(0) > 0)
        def _():
            dg_ref[...] += part

        dxh = dy * g_ref[...]
        dh = add_ref[...] + r * (dxh - xhat * jnp.mean(dxh * xhat, axis=-1, keepdims=True))
        dh_ref[...] = dh
        dhb_ref[...] = dh.astype(BF)

    row = pl.BlockSpec((tm, D), lambda i: (i, 0))
    vec = pl.BlockSpec((1, D), lambda i: (0, 0))
    return pl.pallas_call(
        body,
        out_shape=(jax.ShapeDtypeStruct((T, D), F32), jax.ShapeDtypeStruct((T, D), BF),
                   jax.ShapeDtypeStruct((1, D), F32)),
        grid=(T // tm,),
        in_specs=[row, row, vec, row],
        out_specs=(row, row, vec),
        compiler_params=_params("arbitrary"),
        name=name,
    )(dxn, h, g.reshape(1, D), add)


def _head_norm(x, gain):
    r = lax.rsqrt(jnp.mean(x * x, axis=-1, keepdims=True) + EPS)
    return (x * r) * gain


def hn_fwd(qraw, gain, out_scale, *, name):
    T, D = qraw.shape
    H = D // HEAD_DIM
    tm = _tile(T, 528, 16)

    def body(q_ref, g_ref, o_ref):
        gain_v = g_ref[...]
        for hd in range(H):
            sl = slice(hd * HEAD_DIM, (hd + 1) * HEAD_DIM)
            o_ref[:, sl] = (_head_norm(q_ref[:, sl], gain_v) * out_scale).astype(BF)

    return pl.pallas_call(
        body,
        out_shape=jax.ShapeDtypeStruct((T, D), BF),
        grid=(T // tm,),
        in_specs=[pl.BlockSpec((tm, D), lambda i: (i, 0)),
                  pl.BlockSpec((1, HEAD_DIM), lambda i: (0, 0))],
        out_specs=pl.BlockSpec((tm, D), lambda i: (i, 0)),
        compiler_params=_params("parallel"),
        name=name,
    )(qraw, gain.reshape(1, HEAD_DIM))


def kv_post(kv, gain, *, name):
    T, D2 = kv.shape
    D = D2 // 2
    H = D // HEAD_DIM
    tm = _tile(T, 528, 16)

    def body(k_ref, v_ref, g_ref, ko_ref, vo_ref):
        gain_v = g_ref[...]
        for hd in range(H):
            sl = slice(hd * HEAD_DIM, (hd + 1) * HEAD_DIM)
            ko_ref[:, sl] = _head_norm(k_ref[:, sl], gain_v).astype(BF)
        vo_ref[...] = v_ref[...].astype(BF)

    blk = pl.BlockSpec((tm, D), lambda i: (i, 0))
    return pl.pallas_call(
        body,
        out_shape=(jax.ShapeDtypeStruct((T, D), BF), jax.ShapeDtypeStruct((T, D), BF)),
        grid=(T // tm,),
        in_specs=[blk, pl.BlockSpec((tm, D), lambda i: (i, 1)),
                  pl.BlockSpec((1, HEAD_DIM), lambda i: (0, 0))],
        out_specs=(blk, blk),
        compiler_params=_params("parallel"),
        name=name,
    )(kv, kv, gain.reshape(1, HEAD_DIM))


def hn_bwd(dq, qraw, gain, out_scale, *, name):
    T, D = dq.shape
    H = D // HEAD_DIM
    tm = _tile(T, 264, 16)

    def body(dq_ref, q_ref, g_ref, o_ref, dg_ref):
        gain_v = g_ref[...]
        part = jnp.zeros((1, HEAD_DIM), F32)
        for hd in range(H):
            sl = slice(hd * HEAD_DIM, (hd + 1) * HEAD_DIM)
            x = q_ref[:, sl]
            dy = dq_ref[:, sl] * out_scale
            r = lax.rsqrt(jnp.mean(x * x, axis=-1, keepdims=True) + EPS)
            xhat = x * r
            part = part + jnp.sum(dy * xhat, axis=0, keepdims=True)
            dxh = dy * gain_v
            o_ref[:, sl] = (r * (dxh - xhat * jnp.mean(dxh * xhat, axis=-1, keepdims=True))).astype(BF)

        @pl.when(pl.program_id(0) == 0)
        def _():
            dg_ref[...] = part

        @pl.when(pl.program_id(0) > 0)
        def _():
            dg_ref[...] += part

    blk = pl.BlockSpec((tm, D), lambda i: (i, 0))
    vec = pl.BlockSpec((1, HEAD_DIM), lambda i: (0, 0))
    return pl.pallas_call(
        body,
        out_shape=(jax.ShapeDtypeStruct((T, D), BF), jax.ShapeDtypeStruct((1, HEAD_DIM), F32)),
        grid=(T // tm,),
        in_specs=[blk, blk, vec],
        out_specs=(blk, vec),
        compiler_params=_params("arbitrary"),
        name=name,
    )(dq, qraw, gain.reshape(1, HEAD_DIM))


def _shift_down(cur, above, k, rowc):
    out = pltpu.roll(cur, k, 0)
    for i in range(k):
        out = jnp.where(rowc == i, above[8 - k + i:8 - k + i + 1], out)
    return out


def _shift_up(cur, below, k, rowc):
    R = cur.shape[0]
    out = pltpu.roll(cur, R - k, 0)
    for i in range(k):
        out = jnp.where(rowc == R - k + i, below[i:i + 1], out)
    return out


def _conv3(u, u_above, wv, rowc):
    u1 = _shift_down(u, u_above, 1, rowc)
    u2 = _shift_down(u, u_above, 2, rowc)
    return wv[0:1] * u2 + wv[1:2] * u1 + wv[2:3] * u, u1, u2


def conv_fwd(proj, w, *, name):
    T, D3 = proj.shape
    D = D3 // 3
    tc = LANES if D % LANES == 0 else D
    nb = D // tc
    R = _tile(T, 264, 8)

    def body(b_ref, c_ref, h_ref, w_ref, y_ref):
        rowc = lax.broadcasted_iota(jnp.int32, (R, 1), 0)
        wv = w_ref[...]
        for r0 in range(0, T, R):
            rows = slice(r0, r0 + R)
            u = c_ref[rows, :] * h_ref[rows, :]
            if r0 == 0:
                above = jnp.zeros((8, tc), F32)
            else:
                above = c_ref[r0 - 8:r0, :] * h_ref[r0 - 8:r0, :]
            conv, _, _ = _conv3(u, above, wv, rowc)
            y_ref[rows, :] = (b_ref[rows, :] * conv).astype(BF)

    return pl.pallas_call(
        body,
        out_shape=jax.ShapeDtypeStruct((T, D), BF),
        grid=(nb,),
        in_specs=[
            pl.BlockSpec((T, tc), lambda j: (0, j)),
            pl.BlockSpec((T, tc), lambda j: (0, nb + j)),
            pl.BlockSpec((T, tc), lambda j: (0, 2 * nb + j)),
            pl.BlockSpec((3, tc), lambda j: (0, j)),
        ],
        out_specs=pl.BlockSpec((T, tc), lambda j: (0, j)),
        compiler_params=_params("parallel"),
        name=name,
    )(proj, proj, proj, w)


def conv_bwd(dy, proj, w, *, name, dep=None):
    T, D = dy.shape
    tc = LANES if D % LANES == 0 else D
    nb = D // tc
    R = _tile(T, 264, 8)

    def body(dy_ref, b_ref, c_ref, h_ref, w_ref, db_ref, dc_ref, dh_ref, dw_ref):
        rowc = lax.broadcasted_iota(jnp.int32, (R, 1), 0)
        wv = w_ref[...]
        dw = [jnp.zeros((1, tc), F32) for _ in range(3)]
        for r0 in range(0, T, R):
            rows = slice(r0, r0 + R)
            c = c_ref[rows, :]
            hh = h_ref[rows, :]
            u = c * hh
            if r0 == 0:
                above = jnp.zeros((8, tc), F32)
            else:
                above = c_ref[r0 - 8:r0, :] * h_ref[r0 - 8:r0, :]
            conv, u1, u2 = _conv3(u, above, wv, rowc)
            dyv = dy_ref[rows, :]
            db_ref[rows, :] = (dyv * conv).astype(BF)
            dconv = dyv * b_ref[rows, :]
            if r0 + R == T:
                below = jnp.zeros((8, tc), F32)
            else:
                below = dy_ref[r0 + R:r0 + R + 8, :] * b_ref[r0 + R:r0 + R + 8, :]
            dw[0] = dw[0] + jnp.sum(dconv * u2, axis=0, keepdims=True)
            dw[1] = dw[1] + jnp.sum(dconv * u1, axis=0, keepdims=True)
            dw[2] = dw[2] + jnp.sum(dconv * u, axis=0, keepdims=True)
            du = (wv[2:3] * dconv + wv[1:2] * _shift_up(dconv, below, 1, rowc)
                  + wv[0:1] * _shift_up(dconv, below, 2, rowc))
            dc_ref[rows, :] = (du * hh).astype(BF)
            dh_ref[rows, :] = (du * c).astype(BF)
        for i in range(3):
            dw_ref[i:i + 1, :] = dw[i]

    strip = pl.BlockSpec((T, tc), lambda j: (0, j))
    wblk = pl.BlockSpec((3, tc), lambda j: (0, j))
    out = jax.ShapeDtypeStruct((T, D), BF)
    body, in_specs, args = _with_dep(body, [
        strip,
        pl.BlockSpec((T, tc), lambda j: (0, j)),
        pl.BlockSpec((T, tc), lambda j: (0, nb + j)),
        pl.BlockSpec((T, tc), lambda j: (0, 2 * nb + j)),
        wblk,
    ], [dy, proj, proj, proj, w], dep)
    return pl.pallas_call(
        body,
        out_shape=(out, out, out, jax.ShapeDtypeStruct((3, D), F32)),
        grid=(nb,),
        in_specs=in_specs,
        out_specs=(strip, strip, strip, wblk),
        compiler_params=_params("parallel"),
        name=name,
    )(*args)


def _log_sigmoid(z):
    return jnp.minimum(z, 0.0) - jnp.log(1.0 + jnp.exp(-jnp.abs(z)))


def fgate_fwd(logits, bias, pad, *, name):
    T, W = logits.shape
    cb = _tile(T, 128, 8)
    nblk = T // cb

    def body(z_ref, b_ref, c_ref, lf_ref):
        row = lax.broadcasted_iota(jnp.int32, (T, 1), 0)
        lf_ref[...] = jnp.where(row >= pad, _log_sigmoid(z_ref[...] + b_ref[...]), 0.0)
        ri = lax.broadcasted_iota(jnp.int32, (cb, cb), 0)
        ci = lax.broadcasted_iota(jnp.int32, (cb, cb), 1)
        tri = (ci <= ri).astype(F32)

        def step(i, carry):
            rows = pl.ds(pl.multiple_of(i * cb, cb), cb)
            blk = lf_ref[rows, :]
            c_ref[rows, :] = carry + jnp.dot(tri, blk, precision=lax.Precision.HIGHEST,
                                             preferred_element_type=F32)
            return carry + jnp.sum(blk, axis=0, keepdims=True)

        lax.fori_loop(0, nblk, step, jnp.zeros((1, W), F32))

    return pl.pallas_call(
        body,
        out_shape=jax.ShapeDtypeStruct((T, W), F32),
        in_specs=[pl.BlockSpec(memory_space=pltpu.VMEM), pl.BlockSpec(memory_space=pltpu.VMEM)],
        out_specs=pl.BlockSpec(memory_space=pltpu.VMEM),
        scratch_shapes=[pltpu.VMEM((T, W), F32)],
        compiler_params=pltpu.CompilerParams(vmem_limit_bytes=V7X_VMEM_LIMIT),
        name=name,
    )(logits, bias)


def fgate_bwd(dc, logits, bias, pad, *, name):
    T, W = logits.shape
    cb = _tile(T, 128, 8)
    nblk = T // cb

    def body(dc_ref, z_ref, b_ref, dz_ref, db_ref, rs_ref):
        ri = lax.broadcasted_iota(jnp.int32, (cb, cb), 0)
        ci = lax.broadcasted_iota(jnp.int32, (cb, cb), 1)
        triu = (ci >= ri).astype(F32)

        def step(i, carry):
            rows = pl.ds(pl.multiple_of((nblk - 1 - i) * cb, cb), cb)
            blk = dc_ref[rows, :]
            rs_ref[rows, :] = carry + jnp.dot(triu, blk, precision=lax.Precision.HIGHEST,
                                              preferred_element_type=F32)
            return carry + jnp.sum(blk, axis=0, keepdims=True)

        lax.fori_loop(0, nblk, step, jnp.zeros((1, W), F32))
        row = lax.broadcasted_iota(jnp.int32, (T, 1), 0)
        z = z_ref[...] + b_ref[...]
        dz = jnp.where(row >= pad, rs_ref[...] * jax.nn.sigmoid(-z), 0.0)
        dz_ref[...] = dz.astype(BF)
        db_ref[...] = jnp.sum(dz, axis=0, keepdims=True)

    vm = pl.BlockSpec(memory_space=pltpu.VMEM)
    return pl.pallas_call(
        body,
        out_shape=(jax.ShapeDtypeStruct((T, W), BF), jax.ShapeDtypeStruct((1, W), F32)),
        in_specs=[vm, vm, vm],
        out_specs=(vm, vm),
        scratch_shapes=[pltpu.VMEM((T, W), F32)],
        compiler_params=pltpu.CompilerParams(vmem_limit_bytes=V7X_VMEM_LIMIT),
        name=name,
    )(dc, logits, bias)


def _scores(qb, kb, ck):
    return lax.dot_general(qb, kb, (((1,), (1,)), ((), ())), preferred_element_type=F32) - ck


def _causal(s, row, col, pad):
    return jnp.where((col <= row) & (col >= pad), s, NEG)


def attn_fwd(q, k, v, crow, pad, *, name):
    T, D = q.shape
    H = D // HEAD_DIM
    nk, tk = crow.shape[1], crow.shape[3]
    tq = tk
    nq = T // tq

    sub = tq

    def body(q_ref, k_ref, v_ref, cr_ref, o_ref, lse_ref):
        qi = pl.program_id(1)

        def rows_pass(r0):
            qb = q_ref[r0:r0 + sub, :]
            row = qi * tq + r0 + lax.broadcasted_iota(jnp.int32, (sub, 1), 0)

            def step(kc, carry, masked):
                m, l, acc = carry
                rows = pl.ds(pl.multiple_of(kc * tk, tk), tk)
                s = _scores(qb, k_ref[rows, :], cr_ref[kc])
                if masked:
                    s = _causal(s, row, kc * tk + lax.broadcasted_iota(jnp.int32, (1, tk), 1), pad)
                m_new = jnp.maximum(m, jnp.max(s, axis=-1, keepdims=True))
                alpha = jnp.exp(m - m_new)
                p = jnp.exp(s - m_new)
                l = alpha * l + jnp.sum(p, axis=-1, keepdims=True)
                acc = alpha * acc + jnp.dot(p.astype(BF), v_ref[rows, :], preferred_element_type=F32)
                return m_new, l, acc

            init = (jnp.full((sub, 1), NEG, F32), jnp.zeros((sub, 1), F32), jnp.zeros((sub, HEAD_DIM), F32))
            carry = step(0, init, True)
            carry = lax.fori_loop(1, qi, lambda kc, c: step(kc, c, False), carry)
            m, l, acc = lax.cond(qi > 0, lambda c: step(qi, c, True), lambda c: c, carry)
            valid = row >= pad
            o_ref[r0:r0 + sub, :] = jnp.where(valid, acc / l, 0.0).astype(BF)
            lse_ref[r0:r0 + sub, :] = jnp.where(valid, m + jnp.log(l), 0.0)

        for r0 in range(0, tq, sub):
            rows_pass(r0)

    return pl.pallas_call(
        body,
        out_shape=(jax.ShapeDtypeStruct((T, D), BF), jax.ShapeDtypeStruct((H, T, 1), F32)),
        grid=(H, nq),
        in_specs=[
            pl.BlockSpec((tq, HEAD_DIM), lambda h, i: (i, h)),
            pl.BlockSpec((T, HEAD_DIM), lambda h, i: (0, h)),
            pl.BlockSpec((T, HEAD_DIM), lambda h, i: (0, h)),
            pl.BlockSpec((None, nk, 1, tk), lambda h, i: (h, 0, 0, 0)),
        ],
        out_specs=(pl.BlockSpec((tq, HEAD_DIM), lambda h, i: (i, h)),
                   pl.BlockSpec((None, tq, 1), lambda h, i: (h, i, 0))),
        compiler_params=_params("parallel", "arbitrary"),
        name=name,
    )(q, k, v, crow)


def attn_bwd(q, k, v, do, o, lse, crow, prev, pad, *, name, dep=None):
    T, D = q.shape
    H = D // HEAD_DIM
    nk, tk = crow.shape[1], crow.shape[3]
    tq = tk
    nq = T // tq
    has_prev = prev is not None

    def body(*refs):
        q_ref, k_ref, v_ref, do_ref, o_ref, lse_ref, cr_ref = refs[:7]
        refs = refs[7:]
        if has_prev:
            pk_ref, pv_ref, pc_ref, pq_ref = refs[:4]
            refs = refs[4:]
        dq_ref, dk_ref, dv_ref, dck_ref, dcq_ref, delta_ref = refs
        kc = pl.program_id(1)

        @pl.when(kc == 0)
        def _():
            dq_ref[...] = jnp.zeros_like(dq_ref)
            dcq_ref[...] = pq_ref[...] if has_prev else jnp.zeros_like(dcq_ref)
            do_used = do_ref[...].astype(BF).astype(F32)
            delta_ref[...] = jnp.sum(do_used * o_ref[...].astype(F32), axis=-1, keepdims=True)

        kb = k_ref[...]
        vb = v_ref[...]
        ck = cr_ref[...]
        col = kc * tk + lax.broadcasted_iota(jnp.int32, (1, tk), 1)

        def step(qi, carry, masked):
            dk, dv, dck = carry
            rows = pl.ds(pl.multiple_of(qi * tq, tq), tq)
            qb = q_ref[rows, :]
            dob = do_ref[rows, :].astype(BF)
            s = _scores(qb, kb, ck)
            if masked:
                s = _causal(s, qi * tq + lax.broadcasted_iota(jnp.int32, (tq, 1), 0), col, pad)
            p = jnp.exp(s - lse_ref[rows, :])
            dp = lax.dot_general(dob, vb, (((1,), (1,)), ((), ())), preferred_element_type=F32)
            ds = p * (dp - delta_ref[rows, :])
            dsb = ds.astype(BF)
            dv = dv + lax.dot_general(p.astype(BF), dob, (((0,), (0,)), ((), ())),
                                      preferred_element_type=F32)
            dk = dk + lax.dot_general(dsb, qb, (((0,), (0,)), ((), ())), preferred_element_type=F32)
            dq_ref[rows, :] += jnp.dot(dsb, kb, preferred_element_type=F32)
            dcq_ref[rows, :] += jnp.sum(ds, axis=1, keepdims=True)
            dck = dck - jnp.sum(ds, axis=0, keepdims=True)
            return dk, dv, dck

        def rest(masked):
            return lambda c: lax.fori_loop(kc + 1, nq, lambda qi, cc: step(qi, cc, masked), c)

        init = (jnp.zeros((tk, HEAD_DIM), F32), jnp.zeros((tk, HEAD_DIM), F32), jnp.zeros((1, tk), F32))
        carry = step(kc, init, True)
        dk, dv, dck = lax.cond(kc == 0, rest(True), rest(False), carry)
        if has_prev:
            dk = dk + pk_ref[...]
            dv = dv + pv_ref[...]
            dck = dck + pc_ref[...]
        dk_ref[...] = dk
        dv_ref[...] = dv
        dck_ref[...] = dck

    head_all = pl.BlockSpec((T, HEAD_DIM), lambda h, j: (0, h))
    head_blk = pl.BlockSpec((tk, HEAD_DIM), lambda h, j: (j, h))
    col_all = pl.BlockSpec((None, T, 1), lambda h, j: (h, 0, 0))
    row_blk = pl.BlockSpec((None, None, 1, tk), lambda h, j: (h, j, 0, 0))
    in_specs = [head_all, head_blk, head_blk, head_all, head_all, col_all, row_blk]
    args = [q, k, v, do, o, lse, crow]
    if has_prev:
        in_specs += [head_blk, head_blk, row_blk, col_all]
        args += list(prev)
    body, in_specs, args = _with_dep(body, in_specs, args, dep)
    return pl.pallas_call(
        body,
        out_shape=(jax.ShapeDtypeStruct((T, D), F32), jax.ShapeDtypeStruct((T, D), F32),
                   jax.ShapeDtypeStruct((T, D), F32), jax.ShapeDtypeStruct((H, nk, 1, tk), F32),
                   jax.ShapeDtypeStruct((H, T, 1), F32)),
        grid=(H, nk),
        in_specs=in_specs,
        out_specs=(head_all, head_blk, head_blk, row_blk, col_all),
        scratch_shapes=[pltpu.VMEM((T, 1), F32)],
        compiler_params=_params("parallel", "arbitrary"),
        name=name,
    )(*args)


def loss_head(h, target, lead, *, name):
    T, D = h.shape
    tm = lead
    assert T % tm == 0 and target.shape[0] % tm == 0
    inv_d = 1.0 / D

    def body(h_ref, t_ref, dh_ref, dhb_ref, loss_ref):
        i = pl.program_id(0)

        @pl.when(i == 0)
        def _():
            dh_ref[...] = jnp.zeros_like(dh_ref)
            dhb_ref[...] = jnp.zeros_like(dhb_ref)
            loss_ref[...] = jnp.zeros_like(loss_ref)

        @pl.when(i > 0)
        def _():
            e = h_ref[...] - t_ref[...]
            dh = e * inv_d
            dh_ref[...] = dh
            dhb_ref[...] = dh.astype(BF)
            loss_ref[...] += 0.5 * inv_d * jnp.sum(e * e)

    return pl.pallas_call(
        body,
        out_shape=(jax.ShapeDtypeStruct((T, D), F32), jax.ShapeDtypeStruct((T, D), BF),
                   jax.ShapeDtypeStruct((8, LANES), F32)),
        grid=(T // tm,),
        in_specs=[pl.BlockSpec((tm, D), lambda i: (i, 0)),
                  pl.BlockSpec((tm, D), lambda i: (jnp.maximum(i - 1, 0), 0))],
        out_specs=(pl.BlockSpec((tm, D), lambda i: (i, 0)), pl.BlockSpec((tm, D), lambda i: (i, 0)),
                   pl.BlockSpec((8, LANES), lambda i: (0, 0))),
        compiler_params=_params("arbitrary"),
        name=name,
    )(h, target)


def adamw(parts, w, m, v, *, name):
    P, R, C = parts.shape
    tr = _tile(R, max(16, (128 * 1024) // C), 16)

    def body(p_ref, w_ref, m_ref, v_ref, g_ref, d_ref, mo_ref, vo_ref):
        g = p_ref[0].astype(F32)
        for i in range(1, P):
            g = g + p_ref[i].astype(F32)
        m_new = ADAM_B1 * m_ref[...] + (1.0 - ADAM_B1) * g
        v_new = ADAM_B2 * v_ref[...] + (1.0 - ADAM_B2) * jnp.square(g)
        m_hat = m_new / (1.0 - ADAM_B1 ** ADAM_STEP)
        v_hat = v_new / (1.0 - ADAM_B2 ** ADAM_STEP)
        g_ref[...] = g
        d_ref[...] = -ADAM_LR * (m_hat / (jnp.sqrt(v_hat) + ADAM_EPS) + ADAM_WD * w_ref[...])
        mo_ref[...] = m_new
        vo_ref[...] = v_new

    blk = pl.BlockSpec((tr, C), lambda i: (i, 0))
    out = jax.ShapeDtypeStruct((R, C), F32)
    return pl.pallas_call(
        body,
        out_shape=(out, out, out, out),
        grid=(R // tr,),
        in_specs=[pl.BlockSpec((P, tr, C), lambda i: (0, i, 0)), blk, blk, blk],
        out_specs=(blk, blk, blk, blk),
        compiler_params=_params("parallel"),
        name=name,
    )(parts, w, m, v)


def _flip(v, bit):
    return 1 - v if bit else v


def all_gather(shard, *, name):
    def body(x_ref, out_ref, send_sems, recv_sems, local_sem):
        x, y, c = lax.axis_index("x"), lax.axis_index("y"), lax.axis_index("c")
        me, sibling = (x, y, c), (x, y, 1 - c)
        chips = [(1 - x, y), (x, 1 - y), (1 - x, 1 - y)]

        def block(px, py, pc):
            return out_ref.at[4 * px + 2 * py + pc]

        def copy(k, blk, to, src=None):
            return pltpu.make_async_remote_copy(
                src_ref=block(*blk) if src is None else src,
                dst_ref=block(*blk),
                send_sem=send_sems.at[k],
                recv_sem=recv_sems.at[k],
                device_id=to,
                device_id_type=pl.DeviceIdType.MESH,
            )

        mine = pltpu.make_async_copy(x_ref, block(*me), local_sem)
        mine.start()
        first = [copy(0, me, sibling, src=x_ref)]
        first += [copy(1 + j, me, (*chip, c), src=x_ref) for j, chip in enumerate(chips)]
        for cp in first:
            cp.start()
        passed = [copy(4 + j, (*chip, c), sibling) for j, chip in enumerate(chips)]
        for j, chip in enumerate(chips):
            copy(1 + j, (*chip, c), me).wait_recv()
            passed[j].start()
        copy(0, sibling, me).wait_recv()
        for j, chip in enumerate(chips):
            copy(4 + j, (*chip, 1 - c), me).wait_recv()
        for cp in first + passed:
            cp.wait_send()
        mine.wait()

    return pl.pallas_call(
        body,
        out_shape=jax.ShapeDtypeStruct((N_DEV,) + shard.shape, shard.dtype),
        in_specs=[pl.BlockSpec(memory_space=pl.ANY)],
        out_specs=pl.BlockSpec(memory_space=pl.ANY),
        scratch_shapes=[pltpu.SemaphoreType.DMA((7,)), pltpu.SemaphoreType.DMA((7,)),
                        pltpu.SemaphoreType.DMA],
        name=name,
    )(shard)


def exchange_slabs(slabs, *, name):
    def body(g_ref, r_ref, send_sems, recv_sems, local_sem):
        x, y, c = lax.axis_index("x"), lax.axis_index("y"), lax.axis_index("c")
        me = 4 * x + 2 * y + c
        mine = pltpu.make_async_copy(g_ref.at[me], r_ref.at[me], local_sem)
        mine.start()
        sends, recvs = [], []
        for k in range(1, N_DEV):
            px, py, pc = _flip(x, (k >> 2) & 1), _flip(y, (k >> 1) & 1), _flip(c, k & 1)
            peer = 4 * px + 2 * py + pc
            sends.append(pltpu.make_async_remote_copy(
                src_ref=g_ref.at[peer], dst_ref=r_ref.at[me],
                send_sem=send_sems.at[k - 1], recv_sem=recv_sems.at[k - 1],
                device_id=(px, py, pc), device_id_type=pl.DeviceIdType.MESH))
            recvs.append(pltpu.make_async_remote_copy(
                src_ref=g_ref.at[peer], dst_ref=r_ref.at[peer],
                send_sem=send_sems.at[k - 1], recv_sem=recv_sems.at[k - 1],
                device_id=(px, py, pc), device_id_type=pl.DeviceIdType.MESH))
        for cp in sends:
            cp.start()
        for cp in recvs:
            cp.wait_recv()
        for cp in sends:
            cp.wait_send()
        mine.wait()

    return pl.pallas_call(
        body,
        out_shape=jax.ShapeDtypeStruct(slabs.shape, slabs.dtype),
        in_specs=[pl.BlockSpec(memory_space=pl.ANY)],
        out_specs=pl.BlockSpec(memory_space=pl.ANY),
        scratch_shapes=[pltpu.SemaphoreType.DMA((7,)), pltpu.SemaphoreType.DMA((7,)),
                        pltpu.SemaphoreType.DMA],
        name=name,
    )(slabs)


def reduce_adamw(slabs, w, m, v, *, name):
    got = exchange_slabs(slabs, name=name + "_xchg")
    return adamw(got, w, m, v, name=name + "_adamw")


_HBM = pl.BlockSpec(memory_space=pltpu.HBM)
_SEM = pl.BlockSpec(memory_space=pltpu.SEMAPHORE)
_ANY = pl.BlockSpec(memory_space=pl.ANY)
_EFFECT = pltpu.SideEffectType.DATAFLOW_SIDE_EFFECTING
_N_FIRST = 4


def _first_copies(land_ref, send_sems, recv_sems):
    x, y, c = lax.axis_index("x"), lax.axis_index("y"), lax.axis_index("c")
    mine = land_ref.at[4 * x + 2 * y + c]
    targets = [(x, y, 1 - c), (1 - x, y, c), (x, 1 - y, c), (1 - x, 1 - y, c)]
    sends, recvs = [], []
    for k, (px, py, pc) in enumerate(targets):
        common = dict(send_sem=send_sems.at[k], recv_sem=recv_sems.at[k], device_id=(px, py, pc),
                      device_id_type=pl.DeviceIdType.MESH)
        sends.append(pltpu.make_async_remote_copy(src_ref=mine, dst_ref=mine, **common))
        theirs = land_ref.at[4 * px + 2 * py + pc]
        recvs.append(pltpu.make_async_remote_copy(src_ref=theirs, dst_ref=theirs, **common))
    return sends, recvs


def _second_copies(land_ref, send_sems, recv_sems):
    x, y, c = lax.axis_index("x"), lax.axis_index("y"), lax.axis_index("c")
    sends, recvs = [], []
    for j, (px, py) in enumerate([(1 - x, y), (x, 1 - y), (1 - x, 1 - y)]):
        common = dict(send_sem=send_sems.at[j], recv_sem=recv_sems.at[j], device_id=(x, y, 1 - c),
                      device_id_type=pl.DeviceIdType.MESH)
        blk = land_ref.at[4 * px + 2 * py + c]
        sends.append(pltpu.make_async_remote_copy(src_ref=blk, dst_ref=blk, **common))
        got = land_ref.at[4 * px + 2 * py + (1 - c)]
        recvs.append(pltpu.make_async_remote_copy(src_ref=got, dst_ref=got, **common))
    return sends, recvs


def gather_start(shard, me, after, *, name):
    R, C = shard.shape
    tr = _tile(R, max(16, (512 * 1024) // C), 16)

    def place_body(me_ref, x_ref, o_ref):
        o_ref[...] = x_ref[...].astype(BF)

    land = pl.pallas_call(
        place_body, name=name + "_own",
        out_shape=jax.ShapeDtypeStruct((N_DEV, R, C), BF),
        grid_spec=pltpu.PrefetchScalarGridSpec(
            num_scalar_prefetch=1,
            grid=(R // tr,),
            in_specs=[pl.BlockSpec((tr, C), lambda i, me_ref: (i, 0))],
            out_specs=pl.BlockSpec((None, tr, C), lambda i, me_ref: (me_ref[0], i, 0)),
        ),
        compiler_params=_params("parallel"),
    )(me.reshape(1).astype(jnp.int32), shard)

    def body(land_ref, after_ref, send_sems, recv_sems, land_thru, token):
        sends, _ = _first_copies(land_ref, send_sems, recv_sems)
        for cp in sends:
            cp.start()
        token[...] = jnp.zeros_like(token)

    send_sems, recv_sems, land_thru, token = pl.pallas_call(
        body, name=name + "_s1",
        out_shape=(pltpu.SemaphoreType.DMA((_N_FIRST,)), pltpu.SemaphoreType.DMA((_N_FIRST,)),
                   pltpu.HBM(land.shape, land.dtype), jax.ShapeDtypeStruct((8, LANES), F32)),
        in_specs=(_HBM, _ANY),
        out_specs=(_SEM, _SEM, _HBM, pl.BlockSpec(memory_space=pltpu.VMEM)),
        input_output_aliases={0: 2},
        compiler_params=pltpu.CompilerParams(has_side_effects=_EFFECT),
    )(pltpu.with_memory_space_constraint(land, pltpu.HBM), after)
    return (send_sems, recv_sems, land_thru), token


def gather_mid(handle, after, *, name):
    send_sems, recv_sems, land_thru = handle

    def body(land_ref, send1, recv1, after_ref, send2, recv2, land_out, token):
        sends, recvs = _first_copies(land_ref, send1, recv1)
        for cp in sends:
            cp.wait_send()
        for cp in recvs:
            cp.wait_recv()
        seconds, _ = _second_copies(land_ref, send2, recv2)
        for cp in seconds:
            cp.start()
        token[...] = jnp.zeros_like(token)

    send2, recv2, land2, token = pl.pallas_call(
        body, name=name + "_s2",
        out_shape=(pltpu.SemaphoreType.DMA((3,)), pltpu.SemaphoreType.DMA((3,)),
                   pltpu.HBM(land_thru.shape, land_thru.dtype), jax.ShapeDtypeStruct((8, LANES), F32)),
        in_specs=(_HBM, _SEM, _SEM, _ANY),
        out_specs=(_SEM, _SEM, _HBM, pl.BlockSpec(memory_space=pltpu.VMEM)),
        input_output_aliases={0: 2},
        compiler_params=pltpu.CompilerParams(has_side_effects=_EFFECT),
    )(land_thru, send_sems, recv_sems, after)
    return (send2, recv2, land2), token


def gather_finish(handle, after, *, name):
    send2, recv2, land2 = handle

    def body(land_ref, send2, recv2, after_ref, got_ref):
        sends, recvs = _second_copies(land_ref, send2, recv2)
        for cp in sends:
            cp.wait_send()
        for cp in recvs:
            cp.wait_recv()

    return pl.pallas_call(
        body, name=name + "_w",
        out_shape=pltpu.HBM(land2.shape, land2.dtype),
        in_specs=(_HBM, _SEM, _SEM, _ANY),
        out_specs=_HBM,
        input_output_aliases={0: 0},
        compiler_params=pltpu.CompilerParams(has_side_effects=_EFFECT),
    )(land2, send2, recv2, after)


def _slab_copies(g_ref, r_ref, send_sems, recv_sems):
    x, y, c = lax.axis_index("x"), lax.axis_index("y"), lax.axis_index("c")
    me = 4 * x + 2 * y + c
    sends, recvs = [], []
    for k in range(1, N_DEV):
        px, py, pc = _flip(x, (k >> 2) & 1), _flip(y, (k >> 1) & 1), _flip(c, k & 1)
        peer = 4 * px + 2 * py + pc
        common = dict(send_sem=send_sems.at[k - 1], recv_sem=recv_sems.at[k - 1], device_id=(px, py, pc),
                      device_id_type=pl.DeviceIdType.MESH)
        sends.append(pltpu.make_async_remote_copy(src_ref=g_ref.at[peer], dst_ref=r_ref.at[me], **common))
        recvs.append(pltpu.make_async_remote_copy(src_ref=g_ref.at[peer], dst_ref=r_ref.at[peer], **common))
    return sends, recvs


def exchange_start(slabs, *, name):
    land = lax.empty(slabs.shape, slabs.dtype)

    def body(g_ref, r_ref, send_sems, recv_sems, g_thru, r_thru, token):
        sends, _ = _slab_copies(g_ref, r_ref, send_sems, recv_sems)
        for cp in sends:
            cp.start()
        token[...] = jnp.zeros_like(token)

    send_sems, recv_sems, g_thru, r_thru, token = pl.pallas_call(
        body, name=name,
        out_shape=(pltpu.SemaphoreType.DMA((N_DEV - 1,)), pltpu.SemaphoreType.DMA((N_DEV - 1,)),
                   pltpu.HBM(slabs.shape, slabs.dtype), pltpu.HBM(slabs.shape, slabs.dtype),
                   jax.ShapeDtypeStruct((8, LANES), F32)),
        in_specs=(_HBM, _HBM),
        out_specs=(_SEM, _SEM, _HBM, _HBM, pl.BlockSpec(memory_space=pltpu.VMEM)),
        input_output_aliases={0: 2, 1: 3},
        compiler_params=pltpu.CompilerParams(has_side_effects=_EFFECT),
    )(pltpu.with_memory_space_constraint(slabs, pltpu.HBM), pltpu.with_memory_space_constraint(land, pltpu.HBM))
    return (send_sems, recv_sems, g_thru, r_thru), token


def exchange_finish(handle, after, *, name):
    send_sems, recv_sems, g_thru, r_thru = handle

    def body(g_ref, r_ref, send_sems, recv_sems, after_ref, g_out, r_out):
        sends, recvs = _slab_copies(g_ref, r_ref, send_sems, recv_sems)
        for cp in sends:
            cp.wait_send()
        for cp in recvs:
            cp.wait_recv()

    return pl.pallas_call(
        body, name=name,
        out_shape=(pltpu.HBM(g_thru.shape, g_thru.dtype), pltpu.HBM(r_thru.shape, r_thru.dtype)),
        in_specs=(_HBM, _HBM, _SEM, _SEM, _ANY),
        out_specs=(_HBM, _HBM),
        input_output_aliases={0: 0, 1: 1},
        compiler_params=pltpu.CompilerParams(has_side_effects=_EFFECT),
    )(g_thru, r_thru, send_sems, recv_sems, after)


def adamw_own(own, got, me, w, m, v, *, name):
    P, R, C = got.shape
    tr = _tile(R, max(16, (128 * 1024) // C), 16)

    def body(me_ref, own_ref, p_ref, w_ref, m_ref, v_ref, g_ref, d_ref, mo_ref, vo_ref):
        mine = own_ref[...].astype(F32)
        g = None
        for i in range(P):
            term = jnp.where(me_ref[0] == i, mine, p_ref[i].astype(F32))
            g = term if g is None else g + term
        m_new = ADAM_B1 * m_ref[...] + (1.0 - ADAM_B1) * g
        v_new = ADAM_B2 * v_ref[...] + (1.0 - ADAM_B2) * jnp.square(g)
        m_hat = m_new / (1.0 - ADAM_B1 ** ADAM_STEP)
        v_hat = v_new / (1.0 - ADAM_B2 ** ADAM_STEP)
        g_ref[...] = g
        d_ref[...] = -ADAM_LR * (m_hat / (jnp.sqrt(v_hat) + ADAM_EPS) + ADAM_WD * w_ref[...])
        mo_ref[...] = m_new
        vo_ref[...] = v_new

    blk = pl.BlockSpec((tr, C), lambda i, me_ref: (i, 0))
    out = jax.ShapeDtypeStruct((R, C), F32)
    return pl.pallas_call(
        body,
        out_shape=(out, out, out, out),
        grid_spec=pltpu.PrefetchScalarGridSpec(
            num_scalar_prefetch=1,
            grid=(R // tr,),
            in_specs=[pl.BlockSpec((None, tr, C), lambda i, me_ref: (me_ref[0], i, 0)),
                      pl.BlockSpec((P, tr, C), lambda i, me_ref: (0, i, 0)), blk, blk, blk],
            out_specs=(blk, blk, blk, blk),
        ),
        compiler_params=_params("parallel"),
        name=name,
    )(me.reshape(1).astype(jnp.int32), own, got, w, m, v)


def _pad_rows(a, rows):
    return jnp.pad(a, ((0, rows - a.shape[0]), (0, 0)))


def _pad_cols(a, cols):
    return jnp.pad(a, ((0, 0), (0, cols - a.shape[1])))


def kernel(x, meta, a_norm, a_w_in, a_conv, a_w_out, kv_norm, w_kv, k_norm, w_f, b_f, b_norm, b_w_q, b_q_norm, b_w_o, ffn_norm, ffn_w_gu, ffn_w_down, loss_target, m_meta, m_a_norm, m_a_w_in, m_a_conv, m_a_w_out, m_kv_norm, m_w_kv, m_k_norm, m_w_f, m_b_f, m_b_norm, m_b_w_q, m_b_q_norm, m_b_w_o, m_ffn_norm, m_ffn_w_gu, m_ffn_w_down, v_meta, v_a_norm, v_a_w_in, v_a_conv, v_a_w_out, v_kv_norm, v_w_kv, v_k_norm, v_w_f, v_b_f, v_b_norm, v_b_w_q, v_b_q_norm, v_b_w_o, v_ffn_norm, v_ffn_w_gu, v_ffn_w_down):
    S, D = x.shape[1], x.shape[2]
    n_meta = meta.shape[0]
    Ds = meta.shape[1]
    H = D // HEAD_DIM
    n_a, n_b = a_w_in.shape[0], b_w_q.shape[0]
    depth = n_a + n_b
    Fs = ffn_w_down.shape[1]
    pad = BLOCK - n_meta
    lead = pad + n_meta
    T = lead + S
    tk_attn = _tile(T, 384, LANES)
    nk_attn = T // tk_attn
    q_scale = 1.0 / math.sqrt(HEAD_DIM)
    my = 4 * lax.axis_index("x") + 2 * lax.axis_index("y") + lax.axis_index("c")

    wf_t = w_f.reshape(H, Ds)
    small = jnp.concatenate([meta, _pad_rows(a_norm, 8), _pad_rows(a_conv.reshape(n_a * 3, Ds), 8), wf_t], axis=0)
    r_an, r_ac, r_wf = n_meta, n_meta + 8, n_meta + 16
    gs = all_gather(small, name="ag_small")
    unshard = lambda blk: jnp.transpose(blk, (1, 0, 2)).reshape(blk.shape[1], D)
    meta_full = unshard(gs[:, 0:n_meta])
    a_norm_full = unshard(gs[:, r_an:r_an + n_a])
    a_conv_full = unshard(gs[:, r_ac:r_ac + 3 * n_a]).reshape(n_a, 3, D)
    w_f_full = gs[:, r_wf:r_wf + H].reshape(D, H)
    wf_pad = _pad_cols(w_f_full, LANES).astype(BF)[None]
    bf_pad = _pad_cols(b_f.reshape(1, H), LANES)

    def layer_shards(l):
        if l < n_a:
            mix = [(("in", l), a_w_in[l]), (("out", l), a_w_out[l])]
        else:
            j = l - n_a
            mix = ([(("kv", 0), w_kv)] if j == 0 else []) + [(("q", j), b_w_q[j]), (("o", j), b_w_o[j])]
        return mix + [(("gu", l), ffn_w_gu[l]), (("dn", l), ffn_w_down[l])]

    first_level, second_level, W = {}, {}, {}
    st = {"done": None, "tok": None}

    def note(val):
        st["done"] = val
        return val

    def take():
        tok, st["tok"] = st["tok"], None
        return tok

    def chain_after(default):
        if st["tok"] is not None:
            return st["tok"]
        return default if st["done"] is None else st["done"]

    def ag_name(key):
        return f"ag_{key[0]}{key[1]}"

    def start_layer(l):
        for key, shard in layer_shards(l):
            first_level[key], st["tok"] = gather_start(shard, my, chain_after(shard), name=ag_name(key))

    def pass_on(keys):
        for key in keys:
            second_level[key], st["tok"] = gather_mid(first_level.pop(key), chain_after(None), name=ag_name(key))

    def weight(key, shape=None):
        w = gather_finish(second_level.pop(key), st["done"], name=ag_name(key))
        W[key] = w if shape is None else w.reshape(shape)
        return W[key]

    def layer_keys(l):
        keys = [key for key, _ in layer_shards(l)]
        return keys[:-2], keys[-2:]

    h = note(jnp.concatenate([jnp.zeros((pad, D), F32), meta_full, x[0]], axis=0))
    start_layer(0)
    pass_on(layer_keys(0)[0])
    saved = []
    shared = None
    for l in range(depth):
        rec = {"h": h}
        mix_keys, ffn_keys = layer_keys(l)

        def ahead():
            if l >= 1:
                pass_on(ffn_keys)
            if l + 1 < depth:
                start_layer(l + 1)

        if l < n_a:
            xn = note(rms_fwd(h, a_norm_full[l], name=f"a{l}_norm", dep=take()))
            ahead()
            proj = note(mm_nn(xn, weight(("in", l)), name=f"a{l}_in", dep=take()))
            if l == 0:
                pass_on(ffn_keys[:1])
            y = note(conv_fwd(proj, a_conv_full[l], name=f"a{l}_conv"))
            h1 = note(mm_nn(y, weight(("out", l), (1, D, D)), add=h, name=f"a{l}_out", dep=take()))
            rec.update(xn=xn, proj=proj, y=y)
        else:
            j = l - n_a
            if j == 0:
                xnk = note(rms_fwd(h, kv_norm, name="kv_norm", dep=take()))
                ahead()
                kv = note(mm_nn(xnk, weight(("kv", 0)), name="kv_proj", dep=take()))
                k, v = kv_post(kv, k_norm, name="kv_post")
                logits = mm_nn(xnk, wf_pad, name="f_logits", tn_target=LANES)
                cfull = fgate_fwd(logits, bf_pad, pad, name="f_gate")
                crow = jnp.transpose(cfull[:, :H]).reshape(H, nk_attn, 1, tk_attn)
                shared = dict(h=h, xnk=xnk, kv=kv, logits=logits)
                xn = note(rms_fwd(h, b_norm[j], name=f"b{j}_norm"))
            else:
                xn = note(rms_fwd(h, b_norm[j], name=f"b{j}_norm", dep=take()))
                ahead()
            qraw = note(mm_nn(xn, weight(("q", j), (1, D, D)), name=f"b{j}_q", dep=take()))
            q = hn_fwd(qraw, b_q_norm[j], q_scale, name=f"b{j}_qnorm")
            o, lse = attn_fwd(q, k, v, crow, pad, name=f"b{j}_attn")
            note(o)
            h1 = note(mm_nn(o, weight(("o", j), (1, D, D)), add=h, name=f"b{j}_o"))
            rec.update(xn=xn, qraw=qraw, q=q, o=o, lse=lse)
        xn2 = note(rms_fwd(h1, ffn_norm[l], name=f"f{l}_norm", dep=take()))
        if l == 0:
            pass_on(ffn_keys[1:])
        act, g_s, u_s = mm_swiglu(xn2, weight(("gu", l)), name=f"f{l}_gu", dep=take())
        note(act)
        if l + 1 < depth:
            pass_on(layer_keys(l + 1)[0])
        h = note(mm_nn(act, weight(("dn", l), (1, N_DEV * Fs, D)), add=h1, name=f"f{l}_down", tk_target=1408,
                       dep=take()))
        rec.update(h1=h1, xn2=xn2, act=act, g=g_s, u=u_s)
        saved.append(rec)

    dh, dhb, loss_tile = loss_head(h, loss_target[0], lead, name="loss")
    loss = lax.psum(loss_tile[0, 0], MESH_AXES)

    upd = {}
    small_g = {}
    inflight = []

    def big(name, l, section, slabs, w, m, v):
        handle, st["tok"] = exchange_start(slabs.reshape(N_DEV, -1, w.shape[-1]), name=f"{name}{l}_xs")
        inflight.append((section, name, l, handle, w, m, v))

    def land(sections, after):
        for entry in [e for e in inflight if sections is None or e[0] in sections]:
            inflight.remove(entry)
            _, name, l, handle, w, m, v = entry
            own, got = exchange_finish(handle, after, name=f"{name}{l}_xw")
            shp = w.shape
            flat = lambda t: t.reshape(-1, shp[-1])
            res = adamw_own(own, got, my, flat(w), flat(m), flat(v), name=f"{name}{l}_adamw")
            upd.setdefault(name, {})[l] = [r.reshape(shp) for r in res]

    dk = dv = dck = dcq = None
    for l in reversed(range(depth)):
        rec = saved[l]
        land([("ffn", l + 1)], dh)
        dg, du = mm_nt_dswiglu(dhb, W[("dn", l)], rec["g"], rec["u"], name=f"f{l}_ddown")
        big("ffn_w_down", l, ("ffn", l), mm_tn(rec["act"], dhb, 1, name=f"f{l}_wdown"),
            ffn_w_down[l], m_ffn_w_down[l], v_ffn_w_down[l])
        dgu = jnp.concatenate([dg, du], axis=1)
        big("ffn_w_gu", l, ("ffn", l), mm_tn(rec["xn2"], dgu, N_DEV, name=f"f{l}_wgu", dep=take()),
            ffn_w_gu[l], m_ffn_w_gu[l], v_ffn_w_gu[l])
        dxn2 = mm_nt(dgu, W[("gu", l)], name=f"f{l}_dgu", dep=take())
        dh1, dhb, dgf = rms_bwd(dxn2, rec["h1"], ffn_norm[l], dh, name=f"f{l}_dnorm")
        small_g[("ffn_norm", l)] = dgf
        land([("mix", l + 1)], dh1)
        if l < n_a:
            dy = mm_nt(dhb, W[("out", l)], name=f"a{l}_dout")
            big("a_w_out", l, ("mix", l), mm_tn(rec["y"], dhb, 1, name=f"a{l}_wout"),
                a_w_out[l], m_a_w_out[l], v_a_w_out[l])
            db, dc, dhh, dcw = conv_bwd(dy, rec["proj"], a_conv_full[l], name=f"a{l}_dconv", dep=take())
            small_g[("a_conv", l)] = dcw
            dproj = jnp.concatenate([db, dc, dhh], axis=1)
            big("a_w_in", l, ("mix", l), mm_tn(rec["xn"], dproj, N_DEV, name=f"a{l}_win"),
                a_w_in[l], m_a_w_in[l], v_a_w_in[l])
            dxn = mm_nt(dproj, W[("in", l)], name=f"a{l}_din", dep=take())
            dh, dhb, dga = rms_bwd(dxn, rec["h"], a_norm_full[l], dh1, name=f"a{l}_dnorm")
            small_g[("a_norm", l)] = dga
        else:
            j = l - n_a
            do = mm_nt(dhb, W[("o", j)], name=f"b{j}_do")
            big("b_w_o", j, ("mix", l), mm_tn(rec["o"], dhb, 1, name=f"b{j}_wo"),
                b_w_o[j], m_b_w_o[j], v_b_w_o[j])
            prev = None if dk is None else (dk, dv, dck, dcq)
            dq, dk, dv, dck, dcq = attn_bwd(rec["q"], k, v, do, rec["o"], rec["lse"], crow, prev, pad,
                                            name=f"b{j}_dattn", dep=take())
            dqraw, dqn = hn_bwd(dq, rec["qraw"], b_q_norm[j], q_scale, name=f"b{j}_dqnorm")
            small_g[("b_q_norm", j)] = dqn
            big("b_w_q", j, ("mix", l), mm_tn(rec["xn"], dqraw, 1, name=f"b{j}_wq"),
                b_w_q[j], m_b_w_q[j], v_b_w_q[j])
            dxn = mm_nt(dqraw, W[("q", j)], name=f"b{j}_dq", dep=take())
            dh, dhb, dgb = rms_bwd(dxn, rec["h"], b_norm[j], dh1, name=f"b{j}_dnorm")
            small_g[("b_norm", j)] = dgb
            if j == 0:
                dkraw, dkn = hn_bwd(dk, shared["kv"], k_norm, 1.0, name="kv_dknorm")
                dkv = jnp.concatenate([dkraw, dv.astype(BF)], axis=1)
                dc_full = _pad_cols(jnp.transpose(dck.reshape(H, T) + dcq.reshape(H, T)), LANES)
                dz, dbf = fgate_bwd(dc_full, shared["logits"], bf_pad, pad, name="f_dgate")
                big("w_kv", 0, ("mix", l), mm_tn(shared["xnk"], dkv, N_DEV, name="kv_wkv"), w_kv, m_w_kv, v_w_kv)
                dwf_t = mm_tn(dz, shared["xnk"], 1, name="f_wf", out_dtype=F32, tn_target=1024,
                              dep=take())[0, :H]
                dxn_f = mm_nt(dz, wf_pad, name="f_dxn")
                dxnk = mm_nt(dkv, W[("kv", 0)], add=dxn_f, name="kv_dxn")
                dh, dhb, dgkv = rms_bwd(dxnk, shared["h"], kv_norm, dh, name="kv_dnorm")
    land(None, dh)

    grad_x = dh[lead:][None]

    row8 = lambda a: _pad_rows(_pad_cols(a, D), 8)
    stack = lambda key, n: jnp.concatenate([small_g[(key, i)] for i in range(n)], axis=0)
    g_sharded = jnp.concatenate([dh[pad:lead], row8(stack("a_norm", n_a)), row8(stack("a_conv", n_a)), dwf_t], axis=0)
    g_repl = jnp.concatenate([row8(jnp.concatenate([dgkv, stack("b_norm", n_b)], axis=0)),
                              row8(stack("ffn_norm", depth)),
                              row8(jnp.concatenate([_pad_cols(dkn, D), _pad_cols(stack("b_q_norm", n_b), D),
                                                    _pad_cols(dbf[:, :H], D)], axis=0))], axis=0)
    n_sh = g_sharded.shape[0]
    gathered = all_gather(jnp.concatenate([g_sharded, g_repl], axis=0), name="ag_small_grads")
    parts_sh = lax.dynamic_slice_in_dim(gathered[:, :n_sh], my * Ds, Ds, axis=2)
    parts_rp = gathered[:, n_sh:]

    def pack_sh(t_meta, t_an, t_ac, t_wf):
        return jnp.concatenate([t_meta, _pad_rows(t_an, 8), _pad_rows(t_ac.reshape(n_a * 3, Ds), 8),
                                jnp.transpose(t_wf)], axis=0)

    def pack_rp(t_kv, t_bn, t_fn, t_kn, t_qn, t_bf):
        return jnp.concatenate([row8(jnp.concatenate([t_kv.reshape(1, D), t_bn], axis=0)), row8(t_fn),
                                row8(jnp.concatenate([_pad_cols(t_kn.reshape(1, -1), D), _pad_cols(t_qn, D),
                                                      _pad_cols(t_bf.reshape(1, -1), D)], axis=0))], axis=0)

    res_sh = adamw(parts_sh, pack_sh(meta, a_norm, a_conv, w_f), pack_sh(m_meta, m_a_norm, m_a_conv, m_w_f),
                   pack_sh(v_meta, v_a_norm, v_a_conv, v_w_f), name="small_sharded_adamw")
    res_rp = adamw(parts_rp, pack_rp(kv_norm, b_norm, ffn_norm, k_norm, b_q_norm, b_f),
                   pack_rp(m_kv_norm, m_b_norm, m_ffn_norm, m_k_norm, m_b_q_norm, m_b_f),
                   pack_rp(v_kv_norm, v_b_norm, v_ffn_norm, v_k_norm, v_b_q_norm, v_b_f), name="small_repl_adamw")

    def unpack(kind):
        sh, rp = res_sh[kind], res_rp[kind]
        out = {
            "meta": sh[0:n_meta],
            "a_norm": sh[r_an:r_an + n_a],
            "a_conv": sh[r_ac:r_ac + 3 * n_a].reshape(n_a, 3, Ds),
            "w_f": jnp.transpose(sh[r_wf:r_wf + H]),
            "kv_norm": rp[0],
            "b_norm": rp[1:1 + n_b],
            "ffn_norm": rp[8:8 + depth],
            "k_norm": rp[16, :HEAD_DIM],
            "b_q_norm": rp[17:17 + n_b, :HEAD_DIM],
            "b_f": rp[17 + n_b, :H],
        }
        for name, n in (("a_w_in", n_a), ("a_w_out", n_a), ("b_w_q", n_b), ("b_w_o", n_b),
                        ("ffn_w_gu", depth), ("ffn_w_down", depth)):
            out[name] = jnp.stack([upd[name][i][kind] for i in range(n)], axis=0)
        out["w_kv"] = upd["w_kv"][0][kind]
        return out

    order = ["meta", "a_norm", "a_w_in", "a_conv", "a_w_out", "kv_norm", "w_kv", "k_norm", "w_f", "b_f",
             "b_norm", "b_w_q", "b_q_norm", "b_w_o", "ffn_norm", "ffn_w_gu", "ffn_w_down"]
    outs = [loss, grad_x]
    for kind in range(4):
        vals = unpack(kind)
        outs += [vals[n] for n in order]
    return tuple(outs)
```

```python
import functools
import math

import jax
import jax.numpy as jnp
from jax import lax
from jax.experimental import pallas as pl
from jax.experimental.pallas import tpu as pltpu

N_DEV = 8
MESH_AXES = ("x", "y", "c")
EPS = 1e-6
NEG = -1e30
HEAD_DIM = 128
BLOCK = 128
LANES = 128
V7X_VMEM_LIMIT = 56 * 1024 * 1024

ADAM_LR = 0.001
ADAM_B1 = 0.9
ADAM_B2 = 0.999
ADAM_EPS = 1e-08
ADAM_WD = 0.01
ADAM_STEP = 10

BF = jnp.bfloat16
F32 = jnp.float32


def _tile(n, target, mult):
    best = None
    for t in range(mult, min(n, target) + 1, mult):
        if n % t == 0:
            best = t
    return n if best is None else best


def _params(*sem):
    return pltpu.CompilerParams(dimension_semantics=sem, vmem_limit_bytes=V7X_VMEM_LIMIT)


def _with_dep(body, in_specs, args, dep):
    if dep is None:
        return body, list(in_specs), list(args)
    n_in = len(args)

    def body_dep(*refs):
        body(*refs[:n_in], *refs[n_in + 1:])

    return body_dep, list(in_specs) + [pl.BlockSpec(memory_space=pl.ANY)], list(args) + [dep]


def mm_nn(a, w, *, name, add=None, dep=None, out_dtype=F32, tm_target=1056, tn_target=1024, tk_target=2048):
    M, K = a.shape
    G, K2, n = w.shape
    assert K == K2
    tm = _tile(M, tm_target, 16)
    tn = _tile(n, tn_target, LANES)
    tk = _tile(K, tk_target, LANES)
    nj, nk = n // tn, K // tk
    has_add = add is not None

    def body(*refs):
        if has_add:
            a_ref, w_ref, add_ref, o_ref = refs[:4]
        else:
            a_ref, w_ref, o_ref = refs[:3]
            add_ref = None

        def finish(r):
            if has_add:
                r = r + add_ref[...]
            o_ref[...] = r.astype(out_dtype)

        part = jnp.dot(a_ref[...], w_ref[...], preferred_element_type=F32)
        if nk == 1:
            finish(part)
        else:
            acc_ref = refs[-1]
            k = pl.program_id(2)

            @pl.when(k == 0)
            def _():
                acc_ref[...] = part

            @pl.when(k > 0)
            def _():
                acc_ref[...] += part

            @pl.when(k == nk - 1)
            def _():
                finish(acc_ref[...])

    in_specs = [
        pl.BlockSpec((tm, tk), lambda i, j, k: (i, k)),
        pl.BlockSpec((None, tk, tn), lambda i, j, k: (j // nj, k, j % nj)),
    ]
    args = [a, w]
    if has_add:
        in_specs.append(pl.BlockSpec((tm, tn), lambda i, j, k: (i, j)))
        args.append(add)
    body, in_specs, args = _with_dep(body, in_specs, args, dep)
    return pl.pallas_call(
        body,
        out_shape=jax.ShapeDtypeStruct((M, G * n), out_dtype),
        grid=(M // tm, G * nj, nk),
        in_specs=in_specs,
        out_specs=pl.BlockSpec((tm, tn), lambda i, j, k: (i, j)),
        scratch_shapes=[pltpu.VMEM((tm, tn), F32)] if nk > 1 else [],
        compiler_params=_params("parallel", "parallel", "arbitrary"),
        name=name,
    )(*args)


def mm_swiglu(xn, wgu, *, name, dep=None, save_dtype=BF, tm_target=528):
    M, K = xn.shape
    G, _, n = wgu.shape
    half = G // 2
    tm = _tile(M, tm_target, 16)
    tn = _tile(n, 1408, LANES)
    nj = n // tn
    Fh = half * n

    def body(a_ref, wg_ref, wu_ref, act_ref, g_ref, u_ref):
        a = a_ref[...]
        g = jnp.dot(a, wg_ref[...], preferred_element_type=F32)
        g_ref[...] = g.astype(save_dtype)
        silu = g * jax.nn.sigmoid(g)
        u = jnp.dot(a, wu_ref[...], preferred_element_type=F32)
        u_ref[...] = u.astype(save_dtype)
        act_ref[...] = (silu * u).astype(BF)

    out_block = pl.BlockSpec((tm, tn), lambda j, i: (i, j))
    once = pl.Buffered(1)
    body, in_specs, args = _with_dep(body, [
        pl.BlockSpec((tm, K), lambda j, i: (i, 0)),
        pl.BlockSpec((None, K, tn), lambda j, i: (j // nj, 0, j % nj), pipeline_mode=once),
        pl.BlockSpec((None, K, tn), lambda j, i: (half + j // nj, 0, j % nj), pipeline_mode=once),
    ], [xn, wgu, wgu], dep)
    return pl.pallas_call(
        body,
        out_shape=(jax.ShapeDtypeStruct((M, Fh), BF),
                   jax.ShapeDtypeStruct((M, Fh), save_dtype),
                   jax.ShapeDtypeStruct((M, Fh), save_dtype)),
        grid=(half * nj, M // tm),
        in_specs=in_specs,
        out_specs=(out_block, out_block, out_block),
        compiler_params=_params("parallel", "parallel"),
        name=name,
    )(*args)


def mm_nt(dy, w, *, name, add=None, dep=None, out_dtype=F32, tm_target=1056, tko_target=1024, tc_target=1408):
    if dy.ndim == 2:
        dy = dy.reshape(1, *dy.shape)
    P, M, Np = dy.shape
    G, K, n = w.shape
    assert P * Np == G * n
    tm = _tile(M, tm_target, 16)
    tko = _tile(K, tko_target, LANES)
    tc = _tile(n, tc_target, LANES)
    nc = n // tc
    steps = G * nc
    assert Np % tc == 0
    per_part = Np // tc
    has_add = add is not None

    def body(*refs):
        if has_add:
            dy_ref, w_ref, add_ref, o_ref = refs[:4]
        else:
            dy_ref, w_ref, o_ref = refs[:3]
            add_ref = None

        def finish(r):
            if has_add:
                r = r + add_ref[...]
            o_ref[...] = r.astype(out_dtype)

        part = lax.dot_general(dy_ref[...], w_ref[...], (((1,), (1,)), ((), ())),
                               preferred_element_type=F32)
        if steps == 1:
            finish(part)
        else:
            acc_ref = refs[-1]
            s = pl.program_id(2)

            @pl.when(s == 0)
            def _():
                acc_ref[...] = part

            @pl.when(s > 0)
            def _():
                acc_ref[...] += part

            @pl.when(s == steps - 1)
            def _():
                finish(acc_ref[...])

    in_specs = [
        pl.BlockSpec((None, tm, tc), lambda i, o, s: (s // per_part, i, s % per_part)),
        pl.BlockSpec((None, tko, tc), lambda i, o, s: (s // nc, o, s % nc)),
    ]
    args = [dy, w]
    if has_add:
        in_specs.append(pl.BlockSpec((tm, tko), lambda i, o, s: (i, o)))
        args.append(add)
    body, in_specs, args = _with_dep(body, in_specs, args, dep)
    return pl.pallas_call(
        body,
        out_shape=jax.ShapeDtypeStruct((M, K), out_dtype),
        grid=(M // tm, K // tko, steps),
        in_specs=in_specs,
        out_specs=pl.BlockSpec((tm, tko), lambda i, o, s: (i, o)),
        scratch_shapes=[pltpu.VMEM((tm, tko), F32)] if steps > 1 else [],
        compiler_params=_params("parallel", "parallel", "arbitrary"),
        name=name,
    )(*args)


def mm_nt_dswiglu(dh, w_down, g_s, u_s, *, name, tm_target=1056, tf_target=512):
    M, D = dh.shape
    _, Fh, D2 = w_down.shape
    assert D == D2
    tm = _tile(M, tm_target, 16)
    tf = _tile(Fh, tf_target, LANES)

    def body(dh_ref, w_ref, g_ref, u_ref, dgu_ref):
        dact = lax.dot_general(dh_ref[...], w_ref[...], (((1,), (1,)), ((), ())),
                               preferred_element_type=F32)
        g = g_ref[...].astype(F32)
        u = u_ref[...].astype(F32)
        sig = jax.nn.sigmoid(g)
        dgu_ref[1] = (dact * (g * sig)).astype(BF)
        dgu_ref[0] = (dact * u * (sig * (1.0 + g * (1.0 - sig)))).astype(BF)

    blk = pl.BlockSpec((tm, tf), lambda i, f: (i, f))
    return pl.pallas_call(
        body,
        out_shape=jax.ShapeDtypeStruct((2, M, Fh), BF),
        grid=(M // tm, Fh // tf),
        in_specs=[
            pl.BlockSpec((tm, D), lambda i, f: (i, 0)),
            pl.BlockSpec((None, tf, D), lambda i, f: (0, f, 0)),
            blk, blk,
        ],
        out_specs=pl.BlockSpec((2, tm, tf), lambda i, f: (0, i, f)),
        compiler_params=_params("parallel", "parallel"),
        name=name,
    )(dh, w_down, g_s, u_s)


def mm_tn(a, dy, groups, *, name, dep=None, out_dtype=BF, tk_target=512, tn_target=1408):
    M, K = a.shape
    if dy.ndim == 2:
        dy = dy.reshape(1, *dy.shape)
    P, M2, Np = dy.shape
    N = P * Np
    assert M == M2 and N % groups == 0
    n = N // groups
    tk = _tile(K, tk_target, LANES)
    tn = _tile(n, tn_target, LANES)
    nj = n // tn
    assert Np % tn == 0
    per_part = Np // tn

    def body(a_ref, dy_ref, o_ref):
        o_ref[...] = lax.dot_general(a_ref[...], dy_ref[...], (((0,), (0,)), ((), ())),
                                     preferred_element_type=F32).astype(out_dtype)

    body, in_specs, args = _with_dep(body, [
        pl.BlockSpec((M, tk), lambda i, j: (0, i)),
        pl.BlockSpec((None, M, tn), lambda i, j: (j // per_part, 0, j % per_part)),
    ], [a, dy], dep)
    return pl.pallas_call(
        body,
        out_shape=jax.ShapeDtypeStruct((groups, K, n), out_dtype),
        grid=(K // tk, groups * nj),
        in_specs=in_specs,
        out_specs=pl.BlockSpec((None, tk, tn), lambda i, j: (j // nj, i, j % nj)),
        compiler_params=_params("parallel", "parallel"),
        name=name,
    )(*args)


def rms_fwd(h, g, *, name, dep=None):
    T, D = h.shape
    tm = _tile(T, 528, 16)

    def body(h_ref, g_ref, o_ref):
        x = h_ref[...]
        r = lax.rsqrt(jnp.mean(x * x, axis=-1, keepdims=True) + EPS)
        o_ref[...] = ((x * r) * g_ref[...]).astype(BF)

    body, in_specs, args = _with_dep(
        body, [pl.BlockSpec((tm, D), lambda i: (i, 0)), pl.BlockSpec((1, D), lambda i: (0, 0))],
        [h, g.reshape(1, D)], dep)
    return pl.pallas_call(
        body,
        out_shape=jax.ShapeDtypeStruct((T, D), BF),
        grid=(T // tm,),
        in_specs=in_specs,
        out_specs=pl.BlockSpec((tm, D), lambda i: (i, 0)),
        compiler_params=_params("parallel"),
        name=name,
    )(*args)


def rms_bwd(dxn, h, g, add, *, name):
    T, D = h.shape
    tm = _tile(T, 264, 16)

    def body(dxn_ref, h_ref, g_ref, add_ref, dh_ref, dhb_ref, dg_ref):
        x = h_ref[...]
        dy = dxn_ref[...]
        r = lax.rsqrt(jnp.mean(x * x, axis=-1, keepdims=True) + EPS)
        xhat = x * r
        part = jnp.sum(dy * xhat, axis=0, keepdims=True)

        @pl.when(pl.program_id(0) == 0)
        def _():
            dg_ref[...] = part

        @pl.when(pl.program_id(0) > 0)
        def _():
            dg_ref[...] += part

        dxh = dy * g_ref[...]
        dh = add_ref[...] + r * (dxh - xhat * jnp.mean(dxh * xhat, axis=-1, keepdims=True))
        dh_ref[...] = dh
        dhb_ref[...] = dh.astype(BF)

    row = pl.BlockSpec((tm, D), lambda i: (i, 0))
    vec = pl.BlockSpec((1, D), lambda i: (0, 0))
    return pl.pallas_call(
        body,
        out_shape=(jax.ShapeDtypeStruct((T, D), F32), jax.ShapeDtypeStruct((T, D), BF),
                   jax.ShapeDtypeStruct((1, D), F32)),
        grid=(T // tm,),
        in_specs=[row, row, vec, row],
        out_specs=(row, row, vec),
        compiler_params=_params("arbitrary"),
        name=name,
    )(dxn, h, g.reshape(1, D), add)


def _head_norm(x, gain):
    r = lax.rsqrt(jnp.mean(x * x, axis=-1, keepdims=True) + EPS)
    return (x * r) * gain


def hn_fwd(qraw, gain, out_scale, *, name):
    T, D = qraw.shape
    H = D // HEAD_DIM
    tm = _tile(T, 528, 16)

    def body(q_ref, g_ref, o_ref):
        gain_v = g_ref[...]
        for hd in range(H):
            sl = slice(hd * HEAD_DIM, (hd + 1) * HEAD_DIM)
            o_ref[:, sl] = (_head_norm(q_ref[:, sl], gain_v) * out_scale).astype(BF)

    return pl.pallas_call(
        body,
        out_shape=jax.ShapeDtypeStruct((T, D), BF),
        grid=(T // tm,),
        in_specs=[pl.BlockSpec((tm, D), lambda i: (i, 0)),
                  pl.BlockSpec((1, HEAD_DIM), lambda i: (0, 0))],
        out_specs=pl.BlockSpec((tm, D), lambda i: (i, 0)),
        compiler_params=_params("parallel"),
        name=name,
    )(qraw, gain.reshape(1, HEAD_DIM))


def kv_post(kv, gain, *, name):
    T, D2 = kv.shape
    D = D2 // 2
    H = D // HEAD_DIM
    tm = _tile(T, 528, 16)

    def body(k_ref, v_ref, g_ref, ko_ref, vo_ref):
        gain_v = g_ref[...]
        for hd in range(H):
            sl = slice(hd * HEAD_DIM, (hd + 1) * HEAD_DIM)
            ko_ref[:, sl] = _head_norm(k_ref[:, sl], gain_v).astype(BF)
        vo_ref[...] = v_ref[...].astype(BF)

    blk = pl.BlockSpec((tm, D), lambda i: (i, 0))
    return pl.pallas_call(
        body,
        out_shape=(jax.ShapeDtypeStruct((T, D), BF), jax.ShapeDtypeStruct((T, D), BF)),
        grid=(T // tm,),
        in_specs=[blk, pl.BlockSpec((tm, D), lambda i: (i, 1)),
                  pl.BlockSpec((1, HEAD_DIM), lambda i: (0, 0))],
        out_specs=(blk, blk),
        compiler_params=_params("parallel"),
        name=name,
    )(kv, kv, gain.reshape(1, HEAD_DIM))


def hn_bwd(dq, qraw, gain, out_scale, *, name):
    T, D = dq.shape
    H = D // HEAD_DIM
    tm = _tile(T, 264, 16)

    def body(dq_ref, q_ref, g_ref, o_ref, dg_ref):
        gain_v = g_ref[...]
        part = jnp.zeros((1, HEAD_DIM), F32)
        for hd in range(H):
            sl = slice(hd * HEAD_DIM, (hd + 1) * HEAD_DIM)
            x = q_ref[:, sl]
            dy = dq_ref[:, sl] * out_scale
            r = lax.rsqrt(jnp.mean(x * x, axis=-1, keepdims=True) + EPS)
            xhat = x * r
            part = part + jnp.sum(dy * xhat, axis=0, keepdims=True)
            dxh = dy * gain_v
            o_ref[:, sl] = (r * (dxh - xhat * jnp.mean(dxh * xhat, axis=-1, keepdims=True))).astype(BF)

        @pl.when(pl.program_id(0) == 0)
        def _():
            dg_ref[...] = part

        @pl.when(pl.program_id(0) > 0)
        def _():
            dg_ref[...] += part

    blk = pl.BlockSpec((tm, D), lambda i: (i, 0))
    vec = pl.BlockSpec((1, HEAD_DIM), lambda i: (0, 0))
    return pl.pallas_call(
        body,
        out_shape=(jax.ShapeDtypeStruct((T, D), BF), jax.ShapeDtypeStruct((1, HEAD_DIM), F32)),
        grid=(T // tm,),
        in_specs=[blk, blk, vec],
        out_specs=(blk, vec),
        compiler_params=_params("arbitrary"),
        name=name,
    )(dq, qraw, gain.reshape(1, HEAD_DIM))


def _shift_down(cur, above, k, rowc):
    out = pltpu.roll(cur, k, 0)
    for i in range(k):
        out = jnp.where(rowc == i, above[8 - k + i:8 - k + i + 1], out)
    return out


def _shift_up(cur, below, k, rowc):
    R = cur.shape[0]
    out = pltpu.roll(cur, R - k, 0)
    for i in range(k):
        out = jnp.where(rowc == R - k + i, below[i:i + 1], out)
    return out


def _conv3(u, u_above, wv, rowc):
    u1 = _shift_down(u, u_above, 1, rowc)
    u2 = _shift_down(u, u_above, 2, rowc)
    return wv[0:1] * u2 + wv[1:2] * u1 + wv[2:3] * u, u1, u2


def conv_fwd(proj, w, *, name):
    T, D3 = proj.shape
    D = D3 // 3
    tc = LANES if D % LANES == 0 else D
    nb = D // tc
    R = _tile(T, 264, 8)

    def body(b_ref, c_ref, h_ref, w_ref, y_ref):
        rowc = lax.broadcasted_iota(jnp.int32, (R, 1), 0)
        wv = w_ref[...]
        for r0 in range(0, T, R):
            rows = slice(r0, r0 + R)
            u = c_ref[rows, :] * h_ref[rows, :]
            if r0 == 0:
                above = jnp.zeros((8, tc), F32)
            else:
                above = c_ref[r0 - 8:r0, :] * h_ref[r0 - 8:r0, :]
            conv, _, _ = _conv3(u, above, wv, rowc)
            y_ref[rows, :] = (b_ref[rows, :] * conv).astype(BF)

    return pl.pallas_call(
        body,
        out_shape=jax.ShapeDtypeStruct((T, D), BF),
        grid=(nb,),
        in_specs=[
            pl.BlockSpec((T, tc), lambda j: (0, j)),
            pl.BlockSpec((T, tc), lambda j: (0, nb + j)),
            pl.BlockSpec((T, tc), lambda j: (0, 2 * nb + j)),
            pl.BlockSpec((3, tc), lambda j: (0, j)),
        ],
        out_specs=pl.BlockSpec((T, tc), lambda j: (0, j)),
        compiler_params=_params("parallel"),
        name=name,
    )(proj, proj, proj, w)


def conv_bwd(dy, proj, w, *, name, dep=None):
    T, D = dy.shape
    tc = LANES if D % LANES == 0 else D
    nb = D // tc
    R = _tile(T, 264, 8)

    def body(dy_ref, b_ref, c_ref, h_ref, w_ref, db_ref, dc_ref, dh_ref, dw_ref):
        rowc = lax.broadcasted_iota(jnp.int32, (R, 1), 0)
        wv = w_ref[...]
        dw = [jnp.zeros((1, tc), F32) for _ in range(3)]
        for r0 in range(0, T, R):
            rows = slice(r0, r0 + R)
            c = c_ref[rows, :]
            hh = h_ref[rows, :]
            u = c * hh
            if r0 == 0:
                above = jnp.zeros((8, tc), F32)
            else:
                above = c_ref[r0 - 8:r0, :] * h_ref[r0 - 8:r0, :]
            conv, u1, u2 = _conv3(u, above, wv, rowc)
            dyv = dy_ref[rows, :]
            db_ref[rows, :] = (dyv * conv).astype(BF)
            dconv = dyv * b_ref[rows, :]
            if r0 + R == T:
                below = jnp.zeros((8, tc), F32)
            else:
                below = dy_ref[r0 + R:r0 + R + 8, :] * b_ref[r0 + R:r0 + R + 8, :]
            dw[0] = dw[0] + jnp.sum(dconv * u2, axis=0, keepdims=True)
            dw[1] = dw[1] + jnp.sum(dconv * u1, axis=0, keepdims=True)
            dw[2] = dw[2] + jnp.sum(dconv * u, axis=0, keepdims=True)
            du = (wv[2:3] * dconv + wv[1:2] * _shift_up(dconv, below, 1, rowc)
                  + wv[0:1] * _shift_up(dconv, below, 2, rowc))
            dc_ref[rows, :] = (du * hh).astype(BF)
            dh_ref[rows, :] = (du * c).astype(BF)
        for i in range(3):
            dw_ref[i:i + 1, :] = dw[i]

    strip = pl.BlockSpec((T, tc), lambda j: (0, j))
    wblk = pl.BlockSpec((3, tc), lambda j: (0, j))
    out = jax.ShapeDtypeStruct((T, D), BF)
    body, in_specs, args = _with_dep(body, [
        strip,
        pl.BlockSpec((T, tc), lambda j: (0, j)),
        pl.BlockSpec((T, tc), lambda j: (0, nb + j)),
        pl.BlockSpec((T, tc), lambda j: (0, 2 * nb + j)),
        wblk,
    ], [dy, proj, proj, proj, w], dep)
    return pl.pallas_call(
        body,
        out_shape=(out, out, out, jax.ShapeDtypeStruct((3, D), F32)),
        grid=(nb,),
        in_specs=in_specs,
        out_specs=(strip, strip, strip, wblk),
        compiler_params=_params("parallel"),
        name=name,
    )(*args)


def _log_sigmoid(z):
    return jnp.minimum(z, 0.0) - jnp.log(1.0 + jnp.exp(-jnp.abs(z)))


def fgate_fwd(logits, bias, pad, *, name):
    T, W = logits.shape
    cb = _tile(T, 128, 8)
    nblk = T // cb

    def body(z_ref, b_ref, c_ref, lf_ref):
        row = lax.broadcasted_iota(jnp.int32, (T, 1), 0)
        lf_ref[...] = jnp.where(row >= pad, _log_sigmoid(z_ref[...] + b_ref[...]), 0.0)
        ri = lax.broadcasted_iota(jnp.int32, (cb, cb), 0)
        ci = lax.broadcasted_iota(jnp.int32, (cb, cb), 1)
        tri = (ci <= ri).astype(F32)

        def step(i, carry):
            rows = pl.ds(pl.multiple_of(i * cb, cb), cb)
            blk = lf_ref[rows, :]
            c_ref[rows, :] = carry + jnp.dot(tri, blk, precision=lax.Precision.HIGHEST,
                                             preferred_element_type=F32)
            return carry + jnp.sum(blk, axis=0, keepdims=True)

        lax.fori_loop(0, nblk, step, jnp.zeros((1, W), F32))

    return pl.pallas_call(
        body,
        out_shape=jax.ShapeDtypeStruct((T, W), F32),
        in_specs=[pl.BlockSpec(memory_space=pltpu.VMEM), pl.BlockSpec(memory_space=pltpu.VMEM)],
        out_specs=pl.BlockSpec(memory_space=pltpu.VMEM),
        scratch_shapes=[pltpu.VMEM((T, W), F32)],
        compiler_params=pltpu.CompilerParams(vmem_limit_bytes=V7X_VMEM_LIMIT),
        name=name,
    )(logits, bias)


def fgate_bwd(dc, logits, bias, pad, *, name):
    T, W = logits.shape
    cb = _tile(T, 128, 8)
    nblk = T // cb

    def body(dc_ref, z_ref, b_ref, dz_ref, db_ref, rs_ref):
        ri = lax.broadcasted_iota(jnp.int32, (cb, cb), 0)
        ci = lax.broadcasted_iota(jnp.int32, (cb, cb), 1)
        triu = (ci >= ri).astype(F32)

        def step(i, carry):
            rows = pl.ds(pl.multiple_of((nblk - 1 - i) * cb, cb), cb)
            blk = dc_ref[rows, :]
            rs_ref[rows, :] = carry + jnp.dot(triu, blk, precision=lax.Precision.HIGHEST,
                                              preferred_element_type=F32)
            return carry + jnp.sum(blk, axis=0, keepdims=True)

        lax.fori_loop(0, nblk, step, jnp.zeros((1, W), F32))
        row = lax.broadcasted_iota(jnp.int32, (T, 1), 0)
        z = z_ref[...] + b_ref[...]
        dz = jnp.where(row >= pad, rs_ref[...] * jax.nn.sigmoid(-z), 0.0)
        dz_ref[...] = dz.astype(BF)
        db_ref[...] = jnp.sum(dz, axis=0, keepdims=True)

    vm = pl.BlockSpec(memory_space=pltpu.VMEM)
    return pl.pallas_call(
        body,
        out_shape=(jax.ShapeDtypeStruct((T, W), BF), jax.ShapeDtypeStruct((1, W), F32)),
        in_specs=[vm, vm, vm],
        out_specs=(vm, vm),
        scratch_shapes=[pltpu.VMEM((T, W), F32)],
        compiler_params=pltpu.CompilerParams(vmem_limit_bytes=V7X_VMEM_LIMIT),
        name=name,
    )(dc, logits, bias)


def _scores(qb, kb, ck):
    return lax.dot_general(qb, kb, (((1,), (1,)), ((), ())), preferred_element_type=F32) - ck


def _causal(s, row, col, pad):
    return jnp.where((col <= row) & (col >= pad), s, NEG)


def attn_fwd(q, k, v, crow, pad, *, name):
    T, D = q.shape
    H = D // HEAD_DIM
    nk, tk = crow.shape[1], crow.shape[3]
    tq = tk
    nq = T // tq

    hp = 2 if H % 2 == 0 else 1
    wide = hp * HEAD_DIM

    def body(q_ref, k_ref, v_ref, cr_ref, o_ref, lse_ref):
        qi = pl.program_id(1)
        row = qi * tq + lax.broadcasted_iota(jnp.int32, (tq, 1), 0)
        heads = [slice(a * HEAD_DIM, (a + 1) * HEAD_DIM) for a in range(hp)]
        qbs = [q_ref[:, sl] for sl in heads]

        def step(kc, carry, masked):
            rows = pl.ds(pl.multiple_of(kc * tk, tk), tk)
            out = []
            for a, sl in enumerate(heads):
                m, l, acc = carry[a]
                s = _scores(qbs[a], k_ref[rows, sl], cr_ref[a, kc])
                if masked:
                    s = _causal(s, row, kc * tk + lax.broadcasted_iota(jnp.int32, (1, tk), 1), pad)
                m_new = jnp.maximum(m, jnp.max(s, axis=-1, keepdims=True))
                alpha = jnp.exp(m - m_new)
                p = jnp.exp(s - m_new)
                l = alpha * l + jnp.sum(p, axis=-1, keepdims=True)
                acc = alpha * acc + jnp.dot(p.astype(BF), v_ref[rows, sl], preferred_element_type=F32)
                out.append((m_new, l, acc))
            return tuple(out)

        init = tuple((jnp.full((tq, 1), NEG, F32), jnp.zeros((tq, 1), F32), jnp.zeros((tq, HEAD_DIM), F32))
                     for _ in heads)
        carry = step(0, init, True)
        carry = lax.fori_loop(1, qi, lambda kc, c: step(kc, c, False), carry)
        carry = lax.cond(qi > 0, lambda c: step(qi, c, True), lambda c: c, carry)
        valid = row >= pad
        for a, sl in enumerate(heads):
            m, l, acc = carry[a]
            o_ref[:, sl] = jnp.where(valid, acc / l, 0.0).astype(BF)
            lse_ref[a] = jnp.where(valid, m + jnp.log(l), 0.0)

    return pl.pallas_call(
        body,
        out_shape=(jax.ShapeDtypeStruct((T, D), BF), jax.ShapeDtypeStruct((H, T, 1), F32)),
        grid=(H // hp, nq),
        in_specs=[
            pl.BlockSpec((tq, wide), lambda h, i: (i, h)),
            pl.BlockSpec((T, wide), lambda h, i: (0, h)),
            pl.BlockSpec((T, wide), lambda h, i: (0, h)),
            pl.BlockSpec((hp, nk, 1, tk), lambda h, i: (h, 0, 0, 0)),
        ],
        out_specs=(pl.BlockSpec((tq, wide), lambda h, i: (i, h)),
                   pl.BlockSpec((hp, tq, 1), lambda h, i: (h, i, 0))),
        compiler_params=_params("parallel", "arbitrary"),
        name=name,
    )(q, k, v, crow)


def attn_bwd(q, k, v, do, o, lse, crow, prev, pad, *, name, dep=None):
    T, D = q.shape
    H = D // HEAD_DIM
    nk, tk = crow.shape[1], crow.shape[3]
    tq = tk
    nq = T // tq
    has_prev = prev is not None

    def body(*refs):
        q_ref, k_ref, v_ref, do_ref, o_ref, lse_ref, cr_ref = refs[:7]
        refs = refs[7:]
        if has_prev:
            pk_ref, pv_ref, pc_ref, pq_ref = refs[:4]
            refs = refs[4:]
        dq_ref, dk_ref, dv_ref, dck_ref, dcq_ref, delta_ref = refs
        kc = pl.program_id(1)

        @pl.when(kc == 0)
        def _():
            dq_ref[...] = jnp.zeros_like(dq_ref)
            dcq_ref[...] = pq_ref[...] if has_prev else jnp.zeros_like(dcq_ref)
            do_used = do_ref[...].astype(BF).astype(F32)
            delta_ref[...] = jnp.sum(do_used * o_ref[...].astype(F32), axis=-1, keepdims=True)

        kb = k_ref[...]
        vb = v_ref[...]
        ck = cr_ref[...]
        col = kc * tk + lax.broadcasted_iota(jnp.int32, (1, tk), 1)

        def step(qi, carry, masked):
            dk, dv, dck = carry
            rows = pl.ds(pl.multiple_of(qi * tq, tq), tq)
            qb = q_ref[rows, :]
            dob = do_ref[rows, :].astype(BF)
            s = _scores(qb, kb, ck)
            if masked:
                s = _causal(s, qi * tq + lax.broadcasted_iota(jnp.int32, (tq, 1), 0), col, pad)
            p = jnp.exp(s - lse_ref[rows, :])
            dp = lax.dot_general(dob, vb, (((1,), (1,)), ((), ())), preferred_element_type=F32)
            ds = p * (dp - delta_ref[rows, :])
            dsb = ds.astype(BF)
            dv = dv + lax.dot_general(p.astype(BF), dob, (((0,), (0,)), ((), ())),
                                      preferred_element_type=F32)
            dk = dk + lax.dot_general(dsb, qb, (((0,), (0,)), ((), ())), preferred_element_type=F32)
            dq_ref[rows, :] += jnp.dot(dsb, kb, preferred_element_type=F32)
            dcq_ref[rows, :] += jnp.sum(ds, axis=1, keepdims=True)
            dck = dck - jnp.sum(ds, axis=0, keepdims=True)
            return dk, dv, dck

        def rest(masked):
            return lambda c: lax.fori_loop(kc + 1, nq, lambda qi, cc: step(qi, cc, masked), c)

        init = (jnp.zeros((tk, HEAD_DIM), F32), jnp.zeros((tk, HEAD_DIM), F32), jnp.zeros((1, tk), F32))
        carry = step(kc, init, True)
        dk, dv, dck = lax.cond(kc == 0, rest(True), rest(False), carry)
        if has_prev:
            dk = dk + pk_ref[...]
            dv = dv + pv_ref[...]
            dck = dck + pc_ref[...]
        dk_ref[...] = dk
        dv_ref[...] = dv
        dck_ref[...] = dck

    head_all = pl.BlockSpec((T, HEAD_DIM), lambda h, j: (0, h))
    head_blk = pl.BlockSpec((tk, HEAD_DIM), lambda h, j: (j, h))
    col_all = pl.BlockSpec((None, T, 1), lambda h, j: (h, 0, 0))
    row_blk = pl.BlockSpec((None, None, 1, tk), lambda h, j: (h, j, 0, 0))
    in_specs = [head_all, head_blk, head_blk, head_all, head_all, col_all, row_blk]
    args = [q, k, v, do, o, lse, crow]
    if has_prev:
        in_specs += [head_blk, head_blk, row_blk, col_all]
        args += list(prev)
    body, in_specs, args = _with_dep(body, in_specs, args, dep)
    return pl.pallas_call(
        body,
        out_shape=(jax.ShapeDtypeStruct((T, D), F32), jax.ShapeDtypeStruct((T, D), F32),
                   jax.ShapeDtypeStruct((T, D), F32), jax.ShapeDtypeStruct((H, nk, 1, tk), F32),
                   jax.ShapeDtypeStruct((H, T, 1), F32)),
        grid=(H, nk),
        in_specs=in_specs,
        out_specs=(head_all, head_blk, head_blk, row_blk, col_all),
        scratch_shapes=[pltpu.VMEM((T, 1), F32)],
        compiler_params=_params("parallel", "arbitrary"),
        name=name,
    )(*args)


def loss_head(h, target, lead, *, name):
    T, D = h.shape
    tm = lead
    assert T % tm == 0 and target.shape[0] % tm == 0
    inv_d = 1.0 / D

    def body(h_ref, t_ref, dh_ref, dhb_ref, loss_ref):
        i = pl.program_id(0)

        @pl.when(i == 0)
        def _():
            dh_ref[...] = jnp.zeros_like(dh_ref)
            dhb_ref[...] = jnp.zeros_like(dhb_ref)
            loss_ref[...] = jnp.zeros_like(loss_ref)

        @pl.when(i > 0)
        def _():
            e = h_ref[...] - t_ref[...]
            dh = e * inv_d
            dh_ref[...] = dh
            dhb_ref[...] = dh.astype(BF)
            loss_ref[...] += 0.5 * inv_d * jnp.sum(e * e)

    return pl.pallas_call(
        body,
        out_shape=(jax.ShapeDtypeStruct((T, D), F32), jax.ShapeDtypeStruct((T, D), BF),
                   jax.ShapeDtypeStruct((8, LANES), F32)),
        grid=(T // tm,),
        in_specs=[pl.BlockSpec((tm, D), lambda i: (i, 0)),
                  pl.BlockSpec((tm, D), lambda i: (jnp.maximum(i - 1, 0), 0))],
        out_specs=(pl.BlockSpec((tm, D), lambda i: (i, 0)), pl.BlockSpec((tm, D), lambda i: (i, 0)),
                   pl.BlockSpec((8, LANES), lambda i: (0, 0))),
        compiler_params=_params("arbitrary"),
        name=name,
    )(h, target)


def adamw(parts, w, m, v, *, name):
    P, R, C = parts.shape
    tr = _tile(R, max(16, (128 * 1024) // C), 16)

    def body(p_ref, w_ref, m_ref, v_ref, g_ref, d_ref, mo_ref, vo_ref):
        g = p_ref[0].astype(F32)
        for i in range(1, P):
            g = g + p_ref[i].astype(F32)
        m_new = ADAM_B1 * m_ref[...] + (1.0 - ADAM_B1) * g
        v_new = ADAM_B2 * v_ref[...] + (1.0 - ADAM_B2) * jnp.square(g)
        m_hat = m_new / (1.0 - ADAM_B1 ** ADAM_STEP)
        v_hat = v_new / (1.0 - ADAM_B2 ** ADAM_STEP)
        g_ref[...] = g
        d_ref[...] = -ADAM_LR * (m_hat / (jnp.sqrt(v_hat) + ADAM_EPS) + ADAM_WD * w_ref[...])
        mo_ref[...] = m_new
        vo_ref[...] = v_new

    blk = pl.BlockSpec((tr, C), lambda i: (i, 0))
    out = jax.ShapeDtypeStruct((R, C), F32)
    return pl.pallas_call(
        body,
        out_shape=(out, out, out, out),
        grid=(R // tr,),
        in_specs=[pl.BlockSpec((P, tr, C), lambda i: (0, i, 0)), blk, blk, blk],
        out_specs=(blk, blk, blk, blk),
        compiler_params=_params("parallel"),
        name=name,
    )(parts, w, m, v)


def _flip(v, bit):
    return 1 - v if bit else v


def all_gather(shard, *, name):
    def body(x_ref, out_ref, send_sems, recv_sems, local_sem):
        x, y, c = lax.axis_index("x"), lax.axis_index("y"), lax.axis_index("c")
        me, sibling = (x, y, c), (x, y, 1 - c)
        chips = [(1 - x, y), (x, 1 - y), (1 - x, 1 - y)]

        def block(px, py, pc):
            return out_ref.at[4 * px + 2 * py + pc]

        def copy(k, blk, to, src=None):
            return pltpu.make_async_remote_copy(
                src_ref=block(*blk) if src is None else src,
                dst_ref=block(*blk),
                send_sem=send_sems.at[k],
                recv_sem=recv_sems.at[k],
                device_id=to,
                device_id_type=pl.DeviceIdType.MESH,
            )

        mine = pltpu.make_async_copy(x_ref, block(*me), local_sem)
        mine.start()
        first = [copy(0, me, sibling, src=x_ref)]
        first += [copy(1 + j, me, (*chip, c), src=x_ref) for j, chip in enumerate(chips)]
        for cp in first:
            cp.start()
        passed = [copy(4 + j, (*chip, c), sibling) for j, chip in enumerate(chips)]
        for j, chip in enumerate(chips):
            copy(1 + j, (*chip, c), me).wait_recv()
            passed[j].start()
        copy(0, sibling, me).wait_recv()
        for j, chip in enumerate(chips):
            copy(4 + j, (*chip, 1 - c), me).wait_recv()
        for cp in first + passed:
            cp.wait_send()
        mine.wait()

    return pl.pallas_call(
        body,
        out_shape=jax.ShapeDtypeStruct((N_DEV,) + shard.shape, shard.dtype),
        in_specs=[pl.BlockSpec(memory_space=pl.ANY)],
        out_specs=pl.BlockSpec(memory_space=pl.ANY),
        scratch_shapes=[pltpu.SemaphoreType.DMA((7,)), pltpu.SemaphoreType.DMA((7,)),
                        pltpu.SemaphoreType.DMA],
        name=name,
    )(shard)


def exchange_slabs(slabs, *, name):
    def body(g_ref, r_ref, send_sems, recv_sems, local_sem):
        x, y, c = lax.axis_index("x"), lax.axis_index("y"), lax.axis_index("c")
        me = 4 * x + 2 * y + c
        mine = pltpu.make_async_copy(g_ref.at[me], r_ref.at[me], local_sem)
        mine.start()
        sends, recvs = [], []
        for k in range(1, N_DEV):
            px, py, pc = _flip(x, (k >> 2) & 1), _flip(y, (k >> 1) & 1), _flip(c, k & 1)
            peer = 4 * px + 2 * py + pc
            sends.append(pltpu.make_async_remote_copy(
                src_ref=g_ref.at[peer], dst_ref=r_ref.at[me],
                send_sem=send_sems.at[k - 1], recv_sem=recv_sems.at[k - 1],
                device_id=(px, py, pc), device_id_type=pl.DeviceIdType.MESH))
            recvs.append(pltpu.make_async_remote_copy(
                src_ref=g_ref.at[peer], dst_ref=r_ref.at[peer],
                send_sem=send_sems.at[k - 1], recv_sem=recv_sems.at[k - 1],
                device_id=(px, py, pc), device_id_type=pl.DeviceIdType.MESH))
        for cp in sends:
            cp.start()
        for cp in recvs:
            cp.wait_recv()
        for cp in sends:
            cp.wait_send()
        mine.wait()

    return pl.pallas_call(
        body,
        out_shape=jax.ShapeDtypeStruct(slabs.shape, slabs.dtype),
        in_specs=[pl.BlockSpec(memory_space=pl.ANY)],
        out_specs=pl.BlockSpec(memory_space=pl.ANY),
        scratch_shapes=[pltpu.SemaphoreType.DMA((7,)), pltpu.SemaphoreType.DMA((7,)),
                        pltpu.SemaphoreType.DMA],
        name=name,
    )(slabs)


def reduce_adamw(slabs, w, m, v, *, name):
    got = exchange_slabs(slabs, name=name + "_xchg")
    return adamw(got, w, m, v, name=name + "_adamw")


_HBM = pl.BlockSpec(memory_space=pltpu.HBM)
_SEM = pl.BlockSpec(memory_space=pltpu.SEMAPHORE)
_ANY = pl.BlockSpec(memory_space=pl.ANY)
_EFFECT = pltpu.SideEffectType.DATAFLOW_SIDE_EFFECTING
_N_FIRST = 4


def _first_copies(land_ref, send_sems, recv_sems):
    x, y, c = lax.axis_index("x"), lax.axis_index("y"), lax.axis_index("c")
    mine = land_ref.at[4 * x + 2 * y + c]
    targets = [(x, y, 1 - c), (1 - x, y, c), (x, 1 - y, c), (1 - x, 1 - y, c)]
    sends, recvs = [], []
    for k, (px, py, pc) in enumerate(targets):
        common = dict(send_sem=send_sems.at[k], recv_sem=recv_sems.at[k], device_id=(px, py, pc),
                      device_id_type=pl.DeviceIdType.MESH)
        sends.append(pltpu.make_async_remote_copy(src_ref=mine, dst_ref=mine, **common))
        theirs = land_ref.at[4 * px + 2 * py + pc]
        recvs.append(pltpu.make_async_remote_copy(src_ref=theirs, dst_ref=theirs, **common))
    return sends, recvs


def _second_copies(land_ref, send_sems, recv_sems):
    x, y, c = lax.axis_index("x"), lax.axis_index("y"), lax.axis_index("c")
    sends, recvs = [], []
    for j, (px, py) in enumerate([(1 - x, y), (x, 1 - y), (1 - x, 1 - y)]):
        common = dict(send_sem=send_sems.at[j], recv_sem=recv_sems.at[j], device_id=(x, y, 1 - c),
                      device_id_type=pl.DeviceIdType.MESH)
        blk = land_ref.at[4 * px + 2 * py + c]
        sends.append(pltpu.make_async_remote_copy(src_ref=blk, dst_ref=blk, **common))
        got = land_ref.at[4 * px + 2 * py + (1 - c)]
        recvs.append(pltpu.make_async_remote_copy(src_ref=got, dst_ref=got, **common))
    return sends, recvs


def gather_start(shard, me, after, *, name):
    R, C = shard.shape
    tr = _tile(R, max(16, (512 * 1024) // C), 16)

    def place_body(me_ref, x_ref, o_ref):
        o_ref[...] = x_ref[...].astype(BF)

    land = pl.pallas_call(
        place_body, name=name + "_own",
        out_shape=jax.ShapeDtypeStruct((N_DEV, R, C), BF),
        grid_spec=pltpu.PrefetchScalarGridSpec(
            num_scalar_prefetch=1,
            grid=(R // tr,),
            in_specs=[pl.BlockSpec((tr, C), lambda i, me_ref: (i, 0))],
            out_specs=pl.BlockSpec((None, tr, C), lambda i, me_ref: (me_ref[0], i, 0)),
        ),
        compiler_params=_params("parallel"),
    )(me.reshape(1).astype(jnp.int32), shard)

    def body(land_ref, after_ref, send_sems, recv_sems, land_thru, token):
        sends, _ = _first_copies(land_ref, send_sems, recv_sems)
        for cp in sends:
            cp.start()
        token[...] = jnp.zeros_like(token)

    send_sems, recv_sems, land_thru, token = pl.pallas_call(
        body, name=name + "_s1",
        out_shape=(pltpu.SemaphoreType.DMA((_N_FIRST,)), pltpu.SemaphoreType.DMA((_N_FIRST,)),
                   pltpu.HBM(land.shape, land.dtype), jax.ShapeDtypeStruct((8, LANES), F32)),
        in_specs=(_HBM, _ANY),
        out_specs=(_SEM, _SEM, _HBM, pl.BlockSpec(memory_space=pltpu.VMEM)),
        input_output_aliases={0: 2},
        compiler_params=pltpu.CompilerParams(has_side_effects=_EFFECT),
    )(pltpu.with_memory_space_constraint(land, pltpu.HBM), after)
    return (send_sems, recv_sems, land_thru), token


def gather_mid(handle, after, *, name):
    send_sems, recv_sems, land_thru = handle

    def body(land_ref, send1, recv1, after_ref, send2, recv2, land_out, token):
        sends, recvs = _first_copies(land_ref, send1, recv1)
        for cp in sends:
            cp.wait_send()
        for cp in recvs:
            cp.wait_recv()
        seconds, _ = _second_copies(land_ref, send2, recv2)
        for cp in seconds:
            cp.start()
        token[...] = jnp.zeros_like(token)

    send2, recv2, land2, token = pl.pallas_call(
        body, name=name + "_s2",
        out_shape=(pltpu.SemaphoreType.DMA((3,)), pltpu.SemaphoreType.DMA((3,)),
                   pltpu.HBM(land_thru.shape, land_thru.dtype), jax.ShapeDtypeStruct((8, LANES), F32)),
        in_specs=(_HBM, _SEM, _SEM, _ANY),
        out_specs=(_SEM, _SEM, _HBM, pl.BlockSpec(memory_space=pltpu.VMEM)),
        input_output_aliases={0: 2},
        compiler_params=pltpu.CompilerParams(has_side_effects=_EFFECT),
    )(land_thru, send_sems, recv_sems, after)
    return (send2, recv2, land2), token


def gather_finish(handle, after, *, name):
    send2, recv2, land2 = handle

    def body(land_ref, send2, recv2, after_ref, got_ref):
        sends, recvs = _second_copies(land_ref, send2, recv2)
        for cp in sends:
            cp.wait_send()
        for cp in recvs:
            cp.wait_recv()

    return pl.pallas_call(
        body, name=name + "_w",
        out_shape=pltpu.HBM(land2.shape, land2.dtype),
        in_specs=(_HBM, _SEM, _SEM, _ANY),
        out_specs=_HBM,
        input_output_aliases={0: 0},
        compiler_params=pltpu.CompilerParams(has_side_effects=_EFFECT),
    )(land2, send2, recv2, after)


def _slab_copies(g_ref, r_ref, send_sems, recv_sems):
    x, y, c = lax.axis_index("x"), lax.axis_index("y"), lax.axis_index("c")
    me = 4 * x + 2 * y + c
    sends, recvs = [], []
    for k in range(1, N_DEV):
        px, py, pc = _flip(x, (k >> 2) & 1), _flip(y, (k >> 1) & 1), _flip(c, k & 1)
        peer = 4 * px + 2 * py + pc
        common = dict(send_sem=send_sems.at[k - 1], recv_sem=recv_sems.at[k - 1], device_id=(px, py, pc),
                      device_id_type=pl.DeviceIdType.MESH)
        sends.append(pltpu.make_async_remote_copy(src_ref=g_ref.at[peer], dst_ref=r_ref.at[me], **common))
        recvs.append(pltpu.make_async_remote_copy(src_ref=g_ref.at[peer], dst_ref=r_ref.at[peer], **common))
    return sends, recvs


def exchange_start(slabs, *, name):
    land = lax.empty(slabs.shape, slabs.dtype)

    def body(g_ref, r_ref, send_sems, recv_sems, g_thru, r_thru, token):
        sends, _ = _slab_copies(g_ref, r_ref, send_sems, recv_sems)
        for cp in sends:
            cp.start()
        token[...] = jnp.zeros_like(token)

    send_sems, recv_sems, g_thru, r_thru, token = pl.pallas_call(
        body, name=name,
        out_shape=(pltpu.SemaphoreType.DMA((N_DEV - 1,)), pltpu.SemaphoreType.DMA((N_DEV - 1,)),
                   pltpu.HBM(slabs.shape, slabs.dtype), pltpu.HBM(slabs.shape, slabs.dtype),
                   jax.ShapeDtypeStruct((8, LANES), F32)),
        in_specs=(_HBM, _HBM),
        out_specs=(_SEM, _SEM, _HBM, _HBM, pl.BlockSpec(memory_space=pltpu.VMEM)),
        input_output_aliases={0: 2, 1: 3},
        compiler_params=pltpu.CompilerParams(has_side_effects=_EFFECT),
    )(pltpu.with_memory_space_constraint(slabs, pltpu.HBM), pltpu.with_memory_space_constraint(land, pltpu.HBM))
    return (send_sems, recv_sems, g_thru, r_thru), token


def exchange_finish(handle, after, *, name):
    send_sems, recv_sems, g_thru, r_thru = handle

    def body(g_ref, r_ref, send_sems, recv_sems, after_ref, g_out, r_out):
        sends, recvs = _slab_copies(g_ref, r_ref, send_sems, recv_sems)
        for cp in sends:
            cp.wait_send()
        for cp in recvs:
            cp.wait_recv()

    return pl.pallas_call(
        body, name=name,
        out_shape=(pltpu.HBM(g_thru.shape, g_thru.dtype), pltpu.HBM(r_thru.shape, r_thru.dtype)),
        in_specs=(_HBM, _HBM, _SEM, _SEM, _ANY),
        out_specs=(_HBM, _HBM),
        input_output_aliases={0: 0, 1: 1},
        compiler_params=pltpu.CompilerParams(has_side_effects=_EFFECT),
    )(g_thru, r_thru, send_sems, recv_sems, after)


def adamw_own(own, got, me, w, m, v, layer, prev, *, name):
    P, R, C = got.shape
    L = w.shape[0]
    tr = _tile(R, max(16, (128 * 1024) // C), 16)

    def body(me_ref, own_ref, p_ref, w_ref, m_ref, v_ref, *rest):
        g_ref, d_ref, mo_ref, vo_ref = rest[-4:]
        mine = own_ref[...].astype(F32)
        g = None
        for i in range(P):
            term = jnp.where(me_ref[0] == i, mine, p_ref[i].astype(F32))
            g = term if g is None else g + term
        m_new = ADAM_B1 * m_ref[...] + (1.0 - ADAM_B1) * g
        v_new = ADAM_B2 * v_ref[...] + (1.0 - ADAM_B2) * jnp.square(g)
        m_hat = m_new / (1.0 - ADAM_B1 ** ADAM_STEP)
        v_hat = v_new / (1.0 - ADAM_B2 ** ADAM_STEP)
        g_ref[...] = g
        d_ref[...] = -ADAM_LR * (m_hat / (jnp.sqrt(v_hat) + ADAM_EPS) + ADAM_WD * w_ref[...])
        mo_ref[...] = m_new
        vo_ref[...] = v_new

    blk = pl.BlockSpec((None, tr, C), lambda i, me_ref: (layer, i, 0))
    out = jax.ShapeDtypeStruct((L, R, C), F32)
    in_specs = [pl.BlockSpec((None, tr, C), lambda i, me_ref: (me_ref[0], i, 0)),
                pl.BlockSpec((P, tr, C), lambda i, me_ref: (0, i, 0)), blk, blk, blk]
    args = [me.reshape(1).astype(jnp.int32), own, got, w, m, v]
    aliases = {}
    if prev is not None:
        in_specs += [pl.BlockSpec(memory_space=pl.ANY)] * 4
        aliases = {len(args) + i: i for i in range(4)}
        args += list(prev)
    return pl.pallas_call(
        body,
        out_shape=(out, out, out, out),
        grid_spec=pltpu.PrefetchScalarGridSpec(
            num_scalar_prefetch=1,
            grid=(R // tr,),
            in_specs=in_specs,
            out_specs=(blk, blk, blk, blk),
        ),
        input_output_aliases=aliases,
        compiler_params=_params("parallel"),
        name=name,
    )(*args)


def _pad_rows(a, rows):
    return jnp.pad(a, ((0, rows - a.shape[0]), (0, 0)))


def _pad_cols(a, cols):
    return jnp.pad(a, ((0, 0), (0, cols - a.shape[1])))


def kernel(x, meta, a_norm, a_w_in, a_conv, a_w_out, kv_norm, w_kv, k_norm, w_f, b_f, b_norm, b_w_q, b_q_norm, b_w_o, ffn_norm, ffn_w_gu, ffn_w_down, loss_target, m_meta, m_a_norm, m_a_w_in, m_a_conv, m_a_w_out, m_kv_norm, m_w_kv, m_k_norm, m_w_f, m_b_f, m_b_norm, m_b_w_q, m_b_q_norm, m_b_w_o, m_ffn_norm, m_ffn_w_gu, m_ffn_w_down, v_meta, v_a_norm, v_a_w_in, v_a_conv, v_a_w_out, v_kv_norm, v_w_kv, v_k_norm, v_w_f, v_b_f, v_b_norm, v_b_w_q, v_b_q_norm, v_b_w_o, v_ffn_norm, v_ffn_w_gu, v_ffn_w_down):
    S, D = x.shape[1], x.shape[2]
    n_meta = meta.shape[0]
    Ds = meta.shape[1]
    H = D // HEAD_DIM
    n_a, n_b = a_w_in.shape[0], b_w_q.shape[0]
    depth = n_a + n_b
    Fs = ffn_w_down.shape[1]
    pad = BLOCK - n_meta
    lead = pad + n_meta
    T = lead + S
    tk_attn = _tile(T, 384, LANES)
    nk_attn = T // tk_attn
    q_scale = 1.0 / math.sqrt(HEAD_DIM)
    my = 4 * lax.axis_index("x") + 2 * lax.axis_index("y") + lax.axis_index("c")

    wf_t = w_f.reshape(H, Ds)
    small = jnp.concatenate([meta, _pad_rows(a_norm, 8), _pad_rows(a_conv.reshape(n_a * 3, Ds), 8), wf_t], axis=0)
    r_an, r_ac, r_wf = n_meta, n_meta + 8, n_meta + 16
    gs = all_gather(small, name="ag_small")
    unshard = lambda blk: jnp.transpose(blk, (1, 0, 2)).reshape(blk.shape[1], D)
    meta_full = unshard(gs[:, 0:n_meta])
    a_norm_full = unshard(gs[:, r_an:r_an + n_a])
    a_conv_full = unshard(gs[:, r_ac:r_ac + 3 * n_a]).reshape(n_a, 3, D)
    w_f_full = gs[:, r_wf:r_wf + H].reshape(D, H)
    wf_pad = _pad_cols(w_f_full, LANES).astype(BF)[None]
    bf_pad = _pad_cols(b_f.reshape(1, H), LANES)

    def layer_shards(l):
        if l < n_a:
            mix = [(("in", l), a_w_in[l]), (("out", l), a_w_out[l])]
        else:
            j = l - n_a
            mix = ([(("kv", 0), w_kv)] if j == 0 else []) + [(("q", j), b_w_q[j]), (("o", j), b_w_o[j])]
        return mix + [(("gu", l), ffn_w_gu[l]), (("dn", l), ffn_w_down[l])]

    first_level, second_level, W = {}, {}, {}
    st = {"done": None, "tok": None}

    def note(val):
        st["done"] = val
        return val

    def take():
        tok, st["tok"] = st["tok"], None
        return tok

    def chain_after(default):
        if st["tok"] is not None:
            return st["tok"]
        return default if st["done"] is None else st["done"]

    def ag_name(key):
        return f"ag_{key[0]}{key[1]}"

    def start_layer(l):
        for key, shard in layer_shards(l):
            first_level[key], st["tok"] = gather_start(shard, my, chain_after(shard), name=ag_name(key))

    def pass_on(keys):
        for key in keys:
            second_level[key], st["tok"] = gather_mid(first_level.pop(key), chain_after(None), name=ag_name(key))

    def weight(key, shape=None):
        w = gather_finish(second_level.pop(key), st["done"], name=ag_name(key))
        W[key] = w if shape is None else w.reshape(shape)
        return W[key]

    def layer_keys(l):
        keys = [key for key, _ in layer_shards(l)]
        return keys[:-2], keys[-2:]

    h = note(jnp.concatenate([jnp.zeros((pad, D), F32), meta_full, x[0]], axis=0))
    start_layer(0)
    pass_on(layer_keys(0)[0])
    saved = []
    shared = None
    for l in range(depth):
        rec = {"h": h}
        mix_keys, ffn_keys = layer_keys(l)

        def ahead():
            if l >= 1:
                pass_on(ffn_keys[:1])
            if l + 1 < depth:
                start_layer(l + 1)

        if l < n_a:
            xn = note(rms_fwd(h, a_norm_full[l], name=f"a{l}_norm", dep=take()))
            ahead()
            proj = note(mm_nn(xn, weight(("in", l)), name=f"a{l}_in", dep=take()))
            if l == 0:
                pass_on(ffn_keys[:1])
            y = note(conv_fwd(proj, a_conv_full[l], name=f"a{l}_conv"))
            h1 = note(mm_nn(y, weight(("out", l), (1, D, D)), add=h, name=f"a{l}_out", dep=take()))
            rec.update(xn=xn, proj=proj, y=y)
        else:
            j = l - n_a
            if j == 0:
                xnk = note(rms_fwd(h, kv_norm, name="kv_norm", dep=take()))
                ahead()
                kv = note(mm_nn(xnk, weight(("kv", 0)), name="kv_proj", dep=take()))
                k, v = kv_post(kv, k_norm, name="kv_post")
                logits = mm_nn(xnk, wf_pad, name="f_logits", tn_target=LANES)
                cfull = fgate_fwd(logits, bf_pad, pad, name="f_gate")
                crow = jnp.transpose(cfull[:, :H]).reshape(H, nk_attn, 1, tk_attn)
                shared = dict(h=h, xnk=xnk, kv=kv, logits=logits)
                xn = note(rms_fwd(h, b_norm[j], name=f"b{j}_norm"))
            else:
                xn = note(rms_fwd(h, b_norm[j], name=f"b{j}_norm", dep=take()))
                ahead()
            qraw = note(mm_nn(xn, weight(("q", j), (1, D, D)), name=f"b{j}_q", dep=take()))
            q = hn_fwd(qraw, b_q_norm[j], q_scale, name=f"b{j}_qnorm")
            o, lse = attn_fwd(q, k, v, crow, pad, name=f"b{j}_attn")
            note(o)
            h1 = note(mm_nn(o, weight(("o", j), (1, D, D)), add=h, name=f"b{j}_o"))
            rec.update(xn=xn, qraw=qraw, q=q, o=o, lse=lse)
        xn2 = note(rms_fwd(h1, ffn_norm[l], name=f"f{l}_norm", dep=take()))
        pass_on(ffn_keys[1:])
        act, g_s, u_s = mm_swiglu(xn2, weight(("gu", l)), name=f"f{l}_gu", dep=take())
        note(act)
        if l + 1 < depth:
            pass_on(layer_keys(l + 1)[0])
        h = note(mm_nn(act, weight(("dn", l), (1, N_DEV * Fs, D)), add=h1, name=f"f{l}_down", tk_target=1408,
                       dep=take()))
        rec.update(h1=h1, xn2=xn2, act=act, g=g_s, u=u_s)
        saved.append(rec)

    dh, dhb, loss_tile = loss_head(h, loss_target[0], lead, name="loss")
    loss = lax.psum(loss_tile[0, 0], MESH_AXES)

    upd = {}
    small_g = {}
    inflight = []

    def big(name, l, section, slabs, w, m, v):
        handle, st["tok"] = exchange_start(slabs.reshape(N_DEV, -1, w.shape[-1]), name=f"{name}{l}_xs")
        inflight.append((section, name, l, handle, w, m, v))

    def land(sections, after):
        for entry in [e for e in inflight if sections is None or e[0] in sections]:
            inflight.remove(entry)
            _, name, l, handle, w, m, v = entry
            own, got = exchange_finish(handle, after, name=f"{name}{l}_xw")
            flat = lambda t: t.reshape(w.shape[0], -1, w.shape[-1])
            upd[name] = adamw_own(own, got, my, flat(w), flat(m), flat(v), l, upd.get(name),
                                  name=f"{name}{l}_adamw")

    dk = dv = dck = dcq = None
    for l in reversed(range(depth)):
        rec = saved[l]
        land([("ffn", l + 1)], dh)
        dgu = mm_nt_dswiglu(dhb, W[("dn", l)], rec["g"], rec["u"], name=f"f{l}_ddown")
        big("ffn_w_down", l, ("ffn", l), mm_tn(rec["act"], dhb, 1, name=f"f{l}_wdown"),
            ffn_w_down, m_ffn_w_down, v_ffn_w_down)
        big("ffn_w_gu", l, ("ffn", l), mm_tn(rec["xn2"], dgu, N_DEV, name=f"f{l}_wgu", dep=take()),
            ffn_w_gu, m_ffn_w_gu, v_ffn_w_gu)
        dxn2 = mm_nt(dgu, W[("gu", l)], name=f"f{l}_dgu", dep=take())
        dh1, dhb, dgf = rms_bwd(dxn2, rec["h1"], ffn_norm[l], dh, name=f"f{l}_dnorm")
        small_g[("ffn_norm", l)] = dgf
        land([("mix", l + 1)], dh1)
        if l < n_a:
            dy = mm_nt(dhb, W[("out", l)], name=f"a{l}_dout")
            big("a_w_out", l, ("mix", l), mm_tn(rec["y"], dhb, 1, name=f"a{l}_wout"),
                a_w_out, m_a_w_out, v_a_w_out)
            db, dc, dhh, dcw = conv_bwd(dy, rec["proj"], a_conv_full[l], name=f"a{l}_dconv", dep=take())
            small_g[("a_conv", l)] = dcw
            dproj = jnp.concatenate([db, dc, dhh], axis=1)
            big("a_w_in", l, ("mix", l), mm_tn(rec["xn"], dproj, N_DEV, name=f"a{l}_win"),
                a_w_in, m_a_w_in, v_a_w_in)
            dxn = mm_nt(dproj, W[("in", l)], name=f"a{l}_din", dep=take())
            dh, dhb, dga = rms_bwd(dxn, rec["h"], a_norm_full[l], dh1, name=f"a{l}_dnorm")
            small_g[("a_norm", l)] = dga
        else:
            j = l - n_a
            do = mm_nt(dhb, W[("o", j)], name=f"b{j}_do")
            big("b_w_o", j, ("mix", l), mm_tn(rec["o"], dhb, 1, name=f"b{j}_wo"),
                b_w_o, m_b_w_o, v_b_w_o)
            prev = None if dk is None else (dk, dv, dck, dcq)
            dq, dk, dv, dck, dcq = attn_bwd(rec["q"], k, v, do, rec["o"], rec["lse"], crow, prev, pad,
                                            name=f"b{j}_dattn", dep=take())
            dqraw, dqn = hn_bwd(dq, rec["qraw"], b_q_norm[j], q_scale, name=f"b{j}_dqnorm")
            small_g[("b_q_norm", j)] = dqn
            big("b_w_q", j, ("mix", l), mm_tn(rec["xn"], dqraw, 1, name=f"b{j}_wq"),
                b_w_q, m_b_w_q, v_b_w_q)
            dxn = mm_nt(dqraw, W[("q", j)], name=f"b{j}_dq", dep=take())
            dh, dhb, dgb = rms_bwd(dxn, rec["h"], b_norm[j], dh1, name=f"b{j}_dnorm")
            small_g[("b_norm", j)] = dgb
            if j == 0:
                dkraw, dkn = hn_bwd(dk, shared["kv"], k_norm, 1.0, name="kv_dknorm")
                dkv = jnp.concatenate([dkraw, dv.astype(BF)], axis=1)
                dc_full = _pad_cols(jnp.transpose(dck.reshape(H, T) + dcq.reshape(H, T)), LANES)
                dz, dbf = fgate_bwd(dc_full, shared["logits"], bf_pad, pad, name="f_dgate")
                big("w_kv", 0, ("mix", l), mm_tn(shared["xnk"], dkv, N_DEV, name="kv_wkv"),
                    w_kv[None], m_w_kv[None], v_w_kv[None])
                dwf_t = mm_tn(dz, shared["xnk"], 1, name="f_wf", out_dtype=F32, tn_target=1024,
                              dep=take())[0, :H]
                dxn_f = mm_nt(dz, wf_pad, name="f_dxn")
                dxnk = mm_nt(dkv, W[("kv", 0)], add=dxn_f, name="kv_dxn")
                dh, dhb, dgkv = rms_bwd(dxnk, shared["h"], kv_norm, dh, name="kv_dnorm")
    land(None, dh)

    grad_x = dh[lead:][None]

    row8 = lambda a: _pad_rows(_pad_cols(a, D), 8)
    stack = lambda key, n: jnp.concatenate([small_g[(key, i)] for i in range(n)], axis=0)
    g_sharded = jnp.concatenate([dh[pad:lead], row8(stack("a_norm", n_a)), row8(stack("a_conv", n_a)), dwf_t], axis=0)
    g_repl = jnp.concatenate([row8(jnp.concatenate([dgkv, stack("b_norm", n_b)], axis=0)),
                              row8(stack("ffn_norm", depth)),
                              row8(jnp.concatenate([_pad_cols(dkn, D), _pad_cols(stack("b_q_norm", n_b), D),
                                                    _pad_cols(dbf[:, :H], D)], axis=0))], axis=0)
    n_sh = g_sharded.shape[0]
    gathered = all_gather(jnp.concatenate([g_sharded, g_repl], axis=0), name="ag_small_grads")
    parts_sh = lax.dynamic_slice_in_dim(gathered[:, :n_sh], my * Ds, Ds, axis=2)
    parts_rp = gathered[:, n_sh:]

    def pack_sh(t_meta, t_an, t_ac, t_wf):
        return jnp.concatenate([t_meta, _pad_rows(t_an, 8), _pad_rows(t_ac.reshape(n_a * 3, Ds), 8),
                                jnp.transpose(t_wf)], axis=0)

    def pack_rp(t_kv, t_bn, t_fn, t_kn, t_qn, t_bf):
        return jnp.concatenate([row8(jnp.concatenate([t_kv.reshape(1, D), t_bn], axis=0)), row8(t_fn),
                                row8(jnp.concatenate([_pad_cols(t_kn.reshape(1, -1), D), _pad_cols(t_qn, D),
                                                      _pad_cols(t_bf.reshape(1, -1), D)], axis=0))], axis=0)

    res_sh = adamw(parts_sh, pack_sh(meta, a_norm, a_conv, w_f), pack_sh(m_meta, m_a_norm, m_a_conv, m_w_f),
                   pack_sh(v_meta, v_a_norm, v_a_conv, v_w_f), name="small_sharded_adamw")
    res_rp = adamw(parts_rp, pack_rp(kv_norm, b_norm, ffn_norm, k_norm, b_q_norm, b_f),
                   pack_rp(m_kv_norm, m_b_norm, m_ffn_norm, m_k_norm, m_b_q_norm, m_b_f),
                   pack_rp(v_kv_norm, v_b_norm, v_ffn_norm, v_k_norm, v_b_q_norm, v_b_f), name="small_repl_adamw")

    def unpack(kind):
        sh, rp = res_sh[kind], res_rp[kind]
        out = {
            "meta": sh[0:n_meta],
            "a_norm": sh[r_an:r_an + n_a],
            "a_conv": sh[r_ac:r_ac + 3 * n_a].reshape(n_a, 3, Ds),
            "w_f": jnp.transpose(sh[r_wf:r_wf + H]),
            "kv_norm": rp[0],
            "b_norm": rp[1:1 + n_b],
            "ffn_norm": rp[8:8 + depth],
            "k_norm": rp[16, :HEAD_DIM],
            "b_q_norm": rp[17:17 + n_b, :HEAD_DIM],
            "b_f": rp[17 + n_b, :H],
        }
        for name, like in (("a_w_in", a_w_in), ("a_w_out", a_w_out), ("b_w_q", b_w_q), ("b_w_o", b_w_o),
                           ("ffn_w_gu", ffn_w_gu), ("ffn_w_down", ffn_w_down)):
            out[name] = upd[name][kind].reshape(like.shape)
        out["w_kv"] = upd["w_kv"][kind].reshape(w_kv.shape)
        return out

    order = ["meta", "a_norm", "a_w_in", "a_conv", "a_w_out", "kv_norm", "w_kv", "k_norm", "w_f", "b_f",
             "b_norm", "b_w_q", "b_q_norm", "b_w_o", "ffn_norm", "ffn_w_gu", "ffn_w_down"]
    outs = [loss, grad_x]
    for kind in range(4):
        vals = unpack(kind)
        outs += [vals[n] for n in order]
    return tuple(outs)
```

```python
import functools
import math

import jax
import jax.numpy as jnp
from jax import lax
from jax.experimental import pallas as pl
from jax.experimental.pallas import tpu as pltpu

N_DEV = 8
MESH_AXES = ("x", "y", "c")
EPS = 1e-6
NEG = -1e30
HEAD_DIM = 128
BLOCK = 128
LANES = 128
V7X_VMEM_LIMIT = 56 * 1024 * 1024

ADAM_LR = 0.001
ADAM_B1 = 0.9
ADAM_B2 = 0.999
ADAM_EPS = 1e-08
ADAM_WD = 0.01
ADAM_STEP = 10

BF = jnp.bfloat16
F32 = jnp.float32


def _tile(n, target, mult):
    best = None
    for t in range(mult, min(n, target) + 1, mult):
        if n % t == 0:
            best = t
    return n if best is None else best


def _params(*sem):
    return pltpu.CompilerParams(dimension_semantics=sem, vmem_limit_bytes=V7X_VMEM_LIMIT)


def _with_dep(body, in_specs, args, dep):
    if dep is None:
        return body, list(in_specs), list(args)
    n_in = len(args)

    def body_dep(*refs):
        body(*refs[:n_in], *refs[n_in + 1:])

    return body_dep, list(in_specs) + [pl.BlockSpec(memory_space=pl.ANY)], list(args) + [dep]


def mm_nn(a, w, *, name, add=None, dep=None, out_dtype=F32, tm_target=1056, tn_target=1024, tk_target=2048,
          resident=False):
    M, K = a.shape
    G, K2, n = w.shape
    assert K == K2
    tm = _tile(M, tm_target, 16)
    tn = _tile(n, tn_target, LANES)
    tk = K if resident else _tile(K, tk_target, LANES)
    nj, nk = n // tn, K // tk
    has_add = add is not None
    if resident:
        grid, sem = (G * nj, M // tm), ("parallel", "parallel")
        order = lambda f: (lambda j, i: f(i, j, 0))
        w_mode = dict(pipeline_mode=pl.Buffered(1))
    else:
        grid, sem = (M // tm, G * nj, nk), ("parallel", "parallel", "arbitrary")
        order = lambda f: f
        w_mode = {}

    def body(*refs):
        if has_add:
            a_ref, w_ref, add_ref, o_ref = refs[:4]
        else:
            a_ref, w_ref, o_ref = refs[:3]
            add_ref = None

        def finish(r):
            if has_add:
                r = r + add_ref[...]
            o_ref[...] = r.astype(out_dtype)

        part = jnp.dot(a_ref[...], w_ref[...], preferred_element_type=F32)
        if nk == 1:
            finish(part)
        else:
            acc_ref = refs[-1]
            k = pl.program_id(2)

            @pl.when(k == 0)
            def _():
                acc_ref[...] = part

            @pl.when(k > 0)
            def _():
                acc_ref[...] += part

            @pl.when(k == nk - 1)
            def _():
                finish(acc_ref[...])

    in_specs = [
        pl.BlockSpec((tm, tk), order(lambda i, j, k: (i, k))),
        pl.BlockSpec((None, tk, tn), order(lambda i, j, k: (j // nj, k, j % nj)), **w_mode),
    ]
    args = [a, w]
    if has_add:
        in_specs.append(pl.BlockSpec((tm, tn), order(lambda i, j, k: (i, j))))
        args.append(add)
    body, in_specs, args = _with_dep(body, in_specs, args, dep)
    return pl.pallas_call(
        body,
        out_shape=jax.ShapeDtypeStruct((M, G * n), out_dtype),
        grid=grid,
        in_specs=in_specs,
        out_specs=pl.BlockSpec((tm, tn), order(lambda i, j, k: (i, j))),
        scratch_shapes=[pltpu.VMEM((tm, tn), F32)] if nk > 1 else [],
        compiler_params=_params(*sem),
        name=name,
    )(*args)


def mm_swiglu(xn, wgu, *, name, dep=None, save_dtype=BF, tm_target=528):
    M, K = xn.shape
    G, _, n = wgu.shape
    half = G // 2
    tm = _tile(M, tm_target, 16)
    tn = _tile(n, 1408, LANES)
    nj = n // tn
    Fh = half * n

    def body(a_ref, wg_ref, wu_ref, act_ref, g_ref, u_ref):
        a = a_ref[...]
        g = jnp.dot(a, wg_ref[...], preferred_element_type=F32)
        g_ref[...] = g.astype(save_dtype)
        silu = g * jax.nn.sigmoid(g)
        u = jnp.dot(a, wu_ref[...], preferred_element_type=F32)
        u_ref[...] = u.astype(save_dtype)
        act_ref[...] = (silu * u).astype(BF)

    out_block = pl.BlockSpec((tm, tn), lambda j, i: (i, j))
    once = pl.Buffered(1)
    body, in_specs, args = _with_dep(body, [
        pl.BlockSpec((tm, K), lambda j, i: (i, 0)),
        pl.BlockSpec((None, K, tn), lambda j, i: (j // nj, 0, j % nj), pipeline_mode=once),
        pl.BlockSpec((None, K, tn), lambda j, i: (half + j // nj, 0, j % nj), pipeline_mode=once),
    ], [xn, wgu, wgu], dep)
    return pl.pallas_call(
        body,
        out_shape=(jax.ShapeDtypeStruct((M, Fh), BF),
                   jax.ShapeDtypeStruct((M, Fh), save_dtype),
                   jax.ShapeDtypeStruct((M, Fh), save_dtype)),
        grid=(half * nj, M // tm),
        in_specs=in_specs,
        out_specs=(out_block, out_block, out_block),
        compiler_params=_params("parallel", "parallel"),
        name=name,
    )(*args)


def mm_nt(dy, w, *, name, add=None, dep=None, out_dtype=F32, tm_target=1056, tko_target=1024, tc_target=2048):
    if dy.ndim == 2:
        dy = dy.reshape(1, *dy.shape)
    P, M, Np = dy.shape
    G, K, n = w.shape
    assert P * Np == G * n
    tm = _tile(M, tm_target, 16)
    tko = _tile(K, tko_target, LANES)
    tc = _tile(n, tc_target, LANES)
    nc = n // tc
    steps = G * nc
    assert Np % tc == 0
    per_part = Np // tc
    has_add = add is not None

    def body(*refs):
        if has_add:
            dy_ref, w_ref, add_ref, o_ref = refs[:4]
        else:
            dy_ref, w_ref, o_ref = refs[:3]
            add_ref = None

        def finish(r):
            if has_add:
                r = r + add_ref[...]
            o_ref[...] = r.astype(out_dtype)

        part = lax.dot_general(dy_ref[...], w_ref[...], (((1,), (1,)), ((), ())),
                               preferred_element_type=F32)
        if steps == 1:
            finish(part)
        else:
            acc_ref = refs[-1]
            s = pl.program_id(2)

            @pl.when(s == 0)
            def _():
                acc_ref[...] = part

            @pl.when(s > 0)
            def _():
                acc_ref[...] += part

            @pl.when(s == steps - 1)
            def _():
                finish(acc_ref[...])

    in_specs = [
        pl.BlockSpec((None, tm, tc), lambda i, o, s: (s // per_part, i, s % per_part)),
        pl.BlockSpec((None, tko, tc), lambda i, o, s: (s // nc, o, s % nc)),
    ]
    args = [dy, w]
    if has_add:
        in_specs.append(pl.BlockSpec((tm, tko), lambda i, o, s: (i, o)))
        args.append(add)
    body, in_specs, args = _with_dep(body, in_specs, args, dep)
    return pl.pallas_call(
        body,
        out_shape=jax.ShapeDtypeStruct((M, K), out_dtype),
        grid=(M // tm, K // tko, steps),
        in_specs=in_specs,
        out_specs=pl.BlockSpec((tm, tko), lambda i, o, s: (i, o)),
        scratch_shapes=[pltpu.VMEM((tm, tko), F32)] if steps > 1 else [],
        compiler_params=_params("parallel", "parallel", "arbitrary"),
        name=name,
    )(*args)


def mm_nt_dswiglu(dh, w_down, g_s, u_s, *, name, tm_target=1056, tf_target=512):
    M, D = dh.shape
    _, Fh, D2 = w_down.shape
    assert D == D2
    tm = _tile(M, tm_target, 16)
    tf = _tile(Fh, tf_target, LANES)

    def body(dh_ref, w_ref, g_ref, u_ref, dgu_ref):
        dact = lax.dot_general(dh_ref[...], w_ref[...], (((1,), (1,)), ((), ())),
                               preferred_element_type=F32)
        g = g_ref[...].astype(F32)
        u = u_ref[...].astype(F32)
        sig = jax.nn.sigmoid(g)
        dgu_ref[1] = (dact * (g * sig)).astype(BF)
        dgu_ref[0] = (dact * u * (sig * (1.0 + g * (1.0 - sig)))).astype(BF)

    blk = pl.BlockSpec((tm, tf), lambda i, f: (i, f))
    return pl.pallas_call(
        body,
        out_shape=jax.ShapeDtypeStruct((2, M, Fh), BF),
        grid=(M // tm, Fh // tf),
        in_specs=[
            pl.BlockSpec((tm, D), lambda i, f: (i, 0)),
            pl.BlockSpec((None, tf, D), lambda i, f: (0, f, 0)),
            blk, blk,
        ],
        out_specs=pl.BlockSpec((2, tm, tf), lambda i, f: (0, i, f)),
        compiler_params=_params("parallel", "parallel"),
        name=name,
    )(dh, w_down, g_s, u_s)


def mm_tn(a, dy, groups, *, name, dep=None, out_dtype=BF, tk_target=512, tn_target=1408):
    M, K = a.shape
    if dy.ndim == 2:
        dy = dy.reshape(1, *dy.shape)
    P, M2, Np = dy.shape
    N = P * Np
    assert M == M2 and N % groups == 0
    n = N // groups
    tk = _tile(K, tk_target, LANES)
    tn = _tile(n, tn_target, LANES)
    nj = n // tn
    assert Np % tn == 0
    per_part = Np // tn

    def body(a_ref, dy_ref, o_ref):
        o_ref[...] = lax.dot_general(a_ref[...], dy_ref[...], (((0,), (0,)), ((), ())),
                                     preferred_element_type=F32).astype(out_dtype)

    body, in_specs, args = _with_dep(body, [
        pl.BlockSpec((M, tk), lambda i, j: (0, i)),
        pl.BlockSpec((None, M, tn), lambda i, j: (j // per_part, 0, j % per_part)),
    ], [a, dy], dep)
    return pl.pallas_call(
        body,
        out_shape=jax.ShapeDtypeStruct((groups, K, n), out_dtype),
        grid=(K // tk, groups * nj),
        in_specs=in_specs,
        out_specs=pl.BlockSpec((None, tk, tn), lambda i, j: (j // nj, i, j % nj)),
        compiler_params=_params("parallel", "parallel"),
        name=name,
    )(*args)


def rms_fwd(h, g, *, name, dep=None):
    T, D = h.shape
    tm = _tile(T, 528, 16)

    def body(h_ref, g_ref, o_ref):
        x = h_ref[...]
        r = lax.rsqrt(jnp.mean(x * x, axis=-1, keepdims=True) + EPS)
        o_ref[...] = ((x * r) * g_ref[...]).astype(BF)

    body, in_specs, args = _with_dep(
        body, [pl.BlockSpec((tm, D), lambda i: (i, 0)), pl.BlockSpec((1, D), lambda i: (0, 0))],
        [h, g.reshape(1, D)], dep)
    return pl.pallas_call(
        body,
        out_shape=jax.ShapeDtypeStruct((T, D), BF),
        grid=(T // tm,),
        in_specs=in_specs,
        out_specs=pl.BlockSpec((tm, D), lambda i: (i, 0)),
        compiler_params=_params("parallel"),
        name=name,
    )(*args)


def rms_bwd(dxn, h, g, add, *, name):
    T, D = h.shape
    tm = _tile(T, 264, 16)

    def body(dxn_ref, h_ref, g_ref, add_ref, dh_ref, dhb_ref, dg_ref):
        x = h_ref[...]
        dy = dxn_ref[...]
        r = lax.rsqrt(jnp.mean(x * x, axis=-1, keepdims=True) + EPS)
        xhat = x * r
        part = jnp.sum(dy * xhat, axis=0, keepdims=True)

        @pl.when(pl.program_id(0) == 0)
        def _():
            dg_ref[...] = part

        @pl.when(pl.program_id(0) > 0)
        def _():
            dg_ref[...] += part

        dxh = dy * g_ref[...]
        dh = add_ref[...] + r * (dxh - xhat * jnp.mean(dxh * xhat, axis=-1, keepdims=True))
        dh_ref[...] = dh
        dhb_ref[...] = dh.astype(BF)

    row = pl.BlockSpec((tm, D), lambda i: (i, 0))
    vec = pl.BlockSpec((1, D), lambda i: (0, 0))
    return pl.pallas_call(
        body,
        out_shape=(jax.ShapeDtypeStruct((T, D), F32), jax.ShapeDtypeStruct((T, D), BF),
                   jax.ShapeDtypeStruct((1, D), F32)),
        grid=(T // tm,),
        in_specs=[row, row, vec, row],
        out_specs=(row, row, vec),
        compiler_params=_params("arbitrary"),
        name=name,
    )(dxn, h, g.reshape(1, D), add)


def _head_norm(x, gain):
    r = lax.rsqrt(jnp.mean(x * x, axis=-1, keepdims=True) + EPS)
    return (x * r) * gain


def hn_fwd(qraw, gain, out_scale, *, name):
    T, D = qraw.shape
    H = D // HEAD_DIM
    tm = _tile(T, 528, 16)

    def body(q_ref, g_ref, o_ref):
        gain_v = g_ref[...]
        for hd in range(H):
            sl = slice(hd * HEAD_DIM, (hd + 1) * HEAD_DIM)
            o_ref[:, sl] = (_head_norm(q_ref[:, sl], gain_v) * out_scale).astype(BF)

    return pl.pallas_call(
        body,
        out_shape=jax.ShapeDtypeStruct((T, D), BF),
        grid=(T // tm,),
        in_specs=[pl.BlockSpec((tm, D), lambda i: (i, 0)),
                  pl.BlockSpec((1, HEAD_DIM), lambda i: (0, 0))],
        out_specs=pl.BlockSpec((tm, D), lambda i: (i, 0)),
        compiler_params=_params("parallel"),
        name=name,
    )(qraw, gain.reshape(1, HEAD_DIM))


def kv_post(kv, gain, *, name):
    T, D2 = kv.shape
    D = D2 // 2
    H = D // HEAD_DIM
    tm = _tile(T, 528, 16)

    def body(k_ref, v_ref, g_ref, ko_ref, vo_ref):
        gain_v = g_ref[...]
        for hd in range(H):
            sl = slice(hd * HEAD_DIM, (hd + 1) * HEAD_DIM)
            ko_ref[:, sl] = _head_norm(k_ref[:, sl], gain_v).astype(BF)
        vo_ref[...] = v_ref[...].astype(BF)

    blk = pl.BlockSpec((tm, D), lambda i: (i, 0))
    return pl.pallas_call(
        body,
        out_shape=(jax.ShapeDtypeStruct((T, D), BF), jax.ShapeDtypeStruct((T, D), BF)),
        grid=(T // tm,),
        in_specs=[blk, pl.BlockSpec((tm, D), lambda i: (i, 1)),
                  pl.BlockSpec((1, HEAD_DIM), lambda i: (0, 0))],
        out_specs=(blk, blk),
        compiler_params=_params("parallel"),
        name=name,
    )(kv, kv, gain.reshape(1, HEAD_DIM))


def hn_bwd(dq, qraw, gain, out_scale, *, name):
    T, D = dq.shape
    H = D // HEAD_DIM
    tm = _tile(T, 264, 16)

    def body(dq_ref, q_ref, g_ref, o_ref, dg_ref):
        gain_v = g_ref[...]
        part = jnp.zeros((1, HEAD_DIM), F32)
        for hd in range(H):
            sl = slice(hd * HEAD_DIM, (hd + 1) * HEAD_DIM)
            x = q_ref[:, sl]
            dy = dq_ref[:, sl] * out_scale
            r = lax.rsqrt(jnp.mean(x * x, axis=-1, keepdims=True) + EPS)
            xhat = x * r
            part = part + jnp.sum(dy * xhat, axis=0, keepdims=True)
            dxh = dy * gain_v
            o_ref[:, sl] = (r * (dxh - xhat * jnp.mean(dxh * xhat, axis=-1, keepdims=True))).astype(BF)

        @pl.when(pl.program_id(0) == 0)
        def _():
            dg_ref[...] = part

        @pl.when(pl.program_id(0) > 0)
        def _():
            dg_ref[...] += part

    blk = pl.BlockSpec((tm, D), lambda i: (i, 0))
    vec = pl.BlockSpec((1, HEAD_DIM), lambda i: (0, 0))
    return pl.pallas_call(
        body,
        out_shape=(jax.ShapeDtypeStruct((T, D), BF), jax.ShapeDtypeStruct((1, HEAD_DIM), F32)),
        grid=(T // tm,),
        in_specs=[blk, blk, vec],
        out_specs=(blk, vec),
        compiler_params=_params("arbitrary"),
        name=name,
    )(dq, qraw, gain.reshape(1, HEAD_DIM))


def _shift_down(cur, above, k, rowc):
    out = pltpu.roll(cur, k, 0)
    for i in range(k):
        out = jnp.where(rowc == i, above[8 - k + i:8 - k + i + 1], out)
    return out


def _shift_up(cur, below, k, rowc):
    R = cur.shape[0]
    out = pltpu.roll(cur, R - k, 0)
    for i in range(k):
        out = jnp.where(rowc == R - k + i, below[i:i + 1], out)
    return out


def _conv3(u, u_above, wv, rowc):
    u1 = _shift_down(u, u_above, 1, rowc)
    u2 = _shift_down(u, u_above, 2, rowc)
    return wv[0:1] * u2 + wv[1:2] * u1 + wv[2:3] * u, u1, u2


def conv_fwd(proj, w, *, name):
    T, D3 = proj.shape
    D = D3 // 3
    tc = LANES if D % LANES == 0 else D
    nb = D // tc
    R = _tile(T, 264, 8)

    def body(b_ref, c_ref, h_ref, w_ref, y_ref):
        rowc = lax.broadcasted_iota(jnp.int32, (R, 1), 0)
        wv = w_ref[...]
        for r0 in range(0, T, R):
            rows = slice(r0, r0 + R)
            u = c_ref[rows, :] * h_ref[rows, :]
            if r0 == 0:
                above = jnp.zeros((8, tc), F32)
            else:
                above = c_ref[r0 - 8:r0, :] * h_ref[r0 - 8:r0, :]
            conv, _, _ = _conv3(u, above, wv, rowc)
            y_ref[rows, :] = (b_ref[rows, :] * conv).astype(BF)

    return pl.pallas_call(
        body,
        out_shape=jax.ShapeDtypeStruct((T, D), BF),
        grid=(nb,),
        in_specs=[
            pl.BlockSpec((T, tc), lambda j: (0, j)),
            pl.BlockSpec((T, tc), lambda j: (0, nb + j)),
            pl.BlockSpec((T, tc), lambda j: (0, 2 * nb + j)),
            pl.BlockSpec((3, tc), lambda j: (0, j)),
        ],
        out_specs=pl.BlockSpec((T, tc), lambda j: (0, j)),
        compiler_params=_params("parallel"),
        name=name,
    )(proj, proj, proj, w)


def conv_bwd(dy, proj, w, *, name, dep=None):
    T, D = dy.shape
    tc = LANES if D % LANES == 0 else D
    nb = D // tc
    R = _tile(T, 264, 8)

    def body(dy_ref, b_ref, c_ref, h_ref, w_ref, db_ref, dc_ref, dh_ref, dw_ref):
        rowc = lax.broadcasted_iota(jnp.int32, (R, 1), 0)
        wv = w_ref[...]
        dw = [jnp.zeros((1, tc), F32) for _ in range(3)]
        for r0 in range(0, T, R):
            rows = slice(r0, r0 + R)
            c = c_ref[rows, :]
            hh = h_ref[rows, :]
            u = c * hh
            if r0 == 0:
                above = jnp.zeros((8, tc), F32)
            else:
                above = c_ref[r0 - 8:r0, :] * h_ref[r0 - 8:r0, :]
            conv, u1, u2 = _conv3(u, above, wv, rowc)
            dyv = dy_ref[rows, :]
            db_ref[rows, :] = (dyv * conv).astype(BF)
            dconv = dyv * b_ref[rows, :]
            if r0 + R == T:
                below = jnp.zeros((8, tc), F32)
            else:
                below = dy_ref[r0 + R:r0 + R + 8, :] * b_ref[r0 + R:r0 + R + 8, :]
            dw[0] = dw[0] + jnp.sum(dconv * u2, axis=0, keepdims=True)
            dw[1] = dw[1] + jnp.sum(dconv * u1, axis=0, keepdims=True)
            dw[2] = dw[2] + jnp.sum(dconv * u, axis=0, keepdims=True)
            du = (wv[2:3] * dconv + wv[1:2] * _shift_up(dconv, below, 1, rowc)
                  + wv[0:1] * _shift_up(dconv, below, 2, rowc))
            dc_ref[rows, :] = (du * hh).astype(BF)
            dh_ref[rows, :] = (du * c).astype(BF)
        for i in range(3):
            dw_ref[i:i + 1, :] = dw[i]

    strip = pl.BlockSpec((T, tc), lambda j: (0, j))
    wblk = pl.BlockSpec((3, tc), lambda j: (0, j))
    out = jax.ShapeDtypeStruct((T, D), BF)
    body, in_specs, args = _with_dep(body, [
        strip,
        pl.BlockSpec((T, tc), lambda j: (0, j)),
        pl.BlockSpec((T, tc), lambda j: (0, nb + j)),
        pl.BlockSpec((T, tc), lambda j: (0, 2 * nb + j)),
        wblk,
    ], [dy, proj, proj, proj, w], dep)
    return pl.pallas_call(
        body,
        out_shape=(out, out, out, jax.ShapeDtypeStruct((3, D), F32)),
        grid=(nb,),
        in_specs=in_specs,
        out_specs=(strip, strip, strip, wblk),
        compiler_params=_params("parallel"),
        name=name,
    )(*args)


def _log_sigmoid(z):
    return jnp.minimum(z, 0.0) - jnp.log(1.0 + jnp.exp(-jnp.abs(z)))


def fgate_fwd(logits, bias, pad, *, name):
    T, W = logits.shape
    cb = _tile(T, 128, 8)
    nblk = T // cb

    def body(z_ref, b_ref, c_ref, lf_ref):
        row = lax.broadcasted_iota(jnp.int32, (T, 1), 0)
        lf_ref[...] = jnp.where(row >= pad, _log_sigmoid(z_ref[...] + b_ref[...]), 0.0)
        ri = lax.broadcasted_iota(jnp.int32, (cb, cb), 0)
        ci = lax.broadcasted_iota(jnp.int32, (cb, cb), 1)
        tri = (ci <= ri).astype(F32)

        def step(i, carry):
            rows = pl.ds(pl.multiple_of(i * cb, cb), cb)
            blk = lf_ref[rows, :]
            c_ref[rows, :] = carry + jnp.dot(tri, blk, precision=lax.Precision.HIGHEST,
                                             preferred_element_type=F32)
            return carry + jnp.sum(blk, axis=0, keepdims=True)

        lax.fori_loop(0, nblk, step, jnp.zeros((1, W), F32))

    return pl.pallas_call(
        body,
        out_shape=jax.ShapeDtypeStruct((T, W), F32),
        in_specs=[pl.BlockSpec(memory_space=pltpu.VMEM), pl.BlockSpec(memory_space=pltpu.VMEM)],
        out_specs=pl.BlockSpec(memory_space=pltpu.VMEM),
        scratch_shapes=[pltpu.VMEM((T, W), F32)],
        compiler_params=pltpu.CompilerParams(vmem_limit_bytes=V7X_VMEM_LIMIT),
        name=name,
    )(logits, bias)


def fgate_bwd(dc, logits, bias, pad, *, name):
    T, W = logits.shape
    cb = _tile(T, 128, 8)
    nblk = T // cb

    def body(dc_ref, z_ref, b_ref, dz_ref, db_ref, rs_ref):
        ri = lax.broadcasted_iota(jnp.int32, (cb, cb), 0)
        ci = lax.broadcasted_iota(jnp.int32, (cb, cb), 1)
        triu = (ci >= ri).astype(F32)

        def step(i, carry):
            rows = pl.ds(pl.multiple_of((nblk - 1 - i) * cb, cb), cb)
            blk = dc_ref[rows, :]
            rs_ref[rows, :] = carry + jnp.dot(triu, blk, precision=lax.Precision.HIGHEST,
                                              preferred_element_type=F32)
            return carry + jnp.sum(blk, axis=0, keepdims=True)

        lax.fori_loop(0, nblk, step, jnp.zeros((1, W), F32))
        row = lax.broadcasted_iota(jnp.int32, (T, 1), 0)
        z = z_ref[...] + b_ref[...]
        dz = jnp.where(row >= pad, rs_ref[...] * jax.nn.sigmoid(-z), 0.0)
        dz_ref[...] = dz.astype(BF)
        db_ref[...] = jnp.sum(dz, axis=0, keepdims=True)

    vm = pl.BlockSpec(memory_space=pltpu.VMEM)
    return pl.pallas_call(
        body,
        out_shape=(jax.ShapeDtypeStruct((T, W), BF), jax.ShapeDtypeStruct((1, W), F32)),
        in_specs=[vm, vm, vm],
        out_specs=(vm, vm),
        scratch_shapes=[pltpu.VMEM((T, W), F32)],
        compiler_params=pltpu.CompilerParams(vmem_limit_bytes=V7X_VMEM_LIMIT),
        name=name,
    )(dc, logits, bias)


def _scores(qb, kb, ck):
    return lax.dot_general(qb, kb, (((1,), (1,)), ((), ())), preferred_element_type=F32) - ck


def _causal(s, row, col, pad):
    return jnp.where((col <= row) & (col >= pad), s, NEG)


def attn_fwd(q, k, v, crow, pad, *, name):
    T, D = q.shape
    H = D // HEAD_DIM
    nk, tk = crow.shape[1], crow.shape[3]
    tq = tk
    nq = T // tq

    hp = 2 if H % 2 == 0 else 1
    wide = hp * HEAD_DIM

    def body(q_ref, k_ref, v_ref, cr_ref, o_ref, lse_ref):
        qi = pl.program_id(1)
        row = qi * tq + lax.broadcasted_iota(jnp.int32, (tq, 1), 0)
        heads = [slice(a * HEAD_DIM, (a + 1) * HEAD_DIM) for a in range(hp)]
        qbs = [q_ref[:, sl] for sl in heads]

        def step(kc, carry, masked):
            rows = pl.ds(pl.multiple_of(kc * tk, tk), tk)
            out = []
            for a, sl in enumerate(heads):
                m, l, acc = carry[a]
                s = _scores(qbs[a], k_ref[rows, sl], cr_ref[a, kc])
                if masked:
                    s = _causal(s, row, kc * tk + lax.broadcasted_iota(jnp.int32, (1, tk), 1), pad)
                m_new = jnp.maximum(m, jnp.max(s, axis=-1, keepdims=True))
                alpha = jnp.exp(m - m_new)
                p = jnp.exp(s - m_new)
                l = alpha * l + jnp.sum(p, axis=-1, keepdims=True)
                acc = alpha * acc + jnp.dot(p.astype(BF), v_ref[rows, sl], preferred_element_type=F32)
                out.append((m_new, l, acc))
            return tuple(out)

        init = tuple((jnp.full((tq, 1), NEG, F32), jnp.zeros((tq, 1), F32), jnp.zeros((tq, HEAD_DIM), F32))
                     for _ in heads)
        carry = step(0, init, True)
        carry = lax.fori_loop(1, qi, lambda kc, c: step(kc, c, False), carry)
        carry = lax.cond(qi > 0, lambda c: step(qi, c, True), lambda c: c, carry)
        valid = row >= pad
        for a, sl in enumerate(heads):
            m, l, acc = carry[a]
            o_ref[:, sl] = jnp.where(valid, acc / l, 0.0).astype(BF)
            lse_ref[a] = jnp.where(valid, m + jnp.log(l), 0.0)

    return pl.pallas_call(
        body,
        out_shape=(jax.ShapeDtypeStruct((T, D), BF), jax.ShapeDtypeStruct((H, T, 1), F32)),
        grid=(H // hp, nq),
        in_specs=[
            pl.BlockSpec((tq, wide), lambda h, i: (i, h)),
            pl.BlockSpec((T, wide), lambda h, i: (0, h)),
            pl.BlockSpec((T, wide), lambda h, i: (0, h)),
            pl.BlockSpec((hp, nk, 1, tk), lambda h, i: (h, 0, 0, 0)),
        ],
        out_specs=(pl.BlockSpec((tq, wide), lambda h, i: (i, h)),
                   pl.BlockSpec((hp, tq, 1), lambda h, i: (h, i, 0))),
        compiler_params=_params("parallel", "arbitrary"),
        name=name,
    )(q, k, v, crow)


def attn_bwd(q, k, v, do, o, lse, crow, prev, pad, *, name, dep=None):
    T, D = q.shape
    H = D // HEAD_DIM
    nk, tk = crow.shape[1], crow.shape[3]
    tq = tk
    nq = T // tq
    has_prev = prev is not None

    def body(*refs):
        q_ref, k_ref, v_ref, do_ref, o_ref, lse_ref, cr_ref = refs[:7]
        refs = refs[7:]
        if has_prev:
            pk_ref, pv_ref, pc_ref, pq_ref = refs[:4]
            refs = refs[4:]
        dq_ref, dk_ref, dv_ref, dck_ref, dcq_ref, delta_ref = refs
        kc = pl.program_id(1)

        @pl.when(kc == 0)
        def _():
            dq_ref[...] = jnp.zeros_like(dq_ref)
            dcq_ref[...] = pq_ref[...] if has_prev else jnp.zeros_like(dcq_ref)
            do_used = do_ref[...].astype(BF).astype(F32)
            delta_ref[...] = jnp.sum(do_used * o_ref[...].astype(F32), axis=-1, keepdims=True)

        kb = k_ref[...]
        vb = v_ref[...]
        ck = cr_ref[...]
        col = kc * tk + lax.broadcasted_iota(jnp.int32, (1, tk), 1)

        def step(qi, carry, masked):
            dk, dv, dck = carry
            rows = pl.ds(pl.multiple_of(qi * tq, tq), tq)
            qb = q_ref[rows, :]
            dob = do_ref[rows, :].astype(BF)
            s = _scores(qb, kb, ck)
            if masked:
                s = _causal(s, qi * tq + lax.broadcasted_iota(jnp.int32, (tq, 1), 0), col, pad)
            p = jnp.exp(s - lse_ref[rows, :])
            dp = lax.dot_general(dob, vb, (((1,), (1,)), ((), ())), preferred_element_type=F32)
            ds = p * (dp - delta_ref[rows, :])
            dsb = ds.astype(BF)
            dv = dv + lax.dot_general(p.astype(BF), dob, (((0,), (0,)), ((), ())),
                                      preferred_element_type=F32)
            dk = dk + lax.dot_general(dsb, qb, (((0,), (0,)), ((), ())), preferred_element_type=F32)
            dq_ref[rows, :] += jnp.dot(dsb, kb, preferred_element_type=F32)
            dcq_ref[rows, :] += jnp.sum(ds, axis=1, keepdims=True)
            dck = dck - jnp.sum(ds, axis=0, keepdims=True)
            return dk, dv, dck

        def rest(masked):
            return lambda c: lax.fori_loop(kc + 1, nq, lambda qi, cc: step(qi, cc, masked), c)

        init = (jnp.zeros((tk, HEAD_DIM), F32), jnp.zeros((tk, HEAD_DIM), F32), jnp.zeros((1, tk), F32))
        carry = step(kc, init, True)
        dk, dv, dck = lax.cond(kc == 0, rest(True), rest(False), carry)
        if has_prev:
            dk = dk + pk_ref[...]
            dv = dv + pv_ref[...]
            dck = dck + pc_ref[...]
        dk_ref[...] = dk
        dv_ref[...] = dv
        dck_ref[...] = dck

    head_all = pl.BlockSpec((T, HEAD_DIM), lambda h, j: (0, h))
    head_blk = pl.BlockSpec((tk, HEAD_DIM), lambda h, j: (j, h))
    col_all = pl.BlockSpec((None, T, 1), lambda h, j: (h, 0, 0))
    row_blk = pl.BlockSpec((None, None, 1, tk), lambda h, j: (h, j, 0, 0))
    in_specs = [head_all, head_blk, head_blk, head_all, head_all, col_all, row_blk]
    args = [q, k, v, do, o, lse, crow]
    if has_prev:
        in_specs += [head_blk, head_blk, row_blk, col_all]
        args += list(prev)
    body, in_specs, args = _with_dep(body, in_specs, args, dep)
    return pl.pallas_call(
        body,
        out_shape=(jax.ShapeDtypeStruct((T, D), F32), jax.ShapeDtypeStruct((T, D), F32),
                   jax.ShapeDtypeStruct((T, D), F32), jax.ShapeDtypeStruct((H, nk, 1, tk), F32),
                   jax.ShapeDtypeStruct((H, T, 1), F32)),
        grid=(H, nk),
        in_specs=in_specs,
        out_specs=(head_all, head_blk, head_blk, row_blk, col_all),
        scratch_shapes=[pltpu.VMEM((T, 1), F32)],
        compiler_params=_params("parallel", "arbitrary"),
        name=name,
    )(*args)


def loss_head(h, target, lead, *, name):
    T, D = h.shape
    tm = lead
    assert T % tm == 0 and target.shape[0] % tm == 0
    inv_d = 1.0 / D

    def body(h_ref, t_ref, dh_ref, dhb_ref, loss_ref):
        i = pl.program_id(0)

        @pl.when(i == 0)
        def _():
            dh_ref[...] = jnp.zeros_like(dh_ref)
            dhb_ref[...] = jnp.zeros_like(dhb_ref)
            loss_ref[...] = jnp.zeros_like(loss_ref)

        @pl.when(i > 0)
        def _():
            e = h_ref[...] - t_ref[...]
            dh = e * inv_d
            dh_ref[...] = dh
            dhb_ref[...] = dh.astype(BF)
            loss_ref[...] += 0.5 * inv_d * jnp.sum(e * e)

    return pl.pallas_call(
        body,
        out_shape=(jax.ShapeDtypeStruct((T, D), F32), jax.ShapeDtypeStruct((T, D), BF),
                   jax.ShapeDtypeStruct((8, LANES), F32)),
        grid=(T // tm,),
        in_specs=[pl.BlockSpec((tm, D), lambda i: (i, 0)),
                  pl.BlockSpec((tm, D), lambda i: (jnp.maximum(i - 1, 0), 0))],
        out_specs=(pl.BlockSpec((tm, D), lambda i: (i, 0)), pl.BlockSpec((tm, D), lambda i: (i, 0)),
                   pl.BlockSpec((8, LANES), lambda i: (0, 0))),
        compiler_params=_params("arbitrary"),
        name=name,
    )(h, target)


def adamw(parts, w, m, v, *, name):
    P, R, C = parts.shape
    tr = _tile(R, max(16, (128 * 1024) // C), 16)

    def body(p_ref, w_ref, m_ref, v_ref, g_ref, d_ref, mo_ref, vo_ref):
        g = p_ref[0].astype(F32)
        for i in range(1, P):
            g = g + p_ref[i].astype(F32)
        m_new = ADAM_B1 * m_ref[...] + (1.0 - ADAM_B1) * g
        v_new = ADAM_B2 * v_ref[...] + (1.0 - ADAM_B2) * jnp.square(g)
        m_hat = m_new / (1.0 - ADAM_B1 ** ADAM_STEP)
        v_hat = v_new / (1.0 - ADAM_B2 ** ADAM_STEP)
        g_ref[...] = g
        d_ref[...] = -ADAM_LR * (m_hat / (jnp.sqrt(v_hat) + ADAM_EPS) + ADAM_WD * w_ref[...])
        mo_ref[...] = m_new
        vo_ref[...] = v_new

    blk = pl.BlockSpec((tr, C), lambda i: (i, 0))
    out = jax.ShapeDtypeStruct((R, C), F32)
    return pl.pallas_call(
        body,
        out_shape=(out, out, out, out),
        grid=(R // tr,),
        in_specs=[pl.BlockSpec((P, tr, C), lambda i: (0, i, 0)), blk, blk, blk],
        out_specs=(blk, blk, blk, blk),
        compiler_params=_params("parallel"),
        name=name,
    )(parts, w, m, v)


def _flip(v, bit):
    return 1 - v if bit else v


def all_gather(shard, *, name, dep=None):
    def body(x_ref, out_ref, send_sems, recv_sems, local_sem):
        x, y, c = lax.axis_index("x"), lax.axis_index("y"), lax.axis_index("c")
        me, sibling = (x, y, c), (x, y, 1 - c)
        chips = [(1 - x, y), (x, 1 - y), (1 - x, 1 - y)]

        def block(px, py, pc):
            return out_ref.at[4 * px + 2 * py + pc]

        def copy(k, blk, to, src=None):
            return pltpu.make_async_remote_copy(
                src_ref=block(*blk) if src is None else src,
                dst_ref=block(*blk),
                send_sem=send_sems.at[k],
                recv_sem=recv_sems.at[k],
                device_id=to,
                device_id_type=pl.DeviceIdType.MESH,
            )

        mine = pltpu.make_async_copy(x_ref, block(*me), local_sem)
        mine.start()
        first = [copy(0, me, sibling, src=x_ref)]
        first += [copy(1 + j, me, (*chip, c), src=x_ref) for j, chip in enumerate(chips)]
        for cp in first:
            cp.start()
        passed = [copy(4 + j, (*chip, c), sibling) for j, chip in enumerate(chips)]
        for j, chip in enumerate(chips):
            copy(1 + j, (*chip, c), me).wait_recv()
            passed[j].start()
        copy(0, sibling, me).wait_recv()
        for j, chip in enumerate(chips):
            copy(4 + j, (*chip, 1 - c), me).wait_recv()
        for cp in first + passed:
            cp.wait_send()
        mine.wait()

    body, in_specs, args = _with_dep(body, [pl.BlockSpec(memory_space=pl.ANY)], [shard], dep)
    return pl.pallas_call(
        body,
        out_shape=jax.ShapeDtypeStruct((N_DEV,) + shard.shape, shard.dtype),
        in_specs=in_specs,
        out_specs=pl.BlockSpec(memory_space=pl.ANY),
        scratch_shapes=[pltpu.SemaphoreType.DMA((7,)), pltpu.SemaphoreType.DMA((7,)),
                        pltpu.SemaphoreType.DMA],
        name=name,
    )(*args)


def exchange_slabs(slabs, *, name):
    def body(g_ref, r_ref, send_sems, recv_sems, local_sem):
        x, y, c = lax.axis_index("x"), lax.axis_index("y"), lax.axis_index("c")
        me = 4 * x + 2 * y + c
        mine = pltpu.make_async_copy(g_ref.at[me], r_ref.at[me], local_sem)
        mine.start()
        sends, recvs = [], []
        for k in range(1, N_DEV):
            px, py, pc = _flip(x, (k >> 2) & 1), _flip(y, (k >> 1) & 1), _flip(c, k & 1)
            peer = 4 * px + 2 * py + pc
            sends.append(pltpu.make_async_remote_copy(
                src_ref=g_ref.at[peer], dst_ref=r_ref.at[me],
                send_sem=send_sems.at[k - 1], recv_sem=recv_sems.at[k - 1],
                device_id=(px, py, pc), device_id_type=pl.DeviceIdType.MESH))
            recvs.append(pltpu.make_async_remote_copy(
                src_ref=g_ref.at[peer], dst_ref=r_ref.at[peer],
                send_sem=send_sems.at[k - 1], recv_sem=recv_sems.at[k - 1],
                device_id=(px, py, pc), device_id_type=pl.DeviceIdType.MESH))
        for cp in sends:
            cp.start()
        for cp in recvs:
            cp.wait_recv()
        for cp in sends:
            cp.wait_send()
        mine.wait()

    return pl.pallas_call(
        body,
        out_shape=jax.ShapeDtypeStruct(slabs.shape, slabs.dtype),
        in_specs=[pl.BlockSpec(memory_space=pl.ANY)],
        out_specs=pl.BlockSpec(memory_space=pl.ANY),
        scratch_shapes=[pltpu.SemaphoreType.DMA((7,)), pltpu.SemaphoreType.DMA((7,)),
                        pltpu.SemaphoreType.DMA],
        name=name,
    )(slabs)


def reduce_adamw(slabs, w, m, v, *, name):
    got = exchange_slabs(slabs, name=name + "_xchg")
    return adamw(got, w, m, v, name=name + "_adamw")


_HBM = pl.BlockSpec(memory_space=pltpu.HBM)
_SEM = pl.BlockSpec(memory_space=pltpu.SEMAPHORE)
_ANY = pl.BlockSpec(memory_space=pl.ANY)
_EFFECT = pltpu.SideEffectType.DATAFLOW_SIDE_EFFECTING
_N_FIRST = 4


def _first_copies(land_ref, send_sems, recv_sems):
    x, y, c = lax.axis_index("x"), lax.axis_index("y"), lax.axis_index("c")
    mine = land_ref.at[4 * x + 2 * y + c]
    targets = [(x, y, 1 - c), (1 - x, y, c), (x, 1 - y, c), (1 - x, 1 - y, c)]
    sends, recvs = [], []
    for k, (px, py, pc) in enumerate(targets):
        common = dict(send_sem=send_sems.at[k], recv_sem=recv_sems.at[k], device_id=(px, py, pc),
                      device_id_type=pl.DeviceIdType.MESH)
        sends.append(pltpu.make_async_remote_copy(src_ref=mine, dst_ref=mine, **common))
        theirs = land_ref.at[4 * px + 2 * py + pc]
        recvs.append(pltpu.make_async_remote_copy(src_ref=theirs, dst_ref=theirs, **common))
    return sends, recvs


def _second_copies(land_ref, send_sems, recv_sems):
    x, y, c = lax.axis_index("x"), lax.axis_index("y"), lax.axis_index("c")
    sends, recvs = [], []
    for j, (px, py) in enumerate([(1 - x, y), (x, 1 - y), (1 - x, 1 - y)]):
        common = dict(send_sem=send_sems.at[j], recv_sem=recv_sems.at[j], device_id=(x, y, 1 - c),
                      device_id_type=pl.DeviceIdType.MESH)
        blk = land_ref.at[4 * px + 2 * py + c]
        sends.append(pltpu.make_async_remote_copy(src_ref=blk, dst_ref=blk, **common))
        got = land_ref.at[4 * px + 2 * py + (1 - c)]
        recvs.append(pltpu.make_async_remote_copy(src_ref=got, dst_ref=got, **common))
    return sends, recvs


def gather_start(shard, me, after, *, name):
    R, C = shard.shape
    tr = _tile(R, max(16, (512 * 1024) // C), 16)

    def place_body(me_ref, x_ref, o_ref):
        o_ref[...] = x_ref[...].astype(BF)

    land = pl.pallas_call(
        place_body, name=name + "_own",
        out_shape=jax.ShapeDtypeStruct((N_DEV, R, C), BF),
        grid_spec=pltpu.PrefetchScalarGridSpec(
            num_scalar_prefetch=1,
            grid=(R // tr,),
            in_specs=[pl.BlockSpec((tr, C), lambda i, me_ref: (i, 0))],
            out_specs=pl.BlockSpec((None, tr, C), lambda i, me_ref: (me_ref[0], i, 0)),
        ),
        compiler_params=_params("parallel"),
    )(me.reshape(1).astype(jnp.int32), shard)

    def body(land_ref, after_ref, send_sems, recv_sems, land_thru, token):
        sends, _ = _first_copies(land_ref, send_sems, recv_sems)
        for cp in sends:
            cp.start()
        token[...] = jnp.zeros_like(token)

    send_sems, recv_sems, land_thru, token = pl.pallas_call(
        body, name=name + "_s1",
        out_shape=(pltpu.SemaphoreType.DMA((_N_FIRST,)), pltpu.SemaphoreType.DMA((_N_FIRST,)),
                   pltpu.HBM(land.shape, land.dtype), jax.ShapeDtypeStruct((8, LANES), F32)),
        in_specs=(_HBM, _ANY),
        out_specs=(_SEM, _SEM, _HBM, pl.BlockSpec(memory_space=pltpu.VMEM)),
        input_output_aliases={0: 2},
        compiler_params=pltpu.CompilerParams(has_side_effects=_EFFECT),
    )(pltpu.with_memory_space_constraint(land, pltpu.HBM), after)
    return (send_sems, recv_sems, land_thru), token


def gather_mid(handle, after, *, name):
    send_sems, recv_sems, land_thru = handle

    def body(land_ref, send1, recv1, after_ref, send2, recv2, land_out, token):
        sends, recvs = _first_copies(land_ref, send1, recv1)
        for cp in sends:
            cp.wait_send()
        for cp in recvs:
            cp.wait_recv()
        seconds, _ = _second_copies(land_ref, send2, recv2)
        for cp in seconds:
            cp.start()
        token[...] = jnp.zeros_like(token)

    send2, recv2, land2, token = pl.pallas_call(
        body, name=name + "_s2",
        out_shape=(pltpu.SemaphoreType.DMA((3,)), pltpu.SemaphoreType.DMA((3,)),
                   pltpu.HBM(land_thru.shape, land_thru.dtype), jax.ShapeDtypeStruct((8, LANES), F32)),
        in_specs=(_HBM, _SEM, _SEM, _ANY),
        out_specs=(_SEM, _SEM, _HBM, pl.BlockSpec(memory_space=pltpu.VMEM)),
        input_output_aliases={0: 2},
        compiler_params=pltpu.CompilerParams(has_side_effects=_EFFECT),
    )(land_thru, send_sems, recv_sems, after)
    return (send2, recv2, land2), token


def gather_finish(handle, after, *, name):
    send2, recv2, land2 = handle

    def body(land_ref, send2, recv2, after_ref, got_ref):
        sends, recvs = _second_copies(land_ref, send2, recv2)
        for cp in sends:
            cp.wait_send()
        for cp in recvs:
            cp.wait_recv()

    return pl.pallas_call(
        body, name=name + "_w",
        out_shape=pltpu.HBM(land2.shape, land2.dtype),
        in_specs=(_HBM, _SEM, _SEM, _ANY),
        out_specs=_HBM,
        input_output_aliases={0: 0},
        compiler_params=pltpu.CompilerParams(has_side_effects=_EFFECT),
    )(land2, send2, recv2, after)


def _slab_copies(g_ref, r_ref, send_sems, recv_sems):
    x, y, c = lax.axis_index("x"), lax.axis_index("y"), lax.axis_index("c")
    me = 4 * x + 2 * y + c
    sends, recvs = [], []
    for k in range(1, N_DEV):
        px, py, pc = _flip(x, (k >> 2) & 1), _flip(y, (k >> 1) & 1), _flip(c, k & 1)
        peer = 4 * px + 2 * py + pc
        common = dict(send_sem=send_sems.at[k - 1], recv_sem=recv_sems.at[k - 1], device_id=(px, py, pc),
                      device_id_type=pl.DeviceIdType.MESH)
        sends.append(pltpu.make_async_remote_copy(src_ref=g_ref.at[peer], dst_ref=r_ref.at[me], **common))
        recvs.append(pltpu.make_async_remote_copy(src_ref=g_ref.at[peer], dst_ref=r_ref.at[peer], **common))
    return sends, recvs


def exchange_start(slabs, *, name):
    land = lax.empty(slabs.shape, slabs.dtype)

    def body(g_ref, r_ref, send_sems, recv_sems, g_thru, r_thru, token):
        sends, _ = _slab_copies(g_ref, r_ref, send_sems, recv_sems)
        for cp in sends:
            cp.start()
        token[...] = jnp.zeros_like(token)

    send_sems, recv_sems, g_thru, r_thru, token = pl.pallas_call(
        body, name=name,
        out_shape=(pltpu.SemaphoreType.DMA((N_DEV - 1,)), pltpu.SemaphoreType.DMA((N_DEV - 1,)),
                   pltpu.HBM(slabs.shape, slabs.dtype), pltpu.HBM(slabs.shape, slabs.dtype),
                   jax.ShapeDtypeStruct((8, LANES), F32)),
        in_specs=(_HBM, _HBM),
        out_specs=(_SEM, _SEM, _HBM, _HBM, pl.BlockSpec(memory_space=pltpu.VMEM)),
        input_output_aliases={0: 2, 1: 3},
        compiler_params=pltpu.CompilerParams(has_side_effects=_EFFECT),
    )(pltpu.with_memory_space_constraint(slabs, pltpu.HBM), pltpu.with_memory_space_constraint(land, pltpu.HBM))
    return (send_sems, recv_sems, g_thru, r_thru), token


def exchange_finish(handle, after, *, name):
    send_sems, recv_sems, g_thru, r_thru = handle

    def body(g_ref, r_ref, send_sems, recv_sems, after_ref, g_out, r_out):
        sends, recvs = _slab_copies(g_ref, r_ref, send_sems, recv_sems)
        for cp in sends:
            cp.wait_send()
        for cp in recvs:
            cp.wait_recv()

    return pl.pallas_call(
        body, name=name,
        out_shape=(pltpu.HBM(g_thru.shape, g_thru.dtype), pltpu.HBM(r_thru.shape, r_thru.dtype)),
        in_specs=(_HBM, _HBM, _SEM, _SEM, _ANY),
        out_specs=(_HBM, _HBM),
        input_output_aliases={0: 0, 1: 1},
        compiler_params=pltpu.CompilerParams(has_side_effects=_EFFECT),
    )(g_thru, r_thru, send_sems, recv_sems, after)


def adamw_own(own, got, me, w, m, v, layer, prev, *, name):
    P, R, C = got.shape
    L = w.shape[0]
    tr = _tile(R, max(16, (128 * 1024) // C), 16)

    def body(me_ref, own_ref, p_ref, w_ref, m_ref, v_ref, *rest):
        g_ref, d_ref, mo_ref, vo_ref = rest[-4:]
        mine = own_ref[...].astype(F32)
        g = None
        for i in range(P):
            term = jnp.where(me_ref[0] == i, mine, p_ref[i].astype(F32))
            g = term if g is None else g + term
        m_new = ADAM_B1 * m_ref[...] + (1.0 - ADAM_B1) * g
        v_new = ADAM_B2 * v_ref[...] + (1.0 - ADAM_B2) * jnp.square(g)
        m_hat = m_new / (1.0 - ADAM_B1 ** ADAM_STEP)
        v_hat = v_new / (1.0 - ADAM_B2 ** ADAM_STEP)
        g_ref[...] = g
        d_ref[...] = -ADAM_LR * (m_hat / (jnp.sqrt(v_hat) + ADAM_EPS) + ADAM_WD * w_ref[...])
        mo_ref[...] = m_new
        vo_ref[...] = v_new

    blk = pl.BlockSpec((None, tr, C), lambda i, me_ref: (layer, i, 0))
    out = jax.ShapeDtypeStruct((L, R, C), F32)
    in_specs = [pl.BlockSpec((None, tr, C), lambda i, me_ref: (me_ref[0], i, 0)),
                pl.BlockSpec((P, tr, C), lambda i, me_ref: (0, i, 0)), blk, blk, blk]
    args = [me.reshape(1).astype(jnp.int32), own, got, w, m, v]
    aliases = {}
    if prev is not None:
        in_specs += [pl.BlockSpec(memory_space=pl.ANY)] * 4
        aliases = {len(args) + i: i for i in range(4)}
        args += list(prev)
    return pl.pallas_call(
        body,
        out_shape=(out, out, out, out),
        grid_spec=pltpu.PrefetchScalarGridSpec(
            num_scalar_prefetch=1,
            grid=(R // tr,),
            in_specs=in_specs,
            out_specs=(blk, blk, blk, blk),
        ),
        input_output_aliases=aliases,
        compiler_params=_params("parallel"),
        name=name,
    )(*args)


def _pad_rows(a, rows):
    return jnp.pad(a, ((0, rows - a.shape[0]), (0, 0)))


def _pad_cols(a, cols):
    return jnp.pad(a, ((0, 0), (0, cols - a.shape[1])))


def kernel(x, meta, a_norm, a_w_in, a_conv, a_w_out, kv_norm, w_kv, k_norm, w_f, b_f, b_norm, b_w_q, b_q_norm, b_w_o, ffn_norm, ffn_w_gu, ffn_w_down, loss_target, m_meta, m_a_norm, m_a_w_in, m_a_conv, m_a_w_out, m_kv_norm, m_w_kv, m_k_norm, m_w_f, m_b_f, m_b_norm, m_b_w_q, m_b_q_norm, m_b_w_o, m_ffn_norm, m_ffn_w_gu, m_ffn_w_down, v_meta, v_a_norm, v_a_w_in, v_a_conv, v_a_w_out, v_kv_norm, v_w_kv, v_k_norm, v_w_f, v_b_f, v_b_norm, v_b_w_q, v_b_q_norm, v_b_w_o, v_ffn_norm, v_ffn_w_gu, v_ffn_w_down):
    S, D = x.shape[1], x.shape[2]
    n_meta = meta.shape[0]
    Ds = meta.shape[1]
    H = D // HEAD_DIM
    n_a, n_b = a_w_in.shape[0], b_w_q.shape[0]
    depth = n_a + n_b
    Fs = ffn_w_down.shape[1]
    pad = BLOCK - n_meta
    lead = pad + n_meta
    T = lead + S
    tk_attn = _tile(T, 384, LANES)
    nk_attn = T // tk_attn
    q_scale = 1.0 / math.sqrt(HEAD_DIM)
    my = 4 * lax.axis_index("x") + 2 * lax.axis_index("y") + lax.axis_index("c")

    wf_t = w_f.reshape(H, Ds)
    small = jnp.concatenate([meta, _pad_rows(a_norm, 8), _pad_rows(a_conv.reshape(n_a * 3, Ds), 8), wf_t], axis=0)
    r_an, r_ac, r_wf = n_meta, n_meta + 8, n_meta + 16
    gs = all_gather(small, name="ag_small")
    unshard = lambda blk: jnp.transpose(blk, (1, 0, 2)).reshape(blk.shape[1], D)
    meta_full = unshard(gs[:, 0:n_meta])
    a_norm_full = unshard(gs[:, r_an:r_an + n_a])
    a_conv_full = unshard(gs[:, r_ac:r_ac + 3 * n_a]).reshape(n_a, 3, D)
    w_f_full = gs[:, r_wf:r_wf + H].reshape(D, H)
    wf_pad = _pad_cols(w_f_full, LANES).astype(BF)[None]
    bf_pad = _pad_cols(b_f.reshape(1, H), LANES)

    def layer_shards(l):
        if l < n_a:
            mix = [(("in", l), a_w_in[l]), (("out", l), a_w_out[l])]
        else:
            j = l - n_a
            mix = ([(("kv", 0), w_kv)] if j == 0 else []) + [(("q", j), b_w_q[j]), (("o", j), b_w_o[j])]
        return mix + [(("gu", l), ffn_w_gu[l]), (("dn", l), ffn_w_down[l])]

    first_level, second_level, W = {}, {}, {}
    st = {"done": None, "tok": None}

    def note(val):
        st["done"] = val
        return val

    def take():
        tok, st["tok"] = st["tok"], None
        return tok

    def chain_after(default):
        if st["tok"] is not None:
            return st["tok"]
        return default if st["done"] is None else st["done"]

    def ag_name(key):
        return f"ag_{key[0]}{key[1]}"

    def start_layer(l):
        for key, shard in layer_shards(l):
            first_level[key], st["tok"] = gather_start(shard, my, chain_after(shard), name=ag_name(key))

    def pass_on(keys):
        for key in keys:
            second_level[key], st["tok"] = gather_mid(first_level.pop(key), chain_after(None), name=ag_name(key))

    def weight(key, shape=None):
        w = gather_finish(second_level.pop(key), st["done"], name=ag_name(key))
        W[key] = w if shape is None else w.reshape(shape)
        return W[key]

    def layer_keys(l):
        keys = [key for key, _ in layer_shards(l)]
        return keys[:-2], keys[-2:]

    h = note(jnp.concatenate([jnp.zeros((pad, D), F32), meta_full, x[0]], axis=0))
    start_layer(0)
    pass_on(layer_keys(0)[0])
    saved = []
    shared = None
    for l in range(depth):
        rec = {"h": h}
        mix_keys, ffn_keys = layer_keys(l)

        def ahead():
            if l >= 1:
                pass_on(ffn_keys[:1])
            if l + 1 < depth:
                start_layer(l + 1)

        if l < n_a:
            xn = note(rms_fwd(h, a_norm_full[l], name=f"a{l}_norm", dep=take()))
            ahead()
            proj = note(mm_nn(xn, weight(("in", l)), name=f"a{l}_in", dep=take()))
            if l == 0:
                pass_on(ffn_keys[:1])
            y = note(conv_fwd(proj, a_conv_full[l], name=f"a{l}_conv"))
            h1 = note(mm_nn(y, weight(("out", l), (1, D, D)), add=h, name=f"a{l}_out", dep=take()))
            rec.update(xn=xn, proj=proj, y=y)
        else:
            j = l - n_a
            if j == 0:
                xnk = note(rms_fwd(h, kv_norm, name="kv_norm", dep=take()))
                ahead()
                kv = note(mm_nn(xnk, weight(("kv", 0)), name="kv_proj", dep=take()))
                k, v = kv_post(kv, k_norm, name="kv_post")
                logits = mm_nn(xnk, wf_pad, name="f_logits", tn_target=LANES)
                cfull = fgate_fwd(logits, bf_pad, pad, name="f_gate")
                crow = jnp.transpose(cfull[:, :H]).reshape(H, nk_attn, 1, tk_attn)
                shared = dict(h=h, xnk=xnk, kv=kv, logits=logits)
                xn = note(rms_fwd(h, b_norm[j], name=f"b{j}_norm"))
            else:
                xn = note(rms_fwd(h, b_norm[j], name=f"b{j}_norm", dep=take()))
                ahead()
            qraw = note(mm_nn(xn, weight(("q", j), (1, D, D)), name=f"b{j}_q", dep=take()))
            q = hn_fwd(qraw, b_q_norm[j], q_scale, name=f"b{j}_qnorm")
            o, lse = attn_fwd(q, k, v, crow, pad, name=f"b{j}_attn")
            note(o)
            h1 = note(mm_nn(o, weight(("o", j), (1, D, D)), add=h, name=f"b{j}_o"))
            rec.update(xn=xn, qraw=qraw, q=q, o=o, lse=lse)
        xn2 = note(rms_fwd(h1, ffn_norm[l], name=f"f{l}_norm", dep=take()))
        pass_on(ffn_keys[1:])
        act, g_s, u_s = mm_swiglu(xn2, weight(("gu", l)), name=f"f{l}_gu", dep=take())
        note(act)
        if l + 1 < depth:
            pass_on(layer_keys(l + 1)[0])
        h = note(mm_nn(act, weight(("dn", l), (1, N_DEV * Fs, D)), add=h1, name=f"f{l}_down", resident=True,
                       tm_target=528, tn_target=512, dep=take()))
        rec.update(h1=h1, xn2=xn2, act=act, g=g_s, u=u_s)
        saved.append(rec)

    dh, dhb, loss_tile = loss_head(h, loss_target[0], lead, name="loss")
    loss = lax.psum(loss_tile[0, 0], MESH_AXES)

    upd = {}
    small_g = {}
    inflight = []

    def big(name, l, section, slabs, w, m, v):
        handle, st["tok"] = exchange_start(slabs.reshape(N_DEV, -1, w.shape[-1]), name=f"{name}{l}_xs")
        inflight.append((section, name, l, handle, w, m, v))

    def land(sections, after):
        for entry in [e for e in inflight if sections is None or e[0] in sections]:
            inflight.remove(entry)
            _, name, l, handle, w, m, v = entry
            own, got = exchange_finish(handle, after, name=f"{name}{l}_xw")
            flat = lambda t: t.reshape(w.shape[0], -1, w.shape[-1])
            upd[name] = adamw_own(own, got, my, flat(w), flat(m), flat(v), l, upd.get(name),
                                  name=f"{name}{l}_adamw")

    dk = dv = dck = dcq = None
    for l in reversed(range(depth)):
        rec = saved[l]
        land([("ffn", l + 1)], dh)
        dgu = mm_nt_dswiglu(dhb, W[("dn", l)], rec["g"], rec["u"], name=f"f{l}_ddown")
        big("ffn_w_down", l, ("ffn", l), mm_tn(rec["act"], dhb, 1, name=f"f{l}_wdown"),
            ffn_w_down, m_ffn_w_down, v_ffn_w_down)
        big("ffn_w_gu", l, ("ffn", l), mm_tn(rec["xn2"], dgu, N_DEV, name=f"f{l}_wgu", dep=take()),
            ffn_w_gu, m_ffn_w_gu, v_ffn_w_gu)
        dxn2 = mm_nt(dgu, W[("gu", l)], name=f"f{l}_dgu", dep=take())
        dh1, dhb, dgf = rms_bwd(dxn2, rec["h1"], ffn_norm[l], dh, name=f"f{l}_dnorm")
        small_g[("ffn_norm", l)] = dgf
        land([("mix", l + 1)], dh1)
        if l < n_a:
            dy = mm_nt(dhb, W[("out", l)], name=f"a{l}_dout")
            big("a_w_out", l, ("mix", l), mm_tn(rec["y"], dhb, 1, name=f"a{l}_wout"),
                a_w_out, m_a_w_out, v_a_w_out)
            db, dc, dhh, dcw = conv_bwd(dy, rec["proj"], a_conv_full[l], name=f"a{l}_dconv", dep=take())
            small_g[("a_conv", l)] = dcw
            dproj = jnp.concatenate([db, dc, dhh], axis=1)
            big("a_w_in", l, ("mix", l), mm_tn(rec["xn"], dproj, N_DEV, name=f"a{l}_win"),
                a_w_in, m_a_w_in, v_a_w_in)
            dxn = mm_nt(dproj, W[("in", l)], name=f"a{l}_din", dep=take())
            dh, dhb, dga = rms_bwd(dxn, rec["h"], a_norm_full[l], dh1, name=f"a{l}_dnorm")
            small_g[("a_norm", l)] = dga
        else:
            j = l - n_a
            do = mm_nt(dhb, W[("o", j)], name=f"b{j}_do")
            big("b_w_o", j, ("mix", l), mm_tn(rec["o"], dhb, 1, name=f"b{j}_wo"),
                b_w_o, m_b_w_o, v_b_w_o)
            prev = None if dk is None else (dk, dv, dck, dcq)
            dq, dk, dv, dck, dcq = attn_bwd(rec["q"], k, v, do, rec["o"], rec["lse"], crow, prev, pad,
                                            name=f"b{j}_dattn", dep=take())
            dqraw, dqn = hn_bwd(dq, rec["qraw"], b_q_norm[j], q_scale, name=f"b{j}_dqnorm")
            small_g[("b_q_norm", j)] = dqn
            big("b_w_q", j, ("mix", l), mm_tn(rec["xn"], dqraw, 1, name=f"b{j}_wq"),
                b_w_q, m_b_w_q, v_b_w_q)
            dxn = mm_nt(dqraw, W[("q", j)], name=f"b{j}_dq", dep=take())
            dh, dhb, dgb = rms_bwd(dxn, rec["h"], b_norm[j], dh1, name=f"b{j}_dnorm")
            small_g[("b_norm", j)] = dgb
            if j == 0:
                dkraw, dkn = hn_bwd(dk, shared["kv"], k_norm, 1.0, name="kv_dknorm")
                dkv = jnp.concatenate([dkraw, dv.astype(BF)], axis=1)
                dc_full = _pad_cols(jnp.transpose(dck.reshape(H, T) + dcq.reshape(H, T)), LANES)
                dz, dbf = fgate_bwd(dc_full, shared["logits"], bf_pad, pad, name="f_dgate")
                big("w_kv", 0, ("mix", l), mm_tn(shared["xnk"], dkv, N_DEV, name="kv_wkv"),
                    w_kv[None], m_w_kv[None], v_w_kv[None])
                dwf_t = mm_tn(dz, shared["xnk"], 1, name="f_wf", out_dtype=F32, tn_target=1024,
                              dep=take())[0, :H]
                dxn_f = mm_nt(dz, wf_pad, name="f_dxn")
                dxnk = mm_nt(dkv, W[("kv", 0)], add=dxn_f, name="kv_dxn")
                dh, dhb, dgkv = rms_bwd(dxnk, shared["h"], kv_norm, dh, name="kv_dnorm")
    land(None, dh)

    grad_x = dh[lead:][None]

    row8 = lambda a: _pad_rows(_pad_cols(a, D), 8)
    stack = lambda key, n: jnp.concatenate([small_g[(key, i)] for i in range(n)], axis=0)
    g_sharded = jnp.concatenate([dh[pad:lead], row8(stack("a_norm", n_a)), row8(stack("a_conv", n_a)), dwf_t], axis=0)
    g_repl = jnp.concatenate([row8(jnp.concatenate([dgkv, stack("b_norm", n_b)], axis=0)),
                              row8(stack("ffn_norm", depth)),
                              row8(jnp.concatenate([_pad_cols(dkn, D), _pad_cols(stack("b_q_norm", n_b), D),
                                                    _pad_cols(dbf[:, :H], D)], axis=0))], axis=0)
    n_sh = g_sharded.shape[0]
    gathered = all_gather(jnp.concatenate([g_sharded, g_repl], axis=0), name="ag_small_grads",
                          dep=upd["a_w_in"][0])
    parts_sh = lax.dynamic_slice_in_dim(gathered[:, :n_sh], my * Ds, Ds, axis=2)
    parts_rp = gathered[:, n_sh:]

    def pack_sh(t_meta, t_an, t_ac, t_wf):
        return jnp.concatenate([t_meta, _pad_rows(t_an, 8), _pad_rows(t_ac.reshape(n_a * 3, Ds), 8),
                                jnp.transpose(t_wf)], axis=0)

    def pack_rp(t_kv, t_bn, t_fn, t_kn, t_qn, t_bf):
        return jnp.concatenate([row8(jnp.concatenate([t_kv.reshape(1, D), t_bn], axis=0)), row8(t_fn),
                                row8(jnp.concatenate([_pad_cols(t_kn.reshape(1, -1), D), _pad_cols(t_qn, D),
                                                      _pad_cols(t_bf.reshape(1, -1), D)], axis=0))], axis=0)

    res_sh = adamw(parts_sh, pack_sh(meta, a_norm, a_conv, w_f), pack_sh(m_meta, m_a_norm, m_a_conv, m_w_f),
                   pack_sh(v_meta, v_a_norm, v_a_conv, v_w_f), name="small_sharded_adamw")
    res_rp = adamw(parts_rp, pack_rp(kv_norm, b_norm, ffn_norm, k_norm, b_q_norm, b_f),
                   pack_rp(m_kv_norm, m_b_norm, m_ffn_norm, m_k_norm, m_b_q_norm, m_b_f),
                   pack_rp(v_kv_norm, v_b_norm, v_ffn_norm, v_k_norm, v_b_q_norm, v_b_f), name="small_repl_adamw")

    def unpack(kind):
        sh, rp = res_sh[kind], res_rp[kind]
        out = {
            "meta": sh[0:n_meta],
            "a_norm": sh[r_an:r_an + n_a],
            "a_conv": sh[r_ac:r_ac + 3 * n_a].reshape(n_a, 3, Ds),
            "w_f": jnp.transpose(sh[r_wf:r_wf + H]),
            "kv_norm": rp[0],
            "b_norm": rp[1:1 + n_b],
            "ffn_norm": rp[8:8 + depth],
            "k_norm": rp[16, :HEAD_DIM],
            "b_q_norm": rp[17:17 + n_b, :HEAD_DIM],
            "b_f": rp[17 + n_b, :H],
        }
        for name, like in (("a_w_in", a_w_in), ("a_w_out", a_w_out), ("b_w_q", b_w_q), ("b_w_o", b_w_o),
                           ("ffn_w_gu", ffn_w_gu), ("ffn_w_down", ffn_w_down)):
            out[name] = upd[name][kind].reshape(like.shape)
        out["w_kv"] = upd["w_kv"][kind].reshape(w_kv.shape)
        return out

    order = ["meta", "a_norm", "a_w_in", "a_conv", "a_w_out", "kv_norm", "w_kv", "k_norm", "w_f", "b_f",
             "b_norm", "b_w_q", "b_q_norm", "b_w_o", "ffn_norm", "ffn_w_gu", "ffn_w_down"]
    outs = [loss, grad_x]
    for kind in range(4):
        vals = unpack(kind)
        outs += [vals[n] for n in order]
    return tuple(outs)
```

```python
import functools
import math

import jax
import jax.numpy as jnp
from jax import lax
from jax.experimental import pallas as pl
from jax.experimental.pallas import tpu as pltpu

N_DEV = 8
MESH_AXES = ("x", "y", "c")
EPS = 1e-6
NEG = -1e30
HEAD_DIM = 128
BLOCK = 128
LANES = 128
V7X_VMEM_LIMIT = 56 * 1024 * 1024

ADAM_LR = 0.001
ADAM_B1 = 0.9
ADAM_B2 = 0.999
ADAM_EPS = 1e-08
ADAM_WD = 0.01
ADAM_STEP = 10

BF = jnp.bfloat16
F32 = jnp.float32


def _tile(n, target, mult):
    best = None
    for t in range(mult, min(n, target) + 1, mult):
        if n % t == 0:
            best = t
    return n if best is None else best


def _params(*sem):
    return pltpu.CompilerParams(dimension_semantics=sem, vmem_limit_bytes=V7X_VMEM_LIMIT)


def _with_dep(body, in_specs, args, dep):
    if dep is None:
        return body, list(in_specs), list(args)
    n_in = len(args)

    def body_dep(*refs):
        body(*refs[:n_in], *refs[n_in + 1:])

    return body_dep, list(in_specs) + [pl.BlockSpec(memory_space=pl.ANY)], list(args) + [dep]


def mm_nn(a, w, *, name, add=None, dep=None, out_dtype=F32, tm_target=1056, tn_target=1024, tk_target=2048,
          resident=False):
    M, K = a.shape
    G, K2, n = w.shape
    assert K == K2
    tm = _tile(M, tm_target, 16)
    tn = _tile(n, tn_target, LANES)
    tk = K if resident else _tile(K, tk_target, LANES)
    nj, nk = n // tn, K // tk
    has_add = add is not None
    if resident:
        grid, sem = (G * nj, M // tm), ("parallel", "parallel")
        order = lambda f: (lambda j, i: f(i, j, 0))
        w_mode = dict(pipeline_mode=pl.Buffered(1))
    else:
        grid, sem = (M // tm, G * nj, nk), ("parallel", "parallel", "arbitrary")
        order = lambda f: f
        w_mode = {}

    def body(*refs):
        if has_add:
            a_ref, w_ref, add_ref, o_ref = refs[:4]
        else:
            a_ref, w_ref, o_ref = refs[:3]
            add_ref = None

        def finish(r):
            if has_add:
                r = r + add_ref[...]
            o_ref[...] = r.astype(out_dtype)

        part = jnp.dot(a_ref[...], w_ref[...], preferred_element_type=F32)
        if nk == 1:
            finish(part)
        else:
            acc_ref = refs[-1]
            k = pl.program_id(2)

            @pl.when(k == 0)
            def _():
                acc_ref[...] = part

            @pl.when(k > 0)
            def _():
                acc_ref[...] += part

            @pl.when(k == nk - 1)
            def _():
                finish(acc_ref[...])

    in_specs = [
        pl.BlockSpec((tm, tk), order(lambda i, j, k: (i, k))),
        pl.BlockSpec((None, tk, tn), order(lambda i, j, k: (j // nj, k, j % nj)), **w_mode),
    ]
    args = [a, w]
    if has_add:
        in_specs.append(pl.BlockSpec((tm, tn), order(lambda i, j, k: (i, j))))
        args.append(add)
    body, in_specs, args = _with_dep(body, in_specs, args, dep)
    return pl.pallas_call(
        body,
        out_shape=jax.ShapeDtypeStruct((M, G * n), out_dtype),
        grid=grid,
        in_specs=in_specs,
        out_specs=pl.BlockSpec((tm, tn), order(lambda i, j, k: (i, j))),
        scratch_shapes=[pltpu.VMEM((tm, tn), F32)] if nk > 1 else [],
        compiler_params=_params(*sem),
        name=name,
    )(*args)


def mm_swiglu(xn, wgu, *, name, dep=None, save_dtype=BF, tm_target=528):
    M, K = xn.shape
    G, _, n = wgu.shape
    half = G // 2
    tm = _tile(M, tm_target, 16)
    tn = _tile(n, 1408, LANES)
    nj = n // tn
    Fh = half * n

    def body(a_ref, wg_ref, wu_ref, act_ref, g_ref, u_ref):
        a = a_ref[...]
        g = jnp.dot(a, wg_ref[...], preferred_element_type=F32)
        g_ref[...] = g.astype(save_dtype)
        silu = g * jax.nn.sigmoid(g)
        u = jnp.dot(a, wu_ref[...], preferred_element_type=F32)
        u_ref[...] = u.astype(save_dtype)
        act_ref[...] = (silu * u).astype(BF)

    out_block = pl.BlockSpec((tm, tn), lambda j, i: (i, j))
    once = pl.Buffered(1)
    body, in_specs, args = _with_dep(body, [
        pl.BlockSpec((tm, K), lambda j, i: (i, 0)),
        pl.BlockSpec((None, K, tn), lambda j, i: (j // nj, 0, j % nj), pipeline_mode=once),
        pl.BlockSpec((None, K, tn), lambda j, i: (half + j // nj, 0, j % nj), pipeline_mode=once),
    ], [xn, wgu, wgu], dep)
    return pl.pallas_call(
        body,
        out_shape=(jax.ShapeDtypeStruct((M, Fh), BF),
                   jax.ShapeDtypeStruct((M, Fh), save_dtype),
                   jax.ShapeDtypeStruct((M, Fh), save_dtype)),
        grid=(half * nj, M // tm),
        in_specs=in_specs,
        out_specs=(out_block, out_block, out_block),
        compiler_params=_params("parallel", "parallel"),
        name=name,
    )(*args)


def mm_nt(dy, w, *, name, add=None, dep=None, out_dtype=F32, tm_target=1056, tko_target=1024, tc_target=2048,
          gb=1):
    if dy.ndim == 2:
        dy = dy.reshape(1, *dy.shape)
    P, M, Np = dy.shape
    G, K, n = w.shape
    assert P * Np == G * n
    tm = _tile(M, tm_target, 16)
    tko = _tile(K, tko_target, LANES)
    tc = _tile(n, tc_target, LANES)
    nc = n // tc
    gb = gb if nc == 1 else 1
    assert G % gb == 0 and Np % (gb * tc) == 0
    steps = (G // gb) * nc
    per_part = Np // (gb * tc)
    has_add = add is not None

    def body(*refs):
        if has_add:
            dy_ref, w_ref, add_ref, o_ref = refs[:4]
        else:
            dy_ref, w_ref, o_ref = refs[:3]
            add_ref = None

        def finish(r):
            if has_add:
                r = r + add_ref[...]
            o_ref[...] = r.astype(out_dtype)

        part = None
        for g in range(gb):
            term = lax.dot_general(dy_ref[:, g * tc:(g + 1) * tc], w_ref[g], (((1,), (1,)), ((), ())),
                                   preferred_element_type=F32)
            part = term if part is None else part + term
        if steps == 1:
            finish(part)
        else:
            acc_ref = refs[-1]
            s = pl.program_id(2)

            @pl.when(s == 0)
            def _():
                acc_ref[...] = part

            @pl.when(s > 0)
            def _():
                acc_ref[...] += part

            @pl.when(s == steps - 1)
            def _():
                finish(acc_ref[...])

    in_specs = [
        pl.BlockSpec((None, tm, gb * tc), lambda i, o, s: (s // per_part, i, s % per_part)),
        pl.BlockSpec((gb, tko, tc), lambda i, o, s: (s // nc, o, s % nc)),
    ]
    args = [dy, w]
    if has_add:
        in_specs.append(pl.BlockSpec((tm, tko), lambda i, o, s: (i, o)))
        args.append(add)
    body, in_specs, args = _with_dep(body, in_specs, args, dep)
    return pl.pallas_call(
        body,
        out_shape=jax.ShapeDtypeStruct((M, K), out_dtype),
        grid=(M // tm, K // tko, steps),
        in_specs=in_specs,
        out_specs=pl.BlockSpec((tm, tko), lambda i, o, s: (i, o)),
        scratch_shapes=[pltpu.VMEM((tm, tko), F32)] if steps > 1 else [],
        compiler_params=_params("parallel", "parallel", "arbitrary"),
        name=name,
    )(*args)


def mm_nt_dswiglu(dh, w_down, g_s, u_s, *, name, tm_target=1056, tf_target=512):
    M, D = dh.shape
    _, Fh, D2 = w_down.shape
    assert D == D2
    tm = _tile(M, tm_target, 16)
    tf = _tile(Fh, tf_target, LANES)

    def body(dh_ref, w_ref, g_ref, u_ref, dgu_ref):
        dact = lax.dot_general(dh_ref[...], w_ref[...], (((1,), (1,)), ((), ())),
                               preferred_element_type=F32)
        g = g_ref[...].astype(F32)
        u = u_ref[...].astype(F32)
        sig = jax.nn.sigmoid(g)
        dgu_ref[1] = (dact * (g * sig)).astype(BF)
        dgu_ref[0] = (dact * u * (sig * (1.0 + g * (1.0 - sig)))).astype(BF)

    blk = pl.BlockSpec((tm, tf), lambda i, f: (i, f))
    return pl.pallas_call(
        body,
        out_shape=jax.ShapeDtypeStruct((2, M, Fh), BF),
        grid=(M // tm, Fh // tf),
        in_specs=[
            pl.BlockSpec((tm, D), lambda i, f: (i, 0)),
            pl.BlockSpec((None, tf, D), lambda i, f: (0, f, 0)),
            blk, blk,
        ],
        out_specs=pl.BlockSpec((2, tm, tf), lambda i, f: (0, i, f)),
        compiler_params=_params("parallel", "parallel"),
        name=name,
    )(dh, w_down, g_s, u_s)


def mm_tn(a, dy, groups, *, name, dep=None, out_dtype=BF, tk_target=512, tn_target=1408):
    M, K = a.shape
    if dy.ndim == 2:
        dy = dy.reshape(1, *dy.shape)
    P, M2, Np = dy.shape
    N = P * Np
    assert M == M2 and N % groups == 0
    n = N // groups
    tk = _tile(K, tk_target, LANES)
    tn = _tile(n, tn_target, LANES)
    nj = n // tn
    assert Np % tn == 0
    per_part = Np // tn

    def body(a_ref, dy_ref, o_ref):
        o_ref[...] = lax.dot_general(a_ref[...], dy_ref[...], (((0,), (0,)), ((), ())),
                                     preferred_element_type=F32).astype(out_dtype)

    body, in_specs, args = _with_dep(body, [
        pl.BlockSpec((M, tk), lambda i, j: (0, i)),
        pl.BlockSpec((None, M, tn), lambda i, j: (j // per_part, 0, j % per_part)),
    ], [a, dy], dep)
    return pl.pallas_call(
        body,
        out_shape=jax.ShapeDtypeStruct((groups, K, n), out_dtype),
        grid=(K // tk, groups * nj),
        in_specs=in_specs,
        out_specs=pl.BlockSpec((None, tk, tn), lambda i, j: (j // nj, i, j % nj)),
        compiler_params=_params("parallel", "parallel"),
        name=name,
    )(*args)


def rms_fwd(h, g, *, name, dep=None):
    T, D = h.shape
    tm = _tile(T, 528, 16)

    def body(h_ref, g_ref, o_ref):
        x = h_ref[...]
        r = lax.rsqrt(jnp.mean(x * x, axis=-1, keepdims=True) + EPS)
        o_ref[...] = ((x * r) * g_ref[...]).astype(BF)

    body, in_specs, args = _with_dep(
        body, [pl.BlockSpec((tm, D), lambda i: (i, 0)), pl.BlockSpec((1, D), lambda i: (0, 0))],
        [h, g.reshape(1, D)], dep)
    return pl.pallas_call(
        body,
        out_shape=jax.ShapeDtypeStruct((T, D), BF),
        grid=(T // tm,),
        in_specs=in_specs,
        out_specs=pl.BlockSpec((tm, D), lambda i: (i, 0)),
        compiler_params=_params("parallel"),
        name=name,
    )(*args)


def rms_bwd(dxn, h, g, add, *, name):
    T, D = h.shape
    tm = _tile(T, 264, 16)

    def body(dxn_ref, h_ref, g_ref, add_ref, dh_ref, dhb_ref, dg_ref):
        x = h_ref[...]
        dy = dxn_ref[...]
        r = lax.rsqrt(jnp.mean(x * x, axis=-1, keepdims=True) + EPS)
        xhat = x * r
        part = jnp.sum(dy * xhat, axis=0, keepdims=True)

        @pl.when(pl.program_id(0) == 0)
        def _():
            dg_ref[...] = part

        @pl.when(pl.program_id(0) > 0)
        def _():
            dg_ref[...] += part

        dxh = dy * g_ref[...]
        dh = add_ref[...] + r * (dxh - xhat * jnp.mean(dxh * xhat, axis=-1, keepdims=True))
        dh_ref[...] = dh
        dhb_ref[...] = dh.astype(BF)

    row = pl.BlockSpec((tm, D), lambda i: (i, 0))
    vec = pl.BlockSpec((1, D), lambda i: (0, 0))
    return pl.pallas_call(
        body,
        out_shape=(jax.ShapeDtypeStruct((T, D), F32), jax.ShapeDtypeStruct((T, D), BF),
                   jax.ShapeDtypeStruct((1, D), F32)),
        grid=(T // tm,),
        in_specs=[row, row, vec, row],
        out_specs=(row, row, vec),
        compiler_params=_params("arbitrary"),
        name=name,
    )(dxn, h, g.reshape(1, D), add)


def _head_norm(x, gain):
    r = lax.rsqrt(jnp.mean(x * x, axis=-1, keepdims=True) + EPS)
    return (x * r) * gain


def hn_fwd(qraw, gain, out_scale, *, name):
    T, D = qraw.shape
    H = D // HEAD_DIM
    tm = _tile(T, 528, 16)

    def body(q_ref, g_ref, o_ref):
        gain_v = g_ref[...]
        for hd in range(H):
            sl = slice(hd * HEAD_DIM, (hd + 1) * HEAD_DIM)
            o_ref[:, sl] = (_head_norm(q_ref[:, sl], gain_v) * out_scale).astype(BF)

    return pl.pallas_call(
        body,
        out_shape=jax.ShapeDtypeStruct((T, D), BF),
        grid=(T // tm,),
        in_specs=[pl.BlockSpec((tm, D), lambda i: (i, 0)),
                  pl.BlockSpec((1, HEAD_DIM), lambda i: (0, 0))],
        out_specs=pl.BlockSpec((tm, D), lambda i: (i, 0)),
        compiler_params=_params("parallel"),
        name=name,
    )(qraw, gain.reshape(1, HEAD_DIM))


def kv_post(kv, gain, *, name):
    T, D2 = kv.shape
    D = D2 // 2
    H = D // HEAD_DIM
    tm = _tile(T, 528, 16)

    def body(k_ref, v_ref, g_ref, ko_ref, vo_ref):
        gain_v = g_ref[...]
        for hd in range(H):
            sl = slice(hd * HEAD_DIM, (hd + 1) * HEAD_DIM)
            ko_ref[:, sl] = _head_norm(k_ref[:, sl], gain_v).astype(BF)
        vo_ref[...] = v_ref[...].astype(BF)

    blk = pl.BlockSpec((tm, D), lambda i: (i, 0))
    return pl.pallas_call(
        body,
        out_shape=(jax.ShapeDtypeStruct((T, D), BF), jax.ShapeDtypeStruct((T, D), BF)),
        grid=(T // tm,),
        in_specs=[blk, pl.BlockSpec((tm, D), lambda i: (i, 1)),
                  pl.BlockSpec((1, HEAD_DIM), lambda i: (0, 0))],
        out_specs=(blk, blk),
        compiler_params=_params("parallel"),
        name=name,
    )(kv, kv, gain.reshape(1, HEAD_DIM))


def hn_bwd(dq, qraw, gain, out_scale, *, name):
    T, D = dq.shape
    H = D // HEAD_DIM
    tm = _tile(T, 264, 16)

    def body(dq_ref, q_ref, g_ref, o_ref, dg_ref):
        gain_v = g_ref[...]
        part = jnp.zeros((1, HEAD_DIM), F32)
        for hd in range(H):
            sl = slice(hd * HEAD_DIM, (hd + 1) * HEAD_DIM)
            x = q_ref[:, sl]
            dy = dq_ref[:, sl] * out_scale
            r = lax.rsqrt(jnp.mean(x * x, axis=-1, keepdims=True) + EPS)
            xhat = x * r
            part = part + jnp.sum(dy * xhat, axis=0, keepdims=True)
            dxh = dy * gain_v
            o_ref[:, sl] = (r * (dxh - xhat * jnp.mean(dxh * xhat, axis=-1, keepdims=True))).astype(BF)

        @pl.when(pl.program_id(0) == 0)
        def _():
            dg_ref[...] = part

        @pl.when(pl.program_id(0) > 0)
        def _():
            dg_ref[...] += part

    blk = pl.BlockSpec((tm, D), lambda i: (i, 0))
    vec = pl.BlockSpec((1, HEAD_DIM), lambda i: (0, 0))
    return pl.pallas_call(
        body,
        out_shape=(jax.ShapeDtypeStruct((T, D), BF), jax.ShapeDtypeStruct((1, HEAD_DIM), F32)),
        grid=(T // tm,),
        in_specs=[blk, blk, vec],
        out_specs=(blk, vec),
        compiler_params=_params("arbitrary"),
        name=name,
    )(dq, qraw, gain.reshape(1, HEAD_DIM))


def _shift_down(cur, above, k, rowc):
    out = pltpu.roll(cur, k, 0)
    for i in range(k):
        out = jnp.where(rowc == i, above[8 - k + i:8 - k + i + 1], out)
    return out


def _shift_up(cur, below, k, rowc):
    R = cur.shape[0]
    out = pltpu.roll(cur, R - k, 0)
    for i in range(k):
        out = jnp.where(rowc == R - k + i, below[i:i + 1], out)
    return out


def _conv3(u, u_above, wv, rowc):
    u1 = _shift_down(u, u_above, 1, rowc)
    u2 = _shift_down(u, u_above, 2, rowc)
    return wv[0:1] * u2 + wv[1:2] * u1 + wv[2:3] * u, u1, u2


def conv_fwd(proj, w, *, name):
    T, D3 = proj.shape
    D = D3 // 3
    tc = LANES if D % LANES == 0 else D
    nb = D // tc
    R = _tile(T, 264, 8)

    def body(b_ref, c_ref, h_ref, w_ref, y_ref):
        rowc = lax.broadcasted_iota(jnp.int32, (R, 1), 0)
        wv = w_ref[...]
        for r0 in range(0, T, R):
            rows = slice(r0, r0 + R)
            u = c_ref[rows, :] * h_ref[rows, :]
            if r0 == 0:
                above = jnp.zeros((8, tc), F32)
            else:
                above = c_ref[r0 - 8:r0, :] * h_ref[r0 - 8:r0, :]
            conv, _, _ = _conv3(u, above, wv, rowc)
            y_ref[rows, :] = (b_ref[rows, :] * conv).astype(BF)

    return pl.pallas_call(
        body,
        out_shape=jax.ShapeDtypeStruct((T, D), BF),
        grid=(nb,),
        in_specs=[
            pl.BlockSpec((T, tc), lambda j: (0, j)),
            pl.BlockSpec((T, tc), lambda j: (0, nb + j)),
            pl.BlockSpec((T, tc), lambda j: (0, 2 * nb + j)),
            pl.BlockSpec((3, tc), lambda j: (0, j)),
        ],
        out_specs=pl.BlockSpec((T, tc), lambda j: (0, j)),
        compiler_params=_params("parallel"),
        name=name,
    )(proj, proj, proj, w)


def conv_bwd(dy, proj, w, *, name, dep=None):
    T, D = dy.shape
    tc = LANES if D % LANES == 0 else D
    nb = D // tc
    R = _tile(T, 264, 8)

    def body(dy_ref, b_ref, c_ref, h_ref, w_ref, db_ref, dc_ref, dh_ref, dw_ref):
        rowc = lax.broadcasted_iota(jnp.int32, (R, 1), 0)
        wv = w_ref[...]
        dw = [jnp.zeros((1, tc), F32) for _ in range(3)]
        for r0 in range(0, T, R):
            rows = slice(r0, r0 + R)
            c = c_ref[rows, :]
            hh = h_ref[rows, :]
            u = c * hh
            if r0 == 0:
                above = jnp.zeros((8, tc), F32)
            else:
                above = c_ref[r0 - 8:r0, :] * h_ref[r0 - 8:r0, :]
            conv, u1, u2 = _conv3(u, above, wv, rowc)
            dyv = dy_ref[rows, :]
            db_ref[rows, :] = (dyv * conv).astype(BF)
            dconv = dyv * b_ref[rows, :]
            if r0 + R == T:
                below = jnp.zeros((8, tc), F32)
            else:
                below = dy_ref[r0 + R:r0 + R + 8, :] * b_ref[r0 + R:r0 + R + 8, :]
            dw[0] = dw[0] + jnp.sum(dconv * u2, axis=0, keepdims=True)
            dw[1] = dw[1] + jnp.sum(dconv * u1, axis=0, keepdims=True)
            dw[2] = dw[2] + jnp.sum(dconv * u, axis=0, keepdims=True)
            du = (wv[2:3] * dconv + wv[1:2] * _shift_up(dconv, below, 1, rowc)
                  + wv[0:1] * _shift_up(dconv, below, 2, rowc))
            dc_ref[rows, :] = (du * hh).astype(BF)
            dh_ref[rows, :] = (du * c).astype(BF)
        for i in range(3):
            dw_ref[i:i + 1, :] = dw[i]

    strip = pl.BlockSpec((T, tc), lambda j: (0, j))
    wblk = pl.BlockSpec((3, tc), lambda j: (0, j))
    out = jax.ShapeDtypeStruct((T, D), BF)
    body, in_specs, args = _with_dep(body, [
        strip,
        pl.BlockSpec((T, tc), lambda j: (0, j)),
        pl.BlockSpec((T, tc), lambda j: (0, nb + j)),
        pl.BlockSpec((T, tc), lambda j: (0, 2 * nb + j)),
        wblk,
    ], [dy, proj, proj, proj, w], dep)
    return pl.pallas_call(
        body,
        out_shape=(out, out, out, jax.ShapeDtypeStruct((3, D), F32)),
        grid=(nb,),
        in_specs=in_specs,
        out_specs=(strip, strip, strip, wblk),
        compiler_params=_params("parallel"),
        name=name,
    )(*args)


def _log_sigmoid(z):
    return jnp.minimum(z, 0.0) - jnp.log(1.0 + jnp.exp(-jnp.abs(z)))


def fgate_fwd(logits, bias, pad, *, name):
    T, W = logits.shape
    cb = _tile(T, 128, 8)
    nblk = T // cb

    def body(z_ref, b_ref, c_ref, lf_ref):
        row = lax.broadcasted_iota(jnp.int32, (T, 1), 0)
        lf_ref[...] = jnp.where(row >= pad, _log_sigmoid(z_ref[...] + b_ref[...]), 0.0)
        ri = lax.broadcasted_iota(jnp.int32, (cb, cb), 0)
        ci = lax.broadcasted_iota(jnp.int32, (cb, cb), 1)
        tri = (ci <= ri).astype(F32)

        def step(i, carry):
            rows = pl.ds(pl.multiple_of(i * cb, cb), cb)
            blk = lf_ref[rows, :]
            c_ref[rows, :] = carry + jnp.dot(tri, blk, precision=lax.Precision.HIGHEST,
                                             preferred_element_type=F32)
            return carry + jnp.sum(blk, axis=0, keepdims=True)

        lax.fori_loop(0, nblk, step, jnp.zeros((1, W), F32))

    return pl.pallas_call(
        body,
        out_shape=jax.ShapeDtypeStruct((T, W), F32),
        in_specs=[pl.BlockSpec(memory_space=pltpu.VMEM), pl.BlockSpec(memory_space=pltpu.VMEM)],
        out_specs=pl.BlockSpec(memory_space=pltpu.VMEM),
        scratch_shapes=[pltpu.VMEM((T, W), F32)],
        compiler_params=pltpu.CompilerParams(vmem_limit_bytes=V7X_VMEM_LIMIT),
        name=name,
    )(logits, bias)


def fgate_bwd(dc, logits, bias, pad, *, name):
    T, W = logits.shape
    cb = _tile(T, 128, 8)
    nblk = T // cb

    def body(dc_ref, z_ref, b_ref, dz_ref, db_ref, rs_ref):
        ri = lax.broadcasted_iota(jnp.int32, (cb, cb), 0)
        ci = lax.broadcasted_iota(jnp.int32, (cb, cb), 1)
        triu = (ci >= ri).astype(F32)

        def step(i, carry):
            rows = pl.ds(pl.multiple_of((nblk - 1 - i) * cb, cb), cb)
            blk = dc_ref[rows, :]
            rs_ref[rows, :] = carry + jnp.dot(triu, blk, precision=lax.Precision.HIGHEST,
                                              preferred_element_type=F32)
            return carry + jnp.sum(blk, axis=0, keepdims=True)

        lax.fori_loop(0, nblk, step, jnp.zeros((1, W), F32))
        row = lax.broadcasted_iota(jnp.int32, (T, 1), 0)
        z = z_ref[...] + b_ref[...]
        dz = jnp.where(row >= pad, rs_ref[...] * jax.nn.sigmoid(-z), 0.0)
        dz_ref[...] = dz.astype(BF)
        db_ref[...] = jnp.sum(dz, axis=0, keepdims=True)

    vm = pl.BlockSpec(memory_space=pltpu.VMEM)
    return pl.pallas_call(
        body,
        out_shape=(jax.ShapeDtypeStruct((T, W), BF), jax.ShapeDtypeStruct((1, W), F32)),
        in_specs=[vm, vm, vm],
        out_specs=(vm, vm),
        scratch_shapes=[pltpu.VMEM((T, W), F32)],
        compiler_params=pltpu.CompilerParams(vmem_limit_bytes=V7X_VMEM_LIMIT),
        name=name,
    )(dc, logits, bias)


def _scores(qb, kb, ck):
    return lax.dot_general(qb, kb, (((1,), (1,)), ((), ())), preferred_element_type=F32) - ck


def _causal(s, row, col, pad):
    return jnp.where((col <= row) & (col >= pad), s, NEG)


def attn_fwd(q, k, v, crow, pad, *, name):
    T, D = q.shape
    H = D // HEAD_DIM
    nk, tk = crow.shape[1], crow.shape[3]
    tq = tk
    nq = T // tq

    hp = 2 if H % 2 == 0 else 1
    wide = hp * HEAD_DIM

    def body(q_ref, k_ref, v_ref, cr_ref, o_ref, lse_ref):
        qi = pl.program_id(1)
        row = qi * tq + lax.broadcasted_iota(jnp.int32, (tq, 1), 0)
        heads = [slice(a * HEAD_DIM, (a + 1) * HEAD_DIM) for a in range(hp)]
        qbs = [q_ref[:, sl] for sl in heads]

        def step(kc, carry, masked):
            rows = pl.ds(pl.multiple_of(kc * tk, tk), tk)
            out = []
            for a, sl in enumerate(heads):
                m, l, acc = carry[a]
                s = _scores(qbs[a], k_ref[rows, sl], cr_ref[a, kc])
                if masked:
                    s = _causal(s, row, kc * tk + lax.broadcasted_iota(jnp.int32, (1, tk), 1), pad)
                m_new = jnp.maximum(m, jnp.max(s, axis=-1, keepdims=True))
                alpha = jnp.exp(m - m_new)
                p = jnp.exp(s - m_new)
                l = alpha * l + jnp.sum(p, axis=-1, keepdims=True)
                acc = alpha * acc + jnp.dot(p.astype(BF), v_ref[rows, sl], preferred_element_type=F32)
                out.append((m_new, l, acc))
            return tuple(out)

        init = tuple((jnp.full((tq, 1), NEG, F32), jnp.zeros((tq, 1), F32), jnp.zeros((tq, HEAD_DIM), F32))
                     for _ in heads)
        carry = step(0, init, True)
        carry = lax.fori_loop(1, qi, lambda kc, c: step(kc, c, False), carry)
        carry = lax.cond(qi > 0, lambda c: step(qi, c, True), lambda c: c, carry)
        valid = row >= pad
        for a, sl in enumerate(heads):
            m, l, acc = carry[a]
            o_ref[:, sl] = jnp.where(valid, acc / l, 0.0).astype(BF)
            lse_ref[a] = jnp.where(valid, m + jnp.log(l), 0.0)

    return pl.pallas_call(
        body,
        out_shape=(jax.ShapeDtypeStruct((T, D), BF), jax.ShapeDtypeStruct((H, T, 1), F32)),
        grid=(H // hp, nq),
        in_specs=[
            pl.BlockSpec((tq, wide), lambda h, i: (i, h)),
            pl.BlockSpec((T, wide), lambda h, i: (0, h)),
            pl.BlockSpec((T, wide), lambda h, i: (0, h)),
            pl.BlockSpec((hp, nk, 1, tk), lambda h, i: (h, 0, 0, 0)),
        ],
        out_specs=(pl.BlockSpec((tq, wide), lambda h, i: (i, h)),
                   pl.BlockSpec((hp, tq, 1), lambda h, i: (h, i, 0))),
        compiler_params=_params("parallel", "arbitrary"),
        name=name,
    )(q, k, v, crow)


def attn_bwd(q, k, v, do, o, lse, crow, prev, pad, *, name, dep=None):
    T, D = q.shape
    H = D // HEAD_DIM
    nk, tk = crow.shape[1], crow.shape[3]
    tq = tk
    nq = T // tq
    has_prev = prev is not None

    def body(*refs):
        q_ref, k_ref, v_ref, do_ref, o_ref, lse_ref, cr_ref = refs[:7]
        refs = refs[7:]
        if has_prev:
            pk_ref, pv_ref, pc_ref, pq_ref = refs[:4]
            refs = refs[4:]
        dq_ref, dk_ref, dv_ref, dck_ref, dcq_ref, delta_ref = refs
        kc = pl.program_id(1)

        @pl.when(kc == 0)
        def _():
            dq_ref[...] = jnp.zeros_like(dq_ref)
            dcq_ref[...] = pq_ref[...] if has_prev else jnp.zeros_like(dcq_ref)
            do_used = do_ref[...].astype(BF).astype(F32)
            delta_ref[...] = jnp.sum(do_used * o_ref[...].astype(F32), axis=-1, keepdims=True)

        kb = k_ref[...]
        vb = v_ref[...]
        ck = cr_ref[...]
        col = kc * tk + lax.broadcasted_iota(jnp.int32, (1, tk), 1)

        def step(qi, carry, masked):
            dk, dv, dck = carry
            rows = pl.ds(pl.multiple_of(qi * tq, tq), tq)
            qb = q_ref[rows, :]
            dob = do_ref[rows, :].astype(BF)
            s = _scores(qb, kb, ck)
            if masked:
                s = _causal(s, qi * tq + lax.broadcasted_iota(jnp.int32, (tq, 1), 0), col, pad)
            p = jnp.exp(s - lse_ref[rows, :])
            dp = lax.dot_general(dob, vb, (((1,), (1,)), ((), ())), preferred_element_type=F32)
            ds = p * (dp - delta_ref[rows, :])
            dsb = ds.astype(BF)
            dv = dv + lax.dot_general(p.astype(BF), dob, (((0,), (0,)), ((), ())),
                                      preferred_element_type=F32)
            dk = dk + lax.dot_general(dsb, qb, (((0,), (0,)), ((), ())), preferred_element_type=F32)
            dq_ref[rows, :] += jnp.dot(dsb, kb, preferred_element_type=F32)
            dcq_ref[rows, :] += jnp.sum(ds, axis=1, keepdims=True)
            dck = dck - jnp.sum(ds, axis=0, keepdims=True)
            return dk, dv, dck

        def rest(masked):
            return lambda c: lax.fori_loop(kc + 1, nq, lambda qi, cc: step(qi, cc, masked), c)

        init = (jnp.zeros((tk, HEAD_DIM), F32), jnp.zeros((tk, HEAD_DIM), F32), jnp.zeros((1, tk), F32))
        carry = step(kc, init, True)
        dk, dv, dck = lax.cond(kc == 0, rest(True), rest(False), carry)
        if has_prev:
            dk = dk + pk_ref[...]
            dv = dv + pv_ref[...]
            dck = dck + pc_ref[...]
        dk_ref[...] = dk
        dv_ref[...] = dv
        dck_ref[...] = dck

    head_all = pl.BlockSpec((T, HEAD_DIM), lambda h, j: (0, h))
    head_blk = pl.BlockSpec((tk, HEAD_DIM), lambda h, j: (j, h))
    col_all = pl.BlockSpec((None, T, 1), lambda h, j: (h, 0, 0))
    row_blk = pl.BlockSpec((None, None, 1, tk), lambda h, j: (h, j, 0, 0))
    in_specs = [head_all, head_blk, head_blk, head_all, head_all, col_all, row_blk]
    args = [q, k, v, do, o, lse, crow]
    if has_prev:
        in_specs += [head_blk, head_blk, row_blk, col_all]
        args += list(prev)
    body, in_specs, args = _with_dep(body, in_specs, args, dep)
    return pl.pallas_call(
        body,
        out_shape=(jax.ShapeDtypeStruct((T, D), F32), jax.ShapeDtypeStruct((T, D), F32),
                   jax.ShapeDtypeStruct((T, D), F32), jax.ShapeDtypeStruct((H, nk, 1, tk), F32),
                   jax.ShapeDtypeStruct((H, T, 1), F32)),
        grid=(H, nk),
        in_specs=in_specs,
        out_specs=(head_all, head_blk, head_blk, row_blk, col_all),
        scratch_shapes=[pltpu.VMEM((T, 1), F32)],
        compiler_params=_params("parallel", "arbitrary"),
        name=name,
    )(*args)


def loss_head(h, target, lead, *, name):
    T, D = h.shape
    tm = lead
    assert T % tm == 0 and target.shape[0] % tm == 0
    inv_d = 1.0 / D

    def body(h_ref, t_ref, dh_ref, dhb_ref, loss_ref):
        i = pl.program_id(0)

        @pl.when(i == 0)
        def _():
            dh_ref[...] = jnp.zeros_like(dh_ref)
            dhb_ref[...] = jnp.zeros_like(dhb_ref)
            loss_ref[...] = jnp.zeros_like(loss_ref)

        @pl.when(i > 0)
        def _():
            e = h_ref[...] - t_ref[...]
            dh = e * inv_d
            dh_ref[...] = dh
            dhb_ref[...] = dh.astype(BF)
            loss_ref[...] += 0.5 * inv_d * jnp.sum(e * e)

    return pl.pallas_call(
        body,
        out_shape=(jax.ShapeDtypeStruct((T, D), F32), jax.ShapeDtypeStruct((T, D), BF),
                   jax.ShapeDtypeStruct((8, LANES), F32)),
        grid=(T // tm,),
        in_specs=[pl.BlockSpec((tm, D), lambda i: (i, 0)),
                  pl.BlockSpec((tm, D), lambda i: (jnp.maximum(i - 1, 0), 0))],
        out_specs=(pl.BlockSpec((tm, D), lambda i: (i, 0)), pl.BlockSpec((tm, D), lambda i: (i, 0)),
                   pl.BlockSpec((8, LANES), lambda i: (0, 0))),
        compiler_params=_params("arbitrary"),
        name=name,
    )(h, target)


def adamw(parts, w, m, v, *, name):
    P, R, C = parts.shape
    tr = _tile(R, max(16, (128 * 1024) // C), 16)

    def body(p_ref, w_ref, m_ref, v_ref, g_ref, d_ref, mo_ref, vo_ref):
        g = p_ref[0].astype(F32)
        for i in range(1, P):
            g = g + p_ref[i].astype(F32)
        m_new = ADAM_B1 * m_ref[...] + (1.0 - ADAM_B1) * g
        v_new = ADAM_B2 * v_ref[...] + (1.0 - ADAM_B2) * jnp.square(g)
        m_hat = m_new / (1.0 - ADAM_B1 ** ADAM_STEP)
        v_hat = v_new / (1.0 - ADAM_B2 ** ADAM_STEP)
        g_ref[...] = g
        d_ref[...] = -ADAM_LR * (m_hat / (jnp.sqrt(v_hat) + ADAM_EPS) + ADAM_WD * w_ref[...])
        mo_ref[...] = m_new
        vo_ref[...] = v_new

    blk = pl.BlockSpec((tr, C), lambda i: (i, 0))
    out = jax.ShapeDtypeStruct((R, C), F32)
    return pl.pallas_call(
        body,
        out_shape=(out, out, out, out),
        grid=(R // tr,),
        in_specs=[pl.BlockSpec((P, tr, C), lambda i: (0, i, 0)), blk, blk, blk],
        out_specs=(blk, blk, blk, blk),
        compiler_params=_params("parallel"),
        name=name,
    )(parts, w, m, v)


def _flip(v, bit):
    return 1 - v if bit else v


def all_gather(shard, *, name, dep=None):
    def body(x_ref, out_ref, send_sems, recv_sems, local_sem):
        x, y, c = lax.axis_index("x"), lax.axis_index("y"), lax.axis_index("c")
        me, sibling = (x, y, c), (x, y, 1 - c)
        chips = [(1 - x, y), (x, 1 - y), (1 - x, 1 - y)]

        def block(px, py, pc):
            return out_ref.at[4 * px + 2 * py + pc]

        def copy(k, blk, to, src=None):
            return pltpu.make_async_remote_copy(
                src_ref=block(*blk) if src is None else src,
                dst_ref=block(*blk),
                send_sem=send_sems.at[k],
                recv_sem=recv_sems.at[k],
                device_id=to,
                device_id_type=pl.DeviceIdType.MESH,
            )

        mine = pltpu.make_async_copy(x_ref, block(*me), local_sem)
        mine.start()
        first = [copy(0, me, sibling, src=x_ref)]
        first += [copy(1 + j, me, (*chip, c), src=x_ref) for j, chip in enumerate(chips)]
        for cp in first:
            cp.start()
        passed = [copy(4 + j, (*chip, c), sibling) for j, chip in enumerate(chips)]
        for j, chip in enumerate(chips):
            copy(1 + j, (*chip, c), me).wait_recv()
            passed[j].start()
        copy(0, sibling, me).wait_recv()
        for j, chip in enumerate(chips):
            copy(4 + j, (*chip, 1 - c), me).wait_recv()
        for cp in first + passed:
            cp.wait_send()
        mine.wait()

    body, in_specs, args = _with_dep(body, [pl.BlockSpec(memory_space=pl.ANY)], [shard], dep)
    return pl.pallas_call(
        body,
        out_shape=jax.ShapeDtypeStruct((N_DEV,) + shard.shape, shard.dtype),
        in_specs=in_specs,
        out_specs=pl.BlockSpec(memory_space=pl.ANY),
        scratch_shapes=[pltpu.SemaphoreType.DMA((7,)), pltpu.SemaphoreType.DMA((7,)),
                        pltpu.SemaphoreType.DMA],
        name=name,
    )(*args)


def exchange_slabs(slabs, *, name):
    def body(g_ref, r_ref, send_sems, recv_sems, local_sem):
        x, y, c = lax.axis_index("x"), lax.axis_index("y"), lax.axis_index("c")
        me = 4 * x + 2 * y + c
        mine = pltpu.make_async_copy(g_ref.at[me], r_ref.at[me], local_sem)
        mine.start()
        sends, recvs = [], []
        for k in range(1, N_DEV):
            px, py, pc = _flip(x, (k >> 2) & 1), _flip(y, (k >> 1) & 1), _flip(c, k & 1)
            peer = 4 * px + 2 * py + pc
            sends.append(pltpu.make_async_remote_copy(
                src_ref=g_ref.at[peer], dst_ref=r_ref.at[me],
                send_sem=send_sems.at[k - 1], recv_sem=recv_sems.at[k - 1],
                device_id=(px, py, pc), device_id_type=pl.DeviceIdType.MESH))
            recvs.append(pltpu.make_async_remote_copy(
                src_ref=g_ref.at[peer], dst_ref=r_ref.at[peer],
                send_sem=send_sems.at[k - 1], recv_sem=recv_sems.at[k - 1],
                device_id=(px, py, pc), device_id_type=pl.DeviceIdType.MESH))
        for cp in sends:
            cp.start()
        for cp in recvs:
            cp.wait_recv()
        for cp in sends:
            cp.wait_send()
        mine.wait()

    return pl.pallas_call(
        body,
        out_shape=jax.ShapeDtypeStruct(slabs.shape, slabs.dtype),
        in_specs=[pl.BlockSpec(memory_space=pl.ANY)],
        out_specs=pl.BlockSpec(memory_space=pl.ANY),
        scratch_shapes=[pltpu.SemaphoreType.DMA((7,)), pltpu.SemaphoreType.DMA((7,)),
                        pltpu.SemaphoreType.DMA],
        name=name,
    )(slabs)


def reduce_adamw(slabs, w, m, v, *, name):
    got = exchange_slabs(slabs, name=name + "_xchg")
    return adamw(got, w, m, v, name=name + "_adamw")


_HBM = pl.BlockSpec(memory_space=pltpu.HBM)
_SEM = pl.BlockSpec(memory_space=pltpu.SEMAPHORE)
_ANY = pl.BlockSpec(memory_space=pl.ANY)
_EFFECT = pltpu.SideEffectType.DATAFLOW_SIDE_EFFECTING
_N_FIRST = 4


def _first_copies(land_ref, send_sems, recv_sems):
    x, y, c = lax.axis_index("x"), lax.axis_index("y"), lax.axis_index("c")
    mine = land_ref.at[4 * x + 2 * y + c]
    targets = [(x, y, 1 - c), (1 - x, y, c), (x, 1 - y, c), (1 - x, 1 - y, c)]
    sends, recvs = [], []
    for k, (px, py, pc) in enumerate(targets):
        common = dict(send_sem=send_sems.at[k], recv_sem=recv_sems.at[k], device_id=(px, py, pc),
                      device_id_type=pl.DeviceIdType.MESH)
        sends.append(pltpu.make_async_remote_copy(src_ref=mine, dst_ref=mine, **common))
        theirs = land_ref.at[4 * px + 2 * py + pc]
        recvs.append(pltpu.make_async_remote_copy(src_ref=theirs, dst_ref=theirs, **common))
    return sends, recvs


def _second_copies(land_ref, send_sems, recv_sems):
    x, y, c = lax.axis_index("x"), lax.axis_index("y"), lax.axis_index("c")
    sends, recvs = [], []
    for j, (px, py) in enumerate([(1 - x, y), (x, 1 - y), (1 - x, 1 - y)]):
        common = dict(send_sem=send_sems.at[j], recv_sem=recv_sems.at[j], device_id=(x, y, 1 - c),
                      device_id_type=pl.DeviceIdType.MESH)
        blk = land_ref.at[4 * px + 2 * py + c]
        sends.append(pltpu.make_async_remote_copy(src_ref=blk, dst_ref=blk, **common))
        got = land_ref.at[4 * px + 2 * py + (1 - c)]
        recvs.append(pltpu.make_async_remote_copy(src_ref=got, dst_ref=got, **common))
    return sends, recvs


def gather_start(shard, me, after, *, name):
    R, C = shard.shape
    tr = _tile(R, max(16, (512 * 1024) // C), 16)

    def place_body(me_ref, x_ref, o_ref):
        o_ref[...] = x_ref[...].astype(BF)

    land = pl.pallas_call(
        place_body, name=name + "_own",
        out_shape=jax.ShapeDtypeStruct((N_DEV, R, C), BF),
        grid_spec=pltpu.PrefetchScalarGridSpec(
            num_scalar_prefetch=1,
            grid=(R // tr,),
            in_specs=[pl.BlockSpec((tr, C), lambda i, me_ref: (i, 0))],
            out_specs=pl.BlockSpec((None, tr, C), lambda i, me_ref: (me_ref[0], i, 0)),
        ),
        compiler_params=_params("parallel"),
    )(me.reshape(1).astype(jnp.int32), shard)

    def body(land_ref, after_ref, send_sems, recv_sems, land_thru, token):
        sends, _ = _first_copies(land_ref, send_sems, recv_sems)
        for cp in sends:
            cp.start()
        token[...] = jnp.zeros_like(token)

    send_sems, recv_sems, land_thru, token = pl.pallas_call(
        body, name=name + "_s1",
        out_shape=(pltpu.SemaphoreType.DMA((_N_FIRST,)), pltpu.SemaphoreType.DMA((_N_FIRST,)),
                   pltpu.HBM(land.shape, land.dtype), jax.ShapeDtypeStruct((8, LANES), F32)),
        in_specs=(_HBM, _ANY),
        out_specs=(_SEM, _SEM, _HBM, pl.BlockSpec(memory_space=pltpu.VMEM)),
        input_output_aliases={0: 2},
        compiler_params=pltpu.CompilerParams(has_side_effects=_EFFECT),
    )(pltpu.with_memory_space_constraint(land, pltpu.HBM), after)
    return (send_sems, recv_sems, land_thru), token


def gather_mid(handle, after, *, name):
    send_sems, recv_sems, land_thru = handle

    def body(land_ref, send1, recv1, after_ref, send2, recv2, land_out, token):
        sends, recvs = _first_copies(land_ref, send1, recv1)
        for cp in sends:
            cp.wait_send()
        for cp in recvs:
            cp.wait_recv()
        seconds, _ = _second_copies(land_ref, send2, recv2)
        for cp in seconds:
            cp.start()
        token[...] = jnp.zeros_like(token)

    send2, recv2, land2, token = pl.pallas_call(
        body, name=name + "_s2",
        out_shape=(pltpu.SemaphoreType.DMA((3,)), pltpu.SemaphoreType.DMA((3,)),
                   pltpu.HBM(land_thru.shape, land_thru.dtype), jax.ShapeDtypeStruct((8, LANES), F32)),
        in_specs=(_HBM, _SEM, _SEM, _ANY),
        out_specs=(_SEM, _SEM, _HBM, pl.BlockSpec(memory_space=pltpu.VMEM)),
        input_output_aliases={0: 2},
        compiler_params=pltpu.CompilerParams(has_side_effects=_EFFECT),
    )(land_thru, send_sems, recv_sems, after)
    return (send2, recv2, land2), token


def gather_finish(handle, after, *, name):
    send2, recv2, land2 = handle

    def body(land_ref, send2, recv2, after_ref, got_ref):
        sends, recvs = _second_copies(land_ref, send2, recv2)
        for cp in sends:
            cp.wait_send()
        for cp in recvs:
            cp.wait_recv()

    return pl.pallas_call(
        body, name=name + "_w",
        out_shape=pltpu.HBM(land2.shape, land2.dtype),
        in_specs=(_HBM, _SEM, _SEM, _ANY),
        out_specs=_HBM,
        input_output_aliases={0: 0},
        compiler_params=pltpu.CompilerParams(has_side_effects=_EFFECT),
    )(land2, send2, recv2, after)


def _slab_copies(g_ref, r_ref, send_sems, recv_sems):
    x, y, c = lax.axis_index("x"), lax.axis_index("y"), lax.axis_index("c")
    me = 4 * x + 2 * y + c
    sends, recvs = [], []
    for k in range(1, N_DEV):
        px, py, pc = _flip(x, (k >> 2) & 1), _flip(y, (k >> 1) & 1), _flip(c, k & 1)
        peer = 4 * px + 2 * py + pc
        common = dict(send_sem=send_sems.at[k - 1], recv_sem=recv_sems.at[k - 1], device_id=(px, py, pc),
                      device_id_type=pl.DeviceIdType.MESH)
        sends.append(pltpu.make_async_remote_copy(src_ref=g_ref.at[peer], dst_ref=r_ref.at[me], **common))
        recvs.append(pltpu.make_async_remote_copy(src_ref=g_ref.at[peer], dst_ref=r_ref.at[peer], **common))
    return sends, recvs


def exchange_start(slabs, *, name):
    land = lax.empty(slabs.shape, slabs.dtype)

    def body(g_ref, r_ref, send_sems, recv_sems, g_thru, r_thru, token):
        sends, _ = _slab_copies(g_ref, r_ref, send_sems, recv_sems)
        for cp in sends:
            cp.start()
        token[...] = jnp.zeros_like(token)

    send_sems, recv_sems, g_thru, r_thru, token = pl.pallas_call(
        body, name=name,
        out_shape=(pltpu.SemaphoreType.DMA((N_DEV - 1,)), pltpu.SemaphoreType.DMA((N_DEV - 1,)),
                   pltpu.HBM(slabs.shape, slabs.dtype), pltpu.HBM(slabs.shape, slabs.dtype),
                   jax.ShapeDtypeStruct((8, LANES), F32)),
        in_specs=(_HBM, _HBM),
        out_specs=(_SEM, _SEM, _HBM, _HBM, pl.BlockSpec(memory_space=pltpu.VMEM)),
        input_output_aliases={0: 2, 1: 3},
        compiler_params=pltpu.CompilerParams(has_side_effects=_EFFECT),
    )(pltpu.with_memory_space_constraint(slabs, pltpu.HBM), pltpu.with_memory_space_constraint(land, pltpu.HBM))
    return (send_sems, recv_sems, g_thru, r_thru), token


def exchange_finish(handle, after, *, name):
    send_sems, recv_sems, g_thru, r_thru = handle

    def body(g_ref, r_ref, send_sems, recv_sems, after_ref, g_out, r_out):
        sends, recvs = _slab_copies(g_ref, r_ref, send_sems, recv_sems)
        for cp in sends:
            cp.wait_send()
        for cp in recvs:
            cp.wait_recv()

    return pl.pallas_call(
        body, name=name,
        out_shape=(pltpu.HBM(g_thru.shape, g_thru.dtype), pltpu.HBM(r_thru.shape, r_thru.dtype)),
        in_specs=(_HBM, _HBM, _SEM, _SEM, _ANY),
        out_specs=(_HBM, _HBM),
        input_output_aliases={0: 0, 1: 1},
        compiler_params=pltpu.CompilerParams(has_side_effects=_EFFECT),
    )(g_thru, r_thru, send_sems, recv_sems, after)


def adamw_own(own, got, me, w, m, v, layer, prev, *, name):
    P, R, C = got.shape
    L = w.shape[0]
    tr = _tile(R, max(16, (256 * 1024) // C), 16)

    def body(me_ref, own_ref, p_ref, w_ref, m_ref, v_ref, *rest):
        g_ref, d_ref, mo_ref, vo_ref = rest[-4:]
        mine = own_ref[...].astype(F32)
        g = None
        for i in range(P):
            term = jnp.where(me_ref[0] == i, mine, p_ref[i].astype(F32))
            g = term if g is None else g + term
        m_new = ADAM_B1 * m_ref[...] + (1.0 - ADAM_B1) * g
        v_new = ADAM_B2 * v_ref[...] + (1.0 - ADAM_B2) * jnp.square(g)
        m_hat = m_new / (1.0 - ADAM_B1 ** ADAM_STEP)
        v_hat = v_new / (1.0 - ADAM_B2 ** ADAM_STEP)
        g_ref[...] = g
        d_ref[...] = -ADAM_LR * (m_hat / (jnp.sqrt(v_hat) + ADAM_EPS) + ADAM_WD * w_ref[...])
        mo_ref[...] = m_new
        vo_ref[...] = v_new

    blk = pl.BlockSpec((None, tr, C), lambda i, me_ref: (layer, i, 0))
    out = jax.ShapeDtypeStruct((L, R, C), F32)
    in_specs = [pl.BlockSpec((None, tr, C), lambda i, me_ref: (me_ref[0], i, 0)),
                pl.BlockSpec((P, tr, C), lambda i, me_ref: (0, i, 0)), blk, blk, blk]
    args = [me.reshape(1).astype(jnp.int32), own, got, w, m, v]
    aliases = {}
    if prev is not None:
        in_specs += [pl.BlockSpec(memory_space=pl.ANY)] * 4
        aliases = {len(args) + i: i for i in range(4)}
        args += list(prev)
    return pl.pallas_call(
        body,
        out_shape=(out, out, out, out),
        grid_spec=pltpu.PrefetchScalarGridSpec(
            num_scalar_prefetch=1,
            grid=(R // tr,),
            in_specs=in_specs,
            out_specs=(blk, blk, blk, blk),
        ),
        input_output_aliases=aliases,
        compiler_params=_params("parallel"),
        name=name,
    )(*args)


def _pad_rows(a, rows):
    return jnp.pad(a, ((0, rows - a.shape[0]), (0, 0)))


def _pad_cols(a, cols):
    return jnp.pad(a, ((0, 0), (0, cols - a.shape[1])))


def kernel(x, meta, a_norm, a_w_in, a_conv, a_w_out, kv_norm, w_kv, k_norm, w_f, b_f, b_norm, b_w_q, b_q_norm, b_w_o, ffn_norm, ffn_w_gu, ffn_w_down, loss_target, m_meta, m_a_norm, m_a_w_in, m_a_conv, m_a_w_out, m_kv_norm, m_w_kv, m_k_norm, m_w_f, m_b_f, m_b_norm, m_b_w_q, m_b_q_norm, m_b_w_o, m_ffn_norm, m_ffn_w_gu, m_ffn_w_down, v_meta, v_a_norm, v_a_w_in, v_a_conv, v_a_w_out, v_kv_norm, v_w_kv, v_k_norm, v_w_f, v_b_f, v_b_norm, v_b_w_q, v_b_q_norm, v_b_w_o, v_ffn_norm, v_ffn_w_gu, v_ffn_w_down):
    S, D = x.shape[1], x.shape[2]
    n_meta = meta.shape[0]
    Ds = meta.shape[1]
    H = D // HEAD_DIM
    n_a, n_b = a_w_in.shape[0], b_w_q.shape[0]
    depth = n_a + n_b
    Fs = ffn_w_down.shape[1]
    pad = BLOCK - n_meta
    lead = pad + n_meta
    T = lead + S
    tk_attn = _tile(T, 384, LANES)
    nk_attn = T // tk_attn
    q_scale = 1.0 / math.sqrt(HEAD_DIM)
    my = 4 * lax.axis_index("x") + 2 * lax.axis_index("y") + lax.axis_index("c")

    wf_t = w_f.reshape(H, Ds)
    small = jnp.concatenate([meta, _pad_rows(a_norm, 8), _pad_rows(a_conv.reshape(n_a * 3, Ds), 8), wf_t], axis=0)
    r_an, r_ac, r_wf = n_meta, n_meta + 8, n_meta + 16
    gs = all_gather(small, name="ag_small")
    unshard = lambda blk: jnp.transpose(blk, (1, 0, 2)).reshape(blk.shape[1], D)
    meta_full = unshard(gs[:, 0:n_meta])
    a_norm_full = unshard(gs[:, r_an:r_an + n_a])
    a_conv_full = unshard(gs[:, r_ac:r_ac + 3 * n_a]).reshape(n_a, 3, D)
    w_f_full = gs[:, r_wf:r_wf + H].reshape(D, H)
    wf_pad = _pad_cols(w_f_full, LANES).astype(BF)[None]
    bf_pad = _pad_cols(b_f.reshape(1, H), LANES)

    def layer_shards(l):
        if l < n_a:
            mix = [(("in", l), a_w_in[l]), (("out", l), a_w_out[l])]
        else:
            j = l - n_a
            mix = ([(("kv", 0), w_kv)] if j == 0 else []) + [(("q", j), b_w_q[j]), (("o", j), b_w_o[j])]
        return mix + [(("gu", l), ffn_w_gu[l]), (("dn", l), ffn_w_down[l])]

    first_level, second_level, W = {}, {}, {}
    st = {"done": None, "tok": None}

    def note(val):
        st["done"] = val
        return val

    def take():
        tok, st["tok"] = st["tok"], None
        return tok

    def chain_after(default):
        if st["tok"] is not None:
            return st["tok"]
        return default if st["done"] is None else st["done"]

    def ag_name(key):
        return f"ag_{key[0]}{key[1]}"

    def start_layer(l):
        for key, shard in layer_shards(l):
            first_level[key], st["tok"] = gather_start(shard, my, chain_after(shard), name=ag_name(key))

    def pass_on(keys):
        for key in keys:
            second_level[key], st["tok"] = gather_mid(first_level.pop(key), chain_after(None), name=ag_name(key))

    def weight(key, shape=None):
        w = gather_finish(second_level.pop(key), st["done"], name=ag_name(key))
        W[key] = w if shape is None else w.reshape(shape)
        return W[key]

    def layer_keys(l):
        keys = [key for key, _ in layer_shards(l)]
        return keys[:-2], keys[-2:]

    h = note(jnp.concatenate([jnp.zeros((pad, D), F32), meta_full, x[0]], axis=0))
    start_layer(0)
    pass_on(layer_keys(0)[0])
    saved = []
    shared = None
    for l in range(depth):
        rec = {"h": h}
        mix_keys, ffn_keys = layer_keys(l)

        def ahead():
            if l >= 1:
                pass_on(ffn_keys[:1])
            if l + 1 < depth:
                start_layer(l + 1)

        if l < n_a:
            xn = note(rms_fwd(h, a_norm_full[l], name=f"a{l}_norm", dep=take()))
            ahead()
            proj = note(mm_nn(xn, weight(("in", l)), name=f"a{l}_in", dep=take()))
            if l == 0:
                pass_on(ffn_keys[:1])
            y = note(conv_fwd(proj, a_conv_full[l], name=f"a{l}_conv"))
            h1 = note(mm_nn(y, weight(("out", l), (1, D, D)), add=h, name=f"a{l}_out", dep=take()))
            rec.update(xn=xn, proj=proj, y=y)
        else:
            j = l - n_a
            if j == 0:
                xnk = note(rms_fwd(h, kv_norm, name="kv_norm", dep=take()))
                ahead()
                kv = note(mm_nn(xnk, weight(("kv", 0)), name="kv_proj", dep=take()))
                k, v = kv_post(kv, k_norm, name="kv_post")
                logits = mm_nn(xnk, wf_pad, name="f_logits", tn_target=LANES)
                cfull = fgate_fwd(logits, bf_pad, pad, name="f_gate")
                crow = jnp.transpose(cfull[:, :H]).reshape(H, nk_attn, 1, tk_attn)
                shared = dict(h=h, xnk=xnk, kv=kv, logits=logits)
                xn = note(rms_fwd(h, b_norm[j], name=f"b{j}_norm"))
            else:
                xn = note(rms_fwd(h, b_norm[j], name=f"b{j}_norm", dep=take()))
                ahead()
            qraw = note(mm_nn(xn, weight(("q", j), (1, D, D)), name=f"b{j}_q", dep=take()))
            q = hn_fwd(qraw, b_q_norm[j], q_scale, name=f"b{j}_qnorm")
            o, lse = attn_fwd(q, k, v, crow, pad, name=f"b{j}_attn")
            note(o)
            h1 = note(mm_nn(o, weight(("o", j), (1, D, D)), add=h, name=f"b{j}_o"))
            rec.update(xn=xn, qraw=qraw, q=q, o=o, lse=lse)
        xn2 = note(rms_fwd(h1, ffn_norm[l], name=f"f{l}_norm", dep=take()))
        pass_on(ffn_keys[1:])
        act, g_s, u_s = mm_swiglu(xn2, weight(("gu", l)), name=f"f{l}_gu", dep=take())
        note(act)
        if l + 1 < depth:
            pass_on(layer_keys(l + 1)[0])
        h = note(mm_nn(act, weight(("dn", l), (1, N_DEV * Fs, D)), add=h1, name=f"f{l}_down", resident=True,
                       tm_target=528, tn_target=512, dep=take()))
        rec.update(h1=h1, xn2=xn2, act=act, g=g_s, u=u_s)
        saved.append(rec)

    dh, dhb, loss_tile = loss_head(h, loss_target[0], lead, name="loss")
    loss = lax.psum(loss_tile[0, 0], MESH_AXES)

    upd = {}
    small_g = {}
    inflight = []

    def big(name, l, section, slabs, w, m, v):
        handle, st["tok"] = exchange_start(slabs.reshape(N_DEV, -1, w.shape[-1]), name=f"{name}{l}_xs")
        inflight.append((section, name, l, handle, w, m, v))

    def land(sections, after):
        for entry in [e for e in inflight if sections is None or e[0] in sections]:
            inflight.remove(entry)
            _, name, l, handle, w, m, v = entry
            own, got = exchange_finish(handle, after, name=f"{name}{l}_xw")
            flat = lambda t: t.reshape(w.shape[0], -1, w.shape[-1])
            upd[name] = adamw_own(own, got, my, flat(w), flat(m), flat(v), l, upd.get(name),
                                  name=f"{name}{l}_adamw")

    dk = dv = dck = dcq = None
    for l in reversed(range(depth)):
        rec = saved[l]
        land([("ffn", l + 1)], dh)
        dgu = mm_nt_dswiglu(dhb, W[("dn", l)], rec["g"], rec["u"], name=f"f{l}_ddown")
        big("ffn_w_down", l, ("ffn", l), mm_tn(rec["act"], dhb, 1, name=f"f{l}_wdown"),
            ffn_w_down, m_ffn_w_down, v_ffn_w_down)
        big("ffn_w_gu", l, ("ffn", l), mm_tn(rec["xn2"], dgu, N_DEV, name=f"f{l}_wgu", dep=take()),
            ffn_w_gu, m_ffn_w_gu, v_ffn_w_gu)
        dxn2 = mm_nt(dgu, W[("gu", l)], name=f"f{l}_dgu", gb=2, dep=take())
        dh1, dhb, dgf = rms_bwd(dxn2, rec["h1"], ffn_norm[l], dh, name=f"f{l}_dnorm")
        small_g[("ffn_norm", l)] = dgf
        land([("mix", l + 1)], dh1)
        if l < n_a:
            dy = mm_nt(dhb, W[("out", l)], name=f"a{l}_dout")
            big("a_w_out", l, ("mix", l), mm_tn(rec["y"], dhb, 1, name=f"a{l}_wout"),
                a_w_out, m_a_w_out, v_a_w_out)
            db, dc, dhh, dcw = conv_bwd(dy, rec["proj"], a_conv_full[l], name=f"a{l}_dconv", dep=take())
            small_g[("a_conv", l)] = dcw
            dproj = jnp.concatenate([db, dc, dhh], axis=1)
            big("a_w_in", l, ("mix", l), mm_tn(rec["xn"], dproj, N_DEV, name=f"a{l}_win"),
                a_w_in, m_a_w_in, v_a_w_in)
            dxn = mm_nt(dproj, W[("in", l)], name=f"a{l}_din", gb=4, dep=take())
            dh, dhb, dga = rms_bwd(dxn, rec["h"], a_norm_full[l], dh1, name=f"a{l}_dnorm")
            small_g[("a_norm", l)] = dga
        else:
            j = l - n_a
            do = mm_nt(dhb, W[("o", j)], name=f"b{j}_do")
            big("b_w_o", j, ("mix", l), mm_tn(rec["o"], dhb, 1, name=f"b{j}_wo"),
                b_w_o, m_b_w_o, v_b_w_o)
            prev = None if dk is None else (dk, dv, dck, dcq)
            dq, dk, dv, dck, dcq = attn_bwd(rec["q"], k, v, do, rec["o"], rec["lse"], crow, prev, pad,
                                            name=f"b{j}_dattn", dep=take())
            dqraw, dqn = hn_bwd(dq, rec["qraw"], b_q_norm[j], q_scale, name=f"b{j}_dqnorm")
            small_g[("b_q_norm", j)] = dqn
            big("b_w_q", j, ("mix", l), mm_tn(rec["xn"], dqraw, 1, name=f"b{j}_wq"),
                b_w_q, m_b_w_q, v_b_w_q)
            dxn = mm_nt(dqraw, W[("q", j)], name=f"b{j}_dq", dep=take())
            dh, dhb, dgb = rms_bwd(dxn, rec["h"], b_norm[j], dh1, name=f"b{j}_dnorm")
            small_g[("b_norm", j)] = dgb
            if j == 0:
                dkraw, dkn = hn_bwd(dk, shared["kv"], k_norm, 1.0, name="kv_dknorm")
                dkv = jnp.concatenate([dkraw, dv.astype(BF)], axis=1)
                dc_full = _pad_cols(jnp.transpose(dck.reshape(H, T) + dcq.reshape(H, T)), LANES)
                dz, dbf = fgate_bwd(dc_full, shared["logits"], bf_pad, pad, name="f_dgate")
                big("w_kv", 0, ("mix", l), mm_tn(shared["xnk"], dkv, N_DEV, name="kv_wkv"),
                    w_kv[None], m_w_kv[None], v_w_kv[None])
                dwf_t = mm_tn(dz, shared["xnk"], 1, name="f_wf", out_dtype=F32, tn_target=1024,
                              dep=take())[0, :H]
                dxn_f = mm_nt(dz, wf_pad, name="f_dxn")
                dxnk = mm_nt(dkv, W[("kv", 0)], add=dxn_f, name="kv_dxn", gb=4)
                dh, dhb, dgkv = rms_bwd(dxnk, shared["h"], kv_norm, dh, name="kv_dnorm")
    land(None, dh)

    grad_x = dh[lead:][None]

    row8 = lambda a: _pad_rows(_pad_cols(a, D), 8)
    stack = lambda key, n: jnp.concatenate([small_g[(key, i)] for i in range(n)], axis=0)
    g_sharded = jnp.concatenate([dh[pad:lead], row8(stack("a_norm", n_a)), row8(stack("a_conv", n_a)), dwf_t], axis=0)
    g_repl = jnp.concatenate([row8(jnp.concatenate([dgkv, stack("b_norm", n_b)], axis=0)),
                              row8(stack("ffn_norm", depth)),
                              row8(jnp.concatenate([_pad_cols(dkn, D), _pad_cols(stack("b_q_norm", n_b), D),
                                                    _pad_cols(dbf[:, :H], D)], axis=0))], axis=0)
    n_sh = g_sharded.shape[0]
    gathered = all_gather(jnp.concatenate([g_sharded, g_repl], axis=0), name="ag_small_grads",
                          dep=upd["a_w_in"][0])
    parts_sh = lax.dynamic_slice_in_dim(gathered[:, :n_sh], my * Ds, Ds, axis=2)
    parts_rp = gathered[:, n_sh:]

    def pack_sh(t_meta, t_an, t_ac, t_wf):
        return jnp.concatenate([t_meta, _pad_rows(t_an, 8), _pad_rows(t_ac.reshape(n_a * 3, Ds), 8),
                                jnp.transpose(t_wf)], axis=0)

    def pack_rp(t_kv, t_bn, t_fn, t_kn, t_qn, t_bf):
        return jnp.concatenate([row8(jnp.concatenate([t_kv.reshape(1, D), t_bn], axis=0)), row8(t_fn),
                                row8(jnp.concatenate([_pad_cols(t_kn.reshape(1, -1), D), _pad_cols(t_qn, D),
                                                      _pad_cols(t_bf.reshape(1, -1), D)], axis=0))], axis=0)

    res_sh = adamw(parts_sh, pack_sh(meta, a_norm, a_conv, w_f), pack_sh(m_meta, m_a_norm, m_a_conv, m_w_f),
                   pack_sh(v_meta, v_a_norm, v_a_conv, v_w_f), name="small_sharded_adamw")
    res_rp = adamw(parts_rp, pack_rp(kv_norm, b_norm, ffn_norm, k_norm, b_q_norm, b_f),
                   pack_rp(m_kv_norm, m_b_norm, m_ffn_norm, m_k_norm, m_b_q_norm, m_b_f),
                   pack_rp(v_kv_norm, v_b_norm, v_ffn_norm, v_k_norm, v_b_q_norm, v_b_f), name="small_repl_adamw")

    def unpack(kind):
        sh, rp = res_sh[kind], res_rp[kind]
        out = {
            "meta": sh[0:n_meta],
            "a_norm": sh[r_an:r_an + n_a],
            "a_conv": sh[r_ac:r_ac + 3 * n_a].reshape(n_a, 3, Ds),
            "w_f": jnp.transpose(sh[r_wf:r_wf + H]),
            "kv_norm": rp[0],
            "b_norm": rp[1:1 + n_b],
            "ffn_norm": rp[8:8 + depth],
            "k_norm": rp[16, :HEAD_DIM],
            "b_q_norm": rp[17:17 + n_b, :HEAD_DIM],
            "b_f": rp[17 + n_b, :H],
        }
        for name, like in (("a_w_in", a_w_in), ("a_w_out", a_w_out), ("b_w_q", b_w_q), ("b_w_o", b_w_o),
                           ("ffn_w_gu", ffn_w_gu), ("ffn_w_down", ffn_w_down)):
            out[name] = upd[name][kind].reshape(like.shape)
        out["w_kv"] = upd["w_kv"][kind].reshape(w_kv.shape)
        return out

    order = ["meta", "a_norm", "a_w_in", "a_conv", "a_w_out", "kv_norm", "w_kv", "k_norm", "w_f", "b_f",
             "b_norm", "b_w_q", "b_q_norm", "b_w_o", "ffn_norm", "ffn_w_gu", "ffn_w_down"]
    outs = [loss, grad_x]
    for kind in range(4):
        vals = unpack(kind)
        outs += [vals[n] for n in order]
    return tuple(outs)
```

```python
import functools
import math

import jax
import jax.numpy as jnp
from jax import lax
from jax.experimental import pallas as pl
from jax.experimental.pallas import tpu as pltpu

N_DEV = 8
MESH_AXES = ("x", "y", "c")
EPS = 1e-6
NEG = -1e30
HEAD_DIM = 128
BLOCK = 128
LANES = 128
V7X_VMEM_LIMIT = 56 * 1024 * 1024

ADAM_LR = 0.001
ADAM_B1 = 0.9
ADAM_B2 = 0.999
ADAM_EPS = 1e-08
ADAM_WD = 0.01
ADAM_STEP = 10

BF = jnp.bfloat16
F32 = jnp.float32


def _tile(n, target, mult):
    best = None
    for t in range(mult, min(n, target) + 1, mult):
        if n % t == 0:
            best = t
    return n if best is None else best


def _params(*sem):
    return pltpu.CompilerParams(dimension_semantics=sem, vmem_limit_bytes=V7X_VMEM_LIMIT)


def _with_dep(body, in_specs, args, dep):
    if dep is None:
        return body, list(in_specs), list(args)
    n_in = len(args)

    def body_dep(*refs):
        body(*refs[:n_in], *refs[n_in + 1:])

    return body_dep, list(in_specs) + [pl.BlockSpec(memory_space=pl.ANY)], list(args) + [dep]


def mm_nn(a, w, *, name, add=None, dep=None, out_dtype=F32, tm_target=1056, tn_target=1024, tk_target=2048,
          resident=False):
    M, K = a.shape
    G, K2, n = w.shape
    assert K == K2
    tm = _tile(M, tm_target, 16)
    tn = _tile(n, tn_target, LANES)
    tk = K if resident else _tile(K, tk_target, LANES)
    nj, nk = n // tn, K // tk
    has_add = add is not None
    if resident:
        grid, sem = (G * nj, M // tm), ("parallel", "parallel")
        order = lambda f: (lambda j, i: f(i, j, 0))
        w_mode = dict(pipeline_mode=pl.Buffered(1))
    else:
        grid, sem = (M // tm, G * nj, nk), ("parallel", "parallel", "arbitrary")
        order = lambda f: f
        w_mode = {}

    def body(*refs):
        if has_add:
            a_ref, w_ref, add_ref, o_ref = refs[:4]
        else:
            a_ref, w_ref, o_ref = refs[:3]
            add_ref = None

        def finish(r):
            if has_add:
                r = r + add_ref[...]
            o_ref[...] = r.astype(out_dtype)

        part = jnp.dot(a_ref[...], w_ref[...], preferred_element_type=F32)
        if nk == 1:
            finish(part)
        else:
            acc_ref = refs[-1]
            k = pl.program_id(2)

            @pl.when(k == 0)
            def _():
                acc_ref[...] = part

            @pl.when(k > 0)
            def _():
                acc_ref[...] += part

            @pl.when(k == nk - 1)
            def _():
                finish(acc_ref[...])

    in_specs = [
        pl.BlockSpec((tm, tk), order(lambda i, j, k: (i, k))),
        pl.BlockSpec((None, tk, tn), order(lambda i, j, k: (j // nj, k, j % nj)), **w_mode),
    ]
    args = [a, w]
    if has_add:
        in_specs.append(pl.BlockSpec((tm, tn), order(lambda i, j, k: (i, j))))
        args.append(add)
    body, in_specs, args = _with_dep(body, in_specs, args, dep)
    return pl.pallas_call(
        body,
        out_shape=jax.ShapeDtypeStruct((M, G * n), out_dtype),
        grid=grid,
        in_specs=in_specs,
        out_specs=pl.BlockSpec((tm, tn), order(lambda i, j, k: (i, j))),
        scratch_shapes=[pltpu.VMEM((tm, tn), F32)] if nk > 1 else [],
        compiler_params=_params(*sem),
        name=name,
    )(*args)


def mm_swiglu(xn, wgu, *, name, dep=None, save_dtype=BF, tm_target=528):
    M, K = xn.shape
    G, _, n = wgu.shape
    half = G // 2
    tm = _tile(M, tm_target, 16)
    tn = _tile(n, 1408, LANES)
    nj = n // tn
    Fh = half * n

    def body(a_ref, wg_ref, wu_ref, act_ref, silu_ref, udsilu_ref):
        a = a_ref[...]
        g = jnp.dot(a, wg_ref[...], preferred_element_type=F32)
        sig = jax.nn.sigmoid(g)
        silu = g * sig
        silu_ref[...] = silu.astype(save_dtype)
        dsilu = sig * (1.0 + g * (1.0 - sig))
        u = jnp.dot(a, wu_ref[...], preferred_element_type=F32)
        udsilu_ref[...] = (u * dsilu).astype(save_dtype)
        act_ref[...] = (silu * u).astype(BF)

    out_block = pl.BlockSpec((tm, tn), lambda j, i: (i, j))
    once = pl.Buffered(1)
    body, in_specs, args = _with_dep(body, [
        pl.BlockSpec((tm, K), lambda j, i: (i, 0)),
        pl.BlockSpec((None, K, tn), lambda j, i: (j // nj, 0, j % nj), pipeline_mode=once),
        pl.BlockSpec((None, K, tn), lambda j, i: (half + j // nj, 0, j % nj), pipeline_mode=once),
    ], [xn, wgu, wgu], dep)
    return pl.pallas_call(
        body,
        out_shape=(jax.ShapeDtypeStruct((M, Fh), BF),
                   jax.ShapeDtypeStruct((M, Fh), save_dtype),
                   jax.ShapeDtypeStruct((M, Fh), save_dtype)),
        grid=(half * nj, M // tm),
        in_specs=in_specs,
        out_specs=(out_block, out_block, out_block),
        compiler_params=_params("parallel", "parallel"),
        name=name,
    )(*args)


def mm_nt(dy, w, *, name, add=None, dep=None, out_dtype=F32, tm_target=1056, tko_target=1024, tc_target=2048,
          gb=1):
    if dy.ndim == 2:
        dy = dy.reshape(1, *dy.shape)
    P, M, Np = dy.shape
    G, K, n = w.shape
    assert P * Np == G * n
    tm = _tile(M, tm_target, 16)
    tko = _tile(K, tko_target, LANES)
    tc = _tile(n, tc_target, LANES)
    nc = n // tc
    gb = gb if nc == 1 else 1
    assert G % gb == 0 and Np % (gb * tc) == 0
    steps = (G // gb) * nc
    per_part = Np // (gb * tc)
    has_add = add is not None

    def body(*refs):
        if has_add:
            dy_ref, w_ref, add_ref, o_ref = refs[:4]
        else:
            dy_ref, w_ref, o_ref = refs[:3]
            add_ref = None

        def finish(r):
            if has_add:
                r = r + add_ref[...]
            o_ref[...] = r.astype(out_dtype)

        part = None
        for g in range(gb):
            term = lax.dot_general(dy_ref[:, g * tc:(g + 1) * tc], w_ref[g], (((1,), (1,)), ((), ())),
                                   preferred_element_type=F32)
            part = term if part is None else part + term
        if steps == 1:
            finish(part)
        else:
            acc_ref = refs[-1]
            s = pl.program_id(2)

            @pl.when(s == 0)
            def _():
                acc_ref[...] = part

            @pl.when(s > 0)
            def _():
                acc_ref[...] += part

            @pl.when(s == steps - 1)
            def _():
                finish(acc_ref[...])

    in_specs = [
        pl.BlockSpec((None, tm, gb * tc), lambda i, o, s: (s // per_part, i, s % per_part)),
        pl.BlockSpec((gb, tko, tc), lambda i, o, s: (s // nc, o, s % nc)),
    ]
    args = [dy, w]
    if has_add:
        in_specs.append(pl.BlockSpec((tm, tko), lambda i, o, s: (i, o)))
        args.append(add)
    body, in_specs, args = _with_dep(body, in_specs, args, dep)
    return pl.pallas_call(
        body,
        out_shape=jax.ShapeDtypeStruct((M, K), out_dtype),
        grid=(M // tm, K // tko, steps),
        in_specs=in_specs,
        out_specs=pl.BlockSpec((tm, tko), lambda i, o, s: (i, o)),
        scratch_shapes=[pltpu.VMEM((tm, tko), F32)] if steps > 1 else [],
        compiler_params=_params("parallel", "parallel", "arbitrary"),
        name=name,
    )(*args)


def mm_nt_dswiglu(dh, w_down, g_s, u_s, *, name, tm_target=1056, tf_target=512):
    M, D = dh.shape
    _, Fh, D2 = w_down.shape
    assert D == D2
    tm = _tile(M, tm_target, 16)
    tf = _tile(Fh, tf_target, LANES)

    def body(dh_ref, w_ref, silu_ref, udsilu_ref, dgu_ref):
        dact = lax.dot_general(dh_ref[...], w_ref[...], (((1,), (1,)), ((), ())),
                               preferred_element_type=F32)
        dgu_ref[1] = (dact * silu_ref[...].astype(F32)).astype(BF)
        dgu_ref[0] = (dact * udsilu_ref[...].astype(F32)).astype(BF)

    blk = pl.BlockSpec((tm, tf), lambda i, f: (i, f))
    return pl.pallas_call(
        body,
        out_shape=jax.ShapeDtypeStruct((2, M, Fh), BF),
        grid=(M // tm, Fh // tf),
        in_specs=[
            pl.BlockSpec((tm, D), lambda i, f: (i, 0)),
            pl.BlockSpec((None, tf, D), lambda i, f: (0, f, 0)),
            blk, blk,
        ],
        out_specs=pl.BlockSpec((2, tm, tf), lambda i, f: (0, i, f)),
        compiler_params=_params("parallel", "parallel"),
        name=name,
    )(dh, w_down, g_s, u_s)


def mm_tn(a, dy, groups, *, name, dep=None, out_dtype=BF, tk_target=512, tn_target=1408):
    M, K = a.shape
    if dy.ndim == 2:
        dy = dy.reshape(1, *dy.shape)
    P, M2, Np = dy.shape
    N = P * Np
    assert M == M2 and N % groups == 0
    n = N // groups
    tk = _tile(K, tk_target, LANES)
    tn = _tile(n, tn_target, LANES)
    nj = n // tn
    assert Np % tn == 0
    per_part = Np // tn

    def body(a_ref, dy_ref, o_ref):
        o_ref[...] = lax.dot_general(a_ref[...], dy_ref[...], (((0,), (0,)), ((), ())),
                                     preferred_element_type=F32).astype(out_dtype)

    body, in_specs, args = _with_dep(body, [
        pl.BlockSpec((M, tk), lambda i, j: (0, i)),
        pl.BlockSpec((None, M, tn), lambda i, j: (j // per_part, 0, j % per_part)),
    ], [a, dy], dep)
    return pl.pallas_call(
        body,
        out_shape=jax.ShapeDtypeStruct((groups, K, n), out_dtype),
        grid=(K // tk, groups * nj),
        in_specs=in_specs,
        out_specs=pl.BlockSpec((None, tk, tn), lambda i, j: (j // nj, i, j % nj)),
        compiler_params=_params("parallel", "parallel"),
        name=name,
    )(*args)


def rms_fwd(h, g, *, name, dep=None):
    T, D = h.shape
    tm = _tile(T, 528, 16)

    def body(h_ref, g_ref, o_ref):
        x = h_ref[...]
        r = lax.rsqrt(jnp.mean(x * x, axis=-1, keepdims=True) + EPS)
        o_ref[...] = ((x * r) * g_ref[...]).astype(BF)

    body, in_specs, args = _with_dep(
        body, [pl.BlockSpec((tm, D), lambda i: (i, 0)), pl.BlockSpec((1, D), lambda i: (0, 0))],
        [h, g.reshape(1, D)], dep)
    return pl.pallas_call(
        body,
        out_shape=jax.ShapeDtypeStruct((T, D), BF),
        grid=(T // tm,),
        in_specs=in_specs,
        out_specs=pl.BlockSpec((tm, D), lambda i: (i, 0)),
        compiler_params=_params("parallel"),
        name=name,
    )(*args)


def rms_bwd(dxn, h, g, add, *, name):
    T, D = h.shape
    tm = _tile(T, 264, 16)

    def body(dxn_ref, h_ref, g_ref, add_ref, dh_ref, dhb_ref, dg_ref):
        x = h_ref[...]
        dy = dxn_ref[...]
        r = lax.rsqrt(jnp.mean(x * x, axis=-1, keepdims=True) + EPS)
        xhat = x * r
        part = jnp.sum(dy * xhat, axis=0, keepdims=True)

        @pl.when(pl.program_id(0) == 0)
        def _():
            dg_ref[...] = part

        @pl.when(pl.program_id(0) > 0)
        def _():
            dg_ref[...] += part

        dxh = dy * g_ref[...]
        dh = add_ref[...] + r * (dxh - xhat * jnp.mean(dxh * xhat, axis=-1, keepdims=True))
        dh_ref[...] = dh
        dhb_ref[...] = dh.astype(BF)

    row = pl.BlockSpec((tm, D), lambda i: (i, 0))
    vec = pl.BlockSpec((1, D), lambda i: (0, 0))
    return pl.pallas_call(
        body,
        out_shape=(jax.ShapeDtypeStruct((T, D), F32), jax.ShapeDtypeStruct((T, D), BF),
                   jax.ShapeDtypeStruct((1, D), F32)),
        grid=(T // tm,),
        in_specs=[row, row, vec, row],
        out_specs=(row, row, vec),
        compiler_params=_params("arbitrary"),
        name=name,
    )(dxn, h, g.reshape(1, D), add)


def _head_norm(x, gain):
    r = lax.rsqrt(jnp.mean(x * x, axis=-1, keepdims=True) + EPS)
    return (x * r) * gain


def hn_fwd(qraw, gain, out_scale, *, name):
    T, D = qraw.shape
    H = D // HEAD_DIM
    tm = _tile(T, 528, 16)

    def body(q_ref, g_ref, o_ref):
        gain_v = g_ref[...]
        for hd in range(H):
            sl = slice(hd * HEAD_DIM, (hd + 1) * HEAD_DIM)
            o_ref[:, sl] = (_head_norm(q_ref[:, sl], gain_v) * out_scale).astype(BF)

    return pl.pallas_call(
        body,
        out_shape=jax.ShapeDtypeStruct((T, D), BF),
        grid=(T // tm,),
        in_specs=[pl.BlockSpec((tm, D), lambda i: (i, 0)),
                  pl.BlockSpec((1, HEAD_DIM), lambda i: (0, 0))],
        out_specs=pl.BlockSpec((tm, D), lambda i: (i, 0)),
        compiler_params=_params("parallel"),
        name=name,
    )(qraw, gain.reshape(1, HEAD_DIM))


def kv_post(kv, gain, *, name):
    T, D2 = kv.shape
    D = D2 // 2
    H = D // HEAD_DIM
    tm = _tile(T, 528, 16)

    def body(k_ref, v_ref, g_ref, ko_ref, vo_ref):
        gain_v = g_ref[...]
        for hd in range(H):
            sl = slice(hd * HEAD_DIM, (hd + 1) * HEAD_DIM)
            ko_ref[:, sl] = _head_norm(k_ref[:, sl], gain_v).astype(BF)
        vo_ref[...] = v_ref[...].astype(BF)

    blk = pl.BlockSpec((tm, D), lambda i: (i, 0))
    return pl.pallas_call(
        body,
        out_shape=(jax.ShapeDtypeStruct((T, D), BF), jax.ShapeDtypeStruct((T, D), BF)),
        grid=(T // tm,),
        in_specs=[blk, pl.BlockSpec((tm, D), lambda i: (i, 1)),
                  pl.BlockSpec((1, HEAD_DIM), lambda i: (0, 0))],
        out_specs=(blk, blk),
        compiler_params=_params("parallel"),
        name=name,
    )(kv, kv, gain.reshape(1, HEAD_DIM))


def hn_bwd(dq, qraw, gain, out_scale, *, name):
    T, D = dq.shape
    H = D // HEAD_DIM
    tm = _tile(T, 264, 16)

    def body(dq_ref, q_ref, g_ref, o_ref, dg_ref):
        gain_v = g_ref[...]
        part = jnp.zeros((1, HEAD_DIM), F32)
        for hd in range(H):
            sl = slice(hd * HEAD_DIM, (hd + 1) * HEAD_DIM)
            x = q_ref[:, sl]
            dy = dq_ref[:, sl] * out_scale
            r = lax.rsqrt(jnp.mean(x * x, axis=-1, keepdims=True) + EPS)
            xhat = x * r
            part = part + jnp.sum(dy * xhat, axis=0, keepdims=True)
            dxh = dy * gain_v
            o_ref[:, sl] = (r * (dxh - xhat * jnp.mean(dxh * xhat, axis=-1, keepdims=True))).astype(BF)

        @pl.when(pl.program_id(0) == 0)
        def _():
            dg_ref[...] = part

        @pl.when(pl.program_id(0) > 0)
        def _():
            dg_ref[...] += part

    blk = pl.BlockSpec((tm, D), lambda i: (i, 0))
    vec = pl.BlockSpec((1, HEAD_DIM), lambda i: (0, 0))
    return pl.pallas_call(
        body,
        out_shape=(jax.ShapeDtypeStruct((T, D), BF), jax.ShapeDtypeStruct((1, HEAD_DIM), F32)),
        grid=(T // tm,),
        in_specs=[blk, blk, vec],
        out_specs=(blk, vec),
        compiler_params=_params("arbitrary"),
        name=name,
    )(dq, qraw, gain.reshape(1, HEAD_DIM))


def _shift_down(cur, above, k, rowc):
    out = pltpu.roll(cur, k, 0)
    for i in range(k):
        out = jnp.where(rowc == i, above[8 - k + i:8 - k + i + 1], out)
    return out


def _shift_up(cur, below, k, rowc):
    R = cur.shape[0]
    out = pltpu.roll(cur, R - k, 0)
    for i in range(k):
        out = jnp.where(rowc == R - k + i, below[i:i + 1], out)
    return out


def _conv3(u, u_above, wv, rowc):
    u1 = _shift_down(u, u_above, 1, rowc)
    u2 = _shift_down(u, u_above, 2, rowc)
    return wv[0:1] * u2 + wv[1:2] * u1 + wv[2:3] * u, u1, u2


def conv_fwd(proj, w, *, name):
    T, D3 = proj.shape
    D = D3 // 3
    tc = LANES if D % LANES == 0 else D
    nb = D // tc
    R = _tile(T, 264, 8)

    def body(b_ref, c_ref, h_ref, w_ref, y_ref):
        rowc = lax.broadcasted_iota(jnp.int32, (R, 1), 0)
        wv = w_ref[...]
        for r0 in range(0, T, R):
            rows = slice(r0, r0 + R)
            u = c_ref[rows, :] * h_ref[rows, :]
            if r0 == 0:
                above = jnp.zeros((8, tc), F32)
            else:
                above = c_ref[r0 - 8:r0, :] * h_ref[r0 - 8:r0, :]
            conv, _, _ = _conv3(u, above, wv, rowc)
            y_ref[rows, :] = (b_ref[rows, :] * conv).astype(BF)

    return pl.pallas_call(
        body,
        out_shape=jax.ShapeDtypeStruct((T, D), BF),
        grid=(nb,),
        in_specs=[
            pl.BlockSpec((T, tc), lambda j: (0, j)),
            pl.BlockSpec((T, tc), lambda j: (0, nb + j)),
            pl.BlockSpec((T, tc), lambda j: (0, 2 * nb + j)),
            pl.BlockSpec((3, tc), lambda j: (0, j)),
        ],
        out_specs=pl.BlockSpec((T, tc), lambda j: (0, j)),
        compiler_params=_params("parallel"),
        name=name,
    )(proj, proj, proj, w)


def conv_bwd(dy, proj, w, *, name, dep=None):
    T, D = dy.shape
    tc = LANES if D % LANES == 0 else D
    nb = D // tc
    R = _tile(T, 264, 8)

    def body(dy_ref, b_ref, c_ref, h_ref, w_ref, db_ref, dc_ref, dh_ref, dw_ref):
        rowc = lax.broadcasted_iota(jnp.int32, (R, 1), 0)
        wv = w_ref[...]
        dw = [jnp.zeros((1, tc), F32) for _ in range(3)]
        for r0 in range(0, T, R):
            rows = slice(r0, r0 + R)
            c = c_ref[rows, :]
            hh = h_ref[rows, :]
            u = c * hh
            if r0 == 0:
                above = jnp.zeros((8, tc), F32)
            else:
                above = c_ref[r0 - 8:r0, :] * h_ref[r0 - 8:r0, :]
            conv, u1, u2 = _conv3(u, above, wv, rowc)
            dyv = dy_ref[rows, :]
            db_ref[rows, :] = (dyv * conv).astype(BF)
            dconv = dyv * b_ref[rows, :]
            if r0 + R == T:
                below = jnp.zeros((8, tc), F32)
            else:
                below = dy_ref[r0 + R:r0 + R + 8, :] * b_ref[r0 + R:r0 + R + 8, :]
            dw[0] = dw[0] + jnp.sum(dconv * u2, axis=0, keepdims=True)
            dw[1] = dw[1] + jnp.sum(dconv * u1, axis=0, keepdims=True)
            dw[2] = dw[2] + jnp.sum(dconv * u, axis=0, keepdims=True)
            du = (wv[2:3] * dconv + wv[1:2] * _shift_up(dconv, below, 1, rowc)
                  + wv[0:1] * _shift_up(dconv, below, 2, rowc))
            dc_ref[rows, :] = (du * hh).astype(BF)
            dh_ref[rows, :] = (du * c).astype(BF)
        for i in range(3):
            dw_ref[i:i + 1, :] = dw[i]

    strip = pl.BlockSpec((T, tc), lambda j: (0, j))
    wblk = pl.BlockSpec((3, tc), lambda j: (0, j))
    out = jax.ShapeDtypeStruct((T, D), BF)
    body, in_specs, args = _with_dep(body, [
        strip,
        pl.BlockSpec((T, tc), lambda j: (0, j)),
        pl.BlockSpec((T, tc), lambda j: (0, nb + j)),
        pl.BlockSpec((T, tc), lambda j: (0, 2 * nb + j)),
        wblk,
    ], [dy, proj, proj, proj, w], dep)
    return pl.pallas_call(
        body,
        out_shape=(out, out, out, jax.ShapeDtypeStruct((3, D), F32)),
        grid=(nb,),
        in_specs=in_specs,
        out_specs=(strip, strip, strip, wblk),
        compiler_params=_params("parallel"),
        name=name,
    )(*args)


def _log_sigmoid(z):
    return jnp.minimum(z, 0.0) - jnp.log(1.0 + jnp.exp(-jnp.abs(z)))


def fgate_fwd(logits, bias, pad, *, name):
    T, W = logits.shape
    cb = _tile(T, 128, 8)
    nblk = T // cb

    def body(z_ref, b_ref, c_ref, lf_ref):
        row = lax.broadcasted_iota(jnp.int32, (T, 1), 0)
        lf_ref[...] = jnp.where(row >= pad, _log_sigmoid(z_ref[...] + b_ref[...]), 0.0)
        ri = lax.broadcasted_iota(jnp.int32, (cb, cb), 0)
        ci = lax.broadcasted_iota(jnp.int32, (cb, cb), 1)
        tri = (ci <= ri).astype(F32)

        def step(i, carry):
            rows = pl.ds(pl.multiple_of(i * cb, cb), cb)
            blk = lf_ref[rows, :]
            c_ref[rows, :] = carry + jnp.dot(tri, blk, precision=lax.Precision.HIGHEST,
                                             preferred_element_type=F32)
            return carry + jnp.sum(blk, axis=0, keepdims=True)

        lax.fori_loop(0, nblk, step, jnp.zeros((1, W), F32))

    return pl.pallas_call(
        body,
        out_shape=jax.ShapeDtypeStruct((T, W), F32),
        in_specs=[pl.BlockSpec(memory_space=pltpu.VMEM), pl.BlockSpec(memory_space=pltpu.VMEM)],
        out_specs=pl.BlockSpec(memory_space=pltpu.VMEM),
        scratch_shapes=[pltpu.VMEM((T, W), F32)],
        compiler_params=pltpu.CompilerParams(vmem_limit_bytes=V7X_VMEM_LIMIT),
        name=name,
    )(logits, bias)


def fgate_bwd(dc, logits, bias, pad, *, name):
    T, W = logits.shape
    cb = _tile(T, 128, 8)
    nblk = T // cb

    def body(dc_ref, z_ref, b_ref, dz_ref, db_ref, rs_ref):
        ri = lax.broadcasted_iota(jnp.int32, (cb, cb), 0)
        ci = lax.broadcasted_iota(jnp.int32, (cb, cb), 1)
        triu = (ci >= ri).astype(F32)

        def step(i, carry):
            rows = pl.ds(pl.multiple_of((nblk - 1 - i) * cb, cb), cb)
            blk = dc_ref[rows, :]
            rs_ref[rows, :] = carry + jnp.dot(triu, blk, precision=lax.Precision.HIGHEST,
                                              preferred_element_type=F32)
            return carry + jnp.sum(blk, axis=0, keepdims=True)

        lax.fori_loop(0, nblk, step, jnp.zeros((1, W), F32))
        row = lax.broadcasted_iota(jnp.int32, (T, 1), 0)
        z = z_ref[...] + b_ref[...]
        dz = jnp.where(row >= pad, rs_ref[...] * jax.nn.sigmoid(-z), 0.0)
        dz_ref[...] = dz.astype(BF)
        db_ref[...] = jnp.sum(dz, axis=0, keepdims=True)

    vm = pl.BlockSpec(memory_space=pltpu.VMEM)
    return pl.pallas_call(
        body,
        out_shape=(jax.ShapeDtypeStruct((T, W), BF), jax.ShapeDtypeStruct((1, W), F32)),
        in_specs=[vm, vm, vm],
        out_specs=(vm, vm),
        scratch_shapes=[pltpu.VMEM((T, W), F32)],
        compiler_params=pltpu.CompilerParams(vmem_limit_bytes=V7X_VMEM_LIMIT),
        name=name,
    )(dc, logits, bias)


def _scores(qb, kb, ck):
    return lax.dot_general(qb, kb, (((1,), (1,)), ((), ())), preferred_element_type=F32) - ck


def _causal(s, row, col, pad):
    return jnp.where((col <= row) & (col >= pad), s, NEG)


def attn_fwd(q, k, v, crow, pad, *, name):
    T, D = q.shape
    H = D // HEAD_DIM
    nk, tk = crow.shape[1], crow.shape[3]
    tq = tk
    nq = T // tq

    hp = 2 if H % 2 == 0 else 1
    wide = hp * HEAD_DIM

    def body(q_ref, k_ref, v_ref, cr_ref, o_ref, lse_ref):
        qi = pl.program_id(1)
        row = qi * tq + lax.broadcasted_iota(jnp.int32, (tq, 1), 0)
        heads = [slice(a * HEAD_DIM, (a + 1) * HEAD_DIM) for a in range(hp)]
        qbs = [q_ref[:, sl] for sl in heads]

        def step(kc, carry, masked):
            rows = pl.ds(pl.multiple_of(kc * tk, tk), tk)
            out = []
            for a, sl in enumerate(heads):
                m, l, acc = carry[a]
                s = _scores(qbs[a], k_ref[rows, sl], cr_ref[a, kc])
                if masked:
                    s = _causal(s, row, kc * tk + lax.broadcasted_iota(jnp.int32, (1, tk), 1), pad)
                m_new = jnp.maximum(m, jnp.max(s, axis=-1, keepdims=True))
                alpha = jnp.exp(m - m_new)
                p = jnp.exp(s - m_new)
                l = alpha * l + jnp.sum(p, axis=-1, keepdims=True)
                acc = alpha * acc + jnp.dot(p.astype(BF), v_ref[rows, sl], preferred_element_type=F32)
                out.append((m_new, l, acc))
            return tuple(out)

        init = tuple((jnp.full((tq, 1), NEG, F32), jnp.zeros((tq, 1), F32), jnp.zeros((tq, HEAD_DIM), F32))
                     for _ in heads)
        carry = step(0, init, True)
        carry = lax.fori_loop(1, qi, lambda kc, c: step(kc, c, False), carry)
        carry = lax.cond(qi > 0, lambda c: step(qi, c, True), lambda c: c, carry)
        valid = row >= pad
        for a, sl in enumerate(heads):
            m, l, acc = carry[a]
            o_ref[:, sl] = jnp.where(valid, acc / l, 0.0).astype(BF)
            lse_ref[a] = jnp.where(valid, m + jnp.log(l), 0.0)

    return pl.pallas_call(
        body,
        out_shape=(jax.ShapeDtypeStruct((T, D), BF), jax.ShapeDtypeStruct((H, T, 1), F32)),
        grid=(H // hp, nq),
        in_specs=[
            pl.BlockSpec((tq, wide), lambda h, i: (i, h)),
            pl.BlockSpec((T, wide), lambda h, i: (0, h)),
            pl.BlockSpec((T, wide), lambda h, i: (0, h)),
            pl.BlockSpec((hp, nk, 1, tk), lambda h, i: (h, 0, 0, 0)),
        ],
        out_specs=(pl.BlockSpec((tq, wide), lambda h, i: (i, h)),
                   pl.BlockSpec((hp, tq, 1), lambda h, i: (h, i, 0))),
        compiler_params=_params("parallel", "arbitrary"),
        name=name,
    )(q, k, v, crow)


def attn_bwd(q, k, v, do, o, lse, crow, prev, pad, *, name, dep=None):
    T, D = q.shape
    H = D // HEAD_DIM
    nk, tk = crow.shape[1], crow.shape[3]
    tq = tk
    nq = T // tq
    has_prev = prev is not None

    def body(*refs):
        q_ref, k_ref, v_ref, do_ref, o_ref, lse_ref, cr_ref = refs[:7]
        refs = refs[7:]
        if has_prev:
            pk_ref, pv_ref, pc_ref, pq_ref = refs[:4]
            refs = refs[4:]
        dq_ref, dk_ref, dv_ref, dck_ref, dcq_ref, delta_ref = refs
        kc = pl.program_id(1)

        @pl.when(kc == 0)
        def _():
            dq_ref[...] = jnp.zeros_like(dq_ref)
            dcq_ref[...] = pq_ref[...] if has_prev else jnp.zeros_like(dcq_ref)
            do_used = do_ref[...].astype(BF).astype(F32)
            delta_ref[...] = jnp.sum(do_used * o_ref[...].astype(F32), axis=-1, keepdims=True)

        kb = k_ref[...]
        vb = v_ref[...]
        ck = cr_ref[...]
        col = kc * tk + lax.broadcasted_iota(jnp.int32, (1, tk), 1)

        def step(qi, carry, masked):
            dk, dv, dck = carry
            rows = pl.ds(pl.multiple_of(qi * tq, tq), tq)
            qb = q_ref[rows, :]
            dob = do_ref[rows, :].astype(BF)
            s = _scores(qb, kb, ck)
            if masked:
                s = _causal(s, qi * tq + lax.broadcasted_iota(jnp.int32, (tq, 1), 0), col, pad)
            p = jnp.exp(s - lse_ref[rows, :])
            dp = lax.dot_general(dob, vb, (((1,), (1,)), ((), ())), preferred_element_type=F32)
            ds = p * (dp - delta_ref[rows, :])
            dsb = ds.astype(BF)
            dv = dv + lax.dot_general(p.astype(BF), dob, (((0,), (0,)), ((), ())),
                                      preferred_element_type=F32)
            dk = dk + lax.dot_general(dsb, qb, (((0,), (0,)), ((), ())), preferred_element_type=F32)
            dq_ref[rows, :] += jnp.dot(dsb, kb, preferred_element_type=F32)
            dcq_ref[rows, :] += jnp.sum(ds, axis=1, keepdims=True)
            dck = dck - jnp.sum(ds, axis=0, keepdims=True)
            return dk, dv, dck

        def rest(masked):
            return lambda c: lax.fori_loop(kc + 1, nq, lambda qi, cc: step(qi, cc, masked), c)

        init = (jnp.zeros((tk, HEAD_DIM), F32), jnp.zeros((tk, HEAD_DIM), F32), jnp.zeros((1, tk), F32))
        carry = step(kc, init, True)
        dk, dv, dck = lax.cond(kc == 0, rest(True), rest(False), carry)
        if has_prev:
            dk = dk + pk_ref[...]
            dv = dv + pv_ref[...]
            dck = dck + pc_ref[...]
        dk_ref[...] = dk
        dv_ref[...] = dv
        dck_ref[...] = dck

    head_all = pl.BlockSpec((T, HEAD_DIM), lambda h, j: (0, h))
    head_blk = pl.BlockSpec((tk, HEAD_DIM), lambda h, j: (j, h))
    col_all = pl.BlockSpec((None, T, 1), lambda h, j: (h, 0, 0))
    row_blk = pl.BlockSpec((None, None, 1, tk), lambda h, j: (h, j, 0, 0))
    in_specs = [head_all, head_blk, head_blk, head_all, head_all, col_all, row_blk]
    args = [q, k, v, do, o, lse, crow]
    if has_prev:
        in_specs += [head_blk, head_blk, row_blk, col_all]
        args += list(prev)
    body, in_specs, args = _with_dep(body, in_specs, args, dep)
    return pl.pallas_call(
        body,
        out_shape=(jax.ShapeDtypeStruct((T, D), F32), jax.ShapeDtypeStruct((T, D), F32),
                   jax.ShapeDtypeStruct((T, D), F32), jax.ShapeDtypeStruct((H, nk, 1, tk), F32),
                   jax.ShapeDtypeStruct((H, T, 1), F32)),
        grid=(H, nk),
        in_specs=in_specs,
        out_specs=(head_all, head_blk, head_blk, row_blk, col_all),
        scratch_shapes=[pltpu.VMEM((T, 1), F32)],
        compiler_params=_params("parallel", "arbitrary"),
        name=name,
    )(*args)


def loss_head(h, target, lead, *, name):
    T, D = h.shape
    tm = lead
    assert T % tm == 0 and target.shape[0] % tm == 0
    inv_d = 1.0 / D

    def body(h_ref, t_ref, dh_ref, dhb_ref, loss_ref):
        i = pl.program_id(0)

        @pl.when(i == 0)
        def _():
            dh_ref[...] = jnp.zeros_like(dh_ref)
            dhb_ref[...] = jnp.zeros_like(dhb_ref)
            loss_ref[...] = jnp.zeros_like(loss_ref)

        @pl.when(i > 0)
        def _():
            e = h_ref[...] - t_ref[...]
            dh = e * inv_d
            dh_ref[...] = dh
            dhb_ref[...] = dh.astype(BF)
            loss_ref[...] += 0.5 * inv_d * jnp.sum(e * e)

    return pl.pallas_call(
        body,
        out_shape=(jax.ShapeDtypeStruct((T, D), F32), jax.ShapeDtypeStruct((T, D), BF),
                   jax.ShapeDtypeStruct((8, LANES), F32)),
        grid=(T // tm,),
        in_specs=[pl.BlockSpec((tm, D), lambda i: (i, 0)),
                  pl.BlockSpec((tm, D), lambda i: (jnp.maximum(i - 1, 0), 0))],
        out_specs=(pl.BlockSpec((tm, D), lambda i: (i, 0)), pl.BlockSpec((tm, D), lambda i: (i, 0)),
                   pl.BlockSpec((8, LANES), lambda i: (0, 0))),
        compiler_params=_params("arbitrary"),
        name=name,
    )(h, target)


def adamw(parts, w, m, v, *, name):
    P, R, C = parts.shape
    tr = _tile(R, max(16, (128 * 1024) // C), 16)

    def body(p_ref, w_ref, m_ref, v_ref, g_ref, d_ref, mo_ref, vo_ref):
        g = p_ref[0].astype(F32)
        for i in range(1, P):
            g = g + p_ref[i].astype(F32)
        m_new = ADAM_B1 * m_ref[...] + (1.0 - ADAM_B1) * g
        v_new = ADAM_B2 * v_ref[...] + (1.0 - ADAM_B2) * jnp.square(g)
        m_hat = m_new / (1.0 - ADAM_B1 ** ADAM_STEP)
        v_hat = v_new / (1.0 - ADAM_B2 ** ADAM_STEP)
        g_ref[...] = g
        d_ref[...] = -ADAM_LR * (m_hat / (jnp.sqrt(v_hat) + ADAM_EPS) + ADAM_WD * w_ref[...])
        mo_ref[...] = m_new
        vo_ref[...] = v_new

    blk = pl.BlockSpec((tr, C), lambda i: (i, 0))
    out = jax.ShapeDtypeStruct((R, C), F32)
    return pl.pallas_call(
        body,
        out_shape=(out, out, out, out),
        grid=(R // tr,),
        in_specs=[pl.BlockSpec((P, tr, C), lambda i: (0, i, 0)), blk, blk, blk],
        out_specs=(blk, blk, blk, blk),
        compiler_params=_params("parallel"),
        name=name,
    )(parts, w, m, v)


def _flip(v, bit):
    return 1 - v if bit else v


def all_gather(shard, *, name, dep=None):
    def body(x_ref, out_ref, send_sems, recv_sems, local_sem):
        x, y, c = lax.axis_index("x"), lax.axis_index("y"), lax.axis_index("c")
        me, sibling = (x, y, c), (x, y, 1 - c)
        chips = [(1 - x, y), (x, 1 - y), (1 - x, 1 - y)]

        def block(px, py, pc):
            return out_ref.at[4 * px + 2 * py + pc]

        def copy(k, blk, to, src=None):
            return pltpu.make_async_remote_copy(
                src_ref=block(*blk) if src is None else src,
                dst_ref=block(*blk),
                send_sem=send_sems.at[k],
                recv_sem=recv_sems.at[k],
                device_id=to,
                device_id_type=pl.DeviceIdType.MESH,
            )

        mine = pltpu.make_async_copy(x_ref, block(*me), local_sem)
        mine.start()
        first = [copy(0, me, sibling, src=x_ref)]
        first += [copy(1 + j, me, (*chip, c), src=x_ref) for j, chip in enumerate(chips)]
        for cp in first:
            cp.start()
        passed = [copy(4 + j, (*chip, c), sibling) for j, chip in enumerate(chips)]
        for j, chip in enumerate(chips):
            copy(1 + j, (*chip, c), me).wait_recv()
            passed[j].start()
        copy(0, sibling, me).wait_recv()
        for j, chip in enumerate(chips):
            copy(4 + j, (*chip, 1 - c), me).wait_recv()
        for cp in first + passed:
            cp.wait_send()
        mine.wait()

    body, in_specs, args = _with_dep(body, [pl.BlockSpec(memory_space=pl.ANY)], [shard], dep)
    return pl.pallas_call(
        body,
        out_shape=jax.ShapeDtypeStruct((N_DEV,) + shard.shape, shard.dtype),
        in_specs=in_specs,
        out_specs=pl.BlockSpec(memory_space=pl.ANY),
        scratch_shapes=[pltpu.SemaphoreType.DMA((7,)), pltpu.SemaphoreType.DMA((7,)),
                        pltpu.SemaphoreType.DMA],
        name=name,
    )(*args)


def exchange_slabs(slabs, *, name):
    def body(g_ref, r_ref, send_sems, recv_sems, local_sem):
        x, y, c = lax.axis_index("x"), lax.axis_index("y"), lax.axis_index("c")
        me = 4 * x + 2 * y + c
        mine = pltpu.make_async_copy(g_ref.at[me], r_ref.at[me], local_sem)
        mine.start()
        sends, recvs = [], []
        for k in range(1, N_DEV):
            px, py, pc = _flip(x, (k >> 2) & 1), _flip(y, (k >> 1) & 1), _flip(c, k & 1)
            peer = 4 * px + 2 * py + pc
            sends.append(pltpu.make_async_remote_copy(
                src_ref=g_ref.at[peer], dst_ref=r_ref.at[me],
                send_sem=send_sems.at[k - 1], recv_sem=recv_sems.at[k - 1],
                device_id=(px, py, pc), device_id_type=pl.DeviceIdType.MESH))
            recvs.append(pltpu.make_async_remote_copy(
                src_ref=g_ref.at[peer], dst_ref=r_ref.at[peer],
                send_sem=send_sems.at[k - 1], recv_sem=recv_sems.at[k - 1],
                device_id=(px, py, pc), device_id_type=pl.DeviceIdType.MESH))
        for cp in sends:
            cp.start()
        for cp in recvs:
            cp.wait_recv()
        for cp in sends:
            cp.wait_send()
        mine.wait()

    return pl.pallas_call(
        body,
        out_shape=jax.ShapeDtypeStruct(slabs.shape, slabs.dtype),
        in_specs=[pl.BlockSpec(memory_space=pl.ANY)],
        out_specs=pl.BlockSpec(memory_space=pl.ANY),
        scratch_shapes=[pltpu.SemaphoreType.DMA((7,)), pltpu.SemaphoreType.DMA((7,)),
                        pltpu.SemaphoreType.DMA],
        name=name,
    )(slabs)


def reduce_adamw(slabs, w, m, v, *, name):
    got = exchange_slabs(slabs, name=name + "_xchg")
    return adamw(got, w, m, v, name=name + "_adamw")


_HBM = pl.BlockSpec(memory_space=pltpu.HBM)
_SEM = pl.BlockSpec(memory_space=pltpu.SEMAPHORE)
_ANY = pl.BlockSpec(memory_space=pl.ANY)
_EFFECT = pltpu.SideEffectType.DATAFLOW_SIDE_EFFECTING
_N_FIRST = 4


def _first_copies(land_ref, send_sems, recv_sems):
    x, y, c = lax.axis_index("x"), lax.axis_index("y"), lax.axis_index("c")
    mine = land_ref.at[4 * x + 2 * y + c]
    targets = [(x, y, 1 - c), (1 - x, y, c), (x, 1 - y, c), (1 - x, 1 - y, c)]
    sends, recvs = [], []
    for k, (px, py, pc) in enumerate(targets):
        common = dict(send_sem=send_sems.at[k], recv_sem=recv_sems.at[k], device_id=(px, py, pc),
                      device_id_type=pl.DeviceIdType.MESH)
        sends.append(pltpu.make_async_remote_copy(src_ref=mine, dst_ref=mine, **common))
        theirs = land_ref.at[4 * px + 2 * py + pc]
        recvs.append(pltpu.make_async_remote_copy(src_ref=theirs, dst_ref=theirs, **common))
    return sends, recvs


def _second_copies(land_ref, send_sems, recv_sems):
    x, y, c = lax.axis_index("x"), lax.axis_index("y"), lax.axis_index("c")
    sends, recvs = [], []
    for j, (px, py) in enumerate([(1 - x, y), (x, 1 - y), (1 - x, 1 - y)]):
        common = dict(send_sem=send_sems.at[j], recv_sem=recv_sems.at[j], device_id=(x, y, 1 - c),
                      device_id_type=pl.DeviceIdType.MESH)
        blk = land_ref.at[4 * px + 2 * py + c]
        sends.append(pltpu.make_async_remote_copy(src_ref=blk, dst_ref=blk, **common))
        got = land_ref.at[4 * px + 2 * py + (1 - c)]
        recvs.append(pltpu.make_async_remote_copy(src_ref=got, dst_ref=got, **common))
    return sends, recvs


def gather_start(shard, me, after, *, name):
    R, C = shard.shape
    tr = _tile(R, max(16, (512 * 1024) // C), 16)

    def place_body(me_ref, x_ref, o_ref):
        o_ref[...] = x_ref[...].astype(BF)

    land = pl.pallas_call(
        place_body, name=name + "_own",
        out_shape=jax.ShapeDtypeStruct((N_DEV, R, C), BF),
        grid_spec=pltpu.PrefetchScalarGridSpec(
            num_scalar_prefetch=1,
            grid=(R // tr,),
            in_specs=[pl.BlockSpec((tr, C), lambda i, me_ref: (i, 0))],
            out_specs=pl.BlockSpec((None, tr, C), lambda i, me_ref: (me_ref[0], i, 0)),
        ),
        compiler_params=_params("parallel"),
    )(me.reshape(1).astype(jnp.int32), shard)

    def body(land_ref, after_ref, send_sems, recv_sems, land_thru, token):
        sends, _ = _first_copies(land_ref, send_sems, recv_sems)
        for cp in sends:
            cp.start()
        token[...] = jnp.zeros_like(token)

    send_sems, recv_sems, land_thru, token = pl.pallas_call(
        body, name=name + "_s1",
        out_shape=(pltpu.SemaphoreType.DMA((_N_FIRST,)), pltpu.SemaphoreType.DMA((_N_FIRST,)),
                   pltpu.HBM(land.shape, land.dtype), jax.ShapeDtypeStruct((8, LANES), F32)),
        in_specs=(_HBM, _ANY),
        out_specs=(_SEM, _SEM, _HBM, pl.BlockSpec(memory_space=pltpu.VMEM)),
        input_output_aliases={0: 2},
        compiler_params=pltpu.CompilerParams(has_side_effects=_EFFECT),
    )(pltpu.with_memory_space_constraint(land, pltpu.HBM), after)
    return (send_sems, recv_sems, land_thru), token


def gather_mid(handle, after, *, name):
    send_sems, recv_sems, land_thru = handle

    def body(land_ref, send1, recv1, after_ref, send2, recv2, land_out, token):
        sends, recvs = _first_copies(land_ref, send1, recv1)
        for cp in sends:
            cp.wait_send()
        for cp in recvs:
            cp.wait_recv()
        seconds, _ = _second_copies(land_ref, send2, recv2)
        for cp in seconds:
            cp.start()
        token[...] = jnp.zeros_like(token)

    send2, recv2, land2, token = pl.pallas_call(
        body, name=name + "_s2",
        out_shape=(pltpu.SemaphoreType.DMA((3,)), pltpu.SemaphoreType.DMA((3,)),
                   pltpu.HBM(land_thru.shape, land_thru.dtype), jax.ShapeDtypeStruct((8, LANES), F32)),
        in_specs=(_HBM, _SEM, _SEM, _ANY),
        out_specs=(_SEM, _SEM, _HBM, pl.BlockSpec(memory_space=pltpu.VMEM)),
        input_output_aliases={0: 2},
        compiler_params=pltpu.CompilerParams(has_side_effects=_EFFECT),
    )(land_thru, send_sems, recv_sems, after)
    return (send2, recv2, land2), token


def gather_finish(handle, after, *, name):
    send2, recv2, land2 = handle

    def body(land_ref, send2, recv2, after_ref, got_ref):
        sends, recvs = _second_copies(land_ref, send2, recv2)
        for cp in sends:
            cp.wait_send()
        for cp in recvs:
            cp.wait_recv()

    return pl.pallas_call(
        body, name=name + "_w",
        out_shape=pltpu.HBM(land2.shape, land2.dtype),
        in_specs=(_HBM, _SEM, _SEM, _ANY),
        out_specs=_HBM,
        input_output_aliases={0: 0},
        compiler_params=pltpu.CompilerParams(has_side_effects=_EFFECT),
    )(land2, send2, recv2, after)


def _slab_copies(g_ref, r_ref, send_sems, recv_sems):
    x, y, c = lax.axis_index("x"), lax.axis_index("y"), lax.axis_index("c")
    me = 4 * x + 2 * y + c
    sends, recvs = [], []
    for k in range(1, N_DEV):
        px, py, pc = _flip(x, (k >> 2) & 1), _flip(y, (k >> 1) & 1), _flip(c, k & 1)
        peer = 4 * px + 2 * py + pc
        common = dict(send_sem=send_sems.at[k - 1], recv_sem=recv_sems.at[k - 1], device_id=(px, py, pc),
                      device_id_type=pl.DeviceIdType.MESH)
        sends.append(pltpu.make_async_remote_copy(src_ref=g_ref.at[peer], dst_ref=r_ref.at[me], **common))
        recvs.append(pltpu.make_async_remote_copy(src_ref=g_ref.at[peer], dst_ref=r_ref.at[peer], **common))
    return sends, recvs


def exchange_start(slabs, *, name):
    land = lax.empty(slabs.shape, slabs.dtype)

    def body(g_ref, r_ref, send_sems, recv_sems, g_thru, r_thru, token):
        sends, _ = _slab_copies(g_ref, r_ref, send_sems, recv_sems)
        for cp in sends:
            cp.start()
        token[...] = jnp.zeros_like(token)

    send_sems, recv_sems, g_thru, r_thru, token = pl.pallas_call(
        body, name=name,
        out_shape=(pltpu.SemaphoreType.DMA((N_DEV - 1,)), pltpu.SemaphoreType.DMA((N_DEV - 1,)),
                   pltpu.HBM(slabs.shape, slabs.dtype), pltpu.HBM(slabs.shape, slabs.dtype),
                   jax.ShapeDtypeStruct((8, LANES), F32)),
        in_specs=(_HBM, _HBM),
        out_specs=(_SEM, _SEM, _HBM, _HBM, pl.BlockSpec(memory_space=pltpu.VMEM)),
        input_output_aliases={0: 2, 1: 3},
        compiler_params=pltpu.CompilerParams(has_side_effects=_EFFECT),
    )(pltpu.with_memory_space_constraint(slabs, pltpu.HBM), pltpu.with_memory_space_constraint(land, pltpu.HBM))
    return (send_sems, recv_sems, g_thru, r_thru), token


def exchange_finish(handle, after, *, name):
    send_sems, recv_sems, g_thru, r_thru = handle

    def body(g_ref, r_ref, send_sems, recv_sems, after_ref, g_out, r_out):
        sends, recvs = _slab_copies(g_ref, r_ref, send_sems, recv_sems)
        for cp in sends:
            cp.wait_send()
        for cp in recvs:
            cp.wait_recv()

    return pl.pallas_call(
        body, name=name,
        out_shape=(pltpu.HBM(g_thru.shape, g_thru.dtype), pltpu.HBM(r_thru.shape, r_thru.dtype)),
        in_specs=(_HBM, _HBM, _SEM, _SEM, _ANY),
        out_specs=(_HBM, _HBM),
        input_output_aliases={0: 0, 1: 1},
        compiler_params=pltpu.CompilerParams(has_side_effects=_EFFECT),
    )(g_thru, r_thru, send_sems, recv_sems, after)


def adamw_own(own, got, me, w, m, v, layer, prev, *, name):
    P, R, C = got.shape
    L = w.shape[0]
    tr = _tile(R, max(16, (256 * 1024) // C), 16)

    def body(me_ref, own_ref, p_ref, w_ref, m_ref, v_ref, *rest):
        g_ref, d_ref, mo_ref, vo_ref = rest[-4:]
        mine = own_ref[...].astype(F32)
        g = None
        for i in range(P):
            term = jnp.where(me_ref[0] == i, mine, p_ref[i].astype(F32))
            g = term if g is None else g + term
        m_new = ADAM_B1 * m_ref[...] + (1.0 - ADAM_B1) * g
        v_new = ADAM_B2 * v_ref[...] + (1.0 - ADAM_B2) * jnp.square(g)
        m_hat = m_new / (1.0 - ADAM_B1 ** ADAM_STEP)
        v_hat = v_new / (1.0 - ADAM_B2 ** ADAM_STEP)
        g_ref[...] = g
        d_ref[...] = -ADAM_LR * (m_hat / (jnp.sqrt(v_hat) + ADAM_EPS) + ADAM_WD * w_ref[...])
        mo_ref[...] = m_new
        vo_ref[...] = v_new

    blk = pl.BlockSpec((None, tr, C), lambda i, me_ref: (layer, i, 0))
    out = jax.ShapeDtypeStruct((L, R, C), F32)
    in_specs = [pl.BlockSpec((None, tr, C), lambda i, me_ref: (me_ref[0], i, 0)),
                pl.BlockSpec((P, tr, C), lambda i, me_ref: (0, i, 0)), blk, blk, blk]
    args = [me.reshape(1).astype(jnp.int32), own, got, w, m, v]
    aliases = {}
    if prev is not None:
        in_specs += [pl.BlockSpec(memory_space=pl.ANY)] * 4
        aliases = {len(args) + i: i for i in range(4)}
        args += list(prev)
    return pl.pallas_call(
        body,
        out_shape=(out, out, out, out),
        grid_spec=pltpu.PrefetchScalarGridSpec(
            num_scalar_prefetch=1,
            grid=(R // tr,),
            in_specs=in_specs,
            out_specs=(blk, blk, blk, blk),
        ),
        input_output_aliases=aliases,
        compiler_params=_params("parallel"),
        name=name,
    )(*args)


def _pad_rows(a, rows):
    return jnp.pad(a, ((0, rows - a.shape[0]), (0, 0)))


def _pad_cols(a, cols):
    return jnp.pad(a, ((0, 0), (0, cols - a.shape[1])))


def kernel(x, meta, a_norm, a_w_in, a_conv, a_w_out, kv_norm, w_kv, k_norm, w_f, b_f, b_norm, b_w_q, b_q_norm, b_w_o, ffn_norm, ffn_w_gu, ffn_w_down, loss_target, m_meta, m_a_norm, m_a_w_in, m_a_conv, m_a_w_out, m_kv_norm, m_w_kv, m_k_norm, m_w_f, m_b_f, m_b_norm, m_b_w_q, m_b_q_norm, m_b_w_o, m_ffn_norm, m_ffn_w_gu, m_ffn_w_down, v_meta, v_a_norm, v_a_w_in, v_a_conv, v_a_w_out, v_kv_norm, v_w_kv, v_k_norm, v_w_f, v_b_f, v_b_norm, v_b_w_q, v_b_q_norm, v_b_w_o, v_ffn_norm, v_ffn_w_gu, v_ffn_w_down):
    S, D = x.shape[1], x.shape[2]
    n_meta = meta.shape[0]
    Ds = meta.shape[1]
    H = D // HEAD_DIM
    n_a, n_b = a_w_in.shape[0], b_w_q.shape[0]
    depth = n_a + n_b
    Fs = ffn_w_down.shape[1]
    pad = BLOCK - n_meta
    lead = pad + n_meta
    T = lead + S
    tk_attn = _tile(T, 384, LANES)
    nk_attn = T // tk_attn
    q_scale = 1.0 / math.sqrt(HEAD_DIM)
    my = 4 * lax.axis_index("x") + 2 * lax.axis_index("y") + lax.axis_index("c")

    wf_t = w_f.reshape(H, Ds)
    small = jnp.concatenate([meta, _pad_rows(a_norm, 8), _pad_rows(a_conv.reshape(n_a * 3, Ds), 8), wf_t], axis=0)
    r_an, r_ac, r_wf = n_meta, n_meta + 8, n_meta + 16
    gs = all_gather(small, name="ag_small")
    unshard = lambda blk: jnp.transpose(blk, (1, 0, 2)).reshape(blk.shape[1], D)
    meta_full = unshard(gs[:, 0:n_meta])
    a_norm_full = unshard(gs[:, r_an:r_an + n_a])
    a_conv_full = unshard(gs[:, r_ac:r_ac + 3 * n_a]).reshape(n_a, 3, D)
    w_f_full = gs[:, r_wf:r_wf + H].reshape(D, H)
    wf_pad = _pad_cols(w_f_full, LANES).astype(BF)[None]
    bf_pad = _pad_cols(b_f.reshape(1, H), LANES)

    def layer_shards(l):
        if l < n_a:
            mix = [(("in", l), a_w_in[l]), (("out", l), a_w_out[l])]
        else:
            j = l - n_a
            mix = ([(("kv", 0), w_kv)] if j == 0 else []) + [(("q", j), b_w_q[j]), (("o", j), b_w_o[j])]
        return mix + [(("gu", l), ffn_w_gu[l]), (("dn", l), ffn_w_down[l])]

    first_level, second_level, W = {}, {}, {}
    st = {"done": None, "tok": None}

    def note(val):
        st["done"] = val
        return val

    def take():
        tok, st["tok"] = st["tok"], None
        return tok

    def chain_after(default):
        if st["tok"] is not None:
            return st["tok"]
        return default if st["done"] is None else st["done"]

    def ag_name(key):
        return f"ag_{key[0]}{key[1]}"

    def start_layer(l):
        for key, shard in layer_shards(l):
            first_level[key], st["tok"] = gather_start(shard, my, chain_after(shard), name=ag_name(key))

    def pass_on(keys):
        for key in keys:
            second_level[key], st["tok"] = gather_mid(first_level.pop(key), chain_after(None), name=ag_name(key))

    def weight(key, shape=None):
        w = gather_finish(second_level.pop(key), st["done"], name=ag_name(key))
        W[key] = w if shape is None else w.reshape(shape)
        return W[key]

    def layer_keys(l):
        keys = [key for key, _ in layer_shards(l)]
        return keys[:-2], keys[-2:]

    h = note(jnp.concatenate([jnp.zeros((pad, D), F32), meta_full, x[0]], axis=0))
    start_layer(0)
    pass_on(layer_keys(0)[0])
    saved = []
    shared = None
    for l in range(depth):
        rec = {"h": h}
        mix_keys, ffn_keys = layer_keys(l)

        def ahead():
            if l >= 1:
                pass_on(ffn_keys[:1])
            if l + 1 < depth:
                start_layer(l + 1)

        if l < n_a:
            xn = note(rms_fwd(h, a_norm_full[l], name=f"a{l}_norm", dep=take()))
            ahead()
            proj = note(mm_nn(xn, weight(("in", l)), name=f"a{l}_in", dep=take()))
            if l == 0:
                pass_on(ffn_keys[:1])
            y = note(conv_fwd(proj, a_conv_full[l], name=f"a{l}_conv"))
            h1 = note(mm_nn(y, weight(("out", l), (1, D, D)), add=h, name=f"a{l}_out", dep=take()))
            rec.update(xn=xn, proj=proj, y=y)
        else:
            j = l - n_a
            if j == 0:
                xnk = note(rms_fwd(h, kv_norm, name="kv_norm", dep=take()))
                ahead()
                kv = note(mm_nn(xnk, weight(("kv", 0)), name="kv_proj", dep=take()))
                k, v = kv_post(kv, k_norm, name="kv_post")
                logits = mm_nn(xnk, wf_pad, name="f_logits", tn_target=LANES)
                cfull = fgate_fwd(logits, bf_pad, pad, name="f_gate")
                crow = jnp.transpose(cfull[:, :H]).reshape(H, nk_attn, 1, tk_attn)
                shared = dict(h=h, xnk=xnk, kv=kv, logits=logits)
                xn = note(rms_fwd(h, b_norm[j], name=f"b{j}_norm"))
            else:
                xn = note(rms_fwd(h, b_norm[j], name=f"b{j}_norm", dep=take()))
                ahead()
            qraw = note(mm_nn(xn, weight(("q", j), (1, D, D)), name=f"b{j}_q", dep=take()))
            q = hn_fwd(qraw, b_q_norm[j], q_scale, name=f"b{j}_qnorm")
            o, lse = attn_fwd(q, k, v, crow, pad, name=f"b{j}_attn")
            note(o)
            h1 = note(mm_nn(o, weight(("o", j), (1, D, D)), add=h, name=f"b{j}_o"))
            rec.update(xn=xn, qraw=qraw, q=q, o=o, lse=lse)
        xn2 = note(rms_fwd(h1, ffn_norm[l], name=f"f{l}_norm", dep=take()))
        pass_on(ffn_keys[1:])
        act, g_s, u_s = mm_swiglu(xn2, weight(("gu", l)), name=f"f{l}_gu", dep=take())
        note(act)
        if l + 1 < depth:
            pass_on(layer_keys(l + 1)[0])
        h = note(mm_nn(act, weight(("dn", l), (1, N_DEV * Fs, D)), add=h1, name=f"f{l}_down", resident=True,
                       tm_target=528, tn_target=512, dep=take()))
        rec.update(h1=h1, xn2=xn2, act=act, g=g_s, u=u_s)
        saved.append(rec)

    dh, dhb, loss_tile = loss_head(h, loss_target[0], lead, name="loss")
    loss = lax.psum(loss_tile[0, 0], MESH_AXES)

    upd = {}
    small_g = {}
    inflight = []

    def big(name, l, section, slabs, w, m, v):
        handle, st["tok"] = exchange_start(slabs.reshape(N_DEV, -1, w.shape[-1]), name=f"{name}{l}_xs")
        inflight.append((section, name, l, handle, w, m, v))

    def land(sections, after):
        for entry in [e for e in inflight if sections is None or e[0] in sections]:
            inflight.remove(entry)
            _, name, l, handle, w, m, v = entry
            own, got = exchange_finish(handle, after, name=f"{name}{l}_xw")
            flat = lambda t: t.reshape(w.shape[0], -1, w.shape[-1])
            upd[name] = adamw_own(own, got, my, flat(w), flat(m), flat(v), l, upd.get(name),
                                  name=f"{name}{l}_adamw")

    dk = dv = dck = dcq = None
    for l in reversed(range(depth)):
        rec = saved[l]
        land([("ffn", l + 1)], dh)
        dgu = mm_nt_dswiglu(dhb, W[("dn", l)], rec["g"], rec["u"], name=f"f{l}_ddown")
        big("ffn_w_down", l, ("ffn", l), mm_tn(rec["act"], dhb, 1, name=f"f{l}_wdown"),
            ffn_w_down, m_ffn_w_down, v_ffn_w_down)
        big("ffn_w_gu", l, ("ffn", l), mm_tn(rec["xn2"], dgu, N_DEV, name=f"f{l}_wgu", dep=take()),
            ffn_w_gu, m_ffn_w_gu, v_ffn_w_gu)
        dxn2 = mm_nt(dgu, W[("gu", l)], name=f"f{l}_dgu", gb=2, dep=take())
        dh1, dhb, dgf = rms_bwd(dxn2, rec["h1"], ffn_norm[l], dh, name=f"f{l}_dnorm")
        small_g[("ffn_norm", l)] = dgf
        land([("mix", l + 1)], dh1)
        if l < n_a:
            dy = mm_nt(dhb, W[("out", l)], name=f"a{l}_dout")
            big("a_w_out", l, ("mix", l), mm_tn(rec["y"], dhb, 1, name=f"a{l}_wout"),
                a_w_out, m_a_w_out, v_a_w_out)
            db, dc, dhh, dcw = conv_bwd(dy, rec["proj"], a_conv_full[l], name=f"a{l}_dconv", dep=take())
            small_g[("a_conv", l)] = dcw
            dproj = jnp.concatenate([db, dc, dhh], axis=1)
            big("a_w_in", l, ("mix", l), mm_tn(rec["xn"], dproj, N_DEV, name=f"a{l}_win"),
                a_w_in, m_a_w_in, v_a_w_in)
            dxn = mm_nt(dproj, W[("in", l)], name=f"a{l}_din", gb=4, dep=take())
            dh, dhb, dga = rms_bwd(dxn, rec["h"], a_norm_full[l], dh1, name=f"a{l}_dnorm")
            small_g[("a_norm", l)] = dga
        else:
            j = l - n_a
            do = mm_nt(dhb, W[("o", j)], name=f"b{j}_do")
            big("b_w_o", j, ("mix", l), mm_tn(rec["o"], dhb, 1, name=f"b{j}_wo"),
                b_w_o, m_b_w_o, v_b_w_o)
            prev = None if dk is None else (dk, dv, dck, dcq)
            dq, dk, dv, dck, dcq = attn_bwd(rec["q"], k, v, do, rec["o"], rec["lse"], crow, prev, pad,
                                            name=f"b{j}_dattn", dep=take())
            dqraw, dqn = hn_bwd(dq, rec["qraw"], b_q_norm[j], q_scale, name=f"b{j}_dqnorm")
            small_g[("b_q_norm", j)] = dqn
            big("b_w_q", j, ("mix", l), mm_tn(rec["xn"], dqraw, 1, name=f"b{j}_wq"),
                b_w_q, m_b_w_q, v_b_w_q)
            dxn = mm_nt(dqraw, W[("q", j)], name=f"b{j}_dq", dep=take())
            dh, dhb, dgb = rms_bwd(dxn, rec["h"], b_norm[j], dh1, name=f"b{j}_dnorm")
            small_g[("b_norm", j)] = dgb
            if j == 0:
                dkraw, dkn = hn_bwd(dk, shared["kv"], k_norm, 1.0, name="kv_dknorm")
                dkv = jnp.concatenate([dkraw, dv.astype(BF)], axis=1)
                dc_full = _pad_cols(jnp.transpose(dck.reshape(H, T) + dcq.reshape(H, T)), LANES)
                dz, dbf = fgate_bwd(dc_full, shared["logits"], bf_pad, pad, name="f_dgate")
                big("w_kv", 0, ("mix", l), mm_tn(shared["xnk"], dkv, N_DEV, name="kv_wkv"),
                    w_kv[None], m_w_kv[None], v_w_kv[None])
                dwf_t = mm_tn(dz, shared["xnk"], 1, name="f_wf", out_dtype=F32, tn_target=1024,
                              dep=take())[0, :H]
                dxn_f = mm_nt(dz, wf_pad, name="f_dxn")
                dxnk = mm_nt(dkv, W[("kv", 0)], add=dxn_f, name="kv_dxn", gb=4)
                dh, dhb, dgkv = rms_bwd(dxnk, shared["h"], kv_norm, dh, name="kv_dnorm")
    land(None, dh)

    grad_x = dh[lead:][None]

    row8 = lambda a: _pad_rows(_pad_cols(a, D), 8)
    stack = lambda key, n: jnp.concatenate([small_g[(key, i)] for i in range(n)], axis=0)
    g_sharded = jnp.concatenate([dh[pad:lead], row8(stack("a_norm", n_a)), row8(stack("a_conv", n_a)), dwf_t], axis=0)
    g_repl = jnp.concatenate([row8(jnp.concatenate([dgkv, stack("b_norm", n_b)], axis=0)),
                              row8(stack("ffn_norm", depth)),
                              row8(jnp.concatenate([_pad_cols(dkn, D), _pad_cols(stack("b_q_norm", n_b), D),
                                                    _pad_cols(dbf[:, :H], D)], axis=0))], axis=0)
    n_sh = g_sharded.shape[0]
    gathered = all_gather(jnp.concatenate([g_sharded, g_repl], axis=0), name="ag_small_grads",
                          dep=upd["a_w_in"][0])
    parts_sh = lax.dynamic_slice_in_dim(gathered[:, :n_sh], my * Ds, Ds, axis=2)
    parts_rp = gathered[:, n_sh:]

    def pack_sh(t_meta, t_an, t_ac, t_wf):
        return jnp.concatenate([t_meta, _pad_rows(t_an, 8), _pad_rows(t_ac.reshape(n_a * 3, Ds), 8),
                                jnp.transpose(t_wf)], axis=0)

    def pack_rp(t_kv, t_bn, t_fn, t_kn, t_qn, t_bf):
        return jnp.concatenate([row8(jnp.concatenate([t_kv.reshape(1, D), t_bn], axis=0)), row8(t_fn),
                                row8(jnp.concatenate([_pad_cols(t_kn.reshape(1, -1), D), _pad_cols(t_qn, D),
                                                      _pad_cols(t_bf.reshape(1, -1), D)], axis=0))], axis=0)

    res_sh = adamw(parts_sh, pack_sh(meta, a_norm, a_conv, w_f), pack_sh(m_meta, m_a_norm, m_a_conv, m_w_f),
                   pack_sh(v_meta, v_a_norm, v_a_conv, v_w_f), name="small_sharded_adamw")
    res_rp = adamw(parts_rp, pack_rp(kv_norm, b_norm, ffn_norm, k_norm, b_q_norm, b_f),
                   pack_rp(m_kv_norm, m_b_norm, m_ffn_norm, m_k_norm, m_b_q_norm, m_b_f),
                   pack_rp(v_kv_norm, v_b_norm, v_ffn_norm, v_k_norm, v_b_q_norm, v_b_f), name="small_repl_adamw")

    def unpack(kind):
        sh, rp = res_sh[kind], res_rp[kind]
        out = {
            "meta": sh[0:n_meta],
            "a_norm": sh[r_an:r_an + n_a],
            "a_conv": sh[r_ac:r_ac + 3 * n_a].reshape(n_a, 3, Ds),
            "w_f": jnp.transpose(sh[r_wf:r_wf + H]),
            "kv_norm": rp[0],
            "b_norm": rp[1:1 + n_b],
            "ffn_norm": rp[8:8 + depth],
            "k_norm": rp[16, :HEAD_DIM],
            "b_q_norm": rp[17:17 + n_b, :HEAD_DIM],
            "b_f": rp[17 + n_b, :H],
        }
        for name, like in (("a_w_in", a_w_in), ("a_w_out", a_w_out), ("b_w_q", b_w_q), ("b_w_o", b_w_o),
                           ("ffn_w_gu", ffn_w_gu), ("ffn_w_down", ffn_w_down)):
            out[name] = upd[name][kind].reshape(like.shape)
        out["w_kv"] = upd["w_kv"][kind].reshape(w_kv.shape)
        return out

    order = ["meta", "a_norm", "a_w_in", "a_conv", "a_w_out", "kv_norm", "w_kv", "k_norm", "w_f", "b_f",
             "b_norm", "b_w_q", "b_q_norm", "b_w_o", "ffn_norm", "ffn_w_gu", "ffn_w_down"]
    outs = [loss, grad_x]
    for kind in range(4):
        vals = unpack(kind)
        outs += [vals[n] for n in order]
    return tuple(outs)
```

```python
import functools
import math

import jax
import jax.numpy as jnp
from jax import lax
from jax.experimental import pallas as pl
from jax.experimental.pallas import tpu as pltpu

N_DEV = 8
MESH_AXES = ("x", "y", "c")
EPS = 1e-6
NEG = -1e30
HEAD_DIM = 128
BLOCK = 128
LANES = 128
V7X_VMEM_LIMIT = 56 * 1024 * 1024

ADAM_LR = 0.001
ADAM_B1 = 0.9
ADAM_B2 = 0.999
ADAM_EPS = 1e-08
ADAM_WD = 0.01
ADAM_STEP = 10

BF = jnp.bfloat16
F32 = jnp.float32


def _tile(n, target, mult):
    best = None
    for t in range(mult, min(n, target) + 1, mult):
        if n % t == 0:
            best = t
    return n if best is None else best


def _params(*sem):
    return pltpu.CompilerParams(dimension_semantics=sem, vmem_limit_bytes=V7X_VMEM_LIMIT)


def _with_dep(body, in_specs, args, dep):
    if dep is None:
        return body, list(in_specs), list(args)
    n_in = len(args)

    def body_dep(*refs):
        body(*refs[:n_in], *refs[n_in + 1:])

    return body_dep, list(in_specs) + [pl.BlockSpec(memory_space=pl.ANY)], list(args) + [dep]


def mm_nn(a, w, *, name, add=None, dep=None, out_dtype=F32, tm_target=1056, tn_target=1024, tk_target=2048,
          resident=False):
    M, K = a.shape
    G, K2, n = w.shape
    assert K == K2
    tm = _tile(M, tm_target, 16)
    tn = _tile(n, tn_target, LANES)
    tk = K if resident else _tile(K, tk_target, LANES)
    nj, nk = n // tn, K // tk
    has_add = add is not None
    if resident:
        grid, sem = (G * nj, M // tm), ("parallel", "parallel")
        order = lambda f: (lambda j, i: f(i, j, 0))
        w_mode = dict(pipeline_mode=pl.Buffered(1))
    else:
        grid, sem = (M // tm, G * nj, nk), ("parallel", "parallel", "arbitrary")
        order = lambda f: f
        w_mode = {}

    def body(*refs):
        if has_add:
            a_ref, w_ref, add_ref, o_ref = refs[:4]
        else:
            a_ref, w_ref, o_ref = refs[:3]
            add_ref = None

        def finish(r):
            if has_add:
                r = r + add_ref[...]
            o_ref[...] = r.astype(out_dtype)

        part = jnp.dot(a_ref[...], w_ref[...], preferred_element_type=F32)
        if nk == 1:
            finish(part)
        else:
            acc_ref = refs[-1]
            k = pl.program_id(2)

            @pl.when(k == 0)
            def _():
                acc_ref[...] = part

            @pl.when(k > 0)
            def _():
                acc_ref[...] += part

            @pl.when(k == nk - 1)
            def _():
                finish(acc_ref[...])

    in_specs = [
        pl.BlockSpec((tm, tk), order(lambda i, j, k: (i, k))),
        pl.BlockSpec((None, tk, tn), order(lambda i, j, k: (j // nj, k, j % nj)), **w_mode),
    ]
    args = [a, w]
    if has_add:
        in_specs.append(pl.BlockSpec((tm, tn), order(lambda i, j, k: (i, j))))
        args.append(add)
    body, in_specs, args = _with_dep(body, in_specs, args, dep)
    return pl.pallas_call(
        body,
        out_shape=jax.ShapeDtypeStruct((M, G * n), out_dtype),
        grid=grid,
        in_specs=in_specs,
        out_specs=pl.BlockSpec((tm, tn), order(lambda i, j, k: (i, j))),
        scratch_shapes=[pltpu.VMEM((tm, tn), F32)] if nk > 1 else [],
        compiler_params=_params(*sem),
        name=name,
    )(*args)


def mm_swiglu(xn, wgu, *, name, dep=None, save_dtype=BF, tm_target=528):
    M, K = xn.shape
    G, _, n = wgu.shape
    half = G // 2
    tm = _tile(M, tm_target, 16)
    tn = _tile(n, 1408, LANES)
    nj = n // tn
    Fh = half * n

    def body(a_ref, wg_ref, wu_ref, act_ref, silu_ref, udsilu_ref):
        a = a_ref[...]
        g = jnp.dot(a, wg_ref[...], preferred_element_type=F32)
        sig = jax.nn.sigmoid(g)
        silu = g * sig
        silu_ref[...] = silu.astype(save_dtype)
        dsilu = sig * (1.0 + g * (1.0 - sig))
        u = jnp.dot(a, wu_ref[...], preferred_element_type=F32)
        udsilu_ref[...] = (u * dsilu).astype(save_dtype)
        act_ref[...] = (silu * u).astype(BF)

    out_block = pl.BlockSpec((tm, tn), lambda j, i: (i, j))
    once = pl.Buffered(1)
    body, in_specs, args = _with_dep(body, [
        pl.BlockSpec((tm, K), lambda j, i: (i, 0)),
        pl.BlockSpec((None, K, tn), lambda j, i: (j // nj, 0, j % nj), pipeline_mode=once),
        pl.BlockSpec((None, K, tn), lambda j, i: (half + j // nj, 0, j % nj), pipeline_mode=once),
    ], [xn, wgu, wgu], dep)
    return pl.pallas_call(
        body,
        out_shape=(jax.ShapeDtypeStruct((M, Fh), BF),
                   jax.ShapeDtypeStruct((M, Fh), save_dtype),
                   jax.ShapeDtypeStruct((M, Fh), save_dtype)),
        grid=(half * nj, M // tm),
        in_specs=in_specs,
        out_specs=(out_block, out_block, out_block),
        compiler_params=_params("parallel", "parallel"),
        name=name,
    )(*args)


def mm_nt(dy, w, *, name, add=None, dep=None, out_dtype=F32, tm_target=1056, tko_target=1024, tc_target=2048,
          gb=1):
    if dy.ndim == 2:
        dy = dy.reshape(1, *dy.shape)
    P, M, Np = dy.shape
    G, K, n = w.shape
    assert P * Np == G * n
    tm = _tile(M, tm_target, 16)
    tko = _tile(K, tko_target, LANES)
    tc = _tile(n, tc_target, LANES)
    nc = n // tc
    gb = gb if nc == 1 else 1
    assert G % gb == 0 and Np % (gb * tc) == 0
    steps = (G // gb) * nc
    per_part = Np // (gb * tc)
    has_add = add is not None

    def body(*refs):
        if has_add:
            dy_ref, w_ref, add_ref, o_ref = refs[:4]
        else:
            dy_ref, w_ref, o_ref = refs[:3]
            add_ref = None

        def finish(r):
            if has_add:
                r = r + add_ref[...]
            o_ref[...] = r.astype(out_dtype)

        part = None
        for g in range(gb):
            term = lax.dot_general(dy_ref[:, g * tc:(g + 1) * tc], w_ref[g], (((1,), (1,)), ((), ())),
                                   preferred_element_type=F32)
            part = term if part is None else part + term
        if steps == 1:
            finish(part)
        else:
            acc_ref = refs[-1]
            s = pl.program_id(2)

            @pl.when(s == 0)
            def _():
                acc_ref[...] = part

            @pl.when(s > 0)
            def _():
                acc_ref[...] += part

            @pl.when(s == steps - 1)
            def _():
                finish(acc_ref[...])

    in_specs = [
        pl.BlockSpec((None, tm, gb * tc), lambda i, o, s: (s // per_part, i, s % per_part)),
        pl.BlockSpec((gb, tko, tc), lambda i, o, s: (s // nc, o, s % nc)),
    ]
    args = [dy, w]
    if has_add:
        in_specs.append(pl.BlockSpec((tm, tko), lambda i, o, s: (i, o)))
        args.append(add)
    body, in_specs, args = _with_dep(body, in_specs, args, dep)
    return pl.pallas_call(
        body,
        out_shape=jax.ShapeDtypeStruct((M, K), out_dtype),
        grid=(M // tm, K // tko, steps),
        in_specs=in_specs,
        out_specs=pl.BlockSpec((tm, tko), lambda i, o, s: (i, o)),
        scratch_shapes=[pltpu.VMEM((tm, tko), F32)] if steps > 1 else [],
        compiler_params=_params("parallel", "parallel", "arbitrary"),
        name=name,
    )(*args)


def mm_nt_dswiglu(dh, w_down, g_s, u_s, *, name, tm_target=1056, tf_target=512):
    M, D = dh.shape
    _, Fh, D2 = w_down.shape
    assert D == D2
    tm = _tile(M, tm_target, 16)
    tf = _tile(Fh, tf_target, LANES)

    def body(dh_ref, w_ref, silu_ref, udsilu_ref, dgu_ref):
        dact = lax.dot_general(dh_ref[...], w_ref[...], (((1,), (1,)), ((), ())),
                               preferred_element_type=F32)
        dgu_ref[1] = (dact * silu_ref[...].astype(F32)).astype(BF)
        dgu_ref[0] = (dact * udsilu_ref[...].astype(F32)).astype(BF)

    blk = pl.BlockSpec((tm, tf), lambda i, f: (i, f))
    return pl.pallas_call(
        body,
        out_shape=jax.ShapeDtypeStruct((2, M, Fh), BF),
        grid=(M // tm, Fh // tf),
        in_specs=[
            pl.BlockSpec((tm, D), lambda i, f: (i, 0)),
            pl.BlockSpec((None, tf, D), lambda i, f: (0, f, 0)),
            blk, blk,
        ],
        out_specs=pl.BlockSpec((2, tm, tf), lambda i, f: (0, i, f)),
        compiler_params=_params("parallel", "parallel"),
        name=name,
    )(dh, w_down, g_s, u_s)


def mm_tn(a, dy, groups, *, name, dep=None, out_dtype=BF, tk_target=512, tn_target=1408):
    M, K = a.shape
    if dy.ndim == 2:
        dy = dy.reshape(1, *dy.shape)
    P, M2, Np = dy.shape
    N = P * Np
    assert M == M2 and N % groups == 0
    n = N // groups
    tk = _tile(K, tk_target, LANES)
    tn = _tile(n, tn_target, LANES)
    nj = n // tn
    assert Np % tn == 0
    per_part = Np // tn

    def body(a_ref, dy_ref, o_ref):
        o_ref[...] = lax.dot_general(a_ref[...], dy_ref[...], (((0,), (0,)), ((), ())),
                                     preferred_element_type=F32).astype(out_dtype)

    body, in_specs, args = _with_dep(body, [
        pl.BlockSpec((M, tk), lambda i, j: (0, i)),
        pl.BlockSpec((None, M, tn), lambda i, j: (j // per_part, 0, j % per_part)),
    ], [a, dy], dep)
    return pl.pallas_call(
        body,
        out_shape=jax.ShapeDtypeStruct((groups, K, n), out_dtype),
        grid=(K // tk, groups * nj),
        in_specs=in_specs,
        out_specs=pl.BlockSpec((None, tk, tn), lambda i, j: (j // nj, i, j % nj)),
        compiler_params=_params("parallel", "parallel"),
        name=name,
    )(*args)


def rms_fwd(h, g, *, name, dep=None):
    T, D = h.shape
    tm = _tile(T, 528, 16)

    def body(h_ref, g_ref, o_ref):
        x = h_ref[...]
        r = lax.rsqrt(jnp.mean(x * x, axis=-1, keepdims=True) + EPS)
        o_ref[...] = ((x * r) * g_ref[...]).astype(BF)

    body, in_specs, args = _with_dep(
        body, [pl.BlockSpec((tm, D), lambda i: (i, 0)), pl.BlockSpec((1, D), lambda i: (0, 0))],
        [h, g.reshape(1, D)], dep)
    return pl.pallas_call(
        body,
        out_shape=jax.ShapeDtypeStruct((T, D), BF),
        grid=(T // tm,),
        in_specs=in_specs,
        out_specs=pl.BlockSpec((tm, D), lambda i: (i, 0)),
        compiler_params=_params("parallel"),
        name=name,
    )(*args)


def rms_bwd(dxn, h, g, add, *, name):
    T, D = h.shape
    tm = _tile(T, 264, 16)

    def body(dxn_ref, h_ref, g_ref, add_ref, dh_ref, dhb_ref, dg_ref):
        x = h_ref[...]
        dy = dxn_ref[...]
        r = lax.rsqrt(jnp.mean(x * x, axis=-1, keepdims=True) + EPS)
        xhat = x * r
        part = jnp.sum(dy * xhat, axis=0, keepdims=True)

        @pl.when(pl.program_id(0) == 0)
        def _():
            dg_ref[...] = part

        @pl.when(pl.program_id(0) > 0)
        def _():
            dg_ref[...] += part

        dxh = dy * g_ref[...]
        dh = add_ref[...] + r * (dxh - xhat * jnp.mean(dxh * xhat, axis=-1, keepdims=True))
        dh_ref[...] = dh
        dhb_ref[...] = dh.astype(BF)

    row = pl.BlockSpec((tm, D), lambda i: (i, 0))
    vec = pl.BlockSpec((1, D), lambda i: (0, 0))
    return pl.pallas_call(
        body,
        out_shape=(jax.ShapeDtypeStruct((T, D), F32), jax.ShapeDtypeStruct((T, D), BF),
                   jax.ShapeDtypeStruct((1, D), F32)),
        grid=(T // tm,),
        in_specs=[row, row, vec, row],
        out_specs=(row, row, vec),
        compiler_params=_params("arbitrary"),
        name=name,
    )(dxn, h, g.reshape(1, D), add)


def _head_norm(x, gain):
    r = lax.rsqrt(jnp.mean(x * x, axis=-1, keepdims=True) + EPS)
    return (x * r) * gain


def hn_fwd(qraw, gain, out_scale, *, name):
    T, D = qraw.shape
    H = D // HEAD_DIM
    tm = _tile(T, 528, 16)

    def body(q_ref, g_ref, o_ref):
        gain_v = g_ref[...]
        for hd in range(H):
            sl = slice(hd * HEAD_DIM, (hd + 1) * HEAD_DIM)
            o_ref[:, sl] = (_head_norm(q_ref[:, sl], gain_v) * out_scale).astype(BF)

    return pl.pallas_call(
        body,
        out_shape=jax.ShapeDtypeStruct((T, D), BF),
        grid=(T // tm,),
        in_specs=[pl.BlockSpec((tm, D), lambda i: (i, 0)),
                  pl.BlockSpec((1, HEAD_DIM), lambda i: (0, 0))],
        out_specs=pl.BlockSpec((tm, D), lambda i: (i, 0)),
        compiler_params=_params("parallel"),
        name=name,
    )(qraw, gain.reshape(1, HEAD_DIM))


def kv_post(kv, gain, *, name):
    T, D2 = kv.shape
    D = D2 // 2
    H = D // HEAD_DIM
    tm = _tile(T, 528, 16)

    def body(k_ref, v_ref, g_ref, ko_ref, vo_ref):
        gain_v = g_ref[...]
        for hd in range(H):
            sl = slice(hd * HEAD_DIM, (hd + 1) * HEAD_DIM)
            ko_ref[:, sl] = _head_norm(k_ref[:, sl], gain_v).astype(BF)
        vo_ref[...] = v_ref[...].astype(BF)

    blk = pl.BlockSpec((tm, D), lambda i: (i, 0))
    return pl.pallas_call(
        body,
        out_shape=(jax.ShapeDtypeStruct((T, D), BF), jax.ShapeDtypeStruct((T, D), BF)),
        grid=(T // tm,),
        in_specs=[blk, pl.BlockSpec((tm, D), lambda i: (i, 1)),
                  pl.BlockSpec((1, HEAD_DIM), lambda i: (0, 0))],
        out_specs=(blk, blk),
        compiler_params=_params("parallel"),
        name=name,
    )(kv, kv, gain.reshape(1, HEAD_DIM))


def hn_bwd(dq, qraw, gain, out_scale, *, name):
    T, D = dq.shape
    H = D // HEAD_DIM
    tm = _tile(T, 264, 16)

    def body(dq_ref, q_ref, g_ref, o_ref, dg_ref):
        gain_v = g_ref[...]
        part = jnp.zeros((1, HEAD_DIM), F32)
        for hd in range(H):
            sl = slice(hd * HEAD_DIM, (hd + 1) * HEAD_DIM)
            x = q_ref[:, sl]
            dy = dq_ref[:, sl] * out_scale
            r = lax.rsqrt(jnp.mean(x * x, axis=-1, keepdims=True) + EPS)
            xhat = x * r
            part = part + jnp.sum(dy * xhat, axis=0, keepdims=True)
            dxh = dy * gain_v
            o_ref[:, sl] = (r * (dxh - xhat * jnp.mean(dxh * xhat, axis=-1, keepdims=True))).astype(BF)

        @pl.when(pl.program_id(0) == 0)
        def _():
            dg_ref[...] = part

        @pl.when(pl.program_id(0) > 0)
        def _():
            dg_ref[...] += part

    blk = pl.BlockSpec((tm, D), lambda i: (i, 0))
    vec = pl.BlockSpec((1, HEAD_DIM), lambda i: (0, 0))
    return pl.pallas_call(
        body,
        out_shape=(jax.ShapeDtypeStruct((T, D), BF), jax.ShapeDtypeStruct((1, HEAD_DIM), F32)),
        grid=(T // tm,),
        in_specs=[blk, blk, vec],
        out_specs=(blk, vec),
        compiler_params=_params("arbitrary"),
        name=name,
    )(dq, qraw, gain.reshape(1, HEAD_DIM))


def _shift_down(cur, above, k, rowc):
    out = pltpu.roll(cur, k, 0)
    for i in range(k):
        out = jnp.where(rowc == i, above[8 - k + i:8 - k + i + 1], out)
    return out


def _shift_up(cur, below, k, rowc):
    R = cur.shape[0]
    out = pltpu.roll(cur, R - k, 0)
    for i in range(k):
        out = jnp.where(rowc == R - k + i, below[i:i + 1], out)
    return out


def _conv3(u, u_above, wv, rowc):
    u1 = _shift_down(u, u_above, 1, rowc)
    u2 = _shift_down(u, u_above, 2, rowc)
    return wv[0:1] * u2 + wv[1:2] * u1 + wv[2:3] * u, u1, u2


def conv_fwd(proj, w, *, name):
    T, D3 = proj.shape
    D = D3 // 3
    tc = LANES if D % LANES == 0 else D
    nb = D // tc
    R = _tile(T, 264, 8)

    def body(b_ref, c_ref, h_ref, w_ref, y_ref):
        rowc = lax.broadcasted_iota(jnp.int32, (R, 1), 0)
        wv = w_ref[...]
        for r0 in range(0, T, R):
            rows = slice(r0, r0 + R)
            u = c_ref[rows, :] * h_ref[rows, :]
            if r0 == 0:
                above = jnp.zeros((8, tc), F32)
            else:
                above = c_ref[r0 - 8:r0, :] * h_ref[r0 - 8:r0, :]
            conv, _, _ = _conv3(u, above, wv, rowc)
            y_ref[rows, :] = (b_ref[rows, :] * conv).astype(BF)

    return pl.pallas_call(
        body,
        out_shape=jax.ShapeDtypeStruct((T, D), BF),
        grid=(nb,),
        in_specs=[
            pl.BlockSpec((T, tc), lambda j: (0, j)),
            pl.BlockSpec((T, tc), lambda j: (0, nb + j)),
            pl.BlockSpec((T, tc), lambda j: (0, 2 * nb + j)),
            pl.BlockSpec((3, tc), lambda j: (0, j)),
        ],
        out_specs=pl.BlockSpec((T, tc), lambda j: (0, j)),
        compiler_params=_params("parallel"),
        name=name,
    )(proj, proj, proj, w)


def conv_bwd(dy, proj, w, *, name, dep=None):
    T, D = dy.shape
    tc = LANES if D % LANES == 0 else D
    nb = D // tc
    R = _tile(T, 264, 8)

    def body(dy_ref, b_ref, c_ref, h_ref, w_ref, db_ref, dc_ref, dh_ref, dw_ref):
        rowc = lax.broadcasted_iota(jnp.int32, (R, 1), 0)
        wv = w_ref[...]
        dw = [jnp.zeros((1, tc), F32) for _ in range(3)]
        for r0 in range(0, T, R):
            rows = slice(r0, r0 + R)
            c = c_ref[rows, :]
            hh = h_ref[rows, :]
            u = c * hh
            if r0 == 0:
                above = jnp.zeros((8, tc), F32)
            else:
                above = c_ref[r0 - 8:r0, :] * h_ref[r0 - 8:r0, :]
            conv, u1, u2 = _conv3(u, above, wv, rowc)
            dyv = dy_ref[rows, :]
            db_ref[rows, :] = (dyv * conv).astype(BF)
            dconv = dyv * b_ref[rows, :]
            if r0 + R == T:
                below = jnp.zeros((8, tc), F32)
            else:
                below = dy_ref[r0 + R:r0 + R + 8, :] * b_ref[r0 + R:r0 + R + 8, :]
            dw[0] = dw[0] + jnp.sum(dconv * u2, axis=0, keepdims=True)
            dw[1] = dw[1] + jnp.sum(dconv * u1, axis=0, keepdims=True)
            dw[2] = dw[2] + jnp.sum(dconv * u, axis=0, keepdims=True)
            du = (wv[2:3] * dconv + wv[1:2] * _shift_up(dconv, below, 1, rowc)
                  + wv[0:1] * _shift_up(dconv, below, 2, rowc))
            dc_ref[rows, :] = (du * hh).astype(BF)
            dh_ref[rows, :] = (du * c).astype(BF)
        for i in range(3):
            dw_ref[i:i + 1, :] = dw[i]

    strip = pl.BlockSpec((T, tc), lambda j: (0, j))
    wblk = pl.BlockSpec((3, tc), lambda j: (0, j))
    out = jax.ShapeDtypeStruct((T, D), BF)
    body, in_specs, args = _with_dep(body, [
        strip,
        pl.BlockSpec((T, tc), lambda j: (0, j)),
        pl.BlockSpec((T, tc), lambda j: (0, nb + j)),
        pl.BlockSpec((T, tc), lambda j: (0, 2 * nb + j)),
        wblk,
    ], [dy, proj, proj, proj, w], dep)
    return pl.pallas_call(
        body,
        out_shape=(out, out, out, jax.ShapeDtypeStruct((3, D), F32)),
        grid=(nb,),
        in_specs=in_specs,
        out_specs=(strip, strip, strip, wblk),
        compiler_params=_params("parallel"),
        name=name,
    )(*args)


def _log_sigmoid(z):
    return jnp.minimum(z, 0.0) - jnp.log(1.0 + jnp.exp(-jnp.abs(z)))


def fgate_fwd(logits, bias, pad, *, name):
    T, W = logits.shape
    cb = _tile(T, 128, 8)
    nblk = T // cb

    def body(z_ref, b_ref, c_ref, lf_ref):
        row = lax.broadcasted_iota(jnp.int32, (T, 1), 0)
        lf_ref[...] = jnp.where(row >= pad, _log_sigmoid(z_ref[...] + b_ref[...]), 0.0)
        ri = lax.broadcasted_iota(jnp.int32, (cb, cb), 0)
        ci = lax.broadcasted_iota(jnp.int32, (cb, cb), 1)
        tri = (ci <= ri).astype(F32)

        def step(i, carry):
            rows = pl.ds(pl.multiple_of(i * cb, cb), cb)
            blk = lf_ref[rows, :]
            c_ref[rows, :] = carry + jnp.dot(tri, blk, precision=lax.Precision.HIGHEST,
                                             preferred_element_type=F32)
            return carry + jnp.sum(blk, axis=0, keepdims=True)

        lax.fori_loop(0, nblk, step, jnp.zeros((1, W), F32))

    return pl.pallas_call(
        body,
        out_shape=jax.ShapeDtypeStruct((T, W), F32),
        in_specs=[pl.BlockSpec(memory_space=pltpu.VMEM), pl.BlockSpec(memory_space=pltpu.VMEM)],
        out_specs=pl.BlockSpec(memory_space=pltpu.VMEM),
        scratch_shapes=[pltpu.VMEM((T, W), F32)],
        compiler_params=pltpu.CompilerParams(vmem_limit_bytes=V7X_VMEM_LIMIT),
        name=name,
    )(logits, bias)


def fgate_bwd(dc, logits, bias, pad, *, name):
    T, W = logits.shape
    cb = _tile(T, 128, 8)
    nblk = T // cb

    def body(dc_ref, z_ref, b_ref, dz_ref, db_ref, rs_ref):
        ri = lax.broadcasted_iota(jnp.int32, (cb, cb), 0)
        ci = lax.broadcasted_iota(jnp.int32, (cb, cb), 1)
        triu = (ci >= ri).astype(F32)

        def step(i, carry):
            rows = pl.ds(pl.multiple_of((nblk - 1 - i) * cb, cb), cb)
            blk = dc_ref[rows, :]
            rs_ref[rows, :] = carry + jnp.dot(triu, blk, precision=lax.Precision.HIGHEST,
                                              preferred_element_type=F32)
            return carry + jnp.sum(blk, axis=0, keepdims=True)

        lax.fori_loop(0, nblk, step, jnp.zeros((1, W), F32))
        row = lax.broadcasted_iota(jnp.int32, (T, 1), 0)
        z = z_ref[...] + b_ref[...]
        dz = jnp.where(row >= pad, rs_ref[...] * jax.nn.sigmoid(-z), 0.0)
        dz_ref[...] = dz.astype(BF)
        db_ref[...] = jnp.sum(dz, axis=0, keepdims=True)

    vm = pl.BlockSpec(memory_space=pltpu.VMEM)
    return pl.pallas_call(
        body,
        out_shape=(jax.ShapeDtypeStruct((T, W), BF), jax.ShapeDtypeStruct((1, W), F32)),
        in_specs=[vm, vm, vm],
        out_specs=(vm, vm),
        scratch_shapes=[pltpu.VMEM((T, W), F32)],
        compiler_params=pltpu.CompilerParams(vmem_limit_bytes=V7X_VMEM_LIMIT),
        name=name,
    )(dc, logits, bias)


def _scores(qb, kb, ck):
    return lax.dot_general(qb, kb, (((1,), (1,)), ((), ())), preferred_element_type=F32) - ck


def _causal(s, row, col, pad):
    return jnp.where((col <= row) & (col >= pad), s, NEG)


def attn_fwd(q, k, v, crow, pad, *, name):
    T, D = q.shape
    H = D // HEAD_DIM
    nk, tk = crow.shape[1], crow.shape[3]
    tq = tk
    nq = T // tq

    hp = 2 if H % 2 == 0 else 1
    wide = hp * HEAD_DIM

    def body(q_ref, k_ref, v_ref, cr_ref, o_ref, lse_ref):
        qi = pl.program_id(1)
        row = qi * tq + lax.broadcasted_iota(jnp.int32, (tq, 1), 0)
        heads = [slice(a * HEAD_DIM, (a + 1) * HEAD_DIM) for a in range(hp)]
        qbs = [q_ref[:, sl] for sl in heads]

        def step(kc, carry, masked):
            rows = pl.ds(pl.multiple_of(kc * tk, tk), tk)
            out = []
            for a, sl in enumerate(heads):
                m, l, acc = carry[a]
                s = _scores(qbs[a], k_ref[rows, sl], cr_ref[a, kc])
                if masked:
                    s = _causal(s, row, kc * tk + lax.broadcasted_iota(jnp.int32, (1, tk), 1), pad)
                m_new = jnp.maximum(m, jnp.max(s, axis=-1, keepdims=True))
                alpha = jnp.exp(m - m_new)
                p = jnp.exp(s - m_new)
                l = alpha * l + jnp.sum(p, axis=-1, keepdims=True)
                acc = alpha * acc + jnp.dot(p.astype(BF), v_ref[rows, sl], preferred_element_type=F32)
                out.append((m_new, l, acc))
            return tuple(out)

        init = tuple((jnp.full((tq, 1), NEG, F32), jnp.zeros((tq, 1), F32), jnp.zeros((tq, HEAD_DIM), F32))
                     for _ in heads)
        carry = step(0, init, True)
        carry = lax.fori_loop(1, qi, lambda kc, c: step(kc, c, False), carry)
        carry = lax.cond(qi > 0, lambda c: step(qi, c, True), lambda c: c, carry)
        valid = row >= pad
        for a, sl in enumerate(heads):
            m, l, acc = carry[a]
            o_ref[:, sl] = jnp.where(valid, acc / l, 0.0).astype(BF)
            lse_ref[a] = jnp.where(valid, m + jnp.log(l), 0.0)

    return pl.pallas_call(
        body,
        out_shape=(jax.ShapeDtypeStruct((T, D), BF), jax.ShapeDtypeStruct((H, T, 1), F32)),
        grid=(H // hp, nq),
        in_specs=[
            pl.BlockSpec((tq, wide), lambda h, i: (i, h)),
            pl.BlockSpec((T, wide), lambda h, i: (0, h)),
            pl.BlockSpec((T, wide), lambda h, i: (0, h)),
            pl.BlockSpec((hp, nk, 1, tk), lambda h, i: (h, 0, 0, 0)),
        ],
        out_specs=(pl.BlockSpec((tq, wide), lambda h, i: (i, h)),
                   pl.BlockSpec((hp, tq, 1), lambda h, i: (h, i, 0))),
        compiler_params=_params("parallel", "arbitrary"),
        name=name,
    )(q, k, v, crow)


def attn_bwd(q, k, v, do, o, lse, ccol, prev, pad, tk, *, name, dep=None):
    T, D = q.shape
    H = D // HEAD_DIM
    nk = T // tk
    tq, nq = tk, nk
    has_prev = prev is not None
    hp = 2 if H % 2 == 0 else 1
    wide = hp * HEAD_DIM
    heads = [slice(a * HEAD_DIM, (a + 1) * HEAD_DIM) for a in range(hp)]
    nt = (((1,), (1,)), ((), ()))

    def body(*refs):
        q_ref, k_ref, v_ref, do_ref, o_ref, lse_ref, cc_ref = refs[:7]
        refs = refs[7:]
        if has_prev:
            pk_ref, pv_ref, pc_ref, pq_ref = refs[:4]
            refs = refs[4:]
        dq_ref, dk_ref, dv_ref, dck_ref, dcq_ref, delta_ref = refs
        kc = pl.program_id(1)

        @pl.when(kc == 0)
        def _():
            dq_ref[...] = jnp.zeros_like(dq_ref)
            dcq_ref[...] = pq_ref[...] if has_prev else jnp.zeros_like(dcq_ref)
            ones = jnp.ones((8, HEAD_DIM), F32)
            for a, sl in enumerate(heads):
                for i in range(nq):
                    rows = slice(i * tq, (i + 1) * tq)
                    prod = do_ref[rows, sl].astype(BF).astype(F32) * o_ref[rows, sl].astype(F32)
                    delta_ref[a, i] = lax.dot_general(ones, prod, nt, precision=lax.Precision.HIGHEST,
                                                      preferred_element_type=F32)[0:1]

        kbs = [k_ref[:, sl] for sl in heads]
        vbs = [v_ref[:, sl] for sl in heads]
        cks = [jnp.broadcast_to(cc_ref[a], (tk, tq)) for a in range(hp)]
        krow = kc * tk + lax.broadcasted_iota(jnp.int32, (tk, 1), 0)

        def step(qi, carry, masked):
            rows = pl.ds(pl.multiple_of(qi * tq, tq), tq)
            out = []
            for a, sl in enumerate(heads):
                dk, dv, dck = carry[a]
                qb = q_ref[rows, sl]
                dob = do_ref[rows, sl].astype(BF)
                s = lax.dot_general(kbs[a], qb, nt, preferred_element_type=F32) - cks[a]
                if masked:
                    qcol = qi * tq + lax.broadcasted_iota(jnp.int32, (1, tq), 1)
                    s = jnp.where((krow <= qcol) & (krow >= pad), s, NEG)
                p = jnp.exp(s - lse_ref[a, qi])
                dp = lax.dot_general(vbs[a], dob, nt, preferred_element_type=F32)
                ds = p * (dp - delta_ref[a, qi])
                dsb = ds.astype(BF)
                dv = dv + jnp.dot(p.astype(BF), dob, preferred_element_type=F32)
                dk = dk + jnp.dot(dsb, qb, preferred_element_type=F32)
                dq_ref[rows, sl] += lax.dot_general(dsb, kbs[a], (((0,), (0,)), ((), ())),
                                                    preferred_element_type=F32)
                dcq_ref[a, qi] += jnp.sum(ds, axis=0, keepdims=True)
                part = ds[:, 0:LANES]
                for j in range(1, tq // LANES):
                    part = part + ds[:, j * LANES:(j + 1) * LANES]
                out.append((dk, dv, dck - part))
            return tuple(out)

        def rest(masked):
            return lambda c: lax.fori_loop(kc + 1, nq, lambda qi, cc: step(qi, cc, masked), c)

        init = tuple((jnp.zeros((tk, HEAD_DIM), F32), jnp.zeros((tk, HEAD_DIM), F32), jnp.zeros((tk, LANES), F32))
                     for _ in heads)
        carry = step(kc, init, True)
        carry = lax.cond(kc == 0, rest(True), rest(False), carry)
        for a, sl in enumerate(heads):
            dk, dv, dck = carry[a]
            dck = jnp.sum(dck, axis=1, keepdims=True)
            if has_prev:
                dk = dk + pk_ref[:, sl]
                dv = dv + pv_ref[:, sl]
                dck = dck + pc_ref[a]
            dk_ref[:, sl] = dk
            dv_ref[:, sl] = dv
            dck_ref[a] = dck

    head_all = pl.BlockSpec((T, wide), lambda h, j: (0, h))
    head_blk = pl.BlockSpec((tk, wide), lambda h, j: (j, h))
    rows_all = pl.BlockSpec((hp, nq, 1, tq), lambda h, j: (h, 0, 0, 0))
    col_blk = pl.BlockSpec((hp, tk, 1), lambda h, j: (h, j, 0))
    in_specs = [head_all, head_blk, head_blk, head_all, head_all, rows_all, col_blk]
    args = [q, k, v, do, o, lse, ccol]
    if has_prev:
        in_specs += [head_blk, head_blk, col_blk, rows_all]
        args += list(prev)
    body, in_specs, args = _with_dep(body, in_specs, args, dep)
    return pl.pallas_call(
        body,
        out_shape=(jax.ShapeDtypeStruct((T, D), F32), jax.ShapeDtypeStruct((T, D), F32),
                   jax.ShapeDtypeStruct((T, D), F32), jax.ShapeDtypeStruct((H, T, 1), F32),
                   jax.ShapeDtypeStruct((H, nq, 1, tq), F32)),
        grid=(H // hp, nk),
        in_specs=in_specs,
        out_specs=(head_all, head_blk, head_blk, col_blk, rows_all),
        scratch_shapes=[pltpu.VMEM((hp, nq, 1, tq), F32)],
        compiler_params=_params("parallel", "arbitrary"),
        name=name,
    )(*args)


def loss_head(h, target, lead, *, name):
    T, D = h.shape
    tm = lead
    assert T % tm == 0 and target.shape[0] % tm == 0
    inv_d = 1.0 / D

    def body(h_ref, t_ref, dh_ref, dhb_ref, loss_ref):
        i = pl.program_id(0)

        @pl.when(i == 0)
        def _():
            dh_ref[...] = jnp.zeros_like(dh_ref)
            dhb_ref[...] = jnp.zeros_like(dhb_ref)
            loss_ref[...] = jnp.zeros_like(loss_ref)

        @pl.when(i > 0)
        def _():
            e = h_ref[...] - t_ref[...]
            dh = e * inv_d
            dh_ref[...] = dh
            dhb_ref[...] = dh.astype(BF)
            loss_ref[...] += 0.5 * inv_d * jnp.sum(e * e)

    return pl.pallas_call(
        body,
        out_shape=(jax.ShapeDtypeStruct((T, D), F32), jax.ShapeDtypeStruct((T, D), BF),
                   jax.ShapeDtypeStruct((8, LANES), F32)),
        grid=(T // tm,),
        in_specs=[pl.BlockSpec((tm, D), lambda i: (i, 0)),
                  pl.BlockSpec((tm, D), lambda i: (jnp.maximum(i - 1, 0), 0))],
        out_specs=(pl.BlockSpec((tm, D), lambda i: (i, 0)), pl.BlockSpec((tm, D), lambda i: (i, 0)),
                   pl.BlockSpec((8, LANES), lambda i: (0, 0))),
        compiler_params=_params("arbitrary"),
        name=name,
    )(h, target)


def adamw(parts, w, m, v, *, name):
    P, R, C = parts.shape
    tr = _tile(R, max(16, (128 * 1024) // C), 16)

    def body(p_ref, w_ref, m_ref, v_ref, g_ref, d_ref, mo_ref, vo_ref):
        g = p_ref[0].astype(F32)
        for i in range(1, P):
            g = g + p_ref[i].astype(F32)
        m_new = ADAM_B1 * m_ref[...] + (1.0 - ADAM_B1) * g
        v_new = ADAM_B2 * v_ref[...] + (1.0 - ADAM_B2) * jnp.square(g)
        m_hat = m_new / (1.0 - ADAM_B1 ** ADAM_STEP)
        v_hat = v_new / (1.0 - ADAM_B2 ** ADAM_STEP)
        g_ref[...] = g
        d_ref[...] = -ADAM_LR * (m_hat / (jnp.sqrt(v_hat) + ADAM_EPS) + ADAM_WD * w_ref[...])
        mo_ref[...] = m_new
        vo_ref[...] = v_new

    blk = pl.BlockSpec((tr, C), lambda i: (i, 0))
    out = jax.ShapeDtypeStruct((R, C), F32)
    return pl.pallas_call(
        body,
        out_shape=(out, out, out, out),
        grid=(R // tr,),
        in_specs=[pl.BlockSpec((P, tr, C), lambda i: (0, i, 0)), blk, blk, blk],
        out_specs=(blk, blk, blk, blk),
        compiler_params=_params("parallel"),
        name=name,
    )(parts, w, m, v)


def _flip(v, bit):
    return 1 - v if bit else v


def all_gather(shard, *, name, dep=None):
    def body(x_ref, out_ref, send_sems, recv_sems, local_sem):
        x, y, c = lax.axis_index("x"), lax.axis_index("y"), lax.axis_index("c")
        me, sibling = (x, y, c), (x, y, 1 - c)
        chips = [(1 - x, y), (x, 1 - y), (1 - x, 1 - y)]

        def block(px, py, pc):
            return out_ref.at[4 * px + 2 * py + pc]

        def copy(k, blk, to, src=None):
            return pltpu.make_async_remote_copy(
                src_ref=block(*blk) if src is None else src,
                dst_ref=block(*blk),
                send_sem=send_sems.at[k],
                recv_sem=recv_sems.at[k],
                device_id=to,
                device_id_type=pl.DeviceIdType.MESH,
            )

        mine = pltpu.make_async_copy(x_ref, block(*me), local_sem)
        mine.start()
        first = [copy(0, me, sibling, src=x_ref)]
        first += [copy(1 + j, me, (*chip, c), src=x_ref) for j, chip in enumerate(chips)]
        for cp in first:
            cp.start()
        passed = [copy(4 + j, (*chip, c), sibling) for j, chip in enumerate(chips)]
        for j, chip in enumerate(chips):
            copy(1 + j, (*chip, c), me).wait_recv()
            passed[j].start()
        copy(0, sibling, me).wait_recv()
        for j, chip in enumerate(chips):
            copy(4 + j, (*chip, 1 - c), me).wait_recv()
        for cp in first + passed:
            cp.wait_send()
        mine.wait()

    body, in_specs, args = _with_dep(body, [pl.BlockSpec(memory_space=pl.ANY)], [shard], dep)
    return pl.pallas_call(
        body,
        out_shape=jax.ShapeDtypeStruct((N_DEV,) + shard.shape, shard.dtype),
        in_specs=in_specs,
        out_specs=pl.BlockSpec(memory_space=pl.ANY),
        scratch_shapes=[pltpu.SemaphoreType.DMA((7,)), pltpu.SemaphoreType.DMA((7,)),
                        pltpu.SemaphoreType.DMA],
        name=name,
    )(*args)


def exchange_slabs(slabs, *, name):
    def body(g_ref, r_ref, send_sems, recv_sems, local_sem):
        x, y, c = lax.axis_index("x"), lax.axis_index("y"), lax.axis_index("c")
        me = 4 * x + 2 * y + c
        mine = pltpu.make_async_copy(g_ref.at[me], r_ref.at[me], local_sem)
        mine.start()
        sends, recvs = [], []
        for k in range(1, N_DEV):
            px, py, pc = _flip(x, (k >> 2) & 1), _flip(y, (k >> 1) & 1), _flip(c, k & 1)
            peer = 4 * px + 2 * py + pc
            sends.append(pltpu.make_async_remote_copy(
                src_ref=g_ref.at[peer], dst_ref=r_ref.at[me],
                send_sem=send_sems.at[k - 1], recv_sem=recv_sems.at[k - 1],
                device_id=(px, py, pc), device_id_type=pl.DeviceIdType.MESH))
            recvs.append(pltpu.make_async_remote_copy(
                src_ref=g_ref.at[peer], dst_ref=r_ref.at[peer],
                send_sem=send_sems.at[k - 1], recv_sem=recv_sems.at[k - 1],
                device_id=(px, py, pc), device_id_type=pl.DeviceIdType.MESH))
        for cp in sends:
            cp.start()
        for cp in recvs:
            cp.wait_recv()
        for cp in sends:
            cp.wait_send()
        mine.wait()

    return pl.pallas_call(
        body,
        out_shape=jax.ShapeDtypeStruct(slabs.shape, slabs.dtype),
        in_specs=[pl.BlockSpec(memory_space=pl.ANY)],
        out_specs=pl.BlockSpec(memory_space=pl.ANY),
        scratch_shapes=[pltpu.SemaphoreType.DMA((7,)), pltpu.SemaphoreType.DMA((7,)),
                        pltpu.SemaphoreType.DMA],
        name=name,
    )(slabs)


def reduce_adamw(slabs, w, m, v, *, name):
    got = exchange_slabs(slabs, name=name + "_xchg")
    return adamw(got, w, m, v, name=name + "_adamw")


_HBM = pl.BlockSpec(memory_space=pltpu.HBM)
_SEM = pl.BlockSpec(memory_space=pltpu.SEMAPHORE)
_ANY = pl.BlockSpec(memory_space=pl.ANY)
_EFFECT = pltpu.SideEffectType.DATAFLOW_SIDE_EFFECTING
_N_FIRST = 4


def _first_copies(land_ref, send_sems, recv_sems):
    x, y, c = lax.axis_index("x"), lax.axis_index("y"), lax.axis_index("c")
    mine = land_ref.at[4 * x + 2 * y + c]
    targets = [(x, y, 1 - c), (1 - x, y, c), (x, 1 - y, c), (1 - x, 1 - y, c)]
    sends, recvs = [], []
    for k, (px, py, pc) in enumerate(targets):
        common = dict(send_sem=send_sems.at[k], recv_sem=recv_sems.at[k], device_id=(px, py, pc),
                      device_id_type=pl.DeviceIdType.MESH)
        sends.append(pltpu.make_async_remote_copy(src_ref=mine, dst_ref=mine, **common))
        theirs = land_ref.at[4 * px + 2 * py + pc]
        recvs.append(pltpu.make_async_remote_copy(src_ref=theirs, dst_ref=theirs, **common))
    return sends, recvs


def _second_copies(land_ref, send_sems, recv_sems):
    x, y, c = lax.axis_index("x"), lax.axis_index("y"), lax.axis_index("c")
    sends, recvs = [], []
    for j, (px, py) in enumerate([(1 - x, y), (x, 1 - y), (1 - x, 1 - y)]):
        common = dict(send_sem=send_sems.at[j], recv_sem=recv_sems.at[j], device_id=(x, y, 1 - c),
                      device_id_type=pl.DeviceIdType.MESH)
        blk = land_ref.at[4 * px + 2 * py + c]
        sends.append(pltpu.make_async_remote_copy(src_ref=blk, dst_ref=blk, **common))
        got = land_ref.at[4 * px + 2 * py + (1 - c)]
        recvs.append(pltpu.make_async_remote_copy(src_ref=got, dst_ref=got, **common))
    return sends, recvs


def gather_start(shard, me, after, *, name):
    R, C = shard.shape
    tr = _tile(R, max(16, (512 * 1024) // C), 16)

    def place_body(me_ref, x_ref, o_ref):
        o_ref[...] = x_ref[...].astype(BF)

    land = pl.pallas_call(
        place_body, name=name + "_own",
        out_shape=jax.ShapeDtypeStruct((N_DEV, R, C), BF),
        grid_spec=pltpu.PrefetchScalarGridSpec(
            num_scalar_prefetch=1,
            grid=(R // tr,),
            in_specs=[pl.BlockSpec((tr, C), lambda i, me_ref: (i, 0))],
            out_specs=pl.BlockSpec((None, tr, C), lambda i, me_ref: (me_ref[0], i, 0)),
        ),
        compiler_params=_params("parallel"),
    )(me.reshape(1).astype(jnp.int32), shard)

    def body(land_ref, after_ref, send_sems, recv_sems, land_thru, token):
        sends, _ = _first_copies(land_ref, send_sems, recv_sems)
        for cp in sends:
            cp.start()
        token[...] = jnp.zeros_like(token)

    send_sems, recv_sems, land_thru, token = pl.pallas_call(
        body, name=name + "_s1",
        out_shape=(pltpu.SemaphoreType.DMA((_N_FIRST,)), pltpu.SemaphoreType.DMA((_N_FIRST,)),
                   pltpu.HBM(land.shape, land.dtype), jax.ShapeDtypeStruct((8, LANES), F32)),
        in_specs=(_HBM, _ANY),
        out_specs=(_SEM, _SEM, _HBM, pl.BlockSpec(memory_space=pltpu.VMEM)),
        input_output_aliases={0: 2},
        compiler_params=pltpu.CompilerParams(has_side_effects=_EFFECT),
    )(pltpu.with_memory_space_constraint(land, pltpu.HBM), after)
    return (send_sems, recv_sems, land_thru), token


def gather_mid(handle, after, *, name):
    send_sems, recv_sems, land_thru = handle

    def body(land_ref, send1, recv1, after_ref, send2, recv2, land_out, token):
        sends, recvs = _first_copies(land_ref, send1, recv1)
        for cp in sends:
            cp.wait_send()
        for cp in recvs:
            cp.wait_recv()
        seconds, _ = _second_copies(land_ref, send2, recv2)
        for cp in seconds:
            cp.start()
        token[...] = jnp.zeros_like(token)

    send2, recv2, land2, token = pl.pallas_call(
        body, name=name + "_s2",
        out_shape=(pltpu.SemaphoreType.DMA((3,)), pltpu.SemaphoreType.DMA((3,)),
                   pltpu.HBM(land_thru.shape, land_thru.dtype), jax.ShapeDtypeStruct((8, LANES), F32)),
        in_specs=(_HBM, _SEM, _SEM, _ANY),
        out_specs=(_SEM, _SEM, _HBM, pl.BlockSpec(memory_space=pltpu.VMEM)),
        input_output_aliases={0: 2},
        compiler_params=pltpu.CompilerParams(has_side_effects=_EFFECT),
    )(land_thru, send_sems, recv_sems, after)
    return (send2, recv2, land2), token


def gather_finish(handle, after, *, name):
    send2, recv2, land2 = handle

    def body(land_ref, send2, recv2, after_ref, got_ref):
        sends, recvs = _second_copies(land_ref, send2, recv2)
        for cp in sends:
            cp.wait_send()
        for cp in recvs:
            cp.wait_recv()

    return pl.pallas_call(
        body, name=name + "_w",
        out_shape=pltpu.HBM(land2.shape, land2.dtype),
        in_specs=(_HBM, _SEM, _SEM, _ANY),
        out_specs=_HBM,
        input_output_aliases={0: 0},
        compiler_params=pltpu.CompilerParams(has_side_effects=_EFFECT),
    )(land2, send2, recv2, after)


def _slab_copies(g_ref, r_ref, send_sems, recv_sems):
    x, y, c = lax.axis_index("x"), lax.axis_index("y"), lax.axis_index("c")
    me = 4 * x + 2 * y + c
    sends, recvs = [], []
    for k in range(1, N_DEV):
        px, py, pc = _flip(x, (k >> 2) & 1), _flip(y, (k >> 1) & 1), _flip(c, k & 1)
        peer = 4 * px + 2 * py + pc
        common = dict(send_sem=send_sems.at[k - 1], recv_sem=recv_sems.at[k - 1], device_id=(px, py, pc),
                      device_id_type=pl.DeviceIdType.MESH)
        sends.append(pltpu.make_async_remote_copy(src_ref=g_ref.at[peer], dst_ref=r_ref.at[me], **common))
        recvs.append(pltpu.make_async_remote_copy(src_ref=g_ref.at[peer], dst_ref=r_ref.at[peer], **common))
    return sends, recvs


def exchange_start(slabs, *, name):
    land = lax.empty(slabs.shape, slabs.dtype)

    def body(g_ref, r_ref, send_sems, recv_sems, g_thru, r_thru, token):
        sends, _ = _slab_copies(g_ref, r_ref, send_sems, recv_sems)
        for cp in sends:
            cp.start()
        token[...] = jnp.zeros_like(token)

    send_sems, recv_sems, g_thru, r_thru, token = pl.pallas_call(
        body, name=name,
        out_shape=(pltpu.SemaphoreType.DMA((N_DEV - 1,)), pltpu.SemaphoreType.DMA((N_DEV - 1,)),
                   pltpu.HBM(slabs.shape, slabs.dtype), pltpu.HBM(slabs.shape, slabs.dtype),
                   jax.ShapeDtypeStruct((8, LANES), F32)),
        in_specs=(_HBM, _HBM),
        out_specs=(_SEM, _SEM, _HBM, _HBM, pl.BlockSpec(memory_space=pltpu.VMEM)),
        input_output_aliases={0: 2, 1: 3},
        compiler_params=pltpu.CompilerParams(has_side_effects=_EFFECT),
    )(pltpu.with_memory_space_constraint(slabs, pltpu.HBM), pltpu.with_memory_space_constraint(land, pltpu.HBM))
    return (send_sems, recv_sems, g_thru, r_thru), token


def exchange_finish(handle, after, *, name):
    send_sems, recv_sems, g_thru, r_thru = handle

    def body(g_ref, r_ref, send_sems, recv_sems, after_ref, g_out, r_out):
        sends, recvs = _slab_copies(g_ref, r_ref, send_sems, recv_sems)
        for cp in sends:
            cp.wait_send()
        for cp in recvs:
            cp.wait_recv()

    return pl.pallas_call(
        body, name=name,
        out_shape=(pltpu.HBM(g_thru.shape, g_thru.dtype), pltpu.HBM(r_thru.shape, r_thru.dtype)),
        in_specs=(_HBM, _HBM, _SEM, _SEM, _ANY),
        out_specs=(_HBM, _HBM),
        input_output_aliases={0: 0, 1: 1},
        compiler_params=pltpu.CompilerParams(has_side_effects=_EFFECT),
    )(g_thru, r_thru, send_sems, recv_sems, after)


def adamw_own(own, got, me, w, m, v, layer, prev, *, name):
    P, R, C = got.shape
    L = w.shape[0]
    tr = _tile(R, max(16, (256 * 1024) // C), 16)

    def body(me_ref, own_ref, p_ref, w_ref, m_ref, v_ref, *rest):
        g_ref, d_ref, mo_ref, vo_ref = rest[-4:]
        mine = own_ref[...].astype(F32)
        g = None
        for i in range(P):
            term = jnp.where(me_ref[0] == i, mine, p_ref[i].astype(F32))
            g = term if g is None else g + term
        m_new = ADAM_B1 * m_ref[...] + (1.0 - ADAM_B1) * g
        v_new = ADAM_B2 * v_ref[...] + (1.0 - ADAM_B2) * jnp.square(g)
        m_hat = m_new / (1.0 - ADAM_B1 ** ADAM_STEP)
        v_hat = v_new / (1.0 - ADAM_B2 ** ADAM_STEP)
        g_ref[...] = g
        d_ref[...] = -ADAM_LR * (m_hat / (jnp.sqrt(v_hat) + ADAM_EPS) + ADAM_WD * w_ref[...])
        mo_ref[...] = m_new
        vo_ref[...] = v_new

    blk = pl.BlockSpec((None, tr, C), lambda i, me_ref: (layer, i, 0))
    out = jax.ShapeDtypeStruct((L, R, C), F32)
    in_specs = [pl.BlockSpec((None, tr, C), lambda i, me_ref: (me_ref[0], i, 0)),
                pl.BlockSpec((P, tr, C), lambda i, me_ref: (0, i, 0)), blk, blk, blk]
    args = [me.reshape(1).astype(jnp.int32), own, got, w, m, v]
    aliases = {}
    if prev is not None:
        in_specs += [pl.BlockSpec(memory_space=pl.ANY)] * 4
        aliases = {len(args) + i: i for i in range(4)}
        args += list(prev)
    return pl.pallas_call(
        body,
        out_shape=(out, out, out, out),
        grid_spec=pltpu.PrefetchScalarGridSpec(
            num_scalar_prefetch=1,
            grid=(R // tr,),
            in_specs=in_specs,
            out_specs=(blk, blk, blk, blk),
        ),
        input_output_aliases=aliases,
        compiler_params=_params("parallel"),
        name=name,
    )(*args)


def _pad_rows(a, rows):
    return jnp.pad(a, ((0, rows - a.shape[0]), (0, 0)))


def _pad_cols(a, cols):
    return jnp.pad(a, ((0, 0), (0, cols - a.shape[1])))


def kernel(x, meta, a_norm, a_w_in, a_conv, a_w_out, kv_norm, w_kv, k_norm, w_f, b_f, b_norm, b_w_q, b_q_norm, b_w_o, ffn_norm, ffn_w_gu, ffn_w_down, loss_target, m_meta, m_a_norm, m_a_w_in, m_a_conv, m_a_w_out, m_kv_norm, m_w_kv, m_k_norm, m_w_f, m_b_f, m_b_norm, m_b_w_q, m_b_q_norm, m_b_w_o, m_ffn_norm, m_ffn_w_gu, m_ffn_w_down, v_meta, v_a_norm, v_a_w_in, v_a_conv, v_a_w_out, v_kv_norm, v_w_kv, v_k_norm, v_w_f, v_b_f, v_b_norm, v_b_w_q, v_b_q_norm, v_b_w_o, v_ffn_norm, v_ffn_w_gu, v_ffn_w_down):
    S, D = x.shape[1], x.shape[2]
    n_meta = meta.shape[0]
    Ds = meta.shape[1]
    H = D // HEAD_DIM
    n_a, n_b = a_w_in.shape[0], b_w_q.shape[0]
    depth = n_a + n_b
    Fs = ffn_w_down.shape[1]
    pad = BLOCK - n_meta
    lead = pad + n_meta
    T = lead + S
    tk_attn = _tile(T, 384, LANES)
    nk_attn = T // tk_attn
    q_scale = 1.0 / math.sqrt(HEAD_DIM)
    my = 4 * lax.axis_index("x") + 2 * lax.axis_index("y") + lax.axis_index("c")

    wf_t = w_f.reshape(H, Ds)
    small = jnp.concatenate([meta, _pad_rows(a_norm, 8), _pad_rows(a_conv.reshape(n_a * 3, Ds), 8), wf_t], axis=0)
    r_an, r_ac, r_wf = n_meta, n_meta + 8, n_meta + 16
    gs = all_gather(small, name="ag_small")
    unshard = lambda blk: jnp.transpose(blk, (1, 0, 2)).reshape(blk.shape[1], D)
    meta_full = unshard(gs[:, 0:n_meta])
    a_norm_full = unshard(gs[:, r_an:r_an + n_a])
    a_conv_full = unshard(gs[:, r_ac:r_ac + 3 * n_a]).reshape(n_a, 3, D)
    w_f_full = gs[:, r_wf:r_wf + H].reshape(D, H)
    wf_pad = _pad_cols(w_f_full, LANES).astype(BF)[None]
    bf_pad = _pad_cols(b_f.reshape(1, H), LANES)

    def layer_shards(l):
        if l < n_a:
            mix = [(("in", l), a_w_in[l]), (("out", l), a_w_out[l])]
        else:
            j = l - n_a
            mix = ([(("kv", 0), w_kv)] if j == 0 else []) + [(("q", j), b_w_q[j]), (("o", j), b_w_o[j])]
        return mix + [(("gu", l), ffn_w_gu[l]), (("dn", l), ffn_w_down[l])]

    first_level, second_level, W = {}, {}, {}
    st = {"done": None, "tok": None}

    def note(val):
        st["done"] = val
        return val

    def take():
        tok, st["tok"] = st["tok"], None
        return tok

    def chain_after(default):
        if st["tok"] is not None:
            return st["tok"]
        return default if st["done"] is None else st["done"]

    def ag_name(key):
        return f"ag_{key[0]}{key[1]}"

    def start_layer(l):
        for key, shard in layer_shards(l):
            first_level[key], st["tok"] = gather_start(shard, my, chain_after(shard), name=ag_name(key))

    def pass_on(keys):
        for key in keys:
            second_level[key], st["tok"] = gather_mid(first_level.pop(key), chain_after(None), name=ag_name(key))

    def weight(key, shape=None):
        w = gather_finish(second_level.pop(key), st["done"], name=ag_name(key))
        W[key] = w if shape is None else w.reshape(shape)
        return W[key]

    def layer_keys(l):
        keys = [key for key, _ in layer_shards(l)]
        return keys[:-2], keys[-2:]

    h = note(jnp.concatenate([jnp.zeros((pad, D), F32), meta_full, x[0]], axis=0))
    start_layer(0)
    pass_on(layer_keys(0)[0])
    saved = []
    shared = None
    for l in range(depth):
        rec = {"h": h}
        mix_keys, ffn_keys = layer_keys(l)

        def ahead():
            if l >= 1:
                pass_on(ffn_keys[:1])
            if l + 1 < depth:
                start_layer(l + 1)

        if l < n_a:
            xn = note(rms_fwd(h, a_norm_full[l], name=f"a{l}_norm", dep=take()))
            ahead()
            proj = note(mm_nn(xn, weight(("in", l)), name=f"a{l}_in", dep=take()))
            if l == 0:
                pass_on(ffn_keys[:1])
            y = note(conv_fwd(proj, a_conv_full[l], name=f"a{l}_conv"))
            h1 = note(mm_nn(y, weight(("out", l), (1, D, D)), add=h, name=f"a{l}_out", dep=take()))
            rec.update(xn=xn, proj=proj, y=y)
        else:
            j = l - n_a
            if j == 0:
                xnk = note(rms_fwd(h, kv_norm, name="kv_norm", dep=take()))
                ahead()
                kv = note(mm_nn(xnk, weight(("kv", 0)), name="kv_proj", dep=take()))
                k, v = kv_post(kv, k_norm, name="kv_post")
                logits = mm_nn(xnk, wf_pad, name="f_logits", tn_target=LANES)
                cfull = fgate_fwd(logits, bf_pad, pad, name="f_gate")
                c_t = jnp.transpose(cfull[:, :H])
                crow = c_t.reshape(H, nk_attn, 1, tk_attn)
                ccol = c_t.reshape(H, T, 1)
                shared = dict(h=h, xnk=xnk, kv=kv, logits=logits)
                xn = note(rms_fwd(h, b_norm[j], name=f"b{j}_norm"))
            else:
                xn = note(rms_fwd(h, b_norm[j], name=f"b{j}_norm", dep=take()))
                ahead()
            qraw = note(mm_nn(xn, weight(("q", j), (1, D, D)), name=f"b{j}_q", dep=take()))
            q = hn_fwd(qraw, b_q_norm[j], q_scale, name=f"b{j}_qnorm")
            o, lse = attn_fwd(q, k, v, crow, pad, name=f"b{j}_attn")
            note(o)
            h1 = note(mm_nn(o, weight(("o", j), (1, D, D)), add=h, name=f"b{j}_o"))
            rec.update(xn=xn, qraw=qraw, q=q, o=o, lse=lse)
        xn2 = note(rms_fwd(h1, ffn_norm[l], name=f"f{l}_norm", dep=take()))
        pass_on(ffn_keys[1:])
        act, g_s, u_s = mm_swiglu(xn2, weight(("gu", l)), name=f"f{l}_gu", dep=take())
        note(act)
        if l + 1 < depth:
            pass_on(layer_keys(l + 1)[0])
        h = note(mm_nn(act, weight(("dn", l), (1, N_DEV * Fs, D)), add=h1, name=f"f{l}_down", resident=True,
                       tm_target=528, tn_target=512, dep=take()))
        rec.update(h1=h1, xn2=xn2, act=act, g=g_s, u=u_s)
        saved.append(rec)

    dh, dhb, loss_tile = loss_head(h, loss_target[0], lead, name="loss")
    loss = lax.psum(loss_tile[0, 0], MESH_AXES)

    upd = {}
    small_g = {}
    inflight = []

    def big(name, l, section, slabs, w, m, v):
        handle, st["tok"] = exchange_start(slabs.reshape(N_DEV, -1, w.shape[-1]), name=f"{name}{l}_xs")
        inflight.append((section, name, l, handle, w, m, v))

    def land(sections, after):
        for entry in [e for e in inflight if sections is None or e[0] in sections]:
            inflight.remove(entry)
            _, name, l, handle, w, m, v = entry
            own, got = exchange_finish(handle, after, name=f"{name}{l}_xw")
            flat = lambda t: t.reshape(w.shape[0], -1, w.shape[-1])
            upd[name] = adamw_own(own, got, my, flat(w), flat(m), flat(v), l, upd.get(name),
                                  name=f"{name}{l}_adamw")

    dk = dv = dck = dcq = None
    for l in reversed(range(depth)):
        rec = saved[l]
        land([("ffn", l + 1)], dh)
        dgu = mm_nt_dswiglu(dhb, W[("dn", l)], rec["g"], rec["u"], name=f"f{l}_ddown")
        big("ffn_w_down", l, ("ffn", l), mm_tn(rec["act"], dhb, 1, name=f"f{l}_wdown"),
            ffn_w_down, m_ffn_w_down, v_ffn_w_down)
        big("ffn_w_gu", l, ("ffn", l), mm_tn(rec["xn2"], dgu, N_DEV, name=f"f{l}_wgu", dep=take()),
            ffn_w_gu, m_ffn_w_gu, v_ffn_w_gu)
        dxn2 = mm_nt(dgu, W[("gu", l)], name=f"f{l}_dgu", gb=2, dep=take())
        dh1, dhb, dgf = rms_bwd(dxn2, rec["h1"], ffn_norm[l], dh, name=f"f{l}_dnorm")
        small_g[("ffn_norm", l)] = dgf
        land([("mix", l + 1)], dh1)
        if l < n_a:
            dy = mm_nt(dhb, W[("out", l)], name=f"a{l}_dout")
            big("a_w_out", l, ("mix", l), mm_tn(rec["y"], dhb, 1, name=f"a{l}_wout"),
                a_w_out, m_a_w_out, v_a_w_out)
            db, dc, dhh, dcw = conv_bwd(dy, rec["proj"], a_conv_full[l], name=f"a{l}_dconv", dep=take())
            small_g[("a_conv", l)] = dcw
            dproj = jnp.concatenate([db, dc, dhh], axis=1)
            big("a_w_in", l, ("mix", l), mm_tn(rec["xn"], dproj, N_DEV, name=f"a{l}_win"),
                a_w_in, m_a_w_in, v_a_w_in)
            dxn = mm_nt(dproj, W[("in", l)], name=f"a{l}_din", gb=4, dep=take())
            dh, dhb, dga = rms_bwd(dxn, rec["h"], a_norm_full[l], dh1, name=f"a{l}_dnorm")
            small_g[("a_norm", l)] = dga
        else:
            j = l - n_a
            do = mm_nt(dhb, W[("o", j)], name=f"b{j}_do")
            big("b_w_o", j, ("mix", l), mm_tn(rec["o"], dhb, 1, name=f"b{j}_wo"),
                b_w_o, m_b_w_o, v_b_w_o)
            prev = None if dk is None else (dk, dv, dck, dcq)
            dq, dk, dv, dck, dcq = attn_bwd(rec["q"], k, v, do, rec["o"],
                                            rec["lse"].reshape(H, nk_attn, 1, tk_attn), ccol, prev, pad, tk_attn,
                                            name=f"b{j}_dattn", dep=take())
            dqraw, dqn = hn_bwd(dq, rec["qraw"], b_q_norm[j], q_scale, name=f"b{j}_dqnorm")
            small_g[("b_q_norm", j)] = dqn
            big("b_w_q", j, ("mix", l), mm_tn(rec["xn"], dqraw, 1, name=f"b{j}_wq"),
                b_w_q, m_b_w_q, v_b_w_q)
            dxn = mm_nt(dqraw, W[("q", j)], name=f"b{j}_dq", dep=take())
            dh, dhb, dgb = rms_bwd(dxn, rec["h"], b_norm[j], dh1, name=f"b{j}_dnorm")
            small_g[("b_norm", j)] = dgb
            if j == 0:
                dkraw, dkn = hn_bwd(dk, shared["kv"], k_norm, 1.0, name="kv_dknorm")
                dkv = jnp.concatenate([dkraw, dv.astype(BF)], axis=1)
                dc_full = _pad_cols(jnp.transpose(dck.reshape(H, T) + dcq.reshape(H, T)), LANES)
                dz, dbf = fgate_bwd(dc_full, shared["logits"], bf_pad, pad, name="f_dgate")
                big("w_kv", 0, ("mix", l), mm_tn(shared["xnk"], dkv, N_DEV, name="kv_wkv"),
                    w_kv[None], m_w_kv[None], v_w_kv[None])
                dwf_t = mm_tn(dz, shared["xnk"], 1, name="f_wf", out_dtype=F32, tn_target=1024,
                              dep=take())[0, :H]
                dxn_f = mm_nt(dz, wf_pad, name="f_dxn")
                dxnk = mm_nt(dkv, W[("kv", 0)], add=dxn_f, name="kv_dxn", gb=4)
                dh, dhb, dgkv = rms_bwd(dxnk, shared["h"], kv_norm, dh, name="kv_dnorm")
    land(None, dh)

    grad_x = dh[lead:][None]

    row8 = lambda a: _pad_rows(_pad_cols(a, D), 8)
    stack = lambda key, n: jnp.concatenate([small_g[(key, i)] for i in range(n)], axis=0)
    g_sharded = jnp.concatenate([dh[pad:lead], row8(stack("a_norm", n_a)), row8(stack("a_conv", n_a)), dwf_t], axis=0)
    g_repl = jnp.concatenate([row8(jnp.concatenate([dgkv, stack("b_norm", n_b)], axis=0)),
                              row8(stack("ffn_norm", depth)),
                              row8(jnp.concatenate([_pad_cols(dkn, D), _pad_cols(stack("b_q_norm", n_b), D),
                                                    _pad_cols(dbf[:, :H], D)], axis=0))], axis=0)
    n_sh = g_sharded.shape[0]
    gathered = all_gather(jnp.concatenate([g_sharded, g_repl], axis=0), name="ag_small_grads",
                          dep=upd["a_w_in"][0])
    parts_sh = lax.dynamic_slice_in_dim(gathered[:, :n_sh], my * Ds, Ds, axis=2)
    parts_rp = gathered[:, n_sh:]

    def pack_sh(t_meta, t_an, t_ac, t_wf):
        return jnp.concatenate([t_meta, _pad_rows(t_an, 8), _pad_rows(t_ac.reshape(n_a * 3, Ds), 8),
                                jnp.transpose(t_wf)], axis=0)

    def pack_rp(t_kv, t_bn, t_fn, t_kn, t_qn, t_bf):
        return jnp.concatenate([row8(jnp.concatenate([t_kv.reshape(1, D), t_bn], axis=0)), row8(t_fn),
                                row8(jnp.concatenate([_pad_cols(t_kn.reshape(1, -1), D), _pad_cols(t_qn, D),
                                                      _pad_cols(t_bf.reshape(1, -1), D)], axis=0))], axis=0)

    res_sh = adamw(parts_sh, pack_sh(meta, a_norm, a_conv, w_f), pack_sh(m_meta, m_a_norm, m_a_conv, m_w_f),
                   pack_sh(v_meta, v_a_norm, v_a_conv, v_w_f), name="small_sharded_adamw")
    res_rp = adamw(parts_rp, pack_rp(kv_norm, b_norm, ffn_norm, k_norm, b_q_norm, b_f),
                   pack_rp(m_kv_norm, m_b_norm, m_ffn_norm, m_k_norm, m_b_q_norm, m_b_f),
                   pack_rp(v_kv_norm, v_b_norm, v_ffn_norm, v_k_norm, v_b_q_norm, v_b_f), name="small_repl_adamw")

    def unpack(kind):
        sh, rp = res_sh[kind], res_rp[kind]
        out = {
            "meta": sh[0:n_meta],
            "a_norm": sh[r_an:r_an + n_a],
            "a_conv": sh[r_ac:r_ac + 3 * n_a].reshape(n_a, 3, Ds),
            "w_f": jnp.transpose(sh[r_wf:r_wf + H]),
            "kv_norm": rp[0],
            "b_norm": rp[1:1 + n_b],
            "ffn_norm": rp[8:8 + depth],
            "k_norm": rp[16, :HEAD_DIM],
            "b_q_norm": rp[17:17 + n_b, :HEAD_DIM],
            "b_f": rp[17 + n_b, :H],
        }
        for name, like in (("a_w_in", a_w_in), ("a_w_out", a_w_out), ("b_w_q", b_w_q), ("b_w_o", b_w_o),
                           ("ffn_w_gu", ffn_w_gu), ("ffn_w_down", ffn_w_down)):
            out[name] = upd[name][kind].reshape(like.shape)
        out["w_kv"] = upd["w_kv"][kind].reshape(w_kv.shape)
        return out

    order = ["meta", "a_norm", "a_w_in", "a_conv", "a_w_out", "kv_norm", "w_kv", "k_norm", "w_f", "b_f",
             "b_norm", "b_w_q", "b_q_norm", "b_w_o", "ffn_norm", "ffn_w_gu", "ffn_w_down"]
    outs = [loss, grad_x]
    for kind in range(4):
        vals = unpack(kind)
        outs += [vals[n] for n in order]
    return tuple(outs)
```

```python
import functools
import math

import jax
import jax.numpy as jnp
from jax import lax
from jax.experimental import pallas as pl
from jax.experimental.pallas import tpu as pltpu

N_DEV = 8
MESH_AXES = ("x", "y", "c")
EPS = 1e-6
NEG = -1e30
HEAD_DIM = 128
BLOCK = 128
LANES = 128
V7X_VMEM_LIMIT = 56 * 1024 * 1024

ADAM_LR = 0.001
ADAM_B1 = 0.9
ADAM_B2 = 0.999
ADAM_EPS = 1e-08
ADAM_WD = 0.01
ADAM_STEP = 10

BF = jnp.bfloat16
F32 = jnp.float32


def _tile(n, target, mult):
    best = None
    for t in range(mult, min(n, target) + 1, mult):
        if n % t == 0:
            best = t
    return n if best is None else best


def _params(*sem):
    return pltpu.CompilerParams(dimension_semantics=sem, vmem_limit_bytes=V7X_VMEM_LIMIT)


def _with_dep(body, in_specs, args, dep):
    if dep is None:
        return body, list(in_specs), list(args)
    n_in = len(args)

    def body_dep(*refs):
        body(*refs[:n_in], *refs[n_in + 1:])

    return body_dep, list(in_specs) + [pl.BlockSpec(memory_space=pl.ANY)], list(args) + [dep]


def mm_nn(a, w, *, name, add=None, dep=None, out_dtype=F32, tm_target=1056, tn_target=1024, tk_target=2048,
          resident=False):
    M, K = a.shape
    G, K2, n = w.shape
    assert K == K2
    tm = _tile(M, tm_target, 16)
    tn = _tile(n, tn_target, LANES)
    tk = K if resident else _tile(K, tk_target, LANES)
    nj, nk = n // tn, K // tk
    has_add = add is not None
    if resident:
        grid, sem = (G * nj, M // tm), ("parallel", "parallel")
        order = lambda f: (lambda j, i: f(i, j, 0))
        w_mode = dict(pipeline_mode=pl.Buffered(1))
    else:
        grid, sem = (M // tm, G * nj, nk), ("parallel", "parallel", "arbitrary")
        order = lambda f: f
        w_mode = {}

    def body(*refs):
        if has_add:
            a_ref, w_ref, add_ref, o_ref = refs[:4]
        else:
            a_ref, w_ref, o_ref = refs[:3]
            add_ref = None

        def finish(r):
            if has_add:
                r = r + add_ref[...]
            o_ref[...] = r.astype(out_dtype)

        part = jnp.dot(a_ref[...], w_ref[...], preferred_element_type=F32)
        if nk == 1:
            finish(part)
        else:
            acc_ref = refs[-1]
            k = pl.program_id(2)

            @pl.when(k == 0)
            def _():
                acc_ref[...] = part

            @pl.when(k > 0)
            def _():
                acc_ref[...] += part

            @pl.when(k == nk - 1)
            def _():
                finish(acc_ref[...])

    in_specs = [
        pl.BlockSpec((tm, tk), order(lambda i, j, k: (i, k))),
        pl.BlockSpec((None, tk, tn), order(lambda i, j, k: (j // nj, k, j % nj)), **w_mode),
    ]
    args = [a, w]
    if has_add:
        in_specs.append(pl.BlockSpec((tm, tn), order(lambda i, j, k: (i, j))))
        args.append(add)
    body, in_specs, args = _with_dep(body, in_specs, args, dep)
    return pl.pallas_call(
        body,
        out_shape=jax.ShapeDtypeStruct((M, G * n), out_dtype),
        grid=grid,
        in_specs=in_specs,
        out_specs=pl.BlockSpec((tm, tn), order(lambda i, j, k: (i, j))),
        scratch_shapes=[pltpu.VMEM((tm, tn), F32)] if nk > 1 else [],
        compiler_params=_params(*sem),
        name=name,
    )(*args)


def mm_swiglu(xn, wgu, *, name, dep=None, save_dtype=BF, tm_target=528):
    M, K = xn.shape
    G, _, n = wgu.shape
    half = G // 2
    tm = _tile(M, tm_target, 16)
    tn = _tile(n, 1408, LANES)
    nj = n // tn
    Fh = half * n

    def body(a_ref, wg_ref, wu_ref, act_ref, silu_ref, udsilu_ref):
        a = a_ref[...]
        g = jnp.dot(a, wg_ref[...], preferred_element_type=F32)
        sig = jax.nn.sigmoid(g)
        silu = g * sig
        silu_ref[...] = silu.astype(save_dtype)
        dsilu = sig * (1.0 + g * (1.0 - sig))
        u = jnp.dot(a, wu_ref[...], preferred_element_type=F32)
        udsilu_ref[...] = (u * dsilu).astype(save_dtype)
        act_ref[...] = (silu * u).astype(BF)

    out_block = pl.BlockSpec((tm, tn), lambda j, i: (i, j))
    once = pl.Buffered(1)
    body, in_specs, args = _with_dep(body, [
        pl.BlockSpec((tm, K), lambda j, i: (i, 0)),
        pl.BlockSpec((None, K, tn), lambda j, i: (j // nj, 0, j % nj), pipeline_mode=once),
        pl.BlockSpec((None, K, tn), lambda j, i: (half + j // nj, 0, j % nj), pipeline_mode=once),
    ], [xn, wgu, wgu], dep)
    return pl.pallas_call(
        body,
        out_shape=(jax.ShapeDtypeStruct((M, Fh), BF),
                   jax.ShapeDtypeStruct((M, Fh), save_dtype),
                   jax.ShapeDtypeStruct((M, Fh), save_dtype)),
        grid=(half * nj, M // tm),
        in_specs=in_specs,
        out_specs=(out_block, out_block, out_block),
        compiler_params=_params("parallel", "parallel"),
        name=name,
    )(*args)


def mm_nt(dy, w, *, name, add=None, dep=None, out_dtype=F32, tm_target=1056, tko_target=1024, tc_target=2048,
          gb=1):
    if dy.ndim == 2:
        dy = dy.reshape(1, *dy.shape)
    P, M, Np = dy.shape
    G, K, n = w.shape
    assert P * Np == G * n
    tm = _tile(M, tm_target, 16)
    tko = _tile(K, tko_target, LANES)
    tc = _tile(n, tc_target, LANES)
    nc = n // tc
    gb = gb if nc == 1 else 1
    assert G % gb == 0 and Np % (gb * tc) == 0
    steps = (G // gb) * nc
    per_part = Np // (gb * tc)
    has_add = add is not None

    def body(*refs):
        if has_add:
            dy_ref, w_ref, add_ref, o_ref = refs[:4]
        else:
            dy_ref, w_ref, o_ref = refs[:3]
            add_ref = None

        def finish(r):
            if has_add:
                r = r + add_ref[...]
            o_ref[...] = r.astype(out_dtype)

        part = None
        for g in range(gb):
            term = lax.dot_general(dy_ref[:, g * tc:(g + 1) * tc], w_ref[g], (((1,), (1,)), ((), ())),
                                   preferred_element_type=F32)
            part = term if part is None else part + term
        if steps == 1:
            finish(part)
        else:
            acc_ref = refs[-1]
            s = pl.program_id(2)

            @pl.when(s == 0)
            def _():
                acc_ref[...] = part

            @pl.when(s > 0)
            def _():
                acc_ref[...] += part

            @pl.when(s == steps - 1)
            def _():
                finish(acc_ref[...])

    in_specs = [
        pl.BlockSpec((None, tm, gb * tc), lambda i, o, s: (s // per_part, i, s % per_part)),
        pl.BlockSpec((gb, tko, tc), lambda i, o, s: (s // nc, o, s % nc)),
    ]
    args = [dy, w]
    if has_add:
        in_specs.append(pl.BlockSpec((tm, tko), lambda i, o, s: (i, o)))
        args.append(add)
    body, in_specs, args = _with_dep(body, in_specs, args, dep)
    return pl.pallas_call(
        body,
        out_shape=jax.ShapeDtypeStruct((M, K), out_dtype),
        grid=(M // tm, K // tko, steps),
        in_specs=in_specs,
        out_specs=pl.BlockSpec((tm, tko), lambda i, o, s: (i, o)),
        scratch_shapes=[pltpu.VMEM((tm, tko), F32)] if steps > 1 else [],
        compiler_params=_params("parallel", "parallel", "arbitrary"),
        name=name,
    )(*args)


def mm_nt_dswiglu(dh, w_down, g_s, u_s, *, name, tm_target=1056, tf_target=512):
    M, D = dh.shape
    _, Fh, D2 = w_down.shape
    assert D == D2
    tm = _tile(M, tm_target, 16)
    tf = _tile(Fh, tf_target, LANES)

    def body(dh_ref, w_ref, silu_ref, udsilu_ref, dgu_ref):
        dact = lax.dot_general(dh_ref[...], w_ref[...], (((1,), (1,)), ((), ())),
                               preferred_element_type=F32)
        dgu_ref[1] = (dact * silu_ref[...].astype(F32)).astype(BF)
        dgu_ref[0] = (dact * udsilu_ref[...].astype(F32)).astype(BF)

    blk = pl.BlockSpec((tm, tf), lambda i, f: (i, f))
    return pl.pallas_call(
        body,
        out_shape=jax.ShapeDtypeStruct((2, M, Fh), BF),
        grid=(M // tm, Fh // tf),
        in_specs=[
            pl.BlockSpec((tm, D), lambda i, f: (i, 0)),
            pl.BlockSpec((None, tf, D), lambda i, f: (0, f, 0)),
            blk, blk,
        ],
        out_specs=pl.BlockSpec((2, tm, tf), lambda i, f: (0, i, f)),
        compiler_params=_params("parallel", "parallel"),
        name=name,
    )(dh, w_down, g_s, u_s)


def mm_tn(a, dy, groups, *, name, dep=None, out_dtype=BF, tk_target=512, tn_target=1408):
    M, K = a.shape
    if dy.ndim == 2:
        dy = dy.reshape(1, *dy.shape)
    P, M2, Np = dy.shape
    N = P * Np
    assert M == M2 and N % groups == 0
    n = N // groups
    tk = _tile(K, tk_target, LANES)
    tn = _tile(n, tn_target, LANES)
    nj = n // tn
    assert Np % tn == 0
    per_part = Np // tn

    def body(a_ref, dy_ref, o_ref):
        o_ref[...] = lax.dot_general(a_ref[...], dy_ref[...], (((0,), (0,)), ((), ())),
                                     preferred_element_type=F32).astype(out_dtype)

    body, in_specs, args = _with_dep(body, [
        pl.BlockSpec((M, tk), lambda i, j: (0, i)),
        pl.BlockSpec((None, M, tn), lambda i, j: (j // per_part, 0, j % per_part)),
    ], [a, dy], dep)
    return pl.pallas_call(
        body,
        out_shape=jax.ShapeDtypeStruct((groups, K, n), out_dtype),
        grid=(K // tk, groups * nj),
        in_specs=in_specs,
        out_specs=pl.BlockSpec((None, tk, tn), lambda i, j: (j // nj, i, j % nj)),
        compiler_params=_params("parallel", "parallel"),
        name=name,
    )(*args)


def rms_fwd(h, g, *, name, dep=None):
    T, D = h.shape
    tm = _tile(T, 528, 16)

    def body(h_ref, g_ref, o_ref):
        x = h_ref[...]
        r = lax.rsqrt(jnp.mean(x * x, axis=-1, keepdims=True) + EPS)
        o_ref[...] = ((x * r) * g_ref[...]).astype(BF)

    body, in_specs, args = _with_dep(
        body, [pl.BlockSpec((tm, D), lambda i: (i, 0)), pl.BlockSpec((1, D), lambda i: (0, 0))],
        [h, g.reshape(1, D)], dep)
    return pl.pallas_call(
        body,
        out_shape=jax.ShapeDtypeStruct((T, D), BF),
        grid=(T // tm,),
        in_specs=in_specs,
        out_specs=pl.BlockSpec((tm, D), lambda i: (i, 0)),
        compiler_params=_params("parallel"),
        name=name,
    )(*args)


def rms_bwd(dxn, h, g, add, *, name):
    T, D = h.shape
    tm = _tile(T, 384, 16)

    def body(dxn_ref, h_ref, g_ref, add_ref, dh_ref, dhb_ref, dg_ref):
        x = h_ref[...]
        dy = dxn_ref[...]
        r = lax.rsqrt(jnp.mean(x * x, axis=-1, keepdims=True) + EPS)
        xhat = x * r
        part = jnp.sum(dy * xhat, axis=0, keepdims=True)

        @pl.when(pl.program_id(0) == 0)
        def _():
            dg_ref[...] = part

        @pl.when(pl.program_id(0) > 0)
        def _():
            dg_ref[...] += part

        dxh = dy * g_ref[...]
        dh = add_ref[...] + r * (dxh - xhat * jnp.mean(dxh * xhat, axis=-1, keepdims=True))
        dh_ref[...] = dh
        dhb_ref[...] = dh.astype(BF)

    row = pl.BlockSpec((tm, D), lambda i: (i, 0))
    vec = pl.BlockSpec((1, D), lambda i: (0, 0))
    return pl.pallas_call(
        body,
        out_shape=(jax.ShapeDtypeStruct((T, D), F32), jax.ShapeDtypeStruct((T, D), BF),
                   jax.ShapeDtypeStruct((1, D), F32)),
        grid=(T // tm,),
        in_specs=[row, row, vec, row],
        out_specs=(row, row, vec),
        compiler_params=_params("arbitrary"),
        name=name,
    )(dxn, h, g.reshape(1, D), add)


def _head_norm(x, gain):
    r = lax.rsqrt(jnp.mean(x * x, axis=-1, keepdims=True) + EPS)
    return (x * r) * gain


def hn_fwd(qraw, gain, out_scale, *, name):
    T, D = qraw.shape
    H = D // HEAD_DIM
    tm = _tile(T, 528, 16)

    def body(q_ref, g_ref, o_ref):
        gain_v = g_ref[...]
        for hd in range(H):
            sl = slice(hd * HEAD_DIM, (hd + 1) * HEAD_DIM)
            o_ref[:, sl] = (_head_norm(q_ref[:, sl], gain_v) * out_scale).astype(BF)

    return pl.pallas_call(
        body,
        out_shape=jax.ShapeDtypeStruct((T, D), BF),
        grid=(T // tm,),
        in_specs=[pl.BlockSpec((tm, D), lambda i: (i, 0)),
                  pl.BlockSpec((1, HEAD_DIM), lambda i: (0, 0))],
        out_specs=pl.BlockSpec((tm, D), lambda i: (i, 0)),
        compiler_params=_params("parallel"),
        name=name,
    )(qraw, gain.reshape(1, HEAD_DIM))


def kv_post(kv, gain, *, name):
    T, D2 = kv.shape
    D = D2 // 2
    H = D // HEAD_DIM
    tm = _tile(T, 528, 16)

    def body(k_ref, v_ref, g_ref, ko_ref, vo_ref):
        gain_v = g_ref[...]
        for hd in range(H):
            sl = slice(hd * HEAD_DIM, (hd + 1) * HEAD_DIM)
            ko_ref[:, sl] = _head_norm(k_ref[:, sl], gain_v).astype(BF)
        vo_ref[...] = v_ref[...].astype(BF)

    blk = pl.BlockSpec((tm, D), lambda i: (i, 0))
    return pl.pallas_call(
        body,
        out_shape=(jax.ShapeDtypeStruct((T, D), BF), jax.ShapeDtypeStruct((T, D), BF)),
        grid=(T // tm,),
        in_specs=[blk, pl.BlockSpec((tm, D), lambda i: (i, 1)),
                  pl.BlockSpec((1, HEAD_DIM), lambda i: (0, 0))],
        out_specs=(blk, blk),
        compiler_params=_params("parallel"),
        name=name,
    )(kv, kv, gain.reshape(1, HEAD_DIM))


def hn_bwd(dq, qraw, gain, out_scale, *, name):
    T, D = dq.shape
    H = D // HEAD_DIM
    tm = _tile(T, 384, 16)

    def body(dq_ref, q_ref, g_ref, o_ref, dg_ref):
        gain_v = g_ref[...]
        part = jnp.zeros((1, HEAD_DIM), F32)
        for hd in range(H):
            sl = slice(hd * HEAD_DIM, (hd + 1) * HEAD_DIM)
            x = q_ref[:, sl]
            dy = dq_ref[:, sl] * out_scale
            r = lax.rsqrt(jnp.mean(x * x, axis=-1, keepdims=True) + EPS)
            xhat = x * r
            part = part + jnp.sum(dy * xhat, axis=0, keepdims=True)
            dxh = dy * gain_v
            o_ref[:, sl] = (r * (dxh - xhat * jnp.mean(dxh * xhat, axis=-1, keepdims=True))).astype(BF)

        @pl.when(pl.program_id(0) == 0)
        def _():
            dg_ref[...] = part

        @pl.when(pl.program_id(0) > 0)
        def _():
            dg_ref[...] += part

    blk = pl.BlockSpec((tm, D), lambda i: (i, 0))
    vec = pl.BlockSpec((1, HEAD_DIM), lambda i: (0, 0))
    return pl.pallas_call(
        body,
        out_shape=(jax.ShapeDtypeStruct((T, D), BF), jax.ShapeDtypeStruct((1, HEAD_DIM), F32)),
        grid=(T // tm,),
        in_specs=[blk, blk, vec],
        out_specs=(blk, vec),
        compiler_params=_params("arbitrary"),
        name=name,
    )(dq, qraw, gain.reshape(1, HEAD_DIM))


def _shift_down(cur, above, k, rowc):
    out = pltpu.roll(cur, k, 0)
    for i in range(k):
        out = jnp.where(rowc == i, above[8 - k + i:8 - k + i + 1], out)
    return out


def _shift_up(cur, below, k, rowc):
    R = cur.shape[0]
    out = pltpu.roll(cur, R - k, 0)
    for i in range(k):
        out = jnp.where(rowc == R - k + i, below[i:i + 1], out)
    return out


def _conv3(u, u_above, wv, rowc):
    u1 = _shift_down(u, u_above, 1, rowc)
    u2 = _shift_down(u, u_above, 2, rowc)
    return wv[0:1] * u2 + wv[1:2] * u1 + wv[2:3] * u, u1, u2


def conv_fwd(proj, w, *, name):
    T, D3 = proj.shape
    D = D3 // 3
    tc = LANES if D % LANES == 0 else D
    nb = D // tc
    R = _tile(T, 264, 8)

    def body(b_ref, c_ref, h_ref, w_ref, y_ref):
        rowc = lax.broadcasted_iota(jnp.int32, (R, 1), 0)
        wv = w_ref[...]
        for r0 in range(0, T, R):
            rows = slice(r0, r0 + R)
            u = c_ref[rows, :] * h_ref[rows, :]
            if r0 == 0:
                above = jnp.zeros((8, tc), F32)
            else:
                above = c_ref[r0 - 8:r0, :] * h_ref[r0 - 8:r0, :]
            conv, _, _ = _conv3(u, above, wv, rowc)
            y_ref[rows, :] = (b_ref[rows, :] * conv).astype(BF)

    return pl.pallas_call(
        body,
        out_shape=jax.ShapeDtypeStruct((T, D), BF),
        grid=(nb,),
        in_specs=[
            pl.BlockSpec((T, tc), lambda j: (0, j)),
            pl.BlockSpec((T, tc), lambda j: (0, nb + j)),
            pl.BlockSpec((T, tc), lambda j: (0, 2 * nb + j)),
            pl.BlockSpec((3, tc), lambda j: (0, j)),
        ],
        out_specs=pl.BlockSpec((T, tc), lambda j: (0, j)),
        compiler_params=_params("parallel"),
        name=name,
    )(proj, proj, proj, w)


def conv_bwd(dy, proj, w, *, name, dep=None):
    T, D = dy.shape
    tc = LANES if D % LANES == 0 else D
    nb = D // tc
    R = _tile(T, 264, 8)

    def body(dy_ref, b_ref, c_ref, h_ref, w_ref, db_ref, dc_ref, dh_ref, dw_ref):
        rowc = lax.broadcasted_iota(jnp.int32, (R, 1), 0)
        wv = w_ref[...]
        dw = [jnp.zeros((1, tc), F32) for _ in range(3)]
        for r0 in range(0, T, R):
            rows = slice(r0, r0 + R)
            c = c_ref[rows, :]
            hh = h_ref[rows, :]
            u = c * hh
            if r0 == 0:
                above = jnp.zeros((8, tc), F32)
            else:
                above = c_ref[r0 - 8:r0, :] * h_ref[r0 - 8:r0, :]
            conv, u1, u2 = _conv3(u, above, wv, rowc)
            dyv = dy_ref[rows, :]
            db_ref[rows, :] = (dyv * conv).astype(BF)
            dconv = dyv * b_ref[rows, :]
            if r0 + R == T:
                below = jnp.zeros((8, tc), F32)
            else:
                below = dy_ref[r0 + R:r0 + R + 8, :] * b_ref[r0 + R:r0 + R + 8, :]
            dw[0] = dw[0] + jnp.sum(dconv * u2, axis=0, keepdims=True)
            dw[1] = dw[1] + jnp.sum(dconv * u1, axis=0, keepdims=True)
            dw[2] = dw[2] + jnp.sum(dconv * u, axis=0, keepdims=True)
            du = (wv[2:3] * dconv + wv[1:2] * _shift_up(dconv, below, 1, rowc)
                  + wv[0:1] * _shift_up(dconv, below, 2, rowc))
            dc_ref[rows, :] = (du * hh).astype(BF)
            dh_ref[rows, :] = (du * c).astype(BF)
        for i in range(3):
            dw_ref[i:i + 1, :] = dw[i]

    strip = pl.BlockSpec((T, tc), lambda j: (0, j))
    wblk = pl.BlockSpec((3, tc), lambda j: (0, j))
    out = jax.ShapeDtypeStruct((T, D), BF)
    body, in_specs, args = _with_dep(body, [
        strip,
        pl.BlockSpec((T, tc), lambda j: (0, j)),
        pl.BlockSpec((T, tc), lambda j: (0, nb + j)),
        pl.BlockSpec((T, tc), lambda j: (0, 2 * nb + j)),
        wblk,
    ], [dy, proj, proj, proj, w], dep)
    return pl.pallas_call(
        body,
        out_shape=(out, out, out, jax.ShapeDtypeStruct((3, D), F32)),
        grid=(nb,),
        in_specs=in_specs,
        out_specs=(strip, strip, strip, wblk),
        compiler_params=_params("parallel"),
        name=name,
    )(*args)


def _log_sigmoid(z):
    return jnp.minimum(z, 0.0) - jnp.log(1.0 + jnp.exp(-jnp.abs(z)))


def fgate_fwd(logits, bias, pad, *, name):
    T, W = logits.shape
    cb = _tile(T, 128, 8)
    nblk = T // cb

    def body(z_ref, b_ref, c_ref, lf_ref):
        row = lax.broadcasted_iota(jnp.int32, (T, 1), 0)
        lf_ref[...] = jnp.where(row >= pad, _log_sigmoid(z_ref[...] + b_ref[...]), 0.0)
        ri = lax.broadcasted_iota(jnp.int32, (cb, cb), 0)
        ci = lax.broadcasted_iota(jnp.int32, (cb, cb), 1)
        tri = (ci <= ri).astype(F32)

        def step(i, carry):
            rows = pl.ds(pl.multiple_of(i * cb, cb), cb)
            blk = lf_ref[rows, :]
            c_ref[rows, :] = carry + jnp.dot(tri, blk, precision=lax.Precision.HIGHEST,
                                             preferred_element_type=F32)
            return carry + jnp.sum(blk, axis=0, keepdims=True)

        lax.fori_loop(0, nblk, step, jnp.zeros((1, W), F32))

    return pl.pallas_call(
        body,
        out_shape=jax.ShapeDtypeStruct((T, W), F32),
        in_specs=[pl.BlockSpec(memory_space=pltpu.VMEM), pl.BlockSpec(memory_space=pltpu.VMEM)],
        out_specs=pl.BlockSpec(memory_space=pltpu.VMEM),
        scratch_shapes=[pltpu.VMEM((T, W), F32)],
        compiler_params=pltpu.CompilerParams(vmem_limit_bytes=V7X_VMEM_LIMIT),
        name=name,
    )(logits, bias)


def fgate_bwd(dc, logits, bias, pad, *, name):
    T, W = logits.shape
    cb = _tile(T, 128, 8)
    nblk = T // cb

    def body(dc_ref, z_ref, b_ref, dz_ref, db_ref, rs_ref):
        ri = lax.broadcasted_iota(jnp.int32, (cb, cb), 0)
        ci = lax.broadcasted_iota(jnp.int32, (cb, cb), 1)
        triu = (ci >= ri).astype(F32)

        def step(i, carry):
            rows = pl.ds(pl.multiple_of((nblk - 1 - i) * cb, cb), cb)
            blk = dc_ref[rows, :]
            rs_ref[rows, :] = carry + jnp.dot(triu, blk, precision=lax.Precision.HIGHEST,
                                              preferred_element_type=F32)
            return carry + jnp.sum(blk, axis=0, keepdims=True)

        lax.fori_loop(0, nblk, step, jnp.zeros((1, W), F32))
        row = lax.broadcasted_iota(jnp.int32, (T, 1), 0)
        z = z_ref[...] + b_ref[...]
        dz = jnp.where(row >= pad, rs_ref[...] * jax.nn.sigmoid(-z), 0.0)
        dz_ref[...] = dz.astype(BF)
        db_ref[...] = jnp.sum(dz, axis=0, keepdims=True)

    vm = pl.BlockSpec(memory_space=pltpu.VMEM)
    return pl.pallas_call(
        body,
        out_shape=(jax.ShapeDtypeStruct((T, W), BF), jax.ShapeDtypeStruct((1, W), F32)),
        in_specs=[vm, vm, vm],
        out_specs=(vm, vm),
        scratch_shapes=[pltpu.VMEM((T, W), F32)],
        compiler_params=pltpu.CompilerParams(vmem_limit_bytes=V7X_VMEM_LIMIT),
        name=name,
    )(dc, logits, bias)


def _scores(qb, kb, ck):
    return lax.dot_general(qb, kb, (((1,), (1,)), ((), ())), preferred_element_type=F32) - ck


def _causal(s, row, col, pad):
    return jnp.where((col <= row) & (col >= pad), s, NEG)


def attn_fwd(q, k, v, crow, pad, *, name):
    T, D = q.shape
    H = D // HEAD_DIM
    nk, tk = crow.shape[1], crow.shape[3]
    tq = tk
    nq = T // tq

    hp = 2 if H % 2 == 0 else 1
    wide = hp * HEAD_DIM

    def body(q_ref, k_ref, v_ref, cr_ref, o_ref, lse_ref):
        qi = pl.program_id(1)
        row = qi * tq + lax.broadcasted_iota(jnp.int32, (tq, 1), 0)
        heads = [slice(a * HEAD_DIM, (a + 1) * HEAD_DIM) for a in range(hp)]
        qbs = [q_ref[:, sl] for sl in heads]

        def step(kc, carry, masked):
            rows = pl.ds(pl.multiple_of(kc * tk, tk), tk)
            out = []
            for a, sl in enumerate(heads):
                m, l, acc = carry[a]
                s = _scores(qbs[a], k_ref[rows, sl], cr_ref[a, kc])
                if masked:
                    s = _causal(s, row, kc * tk + lax.broadcasted_iota(jnp.int32, (1, tk), 1), pad)
                m_new = jnp.maximum(m, jnp.max(s, axis=-1, keepdims=True))
                alpha = jnp.exp(m - m_new)
                p = jnp.exp(s - m_new)
                l = alpha * l + jnp.sum(p, axis=-1, keepdims=True)
                acc = alpha * acc + jnp.dot(p.astype(BF), v_ref[rows, sl], preferred_element_type=F32)
                out.append((m_new, l, acc))
            return tuple(out)

        init = tuple((jnp.full((tq, 1), NEG, F32), jnp.zeros((tq, 1), F32), jnp.zeros((tq, HEAD_DIM), F32))
                     for _ in heads)
        carry = step(0, init, True)
        carry = lax.fori_loop(1, qi, lambda kc, c: step(kc, c, False), carry)
        carry = lax.cond(qi > 0, lambda c: step(qi, c, True), lambda c: c, carry)
        valid = row >= pad
        for a, sl in enumerate(heads):
            m, l, acc = carry[a]
            o_ref[:, sl] = jnp.where(valid, acc / l, 0.0).astype(BF)
            lse_ref[a] = jnp.where(valid, m + jnp.log(l), 0.0)

    return pl.pallas_call(
        body,
        out_shape=(jax.ShapeDtypeStruct((T, D), BF), jax.ShapeDtypeStruct((H, T, 1), F32)),
        grid=(H // hp, nq),
        in_specs=[
            pl.BlockSpec((tq, wide), lambda h, i: (i, h)),
            pl.BlockSpec((T, wide), lambda h, i: (0, h)),
            pl.BlockSpec((T, wide), lambda h, i: (0, h)),
            pl.BlockSpec((hp, nk, 1, tk), lambda h, i: (h, 0, 0, 0)),
        ],
        out_specs=(pl.BlockSpec((tq, wide), lambda h, i: (i, h)),
                   pl.BlockSpec((hp, tq, 1), lambda h, i: (h, i, 0))),
        compiler_params=_params("parallel", "arbitrary"),
        name=name,
    )(q, k, v, crow)


def attn_bwd(q, k, v, do, o, lse, ccol, prev, pad, tk, *, name, dep=None):
    T, D = q.shape
    H = D // HEAD_DIM
    nk = T // tk
    tq, nq = tk, nk
    has_prev = prev is not None
    hp = 2 if H % 2 == 0 else 1
    wide = hp * HEAD_DIM
    heads = [slice(a * HEAD_DIM, (a + 1) * HEAD_DIM) for a in range(hp)]
    nt = (((1,), (1,)), ((), ()))

    def body(*refs):
        q_ref, k_ref, v_ref, do_ref, o_ref, lse_ref, cc_ref = refs[:7]
        refs = refs[7:]
        if has_prev:
            pk_ref, pv_ref, pc_ref, pq_ref = refs[:4]
            refs = refs[4:]
        dq_ref, dk_ref, dv_ref, dck_ref, dcq_ref, delta_ref = refs
        kc = pl.program_id(1)

        @pl.when(kc == 0)
        def _():
            dq_ref[...] = jnp.zeros_like(dq_ref)
            dcq_ref[...] = pq_ref[...] if has_prev else jnp.zeros_like(dcq_ref)
            ones = jnp.ones((8, HEAD_DIM), F32)
            for a, sl in enumerate(heads):
                for i in range(nq):
                    rows = slice(i * tq, (i + 1) * tq)
                    prod = do_ref[rows, sl].astype(BF).astype(F32) * o_ref[rows, sl].astype(F32)
                    delta_ref[a, i] = lax.dot_general(ones, prod, nt, precision=lax.Precision.HIGHEST,
                                                      preferred_element_type=F32)[0:1]

        kbs = [k_ref[:, sl] for sl in heads]
        vbs = [v_ref[:, sl] for sl in heads]
        cks = [jnp.broadcast_to(cc_ref[a], (tk, tq)) for a in range(hp)]
        krow = kc * tk + lax.broadcasted_iota(jnp.int32, (tk, 1), 0)

        def step(qi, carry, masked):
            rows = pl.ds(pl.multiple_of(qi * tq, tq), tq)
            out = []
            for a, sl in enumerate(heads):
                dk, dv, dck = carry[a]
                qb = q_ref[rows, sl]
                dob = do_ref[rows, sl].astype(BF)
                s = lax.dot_general(kbs[a], qb, nt, preferred_element_type=F32) - cks[a]
                if masked:
                    qcol = qi * tq + lax.broadcasted_iota(jnp.int32, (1, tq), 1)
                    s = jnp.where((krow <= qcol) & (krow >= pad), s, NEG)
                p = jnp.exp(s - lse_ref[a, qi])
                dp = lax.dot_general(vbs[a], dob, nt, preferred_element_type=F32)
                ds = p * (dp - delta_ref[a, qi])
                dsb = ds.astype(BF)
                dv = dv + jnp.dot(p.astype(BF), dob, preferred_element_type=F32)
                dk = dk + jnp.dot(dsb, qb, preferred_element_type=F32)
                dq_ref[rows, sl] += lax.dot_general(dsb, kbs[a], (((0,), (0,)), ((), ())),
                                                    preferred_element_type=F32)
                dcq_ref[a, qi] += jnp.sum(ds, axis=0, keepdims=True)
                part = ds[:, 0:LANES]
                for j in range(1, tq // LANES):
                    part = part + ds[:, j * LANES:(j + 1) * LANES]
                out.append((dk, dv, dck - part))
            return tuple(out)

        def rest(masked):
            return lambda c: lax.fori_loop(kc + 1, nq, lambda qi, cc: step(qi, cc, masked), c)

        init = tuple((jnp.zeros((tk, HEAD_DIM), F32), jnp.zeros((tk, HEAD_DIM), F32), jnp.zeros((tk, LANES), F32))
                     for _ in heads)
        carry = step(kc, init, True)
        carry = lax.cond(kc == 0, rest(True), rest(False), carry)
        for a, sl in enumerate(heads):
            dk, dv, dck = carry[a]
            dck = jnp.sum(dck, axis=1, keepdims=True)
            if has_prev:
                dk = dk + pk_ref[:, sl]
                dv = dv + pv_ref[:, sl]
                dck = dck + pc_ref[a]
            dk_ref[:, sl] = dk
            dv_ref[:, sl] = dv
            dck_ref[a] = dck

    head_all = pl.BlockSpec((T, wide), lambda h, j: (0, h))
    head_blk = pl.BlockSpec((tk, wide), lambda h, j: (j, h))
    rows_all = pl.BlockSpec((hp, nq, 1, tq), lambda h, j: (h, 0, 0, 0))
    col_blk = pl.BlockSpec((hp, tk, 1), lambda h, j: (h, j, 0))
    in_specs = [head_all, head_blk, head_blk, head_all, head_all, rows_all, col_blk]
    args = [q, k, v, do, o, lse, ccol]
    if has_prev:
        in_specs += [head_blk, head_blk, col_blk, rows_all]
        args += list(prev)
    body, in_specs, args = _with_dep(body, in_specs, args, dep)
    return pl.pallas_call(
        body,
        out_shape=(jax.ShapeDtypeStruct((T, D), F32), jax.ShapeDtypeStruct((T, D), F32),
                   jax.ShapeDtypeStruct((T, D), F32), jax.ShapeDtypeStruct((H, T, 1), F32),
                   jax.ShapeDtypeStruct((H, nq, 1, tq), F32)),
        grid=(H // hp, nk),
        in_specs=in_specs,
        out_specs=(head_all, head_blk, head_blk, col_blk, rows_all),
        scratch_shapes=[pltpu.VMEM((hp, nq, 1, tq), F32)],
        compiler_params=_params("parallel", "arbitrary"),
        name=name,
    )(*args)


def loss_head(h, target, lead, *, name):
    T, D = h.shape
    tm = lead
    assert T % tm == 0 and target.shape[0] % tm == 0
    inv_d = 1.0 / D

    def body(h_ref, t_ref, dh_ref, dhb_ref, loss_ref):
        i = pl.program_id(0)

        @pl.when(i == 0)
        def _():
            dh_ref[...] = jnp.zeros_like(dh_ref)
            dhb_ref[...] = jnp.zeros_like(dhb_ref)
            loss_ref[...] = jnp.zeros_like(loss_ref)

        @pl.when(i > 0)
        def _():
            e = h_ref[...] - t_ref[...]
            dh = e * inv_d
            dh_ref[...] = dh
            dhb_ref[...] = dh.astype(BF)
            loss_ref[...] += 0.5 * inv_d * jnp.sum(e * e)

    return pl.pallas_call(
        body,
        out_shape=(jax.ShapeDtypeStruct((T, D), F32), jax.ShapeDtypeStruct((T, D), BF),
                   jax.ShapeDtypeStruct((8, LANES), F32)),
        grid=(T // tm,),
        in_specs=[pl.BlockSpec((tm, D), lambda i: (i, 0)),
                  pl.BlockSpec((tm, D), lambda i: (jnp.maximum(i - 1, 0), 0))],
        out_specs=(pl.BlockSpec((tm, D), lambda i: (i, 0)), pl.BlockSpec((tm, D), lambda i: (i, 0)),
                   pl.BlockSpec((8, LANES), lambda i: (0, 0))),
        compiler_params=_params("arbitrary"),
        name=name,
    )(h, target)


def adamw(parts, w, m, v, *, name):
    P, R, C = parts.shape
    tr = _tile(R, max(16, (128 * 1024) // C), 16)

    def body(p_ref, w_ref, m_ref, v_ref, g_ref, d_ref, mo_ref, vo_ref):
        g = p_ref[0].astype(F32)
        for i in range(1, P):
            g = g + p_ref[i].astype(F32)
        m_new = ADAM_B1 * m_ref[...] + (1.0 - ADAM_B1) * g
        v_new = ADAM_B2 * v_ref[...] + (1.0 - ADAM_B2) * jnp.square(g)
        m_hat = m_new / (1.0 - ADAM_B1 ** ADAM_STEP)
        v_hat = v_new / (1.0 - ADAM_B2 ** ADAM_STEP)
        g_ref[...] = g
        d_ref[...] = -ADAM_LR * (m_hat / (jnp.sqrt(v_hat) + ADAM_EPS) + ADAM_WD * w_ref[...])
        mo_ref[...] = m_new
        vo_ref[...] = v_new

    blk = pl.BlockSpec((tr, C), lambda i: (i, 0))
    out = jax.ShapeDtypeStruct((R, C), F32)
    return pl.pallas_call(
        body,
        out_shape=(out, out, out, out),
        grid=(R // tr,),
        in_specs=[pl.BlockSpec((P, tr, C), lambda i: (0, i, 0)), blk, blk, blk],
        out_specs=(blk, blk, blk, blk),
        compiler_params=_params("parallel"),
        name=name,
    )(parts, w, m, v)


def _flip(v, bit):
    return 1 - v if bit else v


def all_gather(shard, *, name, dep=None):
    def body(x_ref, out_ref, send_sems, recv_sems, local_sem):
        x, y, c = lax.axis_index("x"), lax.axis_index("y"), lax.axis_index("c")
        me, sibling = (x, y, c), (x, y, 1 - c)
        chips = [(1 - x, y), (x, 1 - y), (1 - x, 1 - y)]

        def block(px, py, pc):
            return out_ref.at[4 * px + 2 * py + pc]

        def copy(k, blk, to, src=None):
            return pltpu.make_async_remote_copy(
                src_ref=block(*blk) if src is None else src,
                dst_ref=block(*blk),
                send_sem=send_sems.at[k],
                recv_sem=recv_sems.at[k],
                device_id=to,
                device_id_type=pl.DeviceIdType.MESH,
            )

        mine = pltpu.make_async_copy(x_ref, block(*me), local_sem)
        mine.start()
        first = [copy(0, me, sibling, src=x_ref)]
        first += [copy(1 + j, me, (*chip, c), src=x_ref) for j, chip in enumerate(chips)]
        for cp in first:
            cp.start()
        passed = [copy(4 + j, (*chip, c), sibling) for j, chip in enumerate(chips)]
        for j, chip in enumerate(chips):
            copy(1 + j, (*chip, c), me).wait_recv()
            passed[j].start()
        copy(0, sibling, me).wait_recv()
        for j, chip in enumerate(chips):
            copy(4 + j, (*chip, 1 - c), me).wait_recv()
        for cp in first + passed:
            cp.wait_send()
        mine.wait()

    body, in_specs, args = _with_dep(body, [pl.BlockSpec(memory_space=pl.ANY)], [shard], dep)
    return pl.pallas_call(
        body,
        out_shape=jax.ShapeDtypeStruct((N_DEV,) + shard.shape, shard.dtype),
        in_specs=in_specs,
        out_specs=pl.BlockSpec(memory_space=pl.ANY),
        scratch_shapes=[pltpu.SemaphoreType.DMA((7,)), pltpu.SemaphoreType.DMA((7,)),
                        pltpu.SemaphoreType.DMA],
        name=name,
    )(*args)


def exchange_slabs(slabs, *, name):
    def body(g_ref, r_ref, send_sems, recv_sems, local_sem):
        x, y, c = lax.axis_index("x"), lax.axis_index("y"), lax.axis_index("c")
        me = 4 * x + 2 * y + c
        mine = pltpu.make_async_copy(g_ref.at[me], r_ref.at[me], local_sem)
        mine.start()
        sends, recvs = [], []
        for k in range(1, N_DEV):
            px, py, pc = _flip(x, (k >> 2) & 1), _flip(y, (k >> 1) & 1), _flip(c, k & 1)
            peer = 4 * px + 2 * py + pc
            sends.append(pltpu.make_async_remote_copy(
                src_ref=g_ref.at[peer], dst_ref=r_ref.at[me],
                send_sem=send_sems.at[k - 1], recv_sem=recv_sems.at[k - 1],
                device_id=(px, py, pc), device_id_type=pl.DeviceIdType.MESH))
            recvs.append(pltpu.make_async_remote_copy(
                src_ref=g_ref.at[peer], dst_ref=r_ref.at[peer],
                send_sem=send_sems.at[k - 1], recv_sem=recv_sems.at[k - 1],
                device_id=(px, py, pc), device_id_type=pl.DeviceIdType.MESH))
        for cp in sends:
            cp.start()
        for cp in recvs:
            cp.wait_recv()
        for cp in sends:
            cp.wait_send()
        mine.wait()

    return pl.pallas_call(
        body,
        out_shape=jax.ShapeDtypeStruct(slabs.shape, slabs.dtype),
        in_specs=[pl.BlockSpec(memory_space=pl.ANY)],
        out_specs=pl.BlockSpec(memory_space=pl.ANY),
        scratch_shapes=[pltpu.SemaphoreType.DMA((7,)), pltpu.SemaphoreType.DMA((7,)),
                        pltpu.SemaphoreType.DMA],
        name=name,
    )(slabs)


def reduce_adamw(slabs, w, m, v, *, name):
    got = exchange_slabs(slabs, name=name + "_xchg")
    return adamw(got, w, m, v, name=name + "_adamw")


_HBM = pl.BlockSpec(memory_space=pltpu.HBM)
_SEM = pl.BlockSpec(memory_space=pltpu.SEMAPHORE)
_ANY = pl.BlockSpec(memory_space=pl.ANY)
_EFFECT = pltpu.SideEffectType.DATAFLOW_SIDE_EFFECTING
_N_FIRST = 4


def _first_copies(land_ref, send_sems, recv_sems):
    x, y, c = lax.axis_index("x"), lax.axis_index("y"), lax.axis_index("c")
    mine = land_ref.at[4 * x + 2 * y + c]
    targets = [(x, y, 1 - c), (1 - x, y, c), (x, 1 - y, c), (1 - x, 1 - y, c)]
    sends, recvs = [], []
    for k, (px, py, pc) in enumerate(targets):
        common = dict(send_sem=send_sems.at[k], recv_sem=recv_sems.at[k], device_id=(px, py, pc),
                      device_id_type=pl.DeviceIdType.MESH)
        sends.append(pltpu.make_async_remote_copy(src_ref=mine, dst_ref=mine, **common))
        theirs = land_ref.at[4 * px + 2 * py + pc]
        recvs.append(pltpu.make_async_remote_copy(src_ref=theirs, dst_ref=theirs, **common))
    return sends, recvs


def _second_copies(land_ref, send_sems, recv_sems):
    x, y, c = lax.axis_index("x"), lax.axis_index("y"), lax.axis_index("c")
    sends, recvs = [], []
    for j, (px, py) in enumerate([(1 - x, y), (x, 1 - y), (1 - x, 1 - y)]):
        common = dict(send_sem=send_sems.at[j], recv_sem=recv_sems.at[j], device_id=(x, y, 1 - c),
                      device_id_type=pl.DeviceIdType.MESH)
        blk = land_ref.at[4 * px + 2 * py + c]
        sends.append(pltpu.make_async_remote_copy(src_ref=blk, dst_ref=blk, **common))
        got = land_ref.at[4 * px + 2 * py + (1 - c)]
        recvs.append(pltpu.make_async_remote_copy(src_ref=got, dst_ref=got, **common))
    return sends, recvs


def gather_start(shard, me, after, *, name):
    R, C = shard.shape
    tr = _tile(R, max(16, (512 * 1024) // C), 16)

    def place_body(me_ref, x_ref, o_ref):
        o_ref[...] = x_ref[...].astype(BF)

    land = pl.pallas_call(
        place_body, name=name + "_own",
        out_shape=jax.ShapeDtypeStruct((N_DEV, R, C), BF),
        grid_spec=pltpu.PrefetchScalarGridSpec(
            num_scalar_prefetch=1,
            grid=(R // tr,),
            in_specs=[pl.BlockSpec((tr, C), lambda i, me_ref: (i, 0))],
            out_specs=pl.BlockSpec((None, tr, C), lambda i, me_ref: (me_ref[0], i, 0)),
        ),
        compiler_params=_params("parallel"),
    )(me.reshape(1).astype(jnp.int32), shard)

    def body(land_ref, after_ref, send_sems, recv_sems, land_thru, token):
        sends, _ = _first_copies(land_ref, send_sems, recv_sems)
        for cp in sends:
            cp.start()
        token[...] = jnp.zeros_like(token)

    send_sems, recv_sems, land_thru, token = pl.pallas_call(
        body, name=name + "_s1",
        out_shape=(pltpu.SemaphoreType.DMA((_N_FIRST,)), pltpu.SemaphoreType.DMA((_N_FIRST,)),
                   pltpu.HBM(land.shape, land.dtype), jax.ShapeDtypeStruct((8, LANES), F32)),
        in_specs=(_HBM, _ANY),
        out_specs=(_SEM, _SEM, _HBM, pl.BlockSpec(memory_space=pltpu.VMEM)),
        input_output_aliases={0: 2},
        compiler_params=pltpu.CompilerParams(has_side_effects=_EFFECT),
    )(pltpu.with_memory_space_constraint(land, pltpu.HBM), after)
    return (send_sems, recv_sems, land_thru), token


def gather_mid(handle, after, *, name):
    send_sems, recv_sems, land_thru = handle

    def body(land_ref, send1, recv1, after_ref, send2, recv2, land_out, token):
        sends, recvs = _first_copies(land_ref, send1, recv1)
        for cp in sends:
            cp.wait_send()
        for cp in recvs:
            cp.wait_recv()
        seconds, _ = _second_copies(land_ref, send2, recv2)
        for cp in seconds:
            cp.start()
        token[...] = jnp.zeros_like(token)

    send2, recv2, land2, token = pl.pallas_call(
        body, name=name + "_s2",
        out_shape=(pltpu.SemaphoreType.DMA((3,)), pltpu.SemaphoreType.DMA((3,)),
                   pltpu.HBM(land_thru.shape, land_thru.dtype), jax.ShapeDtypeStruct((8, LANES), F32)),
        in_specs=(_HBM, _SEM, _SEM, _ANY),
        out_specs=(_SEM, _SEM, _HBM, pl.BlockSpec(memory_space=pltpu.VMEM)),
        input_output_aliases={0: 2},
        compiler_params=pltpu.CompilerParams(has_side_effects=_EFFECT),
    )(land_thru, send_sems, recv_sems, after)
    return (send2, recv2, land2), token


def gather_finish(handle, after, *, name):
    send2, recv2, land2 = handle

    def body(land_ref, send2, recv2, after_ref, got_ref):
        sends, recvs = _second_copies(land_ref, send2, recv2)
        for cp in sends:
            cp.wait_send()
        for cp in recvs:
            cp.wait_recv()

    return pl.pallas_call(
        body, name=name + "_w",
        out_shape=pltpu.HBM(land2.shape, land2.dtype),
        in_specs=(_HBM, _SEM, _SEM, _ANY),
        out_specs=_HBM,
        input_output_aliases={0: 0},
        compiler_params=pltpu.CompilerParams(has_side_effects=_EFFECT),
    )(land2, send2, recv2, after)


def _slab_copies(g_ref, r_ref, send_sems, recv_sems):
    x, y, c = lax.axis_index("x"), lax.axis_index("y"), lax.axis_index("c")
    me = 4 * x + 2 * y + c
    sends, recvs = [], []
    for k in range(1, N_DEV):
        px, py, pc = _flip(x, (k >> 2) & 1), _flip(y, (k >> 1) & 1), _flip(c, k & 1)
        peer = 4 * px + 2 * py + pc
        common = dict(send_sem=send_sems.at[k - 1], recv_sem=recv_sems.at[k - 1], device_id=(px, py, pc),
                      device_id_type=pl.DeviceIdType.MESH)
        sends.append(pltpu.make_async_remote_copy(src_ref=g_ref.at[peer], dst_ref=r_ref.at[me], **common))
        recvs.append(pltpu.make_async_remote_copy(src_ref=g_ref.at[peer], dst_ref=r_ref.at[peer], **common))
    return sends, recvs


def exchange_start(slabs, *, name):
    land = lax.empty(slabs.shape, slabs.dtype)

    def body(g_ref, r_ref, send_sems, recv_sems, g_thru, r_thru, token):
        sends, _ = _slab_copies(g_ref, r_ref, send_sems, recv_sems)
        for cp in sends:
            cp.start()
        token[...] = jnp.zeros_like(token)

    send_sems, recv_sems, g_thru, r_thru, token = pl.pallas_call(
        body, name=name,
        out_shape=(pltpu.SemaphoreType.DMA((N_DEV - 1,)), pltpu.SemaphoreType.DMA((N_DEV - 1,)),
                   pltpu.HBM(slabs.shape, slabs.dtype), pltpu.HBM(slabs.shape, slabs.dtype),
                   jax.ShapeDtypeStruct((8, LANES), F32)),
        in_specs=(_HBM, _HBM),
        out_specs=(_SEM, _SEM, _HBM, _HBM, pl.BlockSpec(memory_space=pltpu.VMEM)),
        input_output_aliases={0: 2, 1: 3},
        compiler_params=pltpu.CompilerParams(has_side_effects=_EFFECT),
    )(pltpu.with_memory_space_constraint(slabs, pltpu.HBM), pltpu.with_memory_space_constraint(land, pltpu.HBM))
    return (send_sems, recv_sems, g_thru, r_thru), token


def exchange_finish(handle, after, *, name):
    send_sems, recv_sems, g_thru, r_thru = handle

    def body(g_ref, r_ref, send_sems, recv_sems, after_ref, g_out, r_out):
        sends, recvs = _slab_copies(g_ref, r_ref, send_sems, recv_sems)
        for cp in sends:
            cp.wait_send()
        for cp in recvs:
            cp.wait_recv()

    return pl.pallas_call(
        body, name=name,
        out_shape=(pltpu.HBM(g_thru.shape, g_thru.dtype), pltpu.HBM(r_thru.shape, r_thru.dtype)),
        in_specs=(_HBM, _HBM, _SEM, _SEM, _ANY),
        out_specs=(_HBM, _HBM),
        input_output_aliases={0: 0, 1: 1},
        compiler_params=pltpu.CompilerParams(has_side_effects=_EFFECT),
    )(g_thru, r_thru, send_sems, recv_sems, after)


def adamw_own(own, got, me, w, m, v, layer, prev, *, name):
    P, R, C = got.shape
    L = w.shape[0]
    tr = _tile(R, max(16, (256 * 1024) // C), 16)

    def body(me_ref, own_ref, p_ref, w_ref, m_ref, v_ref, *rest):
        g_ref, d_ref, mo_ref, vo_ref = rest[-4:]
        mine = own_ref[...].astype(F32)
        g = None
        for i in range(P):
            term = jnp.where(me_ref[0] == i, mine, p_ref[i].astype(F32))
            g = term if g is None else g + term
        m_new = ADAM_B1 * m_ref[...] + (1.0 - ADAM_B1) * g
        v_new = ADAM_B2 * v_ref[...] + (1.0 - ADAM_B2) * jnp.square(g)
        m_hat = m_new / (1.0 - ADAM_B1 ** ADAM_STEP)
        v_hat = v_new / (1.0 - ADAM_B2 ** ADAM_STEP)
        g_ref[...] = g
        d_ref[...] = -ADAM_LR * (m_hat / (jnp.sqrt(v_hat) + ADAM_EPS) + ADAM_WD * w_ref[...])
        mo_ref[...] = m_new
        vo_ref[...] = v_new

    blk = pl.BlockSpec((None, tr, C), lambda i, me_ref: (layer, i, 0))
    out = jax.ShapeDtypeStruct((L, R, C), F32)
    in_specs = [pl.BlockSpec((None, tr, C), lambda i, me_ref: (me_ref[0], i, 0)),
                pl.BlockSpec((P, tr, C), lambda i, me_ref: (0, i, 0)), blk, blk, blk]
    args = [me.reshape(1).astype(jnp.int32), own, got, w, m, v]
    aliases = {}
    if prev is not None:
        in_specs += [pl.BlockSpec(memory_space=pl.ANY)] * 4
        aliases = {len(args) + i: i for i in range(4)}
        args += list(prev)
    return pl.pallas_call(
        body,
        out_shape=(out, out, out, out),
        grid_spec=pltpu.PrefetchScalarGridSpec(
            num_scalar_prefetch=1,
            grid=(R // tr,),
            in_specs=in_specs,
            out_specs=(blk, blk, blk, blk),
        ),
        input_output_aliases=aliases,
        compiler_params=_params("parallel"),
        name=name,
    )(*args)


def _pad_rows(a, rows):
    return jnp.pad(a, ((0, rows - a.shape[0]), (0, 0)))


def _pad_cols(a, cols):
    return jnp.pad(a, ((0, 0), (0, cols - a.shape[1])))


def kernel(x, meta, a_norm, a_w_in, a_conv, a_w_out, kv_norm, w_kv, k_norm, w_f, b_f, b_norm, b_w_q, b_q_norm, b_w_o, ffn_norm, ffn_w_gu, ffn_w_down, loss_target, m_meta, m_a_norm, m_a_w_in, m_a_conv, m_a_w_out, m_kv_norm, m_w_kv, m_k_norm, m_w_f, m_b_f, m_b_norm, m_b_w_q, m_b_q_norm, m_b_w_o, m_ffn_norm, m_ffn_w_gu, m_ffn_w_down, v_meta, v_a_norm, v_a_w_in, v_a_conv, v_a_w_out, v_kv_norm, v_w_kv, v_k_norm, v_w_f, v_b_f, v_b_norm, v_b_w_q, v_b_q_norm, v_b_w_o, v_ffn_norm, v_ffn_w_gu, v_ffn_w_down):
    S, D = x.shape[1], x.shape[2]
    n_meta = meta.shape[0]
    Ds = meta.shape[1]
    H = D // HEAD_DIM
    n_a, n_b = a_w_in.shape[0], b_w_q.shape[0]
    depth = n_a + n_b
    Fs = ffn_w_down.shape[1]
    pad = BLOCK - n_meta
    lead = pad + n_meta
    T = lead + S
    tk_attn = _tile(T, 384, LANES)
    nk_attn = T // tk_attn
    q_scale = 1.0 / math.sqrt(HEAD_DIM)
    my = 4 * lax.axis_index("x") + 2 * lax.axis_index("y") + lax.axis_index("c")

    wf_t = w_f.reshape(H, Ds)
    small = jnp.concatenate([meta, _pad_rows(a_norm, 8), _pad_rows(a_conv.reshape(n_a * 3, Ds), 8), wf_t], axis=0)
    r_an, r_ac, r_wf = n_meta, n_meta + 8, n_meta + 16
    gs = all_gather(small, name="ag_small")
    unshard = lambda blk: jnp.transpose(blk, (1, 0, 2)).reshape(blk.shape[1], D)
    meta_full = unshard(gs[:, 0:n_meta])
    a_norm_full = unshard(gs[:, r_an:r_an + n_a])
    a_conv_full = unshard(gs[:, r_ac:r_ac + 3 * n_a]).reshape(n_a, 3, D)
    w_f_full = gs[:, r_wf:r_wf + H].reshape(D, H)
    wf_pad = _pad_cols(w_f_full, LANES).astype(BF)[None]
    bf_pad = _pad_cols(b_f.reshape(1, H), LANES)

    def layer_shards(l):
        if l < n_a:
            mix = [(("in", l), a_w_in[l]), (("out", l), a_w_out[l])]
        else:
            j = l - n_a
            mix = ([(("kv", 0), w_kv)] if j == 0 else []) + [(("q", j), b_w_q[j]), (("o", j), b_w_o[j])]
        return mix + [(("gu", l), ffn_w_gu[l]), (("dn", l), ffn_w_down[l])]

    first_level, second_level, W = {}, {}, {}
    st = {"done": None, "tok": None}

    def note(val):
        st["done"] = val
        return val

    def take():
        tok, st["tok"] = st["tok"], None
        return tok

    def chain_after(default):
        if st["tok"] is not None:
            return st["tok"]
        return default if st["done"] is None else st["done"]

    def ag_name(key):
        return f"ag_{key[0]}{key[1]}"

    def start_layer(l):
        for key, shard in layer_shards(l):
            first_level[key], st["tok"] = gather_start(shard, my, chain_after(shard), name=ag_name(key))

    def pass_on(keys):
        for key in keys:
            second_level[key], st["tok"] = gather_mid(first_level.pop(key), chain_after(None), name=ag_name(key))

    def weight(key, shape=None):
        w = gather_finish(second_level.pop(key), st["done"], name=ag_name(key))
        W[key] = w if shape is None else w.reshape(shape)
        return W[key]

    def layer_keys(l):
        keys = [key for key, _ in layer_shards(l)]
        return keys[:-2], keys[-2:]

    h = note(jnp.concatenate([jnp.zeros((pad, D), F32), meta_full, x[0]], axis=0))
    start_layer(0)
    pass_on(layer_keys(0)[0])
    saved = []
    shared = None
    for l in range(depth):
        rec = {"h": h}
        mix_keys, ffn_keys = layer_keys(l)

        def ahead():
            if l >= 1:
                pass_on(ffn_keys[:1])
            if l + 1 < depth:
                start_layer(l + 1)

        if l < n_a:
            xn = note(rms_fwd(h, a_norm_full[l], name=f"a{l}_norm", dep=take()))
            ahead()
            proj = note(mm_nn(xn, weight(("in", l)), name=f"a{l}_in", dep=take()))
            if l == 0:
                pass_on(ffn_keys[:1])
            y = note(conv_fwd(proj, a_conv_full[l], name=f"a{l}_conv"))
            h1 = note(mm_nn(y, weight(("out", l), (1, D, D)), add=h, name=f"a{l}_out", dep=take()))
            rec.update(xn=xn, proj=proj, y=y)
        else:
            j = l - n_a
            if j == 0:
                xnk = note(rms_fwd(h, kv_norm, name="kv_norm", dep=take()))
                ahead()
                kv = note(mm_nn(xnk, weight(("kv", 0)), name="kv_proj", dep=take()))
                k, v = kv_post(kv, k_norm, name="kv_post")
                logits = mm_nn(xnk, wf_pad, name="f_logits", tn_target=LANES)
                cfull = fgate_fwd(logits, bf_pad, pad, name="f_gate")
                c_t = jnp.transpose(cfull[:, :H])
                crow = c_t.reshape(H, nk_attn, 1, tk_attn)
                ccol = c_t.reshape(H, T, 1)
                shared = dict(h=h, xnk=xnk, kv=kv, logits=logits)
                xn = note(rms_fwd(h, b_norm[j], name=f"b{j}_norm"))
            else:
                xn = note(rms_fwd(h, b_norm[j], name=f"b{j}_norm", dep=take()))
                ahead()
            qraw = note(mm_nn(xn, weight(("q", j), (1, D, D)), name=f"b{j}_q", dep=take()))
            q = hn_fwd(qraw, b_q_norm[j], q_scale, name=f"b{j}_qnorm")
            o, lse = attn_fwd(q, k, v, crow, pad, name=f"b{j}_attn")
            note(o)
            h1 = note(mm_nn(o, weight(("o", j), (1, D, D)), add=h, name=f"b{j}_o"))
            rec.update(xn=xn, qraw=qraw, q=q, o=o, lse=lse)
        xn2 = note(rms_fwd(h1, ffn_norm[l], name=f"f{l}_norm", dep=take()))
        pass_on(ffn_keys[1:])
        act, g_s, u_s = mm_swiglu(xn2, weight(("gu", l)), name=f"f{l}_gu", dep=take())
        note(act)
        if l + 1 < depth:
            pass_on(layer_keys(l + 1)[0])
        h = note(mm_nn(act, weight(("dn", l), (1, N_DEV * Fs, D)), add=h1, name=f"f{l}_down", resident=True,
                       tm_target=528, tn_target=1024, dep=take()))
        rec.update(h1=h1, xn2=xn2, act=act, g=g_s, u=u_s)
        saved.append(rec)

    dh, dhb, loss_tile = loss_head(h, loss_target[0], lead, name="loss")
    loss = lax.psum(loss_tile[0, 0], MESH_AXES)

    upd = {}
    small_g = {}
    inflight = []

    def big(name, l, section, slabs, w, m, v):
        handle, st["tok"] = exchange_start(slabs.reshape(N_DEV, -1, w.shape[-1]), name=f"{name}{l}_xs")
        inflight.append((section, name, l, handle, w, m, v))

    def land(sections, after):
        for entry in [e for e in inflight if sections is None or e[0] in sections]:
            inflight.remove(entry)
            _, name, l, handle, w, m, v = entry
            own, got = exchange_finish(handle, after, name=f"{name}{l}_xw")
            flat = lambda t: t.reshape(w.shape[0], -1, w.shape[-1])
            upd[name] = adamw_own(own, got, my, flat(w), flat(m), flat(v), l, upd.get(name),
                                  name=f"{name}{l}_adamw")

    dk = dv = dck = dcq = None
    for l in reversed(range(depth)):
        rec = saved[l]
        land([("ffn", l + 1)], dh)
        dgu = mm_nt_dswiglu(dhb, W[("dn", l)], rec["g"], rec["u"], name=f"f{l}_ddown")
        big("ffn_w_down", l, ("ffn", l), mm_tn(rec["act"], dhb, 1, name=f"f{l}_wdown"),
            ffn_w_down, m_ffn_w_down, v_ffn_w_down)
        big("ffn_w_gu", l, ("ffn", l), mm_tn(rec["xn2"], dgu, N_DEV, name=f"f{l}_wgu", dep=take()),
            ffn_w_gu, m_ffn_w_gu, v_ffn_w_gu)
        dxn2 = mm_nt(dgu, W[("gu", l)], name=f"f{l}_dgu", gb=2, dep=take())
        dh1, dhb, dgf = rms_bwd(dxn2, rec["h1"], ffn_norm[l], dh, name=f"f{l}_dnorm")
        small_g[("ffn_norm", l)] = dgf
        land([("mix", l + 1)], dh1)
        if l < n_a:
            dy = mm_nt(dhb, W[("out", l)], name=f"a{l}_dout")
            big("a_w_out", l, ("mix", l), mm_tn(rec["y"], dhb, 1, name=f"a{l}_wout"),
                a_w_out, m_a_w_out, v_a_w_out)
            db, dc, dhh, dcw = conv_bwd(dy, rec["proj"], a_conv_full[l], name=f"a{l}_dconv", dep=take())
            small_g[("a_conv", l)] = dcw
            dproj = jnp.concatenate([db, dc, dhh], axis=1)
            big("a_w_in", l, ("mix", l), mm_tn(rec["xn"], dproj, N_DEV, name=f"a{l}_win"),
                a_w_in, m_a_w_in, v_a_w_in)
            dxn = mm_nt(dproj, W[("in", l)], name=f"a{l}_din", gb=4, dep=take())
            dh, dhb, dga = rms_bwd(dxn, rec["h"], a_norm_full[l], dh1, name=f"a{l}_dnorm")
            small_g[("a_norm", l)] = dga
        else:
            j = l - n_a
            do = mm_nt(dhb, W[("o", j)], name=f"b{j}_do")
            big("b_w_o", j, ("mix", l), mm_tn(rec["o"], dhb, 1, name=f"b{j}_wo"),
                b_w_o, m_b_w_o, v_b_w_o)
            prev = None if dk is None else (dk, dv, dck, dcq)
            dq, dk, dv, dck, dcq = attn_bwd(rec["q"], k, v, do, rec["o"],
                                            rec["lse"].reshape(H, nk_attn, 1, tk_attn), ccol, prev, pad, tk_attn,
                                            name=f"b{j}_dattn", dep=take())
            dqraw, dqn = hn_bwd(dq, rec["qraw"], b_q_norm[j], q_scale, name=f"b{j}_dqnorm")
            small_g[("b_q_norm", j)] = dqn
            big("b_w_q", j, ("mix", l), mm_tn(rec["xn"], dqraw, 1, name=f"b{j}_wq"),
                b_w_q, m_b_w_q, v_b_w_q)
            dxn = mm_nt(dqraw, W[("q", j)], name=f"b{j}_dq", dep=take())
            dh, dhb, dgb = rms_bwd(dxn, rec["h"], b_norm[j], dh1, name=f"b{j}_dnorm")
            small_g[("b_norm", j)] = dgb
            if j == 0:
                dkraw, dkn = hn_bwd(dk, shared["kv"], k_norm, 1.0, name="kv_dknorm")
                dkv = jnp.concatenate([dkraw, dv.astype(BF)], axis=1)
                dc_full = _pad_cols(jnp.transpose(dck.reshape(H, T) + dcq.reshape(H, T)), LANES)
                dz, dbf = fgate_bwd(dc_full, shared["logits"], bf_pad, pad, name="f_dgate")
                big("w_kv", 0, ("mix", l), mm_tn(shared["xnk"], dkv, N_DEV, name="kv_wkv"),
                    w_kv[None], m_w_kv[None], v_w_kv[None])
                dwf_t = mm_tn(dz, shared["xnk"], 1, name="f_wf", out_dtype=F32, tn_target=1024,
                              dep=take())[0, :H]
                dxn_f = mm_nt(dz, wf_pad, name="f_dxn")
                dxnk = mm_nt(dkv, W[("kv", 0)], add=dxn_f, name="kv_dxn", gb=4)
                dh, dhb, dgkv = rms_bwd(dxnk, shared["h"], kv_norm, dh, name="kv_dnorm")
    land(None, dh)

    grad_x = dh[lead:][None]

    row8 = lambda a: _pad_rows(_pad_cols(a, D), 8)
    stack = lambda key, n: jnp.concatenate([small_g[(key, i)] for i in range(n)], axis=0)
    g_sharded = jnp.concatenate([dh[pad:lead], row8(stack("a_norm", n_a)), row8(stack("a_conv", n_a)), dwf_t], axis=0)
    g_repl = jnp.concatenate([row8(jnp.concatenate([dgkv, stack("b_norm", n_b)], axis=0)),
                              row8(stack("ffn_norm", depth)),
                              row8(jnp.concatenate([_pad_cols(dkn, D), _pad_cols(stack("b_q_norm", n_b), D),
                                                    _pad_cols(dbf[:, :H], D)], axis=0))], axis=0)
    n_sh = g_sharded.shape[0]
    gathered = all_gather(jnp.concatenate([g_sharded, g_repl], axis=0), name="ag_small_grads",
                          dep=upd["a_w_in"][0])
    parts_sh = lax.dynamic_slice_in_dim(gathered[:, :n_sh], my * Ds, Ds, axis=2)
    parts_rp = gathered[:, n_sh:]

    def pack_sh(t_meta, t_an, t_ac, t_wf):
        return jnp.concatenate([t_meta, _pad_rows(t_an, 8), _pad_rows(t_ac.reshape(n_a * 3, Ds), 8),
                                jnp.transpose(t_wf)], axis=0)

    def pack_rp(t_kv, t_bn, t_fn, t_kn, t_qn, t_bf):
        return jnp.concatenate([row8(jnp.concatenate([t_kv.reshape(1, D), t_bn], axis=0)), row8(t_fn),
                                row8(jnp.concatenate([_pad_cols(t_kn.reshape(1, -1), D), _pad_cols(t_qn, D),
                                                      _pad_cols(t_bf.reshape(1, -1), D)], axis=0))], axis=0)

    res_sh = adamw(parts_sh, pack_sh(meta, a_norm, a_conv, w_f), pack_sh(m_meta, m_a_norm, m_a_conv, m_w_f),
                   pack_sh(v_meta, v_a_norm, v_a_conv, v_w_f), name="small_sharded_adamw")
    res_rp = adamw(parts_rp, pack_rp(kv_norm, b_norm, ffn_norm, k_norm, b_q_norm, b_f),
                   pack_rp(m_kv_norm, m_b_norm, m_ffn_norm, m_k_norm, m_b_q_norm, m_b_f),
                   pack_rp(v_kv_norm, v_b_norm, v_ffn_norm, v_k_norm, v_b_q_norm, v_b_f), name="small_repl_adamw")

    def unpack(kind):
        sh, rp = res_sh[kind], res_rp[kind]
        out = {
            "meta": sh[0:n_meta],
            "a_norm": sh[r_an:r_an + n_a],
            "a_conv": sh[r_ac:r_ac + 3 * n_a].reshape(n_a, 3, Ds),
            "w_f": jnp.transpose(sh[r_wf:r_wf + H]),
            "kv_norm": rp[0],
            "b_norm": rp[1:1 + n_b],
            "ffn_norm": rp[8:8 + depth],
            "k_norm": rp[16, :HEAD_DIM],
            "b_q_norm": rp[17:17 + n_b, :HEAD_DIM],
            "b_f": rp[17 + n_b, :H],
        }
        for name, like in (("a_w_in", a_w_in), ("a_w_out", a_w_out), ("b_w_q", b_w_q), ("b_w_o", b_w_o),
                           ("ffn_w_gu", ffn_w_gu), ("ffn_w_down", ffn_w_down)):
            out[name] = upd[name][kind].reshape(like.shape)
        out["w_kv"] = upd["w_kv"][kind].reshape(w_kv.shape)
        return out

    order = ["meta", "a_norm", "a_w_in", "a_conv", "a_w_out", "kv_norm", "w_kv", "k_norm", "w_f", "b_f",
             "b_norm", "b_w_q", "b_q_norm", "b_w_o", "ffn_norm", "ffn_w_gu", "ffn_w_down"]
    outs = [loss, grad_x]
    for kind in range(4):
        vals = unpack(kind)
        outs += [vals[n] for n in order]
    return tuple(outs)
```

```python
import functools
import math

import jax
import jax.numpy as jnp
from jax import lax
from jax.experimental import pallas as pl
from jax.experimental.pallas import tpu as pltpu

N_DEV = 8
MESH_AXES = ("x", "y", "c")
EPS = 1e-6
NEG = -1e30
HEAD_DIM = 128
BLOCK = 128
LANES = 128
V7X_VMEM_LIMIT = 56 * 1024 * 1024

ADAM_LR = 0.001
ADAM_B1 = 0.9
ADAM_B2 = 0.999
ADAM_EPS = 1e-08
ADAM_WD = 0.01
ADAM_STEP = 10

BF = jnp.bfloat16
F32 = jnp.float32


def _tile(n, target, mult):
    best = None
    for t in range(mult, min(n, target) + 1, mult):
        if n % t == 0:
            best = t
    return n if best is None else best


def _params(*sem):
    return pltpu.CompilerParams(dimension_semantics=sem, vmem_limit_bytes=V7X_VMEM_LIMIT)


def _with_dep(body, in_specs, args, dep):
    if dep is None:
        return body, list(in_specs), list(args)
    n_in = len(args)

    def body_dep(*refs):
        body(*refs[:n_in], *refs[n_in + 1:])

    return body_dep, list(in_specs) + [pl.BlockSpec(memory_space=pl.ANY)], list(args) + [dep]


def mm_nn(a, w, *, name, add=None, dep=None, out_dtype=F32, tm_target=1056, tn_target=1024, tk_target=2048,
          resident=False):
    M, K = a.shape
    G, K2, n = w.shape
    assert K == K2
    tm = _tile(M, tm_target, 16)
    tn = _tile(n, tn_target, LANES)
    tk = K if resident else _tile(K, tk_target, LANES)
    nj, nk = n // tn, K // tk
    has_add = add is not None
    if resident:
        grid, sem = (G * nj, M // tm), ("parallel", "parallel")
        order = lambda f: (lambda j, i: f(i, j, 0))
        w_mode = dict(pipeline_mode=pl.Buffered(1))
    else:
        grid, sem = (M // tm, G * nj, nk), ("parallel", "parallel", "arbitrary")
        order = lambda f: f
        w_mode = {}

    def body(*refs):
        if has_add:
            a_ref, w_ref, add_ref, o_ref = refs[:4]
        else:
            a_ref, w_ref, o_ref = refs[:3]
            add_ref = None

        def finish(r):
            if has_add:
                r = r + add_ref[...]
            o_ref[...] = r.astype(out_dtype)

        part = jnp.dot(a_ref[...], w_ref[...], preferred_element_type=F32)
        if nk == 1:
            finish(part)
        else:
            acc_ref = refs[-1]
            k = pl.program_id(2)

            @pl.when(k == 0)
            def _():
                acc_ref[...] = part

            @pl.when(k > 0)
            def _():
                acc_ref[...] += part

            @pl.when(k == nk - 1)
            def _():
                finish(acc_ref[...])

    in_specs = [
        pl.BlockSpec((tm, tk), order(lambda i, j, k: (i, k))),
        pl.BlockSpec((None, tk, tn), order(lambda i, j, k: (j // nj, k, j % nj)), **w_mode),
    ]
    args = [a, w]
    if has_add:
        in_specs.append(pl.BlockSpec((tm, tn), order(lambda i, j, k: (i, j))))
        args.append(add)
    body, in_specs, args = _with_dep(body, in_specs, args, dep)
    return pl.pallas_call(
        body,
        out_shape=jax.ShapeDtypeStruct((M, G * n), out_dtype),
        grid=grid,
        in_specs=in_specs,
        out_specs=pl.BlockSpec((tm, tn), order(lambda i, j, k: (i, j))),
        scratch_shapes=[pltpu.VMEM((tm, tn), F32)] if nk > 1 else [],
        compiler_params=_params(*sem),
        name=name,
    )(*args)


def mm_swiglu(xn, wgu, *, name, dep=None, save_dtype=BF, tm_target=528):
    M, K = xn.shape
    G, _, n = wgu.shape
    half = G // 2
    tm = _tile(M, tm_target, 16)
    tn = _tile(n, 1408, LANES)
    nj = n // tn
    Fh = half * n

    def body(a_ref, wg_ref, wu_ref, act_ref, silu_ref, udsilu_ref):
        a = a_ref[...]
        g = jnp.dot(a, wg_ref[...], preferred_element_type=F32)
        sig = jax.nn.sigmoid(g)
        silu = g * sig
        silu_ref[...] = silu.astype(save_dtype)
        dsilu = sig * (1.0 + g * (1.0 - sig))
        u = jnp.dot(a, wu_ref[...], preferred_element_type=F32)
        udsilu_ref[...] = (u * dsilu).astype(save_dtype)
        act_ref[...] = (silu * u).astype(BF)

    out_block = pl.BlockSpec((tm, tn), lambda j, i: (i, j))
    once = pl.Buffered(1)
    body, in_specs, args = _with_dep(body, [
        pl.BlockSpec((tm, K), lambda j, i: (i, 0)),
        pl.BlockSpec((None, K, tn), lambda j, i: (j // nj, 0, j % nj), pipeline_mode=once),
        pl.BlockSpec((None, K, tn), lambda j, i: (half + j // nj, 0, j % nj), pipeline_mode=once),
    ], [xn, wgu, wgu], dep)
    return pl.pallas_call(
        body,
        out_shape=(jax.ShapeDtypeStruct((M, Fh), BF),
                   jax.ShapeDtypeStruct((M, Fh), save_dtype),
                   jax.ShapeDtypeStruct((M, Fh), save_dtype)),
        grid=(half * nj, M // tm),
        in_specs=in_specs,
        out_specs=(out_block, out_block, out_block),
        compiler_params=_params("parallel", "parallel"),
        name=name,
    )(*args)


def mm_nt(dy, w, *, name, add=None, dep=None, out_dtype=F32, tm_target=1056, tko_target=1024, tc_target=2048,
          gb=1):
    if dy.ndim == 2:
        dy = dy.reshape(1, *dy.shape)
    P, M, Np = dy.shape
    G, K, n = w.shape
    assert P * Np == G * n
    tm = _tile(M, tm_target, 16)
    tko = _tile(K, tko_target, LANES)
    tc = _tile(n, tc_target, LANES)
    nc = n // tc
    gb = gb if nc == 1 else 1
    assert G % gb == 0 and Np % (gb * tc) == 0
    steps = (G // gb) * nc
    per_part = Np // (gb * tc)
    has_add = add is not None

    def body(*refs):
        if has_add:
            dy_ref, w_ref, add_ref, o_ref = refs[:4]
        else:
            dy_ref, w_ref, o_ref = refs[:3]
            add_ref = None

        def finish(r):
            if has_add:
                r = r + add_ref[...]
            o_ref[...] = r.astype(out_dtype)

        part = None
        for g in range(gb):
            term = lax.dot_general(dy_ref[:, g * tc:(g + 1) * tc], w_ref[g], (((1,), (1,)), ((), ())),
                                   preferred_element_type=F32)
            part = term if part is None else part + term
        if steps == 1:
            finish(part)
        else:
            acc_ref = refs[-1]
            s = pl.program_id(2)

            @pl.when(s == 0)
            def _():
                acc_ref[...] = part

            @pl.when(s > 0)
            def _():
                acc_ref[...] += part

            @pl.when(s == steps - 1)
            def _():
                finish(acc_ref[...])

    in_specs = [
        pl.BlockSpec((None, tm, gb * tc), lambda i, o, s: (s // per_part, i, s % per_part)),
        pl.BlockSpec((gb, tko, tc), lambda i, o, s: (s // nc, o, s % nc)),
    ]
    args = [dy, w]
    if has_add:
        in_specs.append(pl.BlockSpec((tm, tko), lambda i, o, s: (i, o)))
        args.append(add)
    body, in_specs, args = _with_dep(body, in_specs, args, dep)
    return pl.pallas_call(
        body,
        out_shape=jax.ShapeDtypeStruct((M, K), out_dtype),
        grid=(M // tm, K // tko, steps),
        in_specs=in_specs,
        out_specs=pl.BlockSpec((tm, tko), lambda i, o, s: (i, o)),
        scratch_shapes=[pltpu.VMEM((tm, tko), F32)] if steps > 1 else [],
        compiler_params=_params("parallel", "parallel", "arbitrary"),
        name=name,
    )(*args)


def mm_nt_dswiglu(dh, w_down, g_s, u_s, *, name, tm_target=1056, tf_target=512):
    M, D = dh.shape
    _, Fh, D2 = w_down.shape
    assert D == D2
    tm = _tile(M, tm_target, 16)
    tf = _tile(Fh, tf_target, LANES)

    def body(dh_ref, w_ref, silu_ref, udsilu_ref, dgu_ref):
        dact = lax.dot_general(dh_ref[...], w_ref[...], (((1,), (1,)), ((), ())),
                               preferred_element_type=F32)
        dgu_ref[1] = (dact * silu_ref[...].astype(F32)).astype(BF)
        dgu_ref[0] = (dact * udsilu_ref[...].astype(F32)).astype(BF)

    blk = pl.BlockSpec((tm, tf), lambda i, f: (i, f))
    return pl.pallas_call(
        body,
        out_shape=jax.ShapeDtypeStruct((2, M, Fh), BF),
        grid=(M // tm, Fh // tf),
        in_specs=[
            pl.BlockSpec((tm, D), lambda i, f: (i, 0)),
            pl.BlockSpec((None, tf, D), lambda i, f: (0, f, 0)),
            blk, blk,
        ],
        out_specs=pl.BlockSpec((2, tm, tf), lambda i, f: (0, i, f)),
        compiler_params=_params("parallel", "parallel"),
        name=name,
    )(dh, w_down, g_s, u_s)


def mm_tn(a, dy, groups, *, name, dep=None, out_dtype=BF, tk_target=512, tn_target=1408):
    M, K = a.shape
    if dy.ndim == 2:
        dy = dy.reshape(1, *dy.shape)
    P, M2, Np = dy.shape
    N = P * Np
    assert M == M2 and N % groups == 0
    n = N // groups
    tk = _tile(K, tk_target, LANES)
    tn = _tile(n, tn_target, LANES)
    nj = n // tn
    assert Np % tn == 0
    per_part = Np // tn

    def body(a_ref, dy_ref, o_ref):
        o_ref[...] = lax.dot_general(a_ref[...], dy_ref[...], (((0,), (0,)), ((), ())),
                                     preferred_element_type=F32).astype(out_dtype)

    body, in_specs, args = _with_dep(body, [
        pl.BlockSpec((M, tk), lambda i, j: (0, i)),
        pl.BlockSpec((None, M, tn), lambda i, j: (j // per_part, 0, j % per_part)),
    ], [a, dy], dep)
    return pl.pallas_call(
        body,
        out_shape=jax.ShapeDtypeStruct((groups, K, n), out_dtype),
        grid=(K // tk, groups * nj),
        in_specs=in_specs,
        out_specs=pl.BlockSpec((None, tk, tn), lambda i, j: (j // nj, i, j % nj)),
        compiler_params=_params("parallel", "parallel"),
        name=name,
    )(*args)


def rms_fwd(h, g, *, name, dep=None):
    T, D = h.shape
    tm = _tile(T, 528, 16)

    def body(h_ref, g_ref, o_ref):
        x = h_ref[...]
        r = lax.rsqrt(jnp.mean(x * x, axis=-1, keepdims=True) + EPS)
        o_ref[...] = ((x * r) * g_ref[...]).astype(BF)

    body, in_specs, args = _with_dep(
        body, [pl.BlockSpec((tm, D), lambda i: (i, 0)), pl.BlockSpec((1, D), lambda i: (0, 0))],
        [h, g.reshape(1, D)], dep)
    return pl.pallas_call(
        body,
        out_shape=jax.ShapeDtypeStruct((T, D), BF),
        grid=(T // tm,),
        in_specs=in_specs,
        out_specs=pl.BlockSpec((tm, D), lambda i: (i, 0)),
        compiler_params=_params("parallel"),
        name=name,
    )(*args)


def rms_bwd(dxn, h, g, add, *, name):
    T, D = h.shape
    tm = _tile(T, 384, 16)

    def body(dxn_ref, h_ref, g_ref, add_ref, dh_ref, dhb_ref, dg_ref):
        x = h_ref[...]
        dy = dxn_ref[...]
        r = lax.rsqrt(jnp.mean(x * x, axis=-1, keepdims=True) + EPS)
        xhat = x * r
        part = jnp.sum(dy * xhat, axis=0, keepdims=True)

        @pl.when(pl.program_id(0) == 0)
        def _():
            dg_ref[...] = part

        @pl.when(pl.program_id(0) > 0)
        def _():
            dg_ref[...] += part

        dxh = dy * g_ref[...]
        dh = add_ref[...] + r * (dxh - xhat * jnp.mean(dxh * xhat, axis=-1, keepdims=True))
        dh_ref[...] = dh
        dhb_ref[...] = dh.astype(BF)

    row = pl.BlockSpec((tm, D), lambda i: (i, 0))
    vec = pl.BlockSpec((1, D), lambda i: (0, 0))
    return pl.pallas_call(
        body,
        out_shape=(jax.ShapeDtypeStruct((T, D), F32), jax.ShapeDtypeStruct((T, D), BF),
                   jax.ShapeDtypeStruct((1, D), F32)),
        grid=(T // tm,),
        in_specs=[row, row, vec, row],
        out_specs=(row, row, vec),
        compiler_params=_params("arbitrary"),
        name=name,
    )(dxn, h, g.reshape(1, D), add)


def _head_norm(x, gain):
    r = lax.rsqrt(jnp.mean(x * x, axis=-1, keepdims=True) + EPS)
    return (x * r) * gain


def hn_fwd(qraw, gain, out_scale, *, name):
    T, D = qraw.shape
    H = D // HEAD_DIM
    tm = _tile(T, 528, 16)

    def body(q_ref, g_ref, o_ref):
        gain_v = g_ref[...]
        for hd in range(H):
            sl = slice(hd * HEAD_DIM, (hd + 1) * HEAD_DIM)
            o_ref[:, sl] = (_head_norm(q_ref[:, sl], gain_v) * out_scale).astype(BF)

    return pl.pallas_call(
        body,
        out_shape=jax.ShapeDtypeStruct((T, D), BF),
        grid=(T // tm,),
        in_specs=[pl.BlockSpec((tm, D), lambda i: (i, 0)),
                  pl.BlockSpec((1, HEAD_DIM), lambda i: (0, 0))],
        out_specs=pl.BlockSpec((tm, D), lambda i: (i, 0)),
        compiler_params=_params("parallel"),
        name=name,
    )(qraw, gain.reshape(1, HEAD_DIM))


def kv_post(kv, gain, *, name):
    T, D2 = kv.shape
    D = D2 // 2
    H = D // HEAD_DIM
    tm = _tile(T, 528, 16)

    def body(k_ref, v_ref, g_ref, ko_ref, vo_ref):
        gain_v = g_ref[...]
        for hd in range(H):
            sl = slice(hd * HEAD_DIM, (hd + 1) * HEAD_DIM)
            ko_ref[:, sl] = _head_norm(k_ref[:, sl], gain_v).astype(BF)
        vo_ref[...] = v_ref[...].astype(BF)

    blk = pl.BlockSpec((tm, D), lambda i: (i, 0))
    return pl.pallas_call(
        body,
        out_shape=(jax.ShapeDtypeStruct((T, D), BF), jax.ShapeDtypeStruct((T, D), BF)),
        grid=(T // tm,),
        in_specs=[blk, pl.BlockSpec((tm, D), lambda i: (i, 1)),
                  pl.BlockSpec((1, HEAD_DIM), lambda i: (0, 0))],
        out_specs=(blk, blk),
        compiler_params=_params("parallel"),
        name=name,
    )(kv, kv, gain.reshape(1, HEAD_DIM))


def hn_bwd(dq, qraw, gain, out_scale, *, name):
    T, D = dq.shape
    H = D // HEAD_DIM
    tm = _tile(T, 384, 16)

    def body(dq_ref, q_ref, g_ref, o_ref, dg_ref):
        gain_v = g_ref[...]
        part = jnp.zeros((1, HEAD_DIM), F32)
        for hd in range(H):
            sl = slice(hd * HEAD_DIM, (hd + 1) * HEAD_DIM)
            x = q_ref[:, sl]
            dy = dq_ref[:, sl] * out_scale
            r = lax.rsqrt(jnp.mean(x * x, axis=-1, keepdims=True) + EPS)
            xhat = x * r
            part = part + jnp.sum(dy * xhat, axis=0, keepdims=True)
            dxh = dy * gain_v
            o_ref[:, sl] = (r * (dxh - xhat * jnp.mean(dxh * xhat, axis=-1, keepdims=True))).astype(BF)

        @pl.when(pl.program_id(0) == 0)
        def _():
            dg_ref[...] = part

        @pl.when(pl.program_id(0) > 0)
        def _():
            dg_ref[...] += part

    blk = pl.BlockSpec((tm, D), lambda i: (i, 0))
    vec = pl.BlockSpec((1, HEAD_DIM), lambda i: (0, 0))
    return pl.pallas_call(
        body,
        out_shape=(jax.ShapeDtypeStruct((T, D), BF), jax.ShapeDtypeStruct((1, HEAD_DIM), F32)),
        grid=(T // tm,),
        in_specs=[blk, blk, vec],
        out_specs=(blk, vec),
        compiler_params=_params("arbitrary"),
        name=name,
    )(dq, qraw, gain.reshape(1, HEAD_DIM))


def _shift_down(cur, above, k, rowc):
    out = pltpu.roll(cur, k, 0)
    for i in range(k):
        out = jnp.where(rowc == i, above[8 - k + i:8 - k + i + 1], out)
    return out


def _shift_up(cur, below, k, rowc):
    R = cur.shape[0]
    out = pltpu.roll(cur, R - k, 0)
    for i in range(k):
        out = jnp.where(rowc == R - k + i, below[i:i + 1], out)
    return out


def _conv3(u, u_above, wv, rowc):
    u1 = _shift_down(u, u_above, 1, rowc)
    u2 = _shift_down(u, u_above, 2, rowc)
    return wv[0:1] * u2 + wv[1:2] * u1 + wv[2:3] * u, u1, u2


def conv_fwd(proj, w, *, name):
    T, D3 = proj.shape
    D = D3 // 3
    tc = LANES if D % LANES == 0 else D
    nb = D // tc
    R = _tile(T, 264, 8)

    def body(b_ref, c_ref, h_ref, w_ref, y_ref):
        rowc = lax.broadcasted_iota(jnp.int32, (R, 1), 0)
        wv = w_ref[...]
        for r0 in range(0, T, R):
            rows = slice(r0, r0 + R)
            u = c_ref[rows, :] * h_ref[rows, :]
            if r0 == 0:
                above = jnp.zeros((8, tc), F32)
            else:
                above = c_ref[r0 - 8:r0, :] * h_ref[r0 - 8:r0, :]
            conv, _, _ = _conv3(u, above, wv, rowc)
            y_ref[rows, :] = (b_ref[rows, :] * conv).astype(BF)

    return pl.pallas_call(
        body,
        out_shape=jax.ShapeDtypeStruct((T, D), BF),
        grid=(nb,),
        in_specs=[
            pl.BlockSpec((T, tc), lambda j: (0, j)),
            pl.BlockSpec((T, tc), lambda j: (0, nb + j)),
            pl.BlockSpec((T, tc), lambda j: (0, 2 * nb + j)),
            pl.BlockSpec((3, tc), lambda j: (0, j)),
        ],
        out_specs=pl.BlockSpec((T, tc), lambda j: (0, j)),
        compiler_params=_params("parallel"),
        name=name,
    )(proj, proj, proj, w)


def conv_bwd(dy, proj, w, *, name, dep=None):
    T, D = dy.shape
    tc = LANES if D % LANES == 0 else D
    nb = D // tc
    R = _tile(T, 264, 8)

    def body(dy_ref, b_ref, c_ref, h_ref, w_ref, db_ref, dc_ref, dh_ref, dw_ref):
        rowc = lax.broadcasted_iota(jnp.int32, (R, 1), 0)
        wv = w_ref[...]
        dw = [jnp.zeros((1, tc), F32) for _ in range(3)]
        for r0 in range(0, T, R):
            rows = slice(r0, r0 + R)
            c = c_ref[rows, :]
            hh = h_ref[rows, :]
            u = c * hh
            if r0 == 0:
                above = jnp.zeros((8, tc), F32)
            else:
                above = c_ref[r0 - 8:r0, :] * h_ref[r0 - 8:r0, :]
            conv, u1, u2 = _conv3(u, above, wv, rowc)
            dyv = dy_ref[rows, :]
            db_ref[rows, :] = (dyv * conv).astype(BF)
            dconv = dyv * b_ref[rows, :]
            if r0 + R == T:
                below = jnp.zeros((8, tc), F32)
            else:
                below = dy_ref[r0 + R:r0 + R + 8, :] * b_ref[r0 + R:r0 + R + 8, :]
            dw[0] = dw[0] + jnp.sum(dconv * u2, axis=0, keepdims=True)
            dw[1] = dw[1] + jnp.sum(dconv * u1, axis=0, keepdims=True)
            dw[2] = dw[2] + jnp.sum(dconv * u, axis=0, keepdims=True)
            du = (wv[2:3] * dconv + wv[1:2] * _shift_up(dconv, below, 1, rowc)
                  + wv[0:1] * _shift_up(dconv, below, 2, rowc))
            dc_ref[rows, :] = (du * hh).astype(BF)
            dh_ref[rows, :] = (du * c).astype(BF)
        for i in range(3):
            dw_ref[i:i + 1, :] = dw[i]

    strip = pl.BlockSpec((T, tc), lambda j: (0, j))
    wblk = pl.BlockSpec((3, tc), lambda j: (0, j))
    out = jax.ShapeDtypeStruct((T, D), BF)
    body, in_specs, args = _with_dep(body, [
        strip,
        pl.BlockSpec((T, tc), lambda j: (0, j)),
        pl.BlockSpec((T, tc), lambda j: (0, nb + j)),
        pl.BlockSpec((T, tc), lambda j: (0, 2 * nb + j)),
        wblk,
    ], [dy, proj, proj, proj, w], dep)
    return pl.pallas_call(
        body,
        out_shape=(out, out, out, jax.ShapeDtypeStruct((3, D), F32)),
        grid=(nb,),
        in_specs=in_specs,
        out_specs=(strip, strip, strip, wblk),
        compiler_params=_params("parallel"),
        name=name,
    )(*args)


def _log_sigmoid(z):
    return jnp.minimum(z, 0.0) - jnp.log(1.0 + jnp.exp(-jnp.abs(z)))


def fgate_fwd(logits, bias, pad, *, name):
    T, W = logits.shape
    cb = _tile(T, 128, 8)
    nblk = T // cb

    def body(z_ref, b_ref, c_ref, lf_ref):
        row = lax.broadcasted_iota(jnp.int32, (T, 1), 0)
        lf_ref[...] = jnp.where(row >= pad, _log_sigmoid(z_ref[...] + b_ref[...]), 0.0)
        ri = lax.broadcasted_iota(jnp.int32, (cb, cb), 0)
        ci = lax.broadcasted_iota(jnp.int32, (cb, cb), 1)
        tri = (ci <= ri).astype(F32)

        def step(i, carry):
            rows = pl.ds(pl.multiple_of(i * cb, cb), cb)
            blk = lf_ref[rows, :]
            c_ref[rows, :] = carry + jnp.dot(tri, blk, precision=lax.Precision.HIGHEST,
                                             preferred_element_type=F32)
            return carry + jnp.sum(blk, axis=0, keepdims=True)

        lax.fori_loop(0, nblk, step, jnp.zeros((1, W), F32))

    return pl.pallas_call(
        body,
        out_shape=jax.ShapeDtypeStruct((T, W), F32),
        in_specs=[pl.BlockSpec(memory_space=pltpu.VMEM), pl.BlockSpec(memory_space=pltpu.VMEM)],
        out_specs=pl.BlockSpec(memory_space=pltpu.VMEM),
        scratch_shapes=[pltpu.VMEM((T, W), F32)],
        compiler_params=pltpu.CompilerParams(vmem_limit_bytes=V7X_VMEM_LIMIT),
        name=name,
    )(logits, bias)


def fgate_bwd(dc, logits, bias, pad, *, name):
    T, W = logits.shape
    cb = _tile(T, 128, 8)
    nblk = T // cb

    def body(dc_ref, z_ref, b_ref, dz_ref, db_ref, rs_ref):
        ri = lax.broadcasted_iota(jnp.int32, (cb, cb), 0)
        ci = lax.broadcasted_iota(jnp.int32, (cb, cb), 1)
        triu = (ci >= ri).astype(F32)

        def step(i, carry):
            rows = pl.ds(pl.multiple_of((nblk - 1 - i) * cb, cb), cb)
            blk = dc_ref[rows, :]
            rs_ref[rows, :] = carry + jnp.dot(triu, blk, precision=lax.Precision.HIGHEST,
                                              preferred_element_type=F32)
            return carry + jnp.sum(blk, axis=0, keepdims=True)

        lax.fori_loop(0, nblk, step, jnp.zeros((1, W), F32))
        row = lax.broadcasted_iota(jnp.int32, (T, 1), 0)
        z = z_ref[...] + b_ref[...]
        dz = jnp.where(row >= pad, rs_ref[...] * jax.nn.sigmoid(-z), 0.0)
        dz_ref[...] = dz.astype(BF)
        db_ref[...] = jnp.sum(dz, axis=0, keepdims=True)

    vm = pl.BlockSpec(memory_space=pltpu.VMEM)
    return pl.pallas_call(
        body,
        out_shape=(jax.ShapeDtypeStruct((T, W), BF), jax.ShapeDtypeStruct((1, W), F32)),
        in_specs=[vm, vm, vm],
        out_specs=(vm, vm),
        scratch_shapes=[pltpu.VMEM((T, W), F32)],
        compiler_params=pltpu.CompilerParams(vmem_limit_bytes=V7X_VMEM_LIMIT),
        name=name,
    )(dc, logits, bias)


def _scores(qb, kb, ck):
    return lax.dot_general(qb, kb, (((1,), (1,)), ((), ())), preferred_element_type=F32) - ck


def _causal(s, row, col, pad):
    return jnp.where((col <= row) & (col >= pad), s, NEG)


def attn_fwd(q, k, v, crow, pad, *, name):
    T, D = q.shape
    H = D // HEAD_DIM
    nk, tk = crow.shape[1], crow.shape[3]
    tq = tk
    nq = T // tq

    hp = 2 if H % 2 == 0 else 1
    wide = hp * HEAD_DIM

    def body(q_ref, k_ref, v_ref, cr_ref, o_ref, lse_ref):
        qi = pl.program_id(1)
        row = qi * tq + lax.broadcasted_iota(jnp.int32, (tq, 1), 0)
        heads = [slice(a * HEAD_DIM, (a + 1) * HEAD_DIM) for a in range(hp)]
        qbs = [q_ref[:, sl] for sl in heads]

        def step(kc, carry, masked):
            rows = pl.ds(pl.multiple_of(kc * tk, tk), tk)
            out = []
            for a, sl in enumerate(heads):
                m, l, acc = carry[a]
                s = _scores(qbs[a], k_ref[rows, sl], cr_ref[a, kc])
                if masked:
                    s = _causal(s, row, kc * tk + lax.broadcasted_iota(jnp.int32, (1, tk), 1), pad)
                m_new = jnp.maximum(m, jnp.max(s, axis=-1, keepdims=True))
                alpha = jnp.exp(m - m_new)
                p = jnp.exp(s - m_new)
                l = alpha * l + jnp.sum(p, axis=-1, keepdims=True)
                acc = alpha * acc + jnp.dot(p.astype(BF), v_ref[rows, sl], preferred_element_type=F32)
                out.append((m_new, l, acc))
            return tuple(out)

        init = tuple((jnp.full((tq, 1), NEG, F32), jnp.zeros((tq, 1), F32), jnp.zeros((tq, HEAD_DIM), F32))
                     for _ in heads)
        carry = step(0, init, True)
        carry = lax.fori_loop(1, qi, lambda kc, c: step(kc, c, False), carry)
        carry = lax.cond(qi > 0, lambda c: step(qi, c, True), lambda c: c, carry)
        valid = row >= pad
        for a, sl in enumerate(heads):
            m, l, acc = carry[a]
            o_ref[:, sl] = jnp.where(valid, acc / l, 0.0).astype(BF)
            lse_ref[a] = jnp.where(valid, m + jnp.log(l), 0.0)

    return pl.pallas_call(
        body,
        out_shape=(jax.ShapeDtypeStruct((T, D), BF), jax.ShapeDtypeStruct((H, T, 1), F32)),
        grid=(H // hp, nq),
        in_specs=[
            pl.BlockSpec((tq, wide), lambda h, i: (i, h)),
            pl.BlockSpec((T, wide), lambda h, i: (0, h)),
            pl.BlockSpec((T, wide), lambda h, i: (0, h)),
            pl.BlockSpec((hp, nk, 1, tk), lambda h, i: (h, 0, 0, 0)),
        ],
        out_specs=(pl.BlockSpec((tq, wide), lambda h, i: (i, h)),
                   pl.BlockSpec((hp, tq, 1), lambda h, i: (h, i, 0))),
        compiler_params=_params("parallel", "arbitrary"),
        name=name,
    )(q, k, v, crow)


def attn_bwd(q, k, v, do, o, lse, ccol, prev, pad, tk, *, name, dep=None):
    T, D = q.shape
    H = D // HEAD_DIM
    nk = T // tk
    tq, nq = tk, nk
    has_prev = prev is not None
    hp = 2 if H % 2 == 0 else 1
    wide = hp * HEAD_DIM
    heads = [slice(a * HEAD_DIM, (a + 1) * HEAD_DIM) for a in range(hp)]
    nt = (((1,), (1,)), ((), ()))

    def body(*refs):
        q_ref, k_ref, v_ref, do_ref, o_ref, lse_ref, cc_ref = refs[:7]
        refs = refs[7:]
        if has_prev:
            pk_ref, pv_ref, pc_ref, pq_ref = refs[:4]
            refs = refs[4:]
        dq_ref, dk_ref, dv_ref, dck_ref, dcq_ref, delta_ref = refs
        kc = pl.program_id(1)

        @pl.when(kc == 0)
        def _():
            dq_ref[...] = jnp.zeros_like(dq_ref)
            dcq_ref[...] = pq_ref[...] if has_prev else jnp.zeros_like(dcq_ref)
            ones = jnp.ones((8, HEAD_DIM), F32)
            for a, sl in enumerate(heads):
                for i in range(nq):
                    rows = slice(i * tq, (i + 1) * tq)
                    prod = do_ref[rows, sl].astype(BF).astype(F32) * o_ref[rows, sl].astype(F32)
                    delta_ref[a, i] = lax.dot_general(ones, prod, nt, precision=lax.Precision.HIGHEST,
                                                      preferred_element_type=F32)[0:1]

        kbs = [k_ref[:, sl] for sl in heads]
        vbs = [v_ref[:, sl] for sl in heads]
        cks = [jnp.broadcast_to(cc_ref[a], (tk, tq)) for a in range(hp)]
        krow = kc * tk + lax.broadcasted_iota(jnp.int32, (tk, 1), 0)

        def step(qi, carry, masked):
            rows = pl.ds(pl.multiple_of(qi * tq, tq), tq)
            out = []
            for a, sl in enumerate(heads):
                dk, dv, dck = carry[a]
                qb = q_ref[rows, sl]
                dob = do_ref[rows, sl].astype(BF)
                s = lax.dot_general(kbs[a], qb, nt, preferred_element_type=F32) - cks[a]
                if masked:
                    qcol = qi * tq + lax.broadcasted_iota(jnp.int32, (1, tq), 1)
                    s = jnp.where((krow <= qcol) & (krow >= pad), s, NEG)
                p = jnp.exp(s - lse_ref[a, qi])
                dp = lax.dot_general(vbs[a], dob, nt, preferred_element_type=F32)
                ds = p * (dp - delta_ref[a, qi])
                dsb = ds.astype(BF)
                dv = dv + jnp.dot(p.astype(BF), dob, preferred_element_type=F32)
                dk = dk + jnp.dot(dsb, qb, preferred_element_type=F32)
                dq_ref[rows, sl] += lax.dot_general(dsb, kbs[a], (((0,), (0,)), ((), ())),
                                                    preferred_element_type=F32)
                dcq_ref[a, qi] += jnp.sum(ds, axis=0, keepdims=True)
                part = ds[:, 0:LANES]
                for j in range(1, tq // LANES):
                    part = part + ds[:, j * LANES:(j + 1) * LANES]
                out.append((dk, dv, dck - part))
            return tuple(out)

        def rest(masked):
            return lambda c: lax.fori_loop(kc + 1, nq, lambda qi, cc: step(qi, cc, masked), c)

        init = tuple((jnp.zeros((tk, HEAD_DIM), F32), jnp.zeros((tk, HEAD_DIM), F32), jnp.zeros((tk, LANES), F32))
                     for _ in heads)
        carry = step(kc, init, True)
        carry = lax.cond(kc == 0, rest(True), rest(False), carry)
        for a, sl in enumerate(heads):
            dk, dv, dck = carry[a]
            dck = jnp.sum(dck, axis=1, keepdims=True)
            if has_prev:
                dk = dk + pk_ref[:, sl]
                dv = dv + pv_ref[:, sl]
                dck = dck + pc_ref[a]
            dk_ref[:, sl] = dk
            dv_ref[:, sl] = dv
            dck_ref[a] = dck

    head_all = pl.BlockSpec((T, wide), lambda h, j: (0, h))
    head_blk = pl.BlockSpec((tk, wide), lambda h, j: (j, h))
    rows_all = pl.BlockSpec((hp, nq, 1, tq), lambda h, j: (h, 0, 0, 0))
    col_blk = pl.BlockSpec((hp, tk, 1), lambda h, j: (h, j, 0))
    in_specs = [head_all, head_blk, head_blk, head_all, head_all, rows_all, col_blk]
    args = [q, k, v, do, o, lse, ccol]
    if has_prev:
        in_specs += [head_blk, head_blk, col_blk, rows_all]
        args += list(prev)
    body, in_specs, args = _with_dep(body, in_specs, args, dep)
    return pl.pallas_call(
        body,
        out_shape=(jax.ShapeDtypeStruct((T, D), F32), jax.ShapeDtypeStruct((T, D), F32),
                   jax.ShapeDtypeStruct((T, D), F32), jax.ShapeDtypeStruct((H, T, 1), F32),
                   jax.ShapeDtypeStruct((H, nq, 1, tq), F32)),
        grid=(H // hp, nk),
        in_specs=in_specs,
        out_specs=(head_all, head_blk, head_blk, col_blk, rows_all),
        scratch_shapes=[pltpu.VMEM((hp, nq, 1, tq), F32)],
        compiler_params=_params("parallel", "arbitrary"),
        name=name,
    )(*args)


def loss_head(h, target, lead, *, name):
    T, D = h.shape
    tm = lead
    assert T % tm == 0 and target.shape[0] % tm == 0
    inv_d = 1.0 / D

    def body(h_ref, t_ref, dh_ref, dhb_ref, loss_ref):
        i = pl.program_id(0)

        @pl.when(i == 0)
        def _():
            dh_ref[...] = jnp.zeros_like(dh_ref)
            dhb_ref[...] = jnp.zeros_like(dhb_ref)
            loss_ref[...] = jnp.zeros_like(loss_ref)

        @pl.when(i > 0)
        def _():
            e = h_ref[...] - t_ref[...]
            dh = e * inv_d
            dh_ref[...] = dh
            dhb_ref[...] = dh.astype(BF)
            loss_ref[...] += 0.5 * inv_d * jnp.sum(e * e)

    return pl.pallas_call(
        body,
        out_shape=(jax.ShapeDtypeStruct((T, D), F32), jax.ShapeDtypeStruct((T, D), BF),
                   jax.ShapeDtypeStruct((8, LANES), F32)),
        grid=(T // tm,),
        in_specs=[pl.BlockSpec((tm, D), lambda i: (i, 0)),
                  pl.BlockSpec((tm, D), lambda i: (jnp.maximum(i - 1, 0), 0))],
        out_specs=(pl.BlockSpec((tm, D), lambda i: (i, 0)), pl.BlockSpec((tm, D), lambda i: (i, 0)),
                   pl.BlockSpec((8, LANES), lambda i: (0, 0))),
        compiler_params=_params("arbitrary"),
        name=name,
    )(h, target)


def adamw(parts, w, m, v, *, name):
    P, R, C = parts.shape
    tr = _tile(R, max(16, (128 * 1024) // C), 16)

    def body(p_ref, w_ref, m_ref, v_ref, g_ref, d_ref, mo_ref, vo_ref):
        g = p_ref[0].astype(F32)
        for i in range(1, P):
            g = g + p_ref[i].astype(F32)
        m_new = ADAM_B1 * m_ref[...] + (1.0 - ADAM_B1) * g
        v_new = ADAM_B2 * v_ref[...] + (1.0 - ADAM_B2) * jnp.square(g)
        m_hat = m_new / (1.0 - ADAM_B1 ** ADAM_STEP)
        v_hat = v_new / (1.0 - ADAM_B2 ** ADAM_STEP)
        g_ref[...] = g
        d_ref[...] = -ADAM_LR * (m_hat / (jnp.sqrt(v_hat) + ADAM_EPS) + ADAM_WD * w_ref[...])
        mo_ref[...] = m_new
        vo_ref[...] = v_new

    blk = pl.BlockSpec((tr, C), lambda i: (i, 0))
    out = jax.ShapeDtypeStruct((R, C), F32)
    return pl.pallas_call(
        body,
        out_shape=(out, out, out, out),
        grid=(R // tr,),
        in_specs=[pl.BlockSpec((P, tr, C), lambda i: (0, i, 0)), blk, blk, blk],
        out_specs=(blk, blk, blk, blk),
        compiler_params=_params("parallel"),
        name=name,
    )(parts, w, m, v)


def _flip(v, bit):
    return 1 - v if bit else v


def all_gather(shard, *, name, dep=None):
    def body(x_ref, out_ref, send_sems, recv_sems, local_sem):
        x, y, c = lax.axis_index("x"), lax.axis_index("y"), lax.axis_index("c")
        me, sibling = (x, y, c), (x, y, 1 - c)
        chips = [(1 - x, y), (x, 1 - y), (1 - x, 1 - y)]

        def block(px, py, pc):
            return out_ref.at[4 * px + 2 * py + pc]

        def copy(k, blk, to, src=None):
            return pltpu.make_async_remote_copy(
                src_ref=block(*blk) if src is None else src,
                dst_ref=block(*blk),
                send_sem=send_sems.at[k],
                recv_sem=recv_sems.at[k],
                device_id=to,
                device_id_type=pl.DeviceIdType.MESH,
            )

        mine = pltpu.make_async_copy(x_ref, block(*me), local_sem)
        mine.start()
        first = [copy(0, me, sibling, src=x_ref)]
        first += [copy(1 + j, me, (*chip, c), src=x_ref) for j, chip in enumerate(chips)]
        for cp in first:
            cp.start()
        passed = [copy(4 + j, (*chip, c), sibling) for j, chip in enumerate(chips)]
        for j, chip in enumerate(chips):
            copy(1 + j, (*chip, c), me).wait_recv()
            passed[j].start()
        copy(0, sibling, me).wait_recv()
        for j, chip in enumerate(chips):
            copy(4 + j, (*chip, 1 - c), me).wait_recv()
        for cp in first + passed:
            cp.wait_send()
        mine.wait()

    body, in_specs, args = _with_dep(body, [pl.BlockSpec(memory_space=pl.ANY)], [shard], dep)
    return pl.pallas_call(
        body,
        out_shape=jax.ShapeDtypeStruct((N_DEV,) + shard.shape, shard.dtype),
        in_specs=in_specs,
        out_specs=pl.BlockSpec(memory_space=pl.ANY),
        scratch_shapes=[pltpu.SemaphoreType.DMA((7,)), pltpu.SemaphoreType.DMA((7,)),
                        pltpu.SemaphoreType.DMA],
        name=name,
    )(*args)


def exchange_slabs(slabs, *, name):
    def body(g_ref, r_ref, send_sems, recv_sems, local_sem):
        x, y, c = lax.axis_index("x"), lax.axis_index("y"), lax.axis_index("c")
        me = 4 * x + 2 * y + c
        mine = pltpu.make_async_copy(g_ref.at[me], r_ref.at[me], local_sem)
        mine.start()
        sends, recvs = [], []
        for k in range(1, N_DEV):
            px, py, pc = _flip(x, (k >> 2) & 1), _flip(y, (k >> 1) & 1), _flip(c, k & 1)
            peer = 4 * px + 2 * py + pc
            sends.append(pltpu.make_async_remote_copy(
                src_ref=g_ref.at[peer], dst_ref=r_ref.at[me],
                send_sem=send_sems.at[k - 1], recv_sem=recv_sems.at[k - 1],
                device_id=(px, py, pc), device_id_type=pl.DeviceIdType.MESH))
            recvs.append(pltpu.make_async_remote_copy(
                src_ref=g_ref.at[peer], dst_ref=r_ref.at[peer],
                send_sem=send_sems.at[k - 1], recv_sem=recv_sems.at[k - 1],
                device_id=(px, py, pc), device_id_type=pl.DeviceIdType.MESH))
        for cp in sends:
            cp.start()
        for cp in recvs:
            cp.wait_recv()
        for cp in sends:
            cp.wait_send()
        mine.wait()

    return pl.pallas_call(
        body,
        out_shape=jax.ShapeDtypeStruct(slabs.shape, slabs.dtype),
        in_specs=[pl.BlockSpec(memory_space=pl.ANY)],
        out_specs=pl.BlockSpec(memory_space=pl.ANY),
        scratch_shapes=[pltpu.SemaphoreType.DMA((7,)), pltpu.SemaphoreType.DMA((7,)),
                        pltpu.SemaphoreType.DMA],
        name=name,
    )(slabs)


def reduce_adamw(slabs, w, m, v, *, name):
    got = exchange_slabs(slabs, name=name + "_xchg")
    return adamw(got, w, m, v, name=name + "_adamw")


_HBM = pl.BlockSpec(memory_space=pltpu.HBM)
_SEM = pl.BlockSpec(memory_space=pltpu.SEMAPHORE)
_ANY = pl.BlockSpec(memory_space=pl.ANY)
_EFFECT = pltpu.SideEffectType.DATAFLOW_SIDE_EFFECTING
_N_FIRST = 4


def _first_copies(land_ref, send_sems, recv_sems):
    x, y, c = lax.axis_index("x"), lax.axis_index("y"), lax.axis_index("c")
    mine = land_ref.at[4 * x + 2 * y + c]
    targets = [(x, y, 1 - c), (1 - x, y, c), (x, 1 - y, c), (1 - x, 1 - y, c)]
    sends, recvs = [], []
    for k, (px, py, pc) in enumerate(targets):
        common = dict(send_sem=send_sems.at[k], recv_sem=recv_sems.at[k], device_id=(px, py, pc),
                      device_id_type=pl.DeviceIdType.MESH)
        sends.append(pltpu.make_async_remote_copy(src_ref=mine, dst_ref=mine, **common))
        theirs = land_ref.at[4 * px + 2 * py + pc]
        recvs.append(pltpu.make_async_remote_copy(src_ref=theirs, dst_ref=theirs, **common))
    return sends, recvs


def _second_copies(land_ref, send_sems, recv_sems):
    x, y, c = lax.axis_index("x"), lax.axis_index("y"), lax.axis_index("c")
    sends, recvs = [], []
    for j, (px, py) in enumerate([(1 - x, y), (x, 1 - y), (1 - x, 1 - y)]):
        common = dict(send_sem=send_sems.at[j], recv_sem=recv_sems.at[j], device_id=(x, y, 1 - c),
                      device_id_type=pl.DeviceIdType.MESH)
        blk = land_ref.at[4 * px + 2 * py + c]
        sends.append(pltpu.make_async_remote_copy(src_ref=blk, dst_ref=blk, **common))
        got = land_ref.at[4 * px + 2 * py + (1 - c)]
        recvs.append(pltpu.make_async_remote_copy(src_ref=got, dst_ref=got, **common))
    return sends, recvs


def gather_start(shard, me, after, *, name):
    R, C = shard.shape
    tr = _tile(R, max(16, (512 * 1024) // C), 16)

    def place_body(me_ref, x_ref, o_ref):
        o_ref[...] = x_ref[...].astype(BF)

    land = pl.pallas_call(
        place_body, name=name + "_own",
        out_shape=jax.ShapeDtypeStruct((N_DEV, R, C), BF),
        grid_spec=pltpu.PrefetchScalarGridSpec(
            num_scalar_prefetch=1,
            grid=(R // tr,),
            in_specs=[pl.BlockSpec((tr, C), lambda i, me_ref: (i, 0))],
            out_specs=pl.BlockSpec((None, tr, C), lambda i, me_ref: (me_ref[0], i, 0)),
        ),
        compiler_params=_params("parallel"),
    )(me.reshape(1).astype(jnp.int32), shard)

    def body(land_ref, after_ref, send_sems, recv_sems, land_thru, token):
        sends, _ = _first_copies(land_ref, send_sems, recv_sems)
        for cp in sends:
            cp.start()
        token[...] = jnp.zeros_like(token)

    send_sems, recv_sems, land_thru, token = pl.pallas_call(
        body, name=name + "_s1",
        out_shape=(pltpu.SemaphoreType.DMA((_N_FIRST,)), pltpu.SemaphoreType.DMA((_N_FIRST,)),
                   pltpu.HBM(land.shape, land.dtype), jax.ShapeDtypeStruct((8, LANES), F32)),
        in_specs=(_HBM, _ANY),
        out_specs=(_SEM, _SEM, _HBM, pl.BlockSpec(memory_space=pltpu.VMEM)),
        input_output_aliases={0: 2},
        compiler_params=pltpu.CompilerParams(has_side_effects=_EFFECT),
    )(pltpu.with_memory_space_constraint(land, pltpu.HBM), after)
    return (send_sems, recv_sems, land_thru), token


def gather_mid(handle, after, *, name):
    send_sems, recv_sems, land_thru = handle

    def body(land_ref, send1, recv1, after_ref, send2, recv2, land_out, token):
        sends, recvs = _first_copies(land_ref, send1, recv1)
        for cp in sends:
            cp.wait_send()
        for cp in recvs:
            cp.wait_recv()
        seconds, _ = _second_copies(land_ref, send2, recv2)
        for cp in seconds:
            cp.start()
        token[...] = jnp.zeros_like(token)

    send2, recv2, land2, token = pl.pallas_call(
        body, name=name + "_s2",
        out_shape=(pltpu.SemaphoreType.DMA((3,)), pltpu.SemaphoreType.DMA((3,)),
                   pltpu.HBM(land_thru.shape, land_thru.dtype), jax.ShapeDtypeStruct((8, LANES), F32)),
        in_specs=(_HBM, _SEM, _SEM, _ANY),
        out_specs=(_SEM, _SEM, _HBM, pl.BlockSpec(memory_space=pltpu.VMEM)),
        input_output_aliases={0: 2},
        compiler_params=pltpu.CompilerParams(has_side_effects=_EFFECT),
    )(land_thru, send_sems, recv_sems, after)
    return (send2, recv2, land2), token


def gather_finish(handle, after, *, name):
    send2, recv2, land2 = handle

    def body(land_ref, send2, recv2, after_ref, got_ref):
        sends, recvs = _second_copies(land_ref, send2, recv2)
        for cp in sends:
            cp.wait_send()
        for cp in recvs:
            cp.wait_recv()

    return pl.pallas_call(
        body, name=name + "_w",
        out_shape=pltpu.HBM(land2.shape, land2.dtype),
        in_specs=(_HBM, _SEM, _SEM, _ANY),
        out_specs=_HBM,
        input_output_aliases={0: 0},
        compiler_params=pltpu.CompilerParams(has_side_effects=_EFFECT),
    )(land2, send2, recv2, after)


def _slab_copies(g_ref, r_ref, send_sems, recv_sems):
    x, y, c = lax.axis_index("x"), lax.axis_index("y"), lax.axis_index("c")
    me = 4 * x + 2 * y + c
    sends, recvs = [], []
    for k in range(1, N_DEV):
        px, py, pc = _flip(x, (k >> 2) & 1), _flip(y, (k >> 1) & 1), _flip(c, k & 1)
        peer = 4 * px + 2 * py + pc
        common = dict(send_sem=send_sems.at[k - 1], recv_sem=recv_sems.at[k - 1], device_id=(px, py, pc),
                      device_id_type=pl.DeviceIdType.MESH)
        sends.append(pltpu.make_async_remote_copy(src_ref=g_ref.at[peer], dst_ref=r_ref.at[me], **common))
        recvs.append(pltpu.make_async_remote_copy(src_ref=g_ref.at[peer], dst_ref=r_ref.at[peer], **common))
    return sends, recvs


def exchange_start(slabs, *, name):
    land = lax.empty(slabs.shape, slabs.dtype)

    def body(g_ref, r_ref, send_sems, recv_sems, g_thru, r_thru, token):
        sends, _ = _slab_copies(g_ref, r_ref, send_sems, recv_sems)
        for cp in sends:
            cp.start()
        token[...] = jnp.zeros_like(token)

    send_sems, recv_sems, g_thru, r_thru, token = pl.pallas_call(
        body, name=name,
        out_shape=(pltpu.SemaphoreType.DMA((N_DEV - 1,)), pltpu.SemaphoreType.DMA((N_DEV - 1,)),
                   pltpu.HBM(slabs.shape, slabs.dtype), pltpu.HBM(slabs.shape, slabs.dtype),
                   jax.ShapeDtypeStruct((8, LANES), F32)),
        in_specs=(_HBM, _HBM),
        out_specs=(_SEM, _SEM, _HBM, _HBM, pl.BlockSpec(memory_space=pltpu.VMEM)),
        input_output_aliases={0: 2, 1: 3},
        compiler_params=pltpu.CompilerParams(has_side_effects=_EFFECT),
    )(pltpu.with_memory_space_constraint(slabs, pltpu.HBM), pltpu.with_memory_space_constraint(land, pltpu.HBM))
    return (send_sems, recv_sems, g_thru, r_thru), token


def exchange_finish(handle, after, *, name):
    send_sems, recv_sems, g_thru, r_thru = handle

    def body(g_ref, r_ref, send_sems, recv_sems, after_ref, g_out, r_out):
        sends, recvs = _slab_copies(g_ref, r_ref, send_sems, recv_sems)
        for cp in sends:
            cp.wait_send()
        for cp in recvs:
            cp.wait_recv()

    return pl.pallas_call(
        body, name=name,
        out_shape=(pltpu.HBM(g_thru.shape, g_thru.dtype), pltpu.HBM(r_thru.shape, r_thru.dtype)),
        in_specs=(_HBM, _HBM, _SEM, _SEM, _ANY),
        out_specs=(_HBM, _HBM),
        input_output_aliases={0: 0, 1: 1},
        compiler_params=pltpu.CompilerParams(has_side_effects=_EFFECT),
    )(g_thru, r_thru, send_sems, recv_sems, after)


def adamw_own(own, got, me, w, m, v, layer, prev, *, name):
    P, R, C = got.shape
    L = w.shape[0]
    tr = _tile(R, max(16, (256 * 1024) // C), 16)

    def body(me_ref, own_ref, p_ref, w_ref, m_ref, v_ref, *rest):
        g_ref, d_ref, mo_ref, vo_ref = rest[-4:]
        mine = own_ref[...].astype(F32)
        g = None
        for i in range(P):
            term = jnp.where(me_ref[0] == i, mine, p_ref[i].astype(F32))
            g = term if g is None else g + term
        m_new = ADAM_B1 * m_ref[...] + (1.0 - ADAM_B1) * g
        v_new = ADAM_B2 * v_ref[...] + (1.0 - ADAM_B2) * jnp.square(g)
        m_hat = m_new / (1.0 - ADAM_B1 ** ADAM_STEP)
        v_hat = v_new / (1.0 - ADAM_B2 ** ADAM_STEP)
        g_ref[...] = g
        d_ref[...] = -ADAM_LR * (m_hat / (jnp.sqrt(v_hat) + ADAM_EPS) + ADAM_WD * w_ref[...])
        mo_ref[...] = m_new
        vo_ref[...] = v_new

    blk = pl.BlockSpec((None, tr, C), lambda i, me_ref: (layer, i, 0))
    out = jax.ShapeDtypeStruct((L, R, C), F32)
    in_specs = [pl.BlockSpec((None, tr, C), lambda i, me_ref: (me_ref[0], i, 0)),
                pl.BlockSpec((P, tr, C), lambda i, me_ref: (0, i, 0)), blk, blk, blk]
    args = [me.reshape(1).astype(jnp.int32), own, got, w, m, v]
    aliases = {}
    if prev is not None:
        in_specs += [pl.BlockSpec(memory_space=pl.ANY)] * 4
        aliases = {len(args) + i: i for i in range(4)}
        args += list(prev)
    return pl.pallas_call(
        body,
        out_shape=(out, out, out, out),
        grid_spec=pltpu.PrefetchScalarGridSpec(
            num_scalar_prefetch=1,
            grid=(R // tr,),
            in_specs=in_specs,
            out_specs=(blk, blk, blk, blk),
        ),
        input_output_aliases=aliases,
        compiler_params=_params("parallel"),
        name=name,
    )(*args)


def _pad_rows(a, rows):
    return jnp.pad(a, ((0, rows - a.shape[0]), (0, 0)))


def _pad_cols(a, cols):
    return jnp.pad(a, ((0, 0), (0, cols - a.shape[1])))


def kernel(x, meta, a_norm, a_w_in, a_conv, a_w_out, kv_norm, w_kv, k_norm, w_f, b_f, b_norm, b_w_q, b_q_norm, b_w_o, ffn_norm, ffn_w_gu, ffn_w_down, loss_target, m_meta, m_a_norm, m_a_w_in, m_a_conv, m_a_w_out, m_kv_norm, m_w_kv, m_k_norm, m_w_f, m_b_f, m_b_norm, m_b_w_q, m_b_q_norm, m_b_w_o, m_ffn_norm, m_ffn_w_gu, m_ffn_w_down, v_meta, v_a_norm, v_a_w_in, v_a_conv, v_a_w_out, v_kv_norm, v_w_kv, v_k_norm, v_w_f, v_b_f, v_b_norm, v_b_w_q, v_b_q_norm, v_b_w_o, v_ffn_norm, v_ffn_w_gu, v_ffn_w_down):
    S, D = x.shape[1], x.shape[2]
    n_meta = meta.shape[0]
    Ds = meta.shape[1]
    H = D // HEAD_DIM
    n_a, n_b = a_w_in.shape[0], b_w_q.shape[0]
    depth = n_a + n_b
    Fs = ffn_w_down.shape[1]
    pad = BLOCK - n_meta
    lead = pad + n_meta
    T = lead + S
    tk_attn = _tile(T, 384, LANES)
    nk_attn = T // tk_attn
    q_scale = 1.0 / math.sqrt(HEAD_DIM)
    my = 4 * lax.axis_index("x") + 2 * lax.axis_index("y") + lax.axis_index("c")

    wf_t = w_f.reshape(H, Ds)
    small = jnp.concatenate([meta, _pad_rows(a_norm, 8), _pad_rows(a_conv.reshape(n_a * 3, Ds), 8), wf_t], axis=0)
    r_an, r_ac, r_wf = n_meta, n_meta + 8, n_meta + 16
    gs = all_gather(small, name="ag_small")
    unshard = lambda blk: jnp.transpose(blk, (1, 0, 2)).reshape(blk.shape[1], D)
    meta_full = unshard(gs[:, 0:n_meta])
    a_norm_full = unshard(gs[:, r_an:r_an + n_a])
    a_conv_full = unshard(gs[:, r_ac:r_ac + 3 * n_a]).reshape(n_a, 3, D)
    w_f_full = gs[:, r_wf:r_wf + H].reshape(D, H)
    wf_pad = _pad_cols(w_f_full, LANES).astype(BF)[None]
    bf_pad = _pad_cols(b_f.reshape(1, H), LANES)

    def layer_shards(l):
        if l < n_a:
            mix = [(("in", l), a_w_in[l]), (("out", l), a_w_out[l])]
        else:
            j = l - n_a
            mix = ([(("kv", 0), w_kv)] if j == 0 else []) + [(("q", j), b_w_q[j]), (("o", j), b_w_o[j])]
        return mix + [(("gu", l), ffn_w_gu[l]), (("dn", l), ffn_w_down[l])]

    first_level, second_level, W = {}, {}, {}
    st = {"done": None, "tok": None}

    def note(val):
        st["done"] = val
        return val

    def take():
        tok, st["tok"] = st["tok"], None
        return tok

    def chain_after(default):
        if st["tok"] is not None:
            return st["tok"]
        return default if st["done"] is None else st["done"]

    def ag_name(key):
        return f"ag_{key[0]}{key[1]}"

    def start_layer(l):
        for key, shard in layer_shards(l):
            first_level[key], st["tok"] = gather_start(shard, my, chain_after(shard), name=ag_name(key))

    def pass_on(keys):
        for key in keys:
            second_level[key], st["tok"] = gather_mid(first_level.pop(key), chain_after(None), name=ag_name(key))

    def weight(key, shape=None):
        w = gather_finish(second_level.pop(key), st["done"], name=ag_name(key))
        W[key] = w if shape is None else w.reshape(shape)
        return W[key]

    def layer_keys(l):
        keys = [key for key, _ in layer_shards(l)]
        return keys[:-2], keys[-2:]

    h = note(jnp.concatenate([jnp.zeros((pad, D), F32), meta_full, x[0]], axis=0))
    start_layer(0)
    pass_on(layer_keys(0)[0])
    saved = []
    shared = None
    for l in range(depth):
        rec = {"h": h}
        mix_keys, ffn_keys = layer_keys(l)

        def ahead():
            if l >= 1:
                pass_on(ffn_keys[:1])
            if l + 1 < depth:
                start_layer(l + 1)

        if l < n_a:
            xn = note(rms_fwd(h, a_norm_full[l], name=f"a{l}_norm", dep=take()))
            ahead()
            proj = note(mm_nn(xn, weight(("in", l)), name=f"a{l}_in", dep=take()))
            if l == 0:
                pass_on(ffn_keys[:1])
            y = note(conv_fwd(proj, a_conv_full[l], name=f"a{l}_conv"))
            h1 = note(mm_nn(y, weight(("out", l), (1, D, D)), add=h, name=f"a{l}_out", dep=take()))
            rec.update(xn=xn, proj=proj, y=y)
        else:
            j = l - n_a
            if j == 0:
                xnk = note(rms_fwd(h, kv_norm, name="kv_norm", dep=take()))
                ahead()
                kv = note(mm_nn(xnk, weight(("kv", 0)), name="kv_proj", dep=take()))
                k, v = kv_post(kv, k_norm, name="kv_post")
                logits = mm_nn(xnk, wf_pad, name="f_logits", tn_target=LANES)
                cfull = fgate_fwd(logits, bf_pad, pad, name="f_gate")
                c_t = jnp.transpose(cfull[:, :H])
                crow = c_t.reshape(H, nk_attn, 1, tk_attn)
                ccol = c_t.reshape(H, T, 1)
                shared = dict(h=h, xnk=xnk, kv=kv, logits=logits)
                xn = note(rms_fwd(h, b_norm[j], name=f"b{j}_norm"))
            else:
                xn = note(rms_fwd(h, b_norm[j], name=f"b{j}_norm", dep=take()))
                ahead()
            qraw = note(mm_nn(xn, weight(("q", j), (1, D, D)), name=f"b{j}_q", dep=take()))
            q = hn_fwd(qraw, b_q_norm[j], q_scale, name=f"b{j}_qnorm")
            o, lse = attn_fwd(q, k, v, crow, pad, name=f"b{j}_attn")
            note(o)
            h1 = note(mm_nn(o, weight(("o", j), (1, D, D)), add=h, name=f"b{j}_o"))
            rec.update(xn=xn, qraw=qraw, q=q, o=o, lse=lse)
        xn2 = note(rms_fwd(h1, ffn_norm[l], name=f"f{l}_norm", dep=take()))
        pass_on(ffn_keys[1:])
        act, g_s, u_s = mm_swiglu(xn2, weight(("gu", l)), name=f"f{l}_gu", dep=take())
        note(act)
        if l + 1 < depth:
            pass_on(layer_keys(l + 1)[0])
        h = note(mm_nn(act, weight(("dn", l), (1, N_DEV * Fs, D)), add=h1, name=f"f{l}_down", resident=True,
                       tm_target=528, tn_target=1024, dep=take()))
        rec.update(h1=h1, xn2=xn2, act=act, g=g_s, u=u_s)
        saved.append(rec)

    dh, dhb, loss_tile = loss_head(h, loss_target[0], lead, name="loss")
    loss = lax.psum(loss_tile[0, 0], MESH_AXES)

    upd = {}
    small_g = {}
    inflight = []

    def big(name, l, section, slabs, w, m, v):
        handle, st["tok"] = exchange_start(slabs.reshape(N_DEV, -1, w.shape[-1]), name=f"{name}{l}_xs")
        inflight.append((section, name, l, handle, w, m, v))

    def land(sections, after):
        for entry in [e for e in inflight if sections is None or e[0] in sections]:
            inflight.remove(entry)
            _, name, l, handle, w, m, v = entry
            own, got = exchange_finish(handle, after, name=f"{name}{l}_xw")
            flat = lambda t: t.reshape(w.shape[0], -1, w.shape[-1])
            upd[name] = adamw_own(own, got, my, flat(w), flat(m), flat(v), l, upd.get(name),
                                  name=f"{name}{l}_adamw")

    dk = dv = dck = dcq = None
    for l in reversed(range(depth)):
        rec = saved[l]
        land([("ffn", l + 1)], dh)
        dgu = mm_nt_dswiglu(dhb, W[("dn", l)], rec["g"], rec["u"], name=f"f{l}_ddown")
        big("ffn_w_down", l, ("ffn", l), mm_tn(rec["act"], dhb, 1, name=f"f{l}_wdown"),
            ffn_w_down, m_ffn_w_down, v_ffn_w_down)
        big("ffn_w_gu", l, ("ffn", l), mm_tn(rec["xn2"], dgu, N_DEV, name=f"f{l}_wgu", dep=take()),
            ffn_w_gu, m_ffn_w_gu, v_ffn_w_gu)
        dxn2 = mm_nt(dgu, W[("gu", l)], name=f"f{l}_dgu", gb=2, dep=take())
        dh1, dhb, dgf = rms_bwd(dxn2, rec["h1"], ffn_norm[l], dh, name=f"f{l}_dnorm")
        small_g[("ffn_norm", l)] = dgf
        land([("mix", l + 1)], dh1)
        if l < n_a:
            dy = mm_nt(dhb, W[("out", l)], name=f"a{l}_dout")
            big("a_w_out", l, ("mix", l), mm_tn(rec["y"], dhb, 1, name=f"a{l}_wout"),
                a_w_out, m_a_w_out, v_a_w_out)
            db, dc, dhh, dcw = conv_bwd(dy, rec["proj"], a_conv_full[l], name=f"a{l}_dconv", dep=take())
            small_g[("a_conv", l)] = dcw
            dproj = jnp.concatenate([db, dc, dhh], axis=1)
            big("a_w_in", l, ("mix", l), mm_tn(rec["xn"], dproj, N_DEV, name=f"a{l}_win"),
                a_w_in, m_a_w_in, v_a_w_in)
            dxn = mm_nt(dproj, W[("in", l)], name=f"a{l}_din", gb=4, dep=take())
            dh, dhb, dga = rms_bwd(dxn, rec["h"], a_norm_full[l], dh1, name=f"a{l}_dnorm")
            small_g[("a_norm", l)] = dga
        else:
            j = l - n_a
            do = mm_nt(dhb, W[("o", j)], name=f"b{j}_do")
            big("b_w_o", j, ("mix", l), mm_tn(rec["o"], dhb, 1, name=f"b{j}_wo"),
                b_w_o, m_b_w_o, v_b_w_o)
            prev = None if dk is None else (dk, dv, dck, dcq)
            dq, dk, dv, dck, dcq = attn_bwd(rec["q"], k, v, do, rec["o"],
                                            rec["lse"].reshape(H, nk_attn, 1, tk_attn), ccol, prev, pad, tk_attn,
                                            name=f"b{j}_dattn", dep=take())
            dqraw, dqn = hn_bwd(dq, rec["qraw"], b_q_norm[j], q_scale, name=f"b{j}_dqnorm")
            small_g[("b_q_norm", j)] = dqn
            big("b_w_q", j, ("mix", l), mm_tn(rec["xn"], dqraw, 1, name=f"b{j}_wq"),
                b_w_q, m_b_w_q, v_b_w_q)
            dxn = mm_nt(dqraw, W[("q", j)], name=f"b{j}_dq", dep=take())
            dh, dhb, dgb = rms_bwd(dxn, rec["h"], b_norm[j], dh1, name=f"b{j}_dnorm")
            small_g[("b_norm", j)] = dgb
            if j == 0:
                dkraw, dkn = hn_bwd(dk, shared["kv"], k_norm, 1.0, name="kv_dknorm")
                dkv = jnp.concatenate([dkraw, dv.astype(BF)], axis=1)
                dc_full = _pad_cols(jnp.transpose(dck.reshape(H, T) + dcq.reshape(H, T)), LANES)
                dz, dbf = fgate_bwd(dc_full, shared["logits"], bf_pad, pad, name="f_dgate")
                big("w_kv", 0, ("mix", l), mm_tn(shared["xnk"], dkv, N_DEV, name="kv_wkv"),
                    w_kv[None], m_w_kv[None], v_w_kv[None])
                dwf_t = mm_tn(dz, shared["xnk"], 1, name="f_wf", out_dtype=F32, tn_target=1024,
                              dep=take())[0, :H]
                dxn_f = mm_nt(dz, wf_pad, name="f_dxn")
                dxnk = mm_nt(dkv, W[("kv", 0)], add=dxn_f, name="kv_dxn", gb=4)
                dh, dhb, dgkv = rms_bwd(dxnk, shared["h"], kv_norm, dh, name="kv_dnorm")
    grad_x = dh[lead:][None]

    row8 = lambda a: _pad_rows(_pad_cols(a, D), 8)
    stack = lambda key, n: jnp.concatenate([small_g[(key, i)] for i in range(n)], axis=0)
    g_sharded = jnp.concatenate([dh[pad:lead], row8(stack("a_norm", n_a)), row8(stack("a_conv", n_a)), dwf_t], axis=0)
    g_repl = jnp.concatenate([row8(jnp.concatenate([dgkv, stack("b_norm", n_b)], axis=0)),
                              row8(stack("ffn_norm", depth)),
                              row8(jnp.concatenate([_pad_cols(dkn, D), _pad_cols(stack("b_q_norm", n_b), D),
                                                    _pad_cols(dbf[:, :H], D)], axis=0))], axis=0)
    n_sh = g_sharded.shape[0]
    pack = jnp.concatenate([g_sharded, g_repl], axis=0)
    small_handle, small_tok = exchange_start(jnp.broadcast_to(pack[None], (N_DEV,) + pack.shape),
                                             name="small_grads_xs")
    land(None, small_tok)
    _, got_small = exchange_finish(small_handle, upd["a_w_in"][0], name="small_grads_xw")
    mine_sel = (jnp.arange(N_DEV) == my).reshape(N_DEV, 1, 1)
    gathered = jnp.where(mine_sel, pack[None], got_small)
    parts_sh = lax.dynamic_slice_in_dim(gathered[:, :n_sh], my * Ds, Ds, axis=2)
    parts_rp = gathered[:, n_sh:]

    def pack_sh(t_meta, t_an, t_ac, t_wf):
        return jnp.concatenate([t_meta, _pad_rows(t_an, 8), _pad_rows(t_ac.reshape(n_a * 3, Ds), 8),
                                jnp.transpose(t_wf)], axis=0)

    def pack_rp(t_kv, t_bn, t_fn, t_kn, t_qn, t_bf):
        return jnp.concatenate([row8(jnp.concatenate([t_kv.reshape(1, D), t_bn], axis=0)), row8(t_fn),
                                row8(jnp.concatenate([_pad_cols(t_kn.reshape(1, -1), D), _pad_cols(t_qn, D),
                                                      _pad_cols(t_bf.reshape(1, -1), D)], axis=0))], axis=0)

    res_sh = adamw(parts_sh, pack_sh(meta, a_norm, a_conv, w_f), pack_sh(m_meta, m_a_norm, m_a_conv, m_w_f),
                   pack_sh(v_meta, v_a_norm, v_a_conv, v_w_f), name="small_sharded_adamw")
    res_rp = adamw(parts_rp, pack_rp(kv_norm, b_norm, ffn_norm, k_norm, b_q_norm, b_f),
                   pack_rp(m_kv_norm, m_b_norm, m_ffn_norm, m_k_norm, m_b_q_norm, m_b_f),
                   pack_rp(v_kv_norm, v_b_norm, v_ffn_norm, v_k_norm, v_b_q_norm, v_b_f), name="small_repl_adamw")

    def unpack(kind):
        sh, rp = res_sh[kind], res_rp[kind]
        out = {
            "meta": sh[0:n_meta],
            "a_norm": sh[r_an:r_an + n_a],
            "a_conv": sh[r_ac:r_ac + 3 * n_a].reshape(n_a, 3, Ds),
            "w_f": jnp.transpose(sh[r_wf:r_wf + H]),
            "kv_norm": rp[0],
            "b_norm": rp[1:1 + n_b],
            "ffn_norm": rp[8:8 + depth],
            "k_norm": rp[16, :HEAD_DIM],
            "b_q_norm": rp[17:17 + n_b, :HEAD_DIM],
            "b_f": rp[17 + n_b, :H],
        }
        for name, like in (("a_w_in", a_w_in), ("a_w_out", a_w_out), ("b_w_q", b_w_q), ("b_w_o", b_w_o),
                           ("ffn_w_gu", ffn_w_gu), ("ffn_w_down", ffn_w_down)):
            out[name] = upd[name][kind].reshape(like.shape)
        out["w_kv"] = upd["w_kv"][kind].reshape(w_kv.shape)
        return out

    order = ["meta", "a_norm", "a_w_in", "a_conv", "a_w_out", "kv_norm", "w_kv", "k_norm", "w_f", "b_f",
             "b_norm", "b_w_q", "b_q_norm", "b_w_o", "ffn_norm", "ffn_w_gu", "ffn_w_down"]
    outs = [loss, grad_x]
    for kind in range(4):
        vals = unpack(kind)
        outs += [vals[n] for n in order]
    return tuple(outs)
```

```python
import functools
import math

import jax
import jax.numpy as jnp
from jax import lax
from jax.experimental import pallas as pl
from jax.experimental.pallas import tpu as pltpu

N_DEV = 8
MESH_AXES = ("x", "y", "c")
EPS = 1e-6
NEG = -1e30
HEAD_DIM = 128
BLOCK = 128
LANES = 128
V7X_VMEM_LIMIT = 56 * 1024 * 1024

ADAM_LR = 0.001
ADAM_B1 = 0.9
ADAM_B2 = 0.999
ADAM_EPS = 1e-08
ADAM_WD = 0.01
ADAM_STEP = 10

BF = jnp.bfloat16
F32 = jnp.float32


def _tile(n, target, mult):
    best = None
    for t in range(mult, min(n, target) + 1, mult):
        if n % t == 0:
            best = t
    return n if best is None else best


def _params(*sem):
    return pltpu.CompilerParams(dimension_semantics=sem, vmem_limit_bytes=V7X_VMEM_LIMIT)


def _with_dep(body, in_specs, args, dep):
    if dep is None:
        return body, list(in_specs), list(args)
    n_in = len(args)

    def body_dep(*refs):
        body(*refs[:n_in], *refs[n_in + 1:])

    return body_dep, list(in_specs) + [pl.BlockSpec(memory_space=pl.ANY)], list(args) + [dep]


def mm_nn(a, w, *, name, add=None, dep=None, out_dtype=F32, tm_target=1056, tn_target=1024, tk_target=2048,
          resident=False):
    M, K = a.shape
    G, K2, n = w.shape
    assert K == K2
    tm = _tile(M, tm_target, 16)
    tn = _tile(n, tn_target, LANES)
    tk = K if resident else _tile(K, tk_target, LANES)
    nj, nk = n // tn, K // tk
    has_add = add is not None
    if resident:
        grid, sem = (G * nj, M // tm), ("parallel", "parallel")
        order = lambda f: (lambda j, i: f(i, j, 0))
        w_mode = dict(pipeline_mode=pl.Buffered(2))
    else:
        grid, sem = (M // tm, G * nj, nk), ("parallel", "parallel", "arbitrary")
        order = lambda f: f
        w_mode = {}

    def body(*refs):
        if has_add:
            a_ref, w_ref, add_ref, o_ref = refs[:4]
        else:
            a_ref, w_ref, o_ref = refs[:3]
            add_ref = None

        def finish(r):
            if has_add:
                r = r + add_ref[...]
            o_ref[...] = r.astype(out_dtype)

        part = jnp.dot(a_ref[...], w_ref[...], preferred_element_type=F32)
        if nk == 1:
            finish(part)
        else:
            acc_ref = refs[-1]
            k = pl.program_id(2)

            @pl.when(k == 0)
            def _():
                acc_ref[...] = part

            @pl.when(k > 0)
            def _():
                acc_ref[...] += part

            @pl.when(k == nk - 1)
            def _():
                finish(acc_ref[...])

    in_specs = [
        pl.BlockSpec((tm, tk), order(lambda i, j, k: (i, k))),
        pl.BlockSpec((None, tk, tn), order(lambda i, j, k: (j // nj, k, j % nj)), **w_mode),
    ]
    args = [a, w]
    if has_add:
        in_specs.append(pl.BlockSpec((tm, tn), order(lambda i, j, k: (i, j))))
        args.append(add)
    body, in_specs, args = _with_dep(body, in_specs, args, dep)
    return pl.pallas_call(
        body,
        out_shape=jax.ShapeDtypeStruct((M, G * n), out_dtype),
        grid=grid,
        in_specs=in_specs,
        out_specs=pl.BlockSpec((tm, tn), order(lambda i, j, k: (i, j))),
        scratch_shapes=[pltpu.VMEM((tm, tn), F32)] if nk > 1 else [],
        compiler_params=_params(*sem),
        name=name,
    )(*args)


def mm_swiglu(xn, wgu, *, name, dep=None, save_dtype=BF, tm_target=528):
    M, K = xn.shape
    G, _, n = wgu.shape
    half = G // 2
    tm = _tile(M, tm_target, 16)
    tn = _tile(n, 1408, LANES)
    nj = n // tn
    Fh = half * n

    def body(a_ref, wg_ref, wu_ref, act_ref, silu_ref, udsilu_ref):
        a = a_ref[...]
        g = jnp.dot(a, wg_ref[...], preferred_element_type=F32)
        sig = jax.nn.sigmoid(g)
        silu = g * sig
        silu_ref[...] = silu.astype(save_dtype)
        dsilu = sig * (1.0 + g * (1.0 - sig))
        u = jnp.dot(a, wu_ref[...], preferred_element_type=F32)
        udsilu_ref[...] = (u * dsilu).astype(save_dtype)
        act_ref[...] = (silu * u).astype(BF)

    out_block = pl.BlockSpec((tm, tn), lambda j, i: (i, j))
    once = pl.Buffered(1)
    body, in_specs, args = _with_dep(body, [
        pl.BlockSpec((tm, K), lambda j, i: (i, 0)),
        pl.BlockSpec((None, K, tn), lambda j, i: (j // nj, 0, j % nj), pipeline_mode=once),
        pl.BlockSpec((None, K, tn), lambda j, i: (half + j // nj, 0, j % nj), pipeline_mode=once),
    ], [xn, wgu, wgu], dep)
    return pl.pallas_call(
        body,
        out_shape=(jax.ShapeDtypeStruct((M, Fh), BF),
                   jax.ShapeDtypeStruct((M, Fh), save_dtype),
                   jax.ShapeDtypeStruct((M, Fh), save_dtype)),
        grid=(half * nj, M // tm),
        in_specs=in_specs,
        out_specs=(out_block, out_block, out_block),
        compiler_params=_params("parallel", "parallel"),
        name=name,
    )(*args)


def mm_nt(dy, w, *, name, add=None, dep=None, out_dtype=F32, tm_target=1056, tko_target=1024, tc_target=2048,
          gb=1):
    if dy.ndim == 2:
        dy = dy.reshape(1, *dy.shape)
    P, M, Np = dy.shape
    G, K, n = w.shape
    assert P * Np == G * n
    tm = _tile(M, tm_target, 16)
    tko = _tile(K, tko_target, LANES)
    tc = _tile(n, tc_target, LANES)
    nc = n // tc
    gb = gb if nc == 1 else 1
    assert G % gb == 0 and Np % (gb * tc) == 0
    steps = (G // gb) * nc
    per_part = Np // (gb * tc)
    has_add = add is not None

    def body(*refs):
        if has_add:
            dy_ref, w_ref, add_ref, o_ref = refs[:4]
        else:
            dy_ref, w_ref, o_ref = refs[:3]
            add_ref = None

        def finish(r):
            if has_add:
                r = r + add_ref[...]
            o_ref[...] = r.astype(out_dtype)

        part = None
        for g in range(gb):
            term = lax.dot_general(dy_ref[:, g * tc:(g + 1) * tc], w_ref[g], (((1,), (1,)), ((), ())),
                                   preferred_element_type=F32)
            part = term if part is None else part + term
        if steps == 1:
            finish(part)
        else:
            acc_ref = refs[-1]
            s = pl.program_id(2)

            @pl.when(s == 0)
            def _():
                acc_ref[...] = part

            @pl.when(s > 0)
            def _():
                acc_ref[...] += part

            @pl.when(s == steps - 1)
            def _():
                finish(acc_ref[...])

    in_specs = [
        pl.BlockSpec((None, tm, gb * tc), lambda i, o, s: (s // per_part, i, s % per_part)),
        pl.BlockSpec((gb, tko, tc), lambda i, o, s: (s // nc, o, s % nc)),
    ]
    args = [dy, w]
    if has_add:
        in_specs.append(pl.BlockSpec((tm, tko), lambda i, o, s: (i, o)))
        args.append(add)
    body, in_specs, args = _with_dep(body, in_specs, args, dep)
    return pl.pallas_call(
        body,
        out_shape=jax.ShapeDtypeStruct((M, K), out_dtype),
        grid=(M // tm, K // tko, steps),
        in_specs=in_specs,
        out_specs=pl.BlockSpec((tm, tko), lambda i, o, s: (i, o)),
        scratch_shapes=[pltpu.VMEM((tm, tko), F32)] if steps > 1 else [],
        compiler_params=_params("parallel", "parallel", "arbitrary"),
        name=name,
    )(*args)


def mm_nt_dswiglu(dh, w_down, g_s, u_s, *, name, tm_target=1056, tf_target=512):
    M, D = dh.shape
    _, Fh, D2 = w_down.shape
    assert D == D2
    tm = _tile(M, tm_target, 16)
    tf = _tile(Fh, tf_target, LANES)

    def body(dh_ref, w_ref, silu_ref, udsilu_ref, dgu_ref):
        dact = lax.dot_general(dh_ref[...], w_ref[...], (((1,), (1,)), ((), ())),
                               preferred_element_type=F32)
        dgu_ref[1] = (dact * silu_ref[...].astype(F32)).astype(BF)
        dgu_ref[0] = (dact * udsilu_ref[...].astype(F32)).astype(BF)

    blk = pl.BlockSpec((tm, tf), lambda i, f: (i, f))
    return pl.pallas_call(
        body,
        out_shape=jax.ShapeDtypeStruct((2, M, Fh), BF),
        grid=(M // tm, Fh // tf),
        in_specs=[
            pl.BlockSpec((tm, D), lambda i, f: (i, 0)),
            pl.BlockSpec((None, tf, D), lambda i, f: (0, f, 0)),
            blk, blk,
        ],
        out_specs=pl.BlockSpec((2, tm, tf), lambda i, f: (0, i, f)),
        compiler_params=_params("parallel", "parallel"),
        name=name,
    )(dh, w_down, g_s, u_s)


def mm_tn(a, dy, groups, *, name, dep=None, out_dtype=BF, tk_target=512, tn_target=1408):
    M, K = a.shape
    if dy.ndim == 2:
        dy = dy.reshape(1, *dy.shape)
    P, M2, Np = dy.shape
    N = P * Np
    assert M == M2 and N % groups == 0
    n = N // groups
    tk = _tile(K, tk_target, LANES)
    tn = _tile(n, tn_target, LANES)
    nj = n // tn
    assert Np % tn == 0
    per_part = Np // tn

    def body(a_ref, dy_ref, o_ref):
        o_ref[...] = lax.dot_general(a_ref[...], dy_ref[...], (((0,), (0,)), ((), ())),
                                     preferred_element_type=F32).astype(out_dtype)

    body, in_specs, args = _with_dep(body, [
        pl.BlockSpec((M, tk), lambda i, j: (0, i)),
        pl.BlockSpec((None, M, tn), lambda i, j: (j // per_part, 0, j % per_part)),
    ], [a, dy], dep)
    return pl.pallas_call(
        body,
        out_shape=jax.ShapeDtypeStruct((groups, K, n), out_dtype),
        grid=(K // tk, groups * nj),
        in_specs=in_specs,
        out_specs=pl.BlockSpec((None, tk, tn), lambda i, j: (j // nj, i, j % nj)),
        compiler_params=_params("parallel", "parallel"),
        name=name,
    )(*args)


def rms_fwd(h, g, *, name, dep=None):
    T, D = h.shape
    tm = _tile(T, 528, 16)

    def body(h_ref, g_ref, o_ref):
        x = h_ref[...]
        r = lax.rsqrt(jnp.mean(x * x, axis=-1, keepdims=True) + EPS)
        o_ref[...] = ((x * r) * g_ref[...]).astype(BF)

    body, in_specs, args = _with_dep(
        body, [pl.BlockSpec((tm, D), lambda i: (i, 0)), pl.BlockSpec((1, D), lambda i: (0, 0))],
        [h, g.reshape(1, D)], dep)
    return pl.pallas_call(
        body,
        out_shape=jax.ShapeDtypeStruct((T, D), BF),
        grid=(T // tm,),
        in_specs=in_specs,
        out_specs=pl.BlockSpec((tm, D), lambda i: (i, 0)),
        compiler_params=_params("parallel"),
        name=name,
    )(*args)


def rms_bwd(dxn, h, g, add, *, name):
    T, D = h.shape
    tm = _tile(T, 384, 16)

    def body(dxn_ref, h_ref, g_ref, add_ref, dh_ref, dhb_ref, dg_ref):
        x = h_ref[...]
        dy = dxn_ref[...]
        r = lax.rsqrt(jnp.mean(x * x, axis=-1, keepdims=True) + EPS)
        xhat = x * r
        part = jnp.sum(dy * xhat, axis=0, keepdims=True)

        @pl.when(pl.program_id(0) == 0)
        def _():
            dg_ref[...] = part

        @pl.when(pl.program_id(0) > 0)
        def _():
            dg_ref[...] += part

        dxh = dy * g_ref[...]
        dh = add_ref[...] + r * (dxh - xhat * jnp.mean(dxh * xhat, axis=-1, keepdims=True))
        dh_ref[...] = dh
        dhb_ref[...] = dh.astype(BF)

    row = pl.BlockSpec((tm, D), lambda i: (i, 0))
    vec = pl.BlockSpec((1, D), lambda i: (0, 0))
    return pl.pallas_call(
        body,
        out_shape=(jax.ShapeDtypeStruct((T, D), F32), jax.ShapeDtypeStruct((T, D), BF),
                   jax.ShapeDtypeStruct((1, D), F32)),
        grid=(T // tm,),
        in_specs=[row, row, vec, row],
        out_specs=(row, row, vec),
        compiler_params=_params("arbitrary"),
        name=name,
    )(dxn, h, g.reshape(1, D), add)


def _head_norm(x, gain):
    r = lax.rsqrt(jnp.mean(x * x, axis=-1, keepdims=True) + EPS)
    return (x * r) * gain


def hn_fwd(qraw, gain, out_scale, *, name):
    T, D = qraw.shape
    H = D // HEAD_DIM
    tm = _tile(T, 528, 16)

    def body(q_ref, g_ref, o_ref):
        gain_v = g_ref[...]
        for hd in range(H):
            sl = slice(hd * HEAD_DIM, (hd + 1) * HEAD_DIM)
            o_ref[:, sl] = (_head_norm(q_ref[:, sl], gain_v) * out_scale).astype(BF)

    return pl.pallas_call(
        body,
        out_shape=jax.ShapeDtypeStruct((T, D), BF),
        grid=(T // tm,),
        in_specs=[pl.BlockSpec((tm, D), lambda i: (i, 0)),
                  pl.BlockSpec((1, HEAD_DIM), lambda i: (0, 0))],
        out_specs=pl.BlockSpec((tm, D), lambda i: (i, 0)),
        compiler_params=_params("parallel"),
        name=name,
    )(qraw, gain.reshape(1, HEAD_DIM))


def kv_post(kv, gain, *, name):
    T, D2 = kv.shape
    D = D2 // 2
    H = D // HEAD_DIM
    tm = _tile(T, 528, 16)

    def body(k_ref, v_ref, g_ref, ko_ref, vo_ref):
        gain_v = g_ref[...]
        for hd in range(H):
            sl = slice(hd * HEAD_DIM, (hd + 1) * HEAD_DIM)
            ko_ref[:, sl] = _head_norm(k_ref[:, sl], gain_v).astype(BF)
        vo_ref[...] = v_ref[...].astype(BF)

    blk = pl.BlockSpec((tm, D), lambda i: (i, 0))
    return pl.pallas_call(
        body,
        out_shape=(jax.ShapeDtypeStruct((T, D), BF), jax.ShapeDtypeStruct((T, D), BF)),
        grid=(T // tm,),
        in_specs=[blk, pl.BlockSpec((tm, D), lambda i: (i, 1)),
                  pl.BlockSpec((1, HEAD_DIM), lambda i: (0, 0))],
        out_specs=(blk, blk),
        compiler_params=_params("parallel"),
        name=name,
    )(kv, kv, gain.reshape(1, HEAD_DIM))


def hn_bwd(dq, qraw, gain, out_scale, *, name):
    T, D = dq.shape
    H = D // HEAD_DIM
    tm = _tile(T, 384, 16)

    def body(dq_ref, q_ref, g_ref, o_ref, dg_ref):
        gain_v = g_ref[...]
        part = jnp.zeros((1, HEAD_DIM), F32)
        for hd in range(H):
            sl = slice(hd * HEAD_DIM, (hd + 1) * HEAD_DIM)
            x = q_ref[:, sl]
            dy = dq_ref[:, sl] * out_scale
            r = lax.rsqrt(jnp.mean(x * x, axis=-1, keepdims=True) + EPS)
            xhat = x * r
            part = part + jnp.sum(dy * xhat, axis=0, keepdims=True)
            dxh = dy * gain_v
            o_ref[:, sl] = (r * (dxh - xhat * jnp.mean(dxh * xhat, axis=-1, keepdims=True))).astype(BF)

        @pl.when(pl.program_id(0) == 0)
        def _():
            dg_ref[...] = part

        @pl.when(pl.program_id(0) > 0)
        def _():
            dg_ref[...] += part

    blk = pl.BlockSpec((tm, D), lambda i: (i, 0))
    vec = pl.BlockSpec((1, HEAD_DIM), lambda i: (0, 0))
    return pl.pallas_call(
        body,
        out_shape=(jax.ShapeDtypeStruct((T, D), BF), jax.ShapeDtypeStruct((1, HEAD_DIM), F32)),
        grid=(T // tm,),
        in_specs=[blk, blk, vec],
        out_specs=(blk, vec),
        compiler_params=_params("arbitrary"),
        name=name,
    )(dq, qraw, gain.reshape(1, HEAD_DIM))


def _shift_down(cur, above, k, rowc):
    out = pltpu.roll(cur, k, 0)
    for i in range(k):
        out = jnp.where(rowc == i, above[8 - k + i:8 - k + i + 1], out)
    return out


def _shift_up(cur, below, k, rowc):
    R = cur.shape[0]
    out = pltpu.roll(cur, R - k, 0)
    for i in range(k):
        out = jnp.where(rowc == R - k + i, below[i:i + 1], out)
    return out


def _conv3(u, u_above, wv, rowc):
    u1 = _shift_down(u, u_above, 1, rowc)
    u2 = _shift_down(u, u_above, 2, rowc)
    return wv[0:1] * u2 + wv[1:2] * u1 + wv[2:3] * u, u1, u2


def conv_fwd(proj, w, *, name):
    T, D3 = proj.shape
    D = D3 // 3
    tc = LANES if D % LANES == 0 else D
    nb = D // tc
    R = _tile(T, 264, 8)

    def body(b_ref, c_ref, h_ref, w_ref, y_ref):
        rowc = lax.broadcasted_iota(jnp.int32, (R, 1), 0)
        wv = w_ref[...]
        for r0 in range(0, T, R):
            rows = slice(r0, r0 + R)
            u = c_ref[rows, :] * h_ref[rows, :]
            if r0 == 0:
                above = jnp.zeros((8, tc), F32)
            else:
                above = c_ref[r0 - 8:r0, :] * h_ref[r0 - 8:r0, :]
            conv, _, _ = _conv3(u, above, wv, rowc)
            y_ref[rows, :] = (b_ref[rows, :] * conv).astype(BF)

    return pl.pallas_call(
        body,
        out_shape=jax.ShapeDtypeStruct((T, D), BF),
        grid=(nb,),
        in_specs=[
            pl.BlockSpec((T, tc), lambda j: (0, j)),
            pl.BlockSpec((T, tc), lambda j: (0, nb + j)),
            pl.BlockSpec((T, tc), lambda j: (0, 2 * nb + j)),
            pl.BlockSpec((3, tc), lambda j: (0, j)),
        ],
        out_specs=pl.BlockSpec((T, tc), lambda j: (0, j)),
        compiler_params=_params("parallel"),
        name=name,
    )(proj, proj, proj, w)


def conv_bwd(dy, proj, w, *, name, dep=None):
    T, D = dy.shape
    tc = LANES if D % LANES == 0 else D
    nb = D // tc
    R = _tile(T, 264, 8)

    def body(dy_ref, b_ref, c_ref, h_ref, w_ref, db_ref, dc_ref, dh_ref, dw_ref):
        rowc = lax.broadcasted_iota(jnp.int32, (R, 1), 0)
        wv = w_ref[...]
        dw = [jnp.zeros((1, tc), F32) for _ in range(3)]
        for r0 in range(0, T, R):
            rows = slice(r0, r0 + R)
            c = c_ref[rows, :]
            hh = h_ref[rows, :]
            u = c * hh
            if r0 == 0:
                above = jnp.zeros((8, tc), F32)
            else:
                above = c_ref[r0 - 8:r0, :] * h_ref[r0 - 8:r0, :]
            conv, u1, u2 = _conv3(u, above, wv, rowc)
            dyv = dy_ref[rows, :]
            db_ref[rows, :] = (dyv * conv).astype(BF)
            dconv = dyv * b_ref[rows, :]
            if r0 + R == T:
                below = jnp.zeros((8, tc), F32)
            else:
                below = dy_ref[r0 + R:r0 + R + 8, :] * b_ref[r0 + R:r0 + R + 8, :]
            dw[0] = dw[0] + jnp.sum(dconv * u2, axis=0, keepdims=True)
            dw[1] = dw[1] + jnp.sum(dconv * u1, axis=0, keepdims=True)
            dw[2] = dw[2] + jnp.sum(dconv * u, axis=0, keepdims=True)
            du = (wv[2:3] * dconv + wv[1:2] * _shift_up(dconv, below, 1, rowc)
                  + wv[0:1] * _shift_up(dconv, below, 2, rowc))
            dc_ref[rows, :] = (du * hh).astype(BF)
            dh_ref[rows, :] = (du * c).astype(BF)
        for i in range(3):
            dw_ref[i:i + 1, :] = dw[i]

    strip = pl.BlockSpec((T, tc), lambda j: (0, j))
    wblk = pl.BlockSpec((3, tc), lambda j: (0, j))
    out = jax.ShapeDtypeStruct((T, D), BF)
    body, in_specs, args = _with_dep(body, [
        strip,
        pl.BlockSpec((T, tc), lambda j: (0, j)),
        pl.BlockSpec((T, tc), lambda j: (0, nb + j)),
        pl.BlockSpec((T, tc), lambda j: (0, 2 * nb + j)),
        wblk,
    ], [dy, proj, proj, proj, w], dep)
    return pl.pallas_call(
        body,
        out_shape=(out, out, out, jax.ShapeDtypeStruct((3, D), F32)),
        grid=(nb,),
        in_specs=in_specs,
        out_specs=(strip, strip, strip, wblk),
        compiler_params=_params("parallel"),
        name=name,
    )(*args)


def _log_sigmoid(z):
    return jnp.minimum(z, 0.0) - jnp.log(1.0 + jnp.exp(-jnp.abs(z)))


def fgate_fwd(logits, bias, pad, *, name):
    T, W = logits.shape
    cb = _tile(T, 128, 8)
    nblk = T // cb

    def body(z_ref, b_ref, c_ref, lf_ref):
        row = lax.broadcasted_iota(jnp.int32, (T, 1), 0)
        lf_ref[...] = jnp.where(row >= pad, _log_sigmoid(z_ref[...] + b_ref[...]), 0.0)
        ri = lax.broadcasted_iota(jnp.int32, (cb, cb), 0)
        ci = lax.broadcasted_iota(jnp.int32, (cb, cb), 1)
        tri = (ci <= ri).astype(F32)

        def step(i, carry):
            rows = pl.ds(pl.multiple_of(i * cb, cb), cb)
            blk = lf_ref[rows, :]
            c_ref[rows, :] = carry + jnp.dot(tri, blk, precision=lax.Precision.HIGHEST,
                                             preferred_element_type=F32)
            return carry + jnp.sum(blk, axis=0, keepdims=True)

        lax.fori_loop(0, nblk, step, jnp.zeros((1, W), F32))

    return pl.pallas_call(
        body,
        out_shape=jax.ShapeDtypeStruct((T, W), F32),
        in_specs=[pl.BlockSpec(memory_space=pltpu.VMEM), pl.BlockSpec(memory_space=pltpu.VMEM)],
        out_specs=pl.BlockSpec(memory_space=pltpu.VMEM),
        scratch_shapes=[pltpu.VMEM((T, W), F32)],
        compiler_params=pltpu.CompilerParams(vmem_limit_bytes=V7X_VMEM_LIMIT),
        name=name,
    )(logits, bias)


def fgate_bwd(dc, logits, bias, pad, *, name):
    T, W = logits.shape
    cb = _tile(T, 128, 8)
    nblk = T // cb

    def body(dc_ref, z_ref, b_ref, dz_ref, db_ref, rs_ref):
        ri = lax.broadcasted_iota(jnp.int32, (cb, cb), 0)
        ci = lax.broadcasted_iota(jnp.int32, (cb, cb), 1)
        triu = (ci >= ri).astype(F32)

        def step(i, carry):
            rows = pl.ds(pl.multiple_of((nblk - 1 - i) * cb, cb), cb)
            blk = dc_ref[rows, :]
            rs_ref[rows, :] = carry + jnp.dot(triu, blk, precision=lax.Precision.HIGHEST,
                                              preferred_element_type=F32)
            return carry + jnp.sum(blk, axis=0, keepdims=True)

        lax.fori_loop(0, nblk, step, jnp.zeros((1, W), F32))
        row = lax.broadcasted_iota(jnp.int32, (T, 1), 0)
        z = z_ref[...] + b_ref[...]
        dz = jnp.where(row >= pad, rs_ref[...] * jax.nn.sigmoid(-z), 0.0)
        dz_ref[...] = dz.astype(BF)
        db_ref[...] = jnp.sum(dz, axis=0, keepdims=True)

    vm = pl.BlockSpec(memory_space=pltpu.VMEM)
    return pl.pallas_call(
        body,
        out_shape=(jax.ShapeDtypeStruct((T, W), BF), jax.ShapeDtypeStruct((1, W), F32)),
        in_specs=[vm, vm, vm],
        out_specs=(vm, vm),
        scratch_shapes=[pltpu.VMEM((T, W), F32)],
        compiler_params=pltpu.CompilerParams(vmem_limit_bytes=V7X_VMEM_LIMIT),
        name=name,
    )(dc, logits, bias)


def _scores(qb, kb, ck):
    return lax.dot_general(qb, kb, (((1,), (1,)), ((), ())), preferred_element_type=F32) - ck


def _causal(s, row, col, pad):
    return jnp.where((col <= row) & (col >= pad), s, NEG)


def attn_fwd(q, k, v, crow, pad, *, name):
    T, D = q.shape
    H = D // HEAD_DIM
    nk, tk = crow.shape[1], crow.shape[3]
    tq = tk
    nq = T // tq

    hp = 2 if H % 2 == 0 else 1
    wide = hp * HEAD_DIM

    def body(q_ref, k_ref, v_ref, cr_ref, o_ref, lse_ref):
        qi = pl.program_id(1)
        row = qi * tq + lax.broadcasted_iota(jnp.int32, (tq, 1), 0)
        heads = [slice(a * HEAD_DIM, (a + 1) * HEAD_DIM) for a in range(hp)]
        qbs = [q_ref[:, sl] for sl in heads]

        def step(kc, carry, masked):
            rows = pl.ds(pl.multiple_of(kc * tk, tk), tk)
            out = []
            for a, sl in enumerate(heads):
                m, l, acc = carry[a]
                s = _scores(qbs[a], k_ref[rows, sl], cr_ref[a, kc])
                if masked:
                    s = _causal(s, row, kc * tk + lax.broadcasted_iota(jnp.int32, (1, tk), 1), pad)
                m_new = jnp.maximum(m, jnp.max(s, axis=-1, keepdims=True))
                alpha = jnp.exp(m - m_new)
                p = jnp.exp(s - m_new)
                l = alpha * l + jnp.sum(p, axis=-1, keepdims=True)
                acc = alpha * acc + jnp.dot(p.astype(BF), v_ref[rows, sl], preferred_element_type=F32)
                out.append((m_new, l, acc))
            return tuple(out)

        init = tuple((jnp.full((tq, 1), NEG, F32), jnp.zeros((tq, 1), F32), jnp.zeros((tq, HEAD_DIM), F32))
                     for _ in heads)
        carry = step(0, init, True)
        carry = lax.fori_loop(1, qi, lambda kc, c: step(kc, c, False), carry)
        carry = lax.cond(qi > 0, lambda c: step(qi, c, True), lambda c: c, carry)
        valid = row >= pad
        for a, sl in enumerate(heads):
            m, l, acc = carry[a]
            o_ref[:, sl] = jnp.where(valid, acc / l, 0.0).astype(BF)
            lse_ref[a] = jnp.where(valid, m + jnp.log(l), 0.0)

    return pl.pallas_call(
        body,
        out_shape=(jax.ShapeDtypeStruct((T, D), BF), jax.ShapeDtypeStruct((H, T, 1), F32)),
        grid=(H // hp, nq),
        in_specs=[
            pl.BlockSpec((tq, wide), lambda h, i: (i, h)),
            pl.BlockSpec((T, wide), lambda h, i: (0, h)),
            pl.BlockSpec((T, wide), lambda h, i: (0, h)),
            pl.BlockSpec((hp, nk, 1, tk), lambda h, i: (h, 0, 0, 0)),
        ],
        out_specs=(pl.BlockSpec((tq, wide), lambda h, i: (i, h)),
                   pl.BlockSpec((hp, tq, 1), lambda h, i: (h, i, 0))),
        compiler_params=_params("parallel", "arbitrary"),
        name=name,
    )(q, k, v, crow)


def attn_bwd(q, k, v, do, o, lse, ccol, prev, pad, tk, *, name, dep=None):
    T, D = q.shape
    H = D // HEAD_DIM
    nk = T // tk
    tq, nq = tk, nk
    has_prev = prev is not None
    hp = 2 if H % 2 == 0 else 1
    wide = hp * HEAD_DIM
    heads = [slice(a * HEAD_DIM, (a + 1) * HEAD_DIM) for a in range(hp)]
    nt = (((1,), (1,)), ((), ()))

    def body(*refs):
        q_ref, k_ref, v_ref, do_ref, o_ref, lse_ref, cc_ref = refs[:7]
        refs = refs[7:]
        if has_prev:
            pk_ref, pv_ref, pc_ref, pq_ref = refs[:4]
            refs = refs[4:]
        dq_ref, dk_ref, dv_ref, dck_ref, dcq_ref, delta_ref = refs
        kc = pl.program_id(1)

        @pl.when(kc == 0)
        def _():
            dq_ref[...] = jnp.zeros_like(dq_ref)
            dcq_ref[...] = pq_ref[...] if has_prev else jnp.zeros_like(dcq_ref)
            ones = jnp.ones((8, HEAD_DIM), F32)
            for a, sl in enumerate(heads):
                for i in range(nq):
                    rows = slice(i * tq, (i + 1) * tq)
                    prod = do_ref[rows, sl].astype(BF).astype(F32) * o_ref[rows, sl].astype(F32)
                    delta_ref[a, i] = lax.dot_general(ones, prod, nt, precision=lax.Precision.HIGHEST,
                                                      preferred_element_type=F32)[0:1]

        kbs = [k_ref[:, sl] for sl in heads]
        vbs = [v_ref[:, sl] for sl in heads]
        cks = [jnp.broadcast_to(cc_ref[a], (tk, tq)) for a in range(hp)]
        krow = kc * tk + lax.broadcasted_iota(jnp.int32, (tk, 1), 0)

        def step(qi, carry, masked):
            rows = pl.ds(pl.multiple_of(qi * tq, tq), tq)
            out = []
            for a, sl in enumerate(heads):
                dk, dv, dck = carry[a]
                qb = q_ref[rows, sl]
                dob = do_ref[rows, sl].astype(BF)
                s = lax.dot_general(kbs[a], qb, nt, preferred_element_type=F32) - cks[a]
                if masked:
                    qcol = qi * tq + lax.broadcasted_iota(jnp.int32, (1, tq), 1)
                    s = jnp.where((krow <= qcol) & (krow >= pad), s, NEG)
                p = jnp.exp(s - lse_ref[a, qi])
                dp = lax.dot_general(vbs[a], dob, nt, preferred_element_type=F32)
                ds = p * (dp - delta_ref[a, qi])
                dsb = ds.astype(BF)
                dv = dv + jnp.dot(p.astype(BF), dob, preferred_element_type=F32)
                dk = dk + jnp.dot(dsb, qb, preferred_element_type=F32)
                dq_ref[rows, sl] += lax.dot_general(dsb, kbs[a], (((0,), (0,)), ((), ())),
                                                    preferred_element_type=F32)
                dcq_ref[a, qi] += jnp.sum(ds, axis=0, keepdims=True)
                part = ds[:, 0:LANES]
                for j in range(1, tq // LANES):
                    part = part + ds[:, j * LANES:(j + 1) * LANES]
                out.append((dk, dv, dck - part))
            return tuple(out)

        def rest(masked):
            return lambda c: lax.fori_loop(kc + 1, nq, lambda qi, cc: step(qi, cc, masked), c)

        init = tuple((jnp.zeros((tk, HEAD_DIM), F32), jnp.zeros((tk, HEAD_DIM), F32), jnp.zeros((tk, LANES), F32))
                     for _ in heads)
        carry = step(kc, init, True)
        carry = lax.cond(kc == 0, rest(True), rest(False), carry)
        for a, sl in enumerate(heads):
            dk, dv, dck = carry[a]
            dck = jnp.sum(dck, axis=1, keepdims=True)
            if has_prev:
                dk = dk + pk_ref[:, sl]
                dv = dv + pv_ref[:, sl]
                dck = dck + pc_ref[a]
            dk_ref[:, sl] = dk
            dv_ref[:, sl] = dv
            dck_ref[a] = dck

    head_all = pl.BlockSpec((T, wide), lambda h, j: (0, h))
    head_blk = pl.BlockSpec((tk, wide), lambda h, j: (j, h))
    rows_all = pl.BlockSpec((hp, nq, 1, tq), lambda h, j: (h, 0, 0, 0))
    col_blk = pl.BlockSpec((hp, tk, 1), lambda h, j: (h, j, 0))
    in_specs = [head_all, head_blk, head_blk, head_all, head_all, rows_all, col_blk]
    args = [q, k, v, do, o, lse, ccol]
    if has_prev:
        in_specs += [head_blk, head_blk, col_blk, rows_all]
        args += list(prev)
    body, in_specs, args = _with_dep(body, in_specs, args, dep)
    return pl.pallas_call(
        body,
        out_shape=(jax.ShapeDtypeStruct((T, D), F32), jax.ShapeDtypeStruct((T, D), F32),
                   jax.ShapeDtypeStruct((T, D), F32), jax.ShapeDtypeStruct((H, T, 1), F32),
                   jax.ShapeDtypeStruct((H, nq, 1, tq), F32)),
        grid=(H // hp, nk),
        in_specs=in_specs,
        out_specs=(head_all, head_blk, head_blk, col_blk, rows_all),
        scratch_shapes=[pltpu.VMEM((hp, nq, 1, tq), F32)],
        compiler_params=_params("parallel", "arbitrary"),
        name=name,
    )(*args)


def loss_head(h, target, lead, *, name):
    T, D = h.shape
    tm = lead
    assert T % tm == 0 and target.shape[0] % tm == 0
    inv_d = 1.0 / D

    def body(h_ref, t_ref, dh_ref, dhb_ref, loss_ref):
        i = pl.program_id(0)

        @pl.when(i == 0)
        def _():
            dh_ref[...] = jnp.zeros_like(dh_ref)
            dhb_ref[...] = jnp.zeros_like(dhb_ref)
            loss_ref[...] = jnp.zeros_like(loss_ref)

        @pl.when(i > 0)
        def _():
            e = h_ref[...] - t_ref[...]
            dh = e * inv_d
            dh_ref[...] = dh
            dhb_ref[...] = dh.astype(BF)
            loss_ref[...] += 0.5 * inv_d * jnp.sum(e * e)

    return pl.pallas_call(
        body,
        out_shape=(jax.ShapeDtypeStruct((T, D), F32), jax.ShapeDtypeStruct((T, D), BF),
                   jax.ShapeDtypeStruct((8, LANES), F32)),
        grid=(T // tm,),
        in_specs=[pl.BlockSpec((tm, D), lambda i: (i, 0)),
                  pl.BlockSpec((tm, D), lambda i: (jnp.maximum(i - 1, 0), 0))],
        out_specs=(pl.BlockSpec((tm, D), lambda i: (i, 0)), pl.BlockSpec((tm, D), lambda i: (i, 0)),
                   pl.BlockSpec((8, LANES), lambda i: (0, 0))),
        compiler_params=_params("arbitrary"),
        name=name,
    )(h, target)


def adamw(parts, w, m, v, *, name):
    P, R, C = parts.shape
    tr = _tile(R, max(16, (128 * 1024) // C), 16)

    def body(p_ref, w_ref, m_ref, v_ref, g_ref, d_ref, mo_ref, vo_ref):
        g = p_ref[0].astype(F32)
        for i in range(1, P):
            g = g + p_ref[i].astype(F32)
        m_new = ADAM_B1 * m_ref[...] + (1.0 - ADAM_B1) * g
        v_new = ADAM_B2 * v_ref[...] + (1.0 - ADAM_B2) * jnp.square(g)
        m_hat = m_new / (1.0 - ADAM_B1 ** ADAM_STEP)
        v_hat = v_new / (1.0 - ADAM_B2 ** ADAM_STEP)
        g_ref[...] = g
        d_ref[...] = -ADAM_LR * (m_hat / (jnp.sqrt(v_hat) + ADAM_EPS) + ADAM_WD * w_ref[...])
        mo_ref[...] = m_new
        vo_ref[...] = v_new

    blk = pl.BlockSpec((tr, C), lambda i: (i, 0))
    out = jax.ShapeDtypeStruct((R, C), F32)
    return pl.pallas_call(
        body,
        out_shape=(out, out, out, out),
        grid=(R // tr,),
        in_specs=[pl.BlockSpec((P, tr, C), lambda i: (0, i, 0)), blk, blk, blk],
        out_specs=(blk, blk, blk, blk),
        compiler_params=_params("parallel"),
        name=name,
    )(parts, w, m, v)


def _flip(v, bit):
    return 1 - v if bit else v


def all_gather(shard, *, name, dep=None):
    def body(x_ref, out_ref, send_sems, recv_sems, local_sem):
        x, y, c = lax.axis_index("x"), lax.axis_index("y"), lax.axis_index("c")
        me, sibling = (x, y, c), (x, y, 1 - c)
        chips = [(1 - x, y), (x, 1 - y), (1 - x, 1 - y)]

        def block(px, py, pc):
            return out_ref.at[4 * px + 2 * py + pc]

        def copy(k, blk, to, src=None):
            return pltpu.make_async_remote_copy(
                src_ref=block(*blk) if src is None else src,
                dst_ref=block(*blk),
                send_sem=send_sems.at[k],
                recv_sem=recv_sems.at[k],
                device_id=to,
                device_id_type=pl.DeviceIdType.MESH,
            )

        mine = pltpu.make_async_copy(x_ref, block(*me), local_sem)
        mine.start()
        first = [copy(0, me, sibling, src=x_ref)]
        first += [copy(1 + j, me, (*chip, c), src=x_ref) for j, chip in enumerate(chips)]
        for cp in first:
            cp.start()
        passed = [copy(4 + j, (*chip, c), sibling) for j, chip in enumerate(chips)]
        for j, chip in enumerate(chips):
            copy(1 + j, (*chip, c), me).wait_recv()
            passed[j].start()
        copy(0, sibling, me).wait_recv()
        for j, chip in enumerate(chips):
            copy(4 + j, (*chip, 1 - c), me).wait_recv()
        for cp in first + passed:
            cp.wait_send()
        mine.wait()

    body, in_specs, args = _with_dep(body, [pl.BlockSpec(memory_space=pl.ANY)], [shard], dep)
    return pl.pallas_call(
        body,
        out_shape=jax.ShapeDtypeStruct((N_DEV,) + shard.shape, shard.dtype),
        in_specs=in_specs,
        out_specs=pl.BlockSpec(memory_space=pl.ANY),
        scratch_shapes=[pltpu.SemaphoreType.DMA((7,)), pltpu.SemaphoreType.DMA((7,)),
                        pltpu.SemaphoreType.DMA],
        name=name,
    )(*args)


def exchange_slabs(slabs, *, name):
    def body(g_ref, r_ref, send_sems, recv_sems, local_sem):
        x, y, c = lax.axis_index("x"), lax.axis_index("y"), lax.axis_index("c")
        me = 4 * x + 2 * y + c
        mine = pltpu.make_async_copy(g_ref.at[me], r_ref.at[me], local_sem)
        mine.start()
        sends, recvs = [], []
        for k in range(1, N_DEV):
            px, py, pc = _flip(x, (k >> 2) & 1), _flip(y, (k >> 1) & 1), _flip(c, k & 1)
            peer = 4 * px + 2 * py + pc
            sends.append(pltpu.make_async_remote_copy(
                src_ref=g_ref.at[peer], dst_ref=r_ref.at[me],
                send_sem=send_sems.at[k - 1], recv_sem=recv_sems.at[k - 1],
                device_id=(px, py, pc), device_id_type=pl.DeviceIdType.MESH))
            recvs.append(pltpu.make_async_remote_copy(
                src_ref=g_ref.at[peer], dst_ref=r_ref.at[peer],
                send_sem=send_sems.at[k - 1], recv_sem=recv_sems.at[k - 1],
                device_id=(px, py, pc), device_id_type=pl.DeviceIdType.MESH))
        for cp in sends:
            cp.start()
        for cp in recvs:
            cp.wait_recv()
        for cp in sends:
            cp.wait_send()
        mine.wait()

    return pl.pallas_call(
        body,
        out_shape=jax.ShapeDtypeStruct(slabs.shape, slabs.dtype),
        in_specs=[pl.BlockSpec(memory_space=pl.ANY)],
        out_specs=pl.BlockSpec(memory_space=pl.ANY),
        scratch_shapes=[pltpu.SemaphoreType.DMA((7,)), pltpu.SemaphoreType.DMA((7,)),
                        pltpu.SemaphoreType.DMA],
        name=name,
    )(slabs)


def reduce_adamw(slabs, w, m, v, *, name):
    got = exchange_slabs(slabs, name=name + "_xchg")
    return adamw(got, w, m, v, name=name + "_adamw")


_HBM = pl.BlockSpec(memory_space=pltpu.HBM)
_SEM = pl.BlockSpec(memory_space=pltpu.SEMAPHORE)
_ANY = pl.BlockSpec(memory_space=pl.ANY)
_EFFECT = pltpu.SideEffectType.DATAFLOW_SIDE_EFFECTING
_N_FIRST = 4


def _first_copies(land_ref, send_sems, recv_sems):
    x, y, c = lax.axis_index("x"), lax.axis_index("y"), lax.axis_index("c")
    mine = land_ref.at[4 * x + 2 * y + c]
    targets = [(x, y, 1 - c), (1 - x, y, c), (x, 1 - y, c), (1 - x, 1 - y, c)]
    sends, recvs = [], []
    for k, (px, py, pc) in enumerate(targets):
        common = dict(send_sem=send_sems.at[k], recv_sem=recv_sems.at[k], device_id=(px, py, pc),
                      device_id_type=pl.DeviceIdType.MESH)
        sends.append(pltpu.make_async_remote_copy(src_ref=mine, dst_ref=mine, **common))
        theirs = land_ref.at[4 * px + 2 * py + pc]
        recvs.append(pltpu.make_async_remote_copy(src_ref=theirs, dst_ref=theirs, **common))
    return sends, recvs


def _second_copies(land_ref, send_sems, recv_sems):
    x, y, c = lax.axis_index("x"), lax.axis_index("y"), lax.axis_index("c")
    sends, recvs = [], []
    for j, (px, py) in enumerate([(1 - x, y), (x, 1 - y), (1 - x, 1 - y)]):
        common = dict(send_sem=send_sems.at[j], recv_sem=recv_sems.at[j], device_id=(x, y, 1 - c),
                      device_id_type=pl.DeviceIdType.MESH)
        blk = land_ref.at[4 * px + 2 * py + c]
        sends.append(pltpu.make_async_remote_copy(src_ref=blk, dst_ref=blk, **common))
        got = land_ref.at[4 * px + 2 * py + (1 - c)]
        recvs.append(pltpu.make_async_remote_copy(src_ref=got, dst_ref=got, **common))
    return sends, recvs


def gather_start(shard, me, after, *, name):
    R, C = shard.shape
    tr = _tile(R, max(16, (512 * 1024) // C), 16)

    def place_body(me_ref, x_ref, o_ref):
        o_ref[...] = x_ref[...].astype(BF)

    land = pl.pallas_call(
        place_body, name=name + "_own",
        out_shape=jax.ShapeDtypeStruct((N_DEV, R, C), BF),
        grid_spec=pltpu.PrefetchScalarGridSpec(
            num_scalar_prefetch=1,
            grid=(R // tr,),
            in_specs=[pl.BlockSpec((tr, C), lambda i, me_ref: (i, 0))],
            out_specs=pl.BlockSpec((None, tr, C), lambda i, me_ref: (me_ref[0], i, 0)),
        ),
        compiler_params=_params("parallel"),
    )(me.reshape(1).astype(jnp.int32), shard)

    def body(land_ref, after_ref, send_sems, recv_sems, land_thru, token):
        sends, _ = _first_copies(land_ref, send_sems, recv_sems)
        for cp in sends:
            cp.start()
        token[...] = jnp.zeros_like(token)

    send_sems, recv_sems, land_thru, token = pl.pallas_call(
        body, name=name + "_s1",
        out_shape=(pltpu.SemaphoreType.DMA((_N_FIRST,)), pltpu.SemaphoreType.DMA((_N_FIRST,)),
                   pltpu.HBM(land.shape, land.dtype), jax.ShapeDtypeStruct((8, LANES), F32)),
        in_specs=(_HBM, _ANY),
        out_specs=(_SEM, _SEM, _HBM, pl.BlockSpec(memory_space=pltpu.VMEM)),
        input_output_aliases={0: 2},
        compiler_params=pltpu.CompilerParams(has_side_effects=_EFFECT),
    )(pltpu.with_memory_space_constraint(land, pltpu.HBM), after)
    return (send_sems, recv_sems, land_thru), token


def gather_mid(handle, after, *, name):
    send_sems, recv_sems, land_thru = handle

    def body(land_ref, send1, recv1, after_ref, send2, recv2, land_out, token):
        sends, recvs = _first_copies(land_ref, send1, recv1)
        for cp in sends:
            cp.wait_send()
        for cp in recvs:
            cp.wait_recv()
        seconds, _ = _second_copies(land_ref, send2, recv2)
        for cp in seconds:
            cp.start()
        token[...] = jnp.zeros_like(token)

    send2, recv2, land2, token = pl.pallas_call(
        body, name=name + "_s2",
        out_shape=(pltpu.SemaphoreType.DMA((3,)), pltpu.SemaphoreType.DMA((3,)),
                   pltpu.HBM(land_thru.shape, land_thru.dtype), jax.ShapeDtypeStruct((8, LANES), F32)),
        in_specs=(_HBM, _SEM, _SEM, _ANY),
        out_specs=(_SEM, _SEM, _HBM, pl.BlockSpec(memory_space=pltpu.VMEM)),
        input_output_aliases={0: 2},
        compiler_params=pltpu.CompilerParams(has_side_effects=_EFFECT),
    )(land_thru, send_sems, recv_sems, after)
    return (send2, recv2, land2), token


def gather_finish(handle, after, *, name):
    send2, recv2, land2 = handle

    def body(land_ref, send2, recv2, after_ref, got_ref):
        sends, recvs = _second_copies(land_ref, send2, recv2)
        for cp in sends:
            cp.wait_send()
        for cp in recvs:
            cp.wait_recv()

    return pl.pallas_call(
        body, name=name + "_w",
        out_shape=pltpu.HBM(land2.shape, land2.dtype),
        in_specs=(_HBM, _SEM, _SEM, _ANY),
        out_specs=_HBM,
        input_output_aliases={0: 0},
        compiler_params=pltpu.CompilerParams(has_side_effects=_EFFECT),
    )(land2, send2, recv2, after)


def _slab_copies(g_ref, r_ref, send_sems, recv_sems):
    x, y, c = lax.axis_index("x"), lax.axis_index("y"), lax.axis_index("c")
    me = 4 * x + 2 * y + c
    sends, recvs = [], []
    for k in range(1, N_DEV):
        px, py, pc = _flip(x, (k >> 2) & 1), _flip(y, (k >> 1) & 1), _flip(c, k & 1)
        peer = 4 * px + 2 * py + pc
        common = dict(send_sem=send_sems.at[k - 1], recv_sem=recv_sems.at[k - 1], device_id=(px, py, pc),
                      device_id_type=pl.DeviceIdType.MESH)
        sends.append(pltpu.make_async_remote_copy(src_ref=g_ref.at[peer], dst_ref=r_ref.at[me], **common))
        recvs.append(pltpu.make_async_remote_copy(src_ref=g_ref.at[peer], dst_ref=r_ref.at[peer], **common))
    return sends, recvs


def exchange_start(slabs, *, name):
    land = lax.empty(slabs.shape, slabs.dtype)

    def body(g_ref, r_ref, send_sems, recv_sems, g_thru, r_thru, token):
        sends, _ = _slab_copies(g_ref, r_ref, send_sems, recv_sems)
        for cp in sends:
            cp.start()
        token[...] = jnp.zeros_like(token)

    send_sems, recv_sems, g_thru, r_thru, token = pl.pallas_call(
        body, name=name,
        out_shape=(pltpu.SemaphoreType.DMA((N_DEV - 1,)), pltpu.SemaphoreType.DMA((N_DEV - 1,)),
                   pltpu.HBM(slabs.shape, slabs.dtype), pltpu.HBM(slabs.shape, slabs.dtype),
                   jax.ShapeDtypeStruct((8, LANES), F32)),
        in_specs=(_HBM, _HBM),
        out_specs=(_SEM, _SEM, _HBM, _HBM, pl.BlockSpec(memory_space=pltpu.VMEM)),
        input_output_aliases={0: 2, 1: 3},
        compiler_params=pltpu.CompilerParams(has_side_effects=_EFFECT),
    )(pltpu.with_memory_space_constraint(slabs, pltpu.HBM), pltpu.with_memory_space_constraint(land, pltpu.HBM))
    return (send_sems, recv_sems, g_thru, r_thru), token


def exchange_finish(handle, after, *, name):
    send_sems, recv_sems, g_thru, r_thru = handle

    def body(g_ref, r_ref, send_sems, recv_sems, after_ref, g_out, r_out):
        sends, recvs = _slab_copies(g_ref, r_ref, send_sems, recv_sems)
        for cp in sends:
            cp.wait_send()
        for cp in recvs:
            cp.wait_recv()

    return pl.pallas_call(
        body, name=name,
        out_shape=(pltpu.HBM(g_thru.shape, g_thru.dtype), pltpu.HBM(r_thru.shape, r_thru.dtype)),
        in_specs=(_HBM, _HBM, _SEM, _SEM, _ANY),
        out_specs=(_HBM, _HBM),
        input_output_aliases={0: 0, 1: 1},
        compiler_params=pltpu.CompilerParams(has_side_effects=_EFFECT),
    )(g_thru, r_thru, send_sems, recv_sems, after)


def adamw_own(own, got, me, w, m, v, layer, prev, *, name):
    P, R, C = got.shape
    L = w.shape[0]
    tr = _tile(R, max(16, (256 * 1024) // C), 16)

    def body(me_ref, own_ref, p_ref, w_ref, m_ref, v_ref, *rest):
        g_ref, d_ref, mo_ref, vo_ref = rest[-4:]
        mine = own_ref[...].astype(F32)
        g = None
        for i in range(P):
            term = jnp.where(me_ref[0] == i, mine, p_ref[i].astype(F32))
            g = term if g is None else g + term
        m_new = ADAM_B1 * m_ref[...] + (1.0 - ADAM_B1) * g
        v_new = ADAM_B2 * v_ref[...] + (1.0 - ADAM_B2) * jnp.square(g)
        m_hat = m_new / (1.0 - ADAM_B1 ** ADAM_STEP)
        v_hat = v_new / (1.0 - ADAM_B2 ** ADAM_STEP)
        g_ref[...] = g
        d_ref[...] = -ADAM_LR * (m_hat / (jnp.sqrt(v_hat) + ADAM_EPS) + ADAM_WD * w_ref[...])
        mo_ref[...] = m_new
        vo_ref[...] = v_new

    blk = pl.BlockSpec((None, tr, C), lambda i, me_ref: (layer, i, 0))
    out = jax.ShapeDtypeStruct((L, R, C), F32)
    in_specs = [pl.BlockSpec((None, tr, C), lambda i, me_ref: (me_ref[0], i, 0)),
                pl.BlockSpec((P, tr, C), lambda i, me_ref: (0, i, 0)), blk, blk, blk]
    args = [me.reshape(1).astype(jnp.int32), own, got, w, m, v]
    aliases = {}
    if prev is not None:
        in_specs += [pl.BlockSpec(memory_space=pl.ANY)] * 4
        aliases = {len(args) + i: i for i in range(4)}
        args += list(prev)
    return pl.pallas_call(
        body,
        out_shape=(out, out, out, out),
        grid_spec=pltpu.PrefetchScalarGridSpec(
            num_scalar_prefetch=1,
            grid=(R // tr,),
            in_specs=in_specs,
            out_specs=(blk, blk, blk, blk),
        ),
        input_output_aliases=aliases,
        compiler_params=_params("parallel"),
        name=name,
    )(*args)


def _pad_rows(a, rows):
    return jnp.pad(a, ((0, rows - a.shape[0]), (0, 0)))


def _pad_cols(a, cols):
    return jnp.pad(a, ((0, 0), (0, cols - a.shape[1])))


def kernel(x, meta, a_norm, a_w_in, a_conv, a_w_out, kv_norm, w_kv, k_norm, w_f, b_f, b_norm, b_w_q, b_q_norm, b_w_o, ffn_norm, ffn_w_gu, ffn_w_down, loss_target, m_meta, m_a_norm, m_a_w_in, m_a_conv, m_a_w_out, m_kv_norm, m_w_kv, m_k_norm, m_w_f, m_b_f, m_b_norm, m_b_w_q, m_b_q_norm, m_b_w_o, m_ffn_norm, m_ffn_w_gu, m_ffn_w_down, v_meta, v_a_norm, v_a_w_in, v_a_conv, v_a_w_out, v_kv_norm, v_w_kv, v_k_norm, v_w_f, v_b_f, v_b_norm, v_b_w_q, v_b_q_norm, v_b_w_o, v_ffn_norm, v_ffn_w_gu, v_ffn_w_down):
    S, D = x.shape[1], x.shape[2]
    n_meta = meta.shape[0]
    Ds = meta.shape[1]
    H = D // HEAD_DIM
    n_a, n_b = a_w_in.shape[0], b_w_q.shape[0]
    depth = n_a + n_b
    Fs = ffn_w_down.shape[1]
    pad = BLOCK - n_meta
    lead = pad + n_meta
    T = lead + S
    tk_attn = _tile(T, 384, LANES)
    nk_attn = T // tk_attn
    q_scale = 1.0 / math.sqrt(HEAD_DIM)
    my = 4 * lax.axis_index("x") + 2 * lax.axis_index("y") + lax.axis_index("c")

    wf_t = w_f.reshape(H, Ds)
    small = jnp.concatenate([meta, _pad_rows(a_norm, 8), _pad_rows(a_conv.reshape(n_a * 3, Ds), 8), wf_t], axis=0)
    r_an, r_ac, r_wf = n_meta, n_meta + 8, n_meta + 16
    gs = all_gather(small, name="ag_small")
    unshard = lambda blk: jnp.transpose(blk, (1, 0, 2)).reshape(blk.shape[1], D)
    meta_full = unshard(gs[:, 0:n_meta])
    a_norm_full = unshard(gs[:, r_an:r_an + n_a])
    a_conv_full = unshard(gs[:, r_ac:r_ac + 3 * n_a]).reshape(n_a, 3, D)
    w_f_full = gs[:, r_wf:r_wf + H].reshape(D, H)
    wf_pad = _pad_cols(w_f_full, LANES).astype(BF)[None]
    bf_pad = _pad_cols(b_f.reshape(1, H), LANES)

    def layer_shards(l):
        if l < n_a:
            mix = [(("in", l), a_w_in[l]), (("out", l), a_w_out[l])]
        else:
            j = l - n_a
            mix = ([(("kv", 0), w_kv)] if j == 0 else []) + [(("q", j), b_w_q[j]), (("o", j), b_w_o[j])]
        return mix + [(("gu", l), ffn_w_gu[l]), (("dn", l), ffn_w_down[l])]

    first_level, second_level, W = {}, {}, {}
    st = {"done": None, "tok": None}

    def note(val):
        st["done"] = val
        return val

    def take():
        tok, st["tok"] = st["tok"], None
        return tok

    def chain_after(default):
        if st["tok"] is not None:
            return st["tok"]
        return default if st["done"] is None else st["done"]

    def ag_name(key):
        return f"ag_{key[0]}{key[1]}"

    def start_layer(l):
        for key, shard in layer_shards(l):
            first_level[key], st["tok"] = gather_start(shard, my, chain_after(shard), name=ag_name(key))

    def pass_on(keys):
        for key in keys:
            second_level[key], st["tok"] = gather_mid(first_level.pop(key), chain_after(None), name=ag_name(key))

    def weight(key, shape=None):
        w = gather_finish(second_level.pop(key), st["done"], name=ag_name(key))
        W[key] = w if shape is None else w.reshape(shape)
        return W[key]

    def layer_keys(l):
        keys = [key for key, _ in layer_shards(l)]
        return keys[:-2], keys[-2:]

    h = note(jnp.concatenate([jnp.zeros((pad, D), F32), meta_full, x[0]], axis=0))
    start_layer(0)
    pass_on(layer_keys(0)[0])
    saved = []
    shared = None
    for l in range(depth):
        rec = {"h": h}
        mix_keys, ffn_keys = layer_keys(l)

        def ahead():
            if l >= 1:
                pass_on(ffn_keys[:1])
            if l + 1 < depth:
                start_layer(l + 1)

        if l < n_a:
            xn = note(rms_fwd(h, a_norm_full[l], name=f"a{l}_norm", dep=take()))
            ahead()
            proj = note(mm_nn(xn, weight(("in", l)), name=f"a{l}_in", dep=take()))
            if l == 0:
                pass_on(ffn_keys[:1])
            y = note(conv_fwd(proj, a_conv_full[l], name=f"a{l}_conv"))
            h1 = note(mm_nn(y, weight(("out", l), (1, D, D)), add=h, name=f"a{l}_out", dep=take()))
            rec.update(xn=xn, proj=proj, y=y)
        else:
            j = l - n_a
            if j == 0:
                xnk = note(rms_fwd(h, kv_norm, name="kv_norm", dep=take()))
                ahead()
                kv = note(mm_nn(xnk, weight(("kv", 0)), name="kv_proj", dep=take()))
                k, v = kv_post(kv, k_norm, name="kv_post")
                logits = mm_nn(xnk, wf_pad, name="f_logits", tn_target=LANES)
                cfull = fgate_fwd(logits, bf_pad, pad, name="f_gate")
                c_t = jnp.transpose(cfull[:, :H])
                crow = c_t.reshape(H, nk_attn, 1, tk_attn)
                ccol = c_t.reshape(H, T, 1)
                shared = dict(h=h, xnk=xnk, kv=kv, logits=logits)
                xn = note(rms_fwd(h, b_norm[j], name=f"b{j}_norm"))
            else:
                xn = note(rms_fwd(h, b_norm[j], name=f"b{j}_norm", dep=take()))
                ahead()
            qraw = note(mm_nn(xn, weight(("q", j), (1, D, D)), name=f"b{j}_q", dep=take()))
            q = hn_fwd(qraw, b_q_norm[j], q_scale, name=f"b{j}_qnorm")
            o, lse = attn_fwd(q, k, v, crow, pad, name=f"b{j}_attn")
            note(o)
            h1 = note(mm_nn(o, weight(("o", j), (1, D, D)), add=h, name=f"b{j}_o"))
            rec.update(xn=xn, qraw=qraw, q=q, o=o, lse=lse)
        xn2 = note(rms_fwd(h1, ffn_norm[l], name=f"f{l}_norm", dep=take()))
        pass_on(ffn_keys[1:])
        act, g_s, u_s = mm_swiglu(xn2, weight(("gu", l)), name=f"f{l}_gu", dep=take())
        note(act)
        if l + 1 < depth:
            pass_on(layer_keys(l + 1)[0])
        h = note(mm_nn(act, weight(("dn", l), (1, N_DEV * Fs, D)), add=h1, name=f"f{l}_down", resident=True,
                       tm_target=528, tn_target=1024, dep=take()))
        rec.update(h1=h1, xn2=xn2, act=act, g=g_s, u=u_s)
        saved.append(rec)

    dh, dhb, loss_tile = loss_head(h, loss_target[0], lead, name="loss")
    loss = lax.psum(loss_tile[0, 0], MESH_AXES)

    upd = {}
    small_g = {}
    inflight = []

    def big(name, l, section, slabs, w, m, v):
        handle, st["tok"] = exchange_start(slabs.reshape(N_DEV, -1, w.shape[-1]), name=f"{name}{l}_xs")
        inflight.append((section, name, l, handle, w, m, v))

    def land(sections, after):
        for entry in [e for e in inflight if sections is None or e[0] in sections]:
            inflight.remove(entry)
            _, name, l, handle, w, m, v = entry
            own, got = exchange_finish(handle, after, name=f"{name}{l}_xw")
            flat = lambda t: t.reshape(w.shape[0], -1, w.shape[-1])
            upd[name] = adamw_own(own, got, my, flat(w), flat(m), flat(v), l, upd.get(name),
                                  name=f"{name}{l}_adamw")

    dk = dv = dck = dcq = None
    for l in reversed(range(depth)):
        rec = saved[l]
        land([("ffn", l + 1)], dh)
        dgu = mm_nt_dswiglu(dhb, W[("dn", l)], rec["g"], rec["u"], name=f"f{l}_ddown")
        big("ffn_w_down", l, ("ffn", l), mm_tn(rec["act"], dhb, 1, name=f"f{l}_wdown"),
            ffn_w_down, m_ffn_w_down, v_ffn_w_down)
        big("ffn_w_gu", l, ("ffn", l), mm_tn(rec["xn2"], dgu, N_DEV, name=f"f{l}_wgu", dep=take()),
            ffn_w_gu, m_ffn_w_gu, v_ffn_w_gu)
        dxn2 = mm_nt(dgu, W[("gu", l)], name=f"f{l}_dgu", gb=2, dep=take())
        dh1, dhb, dgf = rms_bwd(dxn2, rec["h1"], ffn_norm[l], dh, name=f"f{l}_dnorm")
        small_g[("ffn_norm", l)] = dgf
        land([("mix", l + 1)], dh1)
        if l < n_a:
            dy = mm_nt(dhb, W[("out", l)], name=f"a{l}_dout")
            big("a_w_out", l, ("mix", l), mm_tn(rec["y"], dhb, 1, name=f"a{l}_wout"),
                a_w_out, m_a_w_out, v_a_w_out)
            db, dc, dhh, dcw = conv_bwd(dy, rec["proj"], a_conv_full[l], name=f"a{l}_dconv", dep=take())
            small_g[("a_conv", l)] = dcw
            dproj = jnp.concatenate([db, dc, dhh], axis=1)
            big("a_w_in", l, ("mix", l), mm_tn(rec["xn"], dproj, N_DEV, name=f"a{l}_win"),
                a_w_in, m_a_w_in, v_a_w_in)
            dxn = mm_nt(dproj, W[("in", l)], name=f"a{l}_din", gb=4, dep=take())
            dh, dhb, dga = rms_bwd(dxn, rec["h"], a_norm_full[l], dh1, name=f"a{l}_dnorm")
            small_g[("a_norm", l)] = dga
        else:
            j = l - n_a
            do = mm_nt(dhb, W[("o", j)], name=f"b{j}_do")
            big("b_w_o", j, ("mix", l), mm_tn(rec["o"], dhb, 1, name=f"b{j}_wo"),
                b_w_o, m_b_w_o, v_b_w_o)
            prev = None if dk is None else (dk, dv, dck, dcq)
            dq, dk, dv, dck, dcq = attn_bwd(rec["q"], k, v, do, rec["o"],
                                            rec["lse"].reshape(H, nk_attn, 1, tk_attn), ccol, prev, pad, tk_attn,
                                            name=f"b{j}_dattn", dep=take())
            dqraw, dqn = hn_bwd(dq, rec["qraw"], b_q_norm[j], q_scale, name=f"b{j}_dqnorm")
            small_g[("b_q_norm", j)] = dqn
            big("b_w_q", j, ("mix", l), mm_tn(rec["xn"], dqraw, 1, name=f"b{j}_wq"),
                b_w_q, m_b_w_q, v_b_w_q)
            dxn = mm_nt(dqraw, W[("q", j)], name=f"b{j}_dq", dep=take())
            dh, dhb, dgb = rms_bwd(dxn, rec["h"], b_norm[j], dh1, name=f"b{j}_dnorm")
            small_g[("b_norm", j)] = dgb
            if j == 0:
                dkraw, dkn = hn_bwd(dk, shared["kv"], k_norm, 1.0, name="kv_dknorm")
                dkv = jnp.concatenate([dkraw, dv.astype(BF)], axis=1)
                dc_full = _pad_cols(jnp.transpose(dck.reshape(H, T) + dcq.reshape(H, T)), LANES)
                dz, dbf = fgate_bwd(dc_full, shared["logits"], bf_pad, pad, name="f_dgate")
                big("w_kv", 0, ("mix", l), mm_tn(shared["xnk"], dkv, N_DEV, name="kv_wkv"),
                    w_kv[None], m_w_kv[None], v_w_kv[None])
                dwf_t = mm_tn(dz, shared["xnk"], 1, name="f_wf", out_dtype=F32, tn_target=1024,
                              dep=take())[0, :H]
                dxn_f = mm_nt(dz, wf_pad, name="f_dxn")
                dxnk = mm_nt(dkv, W[("kv", 0)], add=dxn_f, name="kv_dxn", gb=4)
                dh, dhb, dgkv = rms_bwd(dxnk, shared["h"], kv_norm, dh, name="kv_dnorm")
    land(None, dh)

    grad_x = dh[lead:][None]

    row8 = lambda a: _pad_rows(_pad_cols(a, D), 8)
    stack = lambda key, n: jnp.concatenate([small_g[(key, i)] for i in range(n)], axis=0)
    g_sharded = jnp.concatenate([dh[pad:lead], row8(stack("a_norm", n_a)), row8(stack("a_conv", n_a)), dwf_t], axis=0)
    g_repl = jnp.concatenate([row8(jnp.concatenate([dgkv, stack("b_norm", n_b)], axis=0)),
                              row8(stack("ffn_norm", depth)),
                              row8(jnp.concatenate([_pad_cols(dkn, D), _pad_cols(stack("b_q_norm", n_b), D),
                                                    _pad_cols(dbf[:, :H], D)], axis=0))], axis=0)
    n_sh = g_sharded.shape[0]
    gathered = all_gather(jnp.concatenate([g_sharded, g_repl], axis=0), name="ag_small_grads",
                          dep=upd["a_w_in"][0])
    parts_sh = lax.dynamic_slice_in_dim(gathered[:, :n_sh], my * Ds, Ds, axis=2)
    parts_rp = gathered[:, n_sh:]

    def pack_sh(t_meta, t_an, t_ac, t_wf):
        return jnp.concatenate([t_meta, _pad_rows(t_an, 8), _pad_rows(t_ac.reshape(n_a * 3, Ds), 8),
                                jnp.transpose(t_wf)], axis=0)

    def pack_rp(t_kv, t_bn, t_fn, t_kn, t_qn, t_bf):
        return jnp.concatenate([row8(jnp.concatenate([t_kv.reshape(1, D), t_bn], axis=0)), row8(t_fn),
                                row8(jnp.concatenate([_pad_cols(t_kn.reshape(1, -1), D), _pad_cols(t_qn, D),
                                                      _pad_cols(t_bf.reshape(1, -1), D)], axis=0))], axis=0)

    res_sh = adamw(parts_sh, pack_sh(meta, a_norm, a_conv, w_f), pack_sh(m_meta, m_a_norm, m_a_conv, m_w_f),
                   pack_sh(v_meta, v_a_norm, v_a_conv, v_w_f), name="small_sharded_adamw")
    res_rp = adamw(parts_rp, pack_rp(kv_norm, b_norm, ffn_norm, k_norm, b_q_norm, b_f),
                   pack_rp(m_kv_norm, m_b_norm, m_ffn_norm, m_k_norm, m_b_q_norm, m_b_f),
                   pack_rp(v_kv_norm, v_b_norm, v_ffn_norm, v_k_norm, v_b_q_norm, v_b_f), name="small_repl_adamw")

    def unpack(kind):
        sh, rp = res_sh[kind], res_rp[kind]
        out = {
            "meta": sh[0:n_meta],
            "a_norm": sh[r_an:r_an + n_a],
            "a_conv": sh[r_ac:r_ac + 3 * n_a].reshape(n_a, 3, Ds),
            "w_f": jnp.transpose(sh[r_wf:r_wf + H]),
            "kv_norm": rp[0],
            "b_norm": rp[1:1 + n_b],
            "ffn_norm": rp[8:8 + depth],
            "k_norm": rp[16, :HEAD_DIM],
            "b_q_norm": rp[17:17 + n_b, :HEAD_DIM],
            "b_f": rp[17 + n_b, :H],
        }
        for name, like in (("a_w_in", a_w_in), ("a_w_out", a_w_out), ("b_w_q", b_w_q), ("b_w_o", b_w_o),
                           ("ffn_w_gu", ffn_w_gu), ("ffn_w_down", ffn_w_down)):
            out[name] = upd[name][kind].reshape(like.shape)
        out["w_kv"] = upd["w_kv"][kind].reshape(w_kv.shape)
        return out

    order = ["meta", "a_norm", "a_w_in", "a_conv", "a_w_out", "kv_norm", "w_kv", "k_norm", "w_f", "b_f",
             "b_norm", "b_w_q", "b_q_norm", "b_w_o", "ffn_norm", "ffn_w_gu", "ffn_w_down"]
    outs = [loss, grad_x]
    for kind in range(4):
        vals = unpack(kind)
        outs += [vals[n] for n in order]
    return tuple(outs)
```
